```python
import jax, jax.numpy as jnp
from jax import lax
import numpy as np

D_MODEL = 1024
BATCH = 8
SEQ = 4096
DEPTH = 2

EXPAND = 2
D_INNER = EXPAND * D_MODEL
D_POOL = D_INNER // 2
D_SB = D_INNER - D_POOL
POOL_WINDOWS = (2, 4, 8, 16)
N_POOL_GROUPS = len(POOL_WINDOWS)
POOL_GROUP = D_POOL // N_POOL_GROUPS
SB_HEAD_DIM = 64
SB_HEADS = D_SB // SB_HEAD_DIM
SB_BLOCK = 128
CONV_WIDTH = 3
D_CONV = D_INNER
D_IN_EVEN = D_POOL + 3 * D_SB + D_INNER
D_IN_ODD = 3 * D_CONV + D_INNER
N_EVEN = (DEPTH + 1) // 2
N_ODD = DEPTH // 2
EPS = 1e-6

kernel_name = "hybrid_pool_stickbreak_shortconv_adaln"


def rmsnorm(x, g):
    xf = x.astype(jnp.float32)
    y = xf * lax.rsqrt(jnp.mean(xf * xf, axis=-1, keepdims=True) + EPS)
    return (y * g.astype(jnp.float32)).astype(x.dtype)


def adaln_params(c, w, b):
    m = jax.nn.silu(c) @ w + b
    shift, scale, gate = jnp.split(m, 3, axis=-1)
    return shift[:, None, :], scale[:, None, :], gate[:, None, :]


def pool_mixer(u, w_grp, scale):
    bsz, s, _ = u.shape
    uf = u.astype(jnp.float32)
    cs = jnp.cumsum(uf, axis=1)
    pos = jnp.arange(s, dtype=jnp.float32)
    outs = []
    for gi, w in enumerate(POOL_WINDOWS):
        sl = slice(gi * POOL_GROUP, (gi + 1) * POOL_GROUP)
        cg = cs[..., sl]
        prev = jnp.pad(cg, ((0, 0), (w, 0), (0, 0)))[:, :s]
        count = jnp.minimum(pos + 1.0, float(w))[None, :, None]
        outs.append((cg - prev) / count - uf[..., sl])
    p = jnp.stack(outs, axis=2).astype(u.dtype)
    y = jnp.einsum('bsgc,gcd->bsgd', p, w_grp).reshape(bsz, s, D_POOL)
    return y * scale


def stick_breaking_attention(q, k, v):
    bsz, s, _, _ = q.shape
    qh, kh, vh = (jnp.swapaxes(t, 1, 2) for t in (q, k, v))
    inv_sqrt = 1.0 / np.sqrt(SB_HEAD_DIM).astype(np.float32)
    outs = []
    for blk in range(s // SB_BLOCK):
        q0 = blk * SB_BLOCK
        end = q0 + SB_BLOCK
        qb = qh[:, :, q0:end]
        kb = kh[:, :, :end]
        vb = vh[:, :, :end]
        z = jnp.einsum('bhqd,bhkd->bhqk', qb, kb).astype(jnp.float32) * inv_sqrt
        q_pos = jnp.arange(q0, end)[:, None]
        k_pos = jnp.arange(end)[None, :]
        causal = k_pos < q_pos
        log_fail = jnp.where(causal, -jax.nn.softplus(z), 0.0)
        after = lax.cumsum(log_fail, axis=3, reverse=True) - log_fail
        a = jnp.where(causal, jnp.exp(jax.nn.log_sigmoid(z) + after), 0.0)
        outs.append(jnp.einsum('bhqk,bhkd->bhqd', a.astype(vb.dtype), vb))
    o = jnp.concatenate(outs, axis=2)
    return jnp.swapaxes(o, 1, 2).reshape(bsz, s, SB_HEADS * SB_HEAD_DIM)


def even_mixer(h, w_in, pool_w, pool_scale, w_out):
    bsz, s, _ = h.shape
    proj = h @ w_in
    u_pool, q, k, v, gate = jnp.split(
        proj, [D_POOL, D_POOL + D_SB, D_POOL + 2 * D_SB, D_POOL + 3 * D_SB], axis=-1)
    y_pool = pool_mixer(u_pool, pool_w, pool_scale)
    shp = (bsz, s, SB_HEADS, SB_HEAD_DIM)
    y_sb = stick_breaking_attention(q.reshape(shp), k.reshape(shp), v.reshape(shp))
    y = jnp.concatenate([y_pool, y_sb], axis=-1) * jax.nn.silu(gate)
    return y @ w_out


def odd_mixer(h, w_in, conv_w, conv_b, w_out):
    s = h.shape[1]
    proj = h @ w_in
    gb, gc, u, gate = jnp.split(proj, [D_CONV, 2 * D_CONV, 3 * D_CONV], axis=-1)
    u = gc * u
    up = jnp.pad(u, ((0, 0), (CONV_WIDTH - 1, 0), (0, 0)))
    conv = conv_b + sum(up[:, j:j + s] * conv_w[j] for j in range(CONV_WIDTH))
    y = gb * conv * jax.nn.silu(gate)
    return y @ w_out


def _fwd_setup_inputs(seed: int = 0) -> dict:
    key = jax.random.key(seed)
    ks = jax.random.split(key, 16)
    f32 = jnp.float32
    D = D_MODEL
    nrm = lambda k, shp, sc: jax.random.normal(k, shp, f32) * sc
    return {
        "x": nrm(ks[0], (BATCH, SEQ, D), 1.0),
        "c": nrm(ks[1], (BATCH, D), 1.0),
        "norm_g": 1.0 + nrm(ks[2], (DEPTH, D), 0.02),
        "ada_w": nrm(ks[3], (DEPTH, D, 3 * D), 0.1 * D ** -0.5),
        "ada_b": nrm(ks[4], (DEPTH, 3 * D), 0.01),
        "even_w_in": nrm(ks[5], (N_EVEN, D, D_IN_EVEN), D ** -0.5),
        "pool_w": nrm(ks[6], (N_EVEN, N_POOL_GROUPS, POOL_GROUP, POOL_GROUP), POOL_GROUP ** -0.5),
        "pool_scale": 1.0 + nrm(ks[7], (N_EVEN, D_POOL), 0.02),
        "even_w_out": nrm(ks[8], (N_EVEN, D_INNER, D), D_INNER ** -0.5),
        "odd_w_in": nrm(ks[9], (N_ODD, D, D_IN_ODD), D ** -0.5),
        "conv_w": nrm(ks[10], (N_ODD, CONV_WIDTH, D_CONV), CONV_WIDTH ** -0.5),
        "conv_b": nrm(ks[11], (N_ODD, D_CONV), 0.01),
        "odd_w_out": nrm(ks[12], (N_ODD, D_INNER, D), D_INNER ** -0.5),
        "final_g": 1.0 + nrm(ks[13], (D,), 0.02),
    }


def _fwd_reference(x, c, norm_g, ada_w, ada_b, even_w_in, pool_w, pool_scale, even_w_out,
              odd_w_in, conv_w, conv_b, odd_w_out, final_g):
    for i in range(DEPTH):
        shift, scale, gate = adaln_params(c, ada_w[i], ada_b[i])
        h = rmsnorm(x, norm_g[i]) * (1.0 + scale) + shift
        j = i // 2
        if i % 2 == 0:
            y = even_mixer(h, even_w_in[j], pool_w[j], pool_scale[j], even_w_out[j])
        else:
            y = odd_mixer(h, odd_w_in[j], conv_w[j], conv_b[j], odd_w_out[j])
        x = x + ((1.0 + gate) * y).astype(x.dtype)
    return rmsnorm(x, final_g)


import jax as _jax
import jax.numpy as _jnp

TWIN_FORMAT = 'train_step'
FWD_PARAMS = ['x', 'c', 'norm_g', 'ada_w', 'ada_b', 'even_w_in', 'pool_w', 'pool_scale', 'even_w_out', 'odd_w_in', 'conv_w', 'conv_b', 'odd_w_out', 'final_g']
TWIN_WEIGHTS = ['norm_g', 'ada_w', 'ada_b', 'even_w_in', 'pool_w', 'pool_scale', 'even_w_out', 'odd_w_in', 'conv_w', 'conv_b', 'odd_w_out', 'final_g']
TWIN_DIFF_INPUT = 'x'
TWIN_INPUTS = ['x', 'c', 'norm_g', 'ada_w', 'ada_b', 'even_w_in', 'pool_w', 'pool_scale', 'even_w_out', 'odd_w_in', 'conv_w', 'conv_b', 'odd_w_out', 'final_g', 'loss_target', 'm_norm_g', 'm_ada_w', 'm_ada_b', 'm_even_w_in', 'm_pool_w', 'm_pool_scale', 'm_even_w_out', 'm_odd_w_in', 'm_conv_w', 'm_conv_b', 'm_odd_w_out', 'm_final_g', 'v_norm_g', 'v_ada_w', 'v_ada_b', 'v_even_w_in', 'v_pool_w', 'v_pool_scale', 'v_even_w_out', 'v_odd_w_in', 'v_conv_w', 'v_conv_b', 'v_odd_w_out', 'v_final_g']
TWIN_OUTPUTS = ['loss', 'grad_x', 'grad_norm_g', 'grad_ada_w', 'grad_ada_b', 'grad_even_w_in', 'grad_pool_w', 'grad_pool_scale', 'grad_even_w_out', 'grad_odd_w_in', 'grad_conv_w', 'grad_conv_b', 'grad_odd_w_out', 'grad_final_g', 'delta_norm_g', 'delta_ada_w', 'delta_ada_b', 'delta_even_w_in', 'delta_pool_w', 'delta_pool_scale', 'delta_even_w_out', 'delta_odd_w_in', 'delta_conv_w', 'delta_conv_b', 'delta_odd_w_out', 'delta_final_g', 'new_m_norm_g', 'new_m_ada_w', 'new_m_ada_b', 'new_m_even_w_in', 'new_m_pool_w', 'new_m_pool_scale', 'new_m_even_w_out', 'new_m_odd_w_in', 'new_m_conv_w', 'new_m_conv_b', 'new_m_odd_w_out', 'new_m_final_g', 'new_v_norm_g', 'new_v_ada_w', 'new_v_ada_b', 'new_v_even_w_in', 'new_v_pool_w', 'new_v_pool_scale', 'new_v_even_w_out', 'new_v_odd_w_in', 'new_v_conv_w', 'new_v_conv_b', 'new_v_odd_w_out', 'new_v_final_g']
TWIN_LEAF_KINDS = {'loss': 'loss', 'grad_x': 'grad_x', 'grad_norm_g': 'grad_w', 'grad_ada_w': 'grad_w', 'grad_ada_b': 'grad_w', 'grad_even_w_in': 'grad_w', 'grad_pool_w': 'grad_w', 'grad_pool_scale': 'grad_w', 'grad_even_w_out': 'grad_w', 'grad_odd_w_in': 'grad_w', 'grad_conv_w': 'grad_w', 'grad_conv_b': 'grad_w', 'grad_odd_w_out': 'grad_w', 'grad_final_g': 'grad_w', 'delta_norm_g': 'delta_w', 'delta_ada_w': 'delta_w', 'delta_ada_b': 'delta_w', 'delta_even_w_in': 'delta_w', 'delta_pool_w': 'delta_w', 'delta_pool_scale': 'delta_w', 'delta_even_w_out': 'delta_w', 'delta_odd_w_in': 'delta_w', 'delta_conv_w': 'delta_w', 'delta_conv_b': 'delta_w', 'delta_odd_w_out': 'delta_w', 'delta_final_g': 'delta_w', 'new_m_norm_g': 'new_m', 'new_m_ada_w': 'new_m', 'new_m_ada_b': 'new_m', 'new_m_even_w_in': 'new_m', 'new_m_pool_w': 'new_m', 'new_m_pool_scale': 'new_m', 'new_m_even_w_out': 'new_m', 'new_m_odd_w_in': 'new_m', 'new_m_conv_w': 'new_m', 'new_m_conv_b': 'new_m', 'new_m_odd_w_out': 'new_m', 'new_m_final_g': 'new_m', 'new_v_norm_g': 'new_v', 'new_v_ada_w': 'new_v', 'new_v_ada_b': 'new_v', 'new_v_even_w_in': 'new_v', 'new_v_pool_w': 'new_v', 'new_v_pool_scale': 'new_v', 'new_v_even_w_out': 'new_v', 'new_v_odd_w_in': 'new_v', 'new_v_conv_w': 'new_v', 'new_v_conv_b': 'new_v', 'new_v_odd_w_out': 'new_v', 'new_v_final_g': 'new_v'}


def _forward(args):
    return _fwd_reference(*[args[k] for k in FWD_PARAMS])


def _output_shape():
    out = _jax.eval_shape(lambda: _forward(_fwd_setup_inputs(0)))
    return out.shape, out.dtype

N_MICROBATCH = 1
ADAM_LR = 0.001
ADAM_B1 = 0.9
ADAM_B2 = 0.999
ADAM_EPS = 1e-08
ADAM_WD = 0.01
ADAM_STEP = 10
PER_EXAMPLE_BATCH_AXIS = {'x': 0, 'c': 0, 'loss_target': 0}
SHARED_INPUTS = []
_WEIGHT_DTYPES = {'norm_g': _jnp.float32, 'ada_w': _jnp.float32, 'ada_b': _jnp.float32, 'even_w_in': _jnp.float32, 'pool_w': _jnp.float32, 'pool_scale': _jnp.float32, 'even_w_out': _jnp.float32, 'odd_w_in': _jnp.float32, 'conv_w': _jnp.float32, 'conv_b': _jnp.float32, 'odd_w_out': _jnp.float32, 'final_g': _jnp.float32}
MOMENT_SCALE = {'norm_g': 1.611326e-01, 'ada_w': 9.091169e-02, 'ada_b': 1.596370e-01, 'even_w_in': 5.909116e-02, 'pool_w': 7.932609e-02, 'pool_scale': 7.621019e-02, 'even_w_out': 9.763946e-02, 'odd_w_in': 6.073915e-02, 'conv_w': 6.051559e-02, 'conv_b': 6.085640e-02, 'odd_w_out': 8.490298e-02, 'final_g': 3.198509e+01}


def _to_microbatches(a, axis):
    t = _jnp.moveaxis(a, axis, 0)
    t = t.reshape((N_MICROBATCH, t.shape[0] // N_MICROBATCH) + t.shape[1:])
    return _jnp.moveaxis(t, 1, axis + 1)


def setup_inputs(seed: int = 0) -> dict:
    inp = _fwd_setup_inputs(seed)
    key = _jax.random.fold_in(_jax.random.key(seed), 7919)
    shape, _ = _output_shape()
    out = dict(inp)
    out["loss_target"] = _jax.random.normal(_jax.random.fold_in(key, 0), shape, _jnp.float32)
    for i, name in enumerate(TWIN_WEIGHTS):
        w = inp[name].astype(_jnp.float32)
        if MOMENT_SCALE is None:
            s = _jnp.sqrt(_jnp.mean(_jnp.square(w)) + 1e-30)
        else:
            s = MOMENT_SCALE[name]
        km, kv = _jax.random.split(_jax.random.fold_in(key, i + 1))
        out[name] = w
        out["m_" + name] = s * _jax.random.normal(km, w.shape, _jnp.float32)
        out["v_" + name] = (s * s) * _jax.random.uniform(kv, w.shape, _jnp.float32, 0.5, 1.5)
    if N_MICROBATCH > 1:
        for name, axis in PER_EXAMPLE_BATCH_AXIS.items():
            out[name] = _to_microbatches(out[name], axis)
    return {'x': out['x'], 'c': out['c'], 'norm_g': out['norm_g'], 'ada_w': out['ada_w'], 'ada_b': out['ada_b'], 'even_w_in': out['even_w_in'], 'pool_w': out['pool_w'], 'pool_scale': out['pool_scale'], 'even_w_out': out['even_w_out'], 'odd_w_in': out['odd_w_in'], 'conv_w': out['conv_w'], 'conv_b': out['conv_b'], 'odd_w_out': out['odd_w_out'], 'final_g': out['final_g'], 'loss_target': out['loss_target'], 'm_norm_g': out['m_norm_g'], 'm_ada_w': out['m_ada_w'], 'm_ada_b': out['m_ada_b'], 'm_even_w_in': out['m_even_w_in'], 'm_pool_w': out['m_pool_w'], 'm_pool_scale': out['m_pool_scale'], 'm_even_w_out': out['m_even_w_out'], 'm_odd_w_in': out['m_odd_w_in'], 'm_conv_w': out['m_conv_w'], 'm_conv_b': out['m_conv_b'], 'm_odd_w_out': out['m_odd_w_out'], 'm_final_g': out['m_final_g'], 'v_norm_g': out['v_norm_g'], 'v_ada_w': out['v_ada_w'], 'v_ada_b': out['v_ada_b'], 'v_even_w_in': out['v_even_w_in'], 'v_pool_w': out['v_pool_w'], 'v_pool_scale': out['v_pool_scale'], 'v_even_w_out': out['v_even_w_out'], 'v_odd_w_in': out['v_odd_w_in'], 'v_conv_w': out['v_conv_w'], 'v_conv_b': out['v_conv_b'], 'v_odd_w_out': out['v_odd_w_out'], 'v_final_g': out['v_final_g']}


def _loss(weights, diff, rest, loss_target):
    with _jax.named_scope("forward"):
        args = {**rest, TWIN_DIFF_INPUT: diff, **{k: w.astype(_WEIGHT_DTYPES[k]) for k, w in weights.items()}}
        y = _forward(args)
    with _jax.named_scope("loss_head"):
        err = _jnp.square(y.astype(_jnp.float32) - loss_target)
        return 0.5 * _jnp.sum(_jnp.mean(err, axis=-1)) if err.ndim else 0.5 * err


def _adamw(w, g, m, v):
    m = ADAM_B1 * m + (1.0 - ADAM_B1) * g
    v = ADAM_B2 * v + (1.0 - ADAM_B2) * _jnp.square(g)
    m_hat = m / (1.0 - ADAM_B1 ** ADAM_STEP)
    v_hat = v / (1.0 - ADAM_B2 ** ADAM_STEP)
    delta = -ADAM_LR * (m_hat / (_jnp.sqrt(v_hat) + ADAM_EPS) + ADAM_WD * w)
    return delta, m, v


def reference(x, c, norm_g, ada_w, ada_b, even_w_in, pool_w, pool_scale, even_w_out, odd_w_in, conv_w, conv_b, odd_w_out, final_g, loss_target, m_norm_g, m_ada_w, m_ada_b, m_even_w_in, m_pool_w, m_pool_scale, m_even_w_out, m_odd_w_in, m_conv_w, m_conv_b, m_odd_w_out, m_final_g, v_norm_g, v_ada_w, v_ada_b, v_even_w_in, v_pool_w, v_pool_scale, v_even_w_out, v_odd_w_in, v_conv_w, v_conv_b, v_odd_w_out, v_final_g):
    given = dict(x=x, c=c, norm_g=norm_g, ada_w=ada_w, ada_b=ada_b, even_w_in=even_w_in, pool_w=pool_w, pool_scale=pool_scale, even_w_out=even_w_out, odd_w_in=odd_w_in, conv_w=conv_w, conv_b=conv_b, odd_w_out=odd_w_out, final_g=final_g, loss_target=loss_target, m_norm_g=m_norm_g, m_ada_w=m_ada_w, m_ada_b=m_ada_b, m_even_w_in=m_even_w_in, m_pool_w=m_pool_w, m_pool_scale=m_pool_scale, m_even_w_out=m_even_w_out, m_odd_w_in=m_odd_w_in, m_conv_w=m_conv_w, m_conv_b=m_conv_b, m_odd_w_out=m_odd_w_out, m_final_g=m_final_g, v_norm_g=v_norm_g, v_ada_w=v_ada_w, v_ada_b=v_ada_b, v_even_w_in=v_even_w_in, v_pool_w=v_pool_w, v_pool_scale=v_pool_scale, v_even_w_out=v_even_w_out, v_odd_w_in=v_odd_w_in, v_conv_w=v_conv_w, v_conv_b=v_conv_b, v_odd_w_out=v_odd_w_out, v_final_g=v_final_g)
    weights = {n: given[n] for n in TWIN_WEIGHTS}
    shared = {n: given[n] for n in SHARED_INPUTS}
    per_example = {n: given[n] for n in ['x', 'c']}
    grad_fn = _jax.value_and_grad(_loss, argnums=(0, 1))

    def one_microbatch(ex, loss_target):
        ex = dict(ex)
        diff = ex.pop(TWIN_DIFF_INPUT)
        return grad_fn(weights, diff, {**shared, **ex}, loss_target)

    if N_MICROBATCH == 1:
        loss, (grad_w, grad_x) = one_microbatch(per_example, given["loss_target"])
    else:
        def body(carry, xs):
            loss_sum, grad_sum = carry
            l_k, (gw_k, gx_k) = one_microbatch(xs[0], xs[1])
            with _jax.named_scope("update"):
                return (loss_sum + l_k, _jax.tree.map(_jnp.add, grad_sum, gw_k)), gx_k

        init = (_jnp.zeros((), _jnp.float32), _jax.tree.map(_jnp.zeros_like, weights))
        (loss, grad_w), grad_x = _jax.lax.scan(body, init, (per_example, given["loss_target"]))
    with _jax.named_scope("update"):
        delta_w, new_m, new_v = {}, {}, {}
        for n in TWIN_WEIGHTS:
            delta_w[n], new_m[n], new_v[n] = _adamw(weights[n], grad_w[n], given["m_" + n], given["v_" + n])
    return (loss, grad_x, *[grad_w[n] for n in TWIN_WEIGHTS], *[delta_w[n] for n in TWIN_WEIGHTS],
            *[new_m[n] for n in TWIN_WEIGHTS], *[new_v[n] for n in TWIN_WEIGHTS])
```

```python
import functools

import jax
import jax.numpy as jnp
from jax import lax
from jax.experimental import pallas as pl
from jax.experimental.pallas import tpu as pltpu

F32 = jnp.float32
BF16 = jnp.bfloat16
MESH = pl.DeviceIdType.MESH

D = 1024
DI = 2048
DP = 1024
NE = 6144
NO = 8192
WINDOWS = (2, 4, 8, 16)
PG = 256
HD = 64
NCHIP = 4
NDEV = 8
EPS = 1e-6
INV_SQRT_HD = 0.125

LR, B1, B2, EPS_ADAM, WD, STEP = 0.001, 0.9, 0.999, 1e-08, 0.01, 10

TM = 512
TME = 256
CT = 512
BQ = 128
VMEM_LIMIT = 56 * 1024 * 1024


def _dot(a, b):
    return jnp.dot(a, b, preferred_element_type=F32)


def _dot_nt(a, b):
    return lax.dot_general(a, b, (((1,), (1,)), ((), ())), preferred_element_type=F32)


def _dot_tn(a, b):
    return lax.dot_general(a, b, (((0,), (0,)), ((), ())), preferred_element_type=F32)


def _params(*sem):
    return pltpu.CompilerParams(dimension_semantics=sem, vmem_limit_bytes=VMEM_LIMIT)


def _rowsum(v):
    return jnp.sum(v, axis=0, keepdims=True)


def _norm_inproj(x, vecs, w, name):
    s = x.shape[0]
    ns = w.shape[2]
    n = NCHIP * ns
    tps = ns // CT
    tm = min(TM, s)

    def body(x_ref, vec_ref, w_ref, proj_ref, h_ref):
        @pl.when(pl.program_id(1) == 0)
        def _():
            xv = x_ref[...]
            r = lax.rsqrt(jnp.mean(xv * xv, axis=-1, keepdims=True) + EPS)
            h = ((xv * r) * vec_ref[0:1, :]) * (1.0 + vec_ref[1:2, :]) + vec_ref[2:3, :]
            h_ref[...] = h.astype(BF16)

        proj_ref[...] = _dot(h_ref[...], w_ref[...])

    return pl.pallas_call(
        body, name=name, grid=(s // tm, n // CT),
        in_specs=[pl.BlockSpec((tm, D), lambda i, j: (i, 0)),
                  pl.BlockSpec((8, D), lambda i, j: (0, 0)),
                  pl.BlockSpec((None, D, CT), lambda i, j: (j // tps, 0, j % tps))],
        out_specs=[pl.BlockSpec((tm, CT), lambda i, j: (i, j)),
                   pl.BlockSpec((tm, D), lambda i, j: (i, 0))],
        out_shape=[jax.ShapeDtypeStruct((s, n), F32), jax.ShapeDtypeStruct((s, D), BF16)],
        compiler_params=_params("parallel", "arbitrary"),
    )(x, vecs, w)


def _pool_fwd(proj0, pw, pscale):
    s = proj0.shape[0]
    tm = min(TM, s)
    hb = tm // 16

    def body(u_ref, halo_ref, w_ref, sc_ref, p_ref, y_ref, ext_ref):
        i = pl.program_id(0)
        ext_ref[16:, :] = u_ref[...]
        ext_ref[0:16, :] = jnp.where(i > 0, halo_ref[...], 0.0)
        t = i * tm + lax.broadcasted_iota(jnp.int32, (tm, 1), 0)
        for g, wdw in enumerate(WINDOWS):
            cs = slice(g * PG, (g + 1) * PG)
            u = ext_ref[16:16 + tm, cs]
            acc = u
            for j in range(1, wdw):
                acc = acc + ext_ref[16 - j:16 - j + tm, cs]
            inv = 1.0 / jnp.minimum(t + 1, wdw).astype(F32)
            pb = (acc * inv - u).astype(BF16)
            p_ref[:, cs] = pb
            y_ref[:, cs] = _dot(pb, w_ref[g]) * sc_ref[:, cs]

    return pl.pallas_call(
        body, name="pool_fwd", grid=(s // tm,),
        in_specs=[pl.BlockSpec((tm, DP), lambda i: (i, 0)),
                  pl.BlockSpec((16, DP), lambda i: (jnp.maximum(i * hb - 1, 0), 0)),
                  pl.BlockSpec((4, PG, PG), lambda i: (0, 0, 0)),
                  pl.BlockSpec((1, DP), lambda i: (0, 0))],
        out_specs=[pl.BlockSpec((tm, DP), lambda i: (i, 0)),
                   pl.BlockSpec((tm, DP), lambda i: (i, 0))],
        out_shape=[jax.ShapeDtypeStruct((s, DP), BF16), jax.ShapeDtypeStruct((s, DP), F32)],
        scratch_shapes=[pltpu.VMEM((tm + 16, DP), F32)],
        compiler_params=_params("parallel"),
    )(proj0, proj0, pw, pscale)


def _sb_logits(qh, kh):
    z = _dot_nt(qh, kh) * INV_SQRT_HD
    t = jnp.log1p(jnp.exp(-jnp.abs(z)))
    return -(jnp.maximum(z, 0.0) + t), jnp.minimum(z, 0.0) - t


def _split_dot(v, tri):
    hi = v.astype(BF16)
    lo = (v - hi.astype(F32)).astype(BF16)
    return _dot(hi, tri) + _dot(lo, tri)


def _attn_fwd(proj0):
    s = proj0.shape[0]
    nq = s // BQ

    def body(q_ref, k_ref, v_ref, o_ref, qb_ref, kb_ref, vb_ref):
        qb_ref[...] = q_ref[...].astype(BF16)
        kb_ref[...] = k_ref[...].astype(BF16)
        vb_ref[...] = v_ref[...].astype(BF16)
        row = lax.broadcasted_iota(jnp.int32, (BQ, BQ), 0)
        col = lax.broadcasted_iota(jnp.int32, (BQ, BQ), 1)
        causal = col < row
        after = (row > col).astype(BF16)

        for h in range(2):
            ls = slice(HD * h, HD * (h + 1))

            def qstep(qi, carry, ls=ls):
                q0 = pl.multiple_of(qi * BQ, BQ)
                qh = qb_ref[pl.ds(q0, BQ), ls]

                def block(k0, c_after, diag):
                    kh = kb_ref[pl.ds(k0, BQ), ls]
                    vh = vb_ref[pl.ds(k0, BQ), ls]
                    lf, ls_ = _sb_logits(qh, kh)
                    if diag:
                        lf = jnp.where(causal, lf, 0.0)
                    aft = _split_dot(lf, after)
                    wlog = ls_ + aft if diag else ls_ + aft + c_after
                    a = jnp.exp(wlog)
                    if diag:
                        a = jnp.where(causal, a, 0.0)
                    return _dot(a.astype(BF16), vh), aft[:, 0:1] + lf[:, 0:1]

                o, c_after = block(q0, None, True)

                def kstep(j, kc):
                    o_acc, c_acc = kc
                    k0 = pl.multiple_of((qi - 1 - j) * BQ, BQ)
                    o_blk, rs = block(k0, c_acc, False)
                    return o_acc + o_blk, c_acc + rs

                o, c_after = lax.fori_loop(0, qi, kstep, (o, c_after))
                o_ref[pl.ds(q0, BQ), ls] = o
                return carry

            lax.fori_loop(0, nq, qstep, 0)

    return pl.pallas_call(
        body, name="attn_fwd", grid=(DP // 128,),
        in_specs=[pl.BlockSpec((s, 128), lambda h: (0, 8 + h)),
                  pl.BlockSpec((s, 128), lambda h: (0, 16 + h)),
                  pl.BlockSpec((s, 128), lambda h: (0, 24 + h))],
        out_specs=pl.BlockSpec((s, 128), lambda h: (0, h)),
        out_shape=jax.ShapeDtypeStruct((s, DP), F32),
        scratch_shapes=[pltpu.VMEM((s, 128), BF16)] * 3,
        compiler_params=_params("parallel"),
    )(proj0, proj0, proj0)


def _even_out(ypool, ysb, proj0, wout, x, vecs):
    s = x.shape[0]
    tm = min(TME, s)

    def body(yp_ref, ys_ref, gate_ref, w_ref, x_ref, vec_ref, x1_ref, out_ref, yg_ref):
        gt = gate_ref[...]
        sl = gt * jax.nn.sigmoid(gt)
        yg_ref[:, :DP] = (yp_ref[...] * sl[:, :DP]).astype(BF16)
        yg_ref[:, DP:] = (ys_ref[...] * sl[:, DP:]).astype(BF16)
        out = _dot(yg_ref[...], w_ref[...])
        out_ref[...] = out
        x1_ref[...] = x_ref[...] + (1.0 + vec_ref[3:4, :]) * out

    row = lambda i: (i, 0)
    return pl.pallas_call(
        body, name="even_out", grid=(s // tm,),
        in_specs=[pl.BlockSpec((tm, DP), row), pl.BlockSpec((tm, DP), row),
                  pl.BlockSpec((tm, DI), lambda i: (i, 2)),
                  pl.BlockSpec((DI, D), lambda i: (0, 0)),
                  pl.BlockSpec((tm, D), row), pl.BlockSpec((8, D), lambda i: (0, 0))],
        out_specs=[pl.BlockSpec((tm, D), row), pl.BlockSpec((tm, D), row), pl.BlockSpec((tm, DI), row)],
        out_shape=[jax.ShapeDtypeStruct((s, D), F32), jax.ShapeDtypeStruct((s, D), F32),
                   jax.ShapeDtypeStruct((s, DI), BF16)],
        compiler_params=_params("parallel"),
    )(ypool, ysb, proj0, wout, x, vecs)


def _odd_out(proj1, wout, x1, vecs, cw, cb, target):
    s = x1.shape[0]
    tm = min(TME, s)
    hb = tm // 8

    def body(gb_ref, gc_ref, u_ref, gt_ref, hgc_ref, hu_ref, w_ref, x1_ref, vec_ref, cw_ref, cb_ref, tg_ref,
             dx2_ref, out_ref, y1_ref, acc_ref, ext_ref):
        i = pl.program_id(0)

        @pl.when(i == 0)
        def _():
            acc_ref[...] = jnp.zeros_like(acc_ref)

        ext_ref[8:, :] = gc_ref[...] * u_ref[...]
        ext_ref[0:8, :] = jnp.where(i > 0, hgc_ref[...] * hu_ref[...], 0.0)
        for c in range(DI // CT):
            cs = slice(c * CT, (c + 1) * CT)
            conv = (cb_ref[0:1, cs] + cw_ref[0:1, cs] * ext_ref[6:6 + tm, cs]
                    + cw_ref[1:2, cs] * ext_ref[7:7 + tm, cs] + cw_ref[2:3, cs] * ext_ref[8:8 + tm, cs])
            gt = gt_ref[:, cs]
            y1_ref[:, cs] = (gb_ref[:, cs] * conv * (gt * jax.nn.sigmoid(gt))).astype(BF16)
        out = _dot(y1_ref[...], w_ref[...])
        out_ref[...] = out
        x2 = x1_ref[...] + (1.0 + vec_ref[3:4, :]) * out
        r = lax.rsqrt(jnp.mean(x2 * x2, axis=-1, keepdims=True) + EPS)
        nrm = x2 * r
        fg = vec_ref[4:5, :]
        err = nrm * fg - tg_ref[...]
        acc_ref[1:2, :] += _rowsum(err * err) * (0.5 / D)
        dyf = err * (1.0 / D)
        acc_ref[0:1, :] += _rowsum(dyf * nrm)
        dn = dyf * fg
        dx2_ref[...] = r * (dn - nrm * jnp.mean(dn * nrm, axis=-1, keepdims=True))

    row = lambda i: (i, 0)
    halo = lambda col: (lambda i: (jnp.maximum(i * hb - 1, 0), col))
    const = lambda i: (0, 0)
    return pl.pallas_call(
        body, name="odd_out", grid=(s // tm,),
        in_specs=[pl.BlockSpec((tm, DI), lambda i: (i, 0)), pl.BlockSpec((tm, DI), lambda i: (i, 1)),
                  pl.BlockSpec((tm, DI), lambda i: (i, 2)), pl.BlockSpec((tm, DI), lambda i: (i, 3)),
                  pl.BlockSpec((8, DI), halo(1)), pl.BlockSpec((8, DI), halo(2)),
                  pl.BlockSpec((DI, D), const), pl.BlockSpec((tm, D), row), pl.BlockSpec((8, D), const),
                  pl.BlockSpec((8, DI), const), pl.BlockSpec((1, DI), const), pl.BlockSpec((tm, D), row)],
        out_specs=[pl.BlockSpec((tm, D), row), pl.BlockSpec((tm, D), row), pl.BlockSpec((tm, DI), row),
                   pl.BlockSpec((8, D), const)],
        out_shape=[jax.ShapeDtypeStruct((s, D), F32), jax.ShapeDtypeStruct((s, D), F32),
                   jax.ShapeDtypeStruct((s, DI), BF16), jax.ShapeDtypeStruct((8, D), F32)],
        scratch_shapes=[pltpu.VMEM((tm + 8, DI), F32)],
        compiler_params=_params("arbitrary"),
    )(proj1, proj1, proj1, proj1, proj1, proj1, wout, x1, vecs, cw, cb, target)


def _odd_bwd(dx2, out1, proj1, wout, vecs, cw, cb):
    s = dx2.shape[0]
    tm = min(TME, s)
    nb = s // tm
    hb = tm // 8

    def body(dx2_ref, out1_ref, gb_ref, gc_ref, u_ref, gt_ref, hgc_ref, hu_ref, w_ref, vec_ref, cw_ref, cb_ref,
             dout_ref, dproj_ref, accv_ref, accd_ref, uext_ref, dext_ref, dy_ref):
        i = pl.program_id(0)
        blk = nb - 1 - i

        @pl.when(i == 0)
        def _():
            accv_ref[...] = jnp.zeros_like(accv_ref)
            accd_ref[...] = jnp.zeros_like(accd_ref)
            dext_ref[tm:tm + 8, :] = jnp.zeros((8, DI), F32)

        dx2v = dx2_ref[...]
        accd_ref[0:1, :] += _rowsum(dx2v * out1_ref[...])
        dout = (dx2v * (1.0 + vec_ref[3:4, :])).astype(BF16)
        dout_ref[...] = dout
        dy_ref[...] = _dot_nt(dout, w_ref[...])
        uext_ref[8:, :] = gc_ref[...] * u_ref[...]
        uext_ref[0:8, :] = jnp.where(blk > 0, hgc_ref[...] * hu_ref[...], 0.0)
        for c in range(DI // CT):
            cs = slice(c * CT, (c + 1) * CT)
            u0 = uext_ref[6:6 + tm, cs]
            u1 = uext_ref[7:7 + tm, cs]
            u2 = uext_ref[8:8 + tm, cs]
            w0, w1, w2 = cw_ref[0:1, cs], cw_ref[1:2, cs], cw_ref[2:3, cs]
            conv = cb_ref[0:1, cs] + w0 * u0 + w1 * u1 + w2 * u2
            gt = gt_ref[:, cs]
            sg = jax.nn.sigmoid(gt)
            gb = gb_ref[:, cs]
            dy = dy_ref[:, cs]
            t1 = dy * (gt * sg)
            dproj_ref[:, cs] = (t1 * conv).astype(BF16)
            dconv = t1 * gb
            dproj_ref[:, 3 * DI + c * CT:3 * DI + (c + 1) * CT] = (
                dy * gb * conv * (sg * (1.0 + gt * (1.0 - sg)))).astype(BF16)
            accv_ref[0:1, cs] += _rowsum(dconv * u0)
            accv_ref[1:2, cs] += _rowsum(dconv * u1)
            accv_ref[2:3, cs] += _rowsum(dconv * u2)
            accv_ref[3:4, cs] += _rowsum(dconv)
            dext_ref[0:tm, cs] = dconv
            duu = w2 * dconv + w1 * dext_ref[1:tm + 1, cs] + w0 * dext_ref[2:tm + 2, cs]
            dproj_ref[:, DI + c * CT:DI + (c + 1) * CT] = (duu * u_ref[:, cs]).astype(BF16)
            dproj_ref[:, 2 * DI + c * CT:2 * DI + (c + 1) * CT] = (duu * gc_ref[:, cs]).astype(BF16)
        dext_ref[tm:tm + 8, :] = dext_ref[0:8, :]

    rrow = lambda i: (nb - 1 - i, 0)
    rcol = lambda col: (lambda i: (nb - 1 - i, col))
    halo = lambda col: (lambda i: (jnp.maximum((nb - 1 - i) * hb - 1, 0), col))
    const = lambda i: (0, 0)
    return pl.pallas_call(
        body, name="odd_bwd", grid=(nb,),
        in_specs=[pl.BlockSpec((tm, D), rrow), pl.BlockSpec((tm, D), rrow),
                  pl.BlockSpec((tm, DI), rcol(0)), pl.BlockSpec((tm, DI), rcol(1)),
                  pl.BlockSpec((tm, DI), rcol(2)), pl.BlockSpec((tm, DI), rcol(3)),
                  pl.BlockSpec((8, DI), halo(1)), pl.BlockSpec((8, DI), halo(2)),
                  pl.BlockSpec((DI, D), const), pl.BlockSpec((8, D), const),
                  pl.BlockSpec((8, DI), const), pl.BlockSpec((1, DI), const)],
        out_specs=[pl.BlockSpec((tm, D), rrow), pl.BlockSpec((tm, NO), rrow),
                   pl.BlockSpec((8, DI), const), pl.BlockSpec((8, D), const)],
        out_shape=[jax.ShapeDtypeStruct((s, D), BF16), jax.ShapeDtypeStruct((s, NO), BF16),
                   jax.ShapeDtypeStruct((8, DI), F32), jax.ShapeDtypeStruct((8, D), F32)],
        scratch_shapes=[pltpu.VMEM((tm + 8, DI), F32), pltpu.VMEM((tm + 8, DI), F32), pltpu.VMEM((tm, DI), F32)],
        compiler_params=_params("arbitrary"),
    )(dx2, out1, proj1, proj1, proj1, proj1, proj1, proj1, wout, vecs, cw, cb)


def _grad_w_cols(a, b, name):
    s, m = a.shape
    ns = b.shape[1] // NCHIP
    ts = min(TM, s)

    def body(a_ref, b_ref, o_ref):
        @pl.when(pl.program_id(2) == 0)
        def _():
            o_ref[...] = jnp.zeros_like(o_ref)

        o_ref[...] += _dot_tn(a_ref[...], b_ref[...])

    return pl.pallas_call(
        body, name=name, grid=(m // CT, NCHIP, s // ts),
        in_specs=[pl.BlockSpec((ts, CT), lambda i, j, k: (k, i)),
                  pl.BlockSpec((ts, ns), lambda i, j, k: (k, j))],
        out_specs=pl.BlockSpec((None, CT, ns), lambda i, j, k: (j, i, 0)),
        out_shape=jax.ShapeDtypeStruct((NCHIP, m, ns), F32),
        compiler_params=_params("parallel", "parallel", "arbitrary"),
    )(a, b)


def _grad_w_rows(a, b, name):
    s = a.shape[0]
    ms = a.shape[1] // NCHIP
    n = b.shape[1]
    ts = min(TM, s)

    def body(a_ref, b_ref, o_ref):
        @pl.when(pl.program_id(1) == 0)
        def _():
            o_ref[...] = jnp.zeros_like(o_ref)

        o_ref[...] += _dot_tn(a_ref[...], b_ref[...])

    return pl.pallas_call(
        body, name=name, grid=(NCHIP, s // ts),
        in_specs=[pl.BlockSpec((ts, ms), lambda i, k: (k, i)),
                  pl.BlockSpec((ts, n), lambda i, k: (k, 0))],
        out_specs=pl.BlockSpec((None, ms, n), lambda i, k: (i, 0, 0)),
        out_shape=jax.ShapeDtypeStruct((NCHIP, ms, n), F32),
        compiler_params=_params("parallel", "arbitrary"),
    )(a, b)


def _inproj_bwd(dproj, w, x, dx_in, vecs, name):
    s = x.shape[0]
    ns = w.shape[2]
    n = NCHIP * ns
    tps = ns // CT
    nk = n // CT
    tm = min(TM, s)

    def body(dp_ref, w_ref, x_ref, dxin_ref, vec_ref, dx_ref, acc_ref, dh_ref):
        i = pl.program_id(0)
        k = pl.program_id(1)

        @pl.when((i == 0) & (k == 0))
        def _():
            acc_ref[...] = jnp.zeros_like(acc_ref)

        @pl.when(k == 0)
        def _():
            dh_ref[...] = jnp.zeros_like(dh_ref)

        dh_ref[...] += _dot_nt(dp_ref[...], w_ref[...])

        @pl.when(k == nk - 1)
        def _():
            dh = dh_ref[...]
            xv = x_ref[...]
            r = lax.rsqrt(jnp.mean(xv * xv, axis=-1, keepdims=True) + EPS)
            nrm = xv * r
            g = vec_ref[0:1, :]
            sc1 = 1.0 + vec_ref[1:2, :]
            dhn = dh * nrm
            acc_ref[0:1, :] += _rowsum(dh)
            acc_ref[1:2, :] += _rowsum(dhn) * g
            acc_ref[2:3, :] += _rowsum(dhn) * sc1
            dn = dh * (g * sc1)
            dx_ref[...] = dxin_ref[...] + r * (dn - nrm * jnp.mean(dn * nrm, axis=-1, keepdims=True))

    row = lambda i, k: (i, 0)
    const = lambda i, k: (0, 0)
    return pl.pallas_call(
        body, name=name, grid=(s // tm, nk),
        in_specs=[pl.BlockSpec((tm, CT), lambda i, k: (i, k)),
                  pl.BlockSpec((None, D, CT), lambda i, k: (k // tps, 0, k % tps)),
                  pl.BlockSpec((tm, D), row), pl.BlockSpec((tm, D), row), pl.BlockSpec((8, D), const)],
        out_specs=[pl.BlockSpec((tm, D), row), pl.BlockSpec((8, D), const)],
        out_shape=[jax.ShapeDtypeStruct((s, D), F32), jax.ShapeDtypeStruct((8, D), F32)],
        scratch_shapes=[pltpu.VMEM((tm, D), F32)],
        compiler_params=_params("arbitrary", "arbitrary"),
    )(dproj, w, x, dx_in, vecs)


def _even_bwd(dx1, out0, ypool, ysb, proj0, wout, vecs):
    s = dx1.shape[0]
    tm = min(TME, s)

    def body(dx1_ref, out0_ref, yp_ref, ys_ref, gate_ref, w_ref, vec_ref,
             dout_ref, dyp_ref, dys_ref, dgt_ref, acc_ref):
        @pl.when(pl.program_id(0) == 0)
        def _():
            acc_ref[...] = jnp.zeros_like(acc_ref)

        dx1v = dx1_ref[...]
        acc_ref[0:1, :] += _rowsum(dx1v * out0_ref[...])
        dout = (dx1v * (1.0 + vec_ref[3:4, :])).astype(BF16)
        dout_ref[...] = dout
        dyg = _dot_nt(dout, w_ref[...])
        gt = gate_ref[...]
        sg = jax.nn.sigmoid(gt)
        sl = gt * sg
        dsl = sg * (1.0 + gt * (1.0 - sg))
        dyp_ref[...] = dyg[:, :DP] * sl[:, :DP]
        dys_ref[...] = dyg[:, DP:] * sl[:, DP:]
        dgt_ref[:, :DP] = (dyg[:, :DP] * yp_ref[...] * dsl[:, :DP]).astype(BF16)
        dgt_ref[:, DP:] = (dyg[:, DP:] * ys_ref[...] * dsl[:, DP:]).astype(BF16)

    row = lambda i: (i, 0)
    const = lambda i: (0, 0)
    return pl.pallas_call(
        body, name="even_bwd", grid=(s // tm,),
        in_specs=[pl.BlockSpec((tm, D), row), pl.BlockSpec((tm, D), row),
                  pl.BlockSpec((tm, DP), row), pl.BlockSpec((tm, DP), row),
                  pl.BlockSpec((tm, DI), lambda i: (i, 2)),
                  pl.BlockSpec((DI, D), const), pl.BlockSpec((8, D), const)],
        out_specs=[pl.BlockSpec((tm, D), row), pl.BlockSpec((tm, DP), row), pl.BlockSpec((tm, DP), row),
                   pl.BlockSpec((tm, DI), row), pl.BlockSpec((8, D), const)],
        out_shape=[jax.ShapeDtypeStruct((s, D), BF16), jax.ShapeDtypeStruct((s, DP), F32),
                   jax.ShapeDtypeStruct((s, DP), F32), jax.ShapeDtypeStruct((s, DI), BF16),
                   jax.ShapeDtypeStruct((8, D), F32)],
        compiler_params=_params("arbitrary"),
    )(dx1, out0, ypool, ysb, proj0, wout, vecs)


def _pool_bwd(dyp, p, pw, pscale):
    s = dyp.shape[0]
    tm = min(TM, s)
    nb = s // tm
    hb = tm // 16

    def body(dy_ref, dyh_ref, p_ref, w_ref, sc_ref, du_ref, dw_ref, acc_ref, ext_ref):
        i = pl.program_id(0)

        @pl.when(i == 0)
        def _():
            dw_ref[...] = jnp.zeros_like(dw_ref)
            acc_ref[...] = jnp.zeros_like(acc_ref)

        t = i * tm + lax.broadcasted_iota(jnp.int32, (tm + 16, 1), 0)
        for g, wdw in enumerate(WINDOWS):
            cs = slice(g * PG, (g + 1) * PG)
            sc = sc_ref[:, cs]
            dy = dy_ref[:, cs]
            dyh = jnp.where(i < nb - 1, dyh_ref[:, cs], 0.0)
            pb = p_ref[:, cs]
            wg = w_ref[g]
            acc_ref[0:1, cs] += _rowsum(dy * _dot(pb, wg))
            dypre = (dy * sc).astype(BF16)
            dw_ref[g] += _dot_tn(pb, dypre)
            dp = _dot_nt(dypre, wg)
            dph = _dot_nt((dyh * sc).astype(BF16), wg)
            inv = 1.0 / jnp.minimum(t + 1, wdw).astype(F32)
            ext_ref[0:tm, cs] = dp * inv[0:tm]
            ext_ref[tm:tm + 16, cs] = dph * inv[tm:tm + 16]
            acc = ext_ref[0:tm, cs]
            for j in range(1, wdw):
                acc = acc + ext_ref[j:j + tm, cs]
            du_ref[:, cs] = (acc - dp).astype(BF16)

    row = lambda i: (i, 0)
    return pl.pallas_call(
        body, name="pool_bwd", grid=(nb,),
        in_specs=[pl.BlockSpec((tm, DP), row),
                  pl.BlockSpec((16, DP), lambda i: (jnp.minimum((i + 1) * hb, s // 16 - 1), 0)),
                  pl.BlockSpec((tm, DP), row),
                  pl.BlockSpec((4, PG, PG), lambda i: (0, 0, 0)),
                  pl.BlockSpec((1, DP), lambda i: (0, 0))],
        out_specs=[pl.BlockSpec((tm, DP), row), pl.BlockSpec((4, PG, PG), lambda i: (0, 0, 0)),
                   pl.BlockSpec((8, DP), lambda i: (0, 0))],
        out_shape=[jax.ShapeDtypeStruct((s, DP), BF16), jax.ShapeDtypeStruct((4, PG, PG), F32),
                   jax.ShapeDtypeStruct((8, DP), F32)],
        scratch_shapes=[pltpu.VMEM((tm + 16, DP), F32)],
        compiler_params=_params("arbitrary"),
    )(dyp, dyp, p, pw, pscale)


def _attn_bwd(proj0, ysb, dys):
    s = proj0.shape[0]
    nq = s // BQ

    def body(q_ref, k_ref, v_ref, o_ref, do_ref, dq_ref, dk_ref, dv_ref,
             qb_ref, kb_ref, vb_ref, dob_ref, dka_ref, dva_ref):
        qb_ref[...] = q_ref[...].astype(BF16)
        kb_ref[...] = k_ref[...].astype(BF16)
        vb_ref[...] = v_ref[...].astype(BF16)
        dob_ref[...] = do_ref[...].astype(BF16)
        dka_ref[...] = jnp.zeros_like(dka_ref)
        dva_ref[...] = jnp.zeros_like(dva_ref)
        row = lax.broadcasted_iota(jnp.int32, (BQ, BQ), 0)
        col = lax.broadcasted_iota(jnp.int32, (BQ, BQ), 1)
        causal = col < row
        after = (row > col).astype(BF16)
        from_on = (row >= col).astype(BF16)

        for h in range(2):
            ls = slice(HD * h, HD * (h + 1))

            def qstep(qi, carry, ls=ls):
                q0 = pl.multiple_of(qi * BQ, BQ)
                qh = qb_ref[pl.ds(q0, BQ), ls]
                doh = dob_ref[pl.ds(q0, BQ), ls]
                total = jnp.sum(doh.astype(F32) * o_ref[pl.ds(q0, BQ), ls], axis=1, keepdims=True)

                def block(k0, c_after, c_g, diag):
                    kh = kb_ref[pl.ds(k0, BQ), ls]
                    vh = vb_ref[pl.ds(k0, BQ), ls]
                    lf, ls_ = _sb_logits(qh, kh)
                    if diag:
                        lf = jnp.where(causal, lf, 0.0)
                    aft = _split_dot(lf, after)
                    a = jnp.exp(ls_ + aft + c_after)
                    if diag:
                        a = jnp.where(causal, a, 0.0)
                    ab = a.astype(BF16)
                    g = _dot_nt(doh, vh) * ab.astype(F32)
                    suf = _split_dot(g, from_on)
                    before = total - (suf + c_g)
                    dz = (g - jnp.exp(ls_) * (g + before)) * INV_SQRT_HD
                    if diag:
                        dz = jnp.where(causal, dz, 0.0)
                    dzb = dz.astype(BF16)
                    dka_ref[pl.ds(k0, BQ), ls] += _dot_tn(dzb, qh)
                    dva_ref[pl.ds(k0, BQ), ls] += _dot_tn(ab, doh)
                    return _dot(dzb, kh), aft[:, 0:1] + lf[:, 0:1], suf[:, 0:1]

                zero = jnp.zeros((BQ, 1), F32)
                dq, c_after, c_g = block(q0, zero, zero, True)

                def kstep(j, kc):
                    dq_acc, ca, cg = kc
                    k0 = pl.multiple_of((qi - 1 - j) * BQ, BQ)
                    dq_blk, rs, gs = block(k0, ca, cg, False)
                    return dq_acc + dq_blk, ca + rs, cg + gs

                dq, c_after, c_g = lax.fori_loop(0, qi, kstep, (dq, c_after, c_g))
                dq_ref[pl.ds(q0, BQ), ls] = dq.astype(BF16)
                return carry

            lax.fori_loop(0, nq, qstep, 0)
        dk_ref[...] = dka_ref[...].astype(BF16)
        dv_ref[...] = dva_ref[...].astype(BF16)

    col = lambda h: (0, h)
    return pl.pallas_call(
        body, name="attn_bwd", grid=(DP // 128,),
        in_specs=[pl.BlockSpec((s, 128), lambda h: (0, 8 + h)),
                  pl.BlockSpec((s, 128), lambda h: (0, 16 + h)),
                  pl.BlockSpec((s, 128), lambda h: (0, 24 + h)),
                  pl.BlockSpec((s, 128), col), pl.BlockSpec((s, 128), col)],
        out_specs=[pl.BlockSpec((s, 128), col)] * 3,
        out_shape=[jax.ShapeDtypeStruct((s, DP), BF16)] * 3,
        scratch_shapes=[pltpu.VMEM((s, 128), BF16)] * 4 + [pltpu.VMEM((s, 128), F32)] * 2,
        compiler_params=_params("parallel"),
    )(proj0, proj0, proj0, ysb, dys)


def _adamw_math(w, g, m, v):
    m2 = B1 * m + (1.0 - B1) * g
    v2 = B2 * v + (1.0 - B2) * (g * g)
    m_hat = m2 / (1.0 - B1 ** STEP)
    v_hat = v2 / (1.0 - B2 ** STEP)
    return -LR * (m_hat / (jnp.sqrt(v_hat) + EPS_ADAM) + WD * w), m2, v2


def _adamw(w, g, m, v, name):
    r, c = w.shape
    tr = r
    while tr * c * 4 > (1 << 20) and tr % 16 == 0:
        tr //= 2

    def body(w_ref, g_ref, m_ref, v_ref, d_ref, m2_ref, v2_ref):
        d_ref[...], m2_ref[...], v2_ref[...] = _adamw_math(w_ref[...], g_ref[...], m_ref[...], v_ref[...])

    spec = pl.BlockSpec((tr, c), lambda i: (i, 0))
    return pl.pallas_call(
        body, name=name, grid=(r // tr,),
        in_specs=[spec] * 4, out_specs=[spec] * 3,
        out_shape=[jax.ShapeDtypeStruct((r, c), F32)] * 3,
        compiler_params=_params("parallel"),
    )(w, g, m, v)


def _local_step(x, target, vecs0, vecs1, win0, pw, pscale, wout0, win1, cw8, cb, wout1):
    proj0, h0 = _norm_inproj(x, vecs0, win0, "inproj0")
    p, ypool = _pool_fwd(proj0, pw, pscale)
    ysb = _attn_fwd(proj0)
    x1, out0, yg = _even_out(ypool, ysb, proj0, wout0, x, vecs0)
    proj1, h1 = _norm_inproj(x1, vecs1, win1, "inproj1")
    dx2, out1, y1, acc_f = _odd_out(proj1, wout1, x1, vecs1, cw8, cb, target)

    dout1, dproj1, acc_cv, acc_g1 = _odd_bwd(dx2, out1, proj1, wout1, vecs1, cw8, cb)
    g_wout1 = _grad_w_rows(y1, dout1, "grad_wout1")
    g_win1 = _grad_w_cols(h1, dproj1, "grad_win1")
    dx1, acc_n1 = _inproj_bwd(dproj1, win1, x1, dx2, vecs1, "inproj1_bwd")

    dout0, dyp, dys, dgate0, acc_g0 = _even_bwd(dx1, out0, ypool, ysb, proj0, wout0, vecs0)
    g_wout0 = _grad_w_rows(yg, dout0, "grad_wout0")
    du, g_pw, acc_ps = _pool_bwd(dyp, p, pw, pscale)
    dq, dk, dv = _attn_bwd(proj0, ysb, dys)
    dproj0 = jnp.concatenate([du, dq, dk, dv, dgate0], axis=1)
    g_win0 = _grad_w_cols(h0, dproj0, "grad_win0")
    dx0, acc_n0 = _inproj_bwd(dproj0, win0, x, dx1, vecs0, "inproj0_bwd")

    sums = dict(
        dm0=jnp.concatenate([acc_n0[0:2], acc_g0[0:1]], axis=0),
        dm1=jnp.concatenate([acc_n1[0:2], acc_g1[0:1]], axis=0),
        norm_g=jnp.concatenate([acc_n0[2:3], acc_n1[2:3]], axis=0),
        pool_scale=acc_ps[0:1], final_g=acc_f[0:1], loss=acc_f[1:2],
        conv_w=acc_cv[0:3], conv_b=acc_cv[3:4])
    return dx0, (g_win0, g_pw, g_wout0, g_win1, g_wout1), sums


ANY = pl.BlockSpec(memory_space=pl.ANY)
CHIP_FLIPS = ((1, 0), (0, 1), (1, 1))


def _place():
    return lax.axis_index("x"), lax.axis_index("y"), lax.axis_index("c")


def _flip(v, f):
    return 1 - v if f else v


def _allgather8(v, name):
    m_per, n = v.shape

    def body(x_ref, out_ref, send_sems, recv_sems, local_sem):
        x, y, c = _place()
        me, sibling = (x, y, c), (x, y, 1 - c)
        chips = [(_flip(x, fx), _flip(y, fy)) for fx, fy in CHIP_FLIPS]

        def rows(px, py, pc):
            return out_ref.at[pl.ds((4 * px + 2 * py + pc) * m_per, m_per), :]

        def copy(k, block, to, src=None):
            return pltpu.make_async_remote_copy(
                src_ref=rows(*block) if src is None else src, dst_ref=rows(*block),
                send_sem=send_sems.at[k], recv_sem=recv_sems.at[k], device_id=to, device_id_type=MESH)

        mine = pltpu.make_async_copy(x_ref, rows(*me), local_sem)
        mine.start()
        first = [copy(0, me, sibling, src=x_ref)]
        first += [copy(1 + j, me, (*chip, c), src=x_ref) for j, chip in enumerate(chips)]
        for cp in first:
            cp.start()
        passed = [copy(4 + j, (*chip, c), sibling) for j, chip in enumerate(chips)]
        for j, chip in enumerate(chips):
            copy(1 + j, (*chip, c), me).wait_recv()
            passed[j].start()
        copy(0, sibling, me).wait_recv()
        for j, chip in enumerate(chips):
            copy(4 + j, (*chip, 1 - c), me).wait_recv()
        for cp in first + passed:
            cp.wait_send()
        mine.wait()

    return pl.pallas_call(
        body, name=name,
        out_shape=jax.ShapeDtypeStruct((NDEV * m_per, n), v.dtype),
        in_specs=[pl.BlockSpec(memory_space=pltpu.VMEM)],
        out_specs=pl.BlockSpec(memory_space=pltpu.VMEM),
        scratch_shapes=[pltpu.SemaphoreType.DMA((7,)), pltpu.SemaphoreType.DMA((7,)), pltpu.SemaphoreType.DMA],
    )(v)


def _gather_weights(shards):
    n = len(shards)

    def body(*refs):
        ins, outs = refs[:n], refs[n:2 * n]
        send_sems, recv_sems, local_sems = refs[2 * n:]
        x, y, c = _place()
        me = 2 * x + y
        sends = []
        for w in range(n):
            pltpu.make_async_copy(ins[w], outs[w].at[me], local_sems.at[w]).start()
            for d, (fx, fy) in enumerate(CHIP_FLIPS):
                cp = pltpu.make_async_remote_copy(
                    src_ref=ins[w], dst_ref=outs[w].at[me], send_sem=send_sems.at[3 * w + d],
                    recv_sem=recv_sems.at[3 * w + d], device_id=(_flip(x, fx), _flip(y, fy), c), device_id_type=MESH)
                cp.start()
                sends.append(cp)
        for w in range(n):
            for d, (fx, fy) in enumerate(CHIP_FLIPS):
                px, py = _flip(x, fx), _flip(y, fy)
                pltpu.make_async_remote_copy(
                    src_ref=ins[w], dst_ref=outs[w].at[2 * px + py], send_sem=send_sems.at[3 * w + d],
                    recv_sem=recv_sems.at[3 * w + d], device_id=(px, py, c), device_id_type=MESH).wait_recv()
        for cp in sends:
            cp.wait_send()
        for w in range(n):
            pltpu.make_async_copy(ins[w], outs[w].at[me], local_sems.at[w]).wait()

    return pl.pallas_call(
        body, name="gather_weights",
        out_shape=[jax.ShapeDtypeStruct((NCHIP,) + a.shape, a.dtype) for a in shards],
        in_specs=[ANY] * n, out_specs=[ANY] * n,
        scratch_shapes=[pltpu.SemaphoreType.DMA((3 * n,)), pltpu.SemaphoreType.DMA((3 * n,)),
                        pltpu.SemaphoreType.DMA((n,))],
    )(*shards)


def _send_halves(grads):
    n = len(grads)

    def body(*refs):
        ins, outs = refs[:n], refs[n:2 * n]
        send_sems, recv_sems = refs[2 * n:]
        x, y, c = _place()
        copies = []
        for w in range(n):
            r2 = ins[w].shape[1] // 2
            cp = pltpu.make_async_remote_copy(
                src_ref=ins[w].at[:, pl.ds((1 - c) * r2, r2), :], dst_ref=outs[w],
                send_sem=send_sems.at[w], recv_sem=recv_sems.at[w], device_id=(x, y, 1 - c), device_id_type=MESH)
            cp.start()
            copies.append(cp)
        for cp in copies:
            cp.wait_recv()
        for cp in copies:
            cp.wait_send()

    return pl.pallas_call(
        body, name="rs_send_halves",
        out_shape=[jax.ShapeDtypeStruct((NCHIP, g.shape[1] // 2, g.shape[2]), F32) for g in grads],
        in_specs=[ANY] * n, out_specs=[ANY] * n,
        scratch_shapes=[pltpu.SemaphoreType.DMA((n,)), pltpu.SemaphoreType.DMA((n,))],
    )(*grads)


def _row_tile(rows, cols):
    tr = rows
    while tr * cols * 4 > (1 << 20) and tr % 16 == 0:
        tr //= 2
    return tr


def _add_halves(g, t1, core, name):
    _, r, cdim = g.shape
    r2 = r // 2
    tr = _row_tile(r2, cdim)
    nt = r2 // tr

    def body(core_ref, g_ref, t_ref, p_ref, pb_ref):
        p = g_ref[...] + t_ref[...]
        p_ref[...] = p
        pb_ref[...] = p.astype(BF16)

    blk = pl.BlockSpec((None, tr, cdim), lambda j, i, core_ref: (j, i, 0))
    return pl.pallas_call(
        body, name=name,
        grid_spec=pltpu.PrefetchScalarGridSpec(
            num_scalar_prefetch=1, grid=(NCHIP, nt),
            in_specs=[pl.BlockSpec((None, tr, cdim), lambda j, i, core_ref: (j, core_ref[0] * nt + i, 0)), blk],
            out_specs=[blk, blk]),
        out_shape=[jax.ShapeDtypeStruct((NCHIP, r2, cdim), F32), jax.ShapeDtypeStruct((NCHIP, r2, cdim), BF16)],
        compiler_params=_params("parallel", "parallel"),
    )(core, g, t1)


def _exchange_partials(parts):
    n = len(parts)

    def body(*refs):
        ins, outs = refs[:n], refs[n:2 * n]
        send_sems, recv_sems = refs[2 * n:]
        x, y, c = _place()
        copies = []
        for w in range(n):
            for d, (fx, fy) in enumerate(CHIP_FLIPS):
                px, py = _flip(x, fx), _flip(y, fy)
                cp = pltpu.make_async_remote_copy(
                    src_ref=ins[w].at[2 * px + py], dst_ref=outs[w].at[d], send_sem=send_sems.at[3 * w + d],
                    recv_sem=recv_sems.at[3 * w + d], device_id=(px, py, c), device_id_type=MESH)
                cp.start()
                copies.append(cp)
        for cp in copies:
            cp.wait_recv()
        for cp in copies:
            cp.wait_send()

    return pl.pallas_call(
        body, name="rs_exchange_partials",
        out_shape=[jax.ShapeDtypeStruct((3,) + p.shape[1:], BF16) for p in parts],
        in_specs=[ANY] * n, out_specs=[ANY] * n,
        scratch_shapes=[pltpu.SemaphoreType.DMA((3 * n,)), pltpu.SemaphoreType.DMA((3 * n,))],
    )(*parts)


def _add_partials(p, t2, chip, name):
    _, r2, cdim = p.shape
    tr = _row_tile(r2, cdim)

    def body(chip_ref, p_ref, t_ref, o_ref):
        o_ref[...] = ((p_ref[...] + t_ref[0].astype(F32)) + t_ref[1].astype(F32)) + t_ref[2].astype(F32)

    return pl.pallas_call(
        body, name=name,
        grid_spec=pltpu.PrefetchScalarGridSpec(
            num_scalar_prefetch=1, grid=(r2 // tr,),
            in_specs=[pl.BlockSpec((None, tr, cdim), lambda i, chip_ref: (chip_ref[0], i, 0)),
                      pl.BlockSpec((3, tr, cdim), lambda i, chip_ref: (0, i, 0))],
            out_specs=pl.BlockSpec((tr, cdim), lambda i, chip_ref: (i, 0))),
        out_shape=jax.ShapeDtypeStruct((r2, cdim), F32),
        compiler_params=_params("parallel"),
    )(chip, p, t2)


def _join_halves(halves):
    n = len(halves)

    def body(*refs):
        ins, outs = refs[:n], refs[n:2 * n]
        send_sems, recv_sems, local_sems = refs[2 * n:]
        x, y, c = _place()
        copies = []
        for w in range(n):
            r2 = ins[w].shape[0]
            pltpu.make_async_copy(ins[w], outs[w].at[pl.ds(c * r2, r2), :], local_sems.at[w]).start()
            cp = pltpu.make_async_remote_copy(
                src_ref=ins[w], dst_ref=outs[w].at[pl.ds(c * r2, r2), :], send_sem=send_sems.at[w],
                recv_sem=recv_sems.at[w], device_id=(x, y, 1 - c), device_id_type=MESH)
            cp.start()
            copies.append(cp)
        for w in range(n):
            r2 = ins[w].shape[0]
            pltpu.make_async_remote_copy(
                src_ref=ins[w], dst_ref=outs[w].at[pl.ds((1 - c) * r2, r2), :], send_sem=send_sems.at[w],
                recv_sem=recv_sems.at[w], device_id=(x, y, 1 - c), device_id_type=MESH).wait_recv()
        for cp in copies:
            cp.wait_send()
        for w in range(n):
            r2 = ins[w].shape[0]
            pltpu.make_async_copy(ins[w], outs[w].at[pl.ds(c * r2, r2), :], local_sems.at[w]).wait()

    return pl.pallas_call(
        body, name="rs_join_halves",
        out_shape=[jax.ShapeDtypeStruct((2 * h.shape[0], h.shape[1]), F32) for h in halves],
        in_specs=[ANY] * n, out_specs=[ANY] * n,
        scratch_shapes=[pltpu.SemaphoreType.DMA((n,)), pltpu.SemaphoreType.DMA((n,)), pltpu.SemaphoreType.DMA((n,))],
    )(*halves)


def _ada_fwd(c_all, ada_w):
    nl, _, ns = ada_w.shape

    def body(c_ref, w_ref, o_ref):
        cv = c_ref[...]
        o_ref[...] = _dot((cv * jax.nn.sigmoid(cv)).astype(BF16), w_ref[...].astype(BF16))

    return pl.pallas_call(
        body, name="ada_fwd", grid=(nl,),
        in_specs=[pl.BlockSpec((NDEV, D), lambda i: (0, 0)), pl.BlockSpec((None, D, ns), lambda i: (i, 0, 0))],
        out_specs=pl.BlockSpec((None, NDEV, ns), lambda i: (i, 0, 0)),
        out_shape=jax.ShapeDtypeStruct((nl, NDEV, ns), F32),
        compiler_params=_params("parallel"),
    )(c_all, ada_w)


PACK_ROWS = 24


def _reduce_packed(gathered):
    def body(g_ref, tot_ref, loss_ref):
        tot = g_ref[0:PACK_ROWS, :]
        for dev in range(1, NDEV):
            tot = tot + g_ref[dev * PACK_ROWS:(dev + 1) * PACK_ROWS, :]
        tot_ref[...] = tot
        loss_ref[...] = jnp.zeros((8, 128), F32) + jnp.sum(tot[10:11, :])

    return pl.pallas_call(
        body, name="reduce_packed",
        out_shape=[jax.ShapeDtypeStruct((PACK_ROWS, D), F32), jax.ShapeDtypeStruct((8, 128), F32)],
    )(gathered)


def _ada_w_update(c_t, dms, w, m, v):
    nl, _, ns = w.shape
    tr = 256

    def body(ct_ref, dm_ref, w_ref, m_ref, v_ref, g_ref, d_ref, m2_ref, v2_ref):
        ct = ct_ref[...]
        sc = ct * jax.nn.sigmoid(ct)
        dm = dm_ref[...]
        g = sc[:, 0:1] * dm[0:1, :]
        for b in range(1, NDEV):
            g = g + sc[:, b:b + 1] * dm[b:b + 1, :]
        g_ref[...] = g
        d_ref[...], m2_ref[...], v2_ref[...] = _adamw_math(w_ref[...], g, m_ref[...], v_ref[...])

    blk = pl.BlockSpec((None, tr, ns), lambda i, j: (i, j, 0))
    return pl.pallas_call(
        body, name="ada_w_update", grid=(nl, D // tr),
        in_specs=[pl.BlockSpec((tr, NDEV), lambda i, j: (j, 0)),
                  pl.BlockSpec((None, NDEV, ns), lambda i, j: (i, 0, 0)), blk, blk, blk],
        out_specs=[blk] * 4,
        out_shape=[jax.ShapeDtypeStruct((nl, D, ns), F32)] * 4,
        compiler_params=_params("parallel", "parallel"),
    )(c_t, dms, w, m, v)


def _chip_major(a, parts):
    g, _, cdim = a.shape
    return jnp.transpose(a.reshape(g, NCHIP, parts, cdim), (1, 0, 2, 3)).reshape(NCHIP, g * parts, cdim)


def kernel(x, c, norm_g, ada_w, ada_b, even_w_in, pool_w, pool_scale, even_w_out, odd_w_in, conv_w, conv_b, odd_w_out, final_g, loss_target, m_norm_g, m_ada_w, m_ada_b, m_even_w_in, m_pool_w, m_pool_scale, m_even_w_out, m_odd_w_in, m_conv_w, m_conv_b, m_odd_w_out, m_final_g, v_norm_g, v_ada_w, v_ada_b, v_even_w_in, v_pool_w, v_pool_scale, v_even_w_out, v_odd_w_in, v_conv_w, v_conv_b, v_odd_w_out, v_final_g):
    ix, iy, ic = _place()
    chip = 2 * ix + iy
    batch = 2 * chip + ic
    chip_arr = jnp.reshape(chip, (1,)).astype(jnp.int32)
    core_arr = jnp.reshape(ic, (1,)).astype(jnp.int32)
    ns_ada = ada_w.shape[2]

    c_all = _allgather8(jnp.pad(c, ((0, 7), (0, 0))), "gather_c")[0::8]
    m_cols = _allgather8(_ada_fwd(c_all, ada_w).reshape(2 * NDEV, ns_ada), "gather_ada")
    m_cols = m_cols.reshape(NCHIP, 2, 2, NDEV, ns_ada)[:, 0]
    m_mine = lax.dynamic_index_in_dim(m_cols, batch, axis=2, keepdims=False)
    m_mine = jnp.transpose(m_mine, (1, 0, 2)).reshape(2, 3 * D) + ada_b
    zrow = jnp.zeros((3, D), F32)

    def vec_rows(i):
        sh, sc, gt = m_mine[i, 0:D], m_mine[i, D:2 * D], m_mine[i, 2 * D:3 * D]
        return jnp.concatenate([jnp.stack([norm_g[i], sc, sh, gt, final_g]), zrow], axis=0)

    win0, pw, wout0, win1, wout1 = _gather_weights([
        even_w_in[0].astype(BF16), pool_w[0].astype(BF16), even_w_out[0].astype(BF16),
        odd_w_in[0].astype(BF16), odd_w_out[0].astype(BF16)])
    pw_full = jnp.transpose(pw, (1, 0, 2, 3)).reshape(4, PG, PG)
    cw_rows = _allgather8(jnp.concatenate([conv_w[0], conv_b, jnp.zeros((4, conv_b.shape[1]), F32)], axis=0),
                          "gather_conv")
    cw_rows = cw_rows.reshape(NCHIP, 2, 8, -1)[:, 0]
    cw_full = jnp.transpose(cw_rows, (1, 0, 2)).reshape(8, DI)
    cw8 = jnp.concatenate([cw_full[0:3], jnp.zeros((5, DI), F32)], axis=0)
    cb_full = cw_full[3:4]

    dx0, grads, sums = _local_step(
        x[0], loss_target[0], vec_rows(0), vec_rows(1), win0, pw_full, pool_scale,
        wout0.reshape(DI, D), win1, cw8, cb_full, wout1.reshape(DI, D))
    g_win0, g_pw, g_wout0, g_win1, g_wout1 = grads

    big = [g_win0, _chip_major(g_pw, PG // NCHIP), g_wout0, g_win1, g_wout1]
    names = ["win0", "pool", "wout0", "win1", "wout1"]
    from_sibling = _send_halves(big)
    part = [_add_halves(g, t, core_arr, "rs_add_halves_" + nm) for g, t, nm in zip(big, from_sibling, names)]
    received = _exchange_partials([pb for _, pb in part])
    halves = [_add_partials(p, t, chip_arr, "rs_add_partials_" + nm) for (p, _), t, nm in zip(part, received, names)]
    r_win0, r_pw, r_wout0, r_win1, r_wout1 = _join_halves(halves)

    packed = jnp.concatenate([
        sums["dm0"], sums["dm1"], sums["norm_g"], sums["pool_scale"], sums["final_g"], sums["loss"],
        sums["conv_w"].reshape(6, D), sums["conv_b"].reshape(2, D), jnp.zeros((PACK_ROWS - 19, D), F32)], axis=0)
    gathered = _allgather8(packed, "gather_sums")
    tot, loss8 = _reduce_packed(gathered)
    loss = loss8[0, 0]
    g_norm_g, g_pool_scale, g_final_g = tot[6:8], tot[8:9], tot[9]
    g_ada_b = tot[0:6].reshape(2, 3 * D)
    g_conv_w = lax.dynamic_slice_in_dim(tot[11:17].reshape(3, DI), chip * (DI // NCHIP), DI // NCHIP, axis=1)
    g_conv_b = lax.dynamic_slice_in_dim(tot[17:19].reshape(1, DI), chip * (DI // NCHIP), DI // NCHIP, axis=1)
    dm_all = gathered.reshape(NDEV, PACK_ROWS, D)[:, 0:6].reshape(NDEV, 2, 3 * D)
    dm_cols = jnp.transpose(lax.dynamic_slice_in_dim(dm_all, chip * ns_ada, ns_ada, axis=2), (1, 0, 2))
    g_ada_w, d_ada_w, nm_ada_w, nv_ada_w = _ada_w_update(jnp.transpose(c_all), dm_cols, ada_w, m_ada_w, v_ada_w)

    def upd(w, g, m, v, name):
        shape = w.shape
        w2, m2, v2 = (a.reshape(g.shape) for a in (w, m, v))
        d, nm, nv = _adamw(w2, g, m2, v2, name)
        return g.reshape(shape), d.reshape(shape), nm.reshape(shape), nv.reshape(shape)

    o_win0 = upd(even_w_in, r_win0, m_even_w_in, v_even_w_in, "adamw_win0")
    o_pw = upd(pool_w, r_pw, m_pool_w, v_pool_w, "adamw_pool")
    o_wout0 = upd(even_w_out, r_wout0, m_even_w_out, v_even_w_out, "adamw_wout0")
    o_win1 = upd(odd_w_in, r_win1, m_odd_w_in, v_odd_w_in, "adamw_win1")
    o_wout1 = upd(odd_w_out, r_wout1, m_odd_w_out, v_odd_w_out, "adamw_wout1")

    def pack_small(ng, ab, ps, fg, cwv, cbv):
        conv = jnp.concatenate([cwv.reshape(3, -1), cbv.reshape(1, -1)], axis=0).reshape(2, D)
        return jnp.concatenate([ng, ab.reshape(6, D), ps, fg.reshape(1, D), conv, jnp.zeros((4, D), F32)], axis=0)

    sw = pack_small(norm_g, ada_b, pool_scale, final_g, conv_w, conv_b)
    sg = pack_small(g_norm_g, g_ada_b, g_pool_scale, g_final_g, g_conv_w, g_conv_b)
    sm = pack_small(m_norm_g, m_ada_b, m_pool_scale, m_final_g, m_conv_w, m_conv_b)
    sv = pack_small(v_norm_g, v_ada_b, v_pool_scale, v_final_g, v_conv_w, v_conv_b) + jnp.concatenate(
        [jnp.zeros((12, D), F32), jnp.ones((4, D), F32)], axis=0)
    small = _adamw(sw, sg, sm, sv, "adamw_small")

    def unpack_small(a):
        conv = a[10:12].reshape(4, -1)
        return dict(norm_g=a[0:2], ada_b=a[2:8].reshape(2, 3 * D), pool_scale=a[8:9], final_g=a[9],
                    conv_w=conv[0:3].reshape(conv_w.shape), conv_b=conv[3:4].reshape(conv_b.shape))

    s_grad = dict(norm_g=g_norm_g, ada_b=g_ada_b, pool_scale=g_pool_scale, final_g=g_final_g,
                  conv_w=g_conv_w.reshape(conv_w.shape), conv_b=g_conv_b.reshape(conv_b.shape))
    s_out = [s_grad] + [unpack_small(a) for a in small]

    outs = []
    for k in range(4):
        sm_k = s_out[k]
        outs.append([sm_k["norm_g"], (g_ada_w, d_ada_w, nm_ada_w, nv_ada_w)[k], sm_k["ada_b"], o_win0[k], o_pw[k],
                     sm_k["pool_scale"], o_wout0[k], o_win1[k], sm_k["conv_w"], sm_k["conv_b"], o_wout1[k],
                     sm_k["final_g"]])
    return (loss, dx0[None], *outs[0], *outs[1], *outs[2], *outs[3])
```

```python
import functools

import jax
import jax.numpy as jnp
from jax import lax
from jax.experimental import pallas as pl
from jax.experimental.pallas import tpu as pltpu

F32 = jnp.float32
BF16 = jnp.bfloat16
MESH = pl.DeviceIdType.MESH

D = 1024
DI = 2048
DP = 1024
NE = 6144
NO = 8192
WINDOWS = (2, 4, 8, 16)
PG = 256
HD = 64
NCHIP = 4
NDEV = 8
EPS = 1e-6
INV_SQRT_HD = 0.125

LR, B1, B2, EPS_ADAM, WD, STEP = 0.001, 0.9, 0.999, 1e-08, 0.01, 10

TM = 512
TME = 256
CT = 512
BQ = 512
BK = 256
VMEM_LIMIT = 56 * 1024 * 1024


def _dot(a, b):
    return jnp.dot(a, b, preferred_element_type=F32)


def _dot_nt(a, b):
    return lax.dot_general(a, b, (((1,), (1,)), ((), ())), preferred_element_type=F32)


def _dot_tn(a, b):
    return lax.dot_general(a, b, (((0,), (0,)), ((), ())), preferred_element_type=F32)


def _params(*sem):
    return pltpu.CompilerParams(dimension_semantics=sem, vmem_limit_bytes=VMEM_LIMIT)


def _rowsum(v):
    return jnp.sum(v, axis=0, keepdims=True)


def _norm_inproj(x, vecs, w, name):
    s = x.shape[0]
    ns = w.shape[2]
    n = NCHIP * ns
    tps = ns // CT
    tm = min(TM, s)

    def body(x_ref, vec_ref, w_ref, proj_ref, h_ref):
        @pl.when(pl.program_id(1) == 0)
        def _():
            xv = x_ref[...]
            r = lax.rsqrt(jnp.mean(xv * xv, axis=-1, keepdims=True) + EPS)
            h = ((xv * r) * vec_ref[0:1, :]) * (1.0 + vec_ref[1:2, :]) + vec_ref[2:3, :]
            h_ref[...] = h.astype(BF16)

        proj_ref[...] = _dot(h_ref[...], w_ref[...])

    return pl.pallas_call(
        body, name=name, grid=(s // tm, n // CT),
        in_specs=[pl.BlockSpec((tm, D), lambda i, j: (i, 0)),
                  pl.BlockSpec((8, D), lambda i, j: (0, 0)),
                  pl.BlockSpec((None, D, CT), lambda i, j: (j // tps, 0, j % tps))],
        out_specs=[pl.BlockSpec((tm, CT), lambda i, j: (i, j)),
                   pl.BlockSpec((tm, D), lambda i, j: (i, 0))],
        out_shape=[jax.ShapeDtypeStruct((s, n), F32), jax.ShapeDtypeStruct((s, D), BF16)],
        compiler_params=_params("parallel", "arbitrary"),
    )(x, vecs, w)


def _pool_fwd(proj0, pw, pscale):
    s = proj0.shape[0]
    tm = min(TM, s)
    hb = tm // 16

    def body(u_ref, halo_ref, w_ref, sc_ref, p_ref, y_ref, ext_ref):
        i = pl.program_id(0)
        ext_ref[16:, :] = u_ref[...]
        ext_ref[0:16, :] = jnp.where(i > 0, halo_ref[...], 0.0)
        t = i * tm + lax.broadcasted_iota(jnp.int32, (tm, 1), 0)
        for g, wdw in enumerate(WINDOWS):
            cs = slice(g * PG, (g + 1) * PG)
            u = ext_ref[16:16 + tm, cs]
            acc = u
            for j in range(1, wdw):
                acc = acc + ext_ref[16 - j:16 - j + tm, cs]
            inv = 1.0 / jnp.minimum(t + 1, wdw).astype(F32)
            pb = (acc * inv - u).astype(BF16)
            p_ref[:, cs] = pb
            y_ref[:, cs] = _dot(pb, w_ref[g]) * sc_ref[:, cs]

    return pl.pallas_call(
        body, name="pool_fwd", grid=(s // tm,),
        in_specs=[pl.BlockSpec((tm, DP), lambda i: (i, 0)),
                  pl.BlockSpec((16, DP), lambda i: (jnp.maximum(i * hb - 1, 0), 0)),
                  pl.BlockSpec((4, PG, PG), lambda i: (0, 0, 0)),
                  pl.BlockSpec((1, DP), lambda i: (0, 0))],
        out_specs=[pl.BlockSpec((tm, DP), lambda i: (i, 0)),
                   pl.BlockSpec((tm, DP), lambda i: (i, 0))],
        out_shape=[jax.ShapeDtypeStruct((s, DP), BF16), jax.ShapeDtypeStruct((s, DP), F32)],
        scratch_shapes=[pltpu.VMEM((tm + 16, DP), F32)],
        compiler_params=_params("parallel"),
    )(proj0, proj0, pw, pscale)


def _sb_logits(nz, mask):
    neg_abs = lax.bitcast_convert_type(lax.bitcast_convert_type(nz, jnp.uint32) | jnp.uint32(0x80000000), F32)
    t = jnp.log(1.0 + jnp.exp(neg_abs))
    lf = jnp.minimum(nz, 0.0) - t
    lam = lf - nz
    if mask is not None:
        lf = jnp.where(mask, lf, 0.0)
    return lf, lam


def _split_dot(v, tri):
    hi = v.astype(BF16)
    lo = (v - hi.astype(F32)).astype(BF16)
    return _dot(hi, tri) + _dot(lo, tri)


def _tri_masks():
    row = lax.broadcasted_iota(jnp.int32, (BK, BK), 0)
    col = lax.broadcasted_iota(jnp.int32, (BK, BK), 1)
    return (row > col).astype(BF16), (row >= col).astype(BF16)


def _causal_mask(offset):
    row = lax.broadcasted_iota(jnp.int32, (BQ, BK), 0)
    col = lax.broadcasted_iota(jnp.int32, (BQ, BK), 1)
    return col + offset < row


def _attn_fwd(proj0):
    s = proj0.shape[0]
    nq = s // BQ
    kpq = BQ // BK

    def body(q_ref, k_ref, v_ref, o_ref, qn_ref, k8_ref, vb_ref):
        qn_ref[...] = (-q_ref[...]).astype(BF16)
        k8_ref[...] = (k_ref[...] * INV_SQRT_HD).astype(BF16)
        vb_ref[...] = v_ref[...].astype(BF16)
        after, _ = _tri_masks()
        heads = [slice(HD * h, HD * (h + 1)) for h in range(2)]

        def qstep(qi, carry):
            q0 = pl.multiple_of(qi * BQ, BQ)
            qn = [qn_ref[pl.ds(q0, BQ), ls] for ls in heads]

            def blocks(k_hi, state, masks):
                ks = [pl.multiple_of(k_hi + (kpq - 1 - j) * BK, BK) for j in range(kpq)]
                lanes = [(j, h) for j in range(kpq) for h in range(len(heads))]
                mask = lambda j: None if masks is None else masks[j]
                nz = {jh: _dot_nt(qn[jh[1]], k8_ref[pl.ds(ks[jh[0]], BK), heads[jh[1]]]) for jh in lanes}
                ll = {jh: _sb_logits(nz[jh], mask(jh[0])) for jh in lanes}
                aft = {jh: _dot(ll[jh][0].astype(BF16), after) for jh in lanes}
                new = []
                for h in range(len(heads)):
                    o_acc, c = state[h]
                    for j in range(kpq):
                        a = jnp.exp(ll[j, h][1] + aft[j, h] + c)
                        if masks is not None:
                            a = jnp.where(masks[j], a, 0.0)
                        o_acc = o_acc + _dot(a.astype(BF16), vb_ref[pl.ds(ks[j], BK), heads[h]])
                        c = c + aft[j, h][:, 0:1] + ll[j, h][0][:, 0:1]
                    new.append((o_acc, c))
                return tuple(new)

            state = tuple((jnp.zeros((BQ, HD), F32), jnp.zeros((BQ, 1), F32)) for _ in heads)
            state = blocks(q0, state, [_causal_mask((kpq - 1 - j) * BK) for j in range(kpq)])
            state = lax.fori_loop(0, qi, lambda j, st: blocks(q0 - (j + 1) * BQ, st, None), state)
            for h, ls in enumerate(heads):
                o_ref[pl.ds(q0, BQ), ls] = state[h][0]
            return carry

        lax.fori_loop(0, nq, qstep, 0)

    return pl.pallas_call(
        body, name="attn_fwd", grid=(DP // 128,),
        in_specs=[pl.BlockSpec((s, 128), lambda h: (0, 8 + h)),
                  pl.BlockSpec((s, 128), lambda h: (0, 16 + h)),
                  pl.BlockSpec((s, 128), lambda h: (0, 24 + h))],
        out_specs=pl.BlockSpec((s, 128), lambda h: (0, h)),
        out_shape=jax.ShapeDtypeStruct((s, DP), F32),
        scratch_shapes=[pltpu.VMEM((s, 128), BF16)] * 3,
        compiler_params=_params("parallel"),
    )(proj0, proj0, proj0)


def _even_out(ypool, ysb, proj0, wout, x, vecs):
    s = x.shape[0]
    tm = min(TME, s)

    def body(yp_ref, ys_ref, gate_ref, w_ref, x_ref, vec_ref, x1_ref, out_ref, yg_ref):
        gt = gate_ref[...]
        sl = gt * jax.nn.sigmoid(gt)
        yg_ref[:, :DP] = (yp_ref[...] * sl[:, :DP]).astype(BF16)
        yg_ref[:, DP:] = (ys_ref[...] * sl[:, DP:]).astype(BF16)
        out = _dot(yg_ref[...], w_ref[...])
        out_ref[...] = out
        x1_ref[...] = x_ref[...] + (1.0 + vec_ref[3:4, :]) * out

    row = lambda i: (i, 0)
    return pl.pallas_call(
        body, name="even_out", grid=(s // tm,),
        in_specs=[pl.BlockSpec((tm, DP), row), pl.BlockSpec((tm, DP), row),
                  pl.BlockSpec((tm, DI), lambda i: (i, 2)),
                  pl.BlockSpec((DI, D), lambda i: (0, 0)),
                  pl.BlockSpec((tm, D), row), pl.BlockSpec((8, D), lambda i: (0, 0))],
        out_specs=[pl.BlockSpec((tm, D), row), pl.BlockSpec((tm, D), row), pl.BlockSpec((tm, DI), row)],
        out_shape=[jax.ShapeDtypeStruct((s, D), F32), jax.ShapeDtypeStruct((s, D), F32),
                   jax.ShapeDtypeStruct((s, DI), BF16)],
        compiler_params=_params("parallel"),
    )(ypool, ysb, proj0, wout, x, vecs)


def _odd_out(proj1, wout, x1, vecs, cw, cb, target):
    s = x1.shape[0]
    tm = min(TME, s)
    hb = tm // 8

    def body(gb_ref, gc_ref, u_ref, gt_ref, hgc_ref, hu_ref, w_ref, x1_ref, vec_ref, cw_ref, cb_ref, tg_ref,
             dx2_ref, out_ref, y1_ref, acc_ref, ext_ref):
        i = pl.program_id(0)

        @pl.when(i == 0)
        def _():
            acc_ref[...] = jnp.zeros_like(acc_ref)

        ext_ref[8:, :] = gc_ref[...] * u_ref[...]
        ext_ref[0:8, :] = jnp.where(i > 0, hgc_ref[...] * hu_ref[...], 0.0)
        for c in range(DI // CT):
            cs = slice(c * CT, (c + 1) * CT)
            conv = (cb_ref[0:1, cs] + cw_ref[0:1, cs] * ext_ref[6:6 + tm, cs]
                    + cw_ref[1:2, cs] * ext_ref[7:7 + tm, cs] + cw_ref[2:3, cs] * ext_ref[8:8 + tm, cs])
            gt = gt_ref[:, cs]
            y1_ref[:, cs] = (gb_ref[:, cs] * conv * (gt * jax.nn.sigmoid(gt))).astype(BF16)
        out = _dot(y1_ref[...], w_ref[...])
        out_ref[...] = out
        x2 = x1_ref[...] + (1.0 + vec_ref[3:4, :]) * out
        r = lax.rsqrt(jnp.mean(x2 * x2, axis=-1, keepdims=True) + EPS)
        nrm = x2 * r
        fg = vec_ref[4:5, :]
        err = nrm * fg - tg_ref[...]
        acc_ref[1:2, :] += _rowsum(err * err) * (0.5 / D)
        dyf = err * (1.0 / D)
        acc_ref[0:1, :] += _rowsum(dyf * nrm)
        dn = dyf * fg
        dx2_ref[...] = r * (dn - nrm * jnp.mean(dn * nrm, axis=-1, keepdims=True))

    row = lambda i: (i, 0)
    halo = lambda col: (lambda i: (jnp.maximum(i * hb - 1, 0), col))
    const = lambda i: (0, 0)
    return pl.pallas_call(
        body, name="odd_out", grid=(s // tm,),
        in_specs=[pl.BlockSpec((tm, DI), lambda i: (i, 0)), pl.BlockSpec((tm, DI), lambda i: (i, 1)),
                  pl.BlockSpec((tm, DI), lambda i: (i, 2)), pl.BlockSpec((tm, DI), lambda i: (i, 3)),
                  pl.BlockSpec((8, DI), halo(1)), pl.BlockSpec((8, DI), halo(2)),
                  pl.BlockSpec((DI, D), const), pl.BlockSpec((tm, D), row), pl.BlockSpec((8, D), const),
                  pl.BlockSpec((8, DI), const), pl.BlockSpec((1, DI), const), pl.BlockSpec((tm, D), row)],
        out_specs=[pl.BlockSpec((tm, D), row), pl.BlockSpec((tm, D), row), pl.BlockSpec((tm, DI), row),
                   pl.BlockSpec((8, D), const)],
        out_shape=[jax.ShapeDtypeStruct((s, D), F32), jax.ShapeDtypeStruct((s, D), F32),
                   jax.ShapeDtypeStruct((s, DI), BF16), jax.ShapeDtypeStruct((8, D), F32)],
        scratch_shapes=[pltpu.VMEM((tm + 8, DI), F32)],
        compiler_params=_params("arbitrary"),
    )(proj1, proj1, proj1, proj1, proj1, proj1, wout, x1, vecs, cw, cb, target)


def _odd_bwd(dx2, out1, proj1, wout, vecs, cw, cb):
    s = dx2.shape[0]
    tm = min(TME, s)
    nb = s // tm
    hb = tm // 8

    def body(dx2_ref, out1_ref, gb_ref, gc_ref, u_ref, gt_ref, hgc_ref, hu_ref, w_ref, vec_ref, cw_ref, cb_ref,
             dout_ref, dproj_ref, accv_ref, accd_ref, uext_ref, dext_ref, dy_ref):
        i = pl.program_id(0)
        blk = nb - 1 - i

        @pl.when(i == 0)
        def _():
            accv_ref[...] = jnp.zeros_like(accv_ref)
            accd_ref[...] = jnp.zeros_like(accd_ref)
            dext_ref[tm:tm + 8, :] = jnp.zeros((8, DI), F32)

        dx2v = dx2_ref[...]
        accd_ref[0:1, :] += _rowsum(dx2v * out1_ref[...])
        dout = (dx2v * (1.0 + vec_ref[3:4, :])).astype(BF16)
        dout_ref[...] = dout
        dy_ref[...] = _dot_nt(dout, w_ref[...])
        uext_ref[8:, :] = gc_ref[...] * u_ref[...]
        uext_ref[0:8, :] = jnp.where(blk > 0, hgc_ref[...] * hu_ref[...], 0.0)
        for c in range(DI // CT):
            cs = slice(c * CT, (c + 1) * CT)
            u0 = uext_ref[6:6 + tm, cs]
            u1 = uext_ref[7:7 + tm, cs]
            u2 = uext_ref[8:8 + tm, cs]
            w0, w1, w2 = cw_ref[0:1, cs], cw_ref[1:2, cs], cw_ref[2:3, cs]
            conv = cb_ref[0:1, cs] + w0 * u0 + w1 * u1 + w2 * u2
            gt = gt_ref[:, cs]
            sg = jax.nn.sigmoid(gt)
            gb = gb_ref[:, cs]
            dy = dy_ref[:, cs]
            t1 = dy * (gt * sg)
            dproj_ref[:, cs] = (t1 * conv).astype(BF16)
            dconv = t1 * gb
            dproj_ref[:, 3 * DI + c * CT:3 * DI + (c + 1) * CT] = (
                dy * gb * conv * (sg * (1.0 + gt * (1.0 - sg)))).astype(BF16)
            accv_ref[0:1, cs] += _rowsum(dconv * u0)
            accv_ref[1:2, cs] += _rowsum(dconv * u1)
            accv_ref[2:3, cs] += _rowsum(dconv * u2)
            accv_ref[3:4, cs] += _rowsum(dconv)
            dext_ref[0:tm, cs] = dconv
            duu = w2 * dconv + w1 * dext_ref[1:tm + 1, cs] + w0 * dext_ref[2:tm + 2, cs]
            dproj_ref[:, DI + c * CT:DI + (c + 1) * CT] = (duu * u_ref[:, cs]).astype(BF16)
            dproj_ref[:, 2 * DI + c * CT:2 * DI + (c + 1) * CT] = (duu * gc_ref[:, cs]).astype(BF16)
        dext_ref[tm:tm + 8, :] = dext_ref[0:8, :]

    rrow = lambda i: (nb - 1 - i, 0)
    rcol = lambda col: (lambda i: (nb - 1 - i, col))
    halo = lambda col: (lambda i: (jnp.maximum((nb - 1 - i) * hb - 1, 0), col))
    const = lambda i: (0, 0)
    return pl.pallas_call(
        body, name="odd_bwd", grid=(nb,),
        in_specs=[pl.BlockSpec((tm, D), rrow), pl.BlockSpec((tm, D), rrow),
                  pl.BlockSpec((tm, DI), rcol(0)), pl.BlockSpec((tm, DI), rcol(1)),
                  pl.BlockSpec((tm, DI), rcol(2)), pl.BlockSpec((tm, DI), rcol(3)),
                  pl.BlockSpec((8, DI), halo(1)), pl.BlockSpec((8, DI), halo(2)),
                  pl.BlockSpec((DI, D), const), pl.BlockSpec((8, D), const),
                  pl.BlockSpec((8, DI), const), pl.BlockSpec((1, DI), const)],
        out_specs=[pl.BlockSpec((tm, D), rrow), pl.BlockSpec((tm, NO), rrow),
                   pl.BlockSpec((8, DI), const), pl.BlockSpec((8, D), const)],
        out_shape=[jax.ShapeDtypeStruct((s, D), BF16), jax.ShapeDtypeStruct((s, NO), BF16),
                   jax.ShapeDtypeStruct((8, DI), F32), jax.ShapeDtypeStruct((8, D), F32)],
        scratch_shapes=[pltpu.VMEM((tm + 8, DI), F32), pltpu.VMEM((tm + 8, DI), F32), pltpu.VMEM((tm, DI), F32)],
        compiler_params=_params("arbitrary"),
    )(dx2, out1, proj1, proj1, proj1, proj1, proj1, proj1, wout, vecs, cw, cb)


def _grad_w_cols(a, b, name):
    s, m = a.shape
    ns = b.shape[1] // NCHIP
    ts = min(TM, s)

    def body(a_ref, b_ref, o_ref):
        @pl.when(pl.program_id(2) == 0)
        def _():
            o_ref[...] = jnp.zeros_like(o_ref)

        o_ref[...] += _dot_tn(a_ref[...], b_ref[...])

    return pl.pallas_call(
        body, name=name, grid=(m // CT, NCHIP, s // ts),
        in_specs=[pl.BlockSpec((ts, CT), lambda i, j, k: (k, i)),
                  pl.BlockSpec((ts, ns), lambda i, j, k: (k, j))],
        out_specs=pl.BlockSpec((None, CT, ns), lambda i, j, k: (j, i, 0)),
        out_shape=jax.ShapeDtypeStruct((NCHIP, m, ns), F32),
        compiler_params=_params("parallel", "parallel", "arbitrary"),
    )(a, b)


def _grad_w_rows(a, b, name):
    s = a.shape[0]
    ms = a.shape[1] // NCHIP
    n = b.shape[1]
    ts = min(TM, s)

    def body(a_ref, b_ref, o_ref):
        @pl.when(pl.program_id(1) == 0)
        def _():
            o_ref[...] = jnp.zeros_like(o_ref)

        o_ref[...] += _dot_tn(a_ref[...], b_ref[...])

    return pl.pallas_call(
        body, name=name, grid=(NCHIP, s // ts),
        in_specs=[pl.BlockSpec((ts, ms), lambda i, k: (k, i)),
                  pl.BlockSpec((ts, n), lambda i, k: (k, 0))],
        out_specs=pl.BlockSpec((None, ms, n), lambda i, k: (i, 0, 0)),
        out_shape=jax.ShapeDtypeStruct((NCHIP, ms, n), F32),
        compiler_params=_params("parallel", "arbitrary"),
    )(a, b)


def _inproj_bwd(dproj, w, x, dx_in, vecs, name):
    s = x.shape[0]
    ns = w.shape[2]
    n = NCHIP * ns
    tps = ns // CT
    nk = n // CT
    tm = min(TM, s)

    def body(dp_ref, w_ref, x_ref, dxin_ref, vec_ref, dx_ref, acc_ref, dh_ref):
        i = pl.program_id(0)
        k = pl.program_id(1)

        @pl.when((i == 0) & (k == 0))
        def _():
            acc_ref[...] = jnp.zeros_like(acc_ref)

        @pl.when(k == 0)
        def _():
            dh_ref[...] = jnp.zeros_like(dh_ref)

        dh_ref[...] += _dot_nt(dp_ref[...], w_ref[...])

        @pl.when(k == nk - 1)
        def _():
            dh = dh_ref[...]
            xv = x_ref[...]
            r = lax.rsqrt(jnp.mean(xv * xv, axis=-1, keepdims=True) + EPS)
            nrm = xv * r
            g = vec_ref[0:1, :]
            sc1 = 1.0 + vec_ref[1:2, :]
            dhn = dh * nrm
            acc_ref[0:1, :] += _rowsum(dh)
            acc_ref[1:2, :] += _rowsum(dhn) * g
            acc_ref[2:3, :] += _rowsum(dhn) * sc1
            dn = dh * (g * sc1)
            dx_ref[...] = dxin_ref[...] + r * (dn - nrm * jnp.mean(dn * nrm, axis=-1, keepdims=True))

    row = lambda i, k: (i, 0)
    const = lambda i, k: (0, 0)
    return pl.pallas_call(
        body, name=name, grid=(s // tm, nk),
        in_specs=[pl.BlockSpec((tm, CT), lambda i, k: (i, k)),
                  pl.BlockSpec((None, D, CT), lambda i, k: (k // tps, 0, k % tps)),
                  pl.BlockSpec((tm, D), row), pl.BlockSpec((tm, D), row), pl.BlockSpec((8, D), const)],
        out_specs=[pl.BlockSpec((tm, D), row), pl.BlockSpec((8, D), const)],
        out_shape=[jax.ShapeDtypeStruct((s, D), F32), jax.ShapeDtypeStruct((8, D), F32)],
        scratch_shapes=[pltpu.VMEM((tm, D), F32)],
        compiler_params=_params("arbitrary", "arbitrary"),
    )(dproj, w, x, dx_in, vecs)


def _even_bwd(dx1, out0, ypool, ysb, proj0, wout, vecs):
    s = dx1.shape[0]
    tm = min(TME, s)

    def body(dx1_ref, out0_ref, yp_ref, ys_ref, gate_ref, w_ref, vec_ref,
             dout_ref, dyp_ref, dys_ref, dgt_ref, acc_ref):
        @pl.when(pl.program_id(0) == 0)
        def _():
            acc_ref[...] = jnp.zeros_like(acc_ref)

        dx1v = dx1_ref[...]
        acc_ref[0:1, :] += _rowsum(dx1v * out0_ref[...])
        dout = (dx1v * (1.0 + vec_ref[3:4, :])).astype(BF16)
        dout_ref[...] = dout
        dyg = _dot_nt(dout, w_ref[...])
        gt = gate_ref[...]
        sg = jax.nn.sigmoid(gt)
        sl = gt * sg
        dsl = sg * (1.0 + gt * (1.0 - sg))
        dyp_ref[...] = dyg[:, :DP] * sl[:, :DP]
        dys_ref[...] = dyg[:, DP:] * sl[:, DP:]
        dgt_ref[:, :DP] = (dyg[:, :DP] * yp_ref[...] * dsl[:, :DP]).astype(BF16)
        dgt_ref[:, DP:] = (dyg[:, DP:] * ys_ref[...] * dsl[:, DP:]).astype(BF16)

    row = lambda i: (i, 0)
    const = lambda i: (0, 0)
    return pl.pallas_call(
        body, name="even_bwd", grid=(s // tm,),
        in_specs=[pl.BlockSpec((tm, D), row), pl.BlockSpec((tm, D), row),
                  pl.BlockSpec((tm, DP), row), pl.BlockSpec((tm, DP), row),
                  pl.BlockSpec((tm, DI), lambda i: (i, 2)),
                  pl.BlockSpec((DI, D), const), pl.BlockSpec((8, D), const)],
        out_specs=[pl.BlockSpec((tm, D), row), pl.BlockSpec((tm, DP), row), pl.BlockSpec((tm, DP), row),
                   pl.BlockSpec((tm, DI), row), pl.BlockSpec((8, D), const)],
        out_shape=[jax.ShapeDtypeStruct((s, D), BF16), jax.ShapeDtypeStruct((s, DP), F32),
                   jax.ShapeDtypeStruct((s, DP), F32), jax.ShapeDtypeStruct((s, DI), BF16),
                   jax.ShapeDtypeStruct((8, D), F32)],
        compiler_params=_params("arbitrary"),
    )(dx1, out0, ypool, ysb, proj0, wout, vecs)


def _pool_bwd(dyp, p, pw, pscale):
    s = dyp.shape[0]
    tm = min(TM, s)
    nb = s // tm
    hb = tm // 16

    def body(dy_ref, dyh_ref, p_ref, w_ref, sc_ref, du_ref, dw_ref, acc_ref, ext_ref):
        i = pl.program_id(0)

        @pl.when(i == 0)
        def _():
            dw_ref[...] = jnp.zeros_like(dw_ref)
            acc_ref[...] = jnp.zeros_like(acc_ref)

        t = i * tm + lax.broadcasted_iota(jnp.int32, (tm + 16, 1), 0)
        for g, wdw in enumerate(WINDOWS):
            cs = slice(g * PG, (g + 1) * PG)
            sc = sc_ref[:, cs]
            dy = dy_ref[:, cs]
            dyh = jnp.where(i < nb - 1, dyh_ref[:, cs], 0.0)
            pb = p_ref[:, cs]
            wg = w_ref[g]
            acc_ref[0:1, cs] += _rowsum(dy * _dot(pb, wg))
            dypre = (dy * sc).astype(BF16)
            dw_ref[g] += _dot_tn(pb, dypre)
            dp = _dot_nt(dypre, wg)
            dph = _dot_nt((dyh * sc).astype(BF16), wg)
            inv = 1.0 / jnp.minimum(t + 1, wdw).astype(F32)
            ext_ref[0:tm, cs] = dp * inv[0:tm]
            ext_ref[tm:tm + 16, cs] = dph * inv[tm:tm + 16]
            acc = ext_ref[0:tm, cs]
            for j in range(1, wdw):
                acc = acc + ext_ref[j:j + tm, cs]
            du_ref[:, cs] = (acc - dp).astype(BF16)

    row = lambda i: (i, 0)
    return pl.pallas_call(
        body, name="pool_bwd", grid=(nb,),
        in_specs=[pl.BlockSpec((tm, DP), row),
                  pl.BlockSpec((16, DP), lambda i: (jnp.minimum((i + 1) * hb, s // 16 - 1), 0)),
                  pl.BlockSpec((tm, DP), row),
                  pl.BlockSpec((4, PG, PG), lambda i: (0, 0, 0)),
                  pl.BlockSpec((1, DP), lambda i: (0, 0))],
        out_specs=[pl.BlockSpec((tm, DP), row), pl.BlockSpec((4, PG, PG), lambda i: (0, 0, 0)),
                   pl.BlockSpec((8, DP), lambda i: (0, 0))],
        out_shape=[jax.ShapeDtypeStruct((s, DP), BF16), jax.ShapeDtypeStruct((4, PG, PG), F32),
                   jax.ShapeDtypeStruct((8, DP), F32)],
        scratch_shapes=[pltpu.VMEM((tm + 16, DP), F32)],
        compiler_params=_params("arbitrary"),
    )(dyp, dyp, p, pw, pscale)


def _attn_bwd(proj0, ysb, dys):
    s = proj0.shape[0]
    nq = s // BQ
    kpq = BQ // BK

    def body(q_ref, k_ref, v_ref, o_ref, do_ref, dq_ref, dk_ref, dv_ref,
             qn_ref, k8_ref, vb_ref, dob_ref, dka_ref, dva_ref):
        qn_ref[...] = (-q_ref[...]).astype(BF16)
        k8_ref[...] = (k_ref[...] * INV_SQRT_HD).astype(BF16)
        vb_ref[...] = v_ref[...].astype(BF16)
        dob_ref[...] = do_ref[...].astype(BF16)
        dka_ref[...] = jnp.zeros_like(dka_ref)
        dva_ref[...] = jnp.zeros_like(dva_ref)
        after, from_on = _tri_masks()
        heads = [slice(HD * h, HD * (h + 1)) for h in range(2)]

        def qstep(qi, carry):
            q0 = pl.multiple_of(qi * BQ, BQ)
            qn = [qn_ref[pl.ds(q0, BQ), ls] for ls in heads]
            do = [dob_ref[pl.ds(q0, BQ), ls] for ls in heads]
            total = [jnp.sum(do[h].astype(F32) * o_ref[pl.ds(q0, BQ), ls], axis=1, keepdims=True)
                     for h, ls in enumerate(heads)]

            def blocks(k_hi, state, masks):
                ks = [pl.multiple_of(k_hi + (kpq - 1 - j) * BK, BK) for j in range(kpq)]
                lanes = [(j, h) for j in range(kpq) for h in range(len(heads))]
                mask = lambda j: None if masks is None else masks[j]
                k8 = {jh: k8_ref[pl.ds(ks[jh[0]], BK), heads[jh[1]]] for jh in lanes}
                nz = {jh: _dot_nt(qn[jh[1]], k8[jh]) for jh in lanes}
                da = {jh: _dot_nt(do[jh[1]], vb_ref[pl.ds(ks[jh[0]], BK), heads[jh[1]]]) for jh in lanes}
                ll = {jh: _sb_logits(nz[jh], mask(jh[0])) for jh in lanes}
                aft = {jh: _dot(ll[jh][0].astype(BF16), after) for jh in lanes}
                new = []
                for h in range(len(heads)):
                    dq_acc, c, cg = state[h]
                    for j in range(kpq):
                        a = jnp.exp(ll[j, h][1] + aft[j, h] + c)
                        if masks is not None:
                            a = jnp.where(masks[j], a, 0.0)
                        ab = a.astype(BF16)
                        g = da[j, h] * ab.astype(F32)
                        suf = _split_dot(g, from_on)
                        dz = g - jnp.exp(ll[j, h][1]) * (g + ((total[h] - cg) - suf))
                        if masks is not None:
                            dz = jnp.where(masks[j], dz, 0.0)
                        dzb = dz.astype(BF16)
                        dka_ref[pl.ds(ks[j], BK), heads[h]] += _dot_tn(dzb, qn[h])
                        dva_ref[pl.ds(ks[j], BK), heads[h]] += _dot_tn(ab, do[h])
                        dq_acc = dq_acc + _dot(dzb, k8[j, h])
                        c = c + aft[j, h][:, 0:1] + ll[j, h][0][:, 0:1]
                        cg = cg + suf[:, 0:1]
                    new.append((dq_acc, c, cg))
                return tuple(new)

            zero = jnp.zeros((BQ, 1), F32)
            state = tuple((jnp.zeros((BQ, HD), F32), zero, zero) for _ in heads)
            state = blocks(q0, state, [_causal_mask((kpq - 1 - j) * BK) for j in range(kpq)])
            state = lax.fori_loop(0, qi, lambda j, st: blocks(q0 - (j + 1) * BQ, st, None), state)
            for h, ls in enumerate(heads):
                dq_ref[pl.ds(q0, BQ), ls] = state[h][0].astype(BF16)
            return carry

        lax.fori_loop(0, nq, qstep, 0)
        dk_ref[...] = (dka_ref[...] * (-INV_SQRT_HD)).astype(BF16)
        dv_ref[...] = dva_ref[...].astype(BF16)

    col = lambda h: (0, h)
    return pl.pallas_call(
        body, name="attn_bwd", grid=(DP // 128,),
        in_specs=[pl.BlockSpec((s, 128), lambda h: (0, 8 + h)),
                  pl.BlockSpec((s, 128), lambda h: (0, 16 + h)),
                  pl.BlockSpec((s, 128), lambda h: (0, 24 + h)),
                  pl.BlockSpec((s, 128), col), pl.BlockSpec((s, 128), col)],
        out_specs=[pl.BlockSpec((s, 128), col)] * 3,
        out_shape=[jax.ShapeDtypeStruct((s, DP), BF16)] * 3,
        scratch_shapes=[pltpu.VMEM((s, 128), BF16)] * 4 + [pltpu.VMEM((s, 128), F32)] * 2,
        compiler_params=_params("parallel"),
    )(proj0, proj0, proj0, ysb, dys)


def _adamw_math(w, g, m, v):
    m2 = B1 * m + (1.0 - B1) * g
    v2 = B2 * v + (1.0 - B2) * (g * g)
    m_hat = m2 / (1.0 - B1 ** STEP)
    v_hat = v2 / (1.0 - B2 ** STEP)
    return -LR * (m_hat / (jnp.sqrt(v_hat) + EPS_ADAM) + WD * w), m2, v2


def _adamw(w, g, m, v, name):
    r, c = w.shape
    tr = r
    while tr * c * 4 > (1 << 20) and tr % 16 == 0:
        tr //= 2

    def body(w_ref, g_ref, m_ref, v_ref, d_ref, m2_ref, v2_ref):
        d_ref[...], m2_ref[...], v2_ref[...] = _adamw_math(w_ref[...], g_ref[...], m_ref[...], v_ref[...])

    spec = pl.BlockSpec((tr, c), lambda i: (i, 0))
    return pl.pallas_call(
        body, name=name, grid=(r // tr,),
        in_specs=[spec] * 4, out_specs=[spec] * 3,
        out_shape=[jax.ShapeDtypeStruct((r, c), F32)] * 3,
        compiler_params=_params("parallel"),
    )(w, g, m, v)


def _local_step(x, target, vecs0, vecs1, win0, pw, pscale, wout0, win1, cw8, cb, wout1):
    proj0, h0 = _norm_inproj(x, vecs0, win0, "inproj0")
    p, ypool = _pool_fwd(proj0, pw, pscale)
    ysb = _attn_fwd(proj0)
    x1, out0, yg = _even_out(ypool, ysb, proj0, wout0, x, vecs0)
    proj1, h1 = _norm_inproj(x1, vecs1, win1, "inproj1")
    dx2, out1, y1, acc_f = _odd_out(proj1, wout1, x1, vecs1, cw8, cb, target)

    dout1, dproj1, acc_cv, acc_g1 = _odd_bwd(dx2, out1, proj1, wout1, vecs1, cw8, cb)
    g_wout1 = _grad_w_rows(y1, dout1, "grad_wout1")
    g_win1 = _grad_w_cols(h1, dproj1, "grad_win1")
    dx1, acc_n1 = _inproj_bwd(dproj1, win1, x1, dx2, vecs1, "inproj1_bwd")

    dout0, dyp, dys, dgate0, acc_g0 = _even_bwd(dx1, out0, ypool, ysb, proj0, wout0, vecs0)
    g_wout0 = _grad_w_rows(yg, dout0, "grad_wout0")
    du, g_pw, acc_ps = _pool_bwd(dyp, p, pw, pscale)
    dq, dk, dv = _attn_bwd(proj0, ysb, dys)
    dproj0 = jnp.concatenate([du, dq, dk, dv, dgate0], axis=1)
    g_win0 = _grad_w_cols(h0, dproj0, "grad_win0")
    dx0, acc_n0 = _inproj_bwd(dproj0, win0, x, dx1, vecs0, "inproj0_bwd")

    sums = dict(
        dm0=jnp.concatenate([acc_n0[0:2], acc_g0[0:1]], axis=0),
        dm1=jnp.concatenate([acc_n1[0:2], acc_g1[0:1]], axis=0),
        norm_g=jnp.concatenate([acc_n0[2:3], acc_n1[2:3]], axis=0),
        pool_scale=acc_ps[0:1], final_g=acc_f[0:1], loss=acc_f[1:2],
        conv_w=acc_cv[0:3], conv_b=acc_cv[3:4])
    return dx0, (g_win0, g_pw, g_wout0, g_win1, g_wout1), sums


ANY = pl.BlockSpec(memory_space=pl.ANY)
CHIP_FLIPS = ((1, 0), (0, 1), (1, 1))


def _place():
    return lax.axis_index("x"), lax.axis_index("y"), lax.axis_index("c")


def _flip(v, f):
    return 1 - v if f else v


def _allgather8(v, name):
    m_per, n = v.shape

    def body(x_ref, out_ref, send_sems, recv_sems, local_sem):
        x, y, c = _place()
        me, sibling = (x, y, c), (x, y, 1 - c)
        chips = [(_flip(x, fx), _flip(y, fy)) for fx, fy in CHIP_FLIPS]

        def rows(px, py, pc):
            return out_ref.at[pl.ds((4 * px + 2 * py + pc) * m_per, m_per), :]

        def copy(k, block, to, src=None):
            return pltpu.make_async_remote_copy(
                src_ref=rows(*block) if src is None else src, dst_ref=rows(*block),
                send_sem=send_sems.at[k], recv_sem=recv_sems.at[k], device_id=to, device_id_type=MESH)

        mine = pltpu.make_async_copy(x_ref, rows(*me), local_sem)
        mine.start()
        first = [copy(0, me, sibling, src=x_ref)]
        first += [copy(1 + j, me, (*chip, c), src=x_ref) for j, chip in enumerate(chips)]
        for cp in first:
            cp.start()
        passed = [copy(4 + j, (*chip, c), sibling) for j, chip in enumerate(chips)]
        for j, chip in enumerate(chips):
            copy(1 + j, (*chip, c), me).wait_recv()
            passed[j].start()
        copy(0, sibling, me).wait_recv()
        for j, chip in enumerate(chips):
            copy(4 + j, (*chip, 1 - c), me).wait_recv()
        for cp in first + passed:
            cp.wait_send()
        mine.wait()

    return pl.pallas_call(
        body, name=name,
        out_shape=jax.ShapeDtypeStruct((NDEV * m_per, n), v.dtype),
        in_specs=[pl.BlockSpec(memory_space=pltpu.VMEM)],
        out_specs=pl.BlockSpec(memory_space=pltpu.VMEM),
        scratch_shapes=[pltpu.SemaphoreType.DMA((7,)), pltpu.SemaphoreType.DMA((7,)), pltpu.SemaphoreType.DMA],
    )(v)


def _gather_weights(shards):
    n = len(shards)

    def body(*refs):
        ins, outs = refs[:n], refs[n:2 * n]
        send_sems, recv_sems, local_sems = refs[2 * n:]
        x, y, c = _place()
        me = 2 * x + y
        sends = []
        for w in range(n):
            pltpu.make_async_copy(ins[w], outs[w].at[me], local_sems.at[w]).start()
            for d, (fx, fy) in enumerate(CHIP_FLIPS):
                cp = pltpu.make_async_remote_copy(
                    src_ref=ins[w], dst_ref=outs[w].at[me], send_sem=send_sems.at[3 * w + d],
                    recv_sem=recv_sems.at[3 * w + d], device_id=(_flip(x, fx), _flip(y, fy), c), device_id_type=MESH)
                cp.start()
                sends.append(cp)
        for w in range(n):
            for d, (fx, fy) in enumerate(CHIP_FLIPS):
                px, py = _flip(x, fx), _flip(y, fy)
                pltpu.make_async_remote_copy(
                    src_ref=ins[w], dst_ref=outs[w].at[2 * px + py], send_sem=send_sems.at[3 * w + d],
                    recv_sem=recv_sems.at[3 * w + d], device_id=(px, py, c), device_id_type=MESH).wait_recv()
        for cp in sends:
            cp.wait_send()
        for w in range(n):
            pltpu.make_async_copy(ins[w], outs[w].at[me], local_sems.at[w]).wait()

    return pl.pallas_call(
        body, name="gather_weights",
        out_shape=[jax.ShapeDtypeStruct((NCHIP,) + a.shape, a.dtype) for a in shards],
        in_specs=[ANY] * n, out_specs=[ANY] * n,
        scratch_shapes=[pltpu.SemaphoreType.DMA((3 * n,)), pltpu.SemaphoreType.DMA((3 * n,)),
                        pltpu.SemaphoreType.DMA((n,))],
    )(*shards)


def _send_halves(grads):
    n = len(grads)

    def body(*refs):
        ins, outs = refs[:n], refs[n:2 * n]
        send_sems, recv_sems = refs[2 * n:]
        x, y, c = _place()
        copies = []
        for w in range(n):
            r2 = ins[w].shape[1] // 2
            cp = pltpu.make_async_remote_copy(
                src_ref=ins[w].at[:, pl.ds((1 - c) * r2, r2), :], dst_ref=outs[w],
                send_sem=send_sems.at[w], recv_sem=recv_sems.at[w], device_id=(x, y, 1 - c), device_id_type=MESH)
            cp.start()
            copies.append(cp)
        for cp in copies:
            cp.wait_recv()
        for cp in copies:
            cp.wait_send()

    return pl.pallas_call(
        body, name="rs_send_halves",
        out_shape=[jax.ShapeDtypeStruct((NCHIP, g.shape[1] // 2, g.shape[2]), F32) for g in grads],
        in_specs=[ANY] * n, out_specs=[ANY] * n,
        scratch_shapes=[pltpu.SemaphoreType.DMA((n,)), pltpu.SemaphoreType.DMA((n,))],
    )(*grads)


def _row_tile(rows, cols):
    tr = rows
    while tr * cols * 4 > (1 << 20) and tr % 16 == 0:
        tr //= 2
    return tr


def _add_halves(g, t1, core, name):
    _, r, cdim = g.shape
    r2 = r // 2
    tr = _row_tile(r2, cdim)
    nt = r2 // tr

    def body(core_ref, g_ref, t_ref, p_ref, pb_ref):
        p = g_ref[...] + t_ref[...]
        p_ref[...] = p
        pb_ref[...] = p.astype(BF16)

    blk = pl.BlockSpec((None, tr, cdim), lambda j, i, core_ref: (j, i, 0))
    return pl.pallas_call(
        body, name=name,
        grid_spec=pltpu.PrefetchScalarGridSpec(
            num_scalar_prefetch=1, grid=(NCHIP, nt),
            in_specs=[pl.BlockSpec((None, tr, cdim), lambda j, i, core_ref: (j, core_ref[0] * nt + i, 0)), blk],
            out_specs=[blk, blk]),
        out_shape=[jax.ShapeDtypeStruct((NCHIP, r2, cdim), F32), jax.ShapeDtypeStruct((NCHIP, r2, cdim), BF16)],
        compiler_params=_params("parallel", "parallel"),
    )(core, g, t1)


def _exchange_partials(parts):
    n = len(parts)

    def body(*refs):
        ins, outs = refs[:n], refs[n:2 * n]
        send_sems, recv_sems = refs[2 * n:]
        x, y, c = _place()
        copies = []
        for w in range(n):
            for d, (fx, fy) in enumerate(CHIP_FLIPS):
                px, py = _flip(x, fx), _flip(y, fy)
                cp = pltpu.make_async_remote_copy(
                    src_ref=ins[w].at[2 * px + py], dst_ref=outs[w].at[d], send_sem=send_sems.at[3 * w + d],
                    recv_sem=recv_sems.at[3 * w + d], device_id=(px, py, c), device_id_type=MESH)
                cp.start()
                copies.append(cp)
        for cp in copies:
            cp.wait_recv()
        for cp in copies:
            cp.wait_send()

    return pl.pallas_call(
        body, name="rs_exchange_partials",
        out_shape=[jax.ShapeDtypeStruct((3,) + p.shape[1:], BF16) for p in parts],
        in_specs=[ANY] * n, out_specs=[ANY] * n,
        scratch_shapes=[pltpu.SemaphoreType.DMA((3 * n,)), pltpu.SemaphoreType.DMA((3 * n,))],
    )(*parts)


def _add_partials(p, t2, chip, name):
    _, r2, cdim = p.shape
    tr = _row_tile(r2, cdim)

    def body(chip_ref, p_ref, t_ref, o_ref):
        o_ref[...] = ((p_ref[...] + t_ref[0].astype(F32)) + t_ref[1].astype(F32)) + t_ref[2].astype(F32)

    return pl.pallas_call(
        body, name=name,
        grid_spec=pltpu.PrefetchScalarGridSpec(
            num_scalar_prefetch=1, grid=(r2 // tr,),
            in_specs=[pl.BlockSpec((None, tr, cdim), lambda i, chip_ref: (chip_ref[0], i, 0)),
                      pl.BlockSpec((3, tr, cdim), lambda i, chip_ref: (0, i, 0))],
            out_specs=pl.BlockSpec((tr, cdim), lambda i, chip_ref: (i, 0))),
        out_shape=jax.ShapeDtypeStruct((r2, cdim), F32),
        compiler_params=_params("parallel"),
    )(chip, p, t2)


def _join_halves(halves):
    n = len(halves)

    def body(*refs):
        ins, outs = refs[:n], refs[n:2 * n]
        send_sems, recv_sems, local_sems = refs[2 * n:]
        x, y, c = _place()
        copies = []
        for w in range(n):
            r2 = ins[w].shape[0]
            pltpu.make_async_copy(ins[w], outs[w].at[pl.ds(c * r2, r2), :], local_sems.at[w]).start()
            cp = pltpu.make_async_remote_copy(
                src_ref=ins[w], dst_ref=outs[w].at[pl.ds(c * r2, r2), :], send_sem=send_sems.at[w],
                recv_sem=recv_sems.at[w], device_id=(x, y, 1 - c), device_id_type=MESH)
            cp.start()
            copies.append(cp)
        for w in range(n):
            r2 = ins[w].shape[0]
            pltpu.make_async_remote_copy(
                src_ref=ins[w], dst_ref=outs[w].at[pl.ds((1 - c) * r2, r2), :], send_sem=send_sems.at[w],
                recv_sem=recv_sems.at[w], device_id=(x, y, 1 - c), device_id_type=MESH).wait_recv()
        for cp in copies:
            cp.wait_send()
        for w in range(n):
            r2 = ins[w].shape[0]
            pltpu.make_async_copy(ins[w], outs[w].at[pl.ds(c * r2, r2), :], local_sems.at[w]).wait()

    return pl.pallas_call(
        body, name="rs_join_halves",
        out_shape=[jax.ShapeDtypeStruct((2 * h.shape[0], h.shape[1]), F32) for h in halves],
        in_specs=[ANY] * n, out_specs=[ANY] * n,
        scratch_shapes=[pltpu.SemaphoreType.DMA((n,)), pltpu.SemaphoreType.DMA((n,)), pltpu.SemaphoreType.DMA((n,))],
    )(*halves)


def _ada_fwd(c_all, ada_w):
    nl, _, ns = ada_w.shape

    def body(c_ref, w_ref, o_ref):
        cv = c_ref[...]
        o_ref[...] = _dot((cv * jax.nn.sigmoid(cv)).astype(BF16), w_ref[...].astype(BF16))

    return pl.pallas_call(
        body, name="ada_fwd", grid=(nl,),
        in_specs=[pl.BlockSpec((NDEV, D), lambda i: (0, 0)), pl.BlockSpec((None, D, ns), lambda i: (i, 0, 0))],
        out_specs=pl.BlockSpec((None, NDEV, ns), lambda i: (i, 0, 0)),
        out_shape=jax.ShapeDtypeStruct((nl, NDEV, ns), F32),
        compiler_params=_params("parallel"),
    )(c_all, ada_w)


PACK_ROWS = 24


def _reduce_packed(gathered):
    def body(g_ref, tot_ref, loss_ref):
        tot = g_ref[0:PACK_ROWS, :]
        for dev in range(1, NDEV):
            tot = tot + g_ref[dev * PACK_ROWS:(dev + 1) * PACK_ROWS, :]
        tot_ref[...] = tot
        loss_ref[...] = jnp.zeros((8, 128), F32) + jnp.sum(tot[10:11, :])

    return pl.pallas_call(
        body, name="reduce_packed",
        out_shape=[jax.ShapeDtypeStruct((PACK_ROWS, D), F32), jax.ShapeDtypeStruct((8, 128), F32)],
    )(gathered)


def _ada_w_update(c_t, dms, w, m, v):
    nl, _, ns = w.shape
    tr = 256

    def body(ct_ref, dm_ref, w_ref, m_ref, v_ref, g_ref, d_ref, m2_ref, v2_ref):
        ct = ct_ref[...]
        sc = ct * jax.nn.sigmoid(ct)
        dm = dm_ref[...]
        g = sc[:, 0:1] * dm[0:1, :]
        for b in range(1, NDEV):
            g = g + sc[:, b:b + 1] * dm[b:b + 1, :]
        g_ref[...] = g
        d_ref[...], m2_ref[...], v2_ref[...] = _adamw_math(w_ref[...], g, m_ref[...], v_ref[...])

    blk = pl.BlockSpec((None, tr, ns), lambda i, j: (i, j, 0))
    return pl.pallas_call(
        body, name="ada_w_update", grid=(nl, D // tr),
        in_specs=[pl.BlockSpec((tr, NDEV), lambda i, j: (j, 0)),
                  pl.BlockSpec((None, NDEV, ns), lambda i, j: (i, 0, 0)), blk, blk, blk],
        out_specs=[blk] * 4,
        out_shape=[jax.ShapeDtypeStruct((nl, D, ns), F32)] * 4,
        compiler_params=_params("parallel", "parallel"),
    )(c_t, dms, w, m, v)


def _chip_major(a, parts):
    g, _, cdim = a.shape
    return jnp.transpose(a.reshape(g, NCHIP, parts, cdim), (1, 0, 2, 3)).reshape(NCHIP, g * parts, cdim)


def kernel(x, c, norm_g, ada_w, ada_b, even_w_in, pool_w, pool_scale, even_w_out, odd_w_in, conv_w, conv_b, odd_w_out, final_g, loss_target, m_norm_g, m_ada_w, m_ada_b, m_even_w_in, m_pool_w, m_pool_scale, m_even_w_out, m_odd_w_in, m_conv_w, m_conv_b, m_odd_w_out, m_final_g, v_norm_g, v_ada_w, v_ada_b, v_even_w_in, v_pool_w, v_pool_scale, v_even_w_out, v_odd_w_in, v_conv_w, v_conv_b, v_odd_w_out, v_final_g):
    ix, iy, ic = _place()
    chip = 2 * ix + iy
    batch = 2 * chip + ic
    chip_arr = jnp.reshape(chip, (1,)).astype(jnp.int32)
    core_arr = jnp.reshape(ic, (1,)).astype(jnp.int32)
    ns_ada = ada_w.shape[2]

    c_all = _allgather8(jnp.pad(c, ((0, 7), (0, 0))), "gather_c")[0::8]
    m_cols = _allgather8(_ada_fwd(c_all, ada_w).reshape(2 * NDEV, ns_ada), "gather_ada")
    m_cols = m_cols.reshape(NCHIP, 2, 2, NDEV, ns_ada)[:, 0]
    m_mine = lax.dynamic_index_in_dim(m_cols, batch, axis=2, keepdims=False)
    m_mine = jnp.transpose(m_mine, (1, 0, 2)).reshape(2, 3 * D) + ada_b
    zrow = jnp.zeros((3, D), F32)

    def vec_rows(i):
        sh, sc, gt = m_mine[i, 0:D], m_mine[i, D:2 * D], m_mine[i, 2 * D:3 * D]
        return jnp.concatenate([jnp.stack([norm_g[i], sc, sh, gt, final_g]), zrow], axis=0)

    win0, pw, wout0, win1, wout1 = _gather_weights([
        even_w_in[0].astype(BF16), pool_w[0].astype(BF16), even_w_out[0].astype(BF16),
        odd_w_in[0].astype(BF16), odd_w_out[0].astype(BF16)])
    pw_full = jnp.transpose(pw, (1, 0, 2, 3)).reshape(4, PG, PG)
    cw_rows = _allgather8(jnp.concatenate([conv_w[0], conv_b, jnp.zeros((4, conv_b.shape[1]), F32)], axis=0),
                          "gather_conv")
    cw_rows = cw_rows.reshape(NCHIP, 2, 8, -1)[:, 0]
    cw_full = jnp.transpose(cw_rows, (1, 0, 2)).reshape(8, DI)
    cw8 = jnp.concatenate([cw_full[0:3], jnp.zeros((5, DI), F32)], axis=0)
    cb_full = cw_full[3:4]

    dx0, grads, sums = _local_step(
        x[0], loss_target[0], vec_rows(0), vec_rows(1), win0, pw_full, pool_scale,
        wout0.reshape(DI, D), win1, cw8, cb_full, wout1.reshape(DI, D))
    g_win0, g_pw, g_wout0, g_win1, g_wout1 = grads

    big = [g_win0, _chip_major(g_pw, PG // NCHIP), g_wout0, g_win1, g_wout1]
    names = ["win0", "pool", "wout0", "win1", "wout1"]
    from_sibling = _send_halves(big)
    part = [_add_halves(g, t, core_arr, "rs_add_halves_" + nm) for g, t, nm in zip(big, from_sibling, names)]
    received = _exchange_partials([pb for _, pb in part])
    halves = [_add_partials(p, t, chip_arr, "rs_add_partials_" + nm) for (p, _), t, nm in zip(part, received, names)]
    r_win0, r_pw, r_wout0, r_win1, r_wout1 = _join_halves(halves)

    packed = jnp.concatenate([
        sums["dm0"], sums["dm1"], sums["norm_g"], sums["pool_scale"], sums["final_g"], sums["loss"],
        sums["conv_w"].reshape(6, D), sums["conv_b"].reshape(2, D), jnp.zeros((PACK_ROWS - 19, D), F32)], axis=0)
    gathered = _allgather8(packed, "gather_sums")
    tot, loss8 = _reduce_packed(gathered)
    loss = loss8[0, 0]
    g_norm_g, g_pool_scale, g_final_g = tot[6:8], tot[8:9], tot[9]
    g_ada_b = tot[0:6].reshape(2, 3 * D)
    g_conv_w = lax.dynamic_slice_in_dim(tot[11:17].reshape(3, DI), chip * (DI // NCHIP), DI // NCHIP, axis=1)
    g_conv_b = lax.dynamic_slice_in_dim(tot[17:19].reshape(1, DI), chip * (DI // NCHIP), DI // NCHIP, axis=1)
    dm_all = gathered.reshape(NDEV, PACK_ROWS, D)[:, 0:6].reshape(NDEV, 2, 3 * D)
    dm_cols = jnp.transpose(lax.dynamic_slice_in_dim(dm_all, chip * ns_ada, ns_ada, axis=2), (1, 0, 2))
    g_ada_w, d_ada_w, nm_ada_w, nv_ada_w = _ada_w_update(jnp.transpose(c_all), dm_cols, ada_w, m_ada_w, v_ada_w)

    def upd(w, g, m, v, name):
        shape = w.shape
        w2, m2, v2 = (a.reshape(g.shape) for a in (w, m, v))
        d, nm, nv = _adamw(w2, g, m2, v2, name)
        return g.reshape(shape), d.reshape(shape), nm.reshape(shape), nv.reshape(shape)

    o_win0 = upd(even_w_in, r_win0, m_even_w_in, v_even_w_in, "adamw_win0")
    o_pw = upd(pool_w, r_pw, m_pool_w, v_pool_w, "adamw_pool")
    o_wout0 = upd(even_w_out, r_wout0, m_even_w_out, v_even_w_out, "adamw_wout0")
    o_win1 = upd(odd_w_in, r_win1, m_odd_w_in, v_odd_w_in, "adamw_win1")
    o_wout1 = upd(odd_w_out, r_wout1, m_odd_w_out, v_odd_w_out, "adamw_wout1")

    def pack_small(ng, ab, ps, fg, cwv, cbv):
        conv = jnp.concatenate([cwv.reshape(3, -1), cbv.reshape(1, -1)], axis=0).reshape(2, D)
        return jnp.concatenate([ng, ab.reshape(6, D), ps, fg.reshape(1, D), conv, jnp.zeros((4, D), F32)], axis=0)

    sw = pack_small(norm_g, ada_b, pool_scale, final_g, conv_w, conv_b)
    sg = pack_small(g_norm_g, g_ada_b, g_pool_scale, g_final_g, g_conv_w, g_conv_b)
    sm = pack_small(m_norm_g, m_ada_b, m_pool_scale, m_final_g, m_conv_w, m_conv_b)
    sv = pack_small(v_norm_g, v_ada_b, v_pool_scale, v_final_g, v_conv_w, v_conv_b) + jnp.concatenate(
        [jnp.zeros((12, D), F32), jnp.ones((4, D), F32)], axis=0)
    small = _adamw(sw, sg, sm, sv, "adamw_small")

    def unpack_small(a):
        conv = a[10:12].reshape(4, -1)
        return dict(norm_g=a[0:2], ada_b=a[2:8].reshape(2, 3 * D), pool_scale=a[8:9], final_g=a[9],
                    conv_w=conv[0:3].reshape(conv_w.shape), conv_b=conv[3:4].reshape(conv_b.shape))

    s_grad = dict(norm_g=g_norm_g, ada_b=g_ada_b, pool_scale=g_pool_scale, final_g=g_final_g,
                  conv_w=g_conv_w.reshape(conv_w.shape), conv_b=g_conv_b.reshape(conv_b.shape))
    s_out = [s_grad] + [unpack_small(a) for a in small]

    outs = []
    for k in range(4):
        sm_k = s_out[k]
        outs.append([sm_k["norm_g"], (g_ada_w, d_ada_w, nm_ada_w, nv_ada_w)[k], sm_k["ada_b"], o_win0[k], o_pw[k],
                     sm_k["pool_scale"], o_wout0[k], o_win1[k], sm_k["conv_w"], sm_k["conv_b"], o_wout1[k],
                     sm_k["final_g"]])
    return (loss, dx0[None], *outs[0], *outs[1], *outs[2], *outs[3])
```

```python
import functools

import jax
import jax.numpy as jnp
from jax import lax
from jax.experimental import pallas as pl
from jax.experimental.pallas import tpu as pltpu

F32 = jnp.float32
BF16 = jnp.bfloat16
MESH = pl.DeviceIdType.MESH

D = 1024
DI = 2048
DP = 1024
NE = 6144
NO = 8192
WINDOWS = (2, 4, 8, 16)
PG = 256
HD = 64
NCHIP = 4
NDEV = 8
EPS = 1e-6
INV_SQRT_HD = 0.125

LR, B1, B2, EPS_ADAM, WD, STEP = 0.001, 0.9, 0.999, 1e-08, 0.01, 10

TM = 512
TME = 256
CT = 512
BQ = 512
BK = 256
VMEM_LIMIT = 56 * 1024 * 1024


def _dot(a, b):
    return jnp.dot(a, b, preferred_element_type=F32)


def _dot_nt(a, b):
    return lax.dot_general(a, b, (((1,), (1,)), ((), ())), preferred_element_type=F32)


def _dot_tn(a, b):
    return lax.dot_general(a, b, (((0,), (0,)), ((), ())), preferred_element_type=F32)


def _params(*sem):
    return pltpu.CompilerParams(dimension_semantics=sem, vmem_limit_bytes=VMEM_LIMIT)


def _rowsum(v):
    return jnp.sum(v, axis=0, keepdims=True)


def _norm_inproj(x, vecs, w, name, comm=None):
    s = x.shape[0]
    ns = w.shape[2]
    n = NCHIP * ns
    tps = ns // CT
    tm = min(TM, s)
    ni, nj = s // tm, n // CT
    comm = comm or _NO_COMM
    nci, nco = len(comm.arrays), len(comm.out_shape)

    def body(*refs):
        x_ref, vec_ref, w_ref = refs[:3]
        proj_ref, h_ref = refs[3 + nci:5 + nci]
        cargs = (refs[3:3 + nci], refs[5 + nci:5 + nci + nco], refs[5 + nci + nco:])
        i, j = pl.program_id(0), pl.program_id(1)
        if nci:
            pl.when((i == 0) & (j == 0))(lambda: comm.start(*cargs))

        @pl.when(j == 0)
        def _():
            xv = x_ref[...]
            r = lax.rsqrt(jnp.mean(xv * xv, axis=-1, keepdims=True) + EPS)
            h = ((xv * r) * vec_ref[0:1, :]) * (1.0 + vec_ref[1:2, :]) + vec_ref[2:3, :]
            h_ref[...] = h.astype(BF16)

        proj_ref[...] = _dot(h_ref[...], w_ref[...])
        if nci:
            pl.when((i == ni - 1) & (j == nj - 1))(lambda: comm.finish(*cargs))

    return pl.pallas_call(
        body, name=name, grid=(ni, nj),
        in_specs=[pl.BlockSpec((tm, D), lambda i, j: (i, 0)),
                  pl.BlockSpec((8, D), lambda i, j: (0, 0)),
                  pl.BlockSpec((None, D, CT), lambda i, j: (j // tps, 0, j % tps))] + [ANY] * nci,
        out_specs=[pl.BlockSpec((tm, CT), lambda i, j: (i, j)),
                   pl.BlockSpec((tm, D), lambda i, j: (i, 0))] + [ANY] * nco,
        out_shape=[jax.ShapeDtypeStruct((s, n), F32), jax.ShapeDtypeStruct((s, D), BF16)] + comm.out_shape,
        scratch_shapes=comm.sems,
        compiler_params=_params("arbitrary", "arbitrary"),
    )(x, vecs, w, *comm.arrays)


def _pool_fwd(proj0, pw, pscale):
    s = proj0.shape[0]
    tm = min(TM, s)
    hb = tm // 16

    def body(u_ref, halo_ref, w_ref, sc_ref, p_ref, y_ref, ext_ref):
        i = pl.program_id(0)
        ext_ref[16:, :] = u_ref[...]
        ext_ref[0:16, :] = jnp.where(i > 0, halo_ref[...], 0.0)
        t = i * tm + lax.broadcasted_iota(jnp.int32, (tm, 1), 0)
        for g, wdw in enumerate(WINDOWS):
            cs = slice(g * PG, (g + 1) * PG)
            u = ext_ref[16:16 + tm, cs]
            acc = u
            for j in range(1, wdw):
                acc = acc + ext_ref[16 - j:16 - j + tm, cs]
            inv = 1.0 / jnp.minimum(t + 1, wdw).astype(F32)
            pb = (acc * inv - u).astype(BF16)
            p_ref[:, cs] = pb
            y_ref[:, cs] = _dot(pb, w_ref[g]) * sc_ref[:, cs]

    return pl.pallas_call(
        body, name="pool_fwd", grid=(s // tm,),
        in_specs=[pl.BlockSpec((tm, DP), lambda i: (i, 0)),
                  pl.BlockSpec((16, DP), lambda i: (jnp.maximum(i * hb - 1, 0), 0)),
                  pl.BlockSpec((4, PG, PG), lambda i: (0, 0, 0)),
                  pl.BlockSpec((1, DP), lambda i: (0, 0))],
        out_specs=[pl.BlockSpec((tm, DP), lambda i: (i, 0)),
                   pl.BlockSpec((tm, DP), lambda i: (i, 0))],
        out_shape=[jax.ShapeDtypeStruct((s, DP), BF16), jax.ShapeDtypeStruct((s, DP), F32)],
        scratch_shapes=[pltpu.VMEM((tm + 16, DP), F32)],
        compiler_params=_params("parallel"),
    )(proj0, proj0, pw, pscale)


def _sb_logits(nz, mask):
    neg_abs = lax.bitcast_convert_type(lax.bitcast_convert_type(nz, jnp.uint32) | jnp.uint32(0x80000000), F32)
    t = jnp.log(1.0 + jnp.exp(neg_abs))
    lf = jnp.minimum(nz, 0.0) - t
    lam = lf - nz
    if mask is not None:
        lf = jnp.where(mask, lf, 0.0)
    return lf, lam


def _split_dot(v, tri):
    hi = v.astype(BF16)
    lo = (v - hi.astype(F32)).astype(BF16)
    return _dot(hi, tri) + _dot(lo, tri)


def _tri_masks():
    row = lax.broadcasted_iota(jnp.int32, (BK, BK), 0)
    col = lax.broadcasted_iota(jnp.int32, (BK, BK), 1)
    return (row > col).astype(BF16), (row >= col).astype(BF16)


def _causal_mask(offset):
    row = lax.broadcasted_iota(jnp.int32, (BQ, BK), 0)
    col = lax.broadcasted_iota(jnp.int32, (BQ, BK), 1)
    return col + offset < row


def _attn_fwd(proj0):
    s = proj0.shape[0]
    nq = s // BQ
    kpq = BQ // BK

    def body(q_ref, k_ref, v_ref, o_ref, qn_ref, k8_ref, vb_ref):
        qn_ref[...] = (-q_ref[...]).astype(BF16)
        k8_ref[...] = (k_ref[...] * INV_SQRT_HD).astype(BF16)
        vb_ref[...] = v_ref[...].astype(BF16)
        after, _ = _tri_masks()
        heads = [slice(HD * h, HD * (h + 1)) for h in range(2)]

        def qstep(qi, carry):
            q0 = pl.multiple_of(qi * BQ, BQ)
            qn = [qn_ref[pl.ds(q0, BQ), ls] for ls in heads]

            def blocks(k_hi, state, masks):
                ks = [pl.multiple_of(k_hi + (kpq - 1 - j) * BK, BK) for j in range(kpq)]
                lanes = [(j, h) for j in range(kpq) for h in range(len(heads))]
                mask = lambda j: None if masks is None else masks[j]
                nz = {jh: _dot_nt(qn[jh[1]], k8_ref[pl.ds(ks[jh[0]], BK), heads[jh[1]]]) for jh in lanes}
                ll = {jh: _sb_logits(nz[jh], mask(jh[0])) for jh in lanes}
                aft = {jh: _dot(ll[jh][0].astype(BF16), after) for jh in lanes}
                new = []
                for h in range(len(heads)):
                    o_acc, c = state[h]
                    for j in range(kpq):
                        a = jnp.exp(ll[j, h][1] + aft[j, h] + c)
                        if masks is not None:
                            a = jnp.where(masks[j], a, 0.0)
                        o_acc = o_acc + _dot(a.astype(BF16), vb_ref[pl.ds(ks[j], BK), heads[h]])
                        c = c + aft[j, h][:, 0:1] + ll[j, h][0][:, 0:1]
                    new.append((o_acc, c))
                return tuple(new)

            state = tuple((jnp.zeros((BQ, HD), F32), jnp.zeros((BQ, 1), F32)) for _ in heads)
            state = blocks(q0, state, [_causal_mask((kpq - 1 - j) * BK) for j in range(kpq)])
            state = lax.fori_loop(0, qi, lambda j, st: blocks(q0 - (j + 1) * BQ, st, None), state)
            for h, ls in enumerate(heads):
                o_ref[pl.ds(q0, BQ), ls] = state[h][0]
            return carry

        lax.fori_loop(0, nq, qstep, 0)

    return pl.pallas_call(
        body, name="attn_fwd", grid=(DP // 128,),
        in_specs=[pl.BlockSpec((s, 128), lambda h: (0, 8 + h)),
                  pl.BlockSpec((s, 128), lambda h: (0, 16 + h)),
                  pl.BlockSpec((s, 128), lambda h: (0, 24 + h))],
        out_specs=pl.BlockSpec((s, 128), lambda h: (0, h)),
        out_shape=jax.ShapeDtypeStruct((s, DP), F32),
        scratch_shapes=[pltpu.VMEM((s, 128), BF16)] * 3,
        compiler_params=_params("parallel"),
    )(proj0, proj0, proj0)


def _even_out(ypool, ysb, proj0, wout, x, vecs):
    s = x.shape[0]
    tm = min(TME, s)

    def body(yp_ref, ys_ref, gate_ref, w_ref, x_ref, vec_ref, x1_ref, out_ref, yg_ref):
        gt = gate_ref[...]
        sl = gt * jax.nn.sigmoid(gt)
        yg_ref[:, :DP] = (yp_ref[...] * sl[:, :DP]).astype(BF16)
        yg_ref[:, DP:] = (ys_ref[...] * sl[:, DP:]).astype(BF16)
        out = _dot(yg_ref[...], w_ref[...])
        out_ref[...] = out
        x1_ref[...] = x_ref[...] + (1.0 + vec_ref[3:4, :]) * out

    row = lambda i: (i, 0)
    return pl.pallas_call(
        body, name="even_out", grid=(s // tm,),
        in_specs=[pl.BlockSpec((tm, DP), row), pl.BlockSpec((tm, DP), row),
                  pl.BlockSpec((tm, DI), lambda i: (i, 2)),
                  pl.BlockSpec((DI, D), lambda i: (0, 0)),
                  pl.BlockSpec((tm, D), row), pl.BlockSpec((8, D), lambda i: (0, 0))],
        out_specs=[pl.BlockSpec((tm, D), row), pl.BlockSpec((tm, D), row), pl.BlockSpec((tm, DI), row)],
        out_shape=[jax.ShapeDtypeStruct((s, D), F32), jax.ShapeDtypeStruct((s, D), F32),
                   jax.ShapeDtypeStruct((s, DI), BF16)],
        compiler_params=_params("parallel"),
    )(ypool, ysb, proj0, wout, x, vecs)


def _odd_out(proj1, wout, x1, vecs, cw, cb, target):
    s = x1.shape[0]
    tm = min(TME, s)
    hb = tm // 8

    def body(gb_ref, gc_ref, u_ref, gt_ref, hgc_ref, hu_ref, w_ref, x1_ref, vec_ref, cw_ref, cb_ref, tg_ref,
             dx2_ref, out_ref, y1_ref, acc_ref, ext_ref):
        i = pl.program_id(0)

        @pl.when(i == 0)
        def _():
            acc_ref[...] = jnp.zeros_like(acc_ref)

        ext_ref[8:, :] = gc_ref[...] * u_ref[...]
        ext_ref[0:8, :] = jnp.where(i > 0, hgc_ref[...] * hu_ref[...], 0.0)
        for c in range(DI // CT):
            cs = slice(c * CT, (c + 1) * CT)
            conv = (cb_ref[0:1, cs] + cw_ref[0:1, cs] * ext_ref[6:6 + tm, cs]
                    + cw_ref[1:2, cs] * ext_ref[7:7 + tm, cs] + cw_ref[2:3, cs] * ext_ref[8:8 + tm, cs])
            gt = gt_ref[:, cs]
            y1_ref[:, cs] = (gb_ref[:, cs] * conv * (gt * jax.nn.sigmoid(gt))).astype(BF16)
        out = _dot(y1_ref[...], w_ref[...])
        out_ref[...] = out
        x2 = x1_ref[...] + (1.0 + vec_ref[3:4, :]) * out
        r = lax.rsqrt(jnp.mean(x2 * x2, axis=-1, keepdims=True) + EPS)
        nrm = x2 * r
        fg = vec_ref[4:5, :]
        err = nrm * fg - tg_ref[...]
        acc_ref[1:2, :] += _rowsum(err * err) * (0.5 / D)
        dyf = err * (1.0 / D)
        acc_ref[0:1, :] += _rowsum(dyf * nrm)
        dn = dyf * fg
        dx2_ref[...] = r * (dn - nrm * jnp.mean(dn * nrm, axis=-1, keepdims=True))

    row = lambda i: (i, 0)
    halo = lambda col: (lambda i: (jnp.maximum(i * hb - 1, 0), col))
    const = lambda i: (0, 0)
    return pl.pallas_call(
        body, name="odd_out", grid=(s // tm,),
        in_specs=[pl.BlockSpec((tm, DI), lambda i: (i, 0)), pl.BlockSpec((tm, DI), lambda i: (i, 1)),
                  pl.BlockSpec((tm, DI), lambda i: (i, 2)), pl.BlockSpec((tm, DI), lambda i: (i, 3)),
                  pl.BlockSpec((8, DI), halo(1)), pl.BlockSpec((8, DI), halo(2)),
                  pl.BlockSpec((DI, D), const), pl.BlockSpec((tm, D), row), pl.BlockSpec((8, D), const),
                  pl.BlockSpec((8, DI), const), pl.BlockSpec((1, DI), const), pl.BlockSpec((tm, D), row)],
        out_specs=[pl.BlockSpec((tm, D), row), pl.BlockSpec((tm, D), row), pl.BlockSpec((tm, DI), row),
                   pl.BlockSpec((8, D), const)],
        out_shape=[jax.ShapeDtypeStruct((s, D), F32), jax.ShapeDtypeStruct((s, D), F32),
                   jax.ShapeDtypeStruct((s, DI), BF16), jax.ShapeDtypeStruct((8, D), F32)],
        scratch_shapes=[pltpu.VMEM((tm + 8, DI), F32)],
        compiler_params=_params("arbitrary"),
    )(proj1, proj1, proj1, proj1, proj1, proj1, wout, x1, vecs, cw, cb, target)


def _odd_bwd(dx2, out1, proj1, wout, vecs, cw, cb):
    s = dx2.shape[0]
    tm = min(TME, s)
    nb = s // tm
    hb = tm // 8

    def body(dx2_ref, out1_ref, gb_ref, gc_ref, u_ref, gt_ref, hgc_ref, hu_ref, w_ref, vec_ref, cw_ref, cb_ref,
             dout_ref, dproj_ref, accv_ref, accd_ref, uext_ref, dext_ref, dy_ref):
        i = pl.program_id(0)
        blk = nb - 1 - i

        @pl.when(i == 0)
        def _():
            accv_ref[...] = jnp.zeros_like(accv_ref)
            accd_ref[...] = jnp.zeros_like(accd_ref)
            dext_ref[tm:tm + 8, :] = jnp.zeros((8, DI), F32)

        dx2v = dx2_ref[...]
        accd_ref[0:1, :] += _rowsum(dx2v * out1_ref[...])
        dout = (dx2v * (1.0 + vec_ref[3:4, :])).astype(BF16)
        dout_ref[...] = dout
        dy_ref[...] = _dot_nt(dout, w_ref[...])
        uext_ref[8:, :] = gc_ref[...] * u_ref[...]
        uext_ref[0:8, :] = jnp.where(blk > 0, hgc_ref[...] * hu_ref[...], 0.0)
        for c in range(DI // CT):
            cs = slice(c * CT, (c + 1) * CT)
            u0 = uext_ref[6:6 + tm, cs]
            u1 = uext_ref[7:7 + tm, cs]
            u2 = uext_ref[8:8 + tm, cs]
            w0, w1, w2 = cw_ref[0:1, cs], cw_ref[1:2, cs], cw_ref[2:3, cs]
            conv = cb_ref[0:1, cs] + w0 * u0 + w1 * u1 + w2 * u2
            gt = gt_ref[:, cs]
            sg = jax.nn.sigmoid(gt)
            gb = gb_ref[:, cs]
            dy = dy_ref[:, cs]
            t1 = dy * (gt * sg)
            dproj_ref[:, cs] = (t1 * conv).astype(BF16)
            dconv = t1 * gb
            dproj_ref[:, 3 * DI + c * CT:3 * DI + (c + 1) * CT] = (
                dy * gb * conv * (sg * (1.0 + gt * (1.0 - sg)))).astype(BF16)
            accv_ref[0:1, cs] += _rowsum(dconv * u0)
            accv_ref[1:2, cs] += _rowsum(dconv * u1)
            accv_ref[2:3, cs] += _rowsum(dconv * u2)
            accv_ref[3:4, cs] += _rowsum(dconv)
            dext_ref[0:tm, cs] = dconv
            duu = w2 * dconv + w1 * dext_ref[1:tm + 1, cs] + w0 * dext_ref[2:tm + 2, cs]
            dproj_ref[:, DI + c * CT:DI + (c + 1) * CT] = (duu * u_ref[:, cs]).astype(BF16)
            dproj_ref[:, 2 * DI + c * CT:2 * DI + (c + 1) * CT] = (duu * gc_ref[:, cs]).astype(BF16)
        dext_ref[tm:tm + 8, :] = dext_ref[0:8, :]

    rrow = lambda i: (nb - 1 - i, 0)
    rcol = lambda col: (lambda i: (nb - 1 - i, col))
    halo = lambda col: (lambda i: (jnp.maximum((nb - 1 - i) * hb - 1, 0), col))
    const = lambda i: (0, 0)
    return pl.pallas_call(
        body, name="odd_bwd", grid=(nb,),
        in_specs=[pl.BlockSpec((tm, D), rrow), pl.BlockSpec((tm, D), rrow),
                  pl.BlockSpec((tm, DI), rcol(0)), pl.BlockSpec((tm, DI), rcol(1)),
                  pl.BlockSpec((tm, DI), rcol(2)), pl.BlockSpec((tm, DI), rcol(3)),
                  pl.BlockSpec((8, DI), halo(1)), pl.BlockSpec((8, DI), halo(2)),
                  pl.BlockSpec((DI, D), const), pl.BlockSpec((8, D), const),
                  pl.BlockSpec((8, DI), const), pl.BlockSpec((1, DI), const)],
        out_specs=[pl.BlockSpec((tm, D), rrow), pl.BlockSpec((tm, NO), rrow),
                   pl.BlockSpec((8, DI), const), pl.BlockSpec((8, D), const)],
        out_shape=[jax.ShapeDtypeStruct((s, D), BF16), jax.ShapeDtypeStruct((s, NO), BF16),
                   jax.ShapeDtypeStruct((8, DI), F32), jax.ShapeDtypeStruct((8, D), F32)],
        scratch_shapes=[pltpu.VMEM((tm + 8, DI), F32), pltpu.VMEM((tm + 8, DI), F32), pltpu.VMEM((tm, DI), F32)],
        compiler_params=_params("arbitrary"),
    )(dx2, out1, proj1, proj1, proj1, proj1, proj1, proj1, wout, vecs, cw, cb)


def _grad_w_cols(a, b, name):
    s, m = a.shape
    ns = b.shape[1] // NCHIP
    ts = min(TM, s)

    def body(a_ref, b_ref, o_ref):
        @pl.when(pl.program_id(2) == 0)
        def _():
            o_ref[...] = jnp.zeros_like(o_ref)

        o_ref[...] += _dot_tn(a_ref[...], b_ref[...])

    return pl.pallas_call(
        body, name=name, grid=(m // CT, NCHIP, s // ts),
        in_specs=[pl.BlockSpec((ts, CT), lambda i, j, k: (k, i)),
                  pl.BlockSpec((ts, ns), lambda i, j, k: (k, j))],
        out_specs=pl.BlockSpec((None, CT, ns), lambda i, j, k: (j, i, 0)),
        out_shape=jax.ShapeDtypeStruct((NCHIP, m, ns), F32),
        compiler_params=_params("parallel", "parallel", "arbitrary"),
    )(a, b)


def _grad_w_rows(a, b, name):
    s = a.shape[0]
    ms = a.shape[1] // NCHIP
    n = b.shape[1]
    ts = min(TM, s)

    def body(a_ref, b_ref, o_ref):
        @pl.when(pl.program_id(1) == 0)
        def _():
            o_ref[...] = jnp.zeros_like(o_ref)

        o_ref[...] += _dot_tn(a_ref[...], b_ref[...])

    return pl.pallas_call(
        body, name=name, grid=(NCHIP, s // ts),
        in_specs=[pl.BlockSpec((ts, ms), lambda i, k: (k, i)),
                  pl.BlockSpec((ts, n), lambda i, k: (k, 0))],
        out_specs=pl.BlockSpec((None, ms, n), lambda i, k: (i, 0, 0)),
        out_shape=jax.ShapeDtypeStruct((NCHIP, ms, n), F32),
        compiler_params=_params("parallel", "arbitrary"),
    )(a, b)


def _inproj_bwd(dproj, w, x, dx_in, vecs, name, comm=None):
    s = x.shape[0]
    ns = w.shape[2]
    n = NCHIP * ns
    tps = ns // CT
    nk = n // CT
    tm = min(TM, s)
    ni = s // tm
    comm = comm or _NO_COMM
    nci, nco = len(comm.arrays), len(comm.out_shape)

    def body(*refs):
        dp_ref, w_ref, x_ref, dxin_ref, vec_ref = refs[:5]
        dx_ref, acc_ref = refs[5 + nci:7 + nci]
        dh_ref = refs[7 + nci + nco]
        cargs = (refs[5:5 + nci], refs[7 + nci:7 + nci + nco], refs[8 + nci + nco:])
        i = pl.program_id(0)
        k = pl.program_id(1)

        @pl.when((i == 0) & (k == 0))
        def _():
            acc_ref[...] = jnp.zeros_like(acc_ref)
            if nci:
                comm.start(*cargs)

        @pl.when(k == 0)
        def _():
            dh_ref[...] = jnp.zeros_like(dh_ref)

        dh_ref[...] += _dot_nt(dp_ref[...], w_ref[...])

        @pl.when(k == nk - 1)
        def _():
            dh = dh_ref[...]
            xv = x_ref[...]
            r = lax.rsqrt(jnp.mean(xv * xv, axis=-1, keepdims=True) + EPS)
            nrm = xv * r
            g = vec_ref[0:1, :]
            sc1 = 1.0 + vec_ref[1:2, :]
            dhn = dh * nrm
            acc_ref[0:1, :] += _rowsum(dh)
            acc_ref[1:2, :] += _rowsum(dhn) * g
            acc_ref[2:3, :] += _rowsum(dhn) * sc1
            dn = dh * (g * sc1)
            dx_ref[...] = dxin_ref[...] + r * (dn - nrm * jnp.mean(dn * nrm, axis=-1, keepdims=True))

        if nci:
            pl.when((i == ni - 1) & (k == nk - 1))(lambda: comm.finish(*cargs))

    row = lambda i, k: (i, 0)
    const = lambda i, k: (0, 0)
    return pl.pallas_call(
        body, name=name, grid=(ni, nk),
        in_specs=[pl.BlockSpec((tm, CT), lambda i, k: (i, k)),
                  pl.BlockSpec((None, D, CT), lambda i, k: (k // tps, 0, k % tps)),
                  pl.BlockSpec((tm, D), row), pl.BlockSpec((tm, D), row), pl.BlockSpec((8, D), const)] + [ANY] * nci,
        out_specs=[pl.BlockSpec((tm, D), row), pl.BlockSpec((8, D), const)] + [ANY] * nco,
        out_shape=[jax.ShapeDtypeStruct((s, D), F32), jax.ShapeDtypeStruct((8, D), F32)] + comm.out_shape,
        scratch_shapes=[pltpu.VMEM((tm, D), F32)] + comm.sems,
        compiler_params=_params("arbitrary", "arbitrary"),
    )(dproj, w, x, dx_in, vecs, *comm.arrays)


def _even_bwd(dx1, out0, ypool, ysb, proj0, wout, vecs):
    s = dx1.shape[0]
    tm = min(TME, s)

    def body(dx1_ref, out0_ref, yp_ref, ys_ref, gate_ref, w_ref, vec_ref,
             dout_ref, dyp_ref, dys_ref, dgt_ref, acc_ref):
        @pl.when(pl.program_id(0) == 0)
        def _():
            acc_ref[...] = jnp.zeros_like(acc_ref)

        dx1v = dx1_ref[...]
        acc_ref[0:1, :] += _rowsum(dx1v * out0_ref[...])
        dout = (dx1v * (1.0 + vec_ref[3:4, :])).astype(BF16)
        dout_ref[...] = dout
        dyg = _dot_nt(dout, w_ref[...])
        gt = gate_ref[...]
        sg = jax.nn.sigmoid(gt)
        sl = gt * sg
        dsl = sg * (1.0 + gt * (1.0 - sg))
        dyp_ref[...] = dyg[:, :DP] * sl[:, :DP]
        dys_ref[...] = dyg[:, DP:] * sl[:, DP:]
        dgt_ref[:, :DP] = (dyg[:, :DP] * yp_ref[...] * dsl[:, :DP]).astype(BF16)
        dgt_ref[:, DP:] = (dyg[:, DP:] * ys_ref[...] * dsl[:, DP:]).astype(BF16)

    row = lambda i: (i, 0)
    const = lambda i: (0, 0)
    return pl.pallas_call(
        body, name="even_bwd", grid=(s // tm,),
        in_specs=[pl.BlockSpec((tm, D), row), pl.BlockSpec((tm, D), row),
                  pl.BlockSpec((tm, DP), row), pl.BlockSpec((tm, DP), row),
                  pl.BlockSpec((tm, DI), lambda i: (i, 2)),
                  pl.BlockSpec((DI, D), const), pl.BlockSpec((8, D), const)],
        out_specs=[pl.BlockSpec((tm, D), row), pl.BlockSpec((tm, DP), row), pl.BlockSpec((tm, DP), row),
                   pl.BlockSpec((tm, DI), row), pl.BlockSpec((8, D), const)],
        out_shape=[jax.ShapeDtypeStruct((s, D), BF16), jax.ShapeDtypeStruct((s, DP), F32),
                   jax.ShapeDtypeStruct((s, DP), F32), jax.ShapeDtypeStruct((s, DI), BF16),
                   jax.ShapeDtypeStruct((8, D), F32)],
        compiler_params=_params("arbitrary"),
    )(dx1, out0, ypool, ysb, proj0, wout, vecs)


def _pool_bwd(dyp, p, pw, pscale):
    s = dyp.shape[0]
    tm = min(TM, s)
    nb = s // tm
    hb = tm // 16

    def body(dy_ref, dyh_ref, p_ref, w_ref, sc_ref, du_ref, dw_ref, acc_ref, ext_ref):
        i = pl.program_id(0)

        @pl.when(i == 0)
        def _():
            dw_ref[...] = jnp.zeros_like(dw_ref)
            acc_ref[...] = jnp.zeros_like(acc_ref)

        t = i * tm + lax.broadcasted_iota(jnp.int32, (tm + 16, 1), 0)
        for g, wdw in enumerate(WINDOWS):
            cs = slice(g * PG, (g + 1) * PG)
            sc = sc_ref[:, cs]
            dy = dy_ref[:, cs]
            dyh = jnp.where(i < nb - 1, dyh_ref[:, cs], 0.0)
            pb = p_ref[:, cs]
            wg = w_ref[g]
            acc_ref[0:1, cs] += _rowsum(dy * _dot(pb, wg))
            dypre = (dy * sc).astype(BF16)
            dw_ref[g] += _dot_tn(pb, dypre)
            dp = _dot_nt(dypre, wg)
            dph = _dot_nt((dyh * sc).astype(BF16), wg)
            inv = 1.0 / jnp.minimum(t + 1, wdw).astype(F32)
            ext_ref[0:tm, cs] = dp * inv[0:tm]
            ext_ref[tm:tm + 16, cs] = dph * inv[tm:tm + 16]
            acc = ext_ref[0:tm, cs]
            for j in range(1, wdw):
                acc = acc + ext_ref[j:j + tm, cs]
            du_ref[:, cs] = (acc - dp).astype(BF16)

    row = lambda i: (i, 0)
    return pl.pallas_call(
        body, name="pool_bwd", grid=(nb,),
        in_specs=[pl.BlockSpec((tm, DP), row),
                  pl.BlockSpec((16, DP), lambda i: (jnp.minimum((i + 1) * hb, s // 16 - 1), 0)),
                  pl.BlockSpec((tm, DP), row),
                  pl.BlockSpec((4, PG, PG), lambda i: (0, 0, 0)),
                  pl.BlockSpec((1, DP), lambda i: (0, 0))],
        out_specs=[pl.BlockSpec((tm, DP), row), pl.BlockSpec((4, PG, PG), lambda i: (0, 0, 0)),
                   pl.BlockSpec((8, DP), lambda i: (0, 0))],
        out_shape=[jax.ShapeDtypeStruct((s, DP), BF16), jax.ShapeDtypeStruct((4, PG, PG), F32),
                   jax.ShapeDtypeStruct((8, DP), F32)],
        scratch_shapes=[pltpu.VMEM((tm + 16, DP), F32)],
        compiler_params=_params("arbitrary"),
    )(dyp, dyp, p, pw, pscale)


def _attn_bwd(proj0, ysb, dys):
    s = proj0.shape[0]
    nq = s // BQ
    kpq = BQ // BK

    def body(q_ref, k_ref, v_ref, o_ref, do_ref, dq_ref, dk_ref, dv_ref,
             qn_ref, k8_ref, vb_ref, dob_ref, dka_ref, dva_ref):
        qn_ref[...] = (-q_ref[...]).astype(BF16)
        k8_ref[...] = (k_ref[...] * INV_SQRT_HD).astype(BF16)
        vb_ref[...] = v_ref[...].astype(BF16)
        dob_ref[...] = do_ref[...].astype(BF16)
        dka_ref[...] = jnp.zeros_like(dka_ref)
        dva_ref[...] = jnp.zeros_like(dva_ref)
        after, from_on = _tri_masks()
        heads = [slice(HD * h, HD * (h + 1)) for h in range(2)]

        def qstep(qi, carry):
            q0 = pl.multiple_of(qi * BQ, BQ)
            qn = [qn_ref[pl.ds(q0, BQ), ls] for ls in heads]
            do = [dob_ref[pl.ds(q0, BQ), ls] for ls in heads]
            total = [jnp.sum(do[h].astype(F32) * o_ref[pl.ds(q0, BQ), ls], axis=1, keepdims=True)
                     for h, ls in enumerate(heads)]

            def blocks(k_hi, state, masks):
                ks = [pl.multiple_of(k_hi + (kpq - 1 - j) * BK, BK) for j in range(kpq)]
                lanes = [(j, h) for j in range(kpq) for h in range(len(heads))]
                mask = lambda j: None if masks is None else masks[j]
                k8 = {jh: k8_ref[pl.ds(ks[jh[0]], BK), heads[jh[1]]] for jh in lanes}
                nz = {jh: _dot_nt(qn[jh[1]], k8[jh]) for jh in lanes}
                da = {jh: _dot_nt(do[jh[1]], vb_ref[pl.ds(ks[jh[0]], BK), heads[jh[1]]]) for jh in lanes}
                ll = {jh: _sb_logits(nz[jh], mask(jh[0])) for jh in lanes}
                aft = {jh: _dot(ll[jh][0].astype(BF16), after) for jh in lanes}
                new = []
                for h in range(len(heads)):
                    dq_acc, c, cg = state[h]
                    for j in range(kpq):
                        a = jnp.exp(ll[j, h][1] + aft[j, h] + c)
                        if masks is not None:
                            a = jnp.where(masks[j], a, 0.0)
                        ab = a.astype(BF16)
                        g = da[j, h] * ab.astype(F32)
                        suf = _split_dot(g, from_on)
                        dz = g - jnp.exp(ll[j, h][1]) * (g + ((total[h] - cg) - suf))
                        if masks is not None:
                            dz = jnp.where(masks[j], dz, 0.0)
                        dzb = dz.astype(BF16)
                        dka_ref[pl.ds(ks[j], BK), heads[h]] += _dot_tn(dzb, qn[h])
                        dva_ref[pl.ds(ks[j], BK), heads[h]] += _dot_tn(ab, do[h])
                        dq_acc = dq_acc + _dot(dzb, k8[j, h])
                        c = c + aft[j, h][:, 0:1] + ll[j, h][0][:, 0:1]
                        cg = cg + suf[:, 0:1]
                    new.append((dq_acc, c, cg))
                return tuple(new)

            zero = jnp.zeros((BQ, 1), F32)
            state = tuple((jnp.zeros((BQ, HD), F32), zero, zero) for _ in heads)
            state = blocks(q0, state, [_causal_mask((kpq - 1 - j) * BK) for j in range(kpq)])
            state = lax.fori_loop(0, qi, lambda j, st: blocks(q0 - (j + 1) * BQ, st, None), state)
            for h, ls in enumerate(heads):
                dq_ref[pl.ds(q0, BQ), ls] = state[h][0].astype(BF16)
            return carry

        lax.fori_loop(0, nq, qstep, 0)
        dk_ref[...] = (dka_ref[...] * (-INV_SQRT_HD)).astype(BF16)
        dv_ref[...] = dva_ref[...].astype(BF16)

    col = lambda h: (0, h)
    return pl.pallas_call(
        body, name="attn_bwd", grid=(DP // 128,),
        in_specs=[pl.BlockSpec((s, 128), lambda h: (0, 8 + h)),
                  pl.BlockSpec((s, 128), lambda h: (0, 16 + h)),
                  pl.BlockSpec((s, 128), lambda h: (0, 24 + h)),
                  pl.BlockSpec((s, 128), col), pl.BlockSpec((s, 128), col)],
        out_specs=[pl.BlockSpec((s, 128), col)] * 3,
        out_shape=[jax.ShapeDtypeStruct((s, DP), BF16)] * 3,
        scratch_shapes=[pltpu.VMEM((s, 128), BF16)] * 4 + [pltpu.VMEM((s, 128), F32)] * 2,
        compiler_params=_params("parallel"),
    )(proj0, proj0, proj0, ysb, dys)


def _adamw_math(w, g, m, v):
    m2 = B1 * m + (1.0 - B1) * g
    v2 = B2 * v + (1.0 - B2) * (g * g)
    m_hat = m2 / (1.0 - B1 ** STEP)
    v_hat = v2 / (1.0 - B2 ** STEP)
    return -LR * (m_hat / (jnp.sqrt(v_hat) + EPS_ADAM) + WD * w), m2, v2


def _adamw(w, g, m, v, name):
    r, c = w.shape
    tr = r
    while tr * c * 4 > (1 << 20) and tr % 16 == 0:
        tr //= 2

    def body(w_ref, g_ref, m_ref, v_ref, d_ref, m2_ref, v2_ref):
        d_ref[...], m2_ref[...], v2_ref[...] = _adamw_math(w_ref[...], g_ref[...], m_ref[...], v_ref[...])

    spec = pl.BlockSpec((tr, c), lambda i: (i, 0))
    return pl.pallas_call(
        body, name=name, grid=(r // tr,),
        in_specs=[spec] * 4, out_specs=[spec] * 3,
        out_shape=[jax.ShapeDtypeStruct((r, c), F32)] * 3,
        compiler_params=_params("parallel"),
    )(w, g, m, v)


def _local_step(x, target, vecs0, vecs1, win0, pw, pscale, wout0, layer1, cw8, cb, sel=None):
    if sel is None:
        (win1, wout1), fetch = layer1, None
    else:
        fetch = _gather_comm(list(layer1))
    proj0, h0, *fetched = _norm_inproj(x, vecs0, win0, "inproj0", fetch)
    if sel is not None:
        win1, wout1 = fetched[0], fetched[1].reshape(DI, D)
    p, ypool = _pool_fwd(proj0, pw, pscale)
    ysb = _attn_fwd(proj0)
    x1, out0, yg = _even_out(ypool, ysb, proj0, wout0, x, vecs0)
    proj1, h1 = _norm_inproj(x1, vecs1, win1, "inproj1")
    dx2, out1, y1, acc_f = _odd_out(proj1, wout1, x1, vecs1, cw8, cb, target)

    def chip_partials(grads, names):
        from_sibling = _send_halves(grads, "rs_send_halves_" + names[0])
        part = [_add_halves(g, t, sel[1:2], "rs_add_halves_" + nm) for g, t, nm in zip(grads, from_sibling, names)]
        return [p32 for p32, _ in part], _exchange_comm([p16 for _, p16 in part])

    dout1, dproj1, acc_cv, acc_g1 = _odd_bwd(dx2, out1, proj1, wout1, vecs1, cw8, cb)
    g_wout1 = _grad_w_rows(y1, dout1, "grad_wout1")
    g_win1 = _grad_w_cols(h1, dproj1, "grad_win1")
    part1, swap1 = chip_partials([g_win1, g_wout1], ["win1", "wout1"]) if sel is not None else (None, None)
    dx1, acc_n1, *got1 = _inproj_bwd(dproj1, win1, x1, dx2, vecs1, "inproj1_bwd", swap1)

    dout0, dyp, dys, dgate0, acc_g0 = _even_bwd(dx1, out0, ypool, ysb, proj0, wout0, vecs0)
    g_wout0 = _grad_w_rows(yg, dout0, "grad_wout0")
    du, g_pw, acc_ps = _pool_bwd(dyp, p, pw, pscale)
    dq, dk, dv = _attn_bwd(proj0, ysb, dys)
    dproj0 = jnp.concatenate([du, dq, dk, dv, dgate0], axis=1)
    g_win0 = _grad_w_cols(h0, dproj0, "grad_win0")
    layer0 = [g_win0, _chip_major(g_pw, PG // NCHIP), g_wout0]
    part0, swap0 = chip_partials(layer0, ["win0", "pool", "wout0"]) if sel is not None else (None, None)
    dx0, acc_n0, *got0 = _inproj_bwd(dproj0, win0, x, dx1, vecs0, "inproj0_bwd", swap0)

    if sel is None:
        grads = (g_win0, g_pw, g_wout0, g_win1, g_wout1)
    else:
        names = ["win0", "pool", "wout0", "win1", "wout1"]
        halves = [_add_partials(p32, t, sel, "rs_add_partials_" + nm)
                  for p32, t, nm in zip(part0 + part1, got0 + got1, names)]
        grads = tuple(_join_halves(halves))

    sums = dict(
        dm0=jnp.concatenate([acc_n0[0:2], acc_g0[0:1]], axis=0),
        dm1=jnp.concatenate([acc_n1[0:2], acc_g1[0:1]], axis=0),
        norm_g=jnp.concatenate([acc_n0[2:3], acc_n1[2:3]], axis=0),
        pool_scale=acc_ps[0:1], final_g=acc_f[0:1], loss=acc_f[1:2],
        conv_w=acc_cv[0:3], conv_b=acc_cv[3:4])
    return dx0, grads, sums


ANY = pl.BlockSpec(memory_space=pl.ANY)
CHIP_FLIPS = ((1, 0), (0, 1), (1, 1))


def _place():
    return lax.axis_index("x"), lax.axis_index("y"), lax.axis_index("c")


def _flip(v, f):
    return 1 - v if f else v


def _allgather8(v, name):
    m_per, n = v.shape

    def body(x_ref, out_ref, send_sems, recv_sems, local_sem):
        x, y, c = _place()
        me, sibling = (x, y, c), (x, y, 1 - c)
        chips = [(_flip(x, fx), _flip(y, fy)) for fx, fy in CHIP_FLIPS]

        def rows(px, py, pc):
            return out_ref.at[pl.ds((4 * px + 2 * py + pc) * m_per, m_per), :]

        def copy(k, block, to, src=None):
            return pltpu.make_async_remote_copy(
                src_ref=rows(*block) if src is None else src, dst_ref=rows(*block),
                send_sem=send_sems.at[k], recv_sem=recv_sems.at[k], device_id=to, device_id_type=MESH)

        mine = pltpu.make_async_copy(x_ref, rows(*me), local_sem)
        mine.start()
        first = [copy(0, me, sibling, src=x_ref)]
        first += [copy(1 + j, me, (*chip, c), src=x_ref) for j, chip in enumerate(chips)]
        for cp in first:
            cp.start()
        passed = [copy(4 + j, (*chip, c), sibling) for j, chip in enumerate(chips)]
        for j, chip in enumerate(chips):
            copy(1 + j, (*chip, c), me).wait_recv()
            passed[j].start()
        copy(0, sibling, me).wait_recv()
        for j, chip in enumerate(chips):
            copy(4 + j, (*chip, 1 - c), me).wait_recv()
        for cp in first + passed:
            cp.wait_send()
        mine.wait()

    return pl.pallas_call(
        body, name=name,
        out_shape=jax.ShapeDtypeStruct((NDEV * m_per, n), v.dtype),
        in_specs=[pl.BlockSpec(memory_space=pltpu.VMEM)],
        out_specs=pl.BlockSpec(memory_space=pltpu.VMEM),
        scratch_shapes=[pltpu.SemaphoreType.DMA((7,)), pltpu.SemaphoreType.DMA((7,)), pltpu.SemaphoreType.DMA],
    )(v)


class _Comm:
    def __init__(self, arrays, out_shape, sems, start, finish):
        self.arrays, self.out_shape, self.sems, self.start, self.finish = arrays, out_shape, sems, start, finish


_NO_COMM = _Comm([], [], [], None, None)


def _run_comm(comm, name):
    n = len(comm.arrays)

    def body(*refs):
        args = (refs[:n], refs[n:n + len(comm.out_shape)], refs[n + len(comm.out_shape):])
        comm.start(*args)
        comm.finish(*args)

    return pl.pallas_call(
        body, name=name, out_shape=comm.out_shape,
        in_specs=[ANY] * n, out_specs=[ANY] * len(comm.out_shape), scratch_shapes=comm.sems,
    )(*comm.arrays)


def _gather_comm(shards):
    n = len(shards)

    def pieces(ins, outs, sems, kinds):
        x, y, c = _place()
        ici_send, ici_recv, fwd_send, fwd_recv, local = sems
        own, sibling = 2 * x + y, (x, y, 1 - c)
        made = {kind: [] for kind in kinds}
        for w in range(n):
            r2 = ins[w].shape[0] // 2
            mine, other = pl.ds(c * r2, r2), pl.ds((1 - c) * r2, r2)
            if "local" in kinds:
                made["local"].append(pltpu.make_async_copy(ins[w], outs[w].at[own], local.at[w]))
            for d, (fx, fy) in enumerate(CHIP_FLIPS):
                px, py, k = _flip(x, fx), _flip(y, fy), 3 * w + d
                peer = 2 * px + py
                ici = dict(send_sem=ici_send.at[k], recv_sem=ici_recv.at[k], device_id=(px, py, c), device_id_type=MESH)
                fwd = dict(send_sem=fwd_send.at[k], recv_sem=fwd_recv.at[k], device_id=sibling, device_id_type=MESH)
                if "ici_out" in kinds:
                    made["ici_out"].append(pltpu.make_async_remote_copy(
                        src_ref=ins[w].at[mine, :], dst_ref=outs[w].at[own, mine, :], **ici))
                if "ici_in" in kinds:
                    made["ici_in"].append(pltpu.make_async_remote_copy(
                        src_ref=ins[w].at[mine, :], dst_ref=outs[w].at[peer, mine, :], **ici))
                if "fwd_out" in kinds:
                    made["fwd_out"].append(pltpu.make_async_remote_copy(
                        src_ref=outs[w].at[peer, mine, :], dst_ref=outs[w].at[peer, mine, :], **fwd))
                if "fwd_in" in kinds:
                    made["fwd_in"].append(pltpu.make_async_remote_copy(
                        src_ref=outs[w].at[peer, other, :], dst_ref=outs[w].at[peer, other, :], **fwd))
        return made

    def start(ins, outs, sems):
        made = pieces(ins, outs, sems, ("local", "ici_out"))
        for cp in made["local"] + made["ici_out"]:
            cp.start()

    def finish(ins, outs, sems):
        made = pieces(ins, outs, sems, ("local", "ici_out", "ici_in", "fwd_out", "fwd_in"))
        for arrived, onward in zip(made["ici_in"], made["fwd_out"]):
            arrived.wait_recv()
            onward.start()
        for cp in made["fwd_in"]:
            cp.wait_recv()
        for cp in made["ici_out"] + made["fwd_out"]:
            cp.wait_send()
        for cp in made["local"]:
            cp.wait()

    dma = pltpu.SemaphoreType.DMA
    return _Comm(list(shards), [jax.ShapeDtypeStruct((NCHIP,) + a.shape, a.dtype) for a in shards],
                 [dma((3 * n,))] * 4 + [dma((n,))], start, finish)


def _exchange_comm(parts):
    n = len(parts)

    def copies(ins, outs, sems):
        x, y, c = _place()
        send_sems, recv_sems = sems
        out = []
        for w in range(n):
            for d, (fx, fy) in enumerate(CHIP_FLIPS):
                px, py = _flip(x, fx), _flip(y, fy)
                out.append(pltpu.make_async_remote_copy(
                    src_ref=ins[w].at[2 * px + py], dst_ref=outs[w].at[d], send_sem=send_sems.at[3 * w + d],
                    recv_sem=recv_sems.at[3 * w + d], device_id=(px, py, c), device_id_type=MESH))
        return out

    def start(ins, outs, sems):
        for cp in copies(ins, outs, sems):
            cp.start()

    def finish(ins, outs, sems):
        cps = copies(ins, outs, sems)
        for cp in cps:
            cp.wait_recv()
        for cp in cps:
            cp.wait_send()

    dma = pltpu.SemaphoreType.DMA
    return _Comm(list(parts), [jax.ShapeDtypeStruct((3,) + p.shape[1:], BF16) for p in parts],
                 [dma((3 * n,))] * 2, start, finish)


def _send_halves(grads, name):
    n = len(grads)

    def body(*refs):
        ins, outs = refs[:n], refs[n:2 * n]
        send_sems, recv_sems = refs[2 * n:]
        x, y, c = _place()
        copies = []
        for w in range(n):
            r2 = ins[w].shape[1] // 2
            cp = pltpu.make_async_remote_copy(
                src_ref=ins[w].at[:, pl.ds((1 - c) * r2, r2), :], dst_ref=outs[w],
                send_sem=send_sems.at[w], recv_sem=recv_sems.at[w], device_id=(x, y, 1 - c), device_id_type=MESH)
            cp.start()
            copies.append(cp)
        for cp in copies:
            cp.wait_recv()
        for cp in copies:
            cp.wait_send()

    return pl.pallas_call(
        body, name=name,
        out_shape=[jax.ShapeDtypeStruct((NCHIP, g.shape[1] // 2, g.shape[2]), F32) for g in grads],
        in_specs=[ANY] * n, out_specs=[ANY] * n,
        scratch_shapes=[pltpu.SemaphoreType.DMA((n,)), pltpu.SemaphoreType.DMA((n,))],
    )(*grads)


def _row_tile(rows, cols):
    tr = rows
    while tr * cols * 4 > (1 << 20) and tr % 16 == 0:
        tr //= 2
    return tr


def _add_halves(g, t1, core, name):
    _, r, cdim = g.shape
    r2 = r // 2
    tr = _row_tile(r2, cdim)
    nt = r2 // tr

    def body(core_ref, g_ref, t_ref, p_ref, pb_ref):
        p = g_ref[...] + t_ref[...]
        p_ref[...] = p
        pb_ref[...] = p.astype(BF16)

    blk = pl.BlockSpec((None, tr, cdim), lambda j, i, core_ref: (j, i, 0))
    return pl.pallas_call(
        body, name=name,
        grid_spec=pltpu.PrefetchScalarGridSpec(
            num_scalar_prefetch=1, grid=(NCHIP, nt),
            in_specs=[pl.BlockSpec((None, tr, cdim), lambda j, i, core_ref: (j, core_ref[0] * nt + i, 0)), blk],
            out_specs=[blk, blk]),
        out_shape=[jax.ShapeDtypeStruct((NCHIP, r2, cdim), F32), jax.ShapeDtypeStruct((NCHIP, r2, cdim), BF16)],
        compiler_params=_params("parallel", "parallel"),
    )(core, g, t1)


def _add_partials(p, t2, sel, name):
    _, r2, cdim = p.shape
    tr = _row_tile(r2, cdim)
    nt = r2 // tr

    def body(sel_ref, p_ref, t_ref, o_ref):
        o_ref[...] = ((p_ref[...] + t_ref[0].astype(F32)) + t_ref[1].astype(F32)) + t_ref[2].astype(F32)

    return pl.pallas_call(
        body, name=name,
        grid_spec=pltpu.PrefetchScalarGridSpec(
            num_scalar_prefetch=1, grid=(nt,),
            in_specs=[pl.BlockSpec((None, tr, cdim), lambda i, sel_ref: (sel_ref[0], i, 0)),
                      pl.BlockSpec((3, tr, cdim), lambda i, sel_ref: (0, i, 0))],
            out_specs=pl.BlockSpec((tr, cdim), lambda i, sel_ref: (sel_ref[1] * nt + i, 0))),
        out_shape=jax.ShapeDtypeStruct((2 * r2, cdim), F32),
        compiler_params=_params("parallel"),
    )(sel, p, t2)


def _join_halves(grads):
    n = len(grads)

    def body(*refs):
        bufs = refs[n:2 * n]
        send_sems, recv_sems = refs[2 * n:]
        x, y, c = _place()
        copies = []
        for w in range(n):
            r2 = bufs[w].shape[0] // 2
            mine = bufs[w].at[pl.ds(c * r2, r2), :]
            cp = pltpu.make_async_remote_copy(
                src_ref=mine, dst_ref=mine, send_sem=send_sems.at[w], recv_sem=recv_sems.at[w],
                device_id=(x, y, 1 - c), device_id_type=MESH)
            cp.start()
            copies.append(cp)
        for w in range(n):
            r2 = bufs[w].shape[0] // 2
            theirs = bufs[w].at[pl.ds((1 - c) * r2, r2), :]
            pltpu.make_async_remote_copy(
                src_ref=theirs, dst_ref=theirs, send_sem=send_sems.at[w], recv_sem=recv_sems.at[w],
                device_id=(x, y, 1 - c), device_id_type=MESH).wait_recv()
        for cp in copies:
            cp.wait_send()

    return pl.pallas_call(
        body, name="rs_join_halves",
        out_shape=[jax.ShapeDtypeStruct(g.shape, F32) for g in grads],
        in_specs=[ANY] * n, out_specs=[ANY] * n, input_output_aliases={w: w for w in range(n)},
        scratch_shapes=[pltpu.SemaphoreType.DMA((n,)), pltpu.SemaphoreType.DMA((n,))],
    )(*grads)


def _ada_fwd(c_all, ada_w):
    nl, _, ns = ada_w.shape

    def body(c_ref, w_ref, o_ref):
        cv = c_ref[...]
        o_ref[...] = _dot((cv * jax.nn.sigmoid(cv)).astype(BF16), w_ref[...].astype(BF16))

    return pl.pallas_call(
        body, name="ada_fwd", grid=(nl,),
        in_specs=[pl.BlockSpec((NDEV, D), lambda i: (0, 0)), pl.BlockSpec((None, D, ns), lambda i: (i, 0, 0))],
        out_specs=pl.BlockSpec((None, NDEV, ns), lambda i: (i, 0, 0)),
        out_shape=jax.ShapeDtypeStruct((nl, NDEV, ns), F32),
        compiler_params=_params("parallel"),
    )(c_all, ada_w)


PACK_ROWS = 24


def _reduce_packed(gathered):
    def body(g_ref, tot_ref, loss_ref):
        tot = g_ref[0:PACK_ROWS, :]
        for dev in range(1, NDEV):
            tot = tot + g_ref[dev * PACK_ROWS:(dev + 1) * PACK_ROWS, :]
        tot_ref[...] = tot
        loss_ref[...] = jnp.zeros((8, 128), F32) + jnp.sum(tot[10:11, :])

    return pl.pallas_call(
        body, name="reduce_packed",
        out_shape=[jax.ShapeDtypeStruct((PACK_ROWS, D), F32), jax.ShapeDtypeStruct((8, 128), F32)],
    )(gathered)


def _ada_w_update(c_t, dms, w, m, v):
    nl, _, ns = w.shape
    tr = 256

    def body(ct_ref, dm_ref, w_ref, m_ref, v_ref, g_ref, d_ref, m2_ref, v2_ref):
        ct = ct_ref[...]
        sc = ct * jax.nn.sigmoid(ct)
        dm = dm_ref[...]
        g = sc[:, 0:1] * dm[0:1, :]
        for b in range(1, NDEV):
            g = g + sc[:, b:b + 1] * dm[b:b + 1, :]
        g_ref[...] = g
        d_ref[...], m2_ref[...], v2_ref[...] = _adamw_math(w_ref[...], g, m_ref[...], v_ref[...])

    blk = pl.BlockSpec((None, tr, ns), lambda i, j: (i, j, 0))
    return pl.pallas_call(
        body, name="ada_w_update", grid=(nl, D // tr),
        in_specs=[pl.BlockSpec((tr, NDEV), lambda i, j: (j, 0)),
                  pl.BlockSpec((None, NDEV, ns), lambda i, j: (i, 0, 0)), blk, blk, blk],
        out_specs=[blk] * 4,
        out_shape=[jax.ShapeDtypeStruct((nl, D, ns), F32)] * 4,
        compiler_params=_params("parallel", "parallel"),
    )(c_t, dms, w, m, v)


def _chip_major(a, parts):
    g, _, cdim = a.shape
    return jnp.transpose(a.reshape(g, NCHIP, parts, cdim), (1, 0, 2, 3)).reshape(NCHIP, g * parts, cdim)


def kernel(x, c, norm_g, ada_w, ada_b, even_w_in, pool_w, pool_scale, even_w_out, odd_w_in, conv_w, conv_b, odd_w_out, final_g, loss_target, m_norm_g, m_ada_w, m_ada_b, m_even_w_in, m_pool_w, m_pool_scale, m_even_w_out, m_odd_w_in, m_conv_w, m_conv_b, m_odd_w_out, m_final_g, v_norm_g, v_ada_w, v_ada_b, v_even_w_in, v_pool_w, v_pool_scale, v_even_w_out, v_odd_w_in, v_conv_w, v_conv_b, v_odd_w_out, v_final_g):
    ix, iy, ic = _place()
    chip = 2 * ix + iy
    batch = 2 * chip + ic
    sel = jnp.stack([chip, ic]).astype(jnp.int32)
    ns_ada = ada_w.shape[2]
    ns_conv = conv_b.shape[1]

    conv_rows = jnp.pad(jnp.concatenate([conv_w[0], conv_b], axis=0), ((0, 3), (0, D - ns_conv)))
    first = _allgather8(jnp.concatenate([c, conv_rows], axis=0), "gather_c_conv").reshape(NCHIP, 2, 8, D)
    c_all = first[:, :, 0].reshape(NDEV, D)
    cw_full = jnp.transpose(first[:, 0, 1:5, 0:ns_conv], (1, 0, 2)).reshape(4, DI)
    cw8 = jnp.concatenate([cw_full[0:3], jnp.zeros((5, DI), F32)], axis=0)
    cb_full = cw_full[3:4]

    m_cols = _allgather8(_ada_fwd(c_all, ada_w).reshape(2 * NDEV, ns_ada), "gather_ada")
    m_cols = m_cols.reshape(NCHIP, 2, 2, NDEV, ns_ada)[:, 0]
    m_mine = lax.dynamic_index_in_dim(m_cols, batch, axis=2, keepdims=False)
    m_mine = jnp.transpose(m_mine, (1, 0, 2)).reshape(2, 3 * D) + ada_b
    zrow = jnp.zeros((3, D), F32)

    def vec_rows(i):
        sh, sc, gt = m_mine[i, 0:D], m_mine[i, D:2 * D], m_mine[i, 2 * D:3 * D]
        return jnp.concatenate([jnp.stack([norm_g[i], sc, sh, gt, final_g]), zrow], axis=0)

    win0, pw, wout0 = _run_comm(_gather_comm([
        even_w_in[0].astype(BF16), pool_w[0].astype(BF16).reshape(PG, PG), even_w_out[0].astype(BF16)]),
        "gather_even_weights")
    pw_full = jnp.transpose(pw.reshape(NCHIP, 4, PG // NCHIP, PG), (1, 0, 2, 3)).reshape(4, PG, PG)

    dx0, grads, sums = _local_step(
        x[0], loss_target[0], vec_rows(0), vec_rows(1), win0, pw_full, pool_scale, wout0.reshape(DI, D),
        (odd_w_in[0].astype(BF16), odd_w_out[0].astype(BF16)), cw8, cb_full, sel)
    r_win0, r_pw, r_wout0, r_win1, r_wout1 = grads

    packed = jnp.concatenate([
        sums["dm0"], sums["dm1"], sums["norm_g"], sums["pool_scale"], sums["final_g"], sums["loss"],
        sums["conv_w"].reshape(6, D), sums["conv_b"].reshape(2, D), jnp.zeros((PACK_ROWS - 19, D), F32)], axis=0)
    gathered = _allgather8(packed, "gather_sums")
    tot, loss8 = _reduce_packed(gathered)
    loss = loss8[0, 0]
    g_norm_g, g_pool_scale, g_final_g = tot[6:8], tot[8:9], tot[9]
    g_ada_b = tot[0:6].reshape(2, 3 * D)
    g_conv_w = lax.dynamic_slice_in_dim(tot[11:17].reshape(3, DI), chip * (DI // NCHIP), DI // NCHIP, axis=1)
    g_conv_b = lax.dynamic_slice_in_dim(tot[17:19].reshape(1, DI), chip * (DI // NCHIP), DI // NCHIP, axis=1)
    dm_all = gathered.reshape(NDEV, PACK_ROWS, D)[:, 0:6].reshape(NDEV, 2, 3 * D)
    dm_cols = jnp.transpose(lax.dynamic_slice_in_dim(dm_all, chip * ns_ada, ns_ada, axis=2), (1, 0, 2))
    g_ada_w, d_ada_w, nm_ada_w, nv_ada_w = _ada_w_update(jnp.transpose(c_all), dm_cols, ada_w, m_ada_w, v_ada_w)

    def upd(w, g, m, v, name):
        shape = w.shape
        w2, m2, v2 = (a.reshape(g.shape) for a in (w, m, v))
        d, nm, nv = _adamw(w2, g, m2, v2, name)
        return g.reshape(shape), d.reshape(shape), nm.reshape(shape), nv.reshape(shape)

    o_win0 = upd(even_w_in, r_win0, m_even_w_in, v_even_w_in, "adamw_win0")
    o_pw = upd(pool_w, r_pw, m_pool_w, v_pool_w, "adamw_pool")
    o_wout0 = upd(even_w_out, r_wout0, m_even_w_out, v_even_w_out, "adamw_wout0")
    o_win1 = upd(odd_w_in, r_win1, m_odd_w_in, v_odd_w_in, "adamw_win1")
    o_wout1 = upd(odd_w_out, r_wout1, m_odd_w_out, v_odd_w_out, "adamw_wout1")

    def pack_small(ng, ab, ps, fg, cwv, cbv):
        conv = jnp.concatenate([cwv.reshape(3, -1), cbv.reshape(1, -1)], axis=0).reshape(2, D)
        return jnp.concatenate([ng, ab.reshape(6, D), ps, fg.reshape(1, D), conv, jnp.zeros((4, D), F32)], axis=0)

    sw = pack_small(norm_g, ada_b, pool_scale, final_g, conv_w, conv_b)
    sg = pack_small(g_norm_g, g_ada_b, g_pool_scale, g_final_g, g_conv_w, g_conv_b)
    sm = pack_small(m_norm_g, m_ada_b, m_pool_scale, m_final_g, m_conv_w, m_conv_b)
    sv = pack_small(v_norm_g, v_ada_b, v_pool_scale, v_final_g, v_conv_w, v_conv_b) + jnp.concatenate(
        [jnp.zeros((12, D), F32), jnp.ones((4, D), F32)], axis=0)
    small = _adamw(sw, sg, sm, sv, "adamw_small")

    def unpack_small(a):
        conv = a[10:12].reshape(4, -1)
        return dict(norm_g=a[0:2], ada_b=a[2:8].reshape(2, 3 * D), pool_scale=a[8:9], final_g=a[9],
                    conv_w=conv[0:3].reshape(conv_w.shape), conv_b=conv[3:4].reshape(conv_b.shape))

    s_grad = dict(norm_g=g_norm_g, ada_b=g_ada_b, pool_scale=g_pool_scale, final_g=g_final_g,
                  conv_w=g_conv_w.reshape(conv_w.shape), conv_b=g_conv_b.reshape(conv_b.shape))
    s_out = [s_grad] + [unpack_small(a) for a in small]

    outs = []
    for k in range(4):
        sm_k = s_out[k]
        outs.append([sm_k["norm_g"], (g_ada_w, d_ada_w, nm_ada_w, nv_ada_w)[k], sm_k["ada_b"], o_win0[k], o_pw[k],
                     sm_k["pool_scale"], o_wout0[k], o_win1[k], sm_k["conv_w"], sm_k["conv_b"], o_wout1[k],
                     sm_k["final_g"]])
    return (loss, dx0[None], *outs[0], *outs[1], *outs[2], *outs[3])
```

```python
import functools

import jax
import jax.numpy as jnp
from jax import lax
from jax.experimental import pallas as pl
from jax.experimental.pallas import tpu as pltpu

F32 = jnp.float32
BF16 = jnp.bfloat16
MESH = pl.DeviceIdType.MESH

D = 1024
DI = 2048
DP = 1024
NE = 6144
NO = 8192
WINDOWS = (2, 4, 8, 16)
PG = 256
HD = 64
NCHIP = 4
NDEV = 8
EPS = 1e-6
INV_SQRT_HD = 0.125

LR, B1, B2, EPS_ADAM, WD, STEP = 0.001, 0.9, 0.999, 1e-08, 0.01, 10

TM = 512
TME = 256
CT = 512
BQ = 512
BK = 256
HALO = 16
VMEM_LIMIT = 56 * 1024 * 1024


def _dot(a, b):
    return jnp.dot(a, b, preferred_element_type=F32)


def _dot_nt(a, b):
    return lax.dot_general(a, b, (((1,), (1,)), ((), ())), preferred_element_type=F32)


def _dot_tn(a, b):
    return lax.dot_general(a, b, (((0,), (0,)), ((), ())), preferred_element_type=F32)


def _params(*sem):
    return pltpu.CompilerParams(dimension_semantics=sem, vmem_limit_bytes=VMEM_LIMIT)


def _rowsum(v):
    return jnp.sum(v, axis=0, keepdims=True)


def _modulated_norm(xv, vec_ref):
    r = lax.rsqrt(jnp.mean(xv * xv, axis=-1, keepdims=True) + EPS)
    return (((xv * r) * vec_ref[0:1, :]) * (1.0 + vec_ref[1:2, :]) + vec_ref[2:3, :]).astype(BF16)


def _norm_mod(x, vecs, name):
    s = x.shape[0]
    tm = min(TM, s)

    def body(x_ref, vec_ref, h_ref):
        h_ref[...] = _modulated_norm(x_ref[...], vec_ref)

    return pl.pallas_call(
        body, name=name, grid=(s // tm,),
        in_specs=[pl.BlockSpec((tm, D), lambda i: (i, 0)), pl.BlockSpec((8, D), lambda i: (0, 0))],
        out_specs=pl.BlockSpec((tm, D), lambda i: (i, 0)),
        out_shape=jax.ShapeDtypeStruct((s, D), BF16),
        compiler_params=_params("parallel"),
    )(x, vecs)


def _inproj(h, w, name, comm=None):
    s = h.shape[0]
    ns = w.shape[2]
    tm = min(TM, s)
    ni = s // tm
    comm = comm or _NO_COMM
    nci, nco = len(comm.arrays), len(comm.out_shape)

    def body(*refs):
        h_ref, w_ref = refs[:2]
        proj_ref = refs[2 + nci]
        cargs = (refs[2:2 + nci], refs[3 + nci:3 + nci + nco], refs[3 + nci + nco:])
        j, i = pl.program_id(0), pl.program_id(1)
        if nci:
            pl.when((j == 0) & (i == 0))(lambda: comm.start(*cargs))
        proj_ref[...] = _dot(h_ref[...], w_ref[...]).astype(BF16)
        if nci:
            pl.when((j == NCHIP - 1) & (i == ni - 1))(lambda: comm.finish(*cargs))

    return pl.pallas_call(
        body, name=name, grid=(NCHIP, ni),
        in_specs=[pl.BlockSpec((tm, D), lambda j, i: (i, 0)),
                  pl.BlockSpec((None, D, ns), lambda j, i: (j, 0, 0))] + [ANY] * nci,
        out_specs=[pl.BlockSpec((tm, ns), lambda j, i: (i, j))] + [ANY] * nco,
        out_shape=[jax.ShapeDtypeStruct((s, NCHIP * ns), BF16)] + comm.out_shape,
        scratch_shapes=comm.sems,
        compiler_params=_params("arbitrary", "arbitrary"),
    )(h, w, *comm.arrays)


def _pool_fwd(proj0, pw, pscale):
    s = proj0.shape[0]
    tm = min(TM, s)
    hb = tm // 16

    def body(u_ref, halo_ref, w_ref, sc_ref, p_ref, y_ref, ext_ref):
        i = pl.program_id(0)
        ext_ref[16:, :] = u_ref[...].astype(F32)
        ext_ref[0:16, :] = jnp.where(i > 0, halo_ref[...].astype(F32), 0.0)
        t = i * tm + lax.broadcasted_iota(jnp.int32, (tm, 1), 0)
        for g, wdw in enumerate(WINDOWS):
            cs = slice(g * PG, (g + 1) * PG)
            u = ext_ref[16:16 + tm, cs]
            acc = u
            for j in range(1, wdw):
                acc = acc + ext_ref[16 - j:16 - j + tm, cs]
            inv = 1.0 / jnp.minimum(t + 1, wdw).astype(F32)
            pb = (acc * inv - u).astype(BF16)
            p_ref[:, cs] = pb
            y_ref[:, cs] = _dot(pb, w_ref[g]) * sc_ref[:, cs]

    return pl.pallas_call(
        body, name="pool_fwd", grid=(s // tm,),
        in_specs=[pl.BlockSpec((tm, DP), lambda i: (i, 0)),
                  pl.BlockSpec((16, DP), lambda i: (jnp.maximum(i * hb - 1, 0), 0)),
                  pl.BlockSpec((4, PG, PG), lambda i: (0, 0, 0)),
                  pl.BlockSpec((1, DP), lambda i: (0, 0))],
        out_specs=[pl.BlockSpec((tm, DP), lambda i: (i, 0)),
                   pl.BlockSpec((tm, DP), lambda i: (i, 0))],
        out_shape=[jax.ShapeDtypeStruct((s, DP), BF16), jax.ShapeDtypeStruct((s, DP), F32)],
        scratch_shapes=[pltpu.VMEM((tm + 16, DP), F32)],
        compiler_params=_params("parallel"),
    )(proj0, proj0, pw, pscale)


def _sb_logits(nz, mask):
    neg_abs = lax.bitcast_convert_type(lax.bitcast_convert_type(nz, jnp.uint32) | jnp.uint32(0x80000000), F32)
    t = jnp.log(1.0 + jnp.exp(neg_abs))
    lf = jnp.minimum(nz, 0.0) - t
    lam = lf - nz
    if mask is not None:
        lf = jnp.where(mask, lf, 0.0)
    return lf, lam


def _split_dot(v, tri):
    hi = v.astype(BF16)
    lo = (v - hi.astype(F32)).astype(BF16)
    return _dot(hi, tri) + _dot(lo, tri)


def _tri_masks():
    row = lax.broadcasted_iota(jnp.int32, (BK, BK), 0)
    col = lax.broadcasted_iota(jnp.int32, (BK, BK), 1)
    return (row > col).astype(BF16), (row >= col).astype(BF16)


def _causal_mask(offset):
    row = lax.broadcasted_iota(jnp.int32, (BQ, BK), 0)
    col = lax.broadcasted_iota(jnp.int32, (BQ, BK), 1)
    return col + offset < row


def _attn_fwd(proj0):
    s = proj0.shape[0]
    nq = s // BQ
    kpq = BQ // BK

    def body(q_ref, k_ref, v_ref, o_ref, qn_ref, k8_ref, vb_ref):
        qn_ref[...] = (-q_ref[...]).astype(BF16)
        k8_ref[...] = (k_ref[...] * INV_SQRT_HD).astype(BF16)
        vb_ref[...] = v_ref[...].astype(BF16)
        after, _ = _tri_masks()
        heads = [slice(HD * h, HD * (h + 1)) for h in range(2)]

        def qstep(qi, carry):
            q0 = pl.multiple_of(qi * BQ, BQ)
            qn = [qn_ref[pl.ds(q0, BQ), ls] for ls in heads]

            def blocks(k_hi, state, masks):
                ks = [pl.multiple_of(k_hi + (kpq - 1 - j) * BK, BK) for j in range(kpq)]
                lanes = [(j, h) for j in range(kpq) for h in range(len(heads))]
                mask = lambda j: None if masks is None else masks[j]
                nz = {jh: _dot_nt(qn[jh[1]], k8_ref[pl.ds(ks[jh[0]], BK), heads[jh[1]]]) for jh in lanes}
                ll = {jh: _sb_logits(nz[jh], mask(jh[0])) for jh in lanes}
                aft = {jh: _dot(ll[jh][0].astype(BF16), after) for jh in lanes}
                new = []
                for h in range(len(heads)):
                    o_acc, c = state[h]
                    for j in range(kpq):
                        a = jnp.exp(ll[j, h][1] + aft[j, h] + c)
                        if masks is not None:
                            a = jnp.where(masks[j], a, 0.0)
                        o_acc = o_acc + _dot(a.astype(BF16), vb_ref[pl.ds(ks[j], BK), heads[h]])
                        c = c + aft[j, h][:, 0:1] + ll[j, h][0][:, 0:1]
                    new.append((o_acc, c))
                return tuple(new)

            state = tuple((jnp.zeros((BQ, HD), F32), jnp.zeros((BQ, 1), F32)) for _ in heads)
            state = blocks(q0, state, [_causal_mask((kpq - 1 - j) * BK) for j in range(kpq)])
            state = lax.fori_loop(0, qi, lambda j, st: blocks(q0 - (j + 1) * BQ, st, None), state)
            for h, ls in enumerate(heads):
                o_ref[pl.ds(q0, BQ), ls] = state[h][0]
            return carry

        lax.fori_loop(0, nq, qstep, 0)

    return pl.pallas_call(
        body, name="attn_fwd", grid=(DP // 128,),
        in_specs=[pl.BlockSpec((s, 128), lambda h: (0, 8 + h)),
                  pl.BlockSpec((s, 128), lambda h: (0, 16 + h)),
                  pl.BlockSpec((s, 128), lambda h: (0, 24 + h))],
        out_specs=pl.BlockSpec((s, 128), lambda h: (0, h)),
        out_shape=jax.ShapeDtypeStruct((s, DP), F32),
        scratch_shapes=[pltpu.VMEM((s, 128), BF16)] * 3,
        compiler_params=_params("parallel"),
    )(proj0, proj0, proj0)


def _even_out(ypool, ysb, proj0, wout, x, vecs, vecs_next):
    s = x.shape[0]
    tm = min(TME, s)

    def body(yp_ref, ys_ref, gate_ref, w_ref, x_ref, vec_ref, vecn_ref, x1_ref, out_ref, yg_ref, hn_ref):
        gt = gate_ref[...].astype(F32)
        sl = gt * jax.nn.sigmoid(gt)
        yg_ref[:, :DP] = (yp_ref[...] * sl[:, :DP]).astype(BF16)
        yg_ref[:, DP:] = (ys_ref[...] * sl[:, DP:]).astype(BF16)
        out = _dot(yg_ref[...], w_ref[...])
        out_ref[...] = out
        x1 = x_ref[...] + (1.0 + vec_ref[3:4, :]) * out
        x1_ref[...] = x1
        hn_ref[...] = _modulated_norm(x1, vecn_ref)

    row = lambda i: (i, 0)
    const = lambda i: (0, 0)
    return pl.pallas_call(
        body, name="even_out", grid=(s // tm,),
        in_specs=[pl.BlockSpec((tm, DP), row), pl.BlockSpec((tm, DP), row),
                  pl.BlockSpec((tm, DI), lambda i: (i, 2)),
                  pl.BlockSpec((DI, D), const),
                  pl.BlockSpec((tm, D), row), pl.BlockSpec((8, D), const), pl.BlockSpec((8, D), const)],
        out_specs=[pl.BlockSpec((tm, D), row), pl.BlockSpec((tm, D), row), pl.BlockSpec((tm, DI), row),
                   pl.BlockSpec((tm, D), row)],
        out_shape=[jax.ShapeDtypeStruct((s, D), F32), jax.ShapeDtypeStruct((s, D), F32),
                   jax.ShapeDtypeStruct((s, DI), BF16), jax.ShapeDtypeStruct((s, D), BF16)],
        compiler_params=_params("parallel"),
    )(ypool, ysb, proj0, wout, x, vecs, vecs_next)


def _odd_out(proj1, wout, x1, vecs, cw, cb, target):
    s = x1.shape[0]
    tm = min(TME, s)
    hb = tm // HALO

    def body(gb_ref, gc_ref, u_ref, gt_ref, hgc_ref, hu_ref, w_ref, x1_ref, vec_ref, cw_ref, cb_ref, tg_ref,
             dx2_ref, out_ref, y1_ref, acc_ref, ext_ref):
        i = pl.program_id(0)

        @pl.when(i == 0)
        def _():
            acc_ref[...] = jnp.zeros_like(acc_ref)

        ext_ref[HALO:, :] = gc_ref[...].astype(F32) * u_ref[...].astype(F32)
        ext_ref[0:HALO, :] = jnp.where(i > 0, hgc_ref[...].astype(F32) * hu_ref[...].astype(F32), 0.0)
        for c in range(DI // CT):
            cs = slice(c * CT, (c + 1) * CT)
            conv = (cb_ref[0:1, cs] + cw_ref[0:1, cs] * ext_ref[HALO - 2:HALO - 2 + tm, cs]
                    + cw_ref[1:2, cs] * ext_ref[HALO - 1:HALO - 1 + tm, cs]
                    + cw_ref[2:3, cs] * ext_ref[HALO:HALO + tm, cs])
            gt = gt_ref[:, cs].astype(F32)
            y1_ref[:, cs] = (gb_ref[:, cs].astype(F32) * conv * (gt * jax.nn.sigmoid(gt))).astype(BF16)
        out = _dot(y1_ref[...], w_ref[...])
        out_ref[...] = out
        x2 = x1_ref[...] + (1.0 + vec_ref[3:4, :]) * out
        r = lax.rsqrt(jnp.mean(x2 * x2, axis=-1, keepdims=True) + EPS)
        nrm = x2 * r
        fg = vec_ref[4:5, :]
        err = nrm * fg - tg_ref[...]
        acc_ref[1:2, :] += _rowsum(err * err) * (0.5 / D)
        dyf = err * (1.0 / D)
        acc_ref[0:1, :] += _rowsum(dyf * nrm)
        dn = dyf * fg
        dx2_ref[...] = r * (dn - nrm * jnp.mean(dn * nrm, axis=-1, keepdims=True))

    row = lambda i: (i, 0)
    halo = lambda col: (lambda i: (jnp.maximum(i * hb - 1, 0), col))
    const = lambda i: (0, 0)
    return pl.pallas_call(
        body, name="odd_out", grid=(s // tm,),
        in_specs=[pl.BlockSpec((tm, DI), lambda i: (i, 0)), pl.BlockSpec((tm, DI), lambda i: (i, 1)),
                  pl.BlockSpec((tm, DI), lambda i: (i, 2)), pl.BlockSpec((tm, DI), lambda i: (i, 3)),
                  pl.BlockSpec((HALO, DI), halo(1)), pl.BlockSpec((HALO, DI), halo(2)),
                  pl.BlockSpec((DI, D), const), pl.BlockSpec((tm, D), row), pl.BlockSpec((8, D), const),
                  pl.BlockSpec((8, DI), const), pl.BlockSpec((1, DI), const), pl.BlockSpec((tm, D), row)],
        out_specs=[pl.BlockSpec((tm, D), row), pl.BlockSpec((tm, D), row), pl.BlockSpec((tm, DI), row),
                   pl.BlockSpec((8, D), const)],
        out_shape=[jax.ShapeDtypeStruct((s, D), F32), jax.ShapeDtypeStruct((s, D), F32),
                   jax.ShapeDtypeStruct((s, DI), BF16), jax.ShapeDtypeStruct((8, D), F32)],
        scratch_shapes=[pltpu.VMEM((tm + HALO, DI), F32)],
        compiler_params=_params("arbitrary"),
    )(proj1, proj1, proj1, proj1, proj1, proj1, wout, x1, vecs, cw, cb, target)


def _odd_bwd(dx2, out1, proj1, wout, vecs, cw, cb):
    s = dx2.shape[0]
    tm = min(TME, s)
    nb = s // tm
    hb = tm // HALO

    def body(dx2_ref, out1_ref, gb_ref, gc_ref, u_ref, gt_ref, hgc_ref, hu_ref, w_ref, vec_ref, cw_ref, cb_ref,
             dout_ref, dproj_ref, accv_ref, accd_ref, uext_ref, dext_ref, dy_ref):
        i = pl.program_id(0)
        blk = nb - 1 - i

        @pl.when(i == 0)
        def _():
            accv_ref[...] = jnp.zeros_like(accv_ref)
            accd_ref[...] = jnp.zeros_like(accd_ref)
            dext_ref[tm:tm + 8, :] = jnp.zeros((8, DI), F32)

        dx2v = dx2_ref[...]
        accd_ref[0:1, :] += _rowsum(dx2v * out1_ref[...])
        dout = (dx2v * (1.0 + vec_ref[3:4, :])).astype(BF16)
        dout_ref[...] = dout
        dy_ref[...] = _dot_nt(dout, w_ref[...])
        uext_ref[HALO:, :] = gc_ref[...].astype(F32) * u_ref[...].astype(F32)
        uext_ref[0:HALO, :] = jnp.where(blk > 0, hgc_ref[...].astype(F32) * hu_ref[...].astype(F32), 0.0)
        for c in range(DI // CT):
            cs = slice(c * CT, (c + 1) * CT)
            u0 = uext_ref[HALO - 2:HALO - 2 + tm, cs]
            u1 = uext_ref[HALO - 1:HALO - 1 + tm, cs]
            u2 = uext_ref[HALO:HALO + tm, cs]
            w0, w1, w2 = cw_ref[0:1, cs], cw_ref[1:2, cs], cw_ref[2:3, cs]
            conv = cb_ref[0:1, cs] + w0 * u0 + w1 * u1 + w2 * u2
            gt = gt_ref[:, cs].astype(F32)
            sg = jax.nn.sigmoid(gt)
            gb = gb_ref[:, cs].astype(F32)
            dy = dy_ref[:, cs]
            t1 = dy * (gt * sg)
            dproj_ref[:, cs] = (t1 * conv).astype(BF16)
            dconv = t1 * gb
            dproj_ref[:, 3 * DI + c * CT:3 * DI + (c + 1) * CT] = (
                dy * gb * conv * (sg * (1.0 + gt * (1.0 - sg)))).astype(BF16)
            accv_ref[0:1, cs] += _rowsum(dconv * u0)
            accv_ref[1:2, cs] += _rowsum(dconv * u1)
            accv_ref[2:3, cs] += _rowsum(dconv * u2)
            accv_ref[3:4, cs] += _rowsum(dconv)
            dext_ref[0:tm, cs] = dconv
            duu = w2 * dconv + w1 * dext_ref[1:tm + 1, cs] + w0 * dext_ref[2:tm + 2, cs]
            dproj_ref[:, DI + c * CT:DI + (c + 1) * CT] = (duu * u_ref[:, cs].astype(F32)).astype(BF16)
            dproj_ref[:, 2 * DI + c * CT:2 * DI + (c + 1) * CT] = (duu * gc_ref[:, cs].astype(F32)).astype(BF16)
        dext_ref[tm:tm + 8, :] = dext_ref[0:8, :]

    rrow = lambda i: (nb - 1 - i, 0)
    rcol = lambda col: (lambda i: (nb - 1 - i, col))
    halo = lambda col: (lambda i: (jnp.maximum((nb - 1 - i) * hb - 1, 0), col))
    const = lambda i: (0, 0)
    return pl.pallas_call(
        body, name="odd_bwd", grid=(nb,),
        in_specs=[pl.BlockSpec((tm, D), rrow), pl.BlockSpec((tm, D), rrow),
                  pl.BlockSpec((tm, DI), rcol(0)), pl.BlockSpec((tm, DI), rcol(1)),
                  pl.BlockSpec((tm, DI), rcol(2)), pl.BlockSpec((tm, DI), rcol(3)),
                  pl.BlockSpec((HALO, DI), halo(1)), pl.BlockSpec((HALO, DI), halo(2)),
                  pl.BlockSpec((DI, D), const), pl.BlockSpec((8, D), const),
                  pl.BlockSpec((8, DI), const), pl.BlockSpec((1, DI), const)],
        out_specs=[pl.BlockSpec((tm, D), rrow), pl.BlockSpec((tm, NO), rrow),
                   pl.BlockSpec((8, DI), const), pl.BlockSpec((8, D), const)],
        out_shape=[jax.ShapeDtypeStruct((s, D), BF16), jax.ShapeDtypeStruct((s, NO), BF16),
                   jax.ShapeDtypeStruct((8, DI), F32), jax.ShapeDtypeStruct((8, D), F32)],
        scratch_shapes=[pltpu.VMEM((tm + HALO, DI), F32), pltpu.VMEM((tm + 8, DI), F32), pltpu.VMEM((tm, DI), F32)],
        compiler_params=_params("arbitrary"),
    )(dx2, out1, proj1, proj1, proj1, proj1, proj1, proj1, wout, vecs, cw, cb)


def _grad_w_cols(a, b, name):
    s, m = a.shape
    ns = b.shape[1] // NCHIP
    ts = min(TM, s)

    def body(a_ref, b_ref, o_ref):
        @pl.when(pl.program_id(1) == 0)
        def _():
            o_ref[...] = jnp.zeros_like(o_ref)

        o_ref[...] += _dot_tn(a_ref[...], b_ref[...])

    return pl.pallas_call(
        body, name=name, grid=(NCHIP, s // ts),
        in_specs=[pl.BlockSpec((ts, m), lambda j, k: (k, 0)),
                  pl.BlockSpec((ts, ns), lambda j, k: (k, j))],
        out_specs=pl.BlockSpec((None, m, ns), lambda j, k: (j, 0, 0)),
        out_shape=jax.ShapeDtypeStruct((NCHIP, m, ns), F32),
        compiler_params=_params("parallel", "arbitrary"),
    )(a, b)


def _grad_w_rows(a, b, name):
    s = a.shape[0]
    ms = a.shape[1] // NCHIP
    n = b.shape[1]
    ts = min(TM, s)

    def body(a_ref, b_ref, o_ref):
        @pl.when(pl.program_id(1) == 0)
        def _():
            o_ref[...] = jnp.zeros_like(o_ref)

        o_ref[...] += _dot_tn(a_ref[...], b_ref[...])

    return pl.pallas_call(
        body, name=name, grid=(NCHIP, s // ts),
        in_specs=[pl.BlockSpec((ts, ms), lambda i, k: (k, i)),
                  pl.BlockSpec((ts, n), lambda i, k: (k, 0))],
        out_specs=pl.BlockSpec((None, ms, n), lambda i, k: (i, 0, 0)),
        out_shape=jax.ShapeDtypeStruct((NCHIP, ms, n), F32),
        compiler_params=_params("parallel", "arbitrary"),
    )(a, b)


def _inproj_bwd(dproj, w, x, dx_in, vecs, name, comm=None):
    s = x.shape[0]
    ns = w.shape[2]
    tm = min(TME, s)
    ni = s // tm
    comm = comm or _NO_COMM
    nci, nco = len(comm.arrays), len(comm.out_shape)

    def body(*refs):
        dp_ref, w_hbm, x_ref, dxin_ref, vec_ref = refs[:5]
        dx_ref, acc_ref = refs[5 + nci:7 + nci]
        w_ref = refs[7 + nci + nco]
        cargs = (refs[5:5 + nci], refs[7 + nci:7 + nci + nco], refs[8 + nci + nco:])
        i = pl.program_id(0)

        @pl.when(i == 0)
        def _():
            acc_ref[...] = jnp.zeros_like(acc_ref)
            if nci:
                comm.start(*cargs)
            pltpu.sync_copy(w_hbm, w_ref)

        dh = _dot_nt(dp_ref[:, 0:ns], w_ref[0])
        for j in range(1, NCHIP):
            dh = dh + _dot_nt(dp_ref[:, j * ns:(j + 1) * ns], w_ref[j])
        xv = x_ref[...]
        r = lax.rsqrt(jnp.mean(xv * xv, axis=-1, keepdims=True) + EPS)
        nrm = xv * r
        g = vec_ref[0:1, :]
        sc1 = 1.0 + vec_ref[1:2, :]
        dhn = dh * nrm
        acc_ref[0:1, :] += _rowsum(dh)
        acc_ref[1:2, :] += _rowsum(dhn) * g
        acc_ref[2:3, :] += _rowsum(dhn) * sc1
        dn = dh * (g * sc1)
        dx_ref[...] = dxin_ref[...] + r * (dn - nrm * jnp.mean(dn * nrm, axis=-1, keepdims=True))

        if nci:
            pl.when(i == ni - 1)(lambda: comm.finish(*cargs))

    row = lambda i: (i, 0)
    const = lambda i: (0, 0)
    return pl.pallas_call(
        body, name=name, grid=(ni,),
        in_specs=[pl.BlockSpec((tm, NCHIP * ns), row), ANY,
                  pl.BlockSpec((tm, D), row), pl.BlockSpec((tm, D), row), pl.BlockSpec((8, D), const)] + [ANY] * nci,
        out_specs=[pl.BlockSpec((tm, D), row), pl.BlockSpec((8, D), const)] + [ANY] * nco,
        out_shape=[jax.ShapeDtypeStruct((s, D), F32), jax.ShapeDtypeStruct((8, D), F32)] + comm.out_shape,
        scratch_shapes=[pltpu.VMEM(w.shape, BF16)] + comm.sems,
        compiler_params=_params("arbitrary"),
    )(dproj, w, x, dx_in, vecs, *comm.arrays)


def _even_bwd(dx1, out0, ypool, ysb, proj0, wout, vecs):
    s = dx1.shape[0]
    tm = min(TME, s)

    def body(dx1_ref, out0_ref, yp_ref, ys_ref, gate_ref, w_ref, vec_ref,
             dout_ref, dyp_ref, dys_ref, dgt_ref, acc_ref):
        @pl.when(pl.program_id(0) == 0)
        def _():
            acc_ref[...] = jnp.zeros_like(acc_ref)

        dx1v = dx1_ref[...]
        acc_ref[0:1, :] += _rowsum(dx1v * out0_ref[...])
        dout = (dx1v * (1.0 + vec_ref[3:4, :])).astype(BF16)
        dout_ref[...] = dout
        dyg = _dot_nt(dout, w_ref[...])
        gt = gate_ref[...].astype(F32)
        sg = jax.nn.sigmoid(gt)
        sl = gt * sg
        dsl = sg * (1.0 + gt * (1.0 - sg))
        dyp_ref[...] = dyg[:, :DP] * sl[:, :DP]
        dys_ref[...] = dyg[:, DP:] * sl[:, DP:]
        dgt_ref[:, :DP] = (dyg[:, :DP] * yp_ref[...] * dsl[:, :DP]).astype(BF16)
        dgt_ref[:, DP:] = (dyg[:, DP:] * ys_ref[...] * dsl[:, DP:]).astype(BF16)

    row = lambda i: (i, 0)
    const = lambda i: (0, 0)
    return pl.pallas_call(
        body, name="even_bwd", grid=(s // tm,),
        in_specs=[pl.BlockSpec((tm, D), row), pl.BlockSpec((tm, D), row),
                  pl.BlockSpec((tm, DP), row), pl.BlockSpec((tm, DP), row),
                  pl.BlockSpec((tm, DI), lambda i: (i, 2)),
                  pl.BlockSpec((DI, D), const), pl.BlockSpec((8, D), const)],
        out_specs=[pl.BlockSpec((tm, D), row), pl.BlockSpec((tm, DP), row), pl.BlockSpec((tm, DP), row),
                   pl.BlockSpec((tm, DI), row), pl.BlockSpec((8, D), const)],
        out_shape=[jax.ShapeDtypeStruct((s, D), BF16), jax.ShapeDtypeStruct((s, DP), F32),
                   jax.ShapeDtypeStruct((s, DP), F32), jax.ShapeDtypeStruct((s, DI), BF16),
                   jax.ShapeDtypeStruct((8, D), F32)],
        compiler_params=_params("arbitrary"),
    )(dx1, out0, ypool, ysb, proj0, wout, vecs)


def _pool_bwd(dyp, p, pw, pscale):
    s = dyp.shape[0]
    tm = min(TM, s)
    nb = s // tm
    hb = tm // 16

    def body(dy_ref, dyh_ref, p_ref, w_ref, sc_ref, du_ref, dw_ref, acc_ref, ext_ref):
        i = pl.program_id(0)

        @pl.when(i == 0)
        def _():
            dw_ref[...] = jnp.zeros_like(dw_ref)
            acc_ref[...] = jnp.zeros_like(acc_ref)

        t = i * tm + lax.broadcasted_iota(jnp.int32, (tm + 16, 1), 0)
        for g, wdw in enumerate(WINDOWS):
            cs = slice(g * PG, (g + 1) * PG)
            sc = sc_ref[:, cs]
            dy = dy_ref[:, cs]
            dyh = jnp.where(i < nb - 1, dyh_ref[:, cs], 0.0)
            pb = p_ref[:, cs]
            wg = w_ref[g]
            acc_ref[0:1, cs] += _rowsum(dy * _dot(pb, wg))
            dypre = (dy * sc).astype(BF16)
            dw_ref[g] += _dot_tn(pb, dypre)
            dp = _dot_nt(dypre, wg)
            dph = _dot_nt((dyh * sc).astype(BF16), wg)
            inv = 1.0 / jnp.minimum(t + 1, wdw).astype(F32)
            ext_ref[0:tm, cs] = dp * inv[0:tm]
            ext_ref[tm:tm + 16, cs] = dph * inv[tm:tm + 16]
            acc = ext_ref[0:tm, cs]
            for j in range(1, wdw):
                acc = acc + ext_ref[j:j + tm, cs]
            du_ref[:, cs] = (acc - dp).astype(BF16)

    row = lambda i: (i, 0)
    return pl.pallas_call(
        body, name="pool_bwd", grid=(nb,),
        in_specs=[pl.BlockSpec((tm, DP), row),
                  pl.BlockSpec((16, DP), lambda i: (jnp.minimum((i + 1) * hb, s // 16 - 1), 0)),
                  pl.BlockSpec((tm, DP), row),
                  pl.BlockSpec((4, PG, PG), lambda i: (0, 0, 0)),
                  pl.BlockSpec((1, DP), lambda i: (0, 0))],
        out_specs=[pl.BlockSpec((tm, DP), row), pl.BlockSpec((4, PG, PG), lambda i: (0, 0, 0)),
                   pl.BlockSpec((8, DP), lambda i: (0, 0))],
        out_shape=[jax.ShapeDtypeStruct((s, DP), BF16), jax.ShapeDtypeStruct((4, PG, PG), F32),
                   jax.ShapeDtypeStruct((8, DP), F32)],
        scratch_shapes=[pltpu.VMEM((tm + 16, DP), F32)],
        compiler_params=_params("arbitrary"),
    )(dyp, dyp, p, pw, pscale)


def _attn_bwd(proj0, ysb, dys):
    s = proj0.shape[0]
    nq = s // BQ
    kpq = BQ // BK

    def body(q_ref, k_ref, v_ref, o_ref, do_ref, dq_ref, dk_ref, dv_ref,
             qn_ref, k8_ref, vb_ref, dob_ref, dka_ref, dva_ref):
        qn_ref[...] = (-q_ref[...]).astype(BF16)
        k8_ref[...] = (k_ref[...] * INV_SQRT_HD).astype(BF16)
        vb_ref[...] = v_ref[...].astype(BF16)
        dob_ref[...] = do_ref[...].astype(BF16)
        dka_ref[...] = jnp.zeros_like(dka_ref)
        dva_ref[...] = jnp.zeros_like(dva_ref)
        after, from_on = _tri_masks()
        heads = [slice(HD * h, HD * (h + 1)) for h in range(2)]

        def qstep(qi, carry):
            q0 = pl.multiple_of(qi * BQ, BQ)
            qn = [qn_ref[pl.ds(q0, BQ), ls] for ls in heads]
            do = [dob_ref[pl.ds(q0, BQ), ls] for ls in heads]
            total = [jnp.sum(do[h].astype(F32) * o_ref[pl.ds(q0, BQ), ls], axis=1, keepdims=True)
                     for h, ls in enumerate(heads)]

            def blocks(k_hi, state, masks):
                ks = [pl.multiple_of(k_hi + (kpq - 1 - j) * BK, BK) for j in range(kpq)]
                lanes = [(j, h) for j in range(kpq) for h in range(len(heads))]
                mask = lambda j: None if masks is None else masks[j]
                k8 = {jh: k8_ref[pl.ds(ks[jh[0]], BK), heads[jh[1]]] for jh in lanes}
                nz = {jh: _dot_nt(qn[jh[1]], k8[jh]) for jh in lanes}
                da = {jh: _dot_nt(do[jh[1]], vb_ref[pl.ds(ks[jh[0]], BK), heads[jh[1]]]) for jh in lanes}
                ll = {jh: _sb_logits(nz[jh], mask(jh[0])) for jh in lanes}
                aft = {jh: _dot(ll[jh][0].astype(BF16), after) for jh in lanes}
                new = []
                for h in range(len(heads)):
                    dq_acc, c, cg = state[h]
                    for j in range(kpq):
                        a = jnp.exp(ll[j, h][1] + aft[j, h] + c)
                        if masks is not None:
                            a = jnp.where(masks[j], a, 0.0)
                        ab = a.astype(BF16)
                        g = da[j, h] * ab.astype(F32)
                        suf = _split_dot(g, from_on)
                        dz = g - jnp.exp(ll[j, h][1]) * (g + ((total[h] - cg) - suf))
                        if masks is not None:
                            dz = jnp.where(masks[j], dz, 0.0)
                        dzb = dz.astype(BF16)
                        dka_ref[pl.ds(ks[j], BK), heads[h]] += _dot_tn(dzb, qn[h])
                        dva_ref[pl.ds(ks[j], BK), heads[h]] += _dot_tn(ab, do[h])
                        dq_acc = dq_acc + _dot(dzb, k8[j, h])
                        c = c + aft[j, h][:, 0:1] + ll[j, h][0][:, 0:1]
                        cg = cg + suf[:, 0:1]
                    new.append((dq_acc, c, cg))
                return tuple(new)

            zero = jnp.zeros((BQ, 1), F32)
            state = tuple((jnp.zeros((BQ, HD), F32), zero, zero) for _ in heads)
            state = blocks(q0, state, [_causal_mask((kpq - 1 - j) * BK) for j in range(kpq)])
            state = lax.fori_loop(0, qi, lambda j, st: blocks(q0 - (j + 1) * BQ, st, None), state)
            for h, ls in enumerate(heads):
                dq_ref[pl.ds(q0, BQ), ls] = state[h][0].astype(BF16)
            return carry

        lax.fori_loop(0, nq, qstep, 0)
        dk_ref[...] = (dka_ref[...] * (-INV_SQRT_HD)).astype(BF16)
        dv_ref[...] = dva_ref[...].astype(BF16)

    col = lambda h: (0, h)
    return pl.pallas_call(
        body, name="attn_bwd", grid=(DP // 128,),
        in_specs=[pl.BlockSpec((s, 128), lambda h: (0, 8 + h)),
                  pl.BlockSpec((s, 128), lambda h: (0, 16 + h)),
                  pl.BlockSpec((s, 128), lambda h: (0, 24 + h)),
                  pl.BlockSpec((s, 128), col), pl.BlockSpec((s, 128), col)],
        out_specs=[pl.BlockSpec((s, 128), col)] * 3,
        out_shape=[jax.ShapeDtypeStruct((s, DP), BF16)] * 3,
        scratch_shapes=[pltpu.VMEM((s, 128), BF16)] * 4 + [pltpu.VMEM((s, 128), F32)] * 2,
        compiler_params=_params("parallel"),
    )(proj0, proj0, proj0, ysb, dys)


def _adamw_math(w, g, m, v):
    m2 = B1 * m + (1.0 - B1) * g
    v2 = B2 * v + (1.0 - B2) * (g * g)
    m_hat = m2 / (1.0 - B1 ** STEP)
    v_hat = v2 / (1.0 - B2 ** STEP)
    return -LR * (m_hat / (jnp.sqrt(v_hat) + EPS_ADAM) + WD * w), m2, v2


def _adamw(w, g, m, v, name):
    r, c = w.shape
    tr = r
    while tr * c * 4 > (1 << 20) and tr % 16 == 0:
        tr //= 2

    def body(w_ref, g_ref, m_ref, v_ref, d_ref, m2_ref, v2_ref):
        d_ref[...], m2_ref[...], v2_ref[...] = _adamw_math(w_ref[...], g_ref[...], m_ref[...], v_ref[...])

    spec = pl.BlockSpec((tr, c), lambda i: (i, 0))
    return pl.pallas_call(
        body, name=name, grid=(r // tr,),
        in_specs=[spec] * 4, out_specs=[spec] * 3,
        out_shape=[jax.ShapeDtypeStruct((r, c), F32)] * 3,
        compiler_params=_params("parallel"),
    )(w, g, m, v)


def _local_step(x, target, vecs0, vecs1, win0, pw, pscale, wout0, layer1, cw8, cb, sel=None):
    if sel is None:
        (win1, wout1), fetch = layer1, None
    else:
        fetch = _gather_comm(list(layer1))
    h0 = _norm_mod(x, vecs0, "norm0")
    proj0, *fetched = _inproj(h0, win0, "inproj0", fetch)
    if sel is not None:
        win1, wout1 = fetched[0], fetched[1].reshape(DI, D)
    p, ypool = _pool_fwd(proj0, pw, pscale)
    ysb = _attn_fwd(proj0)
    x1, out0, yg, h1 = _even_out(ypool, ysb, proj0, wout0, x, vecs0, vecs1)
    proj1, = _inproj(h1, win1, "inproj1")
    dx2, out1, y1, acc_f = _odd_out(proj1, wout1, x1, vecs1, cw8, cb, target)

    def chip_partials(grads, names):
        from_sibling = _send_halves(grads, "rs_send_halves_" + names[0])
        part = [_add_halves(g, t, sel[1:2], "rs_add_halves_" + nm) for g, t, nm in zip(grads, from_sibling, names)]
        return [p32 for p32, _ in part], _exchange_comm([p16 for _, p16 in part])

    dout1, dproj1, acc_cv, acc_g1 = _odd_bwd(dx2, out1, proj1, wout1, vecs1, cw8, cb)
    g_wout1 = _grad_w_rows(y1, dout1, "grad_wout1")
    g_win1 = _grad_w_cols(h1, dproj1, "grad_win1")
    part1, swap1 = chip_partials([g_win1, g_wout1], ["win1", "wout1"]) if sel is not None else (None, None)
    dx1, acc_n1, *got1 = _inproj_bwd(dproj1, win1, x1, dx2, vecs1, "inproj1_bwd", swap1)

    dout0, dyp, dys, dgate0, acc_g0 = _even_bwd(dx1, out0, ypool, ysb, proj0, wout0, vecs0)
    g_wout0 = _grad_w_rows(yg, dout0, "grad_wout0")
    du, g_pw, acc_ps = _pool_bwd(dyp, p, pw, pscale)
    dq, dk, dv = _attn_bwd(proj0, ysb, dys)
    dproj0 = jnp.concatenate([du, dq, dk, dv, dgate0], axis=1)
    g_win0 = _grad_w_cols(h0, dproj0, "grad_win0")
    layer0 = [g_win0, _chip_major(g_pw, PG // NCHIP), g_wout0]
    part0, swap0 = chip_partials(layer0, ["win0", "pool", "wout0"]) if sel is not None else (None, None)
    dx0, acc_n0, *got0 = _inproj_bwd(dproj0, win0, x, dx1, vecs0, "inproj0_bwd", swap0)

    if sel is None:
        grads = (g_win0, g_pw, g_wout0, g_win1, g_wout1)
    else:
        names = ["win0", "pool", "wout0", "win1", "wout1"]
        halves = [_add_partials(p32, t, sel, "rs_add_partials_" + nm)
                  for p32, t, nm in zip(part0 + part1, got0 + got1, names)]
        grads = tuple(_join_halves(halves))

    sums = dict(
        dm0=jnp.concatenate([acc_n0[0:2], acc_g0[0:1]], axis=0),
        dm1=jnp.concatenate([acc_n1[0:2], acc_g1[0:1]], axis=0),
        norm_g=jnp.concatenate([acc_n0[2:3], acc_n1[2:3]], axis=0),
        pool_scale=acc_ps[0:1], final_g=acc_f[0:1], loss=acc_f[1:2],
        conv_w=acc_cv[0:3], conv_b=acc_cv[3:4])
    return dx0, grads, sums


ANY = pl.BlockSpec(memory_space=pl.ANY)
CHIP_FLIPS = ((1, 0), (0, 1), (1, 1))


def _place():
    return lax.axis_index("x"), lax.axis_index("y"), lax.axis_index("c")


def _flip(v, f):
    return 1 - v if f else v


def _allgather8(v, name):
    m_per, n = v.shape

    def body(x_ref, out_ref, send_sems, recv_sems, local_sem):
        x, y, c = _place()
        me, sibling = (x, y, c), (x, y, 1 - c)
        chips = [(_flip(x, fx), _flip(y, fy)) for fx, fy in CHIP_FLIPS]

        def rows(px, py, pc):
            return out_ref.at[pl.ds((4 * px + 2 * py + pc) * m_per, m_per), :]

        def copy(k, block, to, src=None):
            return pltpu.make_async_remote_copy(
                src_ref=rows(*block) if src is None else src, dst_ref=rows(*block),
                send_sem=send_sems.at[k], recv_sem=recv_sems.at[k], device_id=to, device_id_type=MESH)

        mine = pltpu.make_async_copy(x_ref, rows(*me), local_sem)
        mine.start()
        first = [copy(0, me, sibling, src=x_ref)]
        first += [copy(1 + j, me, (*chip, c), src=x_ref) for j, chip in enumerate(chips)]
        for cp in first:
            cp.start()
        passed = [copy(4 + j, (*chip, c), sibling) for j, chip in enumerate(chips)]
        for j, chip in enumerate(chips):
            copy(1 + j, (*chip, c), me).wait_recv()
            passed[j].start()
        copy(0, sibling, me).wait_recv()
        for j, chip in enumerate(chips):
            copy(4 + j, (*chip, 1 - c), me).wait_recv()
        for cp in first + passed:
            cp.wait_send()
        mine.wait()

    return pl.pallas_call(
        body, name=name,
        out_shape=jax.ShapeDtypeStruct((NDEV * m_per, n), v.dtype),
        in_specs=[pl.BlockSpec(memory_space=pltpu.VMEM)],
        out_specs=pl.BlockSpec(memory_space=pltpu.VMEM),
        scratch_shapes=[pltpu.SemaphoreType.DMA((7,)), pltpu.SemaphoreType.DMA((7,)), pltpu.SemaphoreType.DMA],
    )(v)


class _Comm:
    def __init__(self, arrays, out_shape, sems, start, finish):
        self.arrays, self.out_shape, self.sems, self.start, self.finish = arrays, out_shape, sems, start, finish


_NO_COMM = _Comm([], [], [], None, None)


def _run_comm(comm, name):
    n = len(comm.arrays)

    def body(*refs):
        args = (refs[:n], refs[n:n + len(comm.out_shape)], refs[n + len(comm.out_shape):])
        comm.start(*args)
        comm.finish(*args)

    return pl.pallas_call(
        body, name=name, out_shape=comm.out_shape,
        in_specs=[ANY] * n, out_specs=[ANY] * len(comm.out_shape), scratch_shapes=comm.sems,
    )(*comm.arrays)


def _gather_comm(shards):
    n = len(shards)

    def pieces(ins, outs, sems, kinds):
        x, y, c = _place()
        ici_send, ici_recv, fwd_send, fwd_recv, local = sems
        own, sibling = 2 * x + y, (x, y, 1 - c)
        made = {kind: [] for kind in kinds}
        for w in range(n):
            r2 = ins[w].shape[0] // 2
            mine, other = pl.ds(c * r2, r2), pl.ds((1 - c) * r2, r2)
            if "local" in kinds:
                made["local"].append(pltpu.make_async_copy(ins[w], outs[w].at[own], local.at[w]))
            for d, (fx, fy) in enumerate(CHIP_FLIPS):
                px, py, k = _flip(x, fx), _flip(y, fy), 3 * w + d
                peer = 2 * px + py
                ici = dict(send_sem=ici_send.at[k], recv_sem=ici_recv.at[k], device_id=(px, py, c), device_id_type=MESH)
                fwd = dict(send_sem=fwd_send.at[k], recv_sem=fwd_recv.at[k], device_id=sibling, device_id_type=MESH)
                if "ici_out" in kinds:
                    made["ici_out"].append(pltpu.make_async_remote_copy(
                        src_ref=ins[w].at[mine, :], dst_ref=outs[w].at[own, mine, :], **ici))
                if "ici_in" in kinds:
                    made["ici_in"].append(pltpu.make_async_remote_copy(
                        src_ref=ins[w].at[mine, :], dst_ref=outs[w].at[peer, mine, :], **ici))
                if "fwd_out" in kinds:
                    made["fwd_out"].append(pltpu.make_async_remote_copy(
                        src_ref=outs[w].at[peer, mine, :], dst_ref=outs[w].at[peer, mine, :], **fwd))
                if "fwd_in" in kinds:
                    made["fwd_in"].append(pltpu.make_async_remote_copy(
                        src_ref=outs[w].at[peer, other, :], dst_ref=outs[w].at[peer, other, :], **fwd))
        return made

    def start(ins, outs, sems):
        made = pieces(ins, outs, sems, ("local", "ici_out"))
        for cp in made["local"] + made["ici_out"]:
            cp.start()

    def finish(ins, outs, sems):
        made = pieces(ins, outs, sems, ("local", "ici_out", "ici_in", "fwd_out", "fwd_in"))
        for arrived, onward in zip(made["ici_in"], made["fwd_out"]):
            arrived.wait_recv()
            onward.start()
        for cp in made["fwd_in"]:
            cp.wait_recv()
        for cp in made["ici_out"] + made["fwd_out"]:
            cp.wait_send()
        for cp in made["local"]:
            cp.wait()

    dma = pltpu.SemaphoreType.DMA
    return _Comm(list(shards), [jax.ShapeDtypeStruct((NCHIP,) + a.shape, a.dtype) for a in shards],
                 [dma((3 * n,))] * 4 + [dma((n,))], start, finish)


def _exchange_comm(parts):
    n = len(parts)

    def copies(ins, outs, sems):
        x, y, c = _place()
        send_sems, recv_sems = sems
        out = []
        for w in range(n):
            for d, (fx, fy) in enumerate(CHIP_FLIPS):
                px, py = _flip(x, fx), _flip(y, fy)
                out.append(pltpu.make_async_remote_copy(
                    src_ref=ins[w].at[2 * px + py], dst_ref=outs[w].at[d], send_sem=send_sems.at[3 * w + d],
                    recv_sem=recv_sems.at[3 * w + d], device_id=(px, py, c), device_id_type=MESH))
        return out

    def start(ins, outs, sems):
        for cp in copies(ins, outs, sems):
            cp.start()

    def finish(ins, outs, sems):
        cps = copies(ins, outs, sems)
        for cp in cps:
            cp.wait_recv()
        for cp in cps:
            cp.wait_send()

    dma = pltpu.SemaphoreType.DMA
    return _Comm(list(parts), [jax.ShapeDtypeStruct((3,) + p.shape[1:], BF16) for p in parts],
                 [dma((3 * n,))] * 2, start, finish)


def _send_halves(grads, name):
    n = len(grads)

    def body(*refs):
        ins, outs = refs[:n], refs[n:2 * n]
        send_sems, recv_sems = refs[2 * n:]
        x, y, c = _place()
        copies = []
        for w in range(n):
            r2 = ins[w].shape[1] // 2
            cp = pltpu.make_async_remote_copy(
                src_ref=ins[w].at[:, pl.ds((1 - c) * r2, r2), :], dst_ref=outs[w],
                send_sem=send_sems.at[w], recv_sem=recv_sems.at[w], device_id=(x, y, 1 - c), device_id_type=MESH)
            cp.start()
            copies.append(cp)
        for cp in copies:
            cp.wait_recv()
        for cp in copies:
            cp.wait_send()

    return pl.pallas_call(
        body, name=name,
        out_shape=[jax.ShapeDtypeStruct((NCHIP, g.shape[1] // 2, g.shape[2]), F32) for g in grads],
        in_specs=[ANY] * n, out_specs=[ANY] * n,
        scratch_shapes=[pltpu.SemaphoreType.DMA((n,)), pltpu.SemaphoreType.DMA((n,))],
    )(*grads)


def _row_tile(rows, cols):
    tr = rows
    while tr * cols * 4 > (1 << 20) and tr % 16 == 0:
        tr //= 2
    return tr


def _add_halves(g, t1, core, name):
    _, r, cdim = g.shape
    r2 = r // 2
    tr = _row_tile(r2, cdim)
    nt = r2 // tr

    def body(core_ref, g_ref, t_ref, p_ref, pb_ref):
        p = g_ref[...] + t_ref[...]
        p_ref[...] = p
        pb_ref[...] = p.astype(BF16)

    blk = pl.BlockSpec((None, tr, cdim), lambda j, i, core_ref: (j, i, 0))
    return pl.pallas_call(
        body, name=name,
        grid_spec=pltpu.PrefetchScalarGridSpec(
            num_scalar_prefetch=1, grid=(NCHIP, nt),
            in_specs=[pl.BlockSpec((None, tr, cdim), lambda j, i, core_ref: (j, core_ref[0] * nt + i, 0)), blk],
            out_specs=[blk, blk]),
        out_shape=[jax.ShapeDtypeStruct((NCHIP, r2, cdim), F32), jax.ShapeDtypeStruct((NCHIP, r2, cdim), BF16)],
        compiler_params=_params("parallel", "parallel"),
    )(core, g, t1)


def _add_partials(p, t2, sel, name):
    _, r2, cdim = p.shape
    tr = _row_tile(r2, cdim)
    nt = r2 // tr

    def body(sel_ref, p_ref, t_ref, o_ref):
        o_ref[...] = ((p_ref[...] + t_ref[0].astype(F32)) + t_ref[1].astype(F32)) + t_ref[2].astype(F32)

    return pl.pallas_call(
        body, name=name,
        grid_spec=pltpu.PrefetchScalarGridSpec(
            num_scalar_prefetch=1, grid=(nt,),
            in_specs=[pl.BlockSpec((None, tr, cdim), lambda i, sel_ref: (sel_ref[0], i, 0)),
                      pl.BlockSpec((3, tr, cdim), lambda i, sel_ref: (0, i, 0))],
            out_specs=pl.BlockSpec((tr, cdim), lambda i, sel_ref: (sel_ref[1] * nt + i, 0))),
        out_shape=jax.ShapeDtypeStruct((2 * r2, cdim), F32),
        compiler_params=_params("parallel"),
    )(sel, p, t2)


def _join_halves(grads):
    n = len(grads)

    def body(*refs):
        bufs = refs[n:2 * n]
        send_sems, recv_sems = refs[2 * n:]
        x, y, c = _place()
        copies = []
        for w in range(n):
            r2 = bufs[w].shape[0] // 2
            mine = bufs[w].at[pl.ds(c * r2, r2), :]
            cp = pltpu.make_async_remote_copy(
                src_ref=mine, dst_ref=mine, send_sem=send_sems.at[w], recv_sem=recv_sems.at[w],
                device_id=(x, y, 1 - c), device_id_type=MESH)
            cp.start()
            copies.append(cp)
        for w in range(n):
            r2 = bufs[w].shape[0] // 2
            theirs = bufs[w].at[pl.ds((1 - c) * r2, r2), :]
            pltpu.make_async_remote_copy(
                src_ref=theirs, dst_ref=theirs, send_sem=send_sems.at[w], recv_sem=recv_sems.at[w],
                device_id=(x, y, 1 - c), device_id_type=MESH).wait_recv()
        for cp in copies:
            cp.wait_send()

    return pl.pallas_call(
        body, name="rs_join_halves",
        out_shape=[jax.ShapeDtypeStruct(g.shape, F32) for g in grads],
        in_specs=[ANY] * n, out_specs=[ANY] * n, input_output_aliases={w: w for w in range(n)},
        scratch_shapes=[pltpu.SemaphoreType.DMA((n,)), pltpu.SemaphoreType.DMA((n,))],
    )(*grads)


def _ada_fwd(c_all, ada_w):
    nl, _, ns = ada_w.shape

    def body(c_ref, w_ref, o_ref):
        cv = c_ref[...]
        o_ref[...] = _dot((cv * jax.nn.sigmoid(cv)).astype(BF16), w_ref[...].astype(BF16))

    return pl.pallas_call(
        body, name="ada_fwd", grid=(nl,),
        in_specs=[pl.BlockSpec((NDEV, D), lambda i: (0, 0)), pl.BlockSpec((None, D, ns), lambda i: (i, 0, 0))],
        out_specs=pl.BlockSpec((None, NDEV, ns), lambda i: (i, 0, 0)),
        out_shape=jax.ShapeDtypeStruct((nl, NDEV, ns), F32),
        compiler_params=_params("parallel"),
    )(c_all, ada_w)


PACK_ROWS = 24


def _reduce_packed(gathered):
    def body(g_ref, tot_ref, loss_ref):
        tot = g_ref[0:PACK_ROWS, :]
        for dev in range(1, NDEV):
            tot = tot + g_ref[dev * PACK_ROWS:(dev + 1) * PACK_ROWS, :]
        tot_ref[...] = tot
        loss_ref[...] = jnp.zeros((8, 128), F32) + jnp.sum(tot[10:11, :])

    return pl.pallas_call(
        body, name="reduce_packed",
        out_shape=[jax.ShapeDtypeStruct((PACK_ROWS, D), F32), jax.ShapeDtypeStruct((8, 128), F32)],
    )(gathered)


def _ada_w_update(c_t, dms, w, m, v):
    nl, _, ns = w.shape
    tr = 256

    def body(ct_ref, dm_ref, w_ref, m_ref, v_ref, g_ref, d_ref, m2_ref, v2_ref):
        ct = ct_ref[...]
        sc = ct * jax.nn.sigmoid(ct)
        dm = dm_ref[...]
        g = sc[:, 0:1] * dm[0:1, :]
        for b in range(1, NDEV):
            g = g + sc[:, b:b + 1] * dm[b:b + 1, :]
        g_ref[...] = g
        d_ref[...], m2_ref[...], v2_ref[...] = _adamw_math(w_ref[...], g, m_ref[...], v_ref[...])

    blk = pl.BlockSpec((None, tr, ns), lambda i, j: (i, j, 0))
    return pl.pallas_call(
        body, name="ada_w_update", grid=(nl, D // tr),
        in_specs=[pl.BlockSpec((tr, NDEV), lambda i, j: (j, 0)),
                  pl.BlockSpec((None, NDEV, ns), lambda i, j: (i, 0, 0)), blk, blk, blk],
        out_specs=[blk] * 4,
        out_shape=[jax.ShapeDtypeStruct((nl, D, ns), F32)] * 4,
        compiler_params=_params("parallel", "parallel"),
    )(c_t, dms, w, m, v)


def _chip_major(a, parts):
    g, _, cdim = a.shape
    return jnp.transpose(a.reshape(g, NCHIP, parts, cdim), (1, 0, 2, 3)).reshape(NCHIP, g * parts, cdim)


def kernel(x, c, norm_g, ada_w, ada_b, even_w_in, pool_w, pool_scale, even_w_out, odd_w_in, conv_w, conv_b, odd_w_out, final_g, loss_target, m_norm_g, m_ada_w, m_ada_b, m_even_w_in, m_pool_w, m_pool_scale, m_even_w_out, m_odd_w_in, m_conv_w, m_conv_b, m_odd_w_out, m_final_g, v_norm_g, v_ada_w, v_ada_b, v_even_w_in, v_pool_w, v_pool_scale, v_even_w_out, v_odd_w_in, v_conv_w, v_conv_b, v_odd_w_out, v_final_g):
    ix, iy, ic = _place()
    chip = 2 * ix + iy
    batch = 2 * chip + ic
    sel = jnp.stack([chip, ic]).astype(jnp.int32)
    ns_ada = ada_w.shape[2]
    ns_conv = conv_b.shape[1]

    conv_rows = jnp.pad(jnp.concatenate([conv_w[0], conv_b], axis=0), ((0, 3), (0, D - ns_conv)))
    first = _allgather8(jnp.concatenate([c, conv_rows], axis=0), "gather_c_conv").reshape(NCHIP, 2, 8, D)
    c_all = first[:, :, 0].reshape(NDEV, D)
    cw_full = jnp.transpose(first[:, 0, 1:5, 0:ns_conv], (1, 0, 2)).reshape(4, DI)
    cw8 = jnp.concatenate([cw_full[0:3], jnp.zeros((5, DI), F32)], axis=0)
    cb_full = cw_full[3:4]

    m_cols = _allgather8(_ada_fwd(c_all, ada_w).reshape(2 * NDEV, ns_ada), "gather_ada")
    m_cols = m_cols.reshape(NCHIP, 2, 2, NDEV, ns_ada)[:, 0]
    m_mine = lax.dynamic_index_in_dim(m_cols, batch, axis=2, keepdims=False)
    m_mine = jnp.transpose(m_mine, (1, 0, 2)).reshape(2, 3 * D) + ada_b
    zrow = jnp.zeros((3, D), F32)

    def vec_rows(i):
        sh, sc, gt = m_mine[i, 0:D], m_mine[i, D:2 * D], m_mine[i, 2 * D:3 * D]
        return jnp.concatenate([jnp.stack([norm_g[i], sc, sh, gt, final_g]), zrow], axis=0)

    win0, pw, wout0 = _run_comm(_gather_comm([
        even_w_in[0].astype(BF16), pool_w[0].astype(BF16).reshape(PG, PG), even_w_out[0].astype(BF16)]),
        "gather_even_weights")
    pw_full = jnp.transpose(pw.reshape(NCHIP, 4, PG // NCHIP, PG), (1, 0, 2, 3)).reshape(4, PG, PG)

    dx0, grads, sums = _local_step(
        x[0], loss_target[0], vec_rows(0), vec_rows(1), win0, pw_full, pool_scale, wout0.reshape(DI, D),
        (odd_w_in[0].astype(BF16), odd_w_out[0].astype(BF16)), cw8, cb_full, sel)
    r_win0, r_pw, r_wout0, r_win1, r_wout1 = grads

    packed = jnp.concatenate([
        sums["dm0"], sums["dm1"], sums["norm_g"], sums["pool_scale"], sums["final_g"], sums["loss"],
        sums["conv_w"].reshape(6, D), sums["conv_b"].reshape(2, D), jnp.zeros((PACK_ROWS - 19, D), F32)], axis=0)
    gathered = _allgather8(packed, "gather_sums")
    tot, loss8 = _reduce_packed(gathered)
    loss = loss8[0, 0]
    g_norm_g, g_pool_scale, g_final_g = tot[6:8], tot[8:9], tot[9]
    g_ada_b = tot[0:6].reshape(2, 3 * D)
    g_conv_w = lax.dynamic_slice_in_dim(tot[11:17].reshape(3, DI), chip * (DI // NCHIP), DI // NCHIP, axis=1)
    g_conv_b = lax.dynamic_slice_in_dim(tot[17:19].reshape(1, DI), chip * (DI // NCHIP), DI // NCHIP, axis=1)
    dm_all = gathered.reshape(NDEV, PACK_ROWS, D)[:, 0:6].reshape(NDEV, 2, 3 * D)
    dm_cols = jnp.transpose(lax.dynamic_slice_in_dim(dm_all, chip * ns_ada, ns_ada, axis=2), (1, 0, 2))
    g_ada_w, d_ada_w, nm_ada_w, nv_ada_w = _ada_w_update(jnp.transpose(c_all), dm_cols, ada_w, m_ada_w, v_ada_w)

    def upd(w, g, m, v, name):
        shape = w.shape
        w2, m2, v2 = (a.reshape(g.shape) for a in (w, m, v))
        d, nm, nv = _adamw(w2, g, m2, v2, name)
        return g.reshape(shape), d.reshape(shape), nm.reshape(shape), nv.reshape(shape)

    o_win0 = upd(even_w_in, r_win0, m_even_w_in, v_even_w_in, "adamw_win0")
    o_pw = upd(pool_w, r_pw, m_pool_w, v_pool_w, "adamw_pool")
    o_wout0 = upd(even_w_out, r_wout0, m_even_w_out, v_even_w_out, "adamw_wout0")
    o_win1 = upd(odd_w_in, r_win1, m_odd_w_in, v_odd_w_in, "adamw_win1")
    o_wout1 = upd(odd_w_out, r_wout1, m_odd_w_out, v_odd_w_out, "adamw_wout1")

    def pack_small(ng, ab, ps, fg, cwv, cbv):
        conv = jnp.concatenate([cwv.reshape(3, -1), cbv.reshape(1, -1)], axis=0).reshape(2, D)
        return jnp.concatenate([ng, ab.reshape(6, D), ps, fg.reshape(1, D), conv, jnp.zeros((4, D), F32)], axis=0)

    sw = pack_small(norm_g, ada_b, pool_scale, final_g, conv_w, conv_b)
    sg = pack_small(g_norm_g, g_ada_b, g_pool_scale, g_final_g, g_conv_w, g_conv_b)
    sm = pack_small(m_norm_g, m_ada_b, m_pool_scale, m_final_g, m_conv_w, m_conv_b)
    sv = pack_small(v_norm_g, v_ada_b, v_pool_scale, v_final_g, v_conv_w, v_conv_b) + jnp.concatenate(
        [jnp.zeros((12, D), F32), jnp.ones((4, D), F32)], axis=0)
    small = _adamw(sw, sg, sm, sv, "adamw_small")

    def unpack_small(a):
        conv = a[10:12].reshape(4, -1)
        return dict(norm_g=a[0:2], ada_b=a[2:8].reshape(2, 3 * D), pool_scale=a[8:9], final_g=a[9],
                    conv_w=conv[0:3].reshape(conv_w.shape), conv_b=conv[3:4].reshape(conv_b.shape))

    s_grad = dict(norm_g=g_norm_g, ada_b=g_ada_b, pool_scale=g_pool_scale, final_g=g_final_g,
                  conv_w=g_conv_w.reshape(conv_w.shape), conv_b=g_conv_b.reshape(conv_b.shape))
    s_out = [s_grad] + [unpack_small(a) for a in small]

    outs = []
    for k in range(4):
        sm_k = s_out[k]
        outs.append([sm_k["norm_g"], (g_ada_w, d_ada_w, nm_ada_w, nv_ada_w)[k], sm_k["ada_b"], o_win0[k], o_pw[k],
                     sm_k["pool_scale"], o_wout0[k], o_win1[k], sm_k["conv_w"], sm_k["conv_b"], o_wout1[k],
                     sm_k["final_g"]])
    return (loss, dx0[None], *outs[0], *outs[1], *outs[2], *outs[3])
```

```python
import functools

import jax
import jax.numpy as jnp
from jax import lax
from jax.experimental import pallas as pl
from jax.experimental.pallas import tpu as pltpu

F32 = jnp.float32
BF16 = jnp.bfloat16
MESH = pl.DeviceIdType.MESH

D = 1024
DI = 2048
DP = 1024
NE = 6144
NO = 8192
WINDOWS = (2, 4, 8, 16)
PG = 256
HD = 64
NCHIP = 4
NDEV = 8
EPS = 1e-6
INV_SQRT_HD = 0.125

LR, B1, B2, EPS_ADAM, WD, STEP = 0.001, 0.9, 0.999, 1e-08, 0.01, 10

TM = 512
TME = 256
CT = 512
BQ = 512
BK = 256
HALO = 16
VMEM_LIMIT = 56 * 1024 * 1024


def _dot(a, b):
    return jnp.dot(a, b, preferred_element_type=F32)


def _dot_nt(a, b):
    return lax.dot_general(a, b, (((1,), (1,)), ((), ())), preferred_element_type=F32)


def _dot_tn(a, b):
    return lax.dot_general(a, b, (((0,), (0,)), ((), ())), preferred_element_type=F32)


def _params(*sem):
    return pltpu.CompilerParams(dimension_semantics=sem, vmem_limit_bytes=VMEM_LIMIT)


def _rowsum(v):
    return jnp.sum(v, axis=0, keepdims=True)


def _modulated_norm(xv, vec_ref):
    r = lax.rsqrt(jnp.mean(xv * xv, axis=-1, keepdims=True) + EPS)
    return (((xv * r) * vec_ref[0:1, :]) * (1.0 + vec_ref[1:2, :]) + vec_ref[2:3, :]).astype(BF16)


def _norm_mod(x, vecs, name):
    s = x.shape[0]
    tm = min(TM, s)

    def body(x_ref, vec_ref, h_ref):
        h_ref[...] = _modulated_norm(x_ref[...], vec_ref)

    return pl.pallas_call(
        body, name=name, grid=(s // tm,),
        in_specs=[pl.BlockSpec((tm, D), lambda i: (i, 0)), pl.BlockSpec((8, D), lambda i: (0, 0))],
        out_specs=pl.BlockSpec((tm, D), lambda i: (i, 0)),
        out_shape=jax.ShapeDtypeStruct((s, D), BF16),
        compiler_params=_params("parallel"),
    )(x, vecs)


def _inproj(h, w, name, comm=None):
    s = h.shape[0]
    ns = w.shape[2]
    tm = min(TM, s)
    ni = s // tm
    comm = comm or _NO_COMM
    nci, nco = len(comm.arrays), len(comm.out_shape)

    def body(*refs):
        h_ref, w_ref = refs[:2]
        proj_ref = refs[2 + nci]
        cargs = (refs[2:2 + nci], refs[3 + nci:3 + nci + nco], refs[3 + nci + nco:])
        j, i = pl.program_id(0), pl.program_id(1)
        if nci:
            pl.when((j == 0) & (i == 0))(lambda: comm.start(*cargs))
        proj_ref[...] = _dot(h_ref[...], w_ref[...]).astype(BF16)
        if nci:
            pl.when((j == NCHIP - 1) & (i == ni - 1))(lambda: comm.finish(*cargs))

    return pl.pallas_call(
        body, name=name, grid=(NCHIP, ni),
        in_specs=[pl.BlockSpec((tm, D), lambda j, i: (i, 0)),
                  pl.BlockSpec((None, D, ns), lambda j, i: (j, 0, 0))] + [ANY] * nci,
        out_specs=[pl.BlockSpec((tm, ns), lambda j, i: (i, j))] + [ANY] * nco,
        out_shape=[jax.ShapeDtypeStruct((s, NCHIP * ns), BF16)] + comm.out_shape,
        scratch_shapes=comm.sems,
        compiler_params=_params("arbitrary", "arbitrary"),
    )(h, w, *comm.arrays)


def _pool_fwd(proj0, pw, pscale):
    s = proj0.shape[0]
    tm = min(TM, s)
    hb = tm // 16

    def body(u_ref, halo_ref, w_ref, sc_ref, p_ref, y_ref, ext_ref):
        i = pl.program_id(0)
        ext_ref[16:, :] = u_ref[...].astype(F32)
        ext_ref[0:16, :] = jnp.where(i > 0, halo_ref[...].astype(F32), 0.0)
        t = i * tm + lax.broadcasted_iota(jnp.int32, (tm, 1), 0)
        for g, wdw in enumerate(WINDOWS):
            cs = slice(g * PG, (g + 1) * PG)
            u = ext_ref[16:16 + tm, cs]
            acc = u
            for j in range(1, wdw):
                acc = acc + ext_ref[16 - j:16 - j + tm, cs]
            inv = 1.0 / jnp.minimum(t + 1, wdw).astype(F32)
            pb = (acc * inv - u).astype(BF16)
            p_ref[:, cs] = pb
            y_ref[:, cs] = _dot(pb, w_ref[g]) * sc_ref[:, cs]

    return pl.pallas_call(
        body, name="pool_fwd", grid=(s // tm,),
        in_specs=[pl.BlockSpec((tm, DP), lambda i: (i, 0)),
                  pl.BlockSpec((16, DP), lambda i: (jnp.maximum(i * hb - 1, 0), 0)),
                  pl.BlockSpec((4, PG, PG), lambda i: (0, 0, 0)),
                  pl.BlockSpec((1, DP), lambda i: (0, 0))],
        out_specs=[pl.BlockSpec((tm, DP), lambda i: (i, 0)),
                   pl.BlockSpec((tm, DP), lambda i: (i, 0))],
        out_shape=[jax.ShapeDtypeStruct((s, DP), BF16), jax.ShapeDtypeStruct((s, DP), F32)],
        scratch_shapes=[pltpu.VMEM((tm + 16, DP), F32)],
        compiler_params=_params("parallel"),
    )(proj0, proj0, pw, pscale)


def _sb_logits(nz, mask):
    neg_abs = lax.bitcast_convert_type(lax.bitcast_convert_type(nz, jnp.uint32) | jnp.uint32(0x80000000), F32)
    t = jnp.log(1.0 + jnp.exp(neg_abs))
    lf = jnp.minimum(nz, 0.0) - t
    lam = lf - nz
    if mask is not None:
        lf = jnp.where(mask, lf, 0.0)
    return lf, lam


def _split_dot(v, tri):
    hi = v.astype(BF16)
    lo = (v - hi.astype(F32)).astype(BF16)
    return _dot(hi, tri) + _dot(lo, tri)


def _tri_masks():
    row = lax.broadcasted_iota(jnp.int32, (BK, BK), 0)
    col = lax.broadcasted_iota(jnp.int32, (BK, BK), 1)
    return (row > col).astype(BF16), (row >= col).astype(BF16)


def _causal_mask(offset):
    row = lax.broadcasted_iota(jnp.int32, (BQ, BK), 0)
    col = lax.broadcasted_iota(jnp.int32, (BQ, BK), 1)
    return col + offset < row


def _attn_fwd(proj0, comm=None):
    s = proj0.shape[0]
    nq = s // BQ
    kpq = BQ // BK
    nsteps = DP // 128
    comm = comm or _NO_COMM
    nci, nco = len(comm.arrays), len(comm.out_shape)

    def body(*refs):
        q_ref, k_ref, v_ref = refs[:3]
        o_ref = refs[3 + nci]
        qn_ref, k8_ref, vb_ref = refs[4 + nci + nco:7 + nci + nco]
        cargs = (refs[3:3 + nci], refs[4 + nci:4 + nci + nco], refs[7 + nci + nco:])
        if nci:
            pl.when(pl.program_id(0) == 0)(lambda: comm.start(*cargs))
        qn_ref[...] = (-q_ref[...]).astype(BF16)
        k8_ref[...] = (k_ref[...] * INV_SQRT_HD).astype(BF16)
        vb_ref[...] = v_ref[...].astype(BF16)
        after, _ = _tri_masks()
        heads = [slice(HD * h, HD * (h + 1)) for h in range(2)]

        def qstep(qi, carry):
            q0 = pl.multiple_of(qi * BQ, BQ)
            qn = [qn_ref[pl.ds(q0, BQ), ls] for ls in heads]

            def blocks(k_hi, state, masks):
                ks = [pl.multiple_of(k_hi + (kpq - 1 - j) * BK, BK) for j in range(kpq)]
                lanes = [(j, h) for j in range(kpq) for h in range(len(heads))]
                mask = lambda j: None if masks is None else masks[j]
                nz = {jh: _dot_nt(qn[jh[1]], k8_ref[pl.ds(ks[jh[0]], BK), heads[jh[1]]]) for jh in lanes}
                ll = {jh: _sb_logits(nz[jh], mask(jh[0])) for jh in lanes}
                aft = {jh: _dot(ll[jh][0].astype(BF16), after) for jh in lanes}
                new = []
                for h in range(len(heads)):
                    o_acc, c = state[h]
                    for j in range(kpq):
                        a = jnp.exp(ll[j, h][1] + aft[j, h] + c)
                        if masks is not None:
                            a = jnp.where(masks[j], a, 0.0)
                        o_acc = o_acc + _dot(a.astype(BF16), vb_ref[pl.ds(ks[j], BK), heads[h]])
                        c = c + aft[j, h][:, 0:1] + ll[j, h][0][:, 0:1]
                    new.append((o_acc, c))
                return tuple(new)

            state = tuple((jnp.zeros((BQ, HD), F32), jnp.zeros((BQ, 1), F32)) for _ in heads)
            state = blocks(q0, state, [_causal_mask((kpq - 1 - j) * BK) for j in range(kpq)])
            state = lax.fori_loop(0, qi, lambda j, st: blocks(q0 - (j + 1) * BQ, st, None), state)
            for h, ls in enumerate(heads):
                o_ref[pl.ds(q0, BQ), ls] = state[h][0]
            return carry

        lax.fori_loop(0, nq, qstep, 0)
        if nci:
            pl.when(pl.program_id(0) == nsteps - 1)(lambda: comm.finish(*cargs))

    return pl.pallas_call(
        body, name="attn_fwd", grid=(nsteps,),
        in_specs=[pl.BlockSpec((s, 128), lambda h: (0, 8 + h)),
                  pl.BlockSpec((s, 128), lambda h: (0, 16 + h)),
                  pl.BlockSpec((s, 128), lambda h: (0, 24 + h))] + [ANY] * nci,
        out_specs=[pl.BlockSpec((s, 128), lambda h: (0, h))] + [ANY] * nco,
        out_shape=[jax.ShapeDtypeStruct((s, DP), F32)] + comm.out_shape,
        scratch_shapes=[pltpu.VMEM((s, 128), BF16)] * 3 + comm.sems,
        compiler_params=_params("arbitrary"),
    )(proj0, proj0, proj0, *comm.arrays)


def _even_out(ypool, ysb, proj0, wout, x, vecs, vecs_next):
    s = x.shape[0]
    tm = min(TME, s)

    def body(yp_ref, ys_ref, gate_ref, w_ref, x_ref, vec_ref, vecn_ref, x1_ref, out_ref, yg_ref, hn_ref):
        gt = gate_ref[...].astype(F32)
        sl = gt * jax.nn.sigmoid(gt)
        yg_ref[:, :DP] = (yp_ref[...] * sl[:, :DP]).astype(BF16)
        yg_ref[:, DP:] = (ys_ref[...] * sl[:, DP:]).astype(BF16)
        out = _dot(yg_ref[...], w_ref[...])
        out_ref[...] = out
        x1 = x_ref[...] + (1.0 + vec_ref[3:4, :]) * out
        x1_ref[...] = x1
        hn_ref[...] = _modulated_norm(x1, vecn_ref)

    row = lambda i: (i, 0)
    const = lambda i: (0, 0)
    return pl.pallas_call(
        body, name="even_out", grid=(s // tm,),
        in_specs=[pl.BlockSpec((tm, DP), row), pl.BlockSpec((tm, DP), row),
                  pl.BlockSpec((tm, DI), lambda i: (i, 2)),
                  pl.BlockSpec((DI, D), const),
                  pl.BlockSpec((tm, D), row), pl.BlockSpec((8, D), const), pl.BlockSpec((8, D), const)],
        out_specs=[pl.BlockSpec((tm, D), row), pl.BlockSpec((tm, D), row), pl.BlockSpec((tm, DI), row),
                   pl.BlockSpec((tm, D), row)],
        out_shape=[jax.ShapeDtypeStruct((s, D), F32), jax.ShapeDtypeStruct((s, D), F32),
                   jax.ShapeDtypeStruct((s, DI), BF16), jax.ShapeDtypeStruct((s, D), BF16)],
        compiler_params=_params("parallel"),
    )(ypool, ysb, proj0, wout, x, vecs, vecs_next)


def _odd_out(proj1, wout, x1, vecs, cw, cb, target):
    s = x1.shape[0]
    tm = min(TME, s)
    hb = tm // HALO

    def body(gb_ref, gc_ref, u_ref, gt_ref, hgc_ref, hu_ref, w_ref, x1_ref, vec_ref, cw_ref, cb_ref, tg_ref,
             dx2_ref, out_ref, y1_ref, acc_ref, ext_ref):
        i = pl.program_id(0)

        @pl.when(i == 0)
        def _():
            acc_ref[...] = jnp.zeros_like(acc_ref)

        ext_ref[HALO:, :] = gc_ref[...].astype(F32) * u_ref[...].astype(F32)
        ext_ref[0:HALO, :] = jnp.where(i > 0, hgc_ref[...].astype(F32) * hu_ref[...].astype(F32), 0.0)
        for c in range(DI // CT):
            cs = slice(c * CT, (c + 1) * CT)
            conv = (cb_ref[0:1, cs] + cw_ref[0:1, cs] * ext_ref[HALO - 2:HALO - 2 + tm, cs]
                    + cw_ref[1:2, cs] * ext_ref[HALO - 1:HALO - 1 + tm, cs]
                    + cw_ref[2:3, cs] * ext_ref[HALO:HALO + tm, cs])
            gt = gt_ref[:, cs].astype(F32)
            y1_ref[:, cs] = (gb_ref[:, cs].astype(F32) * conv * (gt * jax.nn.sigmoid(gt))).astype(BF16)
        out = _dot(y1_ref[...], w_ref[...])
        out_ref[...] = out
        x2 = x1_ref[...] + (1.0 + vec_ref[3:4, :]) * out
        r = lax.rsqrt(jnp.mean(x2 * x2, axis=-1, keepdims=True) + EPS)
        nrm = x2 * r
        fg = vec_ref[4:5, :]
        err = nrm * fg - tg_ref[...]
        acc_ref[1:2, :] += _rowsum(err * err) * (0.5 / D)
        dyf = err * (1.0 / D)
        acc_ref[0:1, :] += _rowsum(dyf * nrm)
        dn = dyf * fg
        dx2_ref[...] = r * (dn - nrm * jnp.mean(dn * nrm, axis=-1, keepdims=True))

    row = lambda i: (i, 0)
    halo = lambda col: (lambda i: (jnp.maximum(i * hb - 1, 0), col))
    const = lambda i: (0, 0)
    return pl.pallas_call(
        body, name="odd_out", grid=(s // tm,),
        in_specs=[pl.BlockSpec((tm, DI), lambda i: (i, 0)), pl.BlockSpec((tm, DI), lambda i: (i, 1)),
                  pl.BlockSpec((tm, DI), lambda i: (i, 2)), pl.BlockSpec((tm, DI), lambda i: (i, 3)),
                  pl.BlockSpec((HALO, DI), halo(1)), pl.BlockSpec((HALO, DI), halo(2)),
                  pl.BlockSpec((DI, D), const), pl.BlockSpec((tm, D), row), pl.BlockSpec((8, D), const),
                  pl.BlockSpec((8, DI), const), pl.BlockSpec((1, DI), const), pl.BlockSpec((tm, D), row)],
        out_specs=[pl.BlockSpec((tm, D), row), pl.BlockSpec((tm, D), row), pl.BlockSpec((tm, DI), row),
                   pl.BlockSpec((8, D), const)],
        out_shape=[jax.ShapeDtypeStruct((s, D), F32), jax.ShapeDtypeStruct((s, D), F32),
                   jax.ShapeDtypeStruct((s, DI), BF16), jax.ShapeDtypeStruct((8, D), F32)],
        scratch_shapes=[pltpu.VMEM((tm + HALO, DI), F32)],
        compiler_params=_params("arbitrary"),
    )(proj1, proj1, proj1, proj1, proj1, proj1, wout, x1, vecs, cw, cb, target)


def _odd_bwd(dx2, out1, proj1, wout, vecs, cw, cb):
    s = dx2.shape[0]
    tm = min(TME, s)
    nb = s // tm
    hb = tm // HALO

    def body(dx2_ref, out1_ref, gb_ref, gc_ref, u_ref, gt_ref, hgc_ref, hu_ref, w_ref, vec_ref, cw_ref, cb_ref,
             dout_ref, dproj_ref, accv_ref, accd_ref, uext_ref, dext_ref, dy_ref):
        i = pl.program_id(0)
        blk = nb - 1 - i

        @pl.when(i == 0)
        def _():
            accv_ref[...] = jnp.zeros_like(accv_ref)
            accd_ref[...] = jnp.zeros_like(accd_ref)
            dext_ref[tm:tm + 8, :] = jnp.zeros((8, DI), F32)

        dx2v = dx2_ref[...]
        accd_ref[0:1, :] += _rowsum(dx2v * out1_ref[...])
        dout = (dx2v * (1.0 + vec_ref[3:4, :])).astype(BF16)
        dout_ref[...] = dout
        dy_ref[...] = _dot_nt(dout, w_ref[...])
        uext_ref[HALO:, :] = gc_ref[...].astype(F32) * u_ref[...].astype(F32)
        uext_ref[0:HALO, :] = jnp.where(blk > 0, hgc_ref[...].astype(F32) * hu_ref[...].astype(F32), 0.0)
        for c in range(DI // CT):
            cs = slice(c * CT, (c + 1) * CT)
            u0 = uext_ref[HALO - 2:HALO - 2 + tm, cs]
            u1 = uext_ref[HALO - 1:HALO - 1 + tm, cs]
            u2 = uext_ref[HALO:HALO + tm, cs]
            w0, w1, w2 = cw_ref[0:1, cs], cw_ref[1:2, cs], cw_ref[2:3, cs]
            conv = cb_ref[0:1, cs] + w0 * u0 + w1 * u1 + w2 * u2
            gt = gt_ref[:, cs].astype(F32)
            sg = jax.nn.sigmoid(gt)
            gb = gb_ref[:, cs].astype(F32)
            dy = dy_ref[:, cs]
            t1 = dy * (gt * sg)
            dproj_ref[:, cs] = (t1 * conv).astype(BF16)
            dconv = t1 * gb
            dproj_ref[:, 3 * DI + c * CT:3 * DI + (c + 1) * CT] = (
                dy * gb * conv * (sg * (1.0 + gt * (1.0 - sg)))).astype(BF16)
            accv_ref[0:1, cs] += _rowsum(dconv * u0)
            accv_ref[1:2, cs] += _rowsum(dconv * u1)
            accv_ref[2:3, cs] += _rowsum(dconv * u2)
            accv_ref[3:4, cs] += _rowsum(dconv)
            dext_ref[0:tm, cs] = dconv
            duu = w2 * dconv + w1 * dext_ref[1:tm + 1, cs] + w0 * dext_ref[2:tm + 2, cs]
            dproj_ref[:, DI + c * CT:DI + (c + 1) * CT] = (duu * u_ref[:, cs].astype(F32)).astype(BF16)
            dproj_ref[:, 2 * DI + c * CT:2 * DI + (c + 1) * CT] = (duu * gc_ref[:, cs].astype(F32)).astype(BF16)
        dext_ref[tm:tm + 8, :] = dext_ref[0:8, :]

    rrow = lambda i: (nb - 1 - i, 0)
    rcol = lambda col: (lambda i: (nb - 1 - i, col))
    halo = lambda col: (lambda i: (jnp.maximum((nb - 1 - i) * hb - 1, 0), col))
    const = lambda i: (0, 0)
    return pl.pallas_call(
        body, name="odd_bwd", grid=(nb,),
        in_specs=[pl.BlockSpec((tm, D), rrow), pl.BlockSpec((tm, D), rrow),
                  pl.BlockSpec((tm, DI), rcol(0)), pl.BlockSpec((tm, DI), rcol(1)),
                  pl.BlockSpec((tm, DI), rcol(2)), pl.BlockSpec((tm, DI), rcol(3)),
                  pl.BlockSpec((HALO, DI), halo(1)), pl.BlockSpec((HALO, DI), halo(2)),
                  pl.BlockSpec((DI, D), const), pl.BlockSpec((8, D), const),
                  pl.BlockSpec((8, DI), const), pl.BlockSpec((1, DI), const)],
        out_specs=[pl.BlockSpec((tm, D), rrow), pl.BlockSpec((tm, NO), rrow),
                   pl.BlockSpec((8, DI), const), pl.BlockSpec((8, D), const)],
        out_shape=[jax.ShapeDtypeStruct((s, D), BF16), jax.ShapeDtypeStruct((s, NO), BF16),
                   jax.ShapeDtypeStruct((8, DI), F32), jax.ShapeDtypeStruct((8, D), F32)],
        scratch_shapes=[pltpu.VMEM((tm + HALO, DI), F32), pltpu.VMEM((tm + 8, DI), F32), pltpu.VMEM((tm, DI), F32)],
        compiler_params=_params("arbitrary"),
    )(dx2, out1, proj1, proj1, proj1, proj1, proj1, proj1, wout, vecs, cw, cb)


def _grad_w_cols(a, b, name):
    s, m = a.shape
    ns = b.shape[1] // NCHIP
    ts = min(TM, s)

    def body(a_ref, b_ref, o_ref):
        @pl.when(pl.program_id(1) == 0)
        def _():
            o_ref[...] = jnp.zeros_like(o_ref)

        o_ref[...] += _dot_tn(a_ref[...], b_ref[...])

    return pl.pallas_call(
        body, name=name, grid=(NCHIP, s // ts),
        in_specs=[pl.BlockSpec((ts, m), lambda j, k: (k, 0)),
                  pl.BlockSpec((ts, ns), lambda j, k: (k, j))],
        out_specs=pl.BlockSpec((None, m, ns), lambda j, k: (j, 0, 0)),
        out_shape=jax.ShapeDtypeStruct((NCHIP, m, ns), F32),
        compiler_params=_params("parallel", "arbitrary"),
    )(a, b)


def _grad_w_rows(a, b, name):
    s = a.shape[0]
    ms = a.shape[1] // NCHIP
    n = b.shape[1]
    ts = min(TM, s)

    def body(a_ref, b_ref, o_ref):
        @pl.when(pl.program_id(1) == 0)
        def _():
            o_ref[...] = jnp.zeros_like(o_ref)

        o_ref[...] += _dot_tn(a_ref[...], b_ref[...])

    return pl.pallas_call(
        body, name=name, grid=(NCHIP, s // ts),
        in_specs=[pl.BlockSpec((ts, ms), lambda i, k: (k, i)),
                  pl.BlockSpec((ts, n), lambda i, k: (k, 0))],
        out_specs=pl.BlockSpec((None, ms, n), lambda i, k: (i, 0, 0)),
        out_shape=jax.ShapeDtypeStruct((NCHIP, ms, n), F32),
        compiler_params=_params("parallel", "arbitrary"),
    )(a, b)


def _inproj_bwd(dproj, w, x, dx_in, vecs, name, comm=None):
    s = x.shape[0]
    ns = w.shape[2]
    tm = min(TME, s)
    ni = s // tm
    comm = comm or _NO_COMM
    nci, nco = len(comm.arrays), len(comm.out_shape)

    def body(*refs):
        dp_ref, w_hbm, x_ref, dxin_ref, vec_ref = refs[:5]
        dx_ref, acc_ref = refs[5 + nci:7 + nci]
        w_ref = refs[7 + nci + nco]
        cargs = (refs[5:5 + nci], refs[7 + nci:7 + nci + nco], refs[8 + nci + nco:])
        i = pl.program_id(0)

        @pl.when(i == 0)
        def _():
            acc_ref[...] = jnp.zeros_like(acc_ref)
            if nci:
                comm.start(*cargs)
            pltpu.sync_copy(w_hbm, w_ref)

        dh = _dot_nt(dp_ref[:, 0:ns], w_ref[0])
        for j in range(1, NCHIP):
            dh = dh + _dot_nt(dp_ref[:, j * ns:(j + 1) * ns], w_ref[j])
        xv = x_ref[...]
        r = lax.rsqrt(jnp.mean(xv * xv, axis=-1, keepdims=True) + EPS)
        nrm = xv * r
        g = vec_ref[0:1, :]
        sc1 = 1.0 + vec_ref[1:2, :]
        dhn = dh * nrm
        acc_ref[0:1, :] += _rowsum(dh)
        acc_ref[1:2, :] += _rowsum(dhn) * g
        acc_ref[2:3, :] += _rowsum(dhn) * sc1
        dn = dh * (g * sc1)
        dx_ref[...] = dxin_ref[...] + r * (dn - nrm * jnp.mean(dn * nrm, axis=-1, keepdims=True))

        if nci:
            pl.when(i == ni - 1)(lambda: comm.finish(*cargs))

    row = lambda i: (i, 0)
    const = lambda i: (0, 0)
    return pl.pallas_call(
        body, name=name, grid=(ni,),
        in_specs=[pl.BlockSpec((tm, NCHIP * ns), row), ANY,
                  pl.BlockSpec((tm, D), row), pl.BlockSpec((tm, D), row), pl.BlockSpec((8, D), const)] + [ANY] * nci,
        out_specs=[pl.BlockSpec((tm, D), row), pl.BlockSpec((8, D), const)] + [ANY] * nco,
        out_shape=[jax.ShapeDtypeStruct((s, D), F32), jax.ShapeDtypeStruct((8, D), F32)] + comm.out_shape,
        scratch_shapes=[pltpu.VMEM(w.shape, BF16)] + comm.sems,
        compiler_params=_params("arbitrary"),
    )(dproj, w, x, dx_in, vecs, *comm.arrays)


def _even_bwd(dx1, out0, ypool, ysb, proj0, wout, vecs):
    s = dx1.shape[0]
    tm = min(TME, s)

    def body(dx1_ref, out0_ref, yp_ref, ys_ref, gate_ref, w_ref, vec_ref,
             dout_ref, dyp_ref, dys_ref, dgt_ref, acc_ref):
        @pl.when(pl.program_id(0) == 0)
        def _():
            acc_ref[...] = jnp.zeros_like(acc_ref)

        dx1v = dx1_ref[...]
        acc_ref[0:1, :] += _rowsum(dx1v * out0_ref[...])
        dout = (dx1v * (1.0 + vec_ref[3:4, :])).astype(BF16)
        dout_ref[...] = dout
        dyg = _dot_nt(dout, w_ref[...])
        gt = gate_ref[...].astype(F32)
        sg = jax.nn.sigmoid(gt)
        sl = gt * sg
        dsl = sg * (1.0 + gt * (1.0 - sg))
        dyp_ref[...] = dyg[:, :DP] * sl[:, :DP]
        dys_ref[...] = dyg[:, DP:] * sl[:, DP:]
        dgt_ref[:, :DP] = (dyg[:, :DP] * yp_ref[...] * dsl[:, :DP]).astype(BF16)
        dgt_ref[:, DP:] = (dyg[:, DP:] * ys_ref[...] * dsl[:, DP:]).astype(BF16)

    row = lambda i: (i, 0)
    const = lambda i: (0, 0)
    return pl.pallas_call(
        body, name="even_bwd", grid=(s // tm,),
        in_specs=[pl.BlockSpec((tm, D), row), pl.BlockSpec((tm, D), row),
                  pl.BlockSpec((tm, DP), row), pl.BlockSpec((tm, DP), row),
                  pl.BlockSpec((tm, DI), lambda i: (i, 2)),
                  pl.BlockSpec((DI, D), const), pl.BlockSpec((8, D), const)],
        out_specs=[pl.BlockSpec((tm, D), row), pl.BlockSpec((tm, DP), row), pl.BlockSpec((tm, DP), row),
                   pl.BlockSpec((tm, DI), row), pl.BlockSpec((8, D), const)],
        out_shape=[jax.ShapeDtypeStruct((s, D), BF16), jax.ShapeDtypeStruct((s, DP), F32),
                   jax.ShapeDtypeStruct((s, DP), F32), jax.ShapeDtypeStruct((s, DI), BF16),
                   jax.ShapeDtypeStruct((8, D), F32)],
        compiler_params=_params("arbitrary"),
    )(dx1, out0, ypool, ysb, proj0, wout, vecs)


def _pool_bwd(dyp, p, pw, pscale):
    s = dyp.shape[0]
    tm = min(TM, s)
    nb = s // tm
    hb = tm // 16

    def body(dy_ref, dyh_ref, p_ref, w_ref, sc_ref, du_ref, dw_ref, acc_ref, ext_ref):
        i = pl.program_id(0)

        @pl.when(i == 0)
        def _():
            dw_ref[...] = jnp.zeros_like(dw_ref)
            acc_ref[...] = jnp.zeros_like(acc_ref)

        t = i * tm + lax.broadcasted_iota(jnp.int32, (tm + 16, 1), 0)
        for g, wdw in enumerate(WINDOWS):
            cs = slice(g * PG, (g + 1) * PG)
            sc = sc_ref[:, cs]
            dy = dy_ref[:, cs]
            dyh = jnp.where(i < nb - 1, dyh_ref[:, cs], 0.0)
            pb = p_ref[:, cs]
            wg = w_ref[g]
            acc_ref[0:1, cs] += _rowsum(dy * _dot(pb, wg))
            dypre = (dy * sc).astype(BF16)
            dw_ref[g] += _dot_tn(pb, dypre)
            dp = _dot_nt(dypre, wg)
            dph = _dot_nt((dyh * sc).astype(BF16), wg)
            inv = 1.0 / jnp.minimum(t + 1, wdw).astype(F32)
            ext_ref[0:tm, cs] = dp * inv[0:tm]
            ext_ref[tm:tm + 16, cs] = dph * inv[tm:tm + 16]
            acc = ext_ref[0:tm, cs]
            for j in range(1, wdw):
                acc = acc + ext_ref[j:j + tm, cs]
            du_ref[:, cs] = (acc - dp).astype(BF16)

    row = lambda i: (i, 0)
    return pl.pallas_call(
        body, name="pool_bwd", grid=(nb,),
        in_specs=[pl.BlockSpec((tm, DP), row),
                  pl.BlockSpec((16, DP), lambda i: (jnp.minimum((i + 1) * hb, s // 16 - 1), 0)),
                  pl.BlockSpec((tm, DP), row),
                  pl.BlockSpec((4, PG, PG), lambda i: (0, 0, 0)),
                  pl.BlockSpec((1, DP), lambda i: (0, 0))],
        out_specs=[pl.BlockSpec((tm, DP), row), pl.BlockSpec((4, PG, PG), lambda i: (0, 0, 0)),
                   pl.BlockSpec((8, DP), lambda i: (0, 0))],
        out_shape=[jax.ShapeDtypeStruct((s, DP), BF16), jax.ShapeDtypeStruct((4, PG, PG), F32),
                   jax.ShapeDtypeStruct((8, DP), F32)],
        scratch_shapes=[pltpu.VMEM((tm + 16, DP), F32)],
        compiler_params=_params("arbitrary"),
    )(dyp, dyp, p, pw, pscale)


def _attn_bwd(proj0, ysb, dys, du, dgate, comm=None):
    s = proj0.shape[0]
    nq = s // BQ
    kpq = BQ // BK
    nsteps = DP // 128
    comm = comm or _NO_COMM
    nci, nco = len(comm.arrays), len(comm.out_shape)

    def body(*refs):
        q_ref, k_ref, v_ref, o_ref, do_ref, du_hbm, dgate_hbm = refs[:7]
        dproj_hbm = refs[7 + nci]
        scratch = refs[8 + nci + nco:]
        qn_ref, k8_ref, vb_ref, dob_ref, dka_ref, dva_ref, dq_ref, dk_ref, dv_ref, side_sems, part_sems = scratch[:11]
        cargs = (refs[7:7 + nci], refs[8 + nci:8 + nci + nco], scratch[11:])
        step = pl.program_id(0)
        side = [pltpu.make_async_copy(du_hbm, dproj_hbm.at[:, pl.ds(0, DP)], side_sems.at[0]),
                pltpu.make_async_copy(dgate_hbm, dproj_hbm.at[:, pl.ds(NE - DI, DI)], side_sems.at[1])]

        @pl.when(step == 0)
        def _():
            if nci:
                comm.start(*cargs)
            for cp in side:
                cp.start()

        qn_ref[...] = (-q_ref[...]).astype(BF16)
        k8_ref[...] = (k_ref[...] * INV_SQRT_HD).astype(BF16)
        vb_ref[...] = v_ref[...].astype(BF16)
        dob_ref[...] = do_ref[...].astype(BF16)
        dka_ref[...] = jnp.zeros_like(dka_ref)
        dva_ref[...] = jnp.zeros_like(dva_ref)
        after, from_on = _tri_masks()
        heads = [slice(HD * h, HD * (h + 1)) for h in range(2)]

        def qstep(qi, carry):
            q0 = pl.multiple_of(qi * BQ, BQ)
            qn = [qn_ref[pl.ds(q0, BQ), ls] for ls in heads]
            do = [dob_ref[pl.ds(q0, BQ), ls] for ls in heads]
            total = [jnp.sum(do[h].astype(F32) * o_ref[pl.ds(q0, BQ), ls], axis=1, keepdims=True)
                     for h, ls in enumerate(heads)]

            def blocks(k_hi, state, masks):
                ks = [pl.multiple_of(k_hi + (kpq - 1 - j) * BK, BK) for j in range(kpq)]
                lanes = [(j, h) for j in range(kpq) for h in range(len(heads))]
                mask = lambda j: None if masks is None else masks[j]
                k8 = {jh: k8_ref[pl.ds(ks[jh[0]], BK), heads[jh[1]]] for jh in lanes}
                nz = {jh: _dot_nt(qn[jh[1]], k8[jh]) for jh in lanes}
                da = {jh: _dot_nt(do[jh[1]], vb_ref[pl.ds(ks[jh[0]], BK), heads[jh[1]]]) for jh in lanes}
                ll = {jh: _sb_logits(nz[jh], mask(jh[0])) for jh in lanes}
                aft = {jh: _dot(ll[jh][0].astype(BF16), after) for jh in lanes}
                new = []
                for h in range(len(heads)):
                    dq_acc, c, cg = state[h]
                    for j in range(kpq):
                        a = jnp.exp(ll[j, h][1] + aft[j, h] + c)
                        if masks is not None:
                            a = jnp.where(masks[j], a, 0.0)
                        ab = a.astype(BF16)
                        g = da[j, h] * ab.astype(F32)
                        suf = _split_dot(g, from_on)
                        dz = g - jnp.exp(ll[j, h][1]) * (g + ((total[h] - cg) - suf))
                        if masks is not None:
                            dz = jnp.where(masks[j], dz, 0.0)
                        dzb = dz.astype(BF16)
                        dka_ref[pl.ds(ks[j], BK), heads[h]] += _dot_tn(dzb, qn[h])
                        dva_ref[pl.ds(ks[j], BK), heads[h]] += _dot_tn(ab, do[h])
                        dq_acc = dq_acc + _dot(dzb, k8[j, h])
                        c = c + aft[j, h][:, 0:1] + ll[j, h][0][:, 0:1]
                        cg = cg + suf[:, 0:1]
                    new.append((dq_acc, c, cg))
                return tuple(new)

            zero = jnp.zeros((BQ, 1), F32)
            state = tuple((jnp.zeros((BQ, HD), F32), zero, zero) for _ in heads)
            state = blocks(q0, state, [_causal_mask((kpq - 1 - j) * BK) for j in range(kpq)])
            state = lax.fori_loop(0, qi, lambda j, st: blocks(q0 - (j + 1) * BQ, st, None), state)
            for h, ls in enumerate(heads):
                dq_ref[pl.ds(q0, BQ), ls] = state[h][0].astype(BF16)
            return carry

        lax.fori_loop(0, nq, qstep, 0)
        dk_ref[...] = (dka_ref[...] * (-INV_SQRT_HD)).astype(BF16)
        dv_ref[...] = dva_ref[...].astype(BF16)
        lanes0 = pl.multiple_of(step * 128, 128)
        parts = [pltpu.make_async_copy(src, dproj_hbm.at[:, pl.ds((1 + k) * DP + lanes0, 128)], part_sems.at[k])
                 for k, src in enumerate((dq_ref, dk_ref, dv_ref))]
        for cp in parts:
            cp.start()
        for cp in parts:
            cp.wait()

        @pl.when(step == nsteps - 1)
        def _():
            for cp in side:
                cp.wait()
            if nci:
                comm.finish(*cargs)

    col = lambda h: (0, h)
    dma = pltpu.SemaphoreType.DMA
    return pl.pallas_call(
        body, name="attn_bwd", grid=(nsteps,),
        in_specs=[pl.BlockSpec((s, 128), lambda h: (0, 8 + h)),
                  pl.BlockSpec((s, 128), lambda h: (0, 16 + h)),
                  pl.BlockSpec((s, 128), lambda h: (0, 24 + h)),
                  pl.BlockSpec((s, 128), col), pl.BlockSpec((s, 128), col), ANY, ANY] + [ANY] * nci,
        out_specs=[ANY] * (1 + nco),
        out_shape=[jax.ShapeDtypeStruct((s, NE), BF16)] + comm.out_shape,
        scratch_shapes=([pltpu.VMEM((s, 128), BF16)] * 4 + [pltpu.VMEM((s, 128), F32)] * 2
                        + [pltpu.VMEM((s, 128), BF16)] * 3 + [dma((2,)), dma((3,))] + comm.sems),
        compiler_params=_params("arbitrary"),
    )(proj0, proj0, proj0, ysb, dys, du, dgate, *comm.arrays)


def _adamw_math(w, g, m, v):
    m2 = B1 * m + (1.0 - B1) * g
    v2 = B2 * v + (1.0 - B2) * (g * g)
    m_hat = m2 / (1.0 - B1 ** STEP)
    v_hat = v2 / (1.0 - B2 ** STEP)
    return -LR * (m_hat / (jnp.sqrt(v_hat) + EPS_ADAM) + WD * w), m2, v2


def _adamw(w, g, m, v, name):
    r, c = w.shape
    tr = r
    while tr * c * 4 > (1 << 20) and tr % 16 == 0:
        tr //= 2

    def body(w_ref, g_ref, m_ref, v_ref, d_ref, m2_ref, v2_ref):
        d_ref[...], m2_ref[...], v2_ref[...] = _adamw_math(w_ref[...], g_ref[...], m_ref[...], v_ref[...])

    spec = pl.BlockSpec((tr, c), lambda i: (i, 0))
    return pl.pallas_call(
        body, name=name, grid=(r // tr,),
        in_specs=[spec] * 4, out_specs=[spec] * 3,
        out_shape=[jax.ShapeDtypeStruct((r, c), F32)] * 3,
        compiler_params=_params("parallel"),
    )(w, g, m, v)


def _local_step(x, target, vecs0, vecs1, win0, rest, pscale, cw8, cb, sel=None):
    dist = sel is not None
    h0 = _norm_mod(x, vecs0, "norm0")
    proj0, *got = _inproj(h0, win0, "inproj0", _gather_comm(list(rest[0:2])) if dist else None)
    pw, wout0 = (_group_major(got[0]), got[1].reshape(DI, D)) if dist else rest[0:2]
    p, ypool = _pool_fwd(proj0, pw, pscale)
    ysb, *got = _attn_fwd(proj0, _gather_comm(list(rest[2:4])) if dist else None)
    win1, wout1 = (got[0], got[1].reshape(DI, D)) if dist else rest[2:4]
    x1, out0, yg, h1 = _even_out(ypool, ysb, proj0, wout0, x, vecs0, vecs1)
    proj1, = _inproj(h1, win1, "inproj1")
    dx2, out1, y1, acc_f = _odd_out(proj1, wout1, x1, vecs1, cw8, cb, target)

    def chip_partials(grads, names):
        from_sibling = _send_halves(grads, "rs_send_halves_" + names[0])
        part = [_add_halves(g, t, sel[1:2], "rs_add_halves_" + nm) for g, t, nm in zip(grads, from_sibling, names)]
        return [p32 for p32, _ in part], _exchange_comm([p16 for _, p16 in part])

    dout1, dproj1, acc_cv, acc_g1 = _odd_bwd(dx2, out1, proj1, wout1, vecs1, cw8, cb)
    g_wout1 = _grad_w_rows(y1, dout1, "grad_wout1")
    g_win1 = _grad_w_cols(h1, dproj1, "grad_win1")
    dx1, acc_n1 = _inproj_bwd(dproj1, win1, x1, dx2, vecs1, "inproj1_bwd")

    dout0, dyp, dys, dgate0, acc_g0 = _even_bwd(dx1, out0, ypool, ysb, proj0, wout0, vecs0)
    g_wout0 = _grad_w_rows(yg, dout0, "grad_wout0")
    du, g_pw, acc_ps = _pool_bwd(dyp, p, pw, pscale)
    early = [_chip_major(g_pw, PG // NCHIP), g_wout0, g_win1, g_wout1]
    part_a, swap_a = chip_partials(early, ["pool", "wout0", "win1", "wout1"]) if dist else (None, None)
    dproj0, *got_a = _attn_bwd(proj0, ysb, dys, du, dgate0, swap_a)
    g_win0 = _grad_w_cols(h0, dproj0, "grad_win0")
    part_b, swap_b = chip_partials([g_win0], ["win0"]) if dist else (None, None)
    dx0, acc_n0, *got_b = _inproj_bwd(dproj0, win0, x, dx1, vecs0, "inproj0_bwd", swap_b)

    if dist:
        names = ["win0", "pool", "wout0", "win1", "wout1"]
        halves = [_add_partials(p32, t, sel, "rs_add_partials_" + nm)
                  for p32, t, nm in zip(part_b + part_a, got_b + got_a, names)]
        grads = tuple(_join_halves(halves))
    else:
        grads = (g_win0, g_pw, g_wout0, g_win1, g_wout1)

    sums = dict(
        dm0=jnp.concatenate([acc_n0[0:2], acc_g0[0:1]], axis=0),
        dm1=jnp.concatenate([acc_n1[0:2], acc_g1[0:1]], axis=0),
        norm_g=jnp.concatenate([acc_n0[2:3], acc_n1[2:3]], axis=0),
        pool_scale=acc_ps[0:1], final_g=acc_f[0:1], loss=acc_f[1:2],
        conv_w=acc_cv[0:3], conv_b=acc_cv[3:4])
    return dx0, grads, sums


ANY = pl.BlockSpec(memory_space=pl.ANY)
CHIP_FLIPS = ((1, 0), (0, 1), (1, 1))


def _place():
    return lax.axis_index("x"), lax.axis_index("y"), lax.axis_index("c")


def _flip(v, f):
    return 1 - v if f else v


def _allgather8(v, name):
    m_per, n = v.shape

    def body(x_ref, out_ref, send_sems, recv_sems, local_sem):
        x, y, c = _place()
        me, sibling = (x, y, c), (x, y, 1 - c)
        chips = [(_flip(x, fx), _flip(y, fy)) for fx, fy in CHIP_FLIPS]

        def rows(px, py, pc):
            return out_ref.at[pl.ds((4 * px + 2 * py + pc) * m_per, m_per), :]

        def copy(k, block, to, src=None):
            return pltpu.make_async_remote_copy(
                src_ref=rows(*block) if src is None else src, dst_ref=rows(*block),
                send_sem=send_sems.at[k], recv_sem=recv_sems.at[k], device_id=to, device_id_type=MESH)

        mine = pltpu.make_async_copy(x_ref, rows(*me), local_sem)
        mine.start()
        first = [copy(0, me, sibling, src=x_ref)]
        first += [copy(1 + j, me, (*chip, c), src=x_ref) for j, chip in enumerate(chips)]
        for cp in first:
            cp.start()
        passed = [copy(4 + j, (*chip, c), sibling) for j, chip in enumerate(chips)]
        for j, chip in enumerate(chips):
            copy(1 + j, (*chip, c), me).wait_recv()
            passed[j].start()
        copy(0, sibling, me).wait_recv()
        for j, chip in enumerate(chips):
            copy(4 + j, (*chip, 1 - c), me).wait_recv()
        for cp in first + passed:
            cp.wait_send()
        mine.wait()

    return pl.pallas_call(
        body, name=name,
        out_shape=jax.ShapeDtypeStruct((NDEV * m_per, n), v.dtype),
        in_specs=[pl.BlockSpec(memory_space=pltpu.VMEM)],
        out_specs=pl.BlockSpec(memory_space=pltpu.VMEM),
        scratch_shapes=[pltpu.SemaphoreType.DMA((7,)), pltpu.SemaphoreType.DMA((7,)), pltpu.SemaphoreType.DMA],
    )(v)


class _Comm:
    def __init__(self, arrays, out_shape, sems, start, finish):
        self.arrays, self.out_shape, self.sems, self.start, self.finish = arrays, out_shape, sems, start, finish


_NO_COMM = _Comm([], [], [], None, None)


def _run_comm(comm, name):
    n = len(comm.arrays)

    def body(*refs):
        args = (refs[:n], refs[n:n + len(comm.out_shape)], refs[n + len(comm.out_shape):])
        comm.start(*args)
        comm.finish(*args)

    return pl.pallas_call(
        body, name=name, out_shape=comm.out_shape,
        in_specs=[ANY] * n, out_specs=[ANY] * len(comm.out_shape), scratch_shapes=comm.sems,
    )(*comm.arrays)


def _gather_comm(shards):
    n = len(shards)

    def pieces(ins, outs, sems, kinds):
        x, y, c = _place()
        ici_send, ici_recv, fwd_send, fwd_recv, local = sems
        own, sibling = 2 * x + y, (x, y, 1 - c)
        made = {kind: [] for kind in kinds}
        for w in range(n):
            r2 = ins[w].shape[0] // 2
            mine, other = pl.ds(c * r2, r2), pl.ds((1 - c) * r2, r2)
            if "local" in kinds:
                made["local"].append(pltpu.make_async_copy(ins[w], outs[w].at[own], local.at[w]))
            for d, (fx, fy) in enumerate(CHIP_FLIPS):
                px, py, k = _flip(x, fx), _flip(y, fy), 3 * w + d
                peer = 2 * px + py
                ici = dict(send_sem=ici_send.at[k], recv_sem=ici_recv.at[k], device_id=(px, py, c), device_id_type=MESH)
                fwd = dict(send_sem=fwd_send.at[k], recv_sem=fwd_recv.at[k], device_id=sibling, device_id_type=MESH)
                if "ici_out" in kinds:
                    made["ici_out"].append(pltpu.make_async_remote_copy(
                        src_ref=ins[w].at[mine, :], dst_ref=outs[w].at[own, mine, :], **ici))
                if "ici_in" in kinds:
                    made["ici_in"].append(pltpu.make_async_remote_copy(
                        src_ref=ins[w].at[mine, :], dst_ref=outs[w].at[peer, mine, :], **ici))
                if "fwd_out" in kinds:
                    made["fwd_out"].append(pltpu.make_async_remote_copy(
                        src_ref=outs[w].at[peer, mine, :], dst_ref=outs[w].at[peer, mine, :], **fwd))
                if "fwd_in" in kinds:
                    made["fwd_in"].append(pltpu.make_async_remote_copy(
                        src_ref=outs[w].at[peer, other, :], dst_ref=outs[w].at[peer, other, :], **fwd))
        return made

    def start(ins, outs, sems):
        made = pieces(ins, outs, sems, ("local", "ici_out"))
        for cp in made["local"] + made["ici_out"]:
            cp.start()

    def finish(ins, outs, sems):
        made = pieces(ins, outs, sems, ("local", "ici_out", "ici_in", "fwd_out", "fwd_in"))
        for arrived, onward in zip(made["ici_in"], made["fwd_out"]):
            arrived.wait_recv()
            onward.start()
        for cp in made["fwd_in"]:
            cp.wait_recv()
        for cp in made["ici_out"] + made["fwd_out"]:
            cp.wait_send()
        for cp in made["local"]:
            cp.wait()

    dma = pltpu.SemaphoreType.DMA
    return _Comm(list(shards), [jax.ShapeDtypeStruct((NCHIP,) + a.shape, a.dtype) for a in shards],
                 [dma((3 * n,))] * 4 + [dma((n,))], start, finish)


def _exchange_comm(parts):
    n = len(parts)

    def copies(ins, outs, sems):
        x, y, c = _place()
        send_sems, recv_sems = sems
        out = []
        for w in range(n):
            for d, (fx, fy) in enumerate(CHIP_FLIPS):
                px, py = _flip(x, fx), _flip(y, fy)
                out.append(pltpu.make_async_remote_copy(
                    src_ref=ins[w].at[2 * px + py], dst_ref=outs[w].at[d], send_sem=send_sems.at[3 * w + d],
                    recv_sem=recv_sems.at[3 * w + d], device_id=(px, py, c), device_id_type=MESH))
        return out

    def start(ins, outs, sems):
        for cp in copies(ins, outs, sems):
            cp.start()

    def finish(ins, outs, sems):
        cps = copies(ins, outs, sems)
        for cp in cps:
            cp.wait_recv()
        for cp in cps:
            cp.wait_send()

    dma = pltpu.SemaphoreType.DMA
    return _Comm(list(parts), [jax.ShapeDtypeStruct((3,) + p.shape[1:], BF16) for p in parts],
                 [dma((3 * n,))] * 2, start, finish)


def _send_halves(grads, name):
    n = len(grads)

    def body(*refs):
        ins, outs = refs[:n], refs[n:2 * n]
        send_sems, recv_sems = refs[2 * n:]
        x, y, c = _place()
        copies = []
        for w in range(n):
            r2 = ins[w].shape[1] // 2
            cp = pltpu.make_async_remote_copy(
                src_ref=ins[w].at[:, pl.ds((1 - c) * r2, r2), :], dst_ref=outs[w],
                send_sem=send_sems.at[w], recv_sem=recv_sems.at[w], device_id=(x, y, 1 - c), device_id_type=MESH)
            cp.start()
            copies.append(cp)
        for cp in copies:
            cp.wait_recv()
        for cp in copies:
            cp.wait_send()

    return pl.pallas_call(
        body, name=name,
        out_shape=[jax.ShapeDtypeStruct((NCHIP, g.shape[1] // 2, g.shape[2]), F32) for g in grads],
        in_specs=[ANY] * n, out_specs=[ANY] * n,
        scratch_shapes=[pltpu.SemaphoreType.DMA((n,)), pltpu.SemaphoreType.DMA((n,))],
    )(*grads)


def _row_tile(rows, cols):
    tr = rows
    while tr * cols * 4 > (1 << 20) and tr % 16 == 0:
        tr //= 2
    return tr


def _add_halves(g, t1, core, name):
    _, r, cdim = g.shape
    r2 = r // 2
    tr = _row_tile(r2, cdim)
    nt = r2 // tr

    def body(core_ref, g_ref, t_ref, p_ref, pb_ref):
        p = g_ref[...] + t_ref[...]
        p_ref[...] = p
        pb_ref[...] = p.astype(BF16)

    blk = pl.BlockSpec((None, tr, cdim), lambda j, i, core_ref: (j, i, 0))
    return pl.pallas_call(
        body, name=name,
        grid_spec=pltpu.PrefetchScalarGridSpec(
            num_scalar_prefetch=1, grid=(NCHIP, nt),
            in_specs=[pl.BlockSpec((None, tr, cdim), lambda j, i, core_ref: (j, core_ref[0] * nt + i, 0)), blk],
            out_specs=[blk, blk]),
        out_shape=[jax.ShapeDtypeStruct((NCHIP, r2, cdim), F32), jax.ShapeDtypeStruct((NCHIP, r2, cdim), BF16)],
        compiler_params=_params("parallel", "parallel"),
    )(core, g, t1)


def _add_partials(p, t2, sel, name):
    _, r2, cdim = p.shape
    tr = _row_tile(r2, cdim)
    nt = r2 // tr

    def body(sel_ref, p_ref, t_ref, o_ref):
        o_ref[...] = ((p_ref[...] + t_ref[0].astype(F32)) + t_ref[1].astype(F32)) + t_ref[2].astype(F32)

    return pl.pallas_call(
        body, name=name,
        grid_spec=pltpu.PrefetchScalarGridSpec(
            num_scalar_prefetch=1, grid=(nt,),
            in_specs=[pl.BlockSpec((None, tr, cdim), lambda i, sel_ref: (sel_ref[0], i, 0)),
                      pl.BlockSpec((3, tr, cdim), lambda i, sel_ref: (0, i, 0))],
            out_specs=pl.BlockSpec((tr, cdim), lambda i, sel_ref: (sel_ref[1] * nt + i, 0))),
        out_shape=jax.ShapeDtypeStruct((2 * r2, cdim), F32),
        compiler_params=_params("parallel"),
    )(sel, p, t2)


def _join_halves(grads):
    n = len(grads)

    def body(*refs):
        bufs = refs[n:2 * n]
        send_sems, recv_sems = refs[2 * n:]
        x, y, c = _place()
        copies = []
        for w in range(n):
            r2 = bufs[w].shape[0] // 2
            mine = bufs[w].at[pl.ds(c * r2, r2), :]
            cp = pltpu.make_async_remote_copy(
                src_ref=mine, dst_ref=mine, send_sem=send_sems.at[w], recv_sem=recv_sems.at[w],
                device_id=(x, y, 1 - c), device_id_type=MESH)
            cp.start()
            copies.append(cp)
        for w in range(n):
            r2 = bufs[w].shape[0] // 2
            theirs = bufs[w].at[pl.ds((1 - c) * r2, r2), :]
            pltpu.make_async_remote_copy(
                src_ref=theirs, dst_ref=theirs, send_sem=send_sems.at[w], recv_sem=recv_sems.at[w],
                device_id=(x, y, 1 - c), device_id_type=MESH).wait_recv()
        for cp in copies:
            cp.wait_send()

    return pl.pallas_call(
        body, name="rs_join_halves",
        out_shape=[jax.ShapeDtypeStruct(g.shape, F32) for g in grads],
        in_specs=[ANY] * n, out_specs=[ANY] * n, input_output_aliases={w: w for w in range(n)},
        scratch_shapes=[pltpu.SemaphoreType.DMA((n,)), pltpu.SemaphoreType.DMA((n,))],
    )(*grads)


def _ada_fwd(c_all, ada_w):
    nl, _, ns = ada_w.shape

    def body(c_ref, w_ref, o_ref):
        cv = c_ref[...]
        o_ref[...] = _dot((cv * jax.nn.sigmoid(cv)).astype(BF16), w_ref[...].astype(BF16))

    return pl.pallas_call(
        body, name="ada_fwd", grid=(nl,),
        in_specs=[pl.BlockSpec((NDEV, D), lambda i: (0, 0)), pl.BlockSpec((None, D, ns), lambda i: (i, 0, 0))],
        out_specs=pl.BlockSpec((None, NDEV, ns), lambda i: (i, 0, 0)),
        out_shape=jax.ShapeDtypeStruct((nl, NDEV, ns), F32),
        compiler_params=_params("parallel"),
    )(c_all, ada_w)


PACK_ROWS = 24


def _reduce_packed(gathered):
    def body(g_ref, tot_ref, loss_ref):
        tot = g_ref[0:PACK_ROWS, :]
        for dev in range(1, NDEV):
            tot = tot + g_ref[dev * PACK_ROWS:(dev + 1) * PACK_ROWS, :]
        tot_ref[...] = tot
        loss_ref[...] = jnp.zeros((8, 128), F32) + jnp.sum(tot[10:11, :])

    return pl.pallas_call(
        body, name="reduce_packed",
        out_shape=[jax.ShapeDtypeStruct((PACK_ROWS, D), F32), jax.ShapeDtypeStruct((8, 128), F32)],
    )(gathered)


def _ada_w_update(c_t, dms, w, m, v):
    nl, _, ns = w.shape
    tr = 256

    def body(ct_ref, dm_ref, w_ref, m_ref, v_ref, g_ref, d_ref, m2_ref, v2_ref):
        ct = ct_ref[...]
        sc = ct * jax.nn.sigmoid(ct)
        dm = dm_ref[...]
        g = sc[:, 0:1] * dm[0:1, :]
        for b in range(1, NDEV):
            g = g + sc[:, b:b + 1] * dm[b:b + 1, :]
        g_ref[...] = g
        d_ref[...], m2_ref[...], v2_ref[...] = _adamw_math(w_ref[...], g, m_ref[...], v_ref[...])

    blk = pl.BlockSpec((None, tr, ns), lambda i, j: (i, j, 0))
    return pl.pallas_call(
        body, name="ada_w_update", grid=(nl, D // tr),
        in_specs=[pl.BlockSpec((tr, NDEV), lambda i, j: (j, 0)),
                  pl.BlockSpec((None, NDEV, ns), lambda i, j: (i, 0, 0)), blk, blk, blk],
        out_specs=[blk] * 4,
        out_shape=[jax.ShapeDtypeStruct((nl, D, ns), F32)] * 4,
        compiler_params=_params("parallel", "parallel"),
    )(c_t, dms, w, m, v)


def _chip_major(a, parts):
    g, _, cdim = a.shape
    return jnp.transpose(a.reshape(g, NCHIP, parts, cdim), (1, 0, 2, 3)).reshape(NCHIP, g * parts, cdim)


def _group_major(a):
    return jnp.transpose(a.reshape(NCHIP, 4, PG // NCHIP, PG), (1, 0, 2, 3)).reshape(4, PG, PG)


def kernel(x, c, norm_g, ada_w, ada_b, even_w_in, pool_w, pool_scale, even_w_out, odd_w_in, conv_w, conv_b, odd_w_out, final_g, loss_target, m_norm_g, m_ada_w, m_ada_b, m_even_w_in, m_pool_w, m_pool_scale, m_even_w_out, m_odd_w_in, m_conv_w, m_conv_b, m_odd_w_out, m_final_g, v_norm_g, v_ada_w, v_ada_b, v_even_w_in, v_pool_w, v_pool_scale, v_even_w_out, v_odd_w_in, v_conv_w, v_conv_b, v_odd_w_out, v_final_g):
    ix, iy, ic = _place()
    chip = 2 * ix + iy
    batch = 2 * chip + ic
    sel = jnp.stack([chip, ic]).astype(jnp.int32)
    ns_ada = ada_w.shape[2]
    ns_conv = conv_b.shape[1]

    conv_rows = jnp.pad(jnp.concatenate([conv_w[0], conv_b], axis=0), ((0, 3), (0, D - ns_conv)))
    first = _allgather8(jnp.concatenate([c, conv_rows], axis=0), "gather_c_conv").reshape(NCHIP, 2, 8, D)
    c_all = first[:, :, 0].reshape(NDEV, D)
    cw_full = jnp.transpose(first[:, 0, 1:5, 0:ns_conv], (1, 0, 2)).reshape(4, DI)
    cw8 = jnp.concatenate([cw_full[0:3], jnp.zeros((5, DI), F32)], axis=0)
    cb_full = cw_full[3:4]

    m_cols = _allgather8(_ada_fwd(c_all, ada_w).reshape(2 * NDEV, ns_ada), "gather_ada")
    m_cols = m_cols.reshape(NCHIP, 2, 2, NDEV, ns_ada)[:, 0]
    m_mine = lax.dynamic_index_in_dim(m_cols, batch, axis=2, keepdims=False)
    m_mine = jnp.transpose(m_mine, (1, 0, 2)).reshape(2, 3 * D) + ada_b
    zrow = jnp.zeros((3, D), F32)

    def vec_rows(i):
        sh, sc, gt = m_mine[i, 0:D], m_mine[i, D:2 * D], m_mine[i, 2 * D:3 * D]
        return jnp.concatenate([jnp.stack([norm_g[i], sc, sh, gt, final_g]), zrow], axis=0)

    win0, = _run_comm(_gather_comm([even_w_in[0].astype(BF16)]), "gather_win0")
    shards = (pool_w[0].astype(BF16).reshape(PG, PG), even_w_out[0].astype(BF16),
              odd_w_in[0].astype(BF16), odd_w_out[0].astype(BF16))
    dx0, grads, sums = _local_step(
        x[0], loss_target[0], vec_rows(0), vec_rows(1), win0, shards, pool_scale, cw8, cb_full, sel)
    r_win0, r_pw, r_wout0, r_win1, r_wout1 = grads

    packed = jnp.concatenate([
        sums["dm0"], sums["dm1"], sums["norm_g"], sums["pool_scale"], sums["final_g"], sums["loss"],
        sums["conv_w"].reshape(6, D), sums["conv_b"].reshape(2, D), jnp.zeros((PACK_ROWS - 19, D), F32)], axis=0)
    gathered = _allgather8(packed, "gather_sums")
    tot, loss8 = _reduce_packed(gathered)
    loss = loss8[0, 0]
    g_norm_g, g_pool_scale, g_final_g = tot[6:8], tot[8:9], tot[9]
    g_ada_b = tot[0:6].reshape(2, 3 * D)
    g_conv_w = lax.dynamic_slice_in_dim(tot[11:17].reshape(3, DI), chip * (DI // NCHIP), DI // NCHIP, axis=1)
    g_conv_b = lax.dynamic_slice_in_dim(tot[17:19].reshape(1, DI), chip * (DI // NCHIP), DI // NCHIP, axis=1)
    dm_all = gathered.reshape(NDEV, PACK_ROWS, D)[:, 0:6].reshape(NDEV, 2, 3 * D)
    dm_cols = jnp.transpose(lax.dynamic_slice_in_dim(dm_all, chip * ns_ada, ns_ada, axis=2), (1, 0, 2))
    g_ada_w, d_ada_w, nm_ada_w, nv_ada_w = _ada_w_update(jnp.transpose(c_all), dm_cols, ada_w, m_ada_w, v_ada_w)

    def upd(w, g, m, v, name):
        shape = w.shape
        w2, m2, v2 = (a.reshape(g.shape) for a in (w, m, v))
        d, nm, nv = _adamw(w2, g, m2, v2, name)
        return g.reshape(shape), d.reshape(shape), nm.reshape(shape), nv.reshape(shape)

    o_win0 = upd(even_w_in, r_win0, m_even_w_in, v_even_w_in, "adamw_win0")
    o_pw = upd(pool_w, r_pw, m_pool_w, v_pool_w, "adamw_pool")
    o_wout0 = upd(even_w_out, r_wout0, m_even_w_out, v_even_w_out, "adamw_wout0")
    o_win1 = upd(odd_w_in, r_win1, m_odd_w_in, v_odd_w_in, "adamw_win1")
    o_wout1 = upd(odd_w_out, r_wout1, m_odd_w_out, v_odd_w_out, "adamw_wout1")

    def pack_small(ng, ab, ps, fg, cwv, cbv):
        conv = jnp.concatenate([cwv.reshape(3, -1), cbv.reshape(1, -1)], axis=0).reshape(2, D)
        return jnp.concatenate([ng, ab.reshape(6, D), ps, fg.reshape(1, D), conv, jnp.zeros((4, D), F32)], axis=0)

    sw = pack_small(norm_g, ada_b, pool_scale, final_g, conv_w, conv_b)
    sg = pack_small(g_norm_g, g_ada_b, g_pool_scale, g_final_g, g_conv_w, g_conv_b)
    sm = pack_small(m_norm_g, m_ada_b, m_pool_scale, m_final_g, m_conv_w, m_conv_b)
    sv = pack_small(v_norm_g, v_ada_b, v_pool_scale, v_final_g, v_conv_w, v_conv_b) + jnp.concatenate(
        [jnp.zeros((12, D), F32), jnp.ones((4, D), F32)], axis=0)
    small = _adamw(sw, sg, sm, sv, "adamw_small")

    def unpack_small(a):
        conv = a[10:12].reshape(4, -1)
        return dict(norm_g=a[0:2], ada_b=a[2:8].reshape(2, 3 * D), pool_scale=a[8:9], final_g=a[9],
                    conv_w=conv[0:3].reshape(conv_w.shape), conv_b=conv[3:4].reshape(conv_b.shape))

    s_grad = dict(norm_g=g_norm_g, ada_b=g_ada_b, pool_scale=g_pool_scale, final_g=g_final_g,
                  conv_w=g_conv_w.reshape(conv_w.shape), conv_b=g_conv_b.reshape(conv_b.shape))
    s_out = [s_grad] + [unpack_small(a) for a in small]

    outs = []
    for k in range(4):
        sm_k = s_out[k]
        outs.append([sm_k["norm_g"], (g_ada_w, d_ada_w, nm_ada_w, nv_ada_w)[k], sm_k["ada_b"], o_win0[k], o_pw[k],
                     sm_k["pool_scale"], o_wout0[k], o_win1[k], sm_k["conv_w"], sm_k["conv_b"], o_wout1[k],
                     sm_k["final_g"]])
    return (loss, dx0[None], *outs[0], *outs[1], *outs[2], *outs[3])
```

```python
import functools

import jax
import jax.numpy as jnp
from jax import lax
from jax.experimental import pallas as pl
from jax.experimental.pallas import tpu as pltpu

F32 = jnp.float32
BF16 = jnp.bfloat16
MESH = pl.DeviceIdType.MESH

D = 1024
DI = 2048
DP = 1024
NE = 6144
NO = 8192
WINDOWS = (2, 4, 8, 16)
PG = 256
HD = 64
NCHIP = 4
NDEV = 8
EPS = 1e-6
INV_SQRT_HD = 0.125

LR, B1, B2, EPS_ADAM, WD, STEP = 0.001, 0.9, 0.999, 1e-08, 0.01, 10

TM = 512
TME = 256
CT = 512
BQ = 512
BK = 256
HALO = 16
DEAD_LOG_WEIGHT = -104.0
VMEM_LIMIT = 56 * 1024 * 1024


def _dot(a, b):
    return jnp.dot(a, b, preferred_element_type=F32)


def _dot_nt(a, b):
    return lax.dot_general(a, b, (((1,), (1,)), ((), ())), preferred_element_type=F32)


def _dot_tn(a, b):
    return lax.dot_general(a, b, (((0,), (0,)), ((), ())), preferred_element_type=F32)


def _params(*sem):
    return pltpu.CompilerParams(dimension_semantics=sem, vmem_limit_bytes=VMEM_LIMIT)


def _rowsum(v):
    return jnp.sum(v, axis=0, keepdims=True)


def _modulated_norm(xv, vec_ref):
    r = lax.rsqrt(jnp.mean(xv * xv, axis=-1, keepdims=True) + EPS)
    return (((xv * r) * vec_ref[0:1, :]) * (1.0 + vec_ref[1:2, :]) + vec_ref[2:3, :]).astype(BF16)


def _norm_mod(x, vecs, name):
    s = x.shape[0]
    tm = min(TM, s)

    def body(x_ref, vec_ref, h_ref):
        h_ref[...] = _modulated_norm(x_ref[...], vec_ref)

    return pl.pallas_call(
        body, name=name, grid=(s // tm,),
        in_specs=[pl.BlockSpec((tm, D), lambda i: (i, 0)), pl.BlockSpec((8, D), lambda i: (0, 0))],
        out_specs=pl.BlockSpec((tm, D), lambda i: (i, 0)),
        out_shape=jax.ShapeDtypeStruct((s, D), BF16),
        compiler_params=_params("parallel"),
    )(x, vecs)


def _inproj(h, w, name, comm=None):
    s = h.shape[0]
    ns = w.shape[2]
    tm = min(TM, s)
    ni = s // tm
    comm = comm or _NO_COMM
    nci, nco = len(comm.arrays), len(comm.out_shape)

    def body(*refs):
        h_ref, w_ref = refs[:2]
        proj_ref = refs[2 + nci]
        cargs = (refs[2:2 + nci], refs[3 + nci:3 + nci + nco], refs[3 + nci + nco:])
        j, i = pl.program_id(0), pl.program_id(1)
        if nci:
            pl.when((j == 0) & (i == 0))(lambda: comm.start(*cargs))
        proj_ref[...] = _dot(h_ref[...], w_ref[...]).astype(BF16)
        if nci:
            pl.when((j == NCHIP - 1) & (i == ni - 1))(lambda: comm.finish(*cargs))

    return pl.pallas_call(
        body, name=name, grid=(NCHIP, ni),
        in_specs=[pl.BlockSpec((tm, D), lambda j, i: (i, 0)),
                  pl.BlockSpec((None, D, ns), lambda j, i: (j, 0, 0))] + [ANY] * nci,
        out_specs=[pl.BlockSpec((tm, ns), lambda j, i: (i, j))] + [ANY] * nco,
        out_shape=[jax.ShapeDtypeStruct((s, NCHIP * ns), BF16)] + comm.out_shape,
        scratch_shapes=comm.sems,
        compiler_params=_params("arbitrary", "arbitrary"),
    )(h, w, *comm.arrays)


def _pool_fwd(proj0, pw, pscale):
    s = proj0.shape[0]
    tm = min(TM, s)
    hb = tm // 16

    def body(u_ref, halo_ref, w_ref, sc_ref, p_ref, y_ref, ext_ref):
        i = pl.program_id(0)
        ext_ref[16:, :] = u_ref[...].astype(F32)
        ext_ref[0:16, :] = jnp.where(i > 0, halo_ref[...].astype(F32), 0.0)
        t = i * tm + lax.broadcasted_iota(jnp.int32, (tm, 1), 0)
        for g, wdw in enumerate(WINDOWS):
            cs = slice(g * PG, (g + 1) * PG)
            u = ext_ref[16:16 + tm, cs]
            acc = u
            for j in range(1, wdw):
                acc = acc + ext_ref[16 - j:16 - j + tm, cs]
            inv = 1.0 / jnp.minimum(t + 1, wdw).astype(F32)
            pb = (acc * inv - u).astype(BF16)
            p_ref[:, cs] = pb
            y_ref[:, cs] = _dot(pb, w_ref[g]) * sc_ref[:, cs]

    return pl.pallas_call(
        body, name="pool_fwd", grid=(s // tm,),
        in_specs=[pl.BlockSpec((tm, DP), lambda i: (i, 0)),
                  pl.BlockSpec((16, DP), lambda i: (jnp.maximum(i * hb - 1, 0), 0)),
                  pl.BlockSpec((4, PG, PG), lambda i: (0, 0, 0)),
                  pl.BlockSpec((1, DP), lambda i: (0, 0))],
        out_specs=[pl.BlockSpec((tm, DP), lambda i: (i, 0)),
                   pl.BlockSpec((tm, DP), lambda i: (i, 0))],
        out_shape=[jax.ShapeDtypeStruct((s, DP), BF16), jax.ShapeDtypeStruct((s, DP), F32)],
        scratch_shapes=[pltpu.VMEM((tm + 16, DP), F32)],
        compiler_params=_params("parallel"),
    )(proj0, proj0, pw, pscale)


def _sb_logits(nz, mask):
    neg_abs = lax.bitcast_convert_type(lax.bitcast_convert_type(nz, jnp.uint32) | jnp.uint32(0x80000000), F32)
    t = jnp.log(1.0 + jnp.exp(neg_abs))
    lf = jnp.minimum(nz, 0.0) - t
    lam = lf - nz
    if mask is not None:
        lf = jnp.where(mask, lf, 0.0)
    return lf, lam


def _sweep_left(steps, state, step):
    def live(carry):
        j, st = carry
        heaviest = functools.reduce(jnp.maximum, [jnp.max(head[1]) for head in st])
        return (j < steps) & (heaviest > DEAD_LOG_WEIGHT)

    return lax.while_loop(live, lambda carry: (carry[0] + 1, step(carry[0], carry[1])), (0, state))[1]


def _split_dot(v, tri):
    hi = v.astype(BF16)
    lo = (v - hi.astype(F32)).astype(BF16)
    return _dot(hi, tri) + _dot(lo, tri)


def _tri_masks():
    row = lax.broadcasted_iota(jnp.int32, (BK, BK), 0)
    col = lax.broadcasted_iota(jnp.int32, (BK, BK), 1)
    return (row > col).astype(BF16), (row >= col).astype(BF16)


def _causal_mask(offset):
    row = lax.broadcasted_iota(jnp.int32, (BQ, BK), 0)
    col = lax.broadcasted_iota(jnp.int32, (BQ, BK), 1)
    return col + offset < row


def _attn_fwd(proj0, comm=None):
    s = proj0.shape[0]
    nq = s // BQ
    kpq = BQ // BK
    nsteps = DP // 128
    comm = comm or _NO_COMM
    nci, nco = len(comm.arrays), len(comm.out_shape)

    def body(*refs):
        q_ref, k_ref, v_ref = refs[:3]
        o_ref = refs[3 + nci]
        qn_ref, k8_ref, vb_ref = refs[4 + nci + nco:7 + nci + nco]
        cargs = (refs[3:3 + nci], refs[4 + nci:4 + nci + nco], refs[7 + nci + nco:])
        if nci:
            pl.when(pl.program_id(0) == 0)(lambda: comm.start(*cargs))
        qn_ref[...] = (-q_ref[...]).astype(BF16)
        k8_ref[...] = (k_ref[...] * INV_SQRT_HD).astype(BF16)
        vb_ref[...] = v_ref[...].astype(BF16)
        after, _ = _tri_masks()
        heads = [slice(HD * h, HD * (h + 1)) for h in range(2)]

        def qstep(qi, carry):
            q0 = pl.multiple_of(qi * BQ, BQ)
            qn = [qn_ref[pl.ds(q0, BQ), ls] for ls in heads]

            def blocks(k_hi, state, masks):
                ks = [pl.multiple_of(k_hi + (kpq - 1 - j) * BK, BK) for j in range(kpq)]
                lanes = [(j, h) for j in range(kpq) for h in range(len(heads))]
                mask = lambda j: None if masks is None else masks[j]
                nz = {jh: _dot_nt(qn[jh[1]], k8_ref[pl.ds(ks[jh[0]], BK), heads[jh[1]]]) for jh in lanes}
                ll = {jh: _sb_logits(nz[jh], mask(jh[0])) for jh in lanes}
                aft = {jh: _dot(ll[jh][0].astype(BF16), after) for jh in lanes}
                new = []
                for h in range(len(heads)):
                    o_acc, c = state[h]
                    for j in range(kpq):
                        a = jnp.exp(ll[j, h][1] + aft[j, h] + c)
                        if masks is not None:
                            a = jnp.where(masks[j], a, 0.0)
                        o_acc = o_acc + _dot(a.astype(BF16), vb_ref[pl.ds(ks[j], BK), heads[h]])
                        c = c + aft[j, h][:, 0:1] + ll[j, h][0][:, 0:1]
                    new.append((o_acc, c))
                return tuple(new)

            state = tuple((jnp.zeros((BQ, HD), F32), jnp.zeros((BQ, 1), F32)) for _ in heads)
            state = blocks(q0, state, [_causal_mask((kpq - 1 - j) * BK) for j in range(kpq)])
            state = _sweep_left(qi, state, lambda j, st: blocks(q0 - (j + 1) * BQ, st, None))
            for h, ls in enumerate(heads):
                o_ref[pl.ds(q0, BQ), ls] = state[h][0]
            return carry

        lax.fori_loop(0, nq, qstep, 0)
        if nci:
            pl.when(pl.program_id(0) == nsteps - 1)(lambda: comm.finish(*cargs))

    return pl.pallas_call(
        body, name="attn_fwd", grid=(nsteps,),
        in_specs=[pl.BlockSpec((s, 128), lambda h: (0, 8 + h)),
                  pl.BlockSpec((s, 128), lambda h: (0, 16 + h)),
                  pl.BlockSpec((s, 128), lambda h: (0, 24 + h))] + [ANY] * nci,
        out_specs=[pl.BlockSpec((s, 128), lambda h: (0, h))] + [ANY] * nco,
        out_shape=[jax.ShapeDtypeStruct((s, DP), F32)] + comm.out_shape,
        scratch_shapes=[pltpu.VMEM((s, 128), BF16)] * 3 + comm.sems,
        compiler_params=_params("arbitrary"),
    )(proj0, proj0, proj0, *comm.arrays)


def _even_out(ypool, ysb, proj0, wout, x, vecs, vecs_next):
    s = x.shape[0]
    tm = min(TME, s)

    def body(yp_ref, ys_ref, gate_ref, w_ref, x_ref, vec_ref, vecn_ref, x1_ref, out_ref, yg_ref, hn_ref):
        gt = gate_ref[...].astype(F32)
        sl = gt * jax.nn.sigmoid(gt)
        yg_ref[:, :DP] = (yp_ref[...] * sl[:, :DP]).astype(BF16)
        yg_ref[:, DP:] = (ys_ref[...] * sl[:, DP:]).astype(BF16)
        out = _dot(yg_ref[...], w_ref[...])
        out_ref[...] = out
        x1 = x_ref[...] + (1.0 + vec_ref[3:4, :]) * out
        x1_ref[...] = x1
        hn_ref[...] = _modulated_norm(x1, vecn_ref)

    row = lambda i: (i, 0)
    const = lambda i: (0, 0)
    return pl.pallas_call(
        body, name="even_out", grid=(s // tm,),
        in_specs=[pl.BlockSpec((tm, DP), row), pl.BlockSpec((tm, DP), row),
                  pl.BlockSpec((tm, DI), lambda i: (i, 2)),
                  pl.BlockSpec((DI, D), const),
                  pl.BlockSpec((tm, D), row), pl.BlockSpec((8, D), const), pl.BlockSpec((8, D), const)],
        out_specs=[pl.BlockSpec((tm, D), row), pl.BlockSpec((tm, D), row), pl.BlockSpec((tm, DI), row),
                   pl.BlockSpec((tm, D), row)],
        out_shape=[jax.ShapeDtypeStruct((s, D), F32), jax.ShapeDtypeStruct((s, D), F32),
                   jax.ShapeDtypeStruct((s, DI), BF16), jax.ShapeDtypeStruct((s, D), BF16)],
        compiler_params=_params("parallel"),
    )(ypool, ysb, proj0, wout, x, vecs, vecs_next)


def _odd_out(proj1, wout, x1, vecs, cw, cb, target):
    s = x1.shape[0]
    tm = min(TME, s)
    hb = tm // HALO

    def body(gb_ref, gc_ref, u_ref, gt_ref, hgc_ref, hu_ref, w_ref, x1_ref, vec_ref, cw_ref, cb_ref, tg_ref,
             dx2_ref, out_ref, y1_ref, acc_ref, ext_ref):
        i = pl.program_id(0)

        @pl.when(i == 0)
        def _():
            acc_ref[...] = jnp.zeros_like(acc_ref)

        ext_ref[HALO:, :] = gc_ref[...].astype(F32) * u_ref[...].astype(F32)
        ext_ref[0:HALO, :] = jnp.where(i > 0, hgc_ref[...].astype(F32) * hu_ref[...].astype(F32), 0.0)
        for c in range(DI // CT):
            cs = slice(c * CT, (c + 1) * CT)
            conv = (cb_ref[0:1, cs] + cw_ref[0:1, cs] * ext_ref[HALO - 2:HALO - 2 + tm, cs]
                    + cw_ref[1:2, cs] * ext_ref[HALO - 1:HALO - 1 + tm, cs]
                    + cw_ref[2:3, cs] * ext_ref[HALO:HALO + tm, cs])
            gt = gt_ref[:, cs].astype(F32)
            y1_ref[:, cs] = (gb_ref[:, cs].astype(F32) * conv * (gt * jax.nn.sigmoid(gt))).astype(BF16)
        out = _dot(y1_ref[...], w_ref[...])
        out_ref[...] = out
        x2 = x1_ref[...] + (1.0 + vec_ref[3:4, :]) * out
        r = lax.rsqrt(jnp.mean(x2 * x2, axis=-1, keepdims=True) + EPS)
        nrm = x2 * r
        fg = vec_ref[4:5, :]
        err = nrm * fg - tg_ref[...]
        acc_ref[1:2, :] += _rowsum(err * err) * (0.5 / D)
        dyf = err * (1.0 / D)
        acc_ref[0:1, :] += _rowsum(dyf * nrm)
        dn = dyf * fg
        dx2_ref[...] = r * (dn - nrm * jnp.mean(dn * nrm, axis=-1, keepdims=True))

    row = lambda i: (i, 0)
    halo = lambda col: (lambda i: (jnp.maximum(i * hb - 1, 0), col))
    const = lambda i: (0, 0)
    return pl.pallas_call(
        body, name="odd_out", grid=(s // tm,),
        in_specs=[pl.BlockSpec((tm, DI), lambda i: (i, 0)), pl.BlockSpec((tm, DI), lambda i: (i, 1)),
                  pl.BlockSpec((tm, DI), lambda i: (i, 2)), pl.BlockSpec((tm, DI), lambda i: (i, 3)),
                  pl.BlockSpec((HALO, DI), halo(1)), pl.BlockSpec((HALO, DI), halo(2)),
                  pl.BlockSpec((DI, D), const), pl.BlockSpec((tm, D), row), pl.BlockSpec((8, D), const),
                  pl.BlockSpec((8, DI), const), pl.BlockSpec((1, DI), const), pl.BlockSpec((tm, D), row)],
        out_specs=[pl.BlockSpec((tm, D), row), pl.BlockSpec((tm, D), row), pl.BlockSpec((tm, DI), row),
                   pl.BlockSpec((8, D), const)],
        out_shape=[jax.ShapeDtypeStruct((s, D), F32), jax.ShapeDtypeStruct((s, D), F32),
                   jax.ShapeDtypeStruct((s, DI), BF16), jax.ShapeDtypeStruct((8, D), F32)],
        scratch_shapes=[pltpu.VMEM((tm + HALO, DI), F32)],
        compiler_params=_params("arbitrary"),
    )(proj1, proj1, proj1, proj1, proj1, proj1, wout, x1, vecs, cw, cb, target)


def _odd_bwd(dx2, out1, proj1, wout, vecs, cw, cb):
    s = dx2.shape[0]
    tm = min(TME, s)
    nb = s // tm
    hb = tm // HALO

    def body(dx2_ref, out1_ref, gb_ref, gc_ref, u_ref, gt_ref, hgc_ref, hu_ref, w_ref, vec_ref, cw_ref, cb_ref,
             dout_ref, dproj_ref, accv_ref, accd_ref, uext_ref, dext_ref, dy_ref):
        i = pl.program_id(0)
        blk = nb - 1 - i

        @pl.when(i == 0)
        def _():
            accv_ref[...] = jnp.zeros_like(accv_ref)
            accd_ref[...] = jnp.zeros_like(accd_ref)
            dext_ref[tm:tm + 8, :] = jnp.zeros((8, DI), F32)

        dx2v = dx2_ref[...]
        accd_ref[0:1, :] += _rowsum(dx2v * out1_ref[...])
        dout = (dx2v * (1.0 + vec_ref[3:4, :])).astype(BF16)
        dout_ref[...] = dout
        dy_ref[...] = _dot_nt(dout, w_ref[...])
        uext_ref[HALO:, :] = gc_ref[...].astype(F32) * u_ref[...].astype(F32)
        uext_ref[0:HALO, :] = jnp.where(blk > 0, hgc_ref[...].astype(F32) * hu_ref[...].astype(F32), 0.0)
        for c in range(DI // CT):
            cs = slice(c * CT, (c + 1) * CT)
            u0 = uext_ref[HALO - 2:HALO - 2 + tm, cs]
            u1 = uext_ref[HALO - 1:HALO - 1 + tm, cs]
            u2 = uext_ref[HALO:HALO + tm, cs]
            w0, w1, w2 = cw_ref[0:1, cs], cw_ref[1:2, cs], cw_ref[2:3, cs]
            conv = cb_ref[0:1, cs] + w0 * u0 + w1 * u1 + w2 * u2
            gt = gt_ref[:, cs].astype(F32)
            sg = jax.nn.sigmoid(gt)
            gb = gb_ref[:, cs].astype(F32)
            dy = dy_ref[:, cs]
            t1 = dy * (gt * sg)
            dproj_ref[:, cs] = (t1 * conv).astype(BF16)
            dconv = t1 * gb
            dproj_ref[:, 3 * DI + c * CT:3 * DI + (c + 1) * CT] = (
                dy * gb * conv * (sg * (1.0 + gt * (1.0 - sg)))).astype(BF16)
            accv_ref[0:1, cs] += _rowsum(dconv * u0)
            accv_ref[1:2, cs] += _rowsum(dconv * u1)
            accv_ref[2:3, cs] += _rowsum(dconv * u2)
            accv_ref[3:4, cs] += _rowsum(dconv)
            dext_ref[0:tm, cs] = dconv
            duu = w2 * dconv + w1 * dext_ref[1:tm + 1, cs] + w0 * dext_ref[2:tm + 2, cs]
            dproj_ref[:, DI + c * CT:DI + (c + 1) * CT] = (duu * u_ref[:, cs].astype(F32)).astype(BF16)
            dproj_ref[:, 2 * DI + c * CT:2 * DI + (c + 1) * CT] = (duu * gc_ref[:, cs].astype(F32)).astype(BF16)
        dext_ref[tm:tm + 8, :] = dext_ref[0:8, :]

    rrow = lambda i: (nb - 1 - i, 0)
    rcol = lambda col: (lambda i: (nb - 1 - i, col))
    halo = lambda col: (lambda i: (jnp.maximum((nb - 1 - i) * hb - 1, 0), col))
    const = lambda i: (0, 0)
    return pl.pallas_call(
        body, name="odd_bwd", grid=(nb,),
        in_specs=[pl.BlockSpec((tm, D), rrow), pl.BlockSpec((tm, D), rrow),
                  pl.BlockSpec((tm, DI), rcol(0)), pl.BlockSpec((tm, DI), rcol(1)),
                  pl.BlockSpec((tm, DI), rcol(2)), pl.BlockSpec((tm, DI), rcol(3)),
                  pl.BlockSpec((HALO, DI), halo(1)), pl.BlockSpec((HALO, DI), halo(2)),
                  pl.BlockSpec((DI, D), const), pl.BlockSpec((8, D), const),
                  pl.BlockSpec((8, DI), const), pl.BlockSpec((1, DI), const)],
        out_specs=[pl.BlockSpec((tm, D), rrow), pl.BlockSpec((tm, NO), rrow),
                   pl.BlockSpec((8, DI), const), pl.BlockSpec((8, D), const)],
        out_shape=[jax.ShapeDtypeStruct((s, D), BF16), jax.ShapeDtypeStruct((s, NO), BF16),
                   jax.ShapeDtypeStruct((8, DI), F32), jax.ShapeDtypeStruct((8, D), F32)],
        scratch_shapes=[pltpu.VMEM((tm + HALO, DI), F32), pltpu.VMEM((tm + 8, DI), F32), pltpu.VMEM((tm, DI), F32)],
        compiler_params=_params("arbitrary"),
    )(dx2, out1, proj1, proj1, proj1, proj1, proj1, proj1, wout, vecs, cw, cb)


def _grad_w_cols(a, b, name):
    s, m = a.shape
    ns = b.shape[1] // NCHIP
    ts = min(TM, s)

    def body(a_ref, b_ref, o_ref):
        @pl.when(pl.program_id(1) == 0)
        def _():
            o_ref[...] = jnp.zeros_like(o_ref)

        o_ref[...] += _dot_tn(a_ref[...], b_ref[...])

    return pl.pallas_call(
        body, name=name, grid=(NCHIP, s // ts),
        in_specs=[pl.BlockSpec((ts, m), lambda j, k: (k, 0)),
                  pl.BlockSpec((ts, ns), lambda j, k: (k, j))],
        out_specs=pl.BlockSpec((None, m, ns), lambda j, k: (j, 0, 0)),
        out_shape=jax.ShapeDtypeStruct((NCHIP, m, ns), F32),
        compiler_params=_params("parallel", "arbitrary"),
    )(a, b)


def _grad_w_rows(a, b, name):
    s = a.shape[0]
    ms = a.shape[1] // NCHIP
    n = b.shape[1]
    ts = min(TM, s)

    def body(a_ref, b_ref, o_ref):
        @pl.when(pl.program_id(1) == 0)
        def _():
            o_ref[...] = jnp.zeros_like(o_ref)

        o_ref[...] += _dot_tn(a_ref[...], b_ref[...])

    return pl.pallas_call(
        body, name=name, grid=(NCHIP, s // ts),
        in_specs=[pl.BlockSpec((ts, ms), lambda i, k: (k, i)),
                  pl.BlockSpec((ts, n), lambda i, k: (k, 0))],
        out_specs=pl.BlockSpec((None, ms, n), lambda i, k: (i, 0, 0)),
        out_shape=jax.ShapeDtypeStruct((NCHIP, ms, n), F32),
        compiler_params=_params("parallel", "arbitrary"),
    )(a, b)


def _inproj_bwd(dproj, w, x, dx_in, vecs, name, comm=None):
    s = x.shape[0]
    ns = w.shape[2]
    tm = min(TME, s)
    ni = s // tm
    comm = comm or _NO_COMM
    nci, nco = len(comm.arrays), len(comm.out_shape)

    def body(*refs):
        dp_ref, w_hbm, x_ref, dxin_ref, vec_ref = refs[:5]
        dx_ref, acc_ref = refs[5 + nci:7 + nci]
        w_ref = refs[7 + nci + nco]
        cargs = (refs[5:5 + nci], refs[7 + nci:7 + nci + nco], refs[8 + nci + nco:])
        i = pl.program_id(0)

        @pl.when(i == 0)
        def _():
            acc_ref[...] = jnp.zeros_like(acc_ref)
            if nci:
                comm.start(*cargs)
            pltpu.sync_copy(w_hbm, w_ref)

        dh = _dot_nt(dp_ref[:, 0:ns], w_ref[0])
        for j in range(1, NCHIP):
            dh = dh + _dot_nt(dp_ref[:, j * ns:(j + 1) * ns], w_ref[j])
        xv = x_ref[...]
        r = lax.rsqrt(jnp.mean(xv * xv, axis=-1, keepdims=True) + EPS)
        nrm = xv * r
        g = vec_ref[0:1, :]
        sc1 = 1.0 + vec_ref[1:2, :]
        dhn = dh * nrm
        acc_ref[0:1, :] += _rowsum(dh)
        acc_ref[1:2, :] += _rowsum(dhn) * g
        acc_ref[2:3, :] += _rowsum(dhn) * sc1
        dn = dh * (g * sc1)
        dx_ref[...] = dxin_ref[...] + r * (dn - nrm * jnp.mean(dn * nrm, axis=-1, keepdims=True))

        if nci:
            pl.when(i == ni - 1)(lambda: comm.finish(*cargs))

    row = lambda i: (i, 0)
    const = lambda i: (0, 0)
    return pl.pallas_call(
        body, name=name, grid=(ni,),
        in_specs=[pl.BlockSpec((tm, NCHIP * ns), row), ANY,
                  pl.BlockSpec((tm, D), row), pl.BlockSpec((tm, D), row), pl.BlockSpec((8, D), const)] + [ANY] * nci,
        out_specs=[pl.BlockSpec((tm, D), row), pl.BlockSpec((8, D), const)] + [ANY] * nco,
        out_shape=[jax.ShapeDtypeStruct((s, D), F32), jax.ShapeDtypeStruct((8, D), F32)] + comm.out_shape,
        scratch_shapes=[pltpu.VMEM(w.shape, BF16)] + comm.sems,
        compiler_params=_params("arbitrary"),
    )(dproj, w, x, dx_in, vecs, *comm.arrays)


def _even_bwd(dx1, out0, ypool, ysb, proj0, wout, vecs):
    s = dx1.shape[0]
    tm = min(TME, s)

    def body(dx1_ref, out0_ref, yp_ref, ys_ref, gate_ref, w_ref, vec_ref,
             dout_ref, dyp_ref, dys_ref, dgt_ref, acc_ref):
        @pl.when(pl.program_id(0) == 0)
        def _():
            acc_ref[...] = jnp.zeros_like(acc_ref)

        dx1v = dx1_ref[...]
        acc_ref[0:1, :] += _rowsum(dx1v * out0_ref[...])
        dout = (dx1v * (1.0 + vec_ref[3:4, :])).astype(BF16)
        dout_ref[...] = dout
        dyg = _dot_nt(dout, w_ref[...])
        gt = gate_ref[...].astype(F32)
        sg = jax.nn.sigmoid(gt)
        sl = gt * sg
        dsl = sg * (1.0 + gt * (1.0 - sg))
        dyp_ref[...] = dyg[:, :DP] * sl[:, :DP]
        dys_ref[...] = dyg[:, DP:] * sl[:, DP:]
        dgt_ref[:, :DP] = (dyg[:, :DP] * yp_ref[...] * dsl[:, :DP]).astype(BF16)
        dgt_ref[:, DP:] = (dyg[:, DP:] * ys_ref[...] * dsl[:, DP:]).astype(BF16)

    row = lambda i: (i, 0)
    const = lambda i: (0, 0)
    return pl.pallas_call(
        body, name="even_bwd", grid=(s // tm,),
        in_specs=[pl.BlockSpec((tm, D), row), pl.BlockSpec((tm, D), row),
                  pl.BlockSpec((tm, DP), row), pl.BlockSpec((tm, DP), row),
                  pl.BlockSpec((tm, DI), lambda i: (i, 2)),
                  pl.BlockSpec((DI, D), const), pl.BlockSpec((8, D), const)],
        out_specs=[pl.BlockSpec((tm, D), row), pl.BlockSpec((tm, DP), row), pl.BlockSpec((tm, DP), row),
                   pl.BlockSpec((tm, DI), row), pl.BlockSpec((8, D), const)],
        out_shape=[jax.ShapeDtypeStruct((s, D), BF16), jax.ShapeDtypeStruct((s, DP), F32),
                   jax.ShapeDtypeStruct((s, DP), F32), jax.ShapeDtypeStruct((s, DI), BF16),
                   jax.ShapeDtypeStruct((8, D), F32)],
        compiler_params=_params("arbitrary"),
    )(dx1, out0, ypool, ysb, proj0, wout, vecs)


def _pool_bwd(dyp, p, pw, pscale):
    s = dyp.shape[0]
    tm = min(TM, s)
    nb = s // tm
    hb = tm // 16

    def body(dy_ref, dyh_ref, p_ref, w_ref, sc_ref, du_ref, dw_ref, acc_ref, ext_ref):
        i = pl.program_id(0)

        @pl.when(i == 0)
        def _():
            dw_ref[...] = jnp.zeros_like(dw_ref)
            acc_ref[...] = jnp.zeros_like(acc_ref)

        t = i * tm + lax.broadcasted_iota(jnp.int32, (tm + 16, 1), 0)
        for g, wdw in enumerate(WINDOWS):
            cs = slice(g * PG, (g + 1) * PG)
            sc = sc_ref[:, cs]
            dy = dy_ref[:, cs]
            dyh = jnp.where(i < nb - 1, dyh_ref[:, cs], 0.0)
            pb = p_ref[:, cs]
            wg = w_ref[g]
            acc_ref[0:1, cs] += _rowsum(dy * _dot(pb, wg))
            dypre = (dy * sc).astype(BF16)
            dw_ref[g] += _dot_tn(pb, dypre)
            dp = _dot_nt(dypre, wg)
            dph = _dot_nt((dyh * sc).astype(BF16), wg)
            inv = 1.0 / jnp.minimum(t + 1, wdw).astype(F32)
            ext_ref[0:tm, cs] = dp * inv[0:tm]
            ext_ref[tm:tm + 16, cs] = dph * inv[tm:tm + 16]
            acc = ext_ref[0:tm, cs]
            for j in range(1, wdw):
                acc = acc + ext_ref[j:j + tm, cs]
            du_ref[:, cs] = (acc - dp).astype(BF16)

    row = lambda i: (i, 0)
    return pl.pallas_call(
        body, name="pool_bwd", grid=(nb,),
        in_specs=[pl.BlockSpec((tm, DP), row),
                  pl.BlockSpec((16, DP), lambda i: (jnp.minimum((i + 1) * hb, s // 16 - 1), 0)),
                  pl.BlockSpec((tm, DP), row),
                  pl.BlockSpec((4, PG, PG), lambda i: (0, 0, 0)),
                  pl.BlockSpec((1, DP), lambda i: (0, 0))],
        out_specs=[pl.BlockSpec((tm, DP), row), pl.BlockSpec((4, PG, PG), lambda i: (0, 0, 0)),
                   pl.BlockSpec((8, DP), lambda i: (0, 0))],
        out_shape=[jax.ShapeDtypeStruct((s, DP), BF16), jax.ShapeDtypeStruct((4, PG, PG), F32),
                   jax.ShapeDtypeStruct((8, DP), F32)],
        scratch_shapes=[pltpu.VMEM((tm + 16, DP), F32)],
        compiler_params=_params("arbitrary"),
    )(dyp, dyp, p, pw, pscale)


def _attn_bwd(proj0, ysb, dys, du, dgate, comm=None):
    s = proj0.shape[0]
    nq = s // BQ
    kpq = BQ // BK
    nsteps = DP // 128
    comm = comm or _NO_COMM
    nci, nco = len(comm.arrays), len(comm.out_shape)

    def body(*refs):
        q_ref, k_ref, v_ref, o_ref, do_ref, du_hbm, dgate_hbm = refs[:7]
        dproj_hbm = refs[7 + nci]
        scratch = refs[8 + nci + nco:]
        qn_ref, k8_ref, vb_ref, dob_ref, dka_ref, dva_ref, dq_ref, dk_ref, dv_ref, side_sems, part_sems = scratch[:11]
        cargs = (refs[7:7 + nci], refs[8 + nci:8 + nci + nco], scratch[11:])
        step = pl.program_id(0)
        side = [pltpu.make_async_copy(du_hbm, dproj_hbm.at[:, pl.ds(0, DP)], side_sems.at[0]),
                pltpu.make_async_copy(dgate_hbm, dproj_hbm.at[:, pl.ds(NE - DI, DI)], side_sems.at[1])]

        @pl.when(step == 0)
        def _():
            if nci:
                comm.start(*cargs)
            for cp in side:
                cp.start()

        qn_ref[...] = (-q_ref[...]).astype(BF16)
        k8_ref[...] = (k_ref[...] * INV_SQRT_HD).astype(BF16)
        vb_ref[...] = v_ref[...].astype(BF16)
        dob_ref[...] = do_ref[...].astype(BF16)
        dka_ref[...] = jnp.zeros_like(dka_ref)
        dva_ref[...] = jnp.zeros_like(dva_ref)
        after, from_on = _tri_masks()
        heads = [slice(HD * h, HD * (h + 1)) for h in range(2)]

        def qstep(qi, carry):
            q0 = pl.multiple_of(qi * BQ, BQ)
            qn = [qn_ref[pl.ds(q0, BQ), ls] for ls in heads]
            do = [dob_ref[pl.ds(q0, BQ), ls] for ls in heads]
            total = [jnp.sum(do[h].astype(F32) * o_ref[pl.ds(q0, BQ), ls], axis=1, keepdims=True)
                     for h, ls in enumerate(heads)]

            def blocks(k_hi, state, masks):
                ks = [pl.multiple_of(k_hi + (kpq - 1 - j) * BK, BK) for j in range(kpq)]
                lanes = [(j, h) for j in range(kpq) for h in range(len(heads))]
                mask = lambda j: None if masks is None else masks[j]
                k8 = {jh: k8_ref[pl.ds(ks[jh[0]], BK), heads[jh[1]]] for jh in lanes}
                nz = {jh: _dot_nt(qn[jh[1]], k8[jh]) for jh in lanes}
                da = {jh: _dot_nt(do[jh[1]], vb_ref[pl.ds(ks[jh[0]], BK), heads[jh[1]]]) for jh in lanes}
                ll = {jh: _sb_logits(nz[jh], mask(jh[0])) for jh in lanes}
                aft = {jh: _dot(ll[jh][0].astype(BF16), after) for jh in lanes}
                new = []
                for h in range(len(heads)):
                    dq_acc, c, cg = state[h]
                    for j in range(kpq):
                        a = jnp.exp(ll[j, h][1] + aft[j, h] + c)
                        if masks is not None:
                            a = jnp.where(masks[j], a, 0.0)
                        ab = a.astype(BF16)
                        g = da[j, h] * ab.astype(F32)
                        suf = _split_dot(g, from_on)
                        dz = g - jnp.exp(ll[j, h][1]) * (g + ((total[h] - cg) - suf))
                        if masks is not None:
                            dz = jnp.where(masks[j], dz, 0.0)
                        dzb = dz.astype(BF16)
                        dka_ref[pl.ds(ks[j], BK), heads[h]] += _dot_tn(dzb, qn[h])
                        dva_ref[pl.ds(ks[j], BK), heads[h]] += _dot_tn(ab, do[h])
                        dq_acc = dq_acc + _dot(dzb, k8[j, h])
                        c = c + aft[j, h][:, 0:1] + ll[j, h][0][:, 0:1]
                        cg = cg + suf[:, 0:1]
                    new.append((dq_acc, c, cg))
                return tuple(new)

            zero = jnp.zeros((BQ, 1), F32)
            state = tuple((jnp.zeros((BQ, HD), F32), zero, zero) for _ in heads)
            state = blocks(q0, state, [_causal_mask((kpq - 1 - j) * BK) for j in range(kpq)])
            state = _sweep_left(qi, state, lambda j, st: blocks(q0 - (j + 1) * BQ, st, None))
            for h, ls in enumerate(heads):
                dq_ref[pl.ds(q0, BQ), ls] = state[h][0].astype(BF16)
            return carry

        lax.fori_loop(0, nq, qstep, 0)
        dk_ref[...] = (dka_ref[...] * (-INV_SQRT_HD)).astype(BF16)
        dv_ref[...] = dva_ref[...].astype(BF16)
        lanes0 = pl.multiple_of(step * 128, 128)
        parts = [pltpu.make_async_copy(src, dproj_hbm.at[:, pl.ds((1 + k) * DP + lanes0, 128)], part_sems.at[k])
                 for k, src in enumerate((dq_ref, dk_ref, dv_ref))]
        for cp in parts:
            cp.start()
        for cp in parts:
            cp.wait()

        @pl.when(step == nsteps - 1)
        def _():
            for cp in side:
                cp.wait()
            if nci:
                comm.finish(*cargs)

    col = lambda h: (0, h)
    dma = pltpu.SemaphoreType.DMA
    return pl.pallas_call(
        body, name="attn_bwd", grid=(nsteps,),
        in_specs=[pl.BlockSpec((s, 128), lambda h: (0, 8 + h)),
                  pl.BlockSpec((s, 128), lambda h: (0, 16 + h)),
                  pl.BlockSpec((s, 128), lambda h: (0, 24 + h)),
                  pl.BlockSpec((s, 128), col), pl.BlockSpec((s, 128), col), ANY, ANY] + [ANY] * nci,
        out_specs=[ANY] * (1 + nco),
        out_shape=[jax.ShapeDtypeStruct((s, NE), BF16)] + comm.out_shape,
        scratch_shapes=([pltpu.VMEM((s, 128), BF16)] * 4 + [pltpu.VMEM((s, 128), F32)] * 2
                        + [pltpu.VMEM((s, 128), BF16)] * 3 + [dma((2,)), dma((3,))] + comm.sems),
        compiler_params=_params("arbitrary"),
    )(proj0, proj0, proj0, ysb, dys, du, dgate, *comm.arrays)


def _adamw_math(w, g, m, v):
    m2 = B1 * m + (1.0 - B1) * g
    v2 = B2 * v + (1.0 - B2) * (g * g)
    m_hat = m2 / (1.0 - B1 ** STEP)
    v_hat = v2 / (1.0 - B2 ** STEP)
    return -LR * (m_hat / (jnp.sqrt(v_hat) + EPS_ADAM) + WD * w), m2, v2


def _adamw(w, g, m, v, name):
    r, c = w.shape
    tr = r
    while tr * c * 4 > (1 << 20) and tr % 16 == 0:
        tr //= 2

    def body(w_ref, g_ref, m_ref, v_ref, d_ref, m2_ref, v2_ref):
        d_ref[...], m2_ref[...], v2_ref[...] = _adamw_math(w_ref[...], g_ref[...], m_ref[...], v_ref[...])

    spec = pl.BlockSpec((tr, c), lambda i: (i, 0))
    return pl.pallas_call(
        body, name=name, grid=(r // tr,),
        in_specs=[spec] * 4, out_specs=[spec] * 3,
        out_shape=[jax.ShapeDtypeStruct((r, c), F32)] * 3,
        compiler_params=_params("parallel"),
    )(w, g, m, v)


def _local_step(x, target, vecs0, vecs1, win0, rest, pscale, cw8, cb, sel=None):
    dist = sel is not None
    h0 = _norm_mod(x, vecs0, "norm0")
    proj0, *got = _inproj(h0, win0, "inproj0", _gather_comm(list(rest[0:2])) if dist else None)
    pw, wout0 = (_group_major(got[0]), got[1].reshape(DI, D)) if dist else rest[0:2]
    p, ypool = _pool_fwd(proj0, pw, pscale)
    ysb, *got = _attn_fwd(proj0, _gather_comm(list(rest[2:4])) if dist else None)
    win1, wout1 = (got[0], got[1].reshape(DI, D)) if dist else rest[2:4]
    x1, out0, yg, h1 = _even_out(ypool, ysb, proj0, wout0, x, vecs0, vecs1)
    proj1, = _inproj(h1, win1, "inproj1")
    dx2, out1, y1, acc_f = _odd_out(proj1, wout1, x1, vecs1, cw8, cb, target)

    def chip_partials(grads, names):
        from_sibling = _send_halves(grads, "rs_send_halves_" + names[0])
        part = [_add_halves(g, t, sel[1:2], "rs_add_halves_" + nm) for g, t, nm in zip(grads, from_sibling, names)]
        return [p32 for p32, _ in part], _exchange_comm([p16 for _, p16 in part])

    dout1, dproj1, acc_cv, acc_g1 = _odd_bwd(dx2, out1, proj1, wout1, vecs1, cw8, cb)
    g_wout1 = _grad_w_rows(y1, dout1, "grad_wout1")
    g_win1 = _grad_w_cols(h1, dproj1, "grad_win1")
    dx1, acc_n1 = _inproj_bwd(dproj1, win1, x1, dx2, vecs1, "inproj1_bwd")

    dout0, dyp, dys, dgate0, acc_g0 = _even_bwd(dx1, out0, ypool, ysb, proj0, wout0, vecs0)
    g_wout0 = _grad_w_rows(yg, dout0, "grad_wout0")
    du, g_pw, acc_ps = _pool_bwd(dyp, p, pw, pscale)
    early = [_chip_major(g_pw, PG // NCHIP), g_wout0, g_win1, g_wout1]
    part_a, swap_a = chip_partials(early, ["pool", "wout0", "win1", "wout1"]) if dist else (None, None)
    dproj0, *got_a = _attn_bwd(proj0, ysb, dys, du, dgate0, swap_a)
    g_win0 = _grad_w_cols(h0, dproj0, "grad_win0")
    part_b, swap_b = chip_partials([g_win0], ["win0"]) if dist else (None, None)
    dx0, acc_n0, *got_b = _inproj_bwd(dproj0, win0, x, dx1, vecs0, "inproj0_bwd", swap_b)

    if dist:
        names = ["win0", "pool", "wout0", "win1", "wout1"]
        halves = [_add_partials(p32, t, sel, "rs_add_partials_" + nm)
                  for p32, t, nm in zip(part_b + part_a, got_b + got_a, names)]
        grads = tuple(_join_halves(halves))
    else:
        grads = (g_win0, g_pw, g_wout0, g_win1, g_wout1)

    sums = dict(
        dm0=jnp.concatenate([acc_n0[0:2], acc_g0[0:1]], axis=0),
        dm1=jnp.concatenate([acc_n1[0:2], acc_g1[0:1]], axis=0),
        norm_g=jnp.concatenate([acc_n0[2:3], acc_n1[2:3]], axis=0),
        pool_scale=acc_ps[0:1], final_g=acc_f[0:1], loss=acc_f[1:2],
        conv_w=acc_cv[0:3], conv_b=acc_cv[3:4])
    return dx0, grads, sums


ANY = pl.BlockSpec(memory_space=pl.ANY)
CHIP_FLIPS = ((1, 0), (0, 1), (1, 1))


def _place():
    return lax.axis_index("x"), lax.axis_index("y"), lax.axis_index("c")


def _flip(v, f):
    return 1 - v if f else v


def _allgather8(v, name):
    m_per, n = v.shape

    def body(x_ref, out_ref, send_sems, recv_sems, local_sem):
        x, y, c = _place()
        me, sibling = (x, y, c), (x, y, 1 - c)
        chips = [(_flip(x, fx), _flip(y, fy)) for fx, fy in CHIP_FLIPS]

        def rows(px, py, pc):
            return out_ref.at[pl.ds((4 * px + 2 * py + pc) * m_per, m_per), :]

        def copy(k, block, to, src=None):
            return pltpu.make_async_remote_copy(
                src_ref=rows(*block) if src is None else src, dst_ref=rows(*block),
                send_sem=send_sems.at[k], recv_sem=recv_sems.at[k], device_id=to, device_id_type=MESH)

        mine = pltpu.make_async_copy(x_ref, rows(*me), local_sem)
        mine.start()
        first = [copy(0, me, sibling, src=x_ref)]
        first += [copy(1 + j, me, (*chip, c), src=x_ref) for j, chip in enumerate(chips)]
        for cp in first:
            cp.start()
        passed = [copy(4 + j, (*chip, c), sibling) for j, chip in enumerate(chips)]
        for j, chip in enumerate(chips):
            copy(1 + j, (*chip, c), me).wait_recv()
            passed[j].start()
        copy(0, sibling, me).wait_recv()
        for j, chip in enumerate(chips):
            copy(4 + j, (*chip, 1 - c), me).wait_recv()
        for cp in first + passed:
            cp.wait_send()
        mine.wait()

    return pl.pallas_call(
        body, name=name,
        out_shape=jax.ShapeDtypeStruct((NDEV * m_per, n), v.dtype),
        in_specs=[pl.BlockSpec(memory_space=pltpu.VMEM)],
        out_specs=pl.BlockSpec(memory_space=pltpu.VMEM),
        scratch_shapes=[pltpu.SemaphoreType.DMA((7,)), pltpu.SemaphoreType.DMA((7,)), pltpu.SemaphoreType.DMA],
    )(v)


class _Comm:
    def __init__(self, arrays, out_shape, sems, start, finish):
        self.arrays, self.out_shape, self.sems, self.start, self.finish = arrays, out_shape, sems, start, finish


_NO_COMM = _Comm([], [], [], None, None)


def _run_comm(comm, name):
    n = len(comm.arrays)

    def body(*refs):
        args = (refs[:n], refs[n:n + len(comm.out_shape)], refs[n + len(comm.out_shape):])
        comm.start(*args)
        comm.finish(*args)

    return pl.pallas_call(
        body, name=name, out_shape=comm.out_shape,
        in_specs=[ANY] * n, out_specs=[ANY] * len(comm.out_shape), scratch_shapes=comm.sems,
    )(*comm.arrays)


def _gather_comm(shards):
    n = len(shards)

    def pieces(ins, outs, sems, kinds):
        x, y, c = _place()
        ici_send, ici_recv, fwd_send, fwd_recv, local = sems
        own, sibling = 2 * x + y, (x, y, 1 - c)
        made = {kind: [] for kind in kinds}
        for w in range(n):
            r2 = ins[w].shape[0] // 2
            mine, other = pl.ds(c * r2, r2), pl.ds((1 - c) * r2, r2)
            if "local" in kinds:
                made["local"].append(pltpu.make_async_copy(ins[w], outs[w].at[own], local.at[w]))
            for d, (fx, fy) in enumerate(CHIP_FLIPS):
                px, py, k = _flip(x, fx), _flip(y, fy), 3 * w + d
                peer = 2 * px + py
                ici = dict(send_sem=ici_send.at[k], recv_sem=ici_recv.at[k], device_id=(px, py, c), device_id_type=MESH)
                fwd = dict(send_sem=fwd_send.at[k], recv_sem=fwd_recv.at[k], device_id=sibling, device_id_type=MESH)
                if "ici_out" in kinds:
                    made["ici_out"].append(pltpu.make_async_remote_copy(
                        src_ref=ins[w].at[mine, :], dst_ref=outs[w].at[own, mine, :], **ici))
                if "ici_in" in kinds:
                    made["ici_in"].append(pltpu.make_async_remote_copy(
                        src_ref=ins[w].at[mine, :], dst_ref=outs[w].at[peer, mine, :], **ici))
                if "fwd_out" in kinds:
                    made["fwd_out"].append(pltpu.make_async_remote_copy(
                        src_ref=outs[w].at[peer, mine, :], dst_ref=outs[w].at[peer, mine, :], **fwd))
                if "fwd_in" in kinds:
                    made["fwd_in"].append(pltpu.make_async_remote_copy(
                        src_ref=outs[w].at[peer, other, :], dst_ref=outs[w].at[peer, other, :], **fwd))
        return made

    def start(ins, outs, sems):
        made = pieces(ins, outs, sems, ("local", "ici_out"))
        for cp in made["local"] + made["ici_out"]:
            cp.start()

    def finish(ins, outs, sems):
        made = pieces(ins, outs, sems, ("local", "ici_out", "ici_in", "fwd_out", "fwd_in"))
        for arrived, onward in zip(made["ici_in"], made["fwd_out"]):
            arrived.wait_recv()
            onward.start()
        for cp in made["fwd_in"]:
            cp.wait_recv()
        for cp in made["ici_out"] + made["fwd_out"]:
            cp.wait_send()
        for cp in made["local"]:
            cp.wait()

    dma = pltpu.SemaphoreType.DMA
    return _Comm(list(shards), [jax.ShapeDtypeStruct((NCHIP,) + a.shape, a.dtype) for a in shards],
                 [dma((3 * n,))] * 4 + [dma((n,))], start, finish)


def _exchange_comm(parts):
    n = len(parts)

    def copies(ins, outs, sems):
        x, y, c = _place()
        send_sems, recv_sems = sems
        out = []
        for w in range(n):
            for d, (fx, fy) in enumerate(CHIP_FLIPS):
                px, py = _flip(x, fx), _flip(y, fy)
                out.append(pltpu.make_async_remote_copy(
                    src_ref=ins[w].at[2 * px + py], dst_ref=outs[w].at[d], send_sem=send_sems.at[3 * w + d],
                    recv_sem=recv_sems.at[3 * w + d], device_id=(px, py, c), device_id_type=MESH))
        return out

    def start(ins, outs, sems):
        for cp in copies(ins, outs, sems):
            cp.start()

    def finish(ins, outs, sems):
        cps = copies(ins, outs, sems)
        for cp in cps:
            cp.wait_recv()
        for cp in cps:
            cp.wait_send()

    dma = pltpu.SemaphoreType.DMA
    return _Comm(list(parts), [jax.ShapeDtypeStruct((3,) + p.shape[1:], BF16) for p in parts],
                 [dma((3 * n,))] * 2, start, finish)


def _send_halves(grads, name):
    n = len(grads)

    def body(*refs):
        ins, outs = refs[:n], refs[n:2 * n]
        send_sems, recv_sems = refs[2 * n:]
        x, y, c = _place()
        copies = []
        for w in range(n):
            r2 = ins[w].shape[1] // 2
            cp = pltpu.make_async_remote_copy(
                src_ref=ins[w].at[:, pl.ds((1 - c) * r2, r2), :], dst_ref=outs[w],
                send_sem=send_sems.at[w], recv_sem=recv_sems.at[w], device_id=(x, y, 1 - c), device_id_type=MESH)
            cp.start()
            copies.append(cp)
        for cp in copies:
            cp.wait_recv()
        for cp in copies:
            cp.wait_send()

    return pl.pallas_call(
        body, name=name,
        out_shape=[jax.ShapeDtypeStruct((NCHIP, g.shape[1] // 2, g.shape[2]), F32) for g in grads],
        in_specs=[ANY] * n, out_specs=[ANY] * n,
        scratch_shapes=[pltpu.SemaphoreType.DMA((n,)), pltpu.SemaphoreType.DMA((n,))],
    )(*grads)


def _row_tile(rows, cols):
    tr = rows
    while tr * cols * 4 > (1 << 20) and tr % 16 == 0:
        tr //= 2
    return tr


def _add_halves(g, t1, core, name):
    _, r, cdim = g.shape
    r2 = r // 2
    tr = _row_tile(r2, cdim)
    nt = r2 // tr

    def body(core_ref, g_ref, t_ref, p_ref, pb_ref):
        p = g_ref[...] + t_ref[...]
        p_ref[...] = p
        pb_ref[...] = p.astype(BF16)

    blk = pl.BlockSpec((None, tr, cdim), lambda j, i, core_ref: (j, i, 0))
    return pl.pallas_call(
        body, name=name,
        grid_spec=pltpu.PrefetchScalarGridSpec(
            num_scalar_prefetch=1, grid=(NCHIP, nt),
            in_specs=[pl.BlockSpec((None, tr, cdim), lambda j, i, core_ref: (j, core_ref[0] * nt + i, 0)), blk],
            out_specs=[blk, blk]),
        out_shape=[jax.ShapeDtypeStruct((NCHIP, r2, cdim), F32), jax.ShapeDtypeStruct((NCHIP, r2, cdim), BF16)],
        compiler_params=_params("parallel", "parallel"),
    )(core, g, t1)


def _add_partials(p, t2, sel, name):
    _, r2, cdim = p.shape
    tr = _row_tile(r2, cdim)
    nt = r2 // tr

    def body(sel_ref, p_ref, t_ref, o_ref):
        o_ref[...] = ((p_ref[...] + t_ref[0].astype(F32)) + t_ref[1].astype(F32)) + t_ref[2].astype(F32)

    return pl.pallas_call(
        body, name=name,
        grid_spec=pltpu.PrefetchScalarGridSpec(
            num_scalar_prefetch=1, grid=(nt,),
            in_specs=[pl.BlockSpec((None, tr, cdim), lambda i, sel_ref: (sel_ref[0], i, 0)),
                      pl.BlockSpec((3, tr, cdim), lambda i, sel_ref: (0, i, 0))],
            out_specs=pl.BlockSpec((tr, cdim), lambda i, sel_ref: (sel_ref[1] * nt + i, 0))),
        out_shape=jax.ShapeDtypeStruct((2 * r2, cdim), F32),
        compiler_params=_params("parallel"),
    )(sel, p, t2)


def _join_halves(grads):
    n = len(grads)

    def body(*refs):
        bufs = refs[n:2 * n]
        send_sems, recv_sems = refs[2 * n:]
        x, y, c = _place()
        copies = []
        for w in range(n):
            r2 = bufs[w].shape[0] // 2
            mine = bufs[w].at[pl.ds(c * r2, r2), :]
            cp = pltpu.make_async_remote_copy(
                src_ref=mine, dst_ref=mine, send_sem=send_sems.at[w], recv_sem=recv_sems.at[w],
                device_id=(x, y, 1 - c), device_id_type=MESH)
            cp.start()
            copies.append(cp)
        for w in range(n):
            r2 = bufs[w].shape[0] // 2
            theirs = bufs[w].at[pl.ds((1 - c) * r2, r2), :]
            pltpu.make_async_remote_copy(
                src_ref=theirs, dst_ref=theirs, send_sem=send_sems.at[w], recv_sem=recv_sems.at[w],
                device_id=(x, y, 1 - c), device_id_type=MESH).wait_recv()
        for cp in copies:
            cp.wait_send()

    return pl.pallas_call(
        body, name="rs_join_halves",
        out_shape=[jax.ShapeDtypeStruct(g.shape, F32) for g in grads],
        in_specs=[ANY] * n, out_specs=[ANY] * n, input_output_aliases={w: w for w in range(n)},
        scratch_shapes=[pltpu.SemaphoreType.DMA((n,)), pltpu.SemaphoreType.DMA((n,))],
    )(*grads)


def _ada_fwd(c_all, ada_w):
    nl, _, ns = ada_w.shape

    def body(c_ref, w_ref, o_ref):
        cv = c_ref[...]
        o_ref[...] = _dot((cv * jax.nn.sigmoid(cv)).astype(BF16), w_ref[...].astype(BF16))

    return pl.pallas_call(
        body, name="ada_fwd", grid=(nl,),
        in_specs=[pl.BlockSpec((NDEV, D), lambda i: (0, 0)), pl.BlockSpec((None, D, ns), lambda i: (i, 0, 0))],
        out_specs=pl.BlockSpec((None, NDEV, ns), lambda i: (i, 0, 0)),
        out_shape=jax.ShapeDtypeStruct((nl, NDEV, ns), F32),
        compiler_params=_params("parallel"),
    )(c_all, ada_w)


PACK_ROWS = 24


def _reduce_packed(gathered):
    def body(g_ref, tot_ref, loss_ref):
        tot = g_ref[0:PACK_ROWS, :]
        for dev in range(1, NDEV):
            tot = tot + g_ref[dev * PACK_ROWS:(dev + 1) * PACK_ROWS, :]
        tot_ref[...] = tot
        loss_ref[...] = jnp.zeros((8, 128), F32) + jnp.sum(tot[10:11, :])

    return pl.pallas_call(
        body, name="reduce_packed",
        out_shape=[jax.ShapeDtypeStruct((PACK_ROWS, D), F32), jax.ShapeDtypeStruct((8, 128), F32)],
    )(gathered)


def _ada_w_update(c_t, dms, w, m, v):
    nl, _, ns = w.shape
    tr = 256

    def body(ct_ref, dm_ref, w_ref, m_ref, v_ref, g_ref, d_ref, m2_ref, v2_ref):
        ct = ct_ref[...]
        sc = ct * jax.nn.sigmoid(ct)
        dm = dm_ref[...]
        g = sc[:, 0:1] * dm[0:1, :]
        for b in range(1, NDEV):
            g = g + sc[:, b:b + 1] * dm[b:b + 1, :]
        g_ref[...] = g
        d_ref[...], m2_ref[...], v2_ref[...] = _adamw_math(w_ref[...], g, m_ref[...], v_ref[...])

    blk = pl.BlockSpec((None, tr, ns), lambda i, j: (i, j, 0))
    return pl.pallas_call(
        body, name="ada_w_update", grid=(nl, D // tr),
        in_specs=[pl.BlockSpec((tr, NDEV), lambda i, j: (j, 0)),
                  pl.BlockSpec((None, NDEV, ns), lambda i, j: (i, 0, 0)), blk, blk, blk],
        out_specs=[blk] * 4,
        out_shape=[jax.ShapeDtypeStruct((nl, D, ns), F32)] * 4,
        compiler_params=_params("parallel", "parallel"),
    )(c_t, dms, w, m, v)


def _chip_major(a, parts):
    g, _, cdim = a.shape
    return jnp.transpose(a.reshape(g, NCHIP, parts, cdim), (1, 0, 2, 3)).reshape(NCHIP, g * parts, cdim)


def _group_major(a):
    return jnp.transpose(a.reshape(NCHIP, 4, PG // NCHIP, PG), (1, 0, 2, 3)).reshape(4, PG, PG)


def kernel(x, c, norm_g, ada_w, ada_b, even_w_in, pool_w, pool_scale, even_w_out, odd_w_in, conv_w, conv_b, odd_w_out, final_g, loss_target, m_norm_g, m_ada_w, m_ada_b, m_even_w_in, m_pool_w, m_pool_scale, m_even_w_out, m_odd_w_in, m_conv_w, m_conv_b, m_odd_w_out, m_final_g, v_norm_g, v_ada_w, v_ada_b, v_even_w_in, v_pool_w, v_pool_scale, v_even_w_out, v_odd_w_in, v_conv_w, v_conv_b, v_odd_w_out, v_final_g):
    ix, iy, ic = _place()
    chip = 2 * ix + iy
    batch = 2 * chip + ic
    sel = jnp.stack([chip, ic]).astype(jnp.int32)
    ns_ada = ada_w.shape[2]
    ns_conv = conv_b.shape[1]

    conv_rows = jnp.pad(jnp.concatenate([conv_w[0], conv_b], axis=0), ((0, 3), (0, D - ns_conv)))
    first = _allgather8(jnp.concatenate([c, conv_rows], axis=0), "gather_c_conv").reshape(NCHIP, 2, 8, D)
    c_all = first[:, :, 0].reshape(NDEV, D)
    cw_full = jnp.transpose(first[:, 0, 1:5, 0:ns_conv], (1, 0, 2)).reshape(4, DI)
    cw8 = jnp.concatenate([cw_full[0:3], jnp.zeros((5, DI), F32)], axis=0)
    cb_full = cw_full[3:4]

    m_cols = _allgather8(_ada_fwd(c_all, ada_w).reshape(2 * NDEV, ns_ada), "gather_ada")
    m_cols = m_cols.reshape(NCHIP, 2, 2, NDEV, ns_ada)[:, 0]
    m_mine = lax.dynamic_index_in_dim(m_cols, batch, axis=2, keepdims=False)
    m_mine = jnp.transpose(m_mine, (1, 0, 2)).reshape(2, 3 * D) + ada_b
    zrow = jnp.zeros((3, D), F32)

    def vec_rows(i):
        sh, sc, gt = m_mine[i, 0:D], m_mine[i, D:2 * D], m_mine[i, 2 * D:3 * D]
        return jnp.concatenate([jnp.stack([norm_g[i], sc, sh, gt, final_g]), zrow], axis=0)

    win0, = _run_comm(_gather_comm([even_w_in[0].astype(BF16)]), "gather_win0")
    shards = (pool_w[0].astype(BF16).reshape(PG, PG), even_w_out[0].astype(BF16),
              odd_w_in[0].astype(BF16), odd_w_out[0].astype(BF16))
    dx0, grads, sums = _local_step(
        x[0], loss_target[0], vec_rows(0), vec_rows(1), win0, shards, pool_scale, cw8, cb_full, sel)
    r_win0, r_pw, r_wout0, r_win1, r_wout1 = grads

    packed = jnp.concatenate([
        sums["dm0"], sums["dm1"], sums["norm_g"], sums["pool_scale"], sums["final_g"], sums["loss"],
        sums["conv_w"].reshape(6, D), sums["conv_b"].reshape(2, D), jnp.zeros((PACK_ROWS - 19, D), F32)], axis=0)
    gathered = _allgather8(packed, "gather_sums")
    tot, loss8 = _reduce_packed(gathered)
    loss = loss8[0, 0]
    g_norm_g, g_pool_scale, g_final_g = tot[6:8], tot[8:9], tot[9]
    g_ada_b = tot[0:6].reshape(2, 3 * D)
    g_conv_w = lax.dynamic_slice_in_dim(tot[11:17].reshape(3, DI), chip * (DI // NCHIP), DI // NCHIP, axis=1)
    g_conv_b = lax.dynamic_slice_in_dim(tot[17:19].reshape(1, DI), chip * (DI // NCHIP), DI // NCHIP, axis=1)
    dm_all = gathered.reshape(NDEV, PACK_ROWS, D)[:, 0:6].reshape(NDEV, 2, 3 * D)
    dm_cols = jnp.transpose(lax.dynamic_slice_in_dim(dm_all, chip * ns_ada, ns_ada, axis=2), (1, 0, 2))
    g_ada_w, d_ada_w, nm_ada_w, nv_ada_w = _ada_w_update(jnp.transpose(c_all), dm_cols, ada_w, m_ada_w, v_ada_w)

    def upd(w, g, m, v, name):
        shape = w.shape
        w2, m2, v2 = (a.reshape(g.shape) for a in (w, m, v))
        d, nm, nv = _adamw(w2, g, m2, v2, name)
        return g.reshape(shape), d.reshape(shape), nm.reshape(shape), nv.reshape(shape)

    o_win0 = upd(even_w_in, r_win0, m_even_w_in, v_even_w_in, "adamw_win0")
    o_pw = upd(pool_w, r_pw, m_pool_w, v_pool_w, "adamw_pool")
    o_wout0 = upd(even_w_out, r_wout0, m_even_w_out, v_even_w_out, "adamw_wout0")
    o_win1 = upd(odd_w_in, r_win1, m_odd_w_in, v_odd_w_in, "adamw_win1")
    o_wout1 = upd(odd_w_out, r_wout1, m_odd_w_out, v_odd_w_out, "adamw_wout1")

    def pack_small(ng, ab, ps, fg, cwv, cbv):
        conv = jnp.concatenate([cwv.reshape(3, -1), cbv.reshape(1, -1)], axis=0).reshape(2, D)
        return jnp.concatenate([ng, ab.reshape(6, D), ps, fg.reshape(1, D), conv, jnp.zeros((4, D), F32)], axis=0)

    sw = pack_small(norm_g, ada_b, pool_scale, final_g, conv_w, conv_b)
    sg = pack_small(g_norm_g, g_ada_b, g_pool_scale, g_final_g, g_conv_w, g_conv_b)
    sm = pack_small(m_norm_g, m_ada_b, m_pool_scale, m_final_g, m_conv_w, m_conv_b)
    sv = pack_small(v_norm_g, v_ada_b, v_pool_scale, v_final_g, v_conv_w, v_conv_b) + jnp.concatenate(
        [jnp.zeros((12, D), F32), jnp.ones((4, D), F32)], axis=0)
    small = _adamw(sw, sg, sm, sv, "adamw_small")

    def unpack_small(a):
        conv = a[10:12].reshape(4, -1)
        return dict(norm_g=a[0:2], ada_b=a[2:8].reshape(2, 3 * D), pool_scale=a[8:9], final_g=a[9],
                    conv_w=conv[0:3].reshape(conv_w.shape), conv_b=conv[3:4].reshape(conv_b.shape))

    s_grad = dict(norm_g=g_norm_g, ada_b=g_ada_b, pool_scale=g_pool_scale, final_g=g_final_g,
                  conv_w=g_conv_w.reshape(conv_w.shape), conv_b=g_conv_b.reshape(conv_b.shape))
    s_out = [s_grad] + [unpack_small(a) for a in small]

    outs = []
    for k in range(4):
        sm_k = s_out[k]
        outs.append([sm_k["norm_g"], (g_ada_w, d_ada_w, nm_ada_w, nv_ada_w)[k], sm_k["ada_b"], o_win0[k], o_pw[k],
                     sm_k["pool_scale"], o_wout0[k], o_win1[k], sm_k["conv_w"], sm_k["conv_b"], o_wout1[k],
                     sm_k["final_g"]])
    return (loss, dx0[None], *outs[0], *outs[1], *outs[2], *outs[3])
```

```python
import functools

import jax
import jax.numpy as jnp
from jax import lax
from jax.experimental import pallas as pl
from jax.experimental.pallas import tpu as pltpu

F32 = jnp.float32
BF16 = jnp.bfloat16
MESH = pl.DeviceIdType.MESH

D = 1024
DI = 2048
DP = 1024
NE = 6144
NO = 8192
WINDOWS = (2, 4, 8, 16)
PG = 256
HD = 64
NCHIP = 4
NDEV = 8
EPS = 1e-6
INV_SQRT_HD = 0.125

LR, B1, B2, EPS_ADAM, WD, STEP = 0.001, 0.9, 0.999, 1e-08, 0.01, 10

TM = 512
TME = 256
CT = 512
BQ = 512
BK = 256
assert BQ == 2 * BK
HALO = 16
DEAD_LOG_WEIGHT = -104.0
VMEM_LIMIT = 56 * 1024 * 1024


def _dot(a, b):
    return jnp.dot(a, b, preferred_element_type=F32)


def _dot_nt(a, b):
    return lax.dot_general(a, b, (((1,), (1,)), ((), ())), preferred_element_type=F32)


def _dot_tn(a, b):
    return lax.dot_general(a, b, (((0,), (0,)), ((), ())), preferred_element_type=F32)


def _params(*sem):
    return pltpu.CompilerParams(dimension_semantics=sem, vmem_limit_bytes=VMEM_LIMIT)


def _rowsum(v):
    return jnp.sum(v, axis=0, keepdims=True)


def _modulated_norm(xv, vec_ref):
    r = lax.rsqrt(jnp.mean(xv * xv, axis=-1, keepdims=True) + EPS)
    return (((xv * r) * vec_ref[0:1, :]) * (1.0 + vec_ref[1:2, :]) + vec_ref[2:3, :]).astype(BF16)


def _norm_mod(x, vecs, name):
    s = x.shape[0]
    tm = min(TM, s)

    def body(x_ref, vec_ref, h_ref):
        h_ref[...] = _modulated_norm(x_ref[...], vec_ref)

    return pl.pallas_call(
        body, name=name, grid=(s // tm,),
        in_specs=[pl.BlockSpec((tm, D), lambda i: (i, 0)), pl.BlockSpec((8, D), lambda i: (0, 0))],
        out_specs=pl.BlockSpec((tm, D), lambda i: (i, 0)),
        out_shape=jax.ShapeDtypeStruct((s, D), BF16),
        compiler_params=_params("parallel"),
    )(x, vecs)


def _inproj(h, w, name, comm=None):
    s = h.shape[0]
    ns = w.shape[2]
    tm = min(TM, s)
    ni = s // tm
    comm = comm or _NO_COMM
    nci, nco = len(comm.arrays), len(comm.out_shape)

    def body(*refs):
        h_ref, w_ref = refs[:2]
        proj_ref = refs[2 + nci]
        cargs = (refs[2:2 + nci], refs[3 + nci:3 + nci + nco], refs[3 + nci + nco:])
        j, i = pl.program_id(0), pl.program_id(1)
        if nci:
            pl.when((j == 0) & (i == 0))(lambda: comm.start(*cargs))
        proj_ref[...] = _dot(h_ref[...], w_ref[...]).astype(BF16)
        if nci:
            pl.when((j == NCHIP - 1) & (i == ni - 1))(lambda: comm.finish(*cargs))

    return pl.pallas_call(
        body, name=name, grid=(NCHIP, ni),
        in_specs=[pl.BlockSpec((tm, D), lambda j, i: (i, 0)),
                  pl.BlockSpec((None, D, ns), lambda j, i: (j, 0, 0))] + [ANY] * nci,
        out_specs=[pl.BlockSpec((tm, ns), lambda j, i: (i, j))] + [ANY] * nco,
        out_shape=[jax.ShapeDtypeStruct((s, NCHIP * ns), BF16)] + comm.out_shape,
        scratch_shapes=comm.sems,
        compiler_params=_params("arbitrary", "arbitrary"),
    )(h, w, *comm.arrays)


def _pool_fwd(proj0, pw, pscale):
    s = proj0.shape[0]
    tm = min(TM, s)
    hb = tm // 16

    def body(u_ref, halo_ref, w_ref, sc_ref, p_ref, y_ref, ext_ref):
        i = pl.program_id(0)
        ext_ref[16:, :] = u_ref[...].astype(F32)
        ext_ref[0:16, :] = jnp.where(i > 0, halo_ref[...].astype(F32), 0.0)
        t = i * tm + lax.broadcasted_iota(jnp.int32, (tm, 1), 0)
        for g, wdw in enumerate(WINDOWS):
            cs = slice(g * PG, (g + 1) * PG)
            u = ext_ref[16:16 + tm, cs]
            acc = u
            for j in range(1, wdw):
                acc = acc + ext_ref[16 - j:16 - j + tm, cs]
            inv = 1.0 / jnp.minimum(t + 1, wdw).astype(F32)
            pb = (acc * inv - u).astype(BF16)
            p_ref[:, cs] = pb
            y_ref[:, cs] = _dot(pb, w_ref[g]) * sc_ref[:, cs]

    return pl.pallas_call(
        body, name="pool_fwd", grid=(s // tm,),
        in_specs=[pl.BlockSpec((tm, DP), lambda i: (i, 0)),
                  pl.BlockSpec((16, DP), lambda i: (jnp.maximum(i * hb - 1, 0), 0)),
                  pl.BlockSpec((4, PG, PG), lambda i: (0, 0, 0)),
                  pl.BlockSpec((1, DP), lambda i: (0, 0))],
        out_specs=[pl.BlockSpec((tm, DP), lambda i: (i, 0)),
                   pl.BlockSpec((tm, DP), lambda i: (i, 0))],
        out_shape=[jax.ShapeDtypeStruct((s, DP), BF16), jax.ShapeDtypeStruct((s, DP), F32)],
        scratch_shapes=[pltpu.VMEM((tm + 16, DP), F32)],
        compiler_params=_params("parallel"),
    )(proj0, proj0, pw, pscale)


def _sb_logits(nz, mask):
    neg_abs = lax.bitcast_convert_type(lax.bitcast_convert_type(nz, jnp.uint32) | jnp.uint32(0x80000000), F32)
    t = jnp.log(1.0 + jnp.exp(neg_abs))
    lf = jnp.minimum(nz, 0.0) - t
    lam = lf - nz
    if mask is not None:
        lf = jnp.where(mask, lf, 0.0)
    return lf, lam


def _sweep_left(steps, state, step):
    def live(carry):
        j, st = carry
        heaviest = functools.reduce(jnp.maximum, [jnp.max(head[1]) for head in st])
        return (j < steps) & (heaviest > DEAD_LOG_WEIGHT)

    return lax.while_loop(live, lambda carry: (carry[0] + 1, step(carry[0], carry[1])), (0, state))[1]


def _split_dot(v, tri):
    hi = v.astype(BF16)
    lo = (v - hi.astype(F32)).astype(BF16)
    return _dot(hi, tri) + _dot(lo, tri)


def _tri_masks():
    row = lax.broadcasted_iota(jnp.int32, (BK, BK), 0)
    col = lax.broadcasted_iota(jnp.int32, (BK, BK), 1)
    return (row > col).astype(BF16), (row >= col).astype(BF16)


def _causal_mask():
    row = lax.broadcasted_iota(jnp.int32, (BK, BK), 0)
    col = lax.broadcasted_iota(jnp.int32, (BK, BK), 1)
    return col < row


def _attn_fwd(proj0, comm=None):
    s = proj0.shape[0]
    nq = s // BQ
    kpq = BQ // BK
    nsteps = DP // 128
    comm = comm or _NO_COMM
    nci, nco = len(comm.arrays), len(comm.out_shape)

    def body(*refs):
        q_ref, k_ref, v_ref = refs[:3]
        o_ref = refs[3 + nci]
        qn_ref, k8_ref, vb_ref = refs[4 + nci + nco:7 + nci + nco]
        cargs = (refs[3:3 + nci], refs[4 + nci:4 + nci + nco], refs[7 + nci + nco:])
        if nci:
            pl.when(pl.program_id(0) == 0)(lambda: comm.start(*cargs))
        qn_ref[...] = (-q_ref[...]).astype(BF16)
        k8_ref[...] = (k_ref[...] * INV_SQRT_HD).astype(BF16)
        vb_ref[...] = v_ref[...].astype(BF16)
        after, _ = _tri_masks()
        causal = _causal_mask()
        heads = [slice(HD * h, HD * (h + 1)) for h in range(2)]
        hs = range(len(heads))
        lo, hi, full = (0, BK), (BK, BK), (0, BQ)

        def qstep(qi, carry):
            q0 = pl.multiple_of(qi * BQ, BQ)
            qn = [qn_ref[pl.ds(q0, BQ), ls] for ls in heads]

            def sweep(lanes, state):
                nz = [_dot_nt(qn[h][r0:r0 + n], k8_ref[pl.ds(k0, BK), heads[h]]) for h, (r0, n), k0, _ in lanes]
                ll = [_sb_logits(nz[i], lane[3]) for i, lane in enumerate(lanes)]
                aft = [_dot(ll[i][0].astype(BF16), after) for i in range(len(lanes))]
                state = dict(state)
                for i, (h, rows, k0, mask) in enumerate(lanes):
                    o_acc, c = state[h, rows]
                    a = jnp.exp(ll[i][1] + aft[i] + c)
                    if mask is not None:
                        a = jnp.where(mask, a, 0.0)
                    o_acc = o_acc + _dot(a.astype(BF16), vb_ref[pl.ds(k0, BK), heads[h]])
                    state[h, rows] = (o_acc, c + aft[i][:, 0:1] + ll[i][0][:, 0:1])
                return state

            k_lo, k_hi = q0, pl.multiple_of(q0 + BK, BK)
            st = {(h, r): (jnp.zeros((BK, HD), F32), jnp.zeros((BK, 1), F32)) for h in hs for r in (lo, hi)}
            st = sweep([(h, hi, k_hi, causal) for h in hs] + [(h, lo, k_lo, causal) for h in hs]
                       + [(h, hi, k_lo, None) for h in hs], st)
            state = tuple(tuple(jnp.concatenate([st[h, lo][i], st[h, hi][i]], axis=0) for i in range(2)) for h in hs)

            def left(j, st):
                k1 = pl.multiple_of(q0 - (j + 1) * BQ, BQ)
                got = sweep([(h, full, k1 + BK, None) for h in hs] + [(h, full, k1, None) for h in hs],
                            {(h, full): st[h] for h in hs})
                return tuple(got[h, full] for h in hs)

            state = _sweep_left(qi, state, left)
            for h, ls in enumerate(heads):
                o_ref[pl.ds(q0, BQ), ls] = state[h][0]
            return carry

        lax.fori_loop(0, nq, qstep, 0)
        if nci:
            pl.when(pl.program_id(0) == nsteps - 1)(lambda: comm.finish(*cargs))

    return pl.pallas_call(
        body, name="attn_fwd", grid=(nsteps,),
        in_specs=[pl.BlockSpec((s, 128), lambda h: (0, 8 + h)),
                  pl.BlockSpec((s, 128), lambda h: (0, 16 + h)),
                  pl.BlockSpec((s, 128), lambda h: (0, 24 + h))] + [ANY] * nci,
        out_specs=[pl.BlockSpec((s, 128), lambda h: (0, h))] + [ANY] * nco,
        out_shape=[jax.ShapeDtypeStruct((s, DP), F32)] + comm.out_shape,
        scratch_shapes=[pltpu.VMEM((s, 128), BF16)] * 3 + comm.sems,
        compiler_params=_params("arbitrary"),
    )(proj0, proj0, proj0, *comm.arrays)


def _even_out(ypool, ysb, proj0, wout, x, vecs, vecs_next):
    s = x.shape[0]
    tm = min(TME, s)

    def body(yp_ref, ys_ref, gate_ref, w_ref, x_ref, vec_ref, vecn_ref, x1_ref, out_ref, yg_ref, hn_ref):
        gt = gate_ref[...].astype(F32)
        sl = gt * jax.nn.sigmoid(gt)
        yg_ref[:, :DP] = (yp_ref[...] * sl[:, :DP]).astype(BF16)
        yg_ref[:, DP:] = (ys_ref[...] * sl[:, DP:]).astype(BF16)
        out = _dot(yg_ref[...], w_ref[...])
        out_ref[...] = out
        x1 = x_ref[...] + (1.0 + vec_ref[3:4, :]) * out
        x1_ref[...] = x1
        hn_ref[...] = _modulated_norm(x1, vecn_ref)

    row = lambda i: (i, 0)
    const = lambda i: (0, 0)
    return pl.pallas_call(
        body, name="even_out", grid=(s // tm,),
        in_specs=[pl.BlockSpec((tm, DP), row), pl.BlockSpec((tm, DP), row),
                  pl.BlockSpec((tm, DI), lambda i: (i, 2)),
                  pl.BlockSpec((DI, D), const),
                  pl.BlockSpec((tm, D), row), pl.BlockSpec((8, D), const), pl.BlockSpec((8, D), const)],
        out_specs=[pl.BlockSpec((tm, D), row), pl.BlockSpec((tm, D), row), pl.BlockSpec((tm, DI), row),
                   pl.BlockSpec((tm, D), row)],
        out_shape=[jax.ShapeDtypeStruct((s, D), F32), jax.ShapeDtypeStruct((s, D), F32),
                   jax.ShapeDtypeStruct((s, DI), BF16), jax.ShapeDtypeStruct((s, D), BF16)],
        compiler_params=_params("parallel"),
    )(ypool, ysb, proj0, wout, x, vecs, vecs_next)


def _odd_out(proj1, wout, x1, vecs, cw, cb, target):
    s = x1.shape[0]
    tm = min(TME, s)
    hb = tm // HALO

    def body(gb_ref, gc_ref, u_ref, gt_ref, hgc_ref, hu_ref, w_ref, x1_ref, vec_ref, cw_ref, cb_ref, tg_ref,
             dx2_ref, out_ref, y1_ref, acc_ref, ext_ref):
        i = pl.program_id(0)

        @pl.when(i == 0)
        def _():
            acc_ref[...] = jnp.zeros_like(acc_ref)

        ext_ref[HALO:, :] = gc_ref[...].astype(F32) * u_ref[...].astype(F32)
        ext_ref[0:HALO, :] = jnp.where(i > 0, hgc_ref[...].astype(F32) * hu_ref[...].astype(F32), 0.0)
        for c in range(DI // CT):
            cs = slice(c * CT, (c + 1) * CT)
            conv = (cb_ref[0:1, cs] + cw_ref[0:1, cs] * ext_ref[HALO - 2:HALO - 2 + tm, cs]
                    + cw_ref[1:2, cs] * ext_ref[HALO - 1:HALO - 1 + tm, cs]
                    + cw_ref[2:3, cs] * ext_ref[HALO:HALO + tm, cs])
            gt = gt_ref[:, cs].astype(F32)
            y1_ref[:, cs] = (gb_ref[:, cs].astype(F32) * conv * (gt * jax.nn.sigmoid(gt))).astype(BF16)
        out = _dot(y1_ref[...], w_ref[...])
        out_ref[...] = out
        x2 = x1_ref[...] + (1.0 + vec_ref[3:4, :]) * out
        r = lax.rsqrt(jnp.mean(x2 * x2, axis=-1, keepdims=True) + EPS)
        nrm = x2 * r
        fg = vec_ref[4:5, :]
        err = nrm * fg - tg_ref[...]
        acc_ref[1:2, :] += _rowsum(err * err) * (0.5 / D)
        dyf = err * (1.0 / D)
        acc_ref[0:1, :] += _rowsum(dyf * nrm)
        dn = dyf * fg
        dx2_ref[...] = r * (dn - nrm * jnp.mean(dn * nrm, axis=-1, keepdims=True))

    row = lambda i: (i, 0)
    halo = lambda col: (lambda i: (jnp.maximum(i * hb - 1, 0), col))
    const = lambda i: (0, 0)
    return pl.pallas_call(
        body, name="odd_out", grid=(s // tm,),
        in_specs=[pl.BlockSpec((tm, DI), lambda i: (i, 0)), pl.BlockSpec((tm, DI), lambda i: (i, 1)),
                  pl.BlockSpec((tm, DI), lambda i: (i, 2)), pl.BlockSpec((tm, DI), lambda i: (i, 3)),
                  pl.BlockSpec((HALO, DI), halo(1)), pl.BlockSpec((HALO, DI), halo(2)),
                  pl.BlockSpec((DI, D), const), pl.BlockSpec((tm, D), row), pl.BlockSpec((8, D), const),
                  pl.BlockSpec((8, DI), const), pl.BlockSpec((1, DI), const), pl.BlockSpec((tm, D), row)],
        out_specs=[pl.BlockSpec((tm, D), row), pl.BlockSpec((tm, D), row), pl.BlockSpec((tm, DI), row),
                   pl.BlockSpec((8, D), const)],
        out_shape=[jax.ShapeDtypeStruct((s, D), F32), jax.ShapeDtypeStruct((s, D), F32),
                   jax.ShapeDtypeStruct((s, DI), BF16), jax.ShapeDtypeStruct((8, D), F32)],
        scratch_shapes=[pltpu.VMEM((tm + HALO, DI), F32)],
        compiler_params=_params("arbitrary"),
    )(proj1, proj1, proj1, proj1, proj1, proj1, wout, x1, vecs, cw, cb, target)


def _odd_bwd(dx2, out1, proj1, wout, vecs, cw, cb):
    s = dx2.shape[0]
    tm = min(TME, s)
    nb = s // tm
    hb = tm // HALO

    def body(dx2_ref, out1_ref, gb_ref, gc_ref, u_ref, gt_ref, hgc_ref, hu_ref, w_ref, vec_ref, cw_ref, cb_ref,
             dout_ref, dproj_ref, accv_ref, accd_ref, uext_ref, dext_ref, dy_ref):
        i = pl.program_id(0)
        blk = nb - 1 - i

        @pl.when(i == 0)
        def _():
            accv_ref[...] = jnp.zeros_like(accv_ref)
            accd_ref[...] = jnp.zeros_like(accd_ref)
            dext_ref[tm:tm + 8, :] = jnp.zeros((8, DI), F32)

        dx2v = dx2_ref[...]
        accd_ref[0:1, :] += _rowsum(dx2v * out1_ref[...])
        dout = (dx2v * (1.0 + vec_ref[3:4, :])).astype(BF16)
        dout_ref[...] = dout
        dy_ref[...] = _dot_nt(dout, w_ref[...])
        uext_ref[HALO:, :] = gc_ref[...].astype(F32) * u_ref[...].astype(F32)
        uext_ref[0:HALO, :] = jnp.where(blk > 0, hgc_ref[...].astype(F32) * hu_ref[...].astype(F32), 0.0)
        for c in range(DI // CT):
            cs = slice(c * CT, (c + 1) * CT)
            u0 = uext_ref[HALO - 2:HALO - 2 + tm, cs]
            u1 = uext_ref[HALO - 1:HALO - 1 + tm, cs]
            u2 = uext_ref[HALO:HALO + tm, cs]
            w0, w1, w2 = cw_ref[0:1, cs], cw_ref[1:2, cs], cw_ref[2:3, cs]
            conv = cb_ref[0:1, cs] + w0 * u0 + w1 * u1 + w2 * u2
            gt = gt_ref[:, cs].astype(F32)
            sg = jax.nn.sigmoid(gt)
            gb = gb_ref[:, cs].astype(F32)
            dy = dy_ref[:, cs]
            t1 = dy * (gt * sg)
            dproj_ref[:, cs] = (t1 * conv).astype(BF16)
            dconv = t1 * gb
            dproj_ref[:, 3 * DI + c * CT:3 * DI + (c + 1) * CT] = (
                dy * gb * conv * (sg * (1.0 + gt * (1.0 - sg)))).astype(BF16)
            accv_ref[0:1, cs] += _rowsum(dconv * u0)
            accv_ref[1:2, cs] += _rowsum(dconv * u1)
            accv_ref[2:3, cs] += _rowsum(dconv * u2)
            accv_ref[3:4, cs] += _rowsum(dconv)
            dext_ref[0:tm, cs] = dconv
            duu = w2 * dconv + w1 * dext_ref[1:tm + 1, cs] + w0 * dext_ref[2:tm + 2, cs]
            dproj_ref[:, DI + c * CT:DI + (c + 1) * CT] = (duu * u_ref[:, cs].astype(F32)).astype(BF16)
            dproj_ref[:, 2 * DI + c * CT:2 * DI + (c + 1) * CT] = (duu * gc_ref[:, cs].astype(F32)).astype(BF16)
        dext_ref[tm:tm + 8, :] = dext_ref[0:8, :]

    rrow = lambda i: (nb - 1 - i, 0)
    rcol = lambda col: (lambda i: (nb - 1 - i, col))
    halo = lambda col: (lambda i: (jnp.maximum((nb - 1 - i) * hb - 1, 0), col))
    const = lambda i: (0, 0)
    return pl.pallas_call(
        body, name="odd_bwd", grid=(nb,),
        in_specs=[pl.BlockSpec((tm, D), rrow), pl.BlockSpec((tm, D), rrow),
                  pl.BlockSpec((tm, DI), rcol(0)), pl.BlockSpec((tm, DI), rcol(1)),
                  pl.BlockSpec((tm, DI), rcol(2)), pl.BlockSpec((tm, DI), rcol(3)),
                  pl.BlockSpec((HALO, DI), halo(1)), pl.BlockSpec((HALO, DI), halo(2)),
                  pl.BlockSpec((DI, D), const), pl.BlockSpec((8, D), const),
                  pl.BlockSpec((8, DI), const), pl.BlockSpec((1, DI), const)],
        out_specs=[pl.BlockSpec((tm, D), rrow), pl.BlockSpec((tm, NO), rrow),
                   pl.BlockSpec((8, DI), const), pl.BlockSpec((8, D), const)],
        out_shape=[jax.ShapeDtypeStruct((s, D), BF16), jax.ShapeDtypeStruct((s, NO), BF16),
                   jax.ShapeDtypeStruct((8, DI), F32), jax.ShapeDtypeStruct((8, D), F32)],
        scratch_shapes=[pltpu.VMEM((tm + HALO, DI), F32), pltpu.VMEM((tm + 8, DI), F32), pltpu.VMEM((tm, DI), F32)],
        compiler_params=_params("arbitrary"),
    )(dx2, out1, proj1, proj1, proj1, proj1, proj1, proj1, wout, vecs, cw, cb)


def _grad_w_cols(a, b, name):
    s, m = a.shape
    ns = b.shape[1] // NCHIP
    ts = min(TM, s)

    def body(a_ref, b_ref, o_ref):
        @pl.when(pl.program_id(1) == 0)
        def _():
            o_ref[...] = jnp.zeros_like(o_ref)

        o_ref[...] += _dot_tn(a_ref[...], b_ref[...])

    return pl.pallas_call(
        body, name=name, grid=(NCHIP, s // ts),
        in_specs=[pl.BlockSpec((ts, m), lambda j, k: (k, 0)),
                  pl.BlockSpec((ts, ns), lambda j, k: (k, j))],
        out_specs=pl.BlockSpec((None, m, ns), lambda j, k: (j, 0, 0)),
        out_shape=jax.ShapeDtypeStruct((NCHIP, m, ns), F32),
        compiler_params=_params("parallel", "arbitrary"),
    )(a, b)


def _grad_w_rows(a, b, name):
    s = a.shape[0]
    ms = a.shape[1] // NCHIP
    n = b.shape[1]
    ts = min(TM, s)

    def body(a_ref, b_ref, o_ref):
        @pl.when(pl.program_id(1) == 0)
        def _():
            o_ref[...] = jnp.zeros_like(o_ref)

        o_ref[...] += _dot_tn(a_ref[...], b_ref[...])

    return pl.pallas_call(
        body, name=name, grid=(NCHIP, s // ts),
        in_specs=[pl.BlockSpec((ts, ms), lambda i, k: (k, i)),
                  pl.BlockSpec((ts, n), lambda i, k: (k, 0))],
        out_specs=pl.BlockSpec((None, ms, n), lambda i, k: (i, 0, 0)),
        out_shape=jax.ShapeDtypeStruct((NCHIP, ms, n), F32),
        compiler_params=_params("parallel", "arbitrary"),
    )(a, b)


def _inproj_bwd(dproj, w, x, dx_in, vecs, name, comm=None):
    s = x.shape[0]
    ns = w.shape[2]
    tm = min(TME, s)
    ni = s // tm
    comm = comm or _NO_COMM
    nci, nco = len(comm.arrays), len(comm.out_shape)

    def body(*refs):
        dp_ref, w_hbm, x_ref, dxin_ref, vec_ref = refs[:5]
        dx_ref, acc_ref = refs[5 + nci:7 + nci]
        w_ref = refs[7 + nci + nco]
        cargs = (refs[5:5 + nci], refs[7 + nci:7 + nci + nco], refs[8 + nci + nco:])
        i = pl.program_id(0)

        @pl.when(i == 0)
        def _():
            acc_ref[...] = jnp.zeros_like(acc_ref)
            if nci:
                comm.start(*cargs)
            pltpu.sync_copy(w_hbm, w_ref)

        dh = _dot_nt(dp_ref[:, 0:ns], w_ref[0])
        for j in range(1, NCHIP):
            dh = dh + _dot_nt(dp_ref[:, j * ns:(j + 1) * ns], w_ref[j])
        xv = x_ref[...]
        r = lax.rsqrt(jnp.mean(xv * xv, axis=-1, keepdims=True) + EPS)
        nrm = xv * r
        g = vec_ref[0:1, :]
        sc1 = 1.0 + vec_ref[1:2, :]
        dhn = dh * nrm
        acc_ref[0:1, :] += _rowsum(dh)
        acc_ref[1:2, :] += _rowsum(dhn) * g
        acc_ref[2:3, :] += _rowsum(dhn) * sc1
        dn = dh * (g * sc1)
        dx_ref[...] = dxin_ref[...] + r * (dn - nrm * jnp.mean(dn * nrm, axis=-1, keepdims=True))

        if nci:
            pl.when(i == ni - 1)(lambda: comm.finish(*cargs))

    row = lambda i: (i, 0)
    const = lambda i: (0, 0)
    return pl.pallas_call(
        body, name=name, grid=(ni,),
        in_specs=[pl.BlockSpec((tm, NCHIP * ns), row), ANY,
                  pl.BlockSpec((tm, D), row), pl.BlockSpec((tm, D), row), pl.BlockSpec((8, D), const)] + [ANY] * nci,
        out_specs=[pl.BlockSpec((tm, D), row), pl.BlockSpec((8, D), const)] + [ANY] * nco,
        out_shape=[jax.ShapeDtypeStruct((s, D), F32), jax.ShapeDtypeStruct((8, D), F32)] + comm.out_shape,
        scratch_shapes=[pltpu.VMEM(w.shape, BF16)] + comm.sems,
        compiler_params=_params("arbitrary"),
    )(dproj, w, x, dx_in, vecs, *comm.arrays)


def _even_bwd(dx1, out0, ypool, ysb, proj0, wout, vecs):
    s = dx1.shape[0]
    tm = min(TME, s)

    def body(dx1_ref, out0_ref, yp_ref, ys_ref, gate_ref, w_ref, vec_ref,
             dout_ref, dyp_ref, dys_ref, dgt_ref, acc_ref):
        @pl.when(pl.program_id(0) == 0)
        def _():
            acc_ref[...] = jnp.zeros_like(acc_ref)

        dx1v = dx1_ref[...]
        acc_ref[0:1, :] += _rowsum(dx1v * out0_ref[...])
        dout = (dx1v * (1.0 + vec_ref[3:4, :])).astype(BF16)
        dout_ref[...] = dout
        dyg = _dot_nt(dout, w_ref[...])
        gt = gate_ref[...].astype(F32)
        sg = jax.nn.sigmoid(gt)
        sl = gt * sg
        dsl = sg * (1.0 + gt * (1.0 - sg))
        dyp_ref[...] = dyg[:, :DP] * sl[:, :DP]
        dys_ref[...] = dyg[:, DP:] * sl[:, DP:]
        dgt_ref[:, :DP] = (dyg[:, :DP] * yp_ref[...] * dsl[:, :DP]).astype(BF16)
        dgt_ref[:, DP:] = (dyg[:, DP:] * ys_ref[...] * dsl[:, DP:]).astype(BF16)

    row = lambda i: (i, 0)
    const = lambda i: (0, 0)
    return pl.pallas_call(
        body, name="even_bwd", grid=(s // tm,),
        in_specs=[pl.BlockSpec((tm, D), row), pl.BlockSpec((tm, D), row),
                  pl.BlockSpec((tm, DP), row), pl.BlockSpec((tm, DP), row),
                  pl.BlockSpec((tm, DI), lambda i: (i, 2)),
                  pl.BlockSpec((DI, D), const), pl.BlockSpec((8, D), const)],
        out_specs=[pl.BlockSpec((tm, D), row), pl.BlockSpec((tm, DP), row), pl.BlockSpec((tm, DP), row),
                   pl.BlockSpec((tm, DI), lambda i: (i, NE // DI - 1)), pl.BlockSpec((8, D), const)],
        out_shape=[jax.ShapeDtypeStruct((s, D), BF16), jax.ShapeDtypeStruct((s, DP), F32),
                   jax.ShapeDtypeStruct((s, DP), F32), jax.ShapeDtypeStruct((s, NE), BF16),
                   jax.ShapeDtypeStruct((8, D), F32)],
        compiler_params=_params("arbitrary"),
    )(dx1, out0, ypool, ysb, proj0, wout, vecs)


def _pool_bwd(dyp, p, pw, pscale, dproj):
    s = dyp.shape[0]
    tm = min(TM, s)
    nb = s // tm
    hb = tm // 16

    def body(dy_ref, dyh_ref, p_ref, w_ref, sc_ref, dproj_hbm, du_ref, dw_ref, acc_ref, ext_ref):
        i = pl.program_id(0)

        @pl.when(i == 0)
        def _():
            dw_ref[...] = jnp.zeros_like(dw_ref)
            acc_ref[...] = jnp.zeros_like(acc_ref)

        t = i * tm + lax.broadcasted_iota(jnp.int32, (tm + 16, 1), 0)
        for g, wdw in enumerate(WINDOWS):
            cs = slice(g * PG, (g + 1) * PG)
            sc = sc_ref[:, cs]
            dy = dy_ref[:, cs]
            dyh = jnp.where(i < nb - 1, dyh_ref[:, cs], 0.0)
            pb = p_ref[:, cs]
            wg = w_ref[g]
            acc_ref[0:1, cs] += _rowsum(dy * _dot(pb, wg))
            dypre = (dy * sc).astype(BF16)
            dw_ref[g] += _dot_tn(pb, dypre)
            dp = _dot_nt(dypre, wg)
            dph = _dot_nt((dyh * sc).astype(BF16), wg)
            inv = 1.0 / jnp.minimum(t + 1, wdw).astype(F32)
            ext_ref[0:tm, cs] = dp * inv[0:tm]
            ext_ref[tm:tm + 16, cs] = dph * inv[tm:tm + 16]
            acc = ext_ref[0:tm, cs]
            for j in range(1, wdw):
                acc = acc + ext_ref[j:j + tm, cs]
            du_ref[:, cs] = (acc - dp).astype(BF16)

    row = lambda i: (i, 0)
    return pl.pallas_call(
        body, name="pool_bwd", grid=(nb,),
        in_specs=[pl.BlockSpec((tm, DP), row),
                  pl.BlockSpec((16, DP), lambda i: (jnp.minimum((i + 1) * hb, s // 16 - 1), 0)),
                  pl.BlockSpec((tm, DP), row),
                  pl.BlockSpec((4, PG, PG), lambda i: (0, 0, 0)),
                  pl.BlockSpec((1, DP), lambda i: (0, 0)), ANY],
        out_specs=[pl.BlockSpec((tm, DP), row), pl.BlockSpec((4, PG, PG), lambda i: (0, 0, 0)),
                   pl.BlockSpec((8, DP), lambda i: (0, 0))],
        out_shape=[jax.ShapeDtypeStruct((s, NE), BF16), jax.ShapeDtypeStruct((4, PG, PG), F32),
                   jax.ShapeDtypeStruct((8, DP), F32)],
        input_output_aliases={5: 0},
        scratch_shapes=[pltpu.VMEM((tm + 16, DP), F32)],
        compiler_params=_params("arbitrary"),
    )(dyp, dyp, p, pw, pscale, dproj)


def _attn_bwd(proj0, ysb, dys, dproj, comm=None):
    s = proj0.shape[0]
    nq = s // BQ
    kpq = BQ // BK
    nsteps = DP // 128
    comm = comm or _NO_COMM
    nci, nco = len(comm.arrays), len(comm.out_shape)

    def body(*refs):
        q_ref, k_ref, v_ref, o_ref, do_ref = refs[:5]
        dproj_hbm = refs[6 + nci]
        scratch = refs[7 + nci + nco:]
        qn_ref, k8_ref, vb_ref, dob_ref, dka_ref, dva_ref, dq_ref, dk_ref, dv_ref, part_sems = scratch[:10]
        cargs = (refs[6:6 + nci], refs[7 + nci:7 + nci + nco], scratch[10:])
        step = pl.program_id(0)
        if nci:
            pl.when(step == 0)(lambda: comm.start(*cargs))

        qn_ref[...] = (-q_ref[...]).astype(BF16)
        k8_ref[...] = (k_ref[...] * INV_SQRT_HD).astype(BF16)
        vb_ref[...] = v_ref[...].astype(BF16)
        dob_ref[...] = do_ref[...].astype(BF16)
        dka_ref[...] = jnp.zeros_like(dka_ref)
        dva_ref[...] = jnp.zeros_like(dva_ref)
        after, from_on = _tri_masks()
        causal = _causal_mask()
        heads = [slice(HD * h, HD * (h + 1)) for h in range(2)]
        hs = range(len(heads))
        lo, hi, full = (0, BK), (BK, BK), (0, BQ)

        def qstep(qi, carry):
            q0 = pl.multiple_of(qi * BQ, BQ)
            qn = [qn_ref[pl.ds(q0, BQ), ls] for ls in heads]
            do = [dob_ref[pl.ds(q0, BQ), ls] for ls in heads]
            total = [jnp.sum(do[h].astype(F32) * o_ref[pl.ds(q0, BQ), ls], axis=1, keepdims=True)
                     for h, ls in enumerate(heads)]

            def sweep(lanes, state):
                rows_of = lambda i: slice(lanes[i][1][0], lanes[i][1][0] + lanes[i][1][1])
                k8 = [k8_ref[pl.ds(k0, BK), heads[h]] for h, _, k0, _ in lanes]
                nz = [_dot_nt(qn[lane[0]][rows_of(i)], k8[i]) for i, lane in enumerate(lanes)]
                da = [_dot_nt(do[h][rows_of(i)], vb_ref[pl.ds(k0, BK), heads[h]]) for i, (h, _, k0, _) in enumerate(lanes)]
                ll = [_sb_logits(nz[i], lane[3]) for i, lane in enumerate(lanes)]
                aft = [_dot(ll[i][0].astype(BF16), after) for i in range(len(lanes))]
                state = dict(state)
                for i, (h, rows, k0, mask) in enumerate(lanes):
                    dq_acc, c, cg = state[h, rows]
                    a = jnp.exp(ll[i][1] + aft[i] + c)
                    if mask is not None:
                        a = jnp.where(mask, a, 0.0)
                    ab = a.astype(BF16)
                    g = da[i] * ab.astype(F32)
                    suf = _split_dot(g, from_on)
                    dz = g - jnp.exp(ll[i][1]) * (g + ((total[h][rows_of(i)] - cg) - suf))
                    if mask is not None:
                        dz = jnp.where(mask, dz, 0.0)
                    dzb = dz.astype(BF16)
                    dka_ref[pl.ds(k0, BK), heads[h]] += _dot_tn(dzb, qn[h][rows_of(i)])
                    dva_ref[pl.ds(k0, BK), heads[h]] += _dot_tn(ab, do[h][rows_of(i)])
                    state[h, rows] = (dq_acc + _dot(dzb, k8[i]), c + aft[i][:, 0:1] + ll[i][0][:, 0:1],
                                      cg + suf[:, 0:1])
                return state

            k_lo, k_hi = q0, pl.multiple_of(q0 + BK, BK)
            zero = jnp.zeros((BK, 1), F32)
            st = {(h, r): (jnp.zeros((BK, HD), F32), zero, zero) for h in hs for r in (lo, hi)}
            st = sweep([(h, hi, k_hi, causal) for h in hs] + [(h, lo, k_lo, causal) for h in hs]
                       + [(h, hi, k_lo, None) for h in hs], st)
            state = tuple(tuple(jnp.concatenate([st[h, lo][i], st[h, hi][i]], axis=0) for i in range(3)) for h in hs)

            def left(j, st):
                k1 = pl.multiple_of(q0 - (j + 1) * BQ, BQ)
                got = sweep([(h, full, k1 + BK, None) for h in hs] + [(h, full, k1, None) for h in hs],
                            {(h, full): st[h] for h in hs})
                return tuple(got[h, full] for h in hs)

            state = _sweep_left(qi, state, left)
            for h, ls in enumerate(heads):
                dq_ref[pl.ds(q0, BQ), ls] = state[h][0].astype(BF16)
            return carry

        lax.fori_loop(0, nq, qstep, 0)
        dk_ref[...] = (dka_ref[...] * (-INV_SQRT_HD)).astype(BF16)
        dv_ref[...] = dva_ref[...].astype(BF16)
        lanes0 = pl.multiple_of(step * 128, 128)
        parts = [pltpu.make_async_copy(src, dproj_hbm.at[:, pl.ds((1 + k) * DP + lanes0, 128)], part_sems.at[k])
                 for k, src in enumerate((dq_ref, dk_ref, dv_ref))]
        for cp in parts:
            cp.start()
        for cp in parts:
            cp.wait()
        if nci:
            pl.when(step == nsteps - 1)(lambda: comm.finish(*cargs))

    col = lambda h: (0, h)
    return pl.pallas_call(
        body, name="attn_bwd", grid=(nsteps,),
        in_specs=[pl.BlockSpec((s, 128), lambda h: (0, 8 + h)),
                  pl.BlockSpec((s, 128), lambda h: (0, 16 + h)),
                  pl.BlockSpec((s, 128), lambda h: (0, 24 + h)),
                  pl.BlockSpec((s, 128), col), pl.BlockSpec((s, 128), col), ANY] + [ANY] * nci,
        out_specs=[ANY] * (1 + nco),
        out_shape=[jax.ShapeDtypeStruct((s, NE), BF16)] + comm.out_shape,
        input_output_aliases={5: 0},
        scratch_shapes=([pltpu.VMEM((s, 128), BF16)] * 4 + [pltpu.VMEM((s, 128), F32)] * 2
                        + [pltpu.VMEM((s, 128), BF16)] * 3 + [pltpu.SemaphoreType.DMA((3,))] + comm.sems),
        compiler_params=_params("arbitrary"),
    )(proj0, proj0, proj0, ysb, dys, dproj, *comm.arrays)


def _adamw_math(w, g, m, v):
    m2 = B1 * m + (1.0 - B1) * g
    v2 = B2 * v + (1.0 - B2) * (g * g)
    m_hat = m2 / (1.0 - B1 ** STEP)
    v_hat = v2 / (1.0 - B2 ** STEP)
    return -LR * (m_hat / (jnp.sqrt(v_hat) + EPS_ADAM) + WD * w), m2, v2


def _adamw(w, g, m, v, name):
    r, c = w.shape
    tr = r
    while tr * c * 4 > (1 << 20) and tr % 16 == 0:
        tr //= 2

    def body(w_ref, g_ref, m_ref, v_ref, d_ref, m2_ref, v2_ref):
        d_ref[...], m2_ref[...], v2_ref[...] = _adamw_math(w_ref[...], g_ref[...], m_ref[...], v_ref[...])

    spec = pl.BlockSpec((tr, c), lambda i: (i, 0))
    return pl.pallas_call(
        body, name=name, grid=(r // tr,),
        in_specs=[spec] * 4, out_specs=[spec] * 3,
        out_shape=[jax.ShapeDtypeStruct((r, c), F32)] * 3,
        compiler_params=_params("parallel"),
    )(w, g, m, v)


def _local_step(x, target, vecs0, vecs1, win0, rest, pscale, cw8, cb, sel=None):
    dist = sel is not None
    h0 = _norm_mod(x, vecs0, "norm0")
    proj0, *got = _inproj(h0, win0, "inproj0", _gather_comm(list(rest[0:2])) if dist else None)
    pw, wout0 = (_group_major(got[0]), got[1].reshape(DI, D)) if dist else rest[0:2]
    p, ypool = _pool_fwd(proj0, pw, pscale)
    ysb, *got = _attn_fwd(proj0, _gather_comm(list(rest[2:4])) if dist else None)
    win1, wout1 = (got[0], got[1].reshape(DI, D)) if dist else rest[2:4]
    x1, out0, yg, h1 = _even_out(ypool, ysb, proj0, wout0, x, vecs0, vecs1)
    proj1, = _inproj(h1, win1, "inproj1")
    dx2, out1, y1, acc_f = _odd_out(proj1, wout1, x1, vecs1, cw8, cb, target)

    def chip_partials(grads, names):
        from_sibling = _send_halves(grads, "rs_send_halves_" + names[0])
        part = [_add_halves(g, t, sel[1:2], "rs_add_halves_" + nm) for g, t, nm in zip(grads, from_sibling, names)]
        return [p32 for p32, _ in part], _exchange_comm([p16 for _, p16 in part])

    dout1, dproj1, acc_cv, acc_g1 = _odd_bwd(dx2, out1, proj1, wout1, vecs1, cw8, cb)
    g_wout1 = _grad_w_rows(y1, dout1, "grad_wout1")
    g_win1 = _grad_w_cols(h1, dproj1, "grad_win1")
    dx1, acc_n1 = _inproj_bwd(dproj1, win1, x1, dx2, vecs1, "inproj1_bwd")

    dout0, dyp, dys, dproj0, acc_g0 = _even_bwd(dx1, out0, ypool, ysb, proj0, wout0, vecs0)
    g_wout0 = _grad_w_rows(yg, dout0, "grad_wout0")
    dproj0, g_pw, acc_ps = _pool_bwd(dyp, p, pw, pscale, dproj0)
    early = [_chip_major(g_pw, PG // NCHIP), g_wout0, g_win1, g_wout1]
    part_a, swap_a = chip_partials(early, ["pool", "wout0", "win1", "wout1"]) if dist else (None, None)
    dproj0, *got_a = _attn_bwd(proj0, ysb, dys, dproj0, swap_a)
    g_win0 = _grad_w_cols(h0, dproj0, "grad_win0")
    part_b, swap_b = chip_partials([g_win0], ["win0"]) if dist else (None, None)
    dx0, acc_n0, *got_b = _inproj_bwd(dproj0, win0, x, dx1, vecs0, "inproj0_bwd", swap_b)

    if dist:
        names = ["win0", "pool", "wout0", "win1", "wout1"]
        halves = [_add_partials(p32, t, sel, "rs_add_partials_" + nm)
                  for p32, t, nm in zip(part_b + part_a, got_b + got_a, names)]
        grads = tuple(_join_halves(halves))
    else:
        grads = (g_win0, g_pw, g_wout0, g_win1, g_wout1)

    sums = dict(
        dm0=jnp.concatenate([acc_n0[0:2], acc_g0[0:1]], axis=0),
        dm1=jnp.concatenate([acc_n1[0:2], acc_g1[0:1]], axis=0),
        norm_g=jnp.concatenate([acc_n0[2:3], acc_n1[2:3]], axis=0),
        pool_scale=acc_ps[0:1], final_g=acc_f[0:1], loss=acc_f[1:2],
        conv_w=acc_cv[0:3], conv_b=acc_cv[3:4])
    return dx0, grads, sums


ANY = pl.BlockSpec(memory_space=pl.ANY)
CHIP_FLIPS = ((1, 0), (0, 1), (1, 1))


def _place():
    return lax.axis_index("x"), lax.axis_index("y"), lax.axis_index("c")


def _flip(v, f):
    return 1 - v if f else v


def _allgather8(v, name):
    m_per, n = v.shape

    def body(x_ref, out_ref, send_sems, recv_sems, local_sem):
        x, y, c = _place()
        me, sibling = (x, y, c), (x, y, 1 - c)
        chips = [(_flip(x, fx), _flip(y, fy)) for fx, fy in CHIP_FLIPS]

        def rows(px, py, pc):
            return out_ref.at[pl.ds((4 * px + 2 * py + pc) * m_per, m_per), :]

        def copy(k, block, to, src=None):
            return pltpu.make_async_remote_copy(
                src_ref=rows(*block) if src is None else src, dst_ref=rows(*block),
                send_sem=send_sems.at[k], recv_sem=recv_sems.at[k], device_id=to, device_id_type=MESH)

        mine = pltpu.make_async_copy(x_ref, rows(*me), local_sem)
        mine.start()
        first = [copy(0, me, sibling, src=x_ref)]
        first += [copy(1 + j, me, (*chip, c), src=x_ref) for j, chip in enumerate(chips)]
        for cp in first:
            cp.start()
        passed = [copy(4 + j, (*chip, c), sibling) for j, chip in enumerate(chips)]
        for j, chip in enumerate(chips):
            copy(1 + j, (*chip, c), me).wait_recv()
            passed[j].start()
        copy(0, sibling, me).wait_recv()
        for j, chip in enumerate(chips):
            copy(4 + j, (*chip, 1 - c), me).wait_recv()
        for cp in first + passed:
            cp.wait_send()
        mine.wait()

    return pl.pallas_call(
        body, name=name,
        out_shape=jax.ShapeDtypeStruct((NDEV * m_per, n), v.dtype),
        in_specs=[pl.BlockSpec(memory_space=pltpu.VMEM)],
        out_specs=pl.BlockSpec(memory_space=pltpu.VMEM),
        scratch_shapes=[pltpu.SemaphoreType.DMA((7,)), pltpu.SemaphoreType.DMA((7,)), pltpu.SemaphoreType.DMA],
    )(v)


class _Comm:
    def __init__(self, arrays, out_shape, sems, start, finish):
        self.arrays, self.out_shape, self.sems, self.start, self.finish = arrays, out_shape, sems, start, finish


_NO_COMM = _Comm([], [], [], None, None)


def _run_comm(comm, name):
    n = len(comm.arrays)

    def body(*refs):
        args = (refs[:n], refs[n:n + len(comm.out_shape)], refs[n + len(comm.out_shape):])
        comm.start(*args)
        comm.finish(*args)

    return pl.pallas_call(
        body, name=name, out_shape=comm.out_shape,
        in_specs=[ANY] * n, out_specs=[ANY] * len(comm.out_shape), scratch_shapes=comm.sems,
    )(*comm.arrays)


def _gather_comm(shards):
    n = len(shards)

    def pieces(ins, outs, sems, kinds):
        x, y, c = _place()
        ici_send, ici_recv, fwd_send, fwd_recv, local = sems
        own, sibling = 2 * x + y, (x, y, 1 - c)
        made = {kind: [] for kind in kinds}
        for w in range(n):
            r2 = ins[w].shape[0] // 2
            mine, other = pl.ds(c * r2, r2), pl.ds((1 - c) * r2, r2)
            if "local" in kinds:
                made["local"].append(pltpu.make_async_copy(ins[w], outs[w].at[own], local.at[w]))
            for d, (fx, fy) in enumerate(CHIP_FLIPS):
                px, py, k = _flip(x, fx), _flip(y, fy), 3 * w + d
                peer = 2 * px + py
                ici = dict(send_sem=ici_send.at[k], recv_sem=ici_recv.at[k], device_id=(px, py, c), device_id_type=MESH)
                fwd = dict(send_sem=fwd_send.at[k], recv_sem=fwd_recv.at[k], device_id=sibling, device_id_type=MESH)
                if "ici_out" in kinds:
                    made["ici_out"].append(pltpu.make_async_remote_copy(
                        src_ref=ins[w].at[mine, :], dst_ref=outs[w].at[own, mine, :], **ici))
                if "ici_in" in kinds:
                    made["ici_in"].append(pltpu.make_async_remote_copy(
                        src_ref=ins[w].at[mine, :], dst_ref=outs[w].at[peer, mine, :], **ici))
                if "fwd_out" in kinds:
                    made["fwd_out"].append(pltpu.make_async_remote_copy(
                        src_ref=outs[w].at[peer, mine, :], dst_ref=outs[w].at[peer, mine, :], **fwd))
                if "fwd_in" in kinds:
                    made["fwd_in"].append(pltpu.make_async_remote_copy(
                        src_ref=outs[w].at[peer, other, :], dst_ref=outs[w].at[peer, other, :], **fwd))
        return made

    def start(ins, outs, sems):
        made = pieces(ins, outs, sems, ("local", "ici_out"))
        for cp in made["local"] + made["ici_out"]:
            cp.start()

    def finish(ins, outs, sems):
        made = pieces(ins, outs, sems, ("local", "ici_out", "ici_in", "fwd_out", "fwd_in"))
        for arrived, onward in zip(made["ici_in"], made["fwd_out"]):
            arrived.wait_recv()
            onward.start()
        for cp in made["fwd_in"]:
            cp.wait_recv()
        for cp in made["ici_out"] + made["fwd_out"]:
            cp.wait_send()
        for cp in made["local"]:
            cp.wait()

    dma = pltpu.SemaphoreType.DMA
    return _Comm(list(shards), [jax.ShapeDtypeStruct((NCHIP,) + a.shape, a.dtype) for a in shards],
                 [dma((3 * n,))] * 4 + [dma((n,))], start, finish)


def _exchange_comm(parts):
    n = len(parts)

    def copies(ins, outs, sems):
        x, y, c = _place()
        send_sems, recv_sems = sems
        out = []
        for w in range(n):
            for d, (fx, fy) in enumerate(CHIP_FLIPS):
                px, py = _flip(x, fx), _flip(y, fy)
                out.append(pltpu.make_async_remote_copy(
                    src_ref=ins[w].at[2 * px + py], dst_ref=outs[w].at[d], send_sem=send_sems.at[3 * w + d],
                    recv_sem=recv_sems.at[3 * w + d], device_id=(px, py, c), device_id_type=MESH))
        return out

    def start(ins, outs, sems):
        for cp in copies(ins, outs, sems):
            cp.start()

    def finish(ins, outs, sems):
        cps = copies(ins, outs, sems)
        for cp in cps:
            cp.wait_recv()
        for cp in cps:
            cp.wait_send()

    dma = pltpu.SemaphoreType.DMA
    return _Comm(list(parts), [jax.ShapeDtypeStruct((3,) + p.shape[1:], BF16) for p in parts],
                 [dma((3 * n,))] * 2, start, finish)


def _send_halves(grads, name):
    n = len(grads)

    def body(*refs):
        ins, outs = refs[:n], refs[n:2 * n]
        send_sems, recv_sems = refs[2 * n:]
        x, y, c = _place()
        copies = []
        for w in range(n):
            r2 = ins[w].shape[1] // 2
            cp = pltpu.make_async_remote_copy(
                src_ref=ins[w].at[:, pl.ds((1 - c) * r2, r2), :], dst_ref=outs[w],
                send_sem=send_sems.at[w], recv_sem=recv_sems.at[w], device_id=(x, y, 1 - c), device_id_type=MESH)
            cp.start()
            copies.append(cp)
        for cp in copies:
            cp.wait_recv()
        for cp in copies:
            cp.wait_send()

    return pl.pallas_call(
        body, name=name,
        out_shape=[jax.ShapeDtypeStruct((NCHIP, g.shape[1] // 2, g.shape[2]), F32) for g in grads],
        in_specs=[ANY] * n, out_specs=[ANY] * n,
        scratch_shapes=[pltpu.SemaphoreType.DMA((n,)), pltpu.SemaphoreType.DMA((n,))],
    )(*grads)


def _row_tile(rows, cols):
    tr = rows
    while tr * cols * 4 > (1 << 20) and tr % 16 == 0:
        tr //= 2
    return tr


def _add_halves(g, t1, core, name):
    _, r, cdim = g.shape
    r2 = r // 2
    tr = _row_tile(r2, cdim)
    nt = r2 // tr

    def body(core_ref, g_ref, t_ref, p_ref, pb_ref):
        p = g_ref[...] + t_ref[...]
        p_ref[...] = p
        pb_ref[...] = p.astype(BF16)

    blk = pl.BlockSpec((None, tr, cdim), lambda j, i, core_ref: (j, i, 0))
    return pl.pallas_call(
        body, name=name,
        grid_spec=pltpu.PrefetchScalarGridSpec(
            num_scalar_prefetch=1, grid=(NCHIP, nt),
            in_specs=[pl.BlockSpec((None, tr, cdim), lambda j, i, core_ref: (j, core_ref[0] * nt + i, 0)), blk],
            out_specs=[blk, blk]),
        out_shape=[jax.ShapeDtypeStruct((NCHIP, r2, cdim), F32), jax.ShapeDtypeStruct((NCHIP, r2, cdim), BF16)],
        compiler_params=_params("parallel", "parallel"),
    )(core, g, t1)


def _add_partials(p, t2, sel, name):
    _, r2, cdim = p.shape
    tr = _row_tile(r2, cdim)
    nt = r2 // tr

    def body(sel_ref, p_ref, t_ref, o_ref):
        o_ref[...] = ((p_ref[...] + t_ref[0].astype(F32)) + t_ref[1].astype(F32)) + t_ref[2].astype(F32)

    return pl.pallas_call(
        body, name=name,
        grid_spec=pltpu.PrefetchScalarGridSpec(
            num_scalar_prefetch=1, grid=(nt,),
            in_specs=[pl.BlockSpec((None, tr, cdim), lambda i, sel_ref: (sel_ref[0], i, 0)),
                      pl.BlockSpec((3, tr, cdim), lambda i, sel_ref: (0, i, 0))],
            out_specs=pl.BlockSpec((tr, cdim), lambda i, sel_ref: (sel_ref[1] * nt + i, 0))),
        out_shape=jax.ShapeDtypeStruct((2 * r2, cdim), F32),
        compiler_params=_params("parallel"),
    )(sel, p, t2)


def _join_halves(grads):
    n = len(grads)

    def body(*refs):
        bufs = refs[n:2 * n]
        send_sems, recv_sems = refs[2 * n:]
        x, y, c = _place()
        copies = []
        for w in range(n):
            r2 = bufs[w].shape[0] // 2
            mine = bufs[w].at[pl.ds(c * r2, r2), :]
            cp = pltpu.make_async_remote_copy(
                src_ref=mine, dst_ref=mine, send_sem=send_sems.at[w], recv_sem=recv_sems.at[w],
                device_id=(x, y, 1 - c), device_id_type=MESH)
            cp.start()
            copies.append(cp)
        for w in range(n):
            r2 = bufs[w].shape[0] // 2
            theirs = bufs[w].at[pl.ds((1 - c) * r2, r2), :]
            pltpu.make_async_remote_copy(
                src_ref=theirs, dst_ref=theirs, send_sem=send_sems.at[w], recv_sem=recv_sems.at[w],
                device_id=(x, y, 1 - c), device_id_type=MESH).wait_recv()
        for cp in copies:
            cp.wait_send()

    return pl.pallas_call(
        body, name="rs_join_halves",
        out_shape=[jax.ShapeDtypeStruct(g.shape, F32) for g in grads],
        in_specs=[ANY] * n, out_specs=[ANY] * n, input_output_aliases={w: w for w in range(n)},
        scratch_shapes=[pltpu.SemaphoreType.DMA((n,)), pltpu.SemaphoreType.DMA((n,))],
    )(*grads)


def _ada_fwd(c_all, ada_w):
    nl, _, ns = ada_w.shape

    def body(c_ref, w_ref, o_ref):
        cv = c_ref[...]
        o_ref[...] = _dot((cv * jax.nn.sigmoid(cv)).astype(BF16), w_ref[...].astype(BF16))

    return pl.pallas_call(
        body, name="ada_fwd", grid=(nl,),
        in_specs=[pl.BlockSpec((NDEV, D), lambda i: (0, 0)), pl.BlockSpec((None, D, ns), lambda i: (i, 0, 0))],
        out_specs=pl.BlockSpec((None, NDEV, ns), lambda i: (i, 0, 0)),
        out_shape=jax.ShapeDtypeStruct((nl, NDEV, ns), F32),
        compiler_params=_params("parallel"),
    )(c_all, ada_w)


PACK_ROWS = 24


def _reduce_packed(gathered):
    def body(g_ref, tot_ref, loss_ref):
        tot = g_ref[0:PACK_ROWS, :]
        for dev in range(1, NDEV):
            tot = tot + g_ref[dev * PACK_ROWS:(dev + 1) * PACK_ROWS, :]
        tot_ref[...] = tot
        loss_ref[...] = jnp.zeros((8, 128), F32) + jnp.sum(tot[10:11, :])

    return pl.pallas_call(
        body, name="reduce_packed",
        out_shape=[jax.ShapeDtypeStruct((PACK_ROWS, D), F32), jax.ShapeDtypeStruct((8, 128), F32)],
    )(gathered)


def _ada_w_update(c_t, dms, w, m, v):
    nl, _, ns = w.shape
    tr = 256

    def body(ct_ref, dm_ref, w_ref, m_ref, v_ref, g_ref, d_ref, m2_ref, v2_ref):
        ct = ct_ref[...]
        sc = ct * jax.nn.sigmoid(ct)
        dm = dm_ref[...]
        g = sc[:, 0:1] * dm[0:1, :]
        for b in range(1, NDEV):
            g = g + sc[:, b:b + 1] * dm[b:b + 1, :]
        g_ref[...] = g
        d_ref[...], m2_ref[...], v2_ref[...] = _adamw_math(w_ref[...], g, m_ref[...], v_ref[...])

    blk = pl.BlockSpec((None, tr, ns), lambda i, j: (i, j, 0))
    return pl.pallas_call(
        body, name="ada_w_update", grid=(nl, D // tr),
        in_specs=[pl.BlockSpec((tr, NDEV), lambda i, j: (j, 0)),
                  pl.BlockSpec((None, NDEV, ns), lambda i, j: (i, 0, 0)), blk, blk, blk],
        out_specs=[blk] * 4,
        out_shape=[jax.ShapeDtypeStruct((nl, D, ns), F32)] * 4,
        compiler_params=_params("parallel", "parallel"),
    )(c_t, dms, w, m, v)


def _chip_major(a, parts):
    g, _, cdim = a.shape
    return jnp.transpose(a.reshape(g, NCHIP, parts, cdim), (1, 0, 2, 3)).reshape(NCHIP, g * parts, cdim)


def _group_major(a):
    return jnp.transpose(a.reshape(NCHIP, 4, PG // NCHIP, PG), (1, 0, 2, 3)).reshape(4, PG, PG)


def kernel(x, c, norm_g, ada_w, ada_b, even_w_in, pool_w, pool_scale, even_w_out, odd_w_in, conv_w, conv_b, odd_w_out, final_g, loss_target, m_norm_g, m_ada_w, m_ada_b, m_even_w_in, m_pool_w, m_pool_scale, m_even_w_out, m_odd_w_in, m_conv_w, m_conv_b, m_odd_w_out, m_final_g, v_norm_g, v_ada_w, v_ada_b, v_even_w_in, v_pool_w, v_pool_scale, v_even_w_out, v_odd_w_in, v_conv_w, v_conv_b, v_odd_w_out, v_final_g):
    ix, iy, ic = _place()
    chip = 2 * ix + iy
    batch = 2 * chip + ic
    sel = jnp.stack([chip, ic]).astype(jnp.int32)
    ns_ada = ada_w.shape[2]
    ns_conv = conv_b.shape[1]

    conv_rows = jnp.pad(jnp.concatenate([conv_w[0], conv_b], axis=0), ((0, 3), (0, D - ns_conv)))
    first = _allgather8(jnp.concatenate([c, conv_rows], axis=0), "gather_c_conv").reshape(NCHIP, 2, 8, D)
    c_all = first[:, :, 0].reshape(NDEV, D)
    cw_full = jnp.transpose(first[:, 0, 1:5, 0:ns_conv], (1, 0, 2)).reshape(4, DI)
    cw8 = jnp.concatenate([cw_full[0:3], jnp.zeros((5, DI), F32)], axis=0)
    cb_full = cw_full[3:4]

    m_cols = _allgather8(_ada_fwd(c_all, ada_w).reshape(2 * NDEV, ns_ada), "gather_ada")
    m_cols = m_cols.reshape(NCHIP, 2, 2, NDEV, ns_ada)[:, 0]
    m_mine = lax.dynamic_index_in_dim(m_cols, batch, axis=2, keepdims=False)
    m_mine = jnp.transpose(m_mine, (1, 0, 2)).reshape(2, 3 * D) + ada_b
    zrow = jnp.zeros((3, D), F32)

    def vec_rows(i):
        sh, sc, gt = m_mine[i, 0:D], m_mine[i, D:2 * D], m_mine[i, 2 * D:3 * D]
        return jnp.concatenate([jnp.stack([norm_g[i], sc, sh, gt, final_g]), zrow], axis=0)

    win0, = _run_comm(_gather_comm([even_w_in[0].astype(BF16)]), "gather_win0")
    shards = (pool_w[0].astype(BF16).reshape(PG, PG), even_w_out[0].astype(BF16),
              odd_w_in[0].astype(BF16), odd_w_out[0].astype(BF16))
    dx0, grads, sums = _local_step(
        x[0], loss_target[0], vec_rows(0), vec_rows(1), win0, shards, pool_scale, cw8, cb_full, sel)
    r_win0, r_pw, r_wout0, r_win1, r_wout1 = grads

    packed = jnp.concatenate([
        sums["dm0"], sums["dm1"], sums["norm_g"], sums["pool_scale"], sums["final_g"], sums["loss"],
        sums["conv_w"].reshape(6, D), sums["conv_b"].reshape(2, D), jnp.zeros((PACK_ROWS - 19, D), F32)], axis=0)
    gathered = _allgather8(packed, "gather_sums")
    tot, loss8 = _reduce_packed(gathered)
    loss = loss8[0, 0]
    g_norm_g, g_pool_scale, g_final_g = tot[6:8], tot[8:9], tot[9]
    g_ada_b = tot[0:6].reshape(2, 3 * D)
    g_conv_w = lax.dynamic_slice_in_dim(tot[11:17].reshape(3, DI), chip * (DI // NCHIP), DI // NCHIP, axis=1)
    g_conv_b = lax.dynamic_slice_in_dim(tot[17:19].reshape(1, DI), chip * (DI // NCHIP), DI // NCHIP, axis=1)
    dm_all = gathered.reshape(NDEV, PACK_ROWS, D)[:, 0:6].reshape(NDEV, 2, 3 * D)
    dm_cols = jnp.transpose(lax.dynamic_slice_in_dim(dm_all, chip * ns_ada, ns_ada, axis=2), (1, 0, 2))
    g_ada_w, d_ada_w, nm_ada_w, nv_ada_w = _ada_w_update(jnp.transpose(c_all), dm_cols, ada_w, m_ada_w, v_ada_w)

    def upd(w, g, m, v, name):
        shape = w.shape
        w2, m2, v2 = (a.reshape(g.shape) for a in (w, m, v))
        d, nm, nv = _adamw(w2, g, m2, v2, name)
        return g.reshape(shape), d.reshape(shape), nm.reshape(shape), nv.reshape(shape)

    o_win0 = upd(even_w_in, r_win0, m_even_w_in, v_even_w_in, "adamw_win0")
    o_pw = upd(pool_w, r_pw, m_pool_w, v_pool_w, "adamw_pool")
    o_wout0 = upd(even_w_out, r_wout0, m_even_w_out, v_even_w_out, "adamw_wout0")
    o_win1 = upd(odd_w_in, r_win1, m_odd_w_in, v_odd_w_in, "adamw_win1")
    o_wout1 = upd(odd_w_out, r_wout1, m_odd_w_out, v_odd_w_out, "adamw_wout1")

    def pack_small(ng, ab, ps, fg, cwv, cbv):
        conv = jnp.concatenate([cwv.reshape(3, -1), cbv.reshape(1, -1)], axis=0).reshape(2, D)
        return jnp.concatenate([ng, ab.reshape(6, D), ps, fg.reshape(1, D), conv, jnp.zeros((4, D), F32)], axis=0)

    sw = pack_small(norm_g, ada_b, pool_scale, final_g, conv_w, conv_b)
    sg = pack_small(g_norm_g, g_ada_b, g_pool_scale, g_final_g, g_conv_w, g_conv_b)
    sm = pack_small(m_norm_g, m_ada_b, m_pool_scale, m_final_g, m_conv_w, m_conv_b)
    sv = pack_small(v_norm_g, v_ada_b, v_pool_scale, v_final_g, v_conv_w, v_conv_b) + jnp.concatenate(
        [jnp.zeros((12, D), F32), jnp.ones((4, D), F32)], axis=0)
    small = _adamw(sw, sg, sm, sv, "adamw_small")

    def unpack_small(a):
        conv = a[10:12].reshape(4, -1)
        return dict(norm_g=a[0:2], ada_b=a[2:8].reshape(2, 3 * D), pool_scale=a[8:9], final_g=a[9],
                    conv_w=conv[0:3].reshape(conv_w.shape), conv_b=conv[3:4].reshape(conv_b.shape))

    s_grad = dict(norm_g=g_norm_g, ada_b=g_ada_b, pool_scale=g_pool_scale, final_g=g_final_g,
                  conv_w=g_conv_w.reshape(conv_w.shape), conv_b=g_conv_b.reshape(conv_b.shape))
    s_out = [s_grad] + [unpack_small(a) for a in small]

    outs = []
    for k in range(4):
        sm_k = s_out[k]
        outs.append([sm_k["norm_g"], (g_ada_w, d_ada_w, nm_ada_w, nv_ada_w)[k], sm_k["ada_b"], o_win0[k], o_pw[k],
                     sm_k["pool_scale"], o_wout0[k], o_win1[k], sm_k["conv_w"], sm_k["conv_b"], o_wout1[k],
                     sm_k["final_g"]])
    return (loss, dx0[None], *outs[0], *outs[1], *outs[2], *outs[3])
```

```python
import functools

import jax
import jax.numpy as jnp
from jax import lax
from jax.experimental import pallas as pl
from jax.experimental.pallas import tpu as pltpu

F32 = jnp.float32
BF16 = jnp.bfloat16
MESH = pl.DeviceIdType.MESH

D = 1024
DI = 2048
DP = 1024
NE = 6144
NO = 8192
WINDOWS = (2, 4, 8, 16)
PG = 256
HD = 64
NCHIP = 4
NDEV = 8
EPS = 1e-6
INV_SQRT_HD = 0.125

LR, B1, B2, EPS_ADAM, WD, STEP = 0.001, 0.9, 0.999, 1e-08, 0.01, 10

TM = 512
TME = 256
CT = 512
BQ = 512
BK = 256
assert BQ == 2 * BK
HALO = 16
DEAD_LOG_WEIGHT = -104.0
VMEM_LIMIT = 56 * 1024 * 1024


def _dot(a, b):
    return jnp.dot(a, b, preferred_element_type=F32)


def _dot_nt(a, b):
    return lax.dot_general(a, b, (((1,), (1,)), ((), ())), preferred_element_type=F32)


def _dot_tn(a, b):
    return lax.dot_general(a, b, (((0,), (0,)), ((), ())), preferred_element_type=F32)


def _params(*sem):
    return pltpu.CompilerParams(dimension_semantics=sem, vmem_limit_bytes=VMEM_LIMIT)


def _rowsum(v):
    return jnp.sum(v, axis=0, keepdims=True)


def _modulated_norm(xv, vec_ref):
    r = lax.rsqrt(jnp.mean(xv * xv, axis=-1, keepdims=True) + EPS)
    return (((xv * r) * vec_ref[0:1, :]) * (1.0 + vec_ref[1:2, :]) + vec_ref[2:3, :]).astype(BF16)


def _norm_mod(x, vecs, name):
    s = x.shape[0]
    tm = min(TM, s)

    def body(x_ref, vec_ref, h_ref):
        h_ref[...] = _modulated_norm(x_ref[...], vec_ref)

    return pl.pallas_call(
        body, name=name, grid=(s // tm,),
        in_specs=[pl.BlockSpec((tm, D), lambda i: (i, 0)), pl.BlockSpec((8, D), lambda i: (0, 0))],
        out_specs=pl.BlockSpec((tm, D), lambda i: (i, 0)),
        out_shape=jax.ShapeDtypeStruct((s, D), BF16),
        compiler_params=_params("parallel"),
    )(x, vecs)


def _inproj(h, w, name, comm=None):
    s = h.shape[0]
    ns = w.shape[2]
    tm = min(TM, s)
    ni = s // tm
    comm = comm or _NO_COMM
    nci, nco = len(comm.arrays), len(comm.out_shape)

    def body(*refs):
        h_ref, w_ref = refs[:2]
        proj_ref = refs[2 + nci]
        cargs = (refs[2:2 + nci], refs[3 + nci:3 + nci + nco], refs[3 + nci + nco:])
        j, i = pl.program_id(0), pl.program_id(1)
        if nci:
            pl.when((j == 0) & (i == 0))(lambda: comm.start(*cargs))
        proj_ref[...] = _dot(h_ref[...], w_ref[...]).astype(BF16)
        if nci:
            pl.when((j == NCHIP - 1) & (i == ni - 1))(lambda: comm.finish(*cargs))

    return pl.pallas_call(
        body, name=name, grid=(NCHIP, ni),
        in_specs=[pl.BlockSpec((tm, D), lambda j, i: (i, 0)),
                  pl.BlockSpec((None, D, ns), lambda j, i: (j, 0, 0))] + [ANY] * nci,
        out_specs=[pl.BlockSpec((tm, ns), lambda j, i: (i, j))] + [ANY] * nco,
        out_shape=[jax.ShapeDtypeStruct((s, NCHIP * ns), BF16)] + comm.out_shape,
        scratch_shapes=comm.sems,
        compiler_params=_params("arbitrary", "arbitrary"),
    )(h, w, *comm.arrays)


def _pool_fwd(proj0, pw, pscale):
    s = proj0.shape[0]
    tm = min(TM, s)
    hb = tm // 16

    def body(u_ref, halo_ref, w_ref, sc_ref, p_ref, y_ref, ext_ref):
        i = pl.program_id(0)
        ext_ref[16:, :] = u_ref[...].astype(F32)
        ext_ref[0:16, :] = jnp.where(i > 0, halo_ref[...].astype(F32), 0.0)
        t = i * tm + lax.broadcasted_iota(jnp.int32, (tm, 1), 0)
        for g, wdw in enumerate(WINDOWS):
            cs = slice(g * PG, (g + 1) * PG)
            u = ext_ref[16:16 + tm, cs]
            acc = u
            for j in range(1, wdw):
                acc = acc + ext_ref[16 - j:16 - j + tm, cs]
            inv = 1.0 / jnp.minimum(t + 1, wdw).astype(F32)
            pb = (acc * inv - u).astype(BF16)
            p_ref[:, cs] = pb
            y_ref[:, cs] = _dot(pb, w_ref[g]) * sc_ref[:, cs]

    return pl.pallas_call(
        body, name="pool_fwd", grid=(s // tm,),
        in_specs=[pl.BlockSpec((tm, DP), lambda i: (i, 0)),
                  pl.BlockSpec((16, DP), lambda i: (jnp.maximum(i * hb - 1, 0), 0)),
                  pl.BlockSpec((4, PG, PG), lambda i: (0, 0, 0)),
                  pl.BlockSpec((1, DP), lambda i: (0, 0))],
        out_specs=[pl.BlockSpec((tm, DP), lambda i: (i, 0)),
                   pl.BlockSpec((tm, DP), lambda i: (i, 0))],
        out_shape=[jax.ShapeDtypeStruct((s, DP), BF16), jax.ShapeDtypeStruct((s, DP), F32)],
        scratch_shapes=[pltpu.VMEM((tm + 16, DP), F32)],
        compiler_params=_params("parallel"),
    )(proj0, proj0, pw, pscale)


def _sb_logits(nz, mask):
    neg_abs = lax.bitcast_convert_type(lax.bitcast_convert_type(nz, jnp.uint32) | jnp.uint32(0x80000000), F32)
    t = jnp.log(1.0 + jnp.exp(neg_abs))
    lf = jnp.minimum(nz, 0.0) - t
    lam = lf - nz
    if mask is not None:
        lf = jnp.where(mask, lf, 0.0)
    return lf, lam


def _sweep_left(steps, state, step):
    def live(carry):
        j, st = carry
        heaviest = functools.reduce(jnp.maximum, [jnp.max(head[1]) for head in st])
        return (j < steps) & (heaviest > DEAD_LOG_WEIGHT)

    return lax.while_loop(live, lambda carry: (carry[0] + 1, step(carry[0], carry[1])), (0, state))[1]


def _split_dot(v, tri):
    hi = v.astype(BF16)
    lo = (v - hi.astype(F32)).astype(BF16)
    return _dot(hi, tri) + _dot(lo, tri)


def _tri_masks():
    row = lax.broadcasted_iota(jnp.int32, (BK, BK), 0)
    col = lax.broadcasted_iota(jnp.int32, (BK, BK), 1)
    return (row > col).astype(BF16), (row >= col).astype(BF16)


def _causal_mask():
    row = lax.broadcasted_iota(jnp.int32, (BK, BK), 0)
    col = lax.broadcasted_iota(jnp.int32, (BK, BK), 1)
    return col < row


def _attn_fwd(proj0, comm=None):
    s = proj0.shape[0]
    nq = s // BQ
    kpq = BQ // BK
    nsteps = DP // 128
    comm = comm or _NO_COMM
    nci, nco = len(comm.arrays), len(comm.out_shape)

    def body(*refs):
        q_ref, k_ref, v_ref = refs[:3]
        o_ref = refs[3 + nci]
        qn_ref, k8_ref, vb_ref = refs[4 + nci + nco:7 + nci + nco]
        cargs = (refs[3:3 + nci], refs[4 + nci:4 + nci + nco], refs[7 + nci + nco:])
        if nci:
            pl.when(pl.program_id(0) == 0)(lambda: comm.start(*cargs))
        qn_ref[...] = (-q_ref[...]).astype(BF16)
        k8_ref[...] = (k_ref[...] * INV_SQRT_HD).astype(BF16)
        vb_ref[...] = v_ref[...].astype(BF16)
        after, _ = _tri_masks()
        causal = _causal_mask()
        heads = [slice(HD * h, HD * (h + 1)) for h in range(2)]
        hs = range(len(heads))
        lo, hi = (0, BK), (BK, BK)

        def qstep(qi, carry):
            q0 = pl.multiple_of(qi * BQ, BQ)
            qn = [qn_ref[pl.ds(q0, BQ), ls] for ls in heads]

            def sweep(lanes, state):
                nz = [_dot_nt(qn[h][r0:r0 + n], k8_ref[pl.ds(k0, BK), heads[h]]) for h, (r0, n), k0, _ in lanes]
                ll = [_sb_logits(nz[i], lane[3]) for i, lane in enumerate(lanes)]
                aft = [_dot(ll[i][0].astype(BF16), after) for i in range(len(lanes))]
                state = dict(state)
                for i, (h, rows, k0, mask) in enumerate(lanes):
                    o_acc, c = state[h, rows]
                    a = jnp.exp(ll[i][1] + aft[i] + c)
                    if mask is not None:
                        a = jnp.where(mask, a, 0.0)
                    o_acc = o_acc + _dot(a.astype(BF16), vb_ref[pl.ds(k0, BK), heads[h]])
                    state[h, rows] = (o_acc, c + aft[i][:, 0:1] + ll[i][0][:, 0:1])
                return state

            k_lo, k_hi = q0, pl.multiple_of(q0 + BK, BK)
            st = {(h, r): (jnp.zeros((BK, HD), F32), jnp.zeros((BK, 1), F32)) for h in hs for r in (lo, hi)}
            st = sweep([(h, hi, k_hi, causal) for h in hs] + [(h, lo, k_lo, causal) for h in hs]
                       + [(h, hi, k_lo, None) for h in hs], st)
            def left_of(rows):
                def step(j, part):
                    k0 = pl.multiple_of(q0 - (j + 1) * BK, BK)
                    got = sweep([(h, rows, k0, None) for h in hs], {(h, rows): part[h] for h in hs})
                    return tuple(got[h, rows] for h in hs)
                return step

            done = {r: _sweep_left(qi * (BQ // BK), tuple(st[h, r] for h in hs), left_of(r)) for r in (lo, hi)}
            for h, ls in enumerate(heads):
                o_ref[pl.ds(q0, BQ), ls] = jnp.concatenate([done[lo][h][0], done[hi][h][0]], axis=0)
            return carry

        lax.fori_loop(0, nq, qstep, 0)
        if nci:
            pl.when(pl.program_id(0) == nsteps - 1)(lambda: comm.finish(*cargs))

    return pl.pallas_call(
        body, name="attn_fwd", grid=(nsteps,),
        in_specs=[pl.BlockSpec((s, 128), lambda h: (0, 8 + h)),
                  pl.BlockSpec((s, 128), lambda h: (0, 16 + h)),
                  pl.BlockSpec((s, 128), lambda h: (0, 24 + h))] + [ANY] * nci,
        out_specs=[pl.BlockSpec((s, 128), lambda h: (0, h))] + [ANY] * nco,
        out_shape=[jax.ShapeDtypeStruct((s, DP), F32)] + comm.out_shape,
        scratch_shapes=[pltpu.VMEM((s, 128), BF16)] * 3 + comm.sems,
        compiler_params=_params("arbitrary"),
    )(proj0, proj0, proj0, *comm.arrays)


def _even_out(ypool, ysb, proj0, wout, x, vecs, vecs_next):
    s = x.shape[0]
    tm = min(TME, s)

    def body(yp_ref, ys_ref, gate_ref, w_ref, x_ref, vec_ref, vecn_ref, x1_ref, out_ref, yg_ref, hn_ref):
        gt = gate_ref[...].astype(F32)
        sl = gt * jax.nn.sigmoid(gt)
        yg_ref[:, :DP] = (yp_ref[...] * sl[:, :DP]).astype(BF16)
        yg_ref[:, DP:] = (ys_ref[...] * sl[:, DP:]).astype(BF16)
        out = _dot(yg_ref[...], w_ref[...])
        out_ref[...] = out
        x1 = x_ref[...] + (1.0 + vec_ref[3:4, :]) * out
        x1_ref[...] = x1
        hn_ref[...] = _modulated_norm(x1, vecn_ref)

    row = lambda i: (i, 0)
    const = lambda i: (0, 0)
    return pl.pallas_call(
        body, name="even_out", grid=(s // tm,),
        in_specs=[pl.BlockSpec((tm, DP), row), pl.BlockSpec((tm, DP), row),
                  pl.BlockSpec((tm, DI), lambda i: (i, 2)),
                  pl.BlockSpec((DI, D), const),
                  pl.BlockSpec((tm, D), row), pl.BlockSpec((8, D), const), pl.BlockSpec((8, D), const)],
        out_specs=[pl.BlockSpec((tm, D), row), pl.BlockSpec((tm, D), row), pl.BlockSpec((tm, DI), row),
                   pl.BlockSpec((tm, D), row)],
        out_shape=[jax.ShapeDtypeStruct((s, D), F32), jax.ShapeDtypeStruct((s, D), F32),
                   jax.ShapeDtypeStruct((s, DI), BF16), jax.ShapeDtypeStruct((s, D), BF16)],
        compiler_params=_params("parallel"),
    )(ypool, ysb, proj0, wout, x, vecs, vecs_next)


def _odd_out(proj1, wout, x1, vecs, cw, cb, target):
    s = x1.shape[0]
    tm = min(TME, s)
    hb = tm // HALO

    def body(gb_ref, gc_ref, u_ref, gt_ref, hgc_ref, hu_ref, w_ref, x1_ref, vec_ref, cw_ref, cb_ref, tg_ref,
             dx2_ref, out_ref, y1_ref, acc_ref, ext_ref):
        i = pl.program_id(0)

        @pl.when(i == 0)
        def _():
            acc_ref[...] = jnp.zeros_like(acc_ref)

        ext_ref[HALO:, :] = gc_ref[...].astype(F32) * u_ref[...].astype(F32)
        ext_ref[0:HALO, :] = jnp.where(i > 0, hgc_ref[...].astype(F32) * hu_ref[...].astype(F32), 0.0)
        for c in range(DI // CT):
            cs = slice(c * CT, (c + 1) * CT)
            conv = (cb_ref[0:1, cs] + cw_ref[0:1, cs] * ext_ref[HALO - 2:HALO - 2 + tm, cs]
                    + cw_ref[1:2, cs] * ext_ref[HALO - 1:HALO - 1 + tm, cs]
                    + cw_ref[2:3, cs] * ext_ref[HALO:HALO + tm, cs])
            gt = gt_ref[:, cs].astype(F32)
            y1_ref[:, cs] = (gb_ref[:, cs].astype(F32) * conv * (gt * jax.nn.sigmoid(gt))).astype(BF16)
        out = _dot(y1_ref[...], w_ref[...])
        out_ref[...] = out
        x2 = x1_ref[...] + (1.0 + vec_ref[3:4, :]) * out
        r = lax.rsqrt(jnp.mean(x2 * x2, axis=-1, keepdims=True) + EPS)
        nrm = x2 * r
        fg = vec_ref[4:5, :]
        err = nrm * fg - tg_ref[...]
        acc_ref[1:2, :] += _rowsum(err * err) * (0.5 / D)
        dyf = err * (1.0 / D)
        acc_ref[0:1, :] += _rowsum(dyf * nrm)
        dn = dyf * fg
        dx2_ref[...] = r * (dn - nrm * jnp.mean(dn * nrm, axis=-1, keepdims=True))

    row = lambda i: (i, 0)
    halo = lambda col: (lambda i: (jnp.maximum(i * hb - 1, 0), col))
    const = lambda i: (0, 0)
    return pl.pallas_call(
        body, name="odd_out", grid=(s // tm,),
        in_specs=[pl.BlockSpec((tm, DI), lambda i: (i, 0)), pl.BlockSpec((tm, DI), lambda i: (i, 1)),
                  pl.BlockSpec((tm, DI), lambda i: (i, 2)), pl.BlockSpec((tm, DI), lambda i: (i, 3)),
                  pl.BlockSpec((HALO, DI), halo(1)), pl.BlockSpec((HALO, DI), halo(2)),
                  pl.BlockSpec((DI, D), const), pl.BlockSpec((tm, D), row), pl.BlockSpec((8, D), const),
                  pl.BlockSpec((8, DI), const), pl.BlockSpec((1, DI), const), pl.BlockSpec((tm, D), row)],
        out_specs=[pl.BlockSpec((tm, D), row), pl.BlockSpec((tm, D), row), pl.BlockSpec((tm, DI), row),
                   pl.BlockSpec((8, D), const)],
        out_shape=[jax.ShapeDtypeStruct((s, D), F32), jax.ShapeDtypeStruct((s, D), F32),
                   jax.ShapeDtypeStruct((s, DI), BF16), jax.ShapeDtypeStruct((8, D), F32)],
        scratch_shapes=[pltpu.VMEM((tm + HALO, DI), F32)],
        compiler_params=_params("arbitrary"),
    )(proj1, proj1, proj1, proj1, proj1, proj1, wout, x1, vecs, cw, cb, target)


def _odd_bwd(dx2, out1, proj1, wout, vecs, cw, cb):
    s = dx2.shape[0]
    tm = min(TME, s)
    nb = s // tm
    hb = tm // HALO

    def body(dx2_ref, out1_ref, gb_ref, gc_ref, u_ref, gt_ref, hgc_ref, hu_ref, w_ref, vec_ref, cw_ref, cb_ref,
             dout_ref, dproj_ref, accv_ref, accd_ref, uext_ref, dext_ref, dy_ref):
        i = pl.program_id(0)
        blk = nb - 1 - i

        @pl.when(i == 0)
        def _():
            accv_ref[...] = jnp.zeros_like(accv_ref)
            accd_ref[...] = jnp.zeros_like(accd_ref)
            dext_ref[tm:tm + 8, :] = jnp.zeros((8, DI), F32)

        dx2v = dx2_ref[...]
        accd_ref[0:1, :] += _rowsum(dx2v * out1_ref[...])
        dout = (dx2v * (1.0 + vec_ref[3:4, :])).astype(BF16)
        dout_ref[...] = dout
        dy_ref[...] = _dot_nt(dout, w_ref[...])
        uext_ref[HALO:, :] = gc_ref[...].astype(F32) * u_ref[...].astype(F32)
        uext_ref[0:HALO, :] = jnp.where(blk > 0, hgc_ref[...].astype(F32) * hu_ref[...].astype(F32), 0.0)
        for c in range(DI // CT):
            cs = slice(c * CT, (c + 1) * CT)
            u0 = uext_ref[HALO - 2:HALO - 2 + tm, cs]
            u1 = uext_ref[HALO - 1:HALO - 1 + tm, cs]
            u2 = uext_ref[HALO:HALO + tm, cs]
            w0, w1, w2 = cw_ref[0:1, cs], cw_ref[1:2, cs], cw_ref[2:3, cs]
            conv = cb_ref[0:1, cs] + w0 * u0 + w1 * u1 + w2 * u2
            gt = gt_ref[:, cs].astype(F32)
            sg = jax.nn.sigmoid(gt)
            gb = gb_ref[:, cs].astype(F32)
            dy = dy_ref[:, cs]
            t1 = dy * (gt * sg)
            dproj_ref[:, cs] = (t1 * conv).astype(BF16)
            dconv = t1 * gb
            dproj_ref[:, 3 * DI + c * CT:3 * DI + (c + 1) * CT] = (
                dy * gb * conv * (sg * (1.0 + gt * (1.0 - sg)))).astype(BF16)
            accv_ref[0:1, cs] += _rowsum(dconv * u0)
            accv_ref[1:2, cs] += _rowsum(dconv * u1)
            accv_ref[2:3, cs] += _rowsum(dconv * u2)
            accv_ref[3:4, cs] += _rowsum(dconv)
            dext_ref[0:tm, cs] = dconv
            duu = w2 * dconv + w1 * dext_ref[1:tm + 1, cs] + w0 * dext_ref[2:tm + 2, cs]
            dproj_ref[:, DI + c * CT:DI + (c + 1) * CT] = (duu * u_ref[:, cs].astype(F32)).astype(BF16)
            dproj_ref[:, 2 * DI + c * CT:2 * DI + (c + 1) * CT] = (duu * gc_ref[:, cs].astype(F32)).astype(BF16)
        dext_ref[tm:tm + 8, :] = dext_ref[0:8, :]

    rrow = lambda i: (nb - 1 - i, 0)
    rcol = lambda col: (lambda i: (nb - 1 - i, col))
    halo = lambda col: (lambda i: (jnp.maximum((nb - 1 - i) * hb - 1, 0), col))
    const = lambda i: (0, 0)
    return pl.pallas_call(
        body, name="odd_bwd", grid=(nb,),
        in_specs=[pl.BlockSpec((tm, D), rrow), pl.BlockSpec((tm, D), rrow),
                  pl.BlockSpec((tm, DI), rcol(0)), pl.BlockSpec((tm, DI), rcol(1)),
                  pl.BlockSpec((tm, DI), rcol(2)), pl.BlockSpec((tm, DI), rcol(3)),
                  pl.BlockSpec((HALO, DI), halo(1)), pl.BlockSpec((HALO, DI), halo(2)),
                  pl.BlockSpec((DI, D), const), pl.BlockSpec((8, D), const),
                  pl.BlockSpec((8, DI), const), pl.BlockSpec((1, DI), const)],
        out_specs=[pl.BlockSpec((tm, D), rrow), pl.BlockSpec((tm, NO), rrow),
                   pl.BlockSpec((8, DI), const), pl.BlockSpec((8, D), const)],
        out_shape=[jax.ShapeDtypeStruct((s, D), BF16), jax.ShapeDtypeStruct((s, NO), BF16),
                   jax.ShapeDtypeStruct((8, DI), F32), jax.ShapeDtypeStruct((8, D), F32)],
        scratch_shapes=[pltpu.VMEM((tm + HALO, DI), F32), pltpu.VMEM((tm + 8, DI), F32), pltpu.VMEM((tm, DI), F32)],
        compiler_params=_params("arbitrary"),
    )(dx2, out1, proj1, proj1, proj1, proj1, proj1, proj1, wout, vecs, cw, cb)


def _grad_w_cols(a, b, name):
    s, m = a.shape
    ns = b.shape[1] // NCHIP
    ts = min(TM, s)

    def body(a_ref, b_ref, o_ref):
        @pl.when(pl.program_id(1) == 0)
        def _():
            o_ref[...] = jnp.zeros_like(o_ref)

        o_ref[...] += _dot_tn(a_ref[...], b_ref[...])

    return pl.pallas_call(
        body, name=name, grid=(NCHIP, s // ts),
        in_specs=[pl.BlockSpec((ts, m), lambda j, k: (k, 0)),
                  pl.BlockSpec((ts, ns), lambda j, k: (k, j))],
        out_specs=pl.BlockSpec((None, m, ns), lambda j, k: (j, 0, 0)),
        out_shape=jax.ShapeDtypeStruct((NCHIP, m, ns), F32),
        compiler_params=_params("parallel", "arbitrary"),
    )(a, b)


def _grad_w_rows(a, b, name):
    s = a.shape[0]
    ms = a.shape[1] // NCHIP
    n = b.shape[1]
    ts = min(TM, s)

    def body(a_ref, b_ref, o_ref):
        @pl.when(pl.program_id(1) == 0)
        def _():
            o_ref[...] = jnp.zeros_like(o_ref)

        o_ref[...] += _dot_tn(a_ref[...], b_ref[...])

    return pl.pallas_call(
        body, name=name, grid=(NCHIP, s // ts),
        in_specs=[pl.BlockSpec((ts, ms), lambda i, k: (k, i)),
                  pl.BlockSpec((ts, n), lambda i, k: (k, 0))],
        out_specs=pl.BlockSpec((None, ms, n), lambda i, k: (i, 0, 0)),
        out_shape=jax.ShapeDtypeStruct((NCHIP, ms, n), F32),
        compiler_params=_params("parallel", "arbitrary"),
    )(a, b)


def _inproj_bwd(dproj, w, x, dx_in, vecs, name, comm=None):
    s = x.shape[0]
    ns = w.shape[2]
    tm = min(TME, s)
    ni = s // tm
    comm = comm or _NO_COMM
    nci, nco = len(comm.arrays), len(comm.out_shape)

    def body(*refs):
        dp_ref, w_hbm, x_ref, dxin_ref, vec_ref = refs[:5]
        dx_ref, acc_ref = refs[5 + nci:7 + nci]
        w_ref = refs[7 + nci + nco]
        cargs = (refs[5:5 + nci], refs[7 + nci:7 + nci + nco], refs[8 + nci + nco:])
        i = pl.program_id(0)

        @pl.when(i == 0)
        def _():
            acc_ref[...] = jnp.zeros_like(acc_ref)
            if nci:
                comm.start(*cargs)
            pltpu.sync_copy(w_hbm, w_ref)

        dh = _dot_nt(dp_ref[:, 0:ns], w_ref[0])
        for j in range(1, NCHIP):
            dh = dh + _dot_nt(dp_ref[:, j * ns:(j + 1) * ns], w_ref[j])
        xv = x_ref[...]
        r = lax.rsqrt(jnp.mean(xv * xv, axis=-1, keepdims=True) + EPS)
        nrm = xv * r
        g = vec_ref[0:1, :]
        sc1 = 1.0 + vec_ref[1:2, :]
        dhn = dh * nrm
        acc_ref[0:1, :] += _rowsum(dh)
        acc_ref[1:2, :] += _rowsum(dhn) * g
        acc_ref[2:3, :] += _rowsum(dhn) * sc1
        dn = dh * (g * sc1)
        dx_ref[...] = dxin_ref[...] + r * (dn - nrm * jnp.mean(dn * nrm, axis=-1, keepdims=True))

        if nci:
            pl.when(i == ni - 1)(lambda: comm.finish(*cargs))

    row = lambda i: (i, 0)
    const = lambda i: (0, 0)
    return pl.pallas_call(
        body, name=name, grid=(ni,),
        in_specs=[pl.BlockSpec((tm, NCHIP * ns), row), ANY,
                  pl.BlockSpec((tm, D), row), pl.BlockSpec((tm, D), row), pl.BlockSpec((8, D), const)] + [ANY] * nci,
        out_specs=[pl.BlockSpec((tm, D), row), pl.BlockSpec((8, D), const)] + [ANY] * nco,
        out_shape=[jax.ShapeDtypeStruct((s, D), F32), jax.ShapeDtypeStruct((8, D), F32)] + comm.out_shape,
        scratch_shapes=[pltpu.VMEM(w.shape, BF16)] + comm.sems,
        compiler_params=_params("arbitrary"),
    )(dproj, w, x, dx_in, vecs, *comm.arrays)


def _even_bwd(dx1, out0, ypool, ysb, proj0, wout, vecs):
    s = dx1.shape[0]
    tm = min(TME, s)

    def body(dx1_ref, out0_ref, yp_ref, ys_ref, gate_ref, w_ref, vec_ref,
             dout_ref, dyp_ref, dys_ref, dgt_ref, acc_ref):
        @pl.when(pl.program_id(0) == 0)
        def _():
            acc_ref[...] = jnp.zeros_like(acc_ref)

        dx1v = dx1_ref[...]
        acc_ref[0:1, :] += _rowsum(dx1v * out0_ref[...])
        dout = (dx1v * (1.0 + vec_ref[3:4, :])).astype(BF16)
        dout_ref[...] = dout
        dyg = _dot_nt(dout, w_ref[...])
        gt = gate_ref[...].astype(F32)
        sg = jax.nn.sigmoid(gt)
        sl = gt * sg
        dsl = sg * (1.0 + gt * (1.0 - sg))
        dyp_ref[...] = dyg[:, :DP] * sl[:, :DP]
        dys_ref[...] = dyg[:, DP:] * sl[:, DP:]
        dgt_ref[:, :DP] = (dyg[:, :DP] * yp_ref[...] * dsl[:, :DP]).astype(BF16)
        dgt_ref[:, DP:] = (dyg[:, DP:] * ys_ref[...] * dsl[:, DP:]).astype(BF16)

    row = lambda i: (i, 0)
    const = lambda i: (0, 0)
    return pl.pallas_call(
        body, name="even_bwd", grid=(s // tm,),
        in_specs=[pl.BlockSpec((tm, D), row), pl.BlockSpec((tm, D), row),
                  pl.BlockSpec((tm, DP), row), pl.BlockSpec((tm, DP), row),
                  pl.BlockSpec((tm, DI), lambda i: (i, 2)),
                  pl.BlockSpec((DI, D), const), pl.BlockSpec((8, D), const)],
        out_specs=[pl.BlockSpec((tm, D), row), pl.BlockSpec((tm, DP), row), pl.BlockSpec((tm, DP), row),
                   pl.BlockSpec((tm, DI), lambda i: (i, NE // DI - 1)), pl.BlockSpec((8, D), const)],
        out_shape=[jax.ShapeDtypeStruct((s, D), BF16), jax.ShapeDtypeStruct((s, DP), F32),
                   jax.ShapeDtypeStruct((s, DP), F32), jax.ShapeDtypeStruct((s, NE), BF16),
                   jax.ShapeDtypeStruct((8, D), F32)],
        compiler_params=_params("arbitrary"),
    )(dx1, out0, ypool, ysb, proj0, wout, vecs)


def _pool_bwd(dyp, p, pw, pscale, dproj):
    s = dyp.shape[0]
    tm = min(TM, s)
    nb = s // tm
    hb = tm // 16

    def body(dy_ref, dyh_ref, p_ref, w_ref, sc_ref, dproj_hbm, du_ref, dw_ref, acc_ref, ext_ref):
        i = pl.program_id(0)

        @pl.when(i == 0)
        def _():
            dw_ref[...] = jnp.zeros_like(dw_ref)
            acc_ref[...] = jnp.zeros_like(acc_ref)

        t = i * tm + lax.broadcasted_iota(jnp.int32, (tm + 16, 1), 0)
        for g, wdw in enumerate(WINDOWS):
            cs = slice(g * PG, (g + 1) * PG)
            sc = sc_ref[:, cs]
            dy = dy_ref[:, cs]
            dyh = jnp.where(i < nb - 1, dyh_ref[:, cs], 0.0)
            pb = p_ref[:, cs]
            wg = w_ref[g]
            acc_ref[0:1, cs] += _rowsum(dy * _dot(pb, wg))
            dypre = (dy * sc).astype(BF16)
            dw_ref[g] += _dot_tn(pb, dypre)
            dp = _dot_nt(dypre, wg)
            dph = _dot_nt((dyh * sc).astype(BF16), wg)
            inv = 1.0 / jnp.minimum(t + 1, wdw).astype(F32)
            ext_ref[0:tm, cs] = dp * inv[0:tm]
            ext_ref[tm:tm + 16, cs] = dph * inv[tm:tm + 16]
            acc = ext_ref[0:tm, cs]
            for j in range(1, wdw):
                acc = acc + ext_ref[j:j + tm, cs]
            du_ref[:, cs] = (acc - dp).astype(BF16)

    row = lambda i: (i, 0)
    return pl.pallas_call(
        body, name="pool_bwd", grid=(nb,),
        in_specs=[pl.BlockSpec((tm, DP), row),
                  pl.BlockSpec((16, DP), lambda i: (jnp.minimum((i + 1) * hb, s // 16 - 1), 0)),
                  pl.BlockSpec((tm, DP), row),
                  pl.BlockSpec((4, PG, PG), lambda i: (0, 0, 0)),
                  pl.BlockSpec((1, DP), lambda i: (0, 0)), ANY],
        out_specs=[pl.BlockSpec((tm, DP), row), pl.BlockSpec((4, PG, PG), lambda i: (0, 0, 0)),
                   pl.BlockSpec((8, DP), lambda i: (0, 0))],
        out_shape=[jax.ShapeDtypeStruct((s, NE), BF16), jax.ShapeDtypeStruct((4, PG, PG), F32),
                   jax.ShapeDtypeStruct((8, DP), F32)],
        input_output_aliases={5: 0},
        scratch_shapes=[pltpu.VMEM((tm + 16, DP), F32)],
        compiler_params=_params("arbitrary"),
    )(dyp, dyp, p, pw, pscale, dproj)


def _attn_bwd(proj0, ysb, dys, dproj, comm=None):
    s = proj0.shape[0]
    nq = s // BQ
    kpq = BQ // BK
    nsteps = DP // 128
    comm = comm or _NO_COMM
    nci, nco = len(comm.arrays), len(comm.out_shape)

    def body(*refs):
        q_ref, k_ref, v_ref, o_ref, do_ref = refs[:5]
        dproj_hbm = refs[6 + nci]
        scratch = refs[7 + nci + nco:]
        qn_ref, k8_ref, vb_ref, dob_ref, dka_ref, dva_ref, dq_ref, dk_ref, dv_ref, part_sems = scratch[:10]
        cargs = (refs[6:6 + nci], refs[7 + nci:7 + nci + nco], scratch[10:])
        step = pl.program_id(0)
        if nci:
            pl.when(step == 0)(lambda: comm.start(*cargs))

        qn_ref[...] = (-q_ref[...]).astype(BF16)
        k8_ref[...] = (k_ref[...] * INV_SQRT_HD).astype(BF16)
        vb_ref[...] = v_ref[...].astype(BF16)
        dob_ref[...] = do_ref[...].astype(BF16)
        dka_ref[...] = jnp.zeros_like(dka_ref)
        dva_ref[...] = jnp.zeros_like(dva_ref)
        after, from_on = _tri_masks()
        causal = _causal_mask()
        heads = [slice(HD * h, HD * (h + 1)) for h in range(2)]
        hs = range(len(heads))
        lo, hi = (0, BK), (BK, BK)

        def qstep(qi, carry):
            q0 = pl.multiple_of(qi * BQ, BQ)
            qn = [qn_ref[pl.ds(q0, BQ), ls] for ls in heads]
            do = [dob_ref[pl.ds(q0, BQ), ls] for ls in heads]
            total = [jnp.sum(do[h].astype(F32) * o_ref[pl.ds(q0, BQ), ls], axis=1, keepdims=True)
                     for h, ls in enumerate(heads)]

            def sweep(lanes, state):
                rows_of = lambda i: slice(lanes[i][1][0], lanes[i][1][0] + lanes[i][1][1])
                k8 = [k8_ref[pl.ds(k0, BK), heads[h]] for h, _, k0, _ in lanes]
                nz = [_dot_nt(qn[lane[0]][rows_of(i)], k8[i]) for i, lane in enumerate(lanes)]
                da = [_dot_nt(do[h][rows_of(i)], vb_ref[pl.ds(k0, BK), heads[h]]) for i, (h, _, k0, _) in enumerate(lanes)]
                ll = [_sb_logits(nz[i], lane[3]) for i, lane in enumerate(lanes)]
                aft = [_dot(ll[i][0].astype(BF16), after) for i in range(len(lanes))]
                state = dict(state)
                for i, (h, rows, k0, mask) in enumerate(lanes):
                    dq_acc, c, cg = state[h, rows]
                    a = jnp.exp(ll[i][1] + aft[i] + c)
                    if mask is not None:
                        a = jnp.where(mask, a, 0.0)
                    ab = a.astype(BF16)
                    g = da[i] * ab.astype(F32)
                    suf = _split_dot(g, from_on)
                    dz = g - jnp.exp(ll[i][1]) * (g + ((total[h][rows_of(i)] - cg) - suf))
                    if mask is not None:
                        dz = jnp.where(mask, dz, 0.0)
                    dzb = dz.astype(BF16)
                    dka_ref[pl.ds(k0, BK), heads[h]] += _dot_tn(dzb, qn[h][rows_of(i)])
                    dva_ref[pl.ds(k0, BK), heads[h]] += _dot_tn(ab, do[h][rows_of(i)])
                    state[h, rows] = (dq_acc + _dot(dzb, k8[i]), c + aft[i][:, 0:1] + ll[i][0][:, 0:1],
                                      cg + suf[:, 0:1])
                return state

            k_lo, k_hi = q0, pl.multiple_of(q0 + BK, BK)
            zero = jnp.zeros((BK, 1), F32)
            st = {(h, r): (jnp.zeros((BK, HD), F32), zero, zero) for h in hs for r in (lo, hi)}
            st = sweep([(h, hi, k_hi, causal) for h in hs] + [(h, lo, k_lo, causal) for h in hs]
                       + [(h, hi, k_lo, None) for h in hs], st)
            def left_of(rows):
                def step(j, part):
                    k0 = pl.multiple_of(q0 - (j + 1) * BK, BK)
                    got = sweep([(h, rows, k0, None) for h in hs], {(h, rows): part[h] for h in hs})
                    return tuple(got[h, rows] for h in hs)
                return step

            done = {r: _sweep_left(qi * (BQ // BK), tuple(st[h, r] for h in hs), left_of(r)) for r in (lo, hi)}
            for h, ls in enumerate(heads):
                dq_ref[pl.ds(q0, BQ), ls] = jnp.concatenate([done[lo][h][0], done[hi][h][0]], axis=0).astype(BF16)
            return carry

        lax.fori_loop(0, nq, qstep, 0)
        dk_ref[...] = (dka_ref[...] * (-INV_SQRT_HD)).astype(BF16)
        dv_ref[...] = dva_ref[...].astype(BF16)
        lanes0 = pl.multiple_of(step * 128, 128)
        parts = [pltpu.make_async_copy(src, dproj_hbm.at[:, pl.ds((1 + k) * DP + lanes0, 128)], part_sems.at[k])
                 for k, src in enumerate((dq_ref, dk_ref, dv_ref))]
        for cp in parts:
            cp.start()
        for cp in parts:
            cp.wait()
        if nci:
            pl.when(step == nsteps - 1)(lambda: comm.finish(*cargs))

    col = lambda h: (0, h)
    return pl.pallas_call(
        body, name="attn_bwd", grid=(nsteps,),
        in_specs=[pl.BlockSpec((s, 128), lambda h: (0, 8 + h)),
                  pl.BlockSpec((s, 128), lambda h: (0, 16 + h)),
                  pl.BlockSpec((s, 128), lambda h: (0, 24 + h)),
                  pl.BlockSpec((s, 128), col), pl.BlockSpec((s, 128), col), ANY] + [ANY] * nci,
        out_specs=[ANY] * (1 + nco),
        out_shape=[jax.ShapeDtypeStruct((s, NE), BF16)] + comm.out_shape,
        input_output_aliases={5: 0},
        scratch_shapes=([pltpu.VMEM((s, 128), BF16)] * 4 + [pltpu.VMEM((s, 128), F32)] * 2
                        + [pltpu.VMEM((s, 128), BF16)] * 3 + [pltpu.SemaphoreType.DMA((3,))] + comm.sems),
        compiler_params=_params("arbitrary"),
    )(proj0, proj0, proj0, ysb, dys, dproj, *comm.arrays)


def _adamw_math(w, g, m, v):
    m2 = B1 * m + (1.0 - B1) * g
    v2 = B2 * v + (1.0 - B2) * (g * g)
    m_hat = m2 / (1.0 - B1 ** STEP)
    v_hat = v2 / (1.0 - B2 ** STEP)
    return -LR * (m_hat / (jnp.sqrt(v_hat) + EPS_ADAM) + WD * w), m2, v2


def _adamw(w, g, m, v, name):
    r, c = w.shape
    tr = r
    while tr * c * 4 > (1 << 20) and tr % 16 == 0:
        tr //= 2

    def body(w_ref, g_ref, m_ref, v_ref, d_ref, m2_ref, v2_ref):
        d_ref[...], m2_ref[...], v2_ref[...] = _adamw_math(w_ref[...], g_ref[...], m_ref[...], v_ref[...])

    spec = pl.BlockSpec((tr, c), lambda i: (i, 0))
    return pl.pallas_call(
        body, name=name, grid=(r // tr,),
        in_specs=[spec] * 4, out_specs=[spec] * 3,
        out_shape=[jax.ShapeDtypeStruct((r, c), F32)] * 3,
        compiler_params=_params("parallel"),
    )(w, g, m, v)


def _local_step(x, target, vecs0, vecs1, win0, rest, pscale, cw8, cb, sel=None):
    dist = sel is not None
    h0 = _norm_mod(x, vecs0, "norm0")
    proj0, *got = _inproj(h0, win0, "inproj0", _gather_comm(list(rest[0:2])) if dist else None)
    pw, wout0 = (_group_major(got[0]), got[1].reshape(DI, D)) if dist else rest[0:2]
    p, ypool = _pool_fwd(proj0, pw, pscale)
    ysb, *got = _attn_fwd(proj0, _gather_comm(list(rest[2:4])) if dist else None)
    win1, wout1 = (got[0], got[1].reshape(DI, D)) if dist else rest[2:4]
    x1, out0, yg, h1 = _even_out(ypool, ysb, proj0, wout0, x, vecs0, vecs1)
    proj1, = _inproj(h1, win1, "inproj1")
    dx2, out1, y1, acc_f = _odd_out(proj1, wout1, x1, vecs1, cw8, cb, target)

    def chip_partials(grads, names):
        from_sibling = _send_halves(grads, "rs_send_halves_" + names[0])
        part = [_add_halves(g, t, sel[1:2], "rs_add_halves_" + nm) for g, t, nm in zip(grads, from_sibling, names)]
        return [p32 for p32, _ in part], _exchange_comm([p16 for _, p16 in part])

    dout1, dproj1, acc_cv, acc_g1 = _odd_bwd(dx2, out1, proj1, wout1, vecs1, cw8, cb)
    g_wout1 = _grad_w_rows(y1, dout1, "grad_wout1")
    g_win1 = _grad_w_cols(h1, dproj1, "grad_win1")
    dx1, acc_n1 = _inproj_bwd(dproj1, win1, x1, dx2, vecs1, "inproj1_bwd")

    dout0, dyp, dys, dproj0, acc_g0 = _even_bwd(dx1, out0, ypool, ysb, proj0, wout0, vecs0)
    g_wout0 = _grad_w_rows(yg, dout0, "grad_wout0")
    dproj0, g_pw, acc_ps = _pool_bwd(dyp, p, pw, pscale, dproj0)
    early = [_chip_major(g_pw, PG // NCHIP), g_wout0, g_win1, g_wout1]
    part_a, swap_a = chip_partials(early, ["pool", "wout0", "win1", "wout1"]) if dist else (None, None)
    dproj0, *got_a = _attn_bwd(proj0, ysb, dys, dproj0, swap_a)
    g_win0 = _grad_w_cols(h0, dproj0, "grad_win0")
    part_b, swap_b = chip_partials([g_win0], ["win0"]) if dist else (None, None)
    dx0, acc_n0, *got_b = _inproj_bwd(dproj0, win0, x, dx1, vecs0, "inproj0_bwd", swap_b)

    if dist:
        names = ["win0", "pool", "wout0", "win1", "wout1"]
        halves = [_add_partials(p32, t, sel, "rs_add_partials_" + nm)
                  for p32, t, nm in zip(part_b + part_a, got_b + got_a, names)]
        grads = tuple(_join_halves(halves))
    else:
        grads = (g_win0, g_pw, g_wout0, g_win1, g_wout1)

    sums = dict(
        dm0=jnp.concatenate([acc_n0[0:2], acc_g0[0:1]], axis=0),
        dm1=jnp.concatenate([acc_n1[0:2], acc_g1[0:1]], axis=0),
        norm_g=jnp.concatenate([acc_n0[2:3], acc_n1[2:3]], axis=0),
        pool_scale=acc_ps[0:1], final_g=acc_f[0:1], loss=acc_f[1:2],
        conv_w=acc_cv[0:3], conv_b=acc_cv[3:4])
    return dx0, grads, sums


ANY = pl.BlockSpec(memory_space=pl.ANY)
CHIP_FLIPS = ((1, 0), (0, 1), (1, 1))


def _place():
    return lax.axis_index("x"), lax.axis_index("y"), lax.axis_index("c")


def _flip(v, f):
    return 1 - v if f else v


def _allgather8(v, name):
    m_per, n = v.shape

    def body(x_ref, out_ref, send_sems, recv_sems, local_sem):
        x, y, c = _place()
        me, sibling = (x, y, c), (x, y, 1 - c)
        chips = [(_flip(x, fx), _flip(y, fy)) for fx, fy in CHIP_FLIPS]

        def rows(px, py, pc):
            return out_ref.at[pl.ds((4 * px + 2 * py + pc) * m_per, m_per), :]

        def copy(k, block, to, src=None):
            return pltpu.make_async_remote_copy(
                src_ref=rows(*block) if src is None else src, dst_ref=rows(*block),
                send_sem=send_sems.at[k], recv_sem=recv_sems.at[k], device_id=to, device_id_type=MESH)

        mine = pltpu.make_async_copy(x_ref, rows(*me), local_sem)
        mine.start()
        first = [copy(0, me, sibling, src=x_ref)]
        first += [copy(1 + j, me, (*chip, c), src=x_ref) for j, chip in enumerate(chips)]
        for cp in first:
            cp.start()
        passed = [copy(4 + j, (*chip, c), sibling) for j, chip in enumerate(chips)]
        for j, chip in enumerate(chips):
            copy(1 + j, (*chip, c), me).wait_recv()
            passed[j].start()
        copy(0, sibling, me).wait_recv()
        for j, chip in enumerate(chips):
            copy(4 + j, (*chip, 1 - c), me).wait_recv()
        for cp in first + passed:
            cp.wait_send()
        mine.wait()

    return pl.pallas_call(
        body, name=name,
        out_shape=jax.ShapeDtypeStruct((NDEV * m_per, n), v.dtype),
        in_specs=[pl.BlockSpec(memory_space=pltpu.VMEM)],
        out_specs=pl.BlockSpec(memory_space=pltpu.VMEM),
        scratch_shapes=[pltpu.SemaphoreType.DMA((7,)), pltpu.SemaphoreType.DMA((7,)), pltpu.SemaphoreType.DMA],
    )(v)


class _Comm:
    def __init__(self, arrays, out_shape, sems, start, finish):
        self.arrays, self.out_shape, self.sems, self.start, self.finish = arrays, out_shape, sems, start, finish


_NO_COMM = _Comm([], [], [], None, None)


def _run_comm(comm, name):
    n = len(comm.arrays)

    def body(*refs):
        args = (refs[:n], refs[n:n + len(comm.out_shape)], refs[n + len(comm.out_shape):])
        comm.start(*args)
        comm.finish(*args)

    return pl.pallas_call(
        body, name=name, out_shape=comm.out_shape,
        in_specs=[ANY] * n, out_specs=[ANY] * len(comm.out_shape), scratch_shapes=comm.sems,
    )(*comm.arrays)


def _gather_comm(shards):
    n = len(shards)

    def pieces(ins, outs, sems, kinds):
        x, y, c = _place()
        ici_send, ici_recv, fwd_send, fwd_recv, local = sems
        own, sibling = 2 * x + y, (x, y, 1 - c)
        made = {kind: [] for kind in kinds}
        for w in range(n):
            r2 = ins[w].shape[0] // 2
            mine, other = pl.ds(c * r2, r2), pl.ds((1 - c) * r2, r2)
            if "local" in kinds:
                made["local"].append(pltpu.make_async_copy(ins[w], outs[w].at[own], local.at[w]))
            for d, (fx, fy) in enumerate(CHIP_FLIPS):
                px, py, k = _flip(x, fx), _flip(y, fy), 3 * w + d
                peer = 2 * px + py
                ici = dict(send_sem=ici_send.at[k], recv_sem=ici_recv.at[k], device_id=(px, py, c), device_id_type=MESH)
                fwd = dict(send_sem=fwd_send.at[k], recv_sem=fwd_recv.at[k], device_id=sibling, device_id_type=MESH)
                if "ici_out" in kinds:
                    made["ici_out"].append(pltpu.make_async_remote_copy(
                        src_ref=ins[w].at[mine, :], dst_ref=outs[w].at[own, mine, :], **ici))
                if "ici_in" in kinds:
                    made["ici_in"].append(pltpu.make_async_remote_copy(
                        src_ref=ins[w].at[mine, :], dst_ref=outs[w].at[peer, mine, :], **ici))
                if "fwd_out" in kinds:
                    made["fwd_out"].append(pltpu.make_async_remote_copy(
                        src_ref=outs[w].at[peer, mine, :], dst_ref=outs[w].at[peer, mine, :], **fwd))
                if "fwd_in" in kinds:
                    made["fwd_in"].append(pltpu.make_async_remote_copy(
                        src_ref=outs[w].at[peer, other, :], dst_ref=outs[w].at[peer, other, :], **fwd))
        return made

    def start(ins, outs, sems):
        made = pieces(ins, outs, sems, ("local", "ici_out"))
        for cp in made["local"] + made["ici_out"]:
            cp.start()

    def finish(ins, outs, sems):
        made = pieces(ins, outs, sems, ("local", "ici_out", "ici_in", "fwd_out", "fwd_in"))
        for arrived, onward in zip(made["ici_in"], made["fwd_out"]):
            arrived.wait_recv()
            onward.start()
        for cp in made["fwd_in"]:
            cp.wait_recv()
        for cp in made["ici_out"] + made["fwd_out"]:
            cp.wait_send()
        for cp in made["local"]:
            cp.wait()

    dma = pltpu.SemaphoreType.DMA
    return _Comm(list(shards), [jax.ShapeDtypeStruct((NCHIP,) + a.shape, a.dtype) for a in shards],
                 [dma((3 * n,))] * 4 + [dma((n,))], start, finish)


def _exchange_comm(parts):
    n = len(parts)

    def copies(ins, outs, sems):
        x, y, c = _place()
        send_sems, recv_sems = sems
        out = []
        for w in range(n):
            for d, (fx, fy) in enumerate(CHIP_FLIPS):
                px, py = _flip(x, fx), _flip(y, fy)
                out.append(pltpu.make_async_remote_copy(
                    src_ref=ins[w].at[2 * px + py], dst_ref=outs[w].at[d], send_sem=send_sems.at[3 * w + d],
                    recv_sem=recv_sems.at[3 * w + d], device_id=(px, py, c), device_id_type=MESH))
        return out

    def start(ins, outs, sems):
        for cp in copies(ins, outs, sems):
            cp.start()

    def finish(ins, outs, sems):
        cps = copies(ins, outs, sems)
        for cp in cps:
            cp.wait_recv()
        for cp in cps:
            cp.wait_send()

    dma = pltpu.SemaphoreType.DMA
    return _Comm(list(parts), [jax.ShapeDtypeStruct((3,) + p.shape[1:], BF16) for p in parts],
                 [dma((3 * n,))] * 2, start, finish)


def _send_halves(grads, name):
    n = len(grads)

    def body(*refs):
        ins, outs = refs[:n], refs[n:2 * n]
        send_sems, recv_sems = refs[2 * n:]
        x, y, c = _place()
        copies = []
        for w in range(n):
            r2 = ins[w].shape[1] // 2
            cp = pltpu.make_async_remote_copy(
                src_ref=ins[w].at[:, pl.ds((1 - c) * r2, r2), :], dst_ref=outs[w],
                send_sem=send_sems.at[w], recv_sem=recv_sems.at[w], device_id=(x, y, 1 - c), device_id_type=MESH)
            cp.start()
            copies.append(cp)
        for cp in copies:
            cp.wait_recv()
        for cp in copies:
            cp.wait_send()

    return pl.pallas_call(
        body, name=name,
        out_shape=[jax.ShapeDtypeStruct((NCHIP, g.shape[1] // 2, g.shape[2]), F32) for g in grads],
        in_specs=[ANY] * n, out_specs=[ANY] * n,
        scratch_shapes=[pltpu.SemaphoreType.DMA((n,)), pltpu.SemaphoreType.DMA((n,))],
    )(*grads)


def _row_tile(rows, cols):
    tr = rows
    while tr * cols * 4 > (1 << 20) and tr % 16 == 0:
        tr //= 2
    return tr


def _add_halves(g, t1, core, name):
    _, r, cdim = g.shape
    r2 = r // 2
    tr = _row_tile(r2, cdim)
    nt = r2 // tr

    def body(core_ref, g_ref, t_ref, p_ref, pb_ref):
        p = g_ref[...] + t_ref[...]
        p_ref[...] = p
        pb_ref[...] = p.astype(BF16)

    blk = pl.BlockSpec((None, tr, cdim), lambda j, i, core_ref: (j, i, 0))
    return pl.pallas_call(
        body, name=name,
        grid_spec=pltpu.PrefetchScalarGridSpec(
            num_scalar_prefetch=1, grid=(NCHIP, nt),
            in_specs=[pl.BlockSpec((None, tr, cdim), lambda j, i, core_ref: (j, core_ref[0] * nt + i, 0)), blk],
            out_specs=[blk, blk]),
        out_shape=[jax.ShapeDtypeStruct((NCHIP, r2, cdim), F32), jax.ShapeDtypeStruct((NCHIP, r2, cdim), BF16)],
        compiler_params=_params("parallel", "parallel"),
    )(core, g, t1)


def _add_partials(p, t2, sel, name):
    _, r2, cdim = p.shape
    tr = _row_tile(r2, cdim)
    nt = r2 // tr

    def body(sel_ref, p_ref, t_ref, o_ref):
        o_ref[...] = ((p_ref[...] + t_ref[0].astype(F32)) + t_ref[1].astype(F32)) + t_ref[2].astype(F32)

    return pl.pallas_call(
        body, name=name,
        grid_spec=pltpu.PrefetchScalarGridSpec(
            num_scalar_prefetch=1, grid=(nt,),
            in_specs=[pl.BlockSpec((None, tr, cdim), lambda i, sel_ref: (sel_ref[0], i, 0)),
                      pl.BlockSpec((3, tr, cdim), lambda i, sel_ref: (0, i, 0))],
            out_specs=pl.BlockSpec((tr, cdim), lambda i, sel_ref: (sel_ref[1] * nt + i, 0))),
        out_shape=jax.ShapeDtypeStruct((2 * r2, cdim), F32),
        compiler_params=_params("parallel"),
    )(sel, p, t2)


def _join_halves(grads):
    n = len(grads)

    def body(*refs):
        bufs = refs[n:2 * n]
        send_sems, recv_sems = refs[2 * n:]
        x, y, c = _place()
        copies = []
        for w in range(n):
            r2 = bufs[w].shape[0] // 2
            mine = bufs[w].at[pl.ds(c * r2, r2), :]
            cp = pltpu.make_async_remote_copy(
                src_ref=mine, dst_ref=mine, send_sem=send_sems.at[w], recv_sem=recv_sems.at[w],
                device_id=(x, y, 1 - c), device_id_type=MESH)
            cp.start()
            copies.append(cp)
        for w in range(n):
            r2 = bufs[w].shape[0] // 2
            theirs = bufs[w].at[pl.ds((1 - c) * r2, r2), :]
            pltpu.make_async_remote_copy(
                src_ref=theirs, dst_ref=theirs, send_sem=send_sems.at[w], recv_sem=recv_sems.at[w],
                device_id=(x, y, 1 - c), device_id_type=MESH).wait_recv()
        for cp in copies:
            cp.wait_send()

    return pl.pallas_call(
        body, name="rs_join_halves",
        out_shape=[jax.ShapeDtypeStruct(g.shape, F32) for g in grads],
        in_specs=[ANY] * n, out_specs=[ANY] * n, input_output_aliases={w: w for w in range(n)},
        scratch_shapes=[pltpu.SemaphoreType.DMA((n,)), pltpu.SemaphoreType.DMA((n,))],
    )(*grads)


def _ada_fwd(c_all, ada_w):
    nl, _, ns = ada_w.shape

    def body(c_ref, w_ref, o_ref):
        cv = c_ref[...]
        o_ref[...] = _dot((cv * jax.nn.sigmoid(cv)).astype(BF16), w_ref[...].astype(BF16))

    return pl.pallas_call(
        body, name="ada_fwd", grid=(nl,),
        in_specs=[pl.BlockSpec((NDEV, D), lambda i: (0, 0)), pl.BlockSpec((None, D, ns), lambda i: (i, 0, 0))],
        out_specs=pl.BlockSpec((None, NDEV, ns), lambda i: (i, 0, 0)),
        out_shape=jax.ShapeDtypeStruct((nl, NDEV, ns), F32),
        compiler_params=_params("parallel"),
    )(c_all, ada_w)


PACK_ROWS = 24


def _reduce_packed(gathered):
    def body(g_ref, tot_ref, loss_ref):
        tot = g_ref[0:PACK_ROWS, :]
        for dev in range(1, NDEV):
            tot = tot + g_ref[dev * PACK_ROWS:(dev + 1) * PACK_ROWS, :]
        tot_ref[...] = tot
        loss_ref[...] = jnp.zeros((8, 128), F32) + jnp.sum(tot[10:11, :])

    return pl.pallas_call(
        body, name="reduce_packed",
        out_shape=[jax.ShapeDtypeStruct((PACK_ROWS, D), F32), jax.ShapeDtypeStruct((8, 128), F32)],
    )(gathered)


def _ada_w_update(c_t, dms, w, m, v):
    nl, _, ns = w.shape
    tr = 256

    def body(ct_ref, dm_ref, w_ref, m_ref, v_ref, g_ref, d_ref, m2_ref, v2_ref):
        ct = ct_ref[...]
        sc = ct * jax.nn.sigmoid(ct)
        dm = dm_ref[...]
        g = sc[:, 0:1] * dm[0:1, :]
        for b in range(1, NDEV):
            g = g + sc[:, b:b + 1] * dm[b:b + 1, :]
        g_ref[...] = g
        d_ref[...], m2_ref[...], v2_ref[...] = _adamw_math(w_ref[...], g, m_ref[...], v_ref[...])

    blk = pl.BlockSpec((None, tr, ns), lambda i, j: (i, j, 0))
    return pl.pallas_call(
        body, name="ada_w_update", grid=(nl, D // tr),
        in_specs=[pl.BlockSpec((tr, NDEV), lambda i, j: (j, 0)),
                  pl.BlockSpec((None, NDEV, ns), lambda i, j: (i, 0, 0)), blk, blk, blk],
        out_specs=[blk] * 4,
        out_shape=[jax.ShapeDtypeStruct((nl, D, ns), F32)] * 4,
        compiler_params=_params("parallel", "parallel"),
    )(c_t, dms, w, m, v)


def _chip_major(a, parts):
    g, _, cdim = a.shape
    return jnp.transpose(a.reshape(g, NCHIP, parts, cdim), (1, 0, 2, 3)).reshape(NCHIP, g * parts, cdim)


def _group_major(a):
    return jnp.transpose(a.reshape(NCHIP, 4, PG // NCHIP, PG), (1, 0, 2, 3)).reshape(4, PG, PG)


def kernel(x, c, norm_g, ada_w, ada_b, even_w_in, pool_w, pool_scale, even_w_out, odd_w_in, conv_w, conv_b, odd_w_out, final_g, loss_target, m_norm_g, m_ada_w, m_ada_b, m_even_w_in, m_pool_w, m_pool_scale, m_even_w_out, m_odd_w_in, m_conv_w, m_conv_b, m_odd_w_out, m_final_g, v_norm_g, v_ada_w, v_ada_b, v_even_w_in, v_pool_w, v_pool_scale, v_even_w_out, v_odd_w_in, v_conv_w, v_conv_b, v_odd_w_out, v_final_g):
    ix, iy, ic = _place()
    chip = 2 * ix + iy
    batch = 2 * chip + ic
    sel = jnp.stack([chip, ic]).astype(jnp.int32)
    ns_ada = ada_w.shape[2]
    ns_conv = conv_b.shape[1]

    conv_rows = jnp.pad(jnp.concatenate([conv_w[0], conv_b], axis=0), ((0, 3), (0, D - ns_conv)))
    first = _allgather8(jnp.concatenate([c, conv_rows], axis=0), "gather_c_conv").reshape(NCHIP, 2, 8, D)
    c_all = first[:, :, 0].reshape(NDEV, D)
    cw_full = jnp.transpose(first[:, 0, 1:5, 0:ns_conv], (1, 0, 2)).reshape(4, DI)
    cw8 = jnp.concatenate([cw_full[0:3], jnp.zeros((5, DI), F32)], axis=0)
    cb_full = cw_full[3:4]

    m_cols = _allgather8(_ada_fwd(c_all, ada_w).reshape(2 * NDEV, ns_ada), "gather_ada")
    m_cols = m_cols.reshape(NCHIP, 2, 2, NDEV, ns_ada)[:, 0]
    m_mine = lax.dynamic_index_in_dim(m_cols, batch, axis=2, keepdims=False)
    m_mine = jnp.transpose(m_mine, (1, 0, 2)).reshape(2, 3 * D) + ada_b
    zrow = jnp.zeros((3, D), F32)

    def vec_rows(i):
        sh, sc, gt = m_mine[i, 0:D], m_mine[i, D:2 * D], m_mine[i, 2 * D:3 * D]
        return jnp.concatenate([jnp.stack([norm_g[i], sc, sh, gt, final_g]), zrow], axis=0)

    win0, = _run_comm(_gather_comm([even_w_in[0].astype(BF16)]), "gather_win0")
    shards = (pool_w[0].astype(BF16).reshape(PG, PG), even_w_out[0].astype(BF16),
              odd_w_in[0].astype(BF16), odd_w_out[0].astype(BF16))
    dx0, grads, sums = _local_step(
        x[0], loss_target[0], vec_rows(0), vec_rows(1), win0, shards, pool_scale, cw8, cb_full, sel)
    r_win0, r_pw, r_wout0, r_win1, r_wout1 = grads

    packed = jnp.concatenate([
        sums["dm0"], sums["dm1"], sums["norm_g"], sums["pool_scale"], sums["final_g"], sums["loss"],
        sums["conv_w"].reshape(6, D), sums["conv_b"].reshape(2, D), jnp.zeros((PACK_ROWS - 19, D), F32)], axis=0)
    gathered = _allgather8(packed, "gather_sums")
    tot, loss8 = _reduce_packed(gathered)
    loss = loss8[0, 0]
    g_norm_g, g_pool_scale, g_final_g = tot[6:8], tot[8:9], tot[9]
    g_ada_b = tot[0:6].reshape(2, 3 * D)
    g_conv_w = lax.dynamic_slice_in_dim(tot[11:17].reshape(3, DI), chip * (DI // NCHIP), DI // NCHIP, axis=1)
    g_conv_b = lax.dynamic_slice_in_dim(tot[17:19].reshape(1, DI), chip * (DI // NCHIP), DI // NCHIP, axis=1)
    dm_all = gathered.reshape(NDEV, PACK_ROWS, D)[:, 0:6].reshape(NDEV, 2, 3 * D)
    dm_cols = jnp.transpose(lax.dynamic_slice_in_dim(dm_all, chip * ns_ada, ns_ada, axis=2), (1, 0, 2))
    g_ada_w, d_ada_w, nm_ada_w, nv_ada_w = _ada_w_update(jnp.transpose(c_all), dm_cols, ada_w, m_ada_w, v_ada_w)

    def upd(w, g, m, v, name):
        shape = w.shape
        w2, m2, v2 = (a.reshape(g.shape) for a in (w, m, v))
        d, nm, nv = _adamw(w2, g, m2, v2, name)
        return g.reshape(shape), d.reshape(shape), nm.reshape(shape), nv.reshape(shape)

    o_win0 = upd(even_w_in, r_win0, m_even_w_in, v_even_w_in, "adamw_win0")
    o_pw = upd(pool_w, r_pw, m_pool_w, v_pool_w, "adamw_pool")
    o_wout0 = upd(even_w_out, r_wout0, m_even_w_out, v_even_w_out, "adamw_wout0")
    o_win1 = upd(odd_w_in, r_win1, m_odd_w_in, v_odd_w_in, "adamw_win1")
    o_wout1 = upd(odd_w_out, r_wout1, m_odd_w_out, v_odd_w_out, "adamw_wout1")

    def pack_small(ng, ab, ps, fg, cwv, cbv):
        conv = jnp.concatenate([cwv.reshape(3, -1), cbv.reshape(1, -1)], axis=0).reshape(2, D)
        return jnp.concatenate([ng, ab.reshape(6, D), ps, fg.reshape(1, D), conv, jnp.zeros((4, D), F32)], axis=0)

    sw = pack_small(norm_g, ada_b, pool_scale, final_g, conv_w, conv_b)
    sg = pack_small(g_norm_g, g_ada_b, g_pool_scale, g_final_g, g_conv_w, g_conv_b)
    sm = pack_small(m_norm_g, m_ada_b, m_pool_scale, m_final_g, m_conv_w, m_conv_b)
    sv = pack_small(v_norm_g, v_ada_b, v_pool_scale, v_final_g, v_conv_w, v_conv_b) + jnp.concatenate(
        [jnp.zeros((12, D), F32), jnp.ones((4, D), F32)], axis=0)
    small = _adamw(sw, sg, sm, sv, "adamw_small")

    def unpack_small(a):
        conv = a[10:12].reshape(4, -1)
        return dict(norm_g=a[0:2], ada_b=a[2:8].reshape(2, 3 * D), pool_scale=a[8:9], final_g=a[9],
                    conv_w=conv[0:3].reshape(conv_w.shape), conv_b=conv[3:4].reshape(conv_b.shape))

    s_grad = dict(norm_g=g_norm_g, ada_b=g_ada_b, pool_scale=g_pool_scale, final_g=g_final_g,
                  conv_w=g_conv_w.reshape(conv_w.shape), conv_b=g_conv_b.reshape(conv_b.shape))
    s_out = [s_grad] + [unpack_small(a) for a in small]

    outs = []
    for k in range(4):
        sm_k = s_out[k]
        outs.append([sm_k["norm_g"], (g_ada_w, d_ada_w, nm_ada_w, nv_ada_w)[k], sm_k["ada_b"], o_win0[k], o_pw[k],
                     sm_k["pool_scale"], o_wout0[k], o_win1[k], sm_k["conv_w"], sm_k["conv_b"], o_wout1[k],
                     sm_k["final_g"]])
    return (loss, dx0[None], *outs[0], *outs[1], *outs[2], *outs[3])
```

```python
import functools

import jax
import jax.numpy as jnp
from jax import lax
from jax.experimental import pallas as pl
from jax.experimental.pallas import tpu as pltpu

F32 = jnp.float32
BF16 = jnp.bfloat16
MESH = pl.DeviceIdType.MESH

D = 1024
DI = 2048
DP = 1024
NE = 6144
NO = 8192
WINDOWS = (2, 4, 8, 16)
PG = 256
HD = 64
NCHIP = 4
NDEV = 8
EPS = 1e-6
INV_SQRT_HD = 0.125

LR, B1, B2, EPS_ADAM, WD, STEP = 0.001, 0.9, 0.999, 1e-08, 0.01, 10

TM = 512
TME = 256
CT = 512
BQ = 512
BK = 256
assert BQ == 2 * BK
HALO = 16
DEAD_LOG_WEIGHT = -104.0
VMEM_LIMIT = 56 * 1024 * 1024


def _dot(a, b):
    return jnp.dot(a, b, preferred_element_type=F32)


def _dot_nt(a, b):
    return lax.dot_general(a, b, (((1,), (1,)), ((), ())), preferred_element_type=F32)


def _dot_tn(a, b):
    return lax.dot_general(a, b, (((0,), (0,)), ((), ())), preferred_element_type=F32)


def _params(*sem):
    return pltpu.CompilerParams(dimension_semantics=sem, vmem_limit_bytes=VMEM_LIMIT)


def _rowsum(v):
    return jnp.sum(v, axis=0, keepdims=True)


def _modulated_norm(xv, vec_ref):
    r = lax.rsqrt(jnp.mean(xv * xv, axis=-1, keepdims=True) + EPS)
    return (((xv * r) * vec_ref[0:1, :]) * (1.0 + vec_ref[1:2, :]) + vec_ref[2:3, :]).astype(BF16)


def _norm_mod(x, vecs, name):
    s = x.shape[0]
    tm = min(TM, s)

    def body(x_ref, vec_ref, h_ref):
        h_ref[...] = _modulated_norm(x_ref[...], vec_ref)

    return pl.pallas_call(
        body, name=name, grid=(s // tm,),
        in_specs=[pl.BlockSpec((tm, D), lambda i: (i, 0)), pl.BlockSpec((8, D), lambda i: (0, 0))],
        out_specs=pl.BlockSpec((tm, D), lambda i: (i, 0)),
        out_shape=jax.ShapeDtypeStruct((s, D), BF16),
        compiler_params=_params("parallel"),
    )(x, vecs)


def _inproj(h, w, name, comm=None):
    s = h.shape[0]
    ns = w.shape[2]
    tm = min(TM, s)
    ni = s // tm
    comm = comm or _NO_COMM
    nci, nco = len(comm.arrays), len(comm.out_shape)

    def body(*refs):
        h_ref, w_ref = refs[:2]
        proj_ref = refs[2 + nci]
        cargs = (refs[2:2 + nci], refs[3 + nci:3 + nci + nco], refs[3 + nci + nco:])
        j, i = pl.program_id(0), pl.program_id(1)
        if nci:
            pl.when((j == 0) & (i == 0))(lambda: comm.start(*cargs))
        proj_ref[...] = _dot(h_ref[...], w_ref[...]).astype(BF16)
        if nci:
            pl.when((j == NCHIP - 1) & (i == ni - 1))(lambda: comm.finish(*cargs))

    return pl.pallas_call(
        body, name=name, grid=(NCHIP, ni),
        in_specs=[pl.BlockSpec((tm, D), lambda j, i: (i, 0)),
                  pl.BlockSpec((None, D, ns), lambda j, i: (j, 0, 0))] + [ANY] * nci,
        out_specs=[pl.BlockSpec((tm, ns), lambda j, i: (i, j))] + [ANY] * nco,
        out_shape=[jax.ShapeDtypeStruct((s, NCHIP * ns), BF16)] + comm.out_shape,
        scratch_shapes=comm.sems,
        compiler_params=_params("arbitrary", "arbitrary"),
    )(h, w, *comm.arrays)


def _pool_fwd(proj0, pw, pscale):
    s = proj0.shape[0]
    tm = min(TM, s)
    hb = tm // 16

    def body(u_ref, halo_ref, w_ref, sc_ref, p_ref, y_ref, ext_ref):
        i = pl.program_id(0)
        ext_ref[16:, :] = u_ref[...].astype(F32)
        ext_ref[0:16, :] = jnp.where(i > 0, halo_ref[...].astype(F32), 0.0)
        t = i * tm + lax.broadcasted_iota(jnp.int32, (tm, 1), 0)
        for g, wdw in enumerate(WINDOWS):
            cs = slice(g * PG, (g + 1) * PG)
            u = ext_ref[16:16 + tm, cs]
            acc = u
            for j in range(1, wdw):
                acc = acc + ext_ref[16 - j:16 - j + tm, cs]
            inv = 1.0 / jnp.minimum(t + 1, wdw).astype(F32)
            pb = (acc * inv - u).astype(BF16)
            p_ref[:, cs] = pb
            y_ref[:, cs] = _dot(pb, w_ref[g]) * sc_ref[:, cs]

    return pl.pallas_call(
        body, name="pool_fwd", grid=(s // tm,),
        in_specs=[pl.BlockSpec((tm, DP), lambda i: (i, 0)),
                  pl.BlockSpec((16, DP), lambda i: (jnp.maximum(i * hb - 1, 0), 0)),
                  pl.BlockSpec((4, PG, PG), lambda i: (0, 0, 0)),
                  pl.BlockSpec((1, DP), lambda i: (0, 0))],
        out_specs=[pl.BlockSpec((tm, DP), lambda i: (i, 0)),
                   pl.BlockSpec((tm, DP), lambda i: (i, 0))],
        out_shape=[jax.ShapeDtypeStruct((s, DP), BF16), jax.ShapeDtypeStruct((s, DP), F32)],
        scratch_shapes=[pltpu.VMEM((tm + 16, DP), F32)],
        compiler_params=_params("parallel"),
    )(proj0, proj0, pw, pscale)


def _sb_logits(nz, mask):
    neg_abs = lax.bitcast_convert_type(lax.bitcast_convert_type(nz, jnp.uint32) | jnp.uint32(0x80000000), F32)
    t = jnp.log(1.0 + jnp.exp(neg_abs))
    lf = jnp.minimum(nz, 0.0) - t
    lam = lf - nz
    if mask is not None:
        lf = jnp.where(mask, lf, 0.0)
    return lf, lam


def _sweep_left(steps, state, step):
    def live(carry):
        j, st = carry
        heaviest = functools.reduce(jnp.maximum, [jnp.max(head[1]) for head in st])
        return (j < steps) & (heaviest > DEAD_LOG_WEIGHT)

    return lax.while_loop(live, lambda carry: (carry[0] + 1, step(carry[0], carry[1])), (0, state))[1]


def _split_dot(v, tri):
    hi = v.astype(BF16)
    lo = (v - hi.astype(F32)).astype(BF16)
    return _dot(hi, tri) + _dot(lo, tri)


def _tri_masks():
    row = lax.broadcasted_iota(jnp.int32, (BK, BK), 0)
    col = lax.broadcasted_iota(jnp.int32, (BK, BK), 1)
    return (row > col).astype(BF16), (row >= col).astype(BF16)


def _causal_mask():
    row = lax.broadcasted_iota(jnp.int32, (BK, BK), 0)
    col = lax.broadcasted_iota(jnp.int32, (BK, BK), 1)
    return col < row


def _attn_fwd(proj0, comm=None):
    s = proj0.shape[0]
    nq = s // BQ
    kpq = BQ // BK
    nsteps = DP // 128
    comm = comm or _NO_COMM
    nci, nco = len(comm.arrays), len(comm.out_shape)

    def body(*refs):
        q_ref, k_ref, v_ref = refs[:3]
        o_ref = refs[3 + nci]
        qn_ref, k8_ref, vb_ref = refs[4 + nci + nco:7 + nci + nco]
        cargs = (refs[3:3 + nci], refs[4 + nci:4 + nci + nco], refs[7 + nci + nco:])
        if nci:
            pl.when(pl.program_id(0) == 0)(lambda: comm.start(*cargs))
        qn_ref[...] = (-q_ref[...]).astype(BF16)
        k8_ref[...] = (k_ref[...] * INV_SQRT_HD).astype(BF16)
        vb_ref[...] = v_ref[...].astype(BF16)
        after, _ = _tri_masks()
        causal = _causal_mask()
        heads = [slice(HD * h, HD * (h + 1)) for h in range(2)]
        hs = range(len(heads))
        lo, hi = (0, BK), (BK, BK)

        def qstep(qi, carry):
            q0 = pl.multiple_of(qi * BQ, BQ)
            qn = [qn_ref[pl.ds(q0, BQ), ls] for ls in heads]

            def sweep(lanes, state):
                nz = [_dot_nt(qn[h][r0:r0 + n], k8_ref[pl.ds(k0, BK), heads[h]]) for h, (r0, n), k0, _ in lanes]
                ll = [_sb_logits(nz[i], lane[3]) for i, lane in enumerate(lanes)]
                aft = [_dot(ll[i][0].astype(BF16), after) for i in range(len(lanes))]
                state = dict(state)
                for i, (h, rows, k0, mask) in enumerate(lanes):
                    o_acc, c = state[h, rows]
                    a = jnp.exp(ll[i][1] + aft[i] + c)
                    if mask is not None:
                        a = jnp.where(mask, a, 0.0)
                    o_acc = o_acc + _dot(a.astype(BF16), vb_ref[pl.ds(k0, BK), heads[h]])
                    state[h, rows] = (o_acc, c + aft[i][:, 0:1] + ll[i][0][:, 0:1])
                return state

            k_lo, k_hi = q0, pl.multiple_of(q0 + BK, BK)
            st = {(h, r): (jnp.zeros((BK, HD), F32), jnp.zeros((BK, 1), F32)) for h in hs for r in (lo, hi)}
            st = sweep([(h, hi, k_hi, causal) for h in hs] + [(h, lo, k_lo, causal) for h in hs]
                       + [(h, hi, k_lo, None) for h in hs], st)
            def left_of(rows):
                def step(j, part):
                    k0 = pl.multiple_of(q0 - (j + 1) * BK, BK)
                    got = sweep([(h, rows, k0, None) for h in hs], {(h, rows): part[h] for h in hs})
                    return tuple(got[h, rows] for h in hs)
                return step

            done = {r: _sweep_left(qi * (BQ // BK), tuple(st[h, r] for h in hs), left_of(r)) for r in (lo, hi)}
            for h, ls in enumerate(heads):
                o_ref[pl.ds(q0, BQ), ls] = jnp.concatenate([done[lo][h][0], done[hi][h][0]], axis=0)
            return carry

        lax.fori_loop(0, nq, qstep, 0)
        if nci:
            pl.when(pl.program_id(0) == nsteps - 1)(lambda: comm.finish(*cargs))

    return pl.pallas_call(
        body, name="attn_fwd", grid=(nsteps,),
        in_specs=[pl.BlockSpec((s, 128), lambda h: (0, 8 + h)),
                  pl.BlockSpec((s, 128), lambda h: (0, 16 + h)),
                  pl.BlockSpec((s, 128), lambda h: (0, 24 + h))] + [ANY] * nci,
        out_specs=[pl.BlockSpec((s, 128), lambda h: (0, h))] + [ANY] * nco,
        out_shape=[jax.ShapeDtypeStruct((s, DP), F32)] + comm.out_shape,
        scratch_shapes=[pltpu.VMEM((s, 128), BF16)] * 3 + comm.sems,
        compiler_params=_params("arbitrary"),
    )(proj0, proj0, proj0, *comm.arrays)


def _even_out(ypool, ysb, proj0, wout, x, vecs, vecs_next):
    s = x.shape[0]
    tm = min(TME, s)

    def body(yp_ref, ys_ref, gate_ref, w_ref, x_ref, vec_ref, vecn_ref, x1_ref, out_ref, yg_ref, hn_ref):
        gt = gate_ref[...].astype(F32)
        sl = gt * jax.nn.sigmoid(gt)
        yg_ref[:, :DP] = (yp_ref[...] * sl[:, :DP]).astype(BF16)
        yg_ref[:, DP:] = (ys_ref[...] * sl[:, DP:]).astype(BF16)
        out = _dot(yg_ref[...], w_ref[...])
        out_ref[...] = out
        x1 = x_ref[...] + (1.0 + vec_ref[3:4, :]) * out
        x1_ref[...] = x1
        hn_ref[...] = _modulated_norm(x1, vecn_ref)

    row = lambda i: (i, 0)
    const = lambda i: (0, 0)
    return pl.pallas_call(
        body, name="even_out", grid=(s // tm,),
        in_specs=[pl.BlockSpec((tm, DP), row), pl.BlockSpec((tm, DP), row),
                  pl.BlockSpec((tm, DI), lambda i: (i, 2)),
                  pl.BlockSpec((DI, D), const),
                  pl.BlockSpec((tm, D), row), pl.BlockSpec((8, D), const), pl.BlockSpec((8, D), const)],
        out_specs=[pl.BlockSpec((tm, D), row), pl.BlockSpec((tm, D), row), pl.BlockSpec((tm, DI), row),
                   pl.BlockSpec((tm, D), row)],
        out_shape=[jax.ShapeDtypeStruct((s, D), F32), jax.ShapeDtypeStruct((s, D), F32),
                   jax.ShapeDtypeStruct((s, DI), BF16), jax.ShapeDtypeStruct((s, D), BF16)],
        compiler_params=_params("parallel"),
    )(ypool, ysb, proj0, wout, x, vecs, vecs_next)


def _odd_out(proj1, wout, x1, vecs, cw, cb, target):
    s = x1.shape[0]
    tm = min(TME, s)
    hb = tm // HALO

    def body(gb_ref, gc_ref, u_ref, gt_ref, hgc_ref, hu_ref, w_ref, x1_ref, vec_ref, cw_ref, cb_ref, tg_ref,
             dx2_ref, y1_ref, acc_ref, ext_ref):
        i = pl.program_id(0)

        @pl.when(i == 0)
        def _():
            acc_ref[...] = jnp.zeros_like(acc_ref)

        ext_ref[HALO:, :] = gc_ref[...].astype(F32) * u_ref[...].astype(F32)
        ext_ref[0:HALO, :] = jnp.where(i > 0, hgc_ref[...].astype(F32) * hu_ref[...].astype(F32), 0.0)
        for c in range(DI // CT):
            cs = slice(c * CT, (c + 1) * CT)
            conv = (cb_ref[0:1, cs] + cw_ref[0:1, cs] * ext_ref[HALO - 2:HALO - 2 + tm, cs]
                    + cw_ref[1:2, cs] * ext_ref[HALO - 1:HALO - 1 + tm, cs]
                    + cw_ref[2:3, cs] * ext_ref[HALO:HALO + tm, cs])
            gt = gt_ref[:, cs].astype(F32)
            y1_ref[:, cs] = (gb_ref[:, cs].astype(F32) * conv * (gt * jax.nn.sigmoid(gt))).astype(BF16)
        out = _dot(y1_ref[...], w_ref[...])
        x2 = x1_ref[...] + (1.0 + vec_ref[3:4, :]) * out
        r = lax.rsqrt(jnp.mean(x2 * x2, axis=-1, keepdims=True) + EPS)
        nrm = x2 * r
        fg = vec_ref[4:5, :]
        err = nrm * fg - tg_ref[...]
        acc_ref[1:2, :] += _rowsum(err * err) * (0.5 / D)
        dyf = err * (1.0 / D)
        acc_ref[0:1, :] += _rowsum(dyf * nrm)
        dn = dyf * fg
        dx2 = r * (dn - nrm * jnp.mean(dn * nrm, axis=-1, keepdims=True))
        dx2_ref[...] = dx2
        acc_ref[2:3, :] += _rowsum(dx2 * out)

    row = lambda i: (i, 0)
    halo = lambda col: (lambda i: (jnp.maximum(i * hb - 1, 0), col))
    const = lambda i: (0, 0)
    return pl.pallas_call(
        body, name="odd_out", grid=(s // tm,),
        in_specs=[pl.BlockSpec((tm, DI), lambda i: (i, 0)), pl.BlockSpec((tm, DI), lambda i: (i, 1)),
                  pl.BlockSpec((tm, DI), lambda i: (i, 2)), pl.BlockSpec((tm, DI), lambda i: (i, 3)),
                  pl.BlockSpec((HALO, DI), halo(1)), pl.BlockSpec((HALO, DI), halo(2)),
                  pl.BlockSpec((DI, D), const), pl.BlockSpec((tm, D), row), pl.BlockSpec((8, D), const),
                  pl.BlockSpec((8, DI), const), pl.BlockSpec((1, DI), const), pl.BlockSpec((tm, D), row)],
        out_specs=[pl.BlockSpec((tm, D), row), pl.BlockSpec((tm, DI), row), pl.BlockSpec((8, D), const)],
        out_shape=[jax.ShapeDtypeStruct((s, D), F32), jax.ShapeDtypeStruct((s, DI), BF16),
                   jax.ShapeDtypeStruct((8, D), F32)],
        scratch_shapes=[pltpu.VMEM((tm + HALO, DI), F32)],
        compiler_params=_params("arbitrary"),
    )(proj1, proj1, proj1, proj1, proj1, proj1, wout, x1, vecs, cw, cb, target)


def _odd_bwd(dx2, proj1, wout, vecs, cw, cb):
    s = dx2.shape[0]
    tm = min(TME, s)
    nb = s // tm
    hb = tm // HALO

    def body(dx2_ref, gb_ref, gc_ref, u_ref, gt_ref, hgc_ref, hu_ref, w_ref, vec_ref, cw_ref, cb_ref,
             dout_ref, dproj_ref, accv_ref, uext_ref, dext_ref, dy_ref):
        i = pl.program_id(0)
        blk = nb - 1 - i

        @pl.when(i == 0)
        def _():
            accv_ref[...] = jnp.zeros_like(accv_ref)
            dext_ref[tm:tm + 8, :] = jnp.zeros((8, DI), F32)

        dout = (dx2_ref[...] * (1.0 + vec_ref[3:4, :])).astype(BF16)
        dout_ref[...] = dout
        dy_ref[...] = _dot_nt(dout, w_ref[...])
        uext_ref[HALO:, :] = gc_ref[...].astype(F32) * u_ref[...].astype(F32)
        uext_ref[0:HALO, :] = jnp.where(blk > 0, hgc_ref[...].astype(F32) * hu_ref[...].astype(F32), 0.0)
        for c in range(DI // CT):
            cs = slice(c * CT, (c + 1) * CT)
            u0 = uext_ref[HALO - 2:HALO - 2 + tm, cs]
            u1 = uext_ref[HALO - 1:HALO - 1 + tm, cs]
            u2 = uext_ref[HALO:HALO + tm, cs]
            w0, w1, w2 = cw_ref[0:1, cs], cw_ref[1:2, cs], cw_ref[2:3, cs]
            conv = cb_ref[0:1, cs] + w0 * u0 + w1 * u1 + w2 * u2
            gt = gt_ref[:, cs].astype(F32)
            sg = jax.nn.sigmoid(gt)
            gb = gb_ref[:, cs].astype(F32)
            dy = dy_ref[:, cs]
            t1 = dy * (gt * sg)
            dproj_ref[:, cs] = (t1 * conv).astype(BF16)
            dconv = t1 * gb
            dproj_ref[:, 3 * DI + c * CT:3 * DI + (c + 1) * CT] = (
                dy * gb * conv * (sg * (1.0 + gt * (1.0 - sg)))).astype(BF16)
            accv_ref[0:1, cs] += _rowsum(dconv * u0)
            accv_ref[1:2, cs] += _rowsum(dconv * u1)
            accv_ref[2:3, cs] += _rowsum(dconv * u2)
            accv_ref[3:4, cs] += _rowsum(dconv)
            dext_ref[0:tm, cs] = dconv
            duu = w2 * dconv + w1 * dext_ref[1:tm + 1, cs] + w0 * dext_ref[2:tm + 2, cs]
            dproj_ref[:, DI + c * CT:DI + (c + 1) * CT] = (duu * u_ref[:, cs].astype(F32)).astype(BF16)
            dproj_ref[:, 2 * DI + c * CT:2 * DI + (c + 1) * CT] = (duu * gc_ref[:, cs].astype(F32)).astype(BF16)
        dext_ref[tm:tm + 8, :] = dext_ref[0:8, :]

    rrow = lambda i: (nb - 1 - i, 0)
    rcol = lambda col: (lambda i: (nb - 1 - i, col))
    halo = lambda col: (lambda i: (jnp.maximum((nb - 1 - i) * hb - 1, 0), col))
    const = lambda i: (0, 0)
    return pl.pallas_call(
        body, name="odd_bwd", grid=(nb,),
        in_specs=[pl.BlockSpec((tm, D), rrow),
                  pl.BlockSpec((tm, DI), rcol(0)), pl.BlockSpec((tm, DI), rcol(1)),
                  pl.BlockSpec((tm, DI), rcol(2)), pl.BlockSpec((tm, DI), rcol(3)),
                  pl.BlockSpec((HALO, DI), halo(1)), pl.BlockSpec((HALO, DI), halo(2)),
                  pl.BlockSpec((DI, D), const), pl.BlockSpec((8, D), const),
                  pl.BlockSpec((8, DI), const), pl.BlockSpec((1, DI), const)],
        out_specs=[pl.BlockSpec((tm, D), rrow), pl.BlockSpec((tm, NO), rrow), pl.BlockSpec((8, DI), const)],
        out_shape=[jax.ShapeDtypeStruct((s, D), BF16), jax.ShapeDtypeStruct((s, NO), BF16),
                   jax.ShapeDtypeStruct((8, DI), F32)],
        scratch_shapes=[pltpu.VMEM((tm + HALO, DI), F32), pltpu.VMEM((tm + 8, DI), F32), pltpu.VMEM((tm, DI), F32)],
        compiler_params=_params("arbitrary"),
    )(dx2, proj1, proj1, proj1, proj1, proj1, proj1, wout, vecs, cw, cb)


def _grad_w_cols(a, b, name):
    s, m = a.shape
    ns = b.shape[1] // NCHIP
    ts = min(TM, s)

    def body(a_ref, b_ref, o_ref):
        @pl.when(pl.program_id(1) == 0)
        def _():
            o_ref[...] = jnp.zeros_like(o_ref)

        o_ref[...] += _dot_tn(a_ref[...], b_ref[...])

    return pl.pallas_call(
        body, name=name, grid=(NCHIP, s // ts),
        in_specs=[pl.BlockSpec((ts, m), lambda j, k: (k, 0)),
                  pl.BlockSpec((ts, ns), lambda j, k: (k, j))],
        out_specs=pl.BlockSpec((None, m, ns), lambda j, k: (j, 0, 0)),
        out_shape=jax.ShapeDtypeStruct((NCHIP, m, ns), F32),
        compiler_params=_params("parallel", "arbitrary"),
    )(a, b)


def _grad_w_rows(a, b, name):
    s = a.shape[0]
    ms = a.shape[1] // NCHIP
    n = b.shape[1]
    ts = min(TM, s)

    def body(a_ref, b_ref, o_ref):
        @pl.when(pl.program_id(1) == 0)
        def _():
            o_ref[...] = jnp.zeros_like(o_ref)

        o_ref[...] += _dot_tn(a_ref[...], b_ref[...])

    return pl.pallas_call(
        body, name=name, grid=(NCHIP, s // ts),
        in_specs=[pl.BlockSpec((ts, ms), lambda i, k: (k, i)),
                  pl.BlockSpec((ts, n), lambda i, k: (k, 0))],
        out_specs=pl.BlockSpec((None, ms, n), lambda i, k: (i, 0, 0)),
        out_shape=jax.ShapeDtypeStruct((NCHIP, ms, n), F32),
        compiler_params=_params("parallel", "arbitrary"),
    )(a, b)


def _inproj_bwd(dproj, w, x, dx_in, vecs, name, comm=None):
    s = x.shape[0]
    ns = w.shape[2]
    tm = min(TME, s)
    ni = s // tm
    comm = comm or _NO_COMM
    nci, nco = len(comm.arrays), len(comm.out_shape)

    def body(*refs):
        dp_ref, w_hbm, x_ref, dxin_ref, vec_ref = refs[:5]
        dx_ref, acc_ref = refs[5 + nci:7 + nci]
        w_ref = refs[7 + nci + nco]
        cargs = (refs[5:5 + nci], refs[7 + nci:7 + nci + nco], refs[8 + nci + nco:])
        i = pl.program_id(0)

        @pl.when(i == 0)
        def _():
            acc_ref[...] = jnp.zeros_like(acc_ref)
            if nci:
                comm.start(*cargs)
            pltpu.sync_copy(w_hbm, w_ref)

        dh = _dot_nt(dp_ref[:, 0:ns], w_ref[0])
        for j in range(1, NCHIP):
            dh = dh + _dot_nt(dp_ref[:, j * ns:(j + 1) * ns], w_ref[j])
        xv = x_ref[...]
        r = lax.rsqrt(jnp.mean(xv * xv, axis=-1, keepdims=True) + EPS)
        nrm = xv * r
        g = vec_ref[0:1, :]
        sc1 = 1.0 + vec_ref[1:2, :]
        dhn = dh * nrm
        acc_ref[0:1, :] += _rowsum(dh)
        acc_ref[1:2, :] += _rowsum(dhn) * g
        acc_ref[2:3, :] += _rowsum(dhn) * sc1
        dn = dh * (g * sc1)
        dx_ref[...] = dxin_ref[...] + r * (dn - nrm * jnp.mean(dn * nrm, axis=-1, keepdims=True))

        if nci:
            pl.when(i == ni - 1)(lambda: comm.finish(*cargs))

    row = lambda i: (i, 0)
    const = lambda i: (0, 0)
    return pl.pallas_call(
        body, name=name, grid=(ni,),
        in_specs=[pl.BlockSpec((tm, NCHIP * ns), row), ANY,
                  pl.BlockSpec((tm, D), row), pl.BlockSpec((tm, D), row), pl.BlockSpec((8, D), const)] + [ANY] * nci,
        out_specs=[pl.BlockSpec((tm, D), row), pl.BlockSpec((8, D), const)] + [ANY] * nco,
        out_shape=[jax.ShapeDtypeStruct((s, D), F32), jax.ShapeDtypeStruct((8, D), F32)] + comm.out_shape,
        scratch_shapes=[pltpu.VMEM(w.shape, BF16)] + comm.sems,
        compiler_params=_params("arbitrary"),
    )(dproj, w, x, dx_in, vecs, *comm.arrays)


def _even_bwd(dx1, out0, ypool, ysb, proj0, wout, vecs):
    s = dx1.shape[0]
    tm = min(TME, s)

    def body(dx1_ref, out0_ref, yp_ref, ys_ref, gate_ref, w_ref, vec_ref,
             dout_ref, dyp_ref, dys_ref, dgt_ref, acc_ref):
        @pl.when(pl.program_id(0) == 0)
        def _():
            acc_ref[...] = jnp.zeros_like(acc_ref)

        dx1v = dx1_ref[...]
        acc_ref[0:1, :] += _rowsum(dx1v * out0_ref[...])
        dout = (dx1v * (1.0 + vec_ref[3:4, :])).astype(BF16)
        dout_ref[...] = dout
        dyg = _dot_nt(dout, w_ref[...])
        gt = gate_ref[...].astype(F32)
        sg = jax.nn.sigmoid(gt)
        sl = gt * sg
        dsl = sg * (1.0 + gt * (1.0 - sg))
        dyp_ref[...] = dyg[:, :DP] * sl[:, :DP]
        dys_ref[...] = dyg[:, DP:] * sl[:, DP:]
        dgt_ref[:, :DP] = (dyg[:, :DP] * yp_ref[...] * dsl[:, :DP]).astype(BF16)
        dgt_ref[:, DP:] = (dyg[:, DP:] * ys_ref[...] * dsl[:, DP:]).astype(BF16)

    row = lambda i: (i, 0)
    const = lambda i: (0, 0)
    return pl.pallas_call(
        body, name="even_bwd", grid=(s // tm,),
        in_specs=[pl.BlockSpec((tm, D), row), pl.BlockSpec((tm, D), row),
                  pl.BlockSpec((tm, DP), row), pl.BlockSpec((tm, DP), row),
                  pl.BlockSpec((tm, DI), lambda i: (i, 2)),
                  pl.BlockSpec((DI, D), const), pl.BlockSpec((8, D), const)],
        out_specs=[pl.BlockSpec((tm, D), row), pl.BlockSpec((tm, DP), row), pl.BlockSpec((tm, DP), row),
                   pl.BlockSpec((tm, DI), lambda i: (i, NE // DI - 1)), pl.BlockSpec((8, D), const)],
        out_shape=[jax.ShapeDtypeStruct((s, D), BF16), jax.ShapeDtypeStruct((s, DP), F32),
                   jax.ShapeDtypeStruct((s, DP), F32), jax.ShapeDtypeStruct((s, NE), BF16),
                   jax.ShapeDtypeStruct((8, D), F32)],
        compiler_params=_params("arbitrary"),
    )(dx1, out0, ypool, ysb, proj0, wout, vecs)


def _pool_bwd(dyp, p, pw, pscale, dproj):
    s = dyp.shape[0]
    tm = min(TM, s)
    nb = s // tm
    hb = tm // 16

    def body(dy_ref, dyh_ref, p_ref, w_ref, sc_ref, dproj_hbm, du_ref, dw_ref, acc_ref, ext_ref):
        i = pl.program_id(0)

        @pl.when(i == 0)
        def _():
            dw_ref[...] = jnp.zeros_like(dw_ref)
            acc_ref[...] = jnp.zeros_like(acc_ref)

        t = i * tm + lax.broadcasted_iota(jnp.int32, (tm + 16, 1), 0)
        for g, wdw in enumerate(WINDOWS):
            cs = slice(g * PG, (g + 1) * PG)
            sc = sc_ref[:, cs]
            dy = dy_ref[:, cs]
            dyh = jnp.where(i < nb - 1, dyh_ref[:, cs], 0.0)
            pb = p_ref[:, cs]
            wg = w_ref[g]
            acc_ref[0:1, cs] += _rowsum(dy * _dot(pb, wg))
            dypre = (dy * sc).astype(BF16)
            dw_ref[g] += _dot_tn(pb, dypre)
            dp = _dot_nt(dypre, wg)
            dph = _dot_nt((dyh * sc).astype(BF16), wg)
            inv = 1.0 / jnp.minimum(t + 1, wdw).astype(F32)
            ext_ref[0:tm, cs] = dp * inv[0:tm]
            ext_ref[tm:tm + 16, cs] = dph * inv[tm:tm + 16]
            acc = ext_ref[0:tm, cs]
            for j in range(1, wdw):
                acc = acc + ext_ref[j:j + tm, cs]
            du_ref[:, cs] = (acc - dp).astype(BF16)

    row = lambda i: (i, 0)
    return pl.pallas_call(
        body, name="pool_bwd", grid=(nb,),
        in_specs=[pl.BlockSpec((tm, DP), row),
                  pl.BlockSpec((16, DP), lambda i: (jnp.minimum((i + 1) * hb, s // 16 - 1), 0)),
                  pl.BlockSpec((tm, DP), row),
                  pl.BlockSpec((4, PG, PG), lambda i: (0, 0, 0)),
                  pl.BlockSpec((1, DP), lambda i: (0, 0)), ANY],
        out_specs=[pl.BlockSpec((tm, DP), row), pl.BlockSpec((4, PG, PG), lambda i: (0, 0, 0)),
                   pl.BlockSpec((8, DP), lambda i: (0, 0))],
        out_shape=[jax.ShapeDtypeStruct((s, NE), BF16), jax.ShapeDtypeStruct((4, PG, PG), F32),
                   jax.ShapeDtypeStruct((8, DP), F32)],
        input_output_aliases={5: 0},
        scratch_shapes=[pltpu.VMEM((tm + 16, DP), F32)],
        compiler_params=_params("arbitrary"),
    )(dyp, dyp, p, pw, pscale, dproj)


def _attn_bwd(proj0, ysb, dys, dproj, comm=None):
    s = proj0.shape[0]
    nq = s // BQ
    kpq = BQ // BK
    nsteps = DP // 128
    comm = comm or _NO_COMM
    nci, nco = len(comm.arrays), len(comm.out_shape)

    def body(*refs):
        q_ref, k_ref, v_ref, o_ref, do_ref = refs[:5]
        dproj_hbm = refs[6 + nci]
        scratch = refs[7 + nci + nco:]
        qn_ref, k8_ref, vb_ref, dob_ref, dka_ref, dva_ref, dq_ref, dk_ref, dv_ref, part_sems = scratch[:10]
        cargs = (refs[6:6 + nci], refs[7 + nci:7 + nci + nco], scratch[10:])
        step = pl.program_id(0)
        if nci:
            pl.when(step == 0)(lambda: comm.start(*cargs))

        qn_ref[...] = (-q_ref[...]).astype(BF16)
        k8_ref[...] = (k_ref[...] * INV_SQRT_HD).astype(BF16)
        vb_ref[...] = v_ref[...].astype(BF16)
        dob_ref[...] = do_ref[...].astype(BF16)
        dka_ref[...] = jnp.zeros_like(dka_ref)
        dva_ref[...] = jnp.zeros_like(dva_ref)
        after, from_on = _tri_masks()
        causal = _causal_mask()
        heads = [slice(HD * h, HD * (h + 1)) for h in range(2)]
        hs = range(len(heads))
        lo, hi = (0, BK), (BK, BK)

        def qstep(qi, carry):
            q0 = pl.multiple_of(qi * BQ, BQ)
            qn = [qn_ref[pl.ds(q0, BQ), ls] for ls in heads]
            do = [dob_ref[pl.ds(q0, BQ), ls] for ls in heads]
            total = [jnp.sum(do[h].astype(F32) * o_ref[pl.ds(q0, BQ), ls], axis=1, keepdims=True)
                     for h, ls in enumerate(heads)]

            def sweep(lanes, state):
                rows_of = lambda i: slice(lanes[i][1][0], lanes[i][1][0] + lanes[i][1][1])
                k8 = [k8_ref[pl.ds(k0, BK), heads[h]] for h, _, k0, _ in lanes]
                nz = [_dot_nt(qn[lane[0]][rows_of(i)], k8[i]) for i, lane in enumerate(lanes)]
                da = [_dot_nt(do[h][rows_of(i)], vb_ref[pl.ds(k0, BK), heads[h]]) for i, (h, _, k0, _) in enumerate(lanes)]
                ll = [_sb_logits(nz[i], lane[3]) for i, lane in enumerate(lanes)]
                aft = [_dot(ll[i][0].astype(BF16), after) for i in range(len(lanes))]
                state = dict(state)
                for i, (h, rows, k0, mask) in enumerate(lanes):
                    dq_acc, c, cg = state[h, rows]
                    a = jnp.exp(ll[i][1] + aft[i] + c)
                    if mask is not None:
                        a = jnp.where(mask, a, 0.0)
                    ab = a.astype(BF16)
                    g = da[i] * ab.astype(F32)
                    suf = _split_dot(g, from_on)
                    dz = g - jnp.exp(ll[i][1]) * (g + ((total[h][rows_of(i)] - cg) - suf))
                    if mask is not None:
                        dz = jnp.where(mask, dz, 0.0)
                    dzb = dz.astype(BF16)
                    dka_ref[pl.ds(k0, BK), heads[h]] += _dot_tn(dzb, qn[h][rows_of(i)])
                    dva_ref[pl.ds(k0, BK), heads[h]] += _dot_tn(ab, do[h][rows_of(i)])
                    state[h, rows] = (dq_acc + _dot(dzb, k8[i]), c + aft[i][:, 0:1] + ll[i][0][:, 0:1],
                                      cg + suf[:, 0:1])
                return state

            k_lo, k_hi = q0, pl.multiple_of(q0 + BK, BK)
            zero = jnp.zeros((BK, 1), F32)
            st = {(h, r): (jnp.zeros((BK, HD), F32), zero, zero) for h in hs for r in (lo, hi)}
            st = sweep([(h, hi, k_hi, causal) for h in hs] + [(h, lo, k_lo, causal) for h in hs]
                       + [(h, hi, k_lo, None) for h in hs], st)
            def left_of(rows):
                def step(j, part):
                    k0 = pl.multiple_of(q0 - (j + 1) * BK, BK)
                    got = sweep([(h, rows, k0, None) for h in hs], {(h, rows): part[h] for h in hs})
                    return tuple(got[h, rows] for h in hs)
                return step

            done = {r: _sweep_left(qi * (BQ // BK), tuple(st[h, r] for h in hs), left_of(r)) for r in (lo, hi)}
            for h, ls in enumerate(heads):
                dq_ref[pl.ds(q0, BQ), ls] = jnp.concatenate([done[lo][h][0], done[hi][h][0]], axis=0).astype(BF16)
            return carry

        lax.fori_loop(0, nq, qstep, 0)
        dk_ref[...] = (dka_ref[...] * (-INV_SQRT_HD)).astype(BF16)
        dv_ref[...] = dva_ref[...].astype(BF16)
        lanes0 = pl.multiple_of(step * 128, 128)
        parts = [pltpu.make_async_copy(src, dproj_hbm.at[:, pl.ds((1 + k) * DP + lanes0, 128)], part_sems.at[k])
                 for k, src in enumerate((dq_ref, dk_ref, dv_ref))]
        for cp in parts:
            cp.start()
        for cp in parts:
            cp.wait()
        if nci:
            pl.when(step == nsteps - 1)(lambda: comm.finish(*cargs))

    col = lambda h: (0, h)
    return pl.pallas_call(
        body, name="attn_bwd", grid=(nsteps,),
        in_specs=[pl.BlockSpec((s, 128), lambda h: (0, 8 + h)),
                  pl.BlockSpec((s, 128), lambda h: (0, 16 + h)),
                  pl.BlockSpec((s, 128), lambda h: (0, 24 + h)),
                  pl.BlockSpec((s, 128), col), pl.BlockSpec((s, 128), col), ANY] + [ANY] * nci,
        out_specs=[ANY] * (1 + nco),
        out_shape=[jax.ShapeDtypeStruct((s, NE), BF16)] + comm.out_shape,
        input_output_aliases={5: 0},
        scratch_shapes=([pltpu.VMEM((s, 128), BF16)] * 4 + [pltpu.VMEM((s, 128), F32)] * 2
                        + [pltpu.VMEM((s, 128), BF16)] * 3 + [pltpu.SemaphoreType.DMA((3,))] + comm.sems),
        compiler_params=_params("arbitrary"),
    )(proj0, proj0, proj0, ysb, dys, dproj, *comm.arrays)


def _adamw_math(w, g, m, v):
    m2 = B1 * m + (1.0 - B1) * g
    v2 = B2 * v + (1.0 - B2) * (g * g)
    m_hat = m2 / (1.0 - B1 ** STEP)
    v_hat = v2 / (1.0 - B2 ** STEP)
    return -LR * (m_hat / (jnp.sqrt(v_hat) + EPS_ADAM) + WD * w), m2, v2


def _adamw(w, g, m, v, name):
    r, c = w.shape
    tr = r
    while tr * c * 4 > (1 << 20) and tr % 16 == 0:
        tr //= 2

    def body(w_ref, g_ref, m_ref, v_ref, d_ref, m2_ref, v2_ref):
        d_ref[...], m2_ref[...], v2_ref[...] = _adamw_math(w_ref[...], g_ref[...], m_ref[...], v_ref[...])

    spec = pl.BlockSpec((tr, c), lambda i: (i, 0))
    return pl.pallas_call(
        body, name=name, grid=(r // tr,),
        in_specs=[spec] * 4, out_specs=[spec] * 3,
        out_shape=[jax.ShapeDtypeStruct((r, c), F32)] * 3,
        compiler_params=_params("parallel"),
    )(w, g, m, v)


def _local_step(x, target, vecs0, vecs1, win0, rest, pscale, cw8, cb, sel=None):
    dist = sel is not None
    h0 = _norm_mod(x, vecs0, "norm0")
    proj0, *got = _inproj(h0, win0, "inproj0", _gather_comm(list(rest[0:2])) if dist else None)
    pw, wout0 = (_group_major(got[0]), got[1].reshape(DI, D)) if dist else rest[0:2]
    p, ypool = _pool_fwd(proj0, pw, pscale)
    ysb, *got = _attn_fwd(proj0, _gather_comm(list(rest[2:4])) if dist else None)
    win1, wout1 = (got[0], got[1].reshape(DI, D)) if dist else rest[2:4]
    x1, out0, yg, h1 = _even_out(ypool, ysb, proj0, wout0, x, vecs0, vecs1)
    proj1, = _inproj(h1, win1, "inproj1")
    dx2, y1, acc_f = _odd_out(proj1, wout1, x1, vecs1, cw8, cb, target)

    def chip_partials(grads, names):
        from_sibling = _send_halves(grads, "rs_send_halves_" + names[0])
        part = [_add_halves(g, t, sel[1:2], "rs_add_halves_" + nm) for g, t, nm in zip(grads, from_sibling, names)]
        return [p32 for p32, _ in part], _exchange_comm([p16 for _, p16 in part])

    dout1, dproj1, acc_cv = _odd_bwd(dx2, proj1, wout1, vecs1, cw8, cb)
    g_wout1 = _grad_w_rows(y1, dout1, "grad_wout1")
    g_win1 = _grad_w_cols(h1, dproj1, "grad_win1")
    dx1, acc_n1 = _inproj_bwd(dproj1, win1, x1, dx2, vecs1, "inproj1_bwd")

    dout0, dyp, dys, dproj0, acc_g0 = _even_bwd(dx1, out0, ypool, ysb, proj0, wout0, vecs0)
    g_wout0 = _grad_w_rows(yg, dout0, "grad_wout0")
    dproj0, g_pw, acc_ps = _pool_bwd(dyp, p, pw, pscale, dproj0)
    early = [_chip_major(g_pw, PG // NCHIP), g_wout0, g_win1, g_wout1]
    part_a, swap_a = chip_partials(early, ["pool", "wout0", "win1", "wout1"]) if dist else (None, None)
    dproj0, *got_a = _attn_bwd(proj0, ysb, dys, dproj0, swap_a)
    g_win0 = _grad_w_cols(h0, dproj0, "grad_win0")
    part_b, swap_b = chip_partials([g_win0], ["win0"]) if dist else (None, None)
    dx0, acc_n0, *got_b = _inproj_bwd(dproj0, win0, x, dx1, vecs0, "inproj0_bwd", swap_b)

    if dist:
        names = ["win0", "pool", "wout0", "win1", "wout1"]
        halves = [_add_partials(p32, t, sel, "rs_add_partials_" + nm)
                  for p32, t, nm in zip(part_b + part_a, got_b + got_a, names)]
        grads = tuple(_join_halves(halves))
    else:
        grads = (g_win0, g_pw, g_wout0, g_win1, g_wout1)

    sums = dict(
        dm0=jnp.concatenate([acc_n0[0:2], acc_g0[0:1]], axis=0),
        dm1=jnp.concatenate([acc_n1[0:2], acc_f[2:3]], axis=0),
        norm_g=jnp.concatenate([acc_n0[2:3], acc_n1[2:3]], axis=0),
        pool_scale=acc_ps[0:1], final_g=acc_f[0:1], loss=acc_f[1:2],
        conv_w=acc_cv[0:3], conv_b=acc_cv[3:4])
    return dx0, grads, sums


ANY = pl.BlockSpec(memory_space=pl.ANY)
CHIP_FLIPS = ((1, 0), (0, 1), (1, 1))


def _place():
    return lax.axis_index("x"), lax.axis_index("y"), lax.axis_index("c")


def _flip(v, f):
    return 1 - v if f else v


def _allgather8(v, name):
    m_per, n = v.shape

    def body(x_ref, out_ref, send_sems, recv_sems, local_sem):
        x, y, c = _place()
        me, sibling = (x, y, c), (x, y, 1 - c)
        chips = [(_flip(x, fx), _flip(y, fy)) for fx, fy in CHIP_FLIPS]

        def rows(px, py, pc):
            return out_ref.at[pl.ds((4 * px + 2 * py + pc) * m_per, m_per), :]

        def copy(k, block, to, src=None):
            return pltpu.make_async_remote_copy(
                src_ref=rows(*block) if src is None else src, dst_ref=rows(*block),
                send_sem=send_sems.at[k], recv_sem=recv_sems.at[k], device_id=to, device_id_type=MESH)

        mine = pltpu.make_async_copy(x_ref, rows(*me), local_sem)
        mine.start()
        first = [copy(0, me, sibling, src=x_ref)]
        first += [copy(1 + j, me, (*chip, c), src=x_ref) for j, chip in enumerate(chips)]
        for cp in first:
            cp.start()
        passed = [copy(4 + j, (*chip, c), sibling) for j, chip in enumerate(chips)]
        for j, chip in enumerate(chips):
            copy(1 + j, (*chip, c), me).wait_recv()
            passed[j].start()
        copy(0, sibling, me).wait_recv()
        for j, chip in enumerate(chips):
            copy(4 + j, (*chip, 1 - c), me).wait_recv()
        for cp in first + passed:
            cp.wait_send()
        mine.wait()

    return pl.pallas_call(
        body, name=name,
        out_shape=jax.ShapeDtypeStruct((NDEV * m_per, n), v.dtype),
        in_specs=[pl.BlockSpec(memory_space=pltpu.VMEM)],
        out_specs=pl.BlockSpec(memory_space=pltpu.VMEM),
        scratch_shapes=[pltpu.SemaphoreType.DMA((7,)), pltpu.SemaphoreType.DMA((7,)), pltpu.SemaphoreType.DMA],
    )(v)


class _Comm:
    def __init__(self, arrays, out_shape, sems, start, finish):
        self.arrays, self.out_shape, self.sems, self.start, self.finish = arrays, out_shape, sems, start, finish


_NO_COMM = _Comm([], [], [], None, None)


def _run_comm(comm, name):
    n = len(comm.arrays)

    def body(*refs):
        args = (refs[:n], refs[n:n + len(comm.out_shape)], refs[n + len(comm.out_shape):])
        comm.start(*args)
        comm.finish(*args)

    return pl.pallas_call(
        body, name=name, out_shape=comm.out_shape,
        in_specs=[ANY] * n, out_specs=[ANY] * len(comm.out_shape), scratch_shapes=comm.sems,
    )(*comm.arrays)


def _gather_comm(shards):
    n = len(shards)

    def pieces(ins, outs, sems, kinds):
        x, y, c = _place()
        ici_send, ici_recv, fwd_send, fwd_recv = sems[:4]
        own, sibling = 2 * x + y, (x, y, 1 - c)
        made = {kind: [] for kind in kinds}
        for w in range(n):
            r2 = ins[w].shape[0] // 2
            mine, other = pl.ds(c * r2, r2), pl.ds((1 - c) * r2, r2)
            for d, (fx, fy) in enumerate(CHIP_FLIPS):
                px, py, k = _flip(x, fx), _flip(y, fy), 3 * w + d
                peer = 2 * px + py
                ici = dict(send_sem=ici_send.at[k], recv_sem=ici_recv.at[k], device_id=(px, py, c), device_id_type=MESH)
                fwd = dict(send_sem=fwd_send.at[k], recv_sem=fwd_recv.at[k], device_id=sibling, device_id_type=MESH)
                if "ici_out" in kinds:
                    made["ici_out"].append(pltpu.make_async_remote_copy(
                        src_ref=ins[w].at[mine, :], dst_ref=outs[w].at[own, mine, :], **ici))
                if "ici_in" in kinds:
                    made["ici_in"].append(pltpu.make_async_remote_copy(
                        src_ref=ins[w].at[mine, :], dst_ref=outs[w].at[peer, mine, :], **ici))
                if "fwd_out" in kinds:
                    made["fwd_out"].append(pltpu.make_async_remote_copy(
                        src_ref=outs[w].at[peer, mine, :], dst_ref=outs[w].at[peer, mine, :], **fwd))
                if "fwd_in" in kinds:
                    made["fwd_in"].append(pltpu.make_async_remote_copy(
                        src_ref=outs[w].at[peer, other, :], dst_ref=outs[w].at[peer, other, :], **fwd))
        return made

    def stage_rows(a):
        return min(a.shape[0], 256)

    def start(ins, outs, sems):
        for cp in pieces(ins, outs, sems, ("ici_out",))["ici_out"]:
            cp.start()
        x, y, _ = _place()
        for w, stage in enumerate(sems[4:]):
            rows = stage.shape[0]
            for r in range(0, ins[w].shape[0], rows):
                pltpu.sync_copy(ins[w].at[pl.ds(r, rows), :], stage)
                pltpu.sync_copy(stage, outs[w].at[2 * x + y, pl.ds(r, rows), :])

    def finish(ins, outs, sems):
        made = pieces(ins, outs, sems, ("ici_out", "ici_in", "fwd_out", "fwd_in"))
        for arrived, onward in zip(made["ici_in"], made["fwd_out"]):
            arrived.wait_recv()
            onward.start()
        for cp in made["fwd_in"]:
            cp.wait_recv()
        for cp in made["ici_out"] + made["fwd_out"]:
            cp.wait_send()

    dma = pltpu.SemaphoreType.DMA
    return _Comm(list(shards), [jax.ShapeDtypeStruct((NCHIP,) + a.shape, a.dtype) for a in shards],
                 [dma((3 * n,))] * 4 + [pltpu.VMEM((stage_rows(a), a.shape[1]), a.dtype) for a in shards],
                 start, finish)


def _exchange_comm(parts):
    n = len(parts)

    def copies(ins, outs, sems):
        x, y, c = _place()
        send_sems, recv_sems = sems
        out = []
        for w in range(n):
            for d, (fx, fy) in enumerate(CHIP_FLIPS):
                px, py = _flip(x, fx), _flip(y, fy)
                out.append(pltpu.make_async_remote_copy(
                    src_ref=ins[w].at[2 * px + py], dst_ref=outs[w].at[d], send_sem=send_sems.at[3 * w + d],
                    recv_sem=recv_sems.at[3 * w + d], device_id=(px, py, c), device_id_type=MESH))
        return out

    def start(ins, outs, sems):
        for cp in copies(ins, outs, sems):
            cp.start()

    def finish(ins, outs, sems):
        cps = copies(ins, outs, sems)
        for cp in cps:
            cp.wait_recv()
        for cp in cps:
            cp.wait_send()

    dma = pltpu.SemaphoreType.DMA
    return _Comm(list(parts), [jax.ShapeDtypeStruct((3,) + p.shape[1:], BF16) for p in parts],
                 [dma((3 * n,))] * 2, start, finish)


def _send_halves(grads, name):
    n = len(grads)

    def body(*refs):
        ins, outs = refs[:n], refs[n:2 * n]
        send_sems, recv_sems = refs[2 * n:]
        x, y, c = _place()
        copies = []
        for w in range(n):
            r2 = ins[w].shape[1] // 2
            cp = pltpu.make_async_remote_copy(
                src_ref=ins[w].at[:, pl.ds((1 - c) * r2, r2), :], dst_ref=outs[w],
                send_sem=send_sems.at[w], recv_sem=recv_sems.at[w], device_id=(x, y, 1 - c), device_id_type=MESH)
            cp.start()
            copies.append(cp)
        for cp in copies:
            cp.wait_recv()
        for cp in copies:
            cp.wait_send()

    return pl.pallas_call(
        body, name=name,
        out_shape=[jax.ShapeDtypeStruct((NCHIP, g.shape[1] // 2, g.shape[2]), F32) for g in grads],
        in_specs=[ANY] * n, out_specs=[ANY] * n,
        scratch_shapes=[pltpu.SemaphoreType.DMA((n,)), pltpu.SemaphoreType.DMA((n,))],
    )(*grads)


def _row_tile(rows, cols):
    tr = rows
    while tr * cols * 4 > (1 << 20) and tr % 16 == 0:
        tr //= 2
    return tr


def _add_halves(g, t1, core, name):
    _, r, cdim = g.shape
    r2 = r // 2
    tr = _row_tile(r2, cdim)
    nt = r2 // tr

    def body(core_ref, g_ref, t_ref, p_ref, pb_ref):
        p = g_ref[...] + t_ref[...]
        p_ref[...] = p
        pb_ref[...] = p.astype(BF16)

    blk = pl.BlockSpec((None, tr, cdim), lambda j, i, core_ref: (j, i, 0))
    return pl.pallas_call(
        body, name=name,
        grid_spec=pltpu.PrefetchScalarGridSpec(
            num_scalar_prefetch=1, grid=(NCHIP, nt),
            in_specs=[pl.BlockSpec((None, tr, cdim), lambda j, i, core_ref: (j, core_ref[0] * nt + i, 0)), blk],
            out_specs=[blk, blk]),
        out_shape=[jax.ShapeDtypeStruct((NCHIP, r2, cdim), F32), jax.ShapeDtypeStruct((NCHIP, r2, cdim), BF16)],
        compiler_params=_params("parallel", "parallel"),
    )(core, g, t1)


def _add_partials(p, t2, sel, name):
    _, r2, cdim = p.shape
    tr = _row_tile(r2, cdim)
    nt = r2 // tr

    def body(sel_ref, p_ref, t_ref, o_ref):
        o_ref[...] = ((p_ref[...] + t_ref[0].astype(F32)) + t_ref[1].astype(F32)) + t_ref[2].astype(F32)

    return pl.pallas_call(
        body, name=name,
        grid_spec=pltpu.PrefetchScalarGridSpec(
            num_scalar_prefetch=1, grid=(nt,),
            in_specs=[pl.BlockSpec((None, tr, cdim), lambda i, sel_ref: (sel_ref[0], i, 0)),
                      pl.BlockSpec((3, tr, cdim), lambda i, sel_ref: (0, i, 0))],
            out_specs=pl.BlockSpec((tr, cdim), lambda i, sel_ref: (sel_ref[1] * nt + i, 0))),
        out_shape=jax.ShapeDtypeStruct((2 * r2, cdim), F32),
        compiler_params=_params("parallel"),
    )(sel, p, t2)


def _join_halves(grads):
    n = len(grads)

    def body(*refs):
        bufs = refs[n:2 * n]
        send_sems, recv_sems = refs[2 * n:]
        x, y, c = _place()
        copies = []
        for w in range(n):
            r2 = bufs[w].shape[0] // 2
            mine = bufs[w].at[pl.ds(c * r2, r2), :]
            cp = pltpu.make_async_remote_copy(
                src_ref=mine, dst_ref=mine, send_sem=send_sems.at[w], recv_sem=recv_sems.at[w],
                device_id=(x, y, 1 - c), device_id_type=MESH)
            cp.start()
            copies.append(cp)
        for w in range(n):
            r2 = bufs[w].shape[0] // 2
            theirs = bufs[w].at[pl.ds((1 - c) * r2, r2), :]
            pltpu.make_async_remote_copy(
                src_ref=theirs, dst_ref=theirs, send_sem=send_sems.at[w], recv_sem=recv_sems.at[w],
                device_id=(x, y, 1 - c), device_id_type=MESH).wait_recv()
        for cp in copies:
            cp.wait_send()

    return pl.pallas_call(
        body, name="rs_join_halves",
        out_shape=[jax.ShapeDtypeStruct(g.shape, F32) for g in grads],
        in_specs=[ANY] * n, out_specs=[ANY] * n, input_output_aliases={w: w for w in range(n)},
        scratch_shapes=[pltpu.SemaphoreType.DMA((n,)), pltpu.SemaphoreType.DMA((n,))],
    )(*grads)


def _ada_fwd(c_all, ada_w):
    nl, _, ns = ada_w.shape

    def body(c_ref, w_ref, o_ref):
        cv = c_ref[...]
        o_ref[...] = _dot((cv * jax.nn.sigmoid(cv)).astype(BF16), w_ref[...].astype(BF16))

    return pl.pallas_call(
        body, name="ada_fwd", grid=(nl,),
        in_specs=[pl.BlockSpec((NDEV, D), lambda i: (0, 0)), pl.BlockSpec((None, D, ns), lambda i: (i, 0, 0))],
        out_specs=pl.BlockSpec((None, NDEV, ns), lambda i: (i, 0, 0)),
        out_shape=jax.ShapeDtypeStruct((nl, NDEV, ns), F32),
        compiler_params=_params("parallel"),
    )(c_all, ada_w)


PACK_ROWS = 24


def _reduce_packed(gathered):
    def body(g_ref, tot_ref, loss_ref):
        tot = g_ref[0:PACK_ROWS, :]
        for dev in range(1, NDEV):
            tot = tot + g_ref[dev * PACK_ROWS:(dev + 1) * PACK_ROWS, :]
        tot_ref[...] = tot
        loss_ref[...] = jnp.zeros((8, 128), F32) + jnp.sum(tot[10:11, :])

    return pl.pallas_call(
        body, name="reduce_packed",
        out_shape=[jax.ShapeDtypeStruct((PACK_ROWS, D), F32), jax.ShapeDtypeStruct((8, 128), F32)],
    )(gathered)


def _ada_w_update(c_t, dms, w, m, v):
    nl, _, ns = w.shape
    tr = 256

    def body(ct_ref, dm_ref, w_ref, m_ref, v_ref, g_ref, d_ref, m2_ref, v2_ref):
        ct = ct_ref[...]
        sc = ct * jax.nn.sigmoid(ct)
        dm = dm_ref[...]
        g = sc[:, 0:1] * dm[0:1, :]
        for b in range(1, NDEV):
            g = g + sc[:, b:b + 1] * dm[b:b + 1, :]
        g_ref[...] = g
        d_ref[...], m2_ref[...], v2_ref[...] = _adamw_math(w_ref[...], g, m_ref[...], v_ref[...])

    blk = pl.BlockSpec((None, tr, ns), lambda i, j: (i, j, 0))
    return pl.pallas_call(
        body, name="ada_w_update", grid=(nl, D // tr),
        in_specs=[pl.BlockSpec((tr, NDEV), lambda i, j: (j, 0)),
                  pl.BlockSpec((None, NDEV, ns), lambda i, j: (i, 0, 0)), blk, blk, blk],
        out_specs=[blk] * 4,
        out_shape=[jax.ShapeDtypeStruct((nl, D, ns), F32)] * 4,
        compiler_params=_params("parallel", "parallel"),
    )(c_t, dms, w, m, v)


def _chip_major(a, parts):
    g, _, cdim = a.shape
    return jnp.transpose(a.reshape(g, NCHIP, parts, cdim), (1, 0, 2, 3)).reshape(NCHIP, g * parts, cdim)


def _group_major(a):
    return jnp.transpose(a.reshape(NCHIP, 4, PG // NCHIP, PG), (1, 0, 2, 3)).reshape(4, PG, PG)


def kernel(x, c, norm_g, ada_w, ada_b, even_w_in, pool_w, pool_scale, even_w_out, odd_w_in, conv_w, conv_b, odd_w_out, final_g, loss_target, m_norm_g, m_ada_w, m_ada_b, m_even_w_in, m_pool_w, m_pool_scale, m_even_w_out, m_odd_w_in, m_conv_w, m_conv_b, m_odd_w_out, m_final_g, v_norm_g, v_ada_w, v_ada_b, v_even_w_in, v_pool_w, v_pool_scale, v_even_w_out, v_odd_w_in, v_conv_w, v_conv_b, v_odd_w_out, v_final_g):
    ix, iy, ic = _place()
    chip = 2 * ix + iy
    batch = 2 * chip + ic
    sel = jnp.stack([chip, ic]).astype(jnp.int32)
    ns_ada = ada_w.shape[2]
    ns_conv = conv_b.shape[1]

    conv_rows = jnp.pad(jnp.concatenate([conv_w[0], conv_b], axis=0), ((0, 3), (0, D - ns_conv)))
    first = _allgather8(jnp.concatenate([c, conv_rows], axis=0), "gather_c_conv").reshape(NCHIP, 2, 8, D)
    c_all = first[:, :, 0].reshape(NDEV, D)
    cw_full = jnp.transpose(first[:, 0, 1:5, 0:ns_conv], (1, 0, 2)).reshape(4, DI)
    cw8 = jnp.concatenate([cw_full[0:3], jnp.zeros((5, DI), F32)], axis=0)
    cb_full = cw_full[3:4]

    m_cols = _allgather8(_ada_fwd(c_all, ada_w).reshape(2 * NDEV, ns_ada), "gather_ada")
    m_cols = m_cols.reshape(NCHIP, 2, 2, NDEV, ns_ada)[:, 0]
    m_mine = lax.dynamic_index_in_dim(m_cols, batch, axis=2, keepdims=False)
    m_mine = jnp.transpose(m_mine, (1, 0, 2)).reshape(2, 3 * D) + ada_b
    zrow = jnp.zeros((3, D), F32)

    def vec_rows(i):
        sh, sc, gt = m_mine[i, 0:D], m_mine[i, D:2 * D], m_mine[i, 2 * D:3 * D]
        return jnp.concatenate([jnp.stack([norm_g[i], sc, sh, gt, final_g]), zrow], axis=0)

    win0, = _run_comm(_gather_comm([even_w_in[0].astype(BF16)]), "gather_win0")
    shards = (pool_w[0].astype(BF16).reshape(PG, PG), even_w_out[0].astype(BF16),
              odd_w_in[0].astype(BF16), odd_w_out[0].astype(BF16))
    dx0, grads, sums = _local_step(
        x[0], loss_target[0], vec_rows(0), vec_rows(1), win0, shards, pool_scale, cw8, cb_full, sel)
    r_win0, r_pw, r_wout0, r_win1, r_wout1 = grads

    packed = jnp.concatenate([
        sums["dm0"], sums["dm1"], sums["norm_g"], sums["pool_scale"], sums["final_g"], sums["loss"],
        sums["conv_w"].reshape(6, D), sums["conv_b"].reshape(2, D), jnp.zeros((PACK_ROWS - 19, D), F32)], axis=0)
    gathered = _allgather8(packed, "gather_sums")
    tot, loss8 = _reduce_packed(gathered)
    loss = loss8[0, 0]
    g_norm_g, g_pool_scale, g_final_g = tot[6:8], tot[8:9], tot[9]
    g_ada_b = tot[0:6].reshape(2, 3 * D)
    g_conv_w = lax.dynamic_slice_in_dim(tot[11:17].reshape(3, DI), chip * (DI // NCHIP), DI // NCHIP, axis=1)
    g_conv_b = lax.dynamic_slice_in_dim(tot[17:19].reshape(1, DI), chip * (DI // NCHIP), DI // NCHIP, axis=1)
    dm_all = gathered.reshape(NDEV, PACK_ROWS, D)[:, 0:6].reshape(NDEV, 2, 3 * D)
    dm_cols = jnp.transpose(lax.dynamic_slice_in_dim(dm_all, chip * ns_ada, ns_ada, axis=2), (1, 0, 2))
    g_ada_w, d_ada_w, nm_ada_w, nv_ada_w = _ada_w_update(jnp.transpose(c_all), dm_cols, ada_w, m_ada_w, v_ada_w)

    def upd(w, g, m, v, name):
        shape = w.shape
        w2, m2, v2 = (a.reshape(g.shape) for a in (w, m, v))
        d, nm, nv = _adamw(w2, g, m2, v2, name)
        return g.reshape(shape), d.reshape(shape), nm.reshape(shape), nv.reshape(shape)

    o_win0 = upd(even_w_in, r_win0, m_even_w_in, v_even_w_in, "adamw_win0")
    o_pw = upd(pool_w, r_pw, m_pool_w, v_pool_w, "adamw_pool")
    o_wout0 = upd(even_w_out, r_wout0, m_even_w_out, v_even_w_out, "adamw_wout0")
    o_win1 = upd(odd_w_in, r_win1, m_odd_w_in, v_odd_w_in, "adamw_win1")
    o_wout1 = upd(odd_w_out, r_wout1, m_odd_w_out, v_odd_w_out, "adamw_wout1")

    def pack_small(ng, ab, ps, fg, cwv, cbv):
        conv = jnp.concatenate([cwv.reshape(3, -1), cbv.reshape(1, -1)], axis=0).reshape(2, D)
        return jnp.concatenate([ng, ab.reshape(6, D), ps, fg.reshape(1, D), conv, jnp.zeros((4, D), F32)], axis=0)

    sw = pack_small(norm_g, ada_b, pool_scale, final_g, conv_w, conv_b)
    sg = pack_small(g_norm_g, g_ada_b, g_pool_scale, g_final_g, g_conv_w, g_conv_b)
    sm = pack_small(m_norm_g, m_ada_b, m_pool_scale, m_final_g, m_conv_w, m_conv_b)
    sv = pack_small(v_norm_g, v_ada_b, v_pool_scale, v_final_g, v_conv_w, v_conv_b) + jnp.concatenate(
        [jnp.zeros((12, D), F32), jnp.ones((4, D), F32)], axis=0)
    small = _adamw(sw, sg, sm, sv, "adamw_small")

    def unpack_small(a):
        conv = a[10:12].reshape(4, -1)
        return dict(norm_g=a[0:2], ada_b=a[2:8].reshape(2, 3 * D), pool_scale=a[8:9], final_g=a[9],
                    conv_w=conv[0:3].reshape(conv_w.shape), conv_b=conv[3:4].reshape(conv_b.shape))

    s_grad = dict(norm_g=g_norm_g, ada_b=g_ada_b, pool_scale=g_pool_scale, final_g=g_final_g,
                  conv_w=g_conv_w.reshape(conv_w.shape), conv_b=g_conv_b.reshape(conv_b.shape))
    s_out = [s_grad] + [unpack_small(a) for a in small]

    outs = []
    for k in range(4):
        sm_k = s_out[k]
        outs.append([sm_k["norm_g"], (g_ada_w, d_ada_w, nm_ada_w, nv_ada_w)[k], sm_k["ada_b"], o_win0[k], o_pw[k],
                     sm_k["pool_scale"], o_wout0[k], o_win1[k], sm_k["conv_w"], sm_k["conv_b"], o_wout1[k],
                     sm_k["final_g"]])
    return (loss, dx0[None], *outs[0], *outs[1], *outs[2], *outs[3])
```

```python
import functools

import jax
import jax.numpy as jnp
from jax import lax
from jax.experimental import pallas as pl
from jax.experimental.pallas import tpu as pltpu

F32 = jnp.float32
BF16 = jnp.bfloat16
MESH = pl.DeviceIdType.MESH

D = 1024
DI = 2048
DP = 1024
NE = 6144
NO = 8192
WINDOWS = (2, 4, 8, 16)
PG = 256
HD = 64
NCHIP = 4
NDEV = 8
EPS = 1e-6
INV_SQRT_HD = 0.125

LR, B1, B2, EPS_ADAM, WD, STEP = 0.001, 0.9, 0.999, 1e-08, 0.01, 10

TM = 512
TME = 256
CT = 512
BQ = 512
BK = 256
assert BQ == 2 * BK
HALO = 16
DEAD_LOG_WEIGHT = -104.0
VMEM_LIMIT = 56 * 1024 * 1024


def _dot(a, b):
    return jnp.dot(a, b, preferred_element_type=F32)


def _dot_nt(a, b):
    return lax.dot_general(a, b, (((1,), (1,)), ((), ())), preferred_element_type=F32)


def _dot_tn(a, b):
    return lax.dot_general(a, b, (((0,), (0,)), ((), ())), preferred_element_type=F32)


def _params(*sem):
    return pltpu.CompilerParams(dimension_semantics=sem, vmem_limit_bytes=VMEM_LIMIT)


def _sigmoid(v):
    return 0.5 * jnp.tanh(0.5 * v) + 0.5


def _rowsum(v):
    return jnp.sum(v, axis=0, keepdims=True)


def _modulated_norm(xv, vec_ref):
    r = lax.rsqrt(jnp.mean(xv * xv, axis=-1, keepdims=True) + EPS)
    return (((xv * r) * vec_ref[0:1, :]) * (1.0 + vec_ref[1:2, :]) + vec_ref[2:3, :]).astype(BF16)


def _norm_mod(x, vecs, name):
    s = x.shape[0]
    tm = min(TM, s)

    def body(x_ref, vec_ref, h_ref):
        h_ref[...] = _modulated_norm(x_ref[...], vec_ref)

    return pl.pallas_call(
        body, name=name, grid=(s // tm,),
        in_specs=[pl.BlockSpec((tm, D), lambda i: (i, 0)), pl.BlockSpec((8, D), lambda i: (0, 0))],
        out_specs=pl.BlockSpec((tm, D), lambda i: (i, 0)),
        out_shape=jax.ShapeDtypeStruct((s, D), BF16),
        compiler_params=_params("parallel"),
    )(x, vecs)


def _inproj(h, w, name, comm=None):
    s = h.shape[0]
    ns = w.shape[2]
    tm = min(TM, s)
    ni = s // tm
    comm = comm or _NO_COMM
    nci, nco = len(comm.arrays), len(comm.out_shape)

    def body(*refs):
        h_ref, w_ref = refs[:2]
        proj_ref = refs[2 + nci]
        cargs = (refs[2:2 + nci], refs[3 + nci:3 + nci + nco], refs[3 + nci + nco:])
        j, i = pl.program_id(0), pl.program_id(1)
        if nci:
            pl.when((j == 0) & (i == 0))(lambda: comm.start(*cargs))
        proj_ref[...] = _dot(h_ref[...], w_ref[...]).astype(BF16)
        if nci:
            pl.when((j == NCHIP - 1) & (i == ni - 1))(lambda: comm.finish(*cargs))

    return pl.pallas_call(
        body, name=name, grid=(NCHIP, ni),
        in_specs=[pl.BlockSpec((tm, D), lambda j, i: (i, 0)),
                  pl.BlockSpec((None, D, ns), lambda j, i: (j, 0, 0))] + [ANY] * nci,
        out_specs=[pl.BlockSpec((tm, ns), lambda j, i: (i, j))] + [ANY] * nco,
        out_shape=[jax.ShapeDtypeStruct((s, NCHIP * ns), BF16)] + comm.out_shape,
        scratch_shapes=comm.sems,
        compiler_params=_params("arbitrary", "arbitrary"),
    )(h, w, *comm.arrays)


def _pool_fwd(proj0, pw, pscale):
    s = proj0.shape[0]
    tm = min(TM, s)
    hb = tm // 16

    def body(u_ref, halo_ref, w_ref, sc_ref, p_ref, y_ref, ext_ref):
        i = pl.program_id(0)
        ext_ref[16:, :] = u_ref[...].astype(F32)
        ext_ref[0:16, :] = jnp.where(i > 0, halo_ref[...].astype(F32), 0.0)
        t = i * tm + lax.broadcasted_iota(jnp.int32, (tm, 1), 0)
        for g, wdw in enumerate(WINDOWS):
            cs = slice(g * PG, (g + 1) * PG)
            u = ext_ref[16:16 + tm, cs]
            acc = u
            for j in range(1, wdw):
                acc = acc + ext_ref[16 - j:16 - j + tm, cs]
            inv = 1.0 / jnp.minimum(t + 1, wdw).astype(F32)
            pb = (acc * inv - u).astype(BF16)
            p_ref[:, cs] = pb
            y_ref[:, cs] = _dot(pb, w_ref[g]) * sc_ref[:, cs]

    return pl.pallas_call(
        body, name="pool_fwd", grid=(s // tm,),
        in_specs=[pl.BlockSpec((tm, DP), lambda i: (i, 0)),
                  pl.BlockSpec((16, DP), lambda i: (jnp.maximum(i * hb - 1, 0), 0)),
                  pl.BlockSpec((4, PG, PG), lambda i: (0, 0, 0)),
                  pl.BlockSpec((1, DP), lambda i: (0, 0))],
        out_specs=[pl.BlockSpec((tm, DP), lambda i: (i, 0)),
                   pl.BlockSpec((tm, DP), lambda i: (i, 0))],
        out_shape=[jax.ShapeDtypeStruct((s, DP), BF16), jax.ShapeDtypeStruct((s, DP), F32)],
        scratch_shapes=[pltpu.VMEM((tm + 16, DP), F32)],
        compiler_params=_params("parallel"),
    )(proj0, proj0, pw, pscale)


def _sb_logits(nz, mask):
    neg_abs = lax.bitcast_convert_type(lax.bitcast_convert_type(nz, jnp.uint32) | jnp.uint32(0x80000000), F32)
    t = jnp.log(1.0 + jnp.exp(neg_abs))
    lf = jnp.minimum(nz, 0.0) - t
    lam = lf - nz
    if mask is not None:
        lf = jnp.where(mask, lf, 0.0)
    return lf, lam


def _sweep_left(steps, state, step):
    def live(carry):
        j, st = carry
        heaviest = functools.reduce(jnp.maximum, [jnp.max(head[1]) for head in st])
        return (j < steps) & (heaviest > DEAD_LOG_WEIGHT)

    return lax.while_loop(live, lambda carry: (carry[0] + 1, step(carry[0], carry[1])), (0, state))[1]


def _split_dot(v, tri):
    hi = v.astype(BF16)
    lo = (v - hi.astype(F32)).astype(BF16)
    return _dot(hi, tri) + _dot(lo, tri)


def _tri_masks():
    row = lax.broadcasted_iota(jnp.int32, (BK, BK), 0)
    col = lax.broadcasted_iota(jnp.int32, (BK, BK), 1)
    return (row > col).astype(BF16), (row >= col).astype(BF16)


def _causal_mask():
    row = lax.broadcasted_iota(jnp.int32, (BK, BK), 0)
    col = lax.broadcasted_iota(jnp.int32, (BK, BK), 1)
    return col < row


def _attn_fwd(proj0, comm=None):
    s = proj0.shape[0]
    nq = s // BQ
    kpq = BQ // BK
    nsteps = DP // 128
    comm = comm or _NO_COMM
    nci, nco = len(comm.arrays), len(comm.out_shape)

    def body(*refs):
        q_ref, k_ref, v_ref = refs[:3]
        o_ref = refs[3 + nci]
        qn_ref, k8_ref, vb_ref = refs[4 + nci + nco:7 + nci + nco]
        cargs = (refs[3:3 + nci], refs[4 + nci:4 + nci + nco], refs[7 + nci + nco:])
        if nci:
            pl.when(pl.program_id(0) == 0)(lambda: comm.start(*cargs))
        qn_ref[...] = (-q_ref[...]).astype(BF16)
        k8_ref[...] = (k_ref[...] * INV_SQRT_HD).astype(BF16)
        vb_ref[...] = v_ref[...].astype(BF16)
        after, _ = _tri_masks()
        causal = _causal_mask()
        heads = [slice(HD * h, HD * (h + 1)) for h in range(2)]
        hs = range(len(heads))
        lo, hi = (0, BK), (BK, BK)

        def qstep(qi, carry):
            q0 = pl.multiple_of(qi * BQ, BQ)
            qn = [qn_ref[pl.ds(q0, BQ), ls] for ls in heads]

            def sweep(lanes, state):
                nz = [_dot_nt(qn[h][r0:r0 + n], k8_ref[pl.ds(k0, BK), heads[h]]) for h, (r0, n), k0, _ in lanes]
                ll = [_sb_logits(nz[i], lane[3]) for i, lane in enumerate(lanes)]
                aft = [_dot(ll[i][0].astype(BF16), after) for i in range(len(lanes))]
                state = dict(state)
                for i, (h, rows, k0, mask) in enumerate(lanes):
                    o_acc, c = state[h, rows]
                    a = jnp.exp(ll[i][1] + aft[i] + c)
                    if mask is not None:
                        a = jnp.where(mask, a, 0.0)
                    o_acc = o_acc + _dot(a.astype(BF16), vb_ref[pl.ds(k0, BK), heads[h]])
                    state[h, rows] = (o_acc, c + aft[i][:, 0:1] + ll[i][0][:, 0:1])
                return state

            k_lo, k_hi = q0, pl.multiple_of(q0 + BK, BK)
            st = {(h, r): (jnp.zeros((BK, HD), F32), jnp.zeros((BK, 1), F32)) for h in hs for r in (lo, hi)}
            st = sweep([(h, hi, k_hi, causal) for h in hs] + [(h, lo, k_lo, causal) for h in hs]
                       + [(h, hi, k_lo, None) for h in hs], st)
            def left_of(rows):
                def step(j, part):
                    k0 = pl.multiple_of(q0 - (j + 1) * BK, BK)
                    got = sweep([(h, rows, k0, None) for h in hs], {(h, rows): part[h] for h in hs})
                    return tuple(got[h, rows] for h in hs)
                return step

            done = {r: _sweep_left(qi * (BQ // BK), tuple(st[h, r] for h in hs), left_of(r)) for r in (lo, hi)}
            for h, ls in enumerate(heads):
                o_ref[pl.ds(q0, BQ), ls] = jnp.concatenate([done[lo][h][0], done[hi][h][0]], axis=0)
            return carry

        lax.fori_loop(0, nq, qstep, 0)
        if nci:
            pl.when(pl.program_id(0) == nsteps - 1)(lambda: comm.finish(*cargs))

    return pl.pallas_call(
        body, name="attn_fwd", grid=(nsteps,),
        in_specs=[pl.BlockSpec((s, 128), lambda h: (0, 8 + h)),
                  pl.BlockSpec((s, 128), lambda h: (0, 16 + h)),
                  pl.BlockSpec((s, 128), lambda h: (0, 24 + h))] + [ANY] * nci,
        out_specs=[pl.BlockSpec((s, 128), lambda h: (0, h))] + [ANY] * nco,
        out_shape=[jax.ShapeDtypeStruct((s, DP), F32)] + comm.out_shape,
        scratch_shapes=[pltpu.VMEM((s, 128), BF16)] * 3 + comm.sems,
        compiler_params=_params("arbitrary"),
    )(proj0, proj0, proj0, *comm.arrays)


def _even_out(ypool, ysb, proj0, wout, x, vecs, vecs_next):
    s = x.shape[0]
    tm = min(TME, s)

    def body(yp_ref, ys_ref, gate_ref, w_ref, x_ref, vec_ref, vecn_ref, x1_ref, out_ref, yg_ref, hn_ref):
        gt = gate_ref[...].astype(F32)
        sl = gt * _sigmoid(gt)
        yg_ref[:, :DP] = (yp_ref[...] * sl[:, :DP]).astype(BF16)
        yg_ref[:, DP:] = (ys_ref[...] * sl[:, DP:]).astype(BF16)
        out = _dot(yg_ref[...], w_ref[...])
        out_ref[...] = out
        x1 = x_ref[...] + (1.0 + vec_ref[3:4, :]) * out
        x1_ref[...] = x1
        hn_ref[...] = _modulated_norm(x1, vecn_ref)

    row = lambda i: (i, 0)
    const = lambda i: (0, 0)
    return pl.pallas_call(
        body, name="even_out", grid=(s // tm,),
        in_specs=[pl.BlockSpec((tm, DP), row), pl.BlockSpec((tm, DP), row),
                  pl.BlockSpec((tm, DI), lambda i: (i, 2)),
                  pl.BlockSpec((DI, D), const),
                  pl.BlockSpec((tm, D), row), pl.BlockSpec((8, D), const), pl.BlockSpec((8, D), const)],
        out_specs=[pl.BlockSpec((tm, D), row), pl.BlockSpec((tm, D), row), pl.BlockSpec((tm, DI), row),
                   pl.BlockSpec((tm, D), row)],
        out_shape=[jax.ShapeDtypeStruct((s, D), F32), jax.ShapeDtypeStruct((s, D), F32),
                   jax.ShapeDtypeStruct((s, DI), BF16), jax.ShapeDtypeStruct((s, D), BF16)],
        compiler_params=_params("parallel"),
    )(ypool, ysb, proj0, wout, x, vecs, vecs_next)


def _odd_out(proj1, wout, x1, vecs, cw, cb, target):
    s = x1.shape[0]
    tm = min(TME, s)
    hb = tm // HALO

    def body(gb_ref, gc_ref, u_ref, gt_ref, hgc_ref, hu_ref, w_ref, x1_ref, vec_ref, cw_ref, cb_ref, tg_ref,
             dx2_ref, y1_ref, acc_ref, ext_ref):
        i = pl.program_id(0)

        @pl.when(i == 0)
        def _():
            acc_ref[...] = jnp.zeros_like(acc_ref)

        ext_ref[HALO:, :] = gc_ref[...].astype(F32) * u_ref[...].astype(F32)
        ext_ref[0:HALO, :] = jnp.where(i > 0, hgc_ref[...].astype(F32) * hu_ref[...].astype(F32), 0.0)
        for c in range(DI // CT):
            cs = slice(c * CT, (c + 1) * CT)
            conv = (cb_ref[0:1, cs] + cw_ref[0:1, cs] * ext_ref[HALO - 2:HALO - 2 + tm, cs]
                    + cw_ref[1:2, cs] * ext_ref[HALO - 1:HALO - 1 + tm, cs]
                    + cw_ref[2:3, cs] * ext_ref[HALO:HALO + tm, cs])
            gt = gt_ref[:, cs].astype(F32)
            y1_ref[:, cs] = (gb_ref[:, cs].astype(F32) * conv * (gt * _sigmoid(gt))).astype(BF16)
        out = _dot(y1_ref[...], w_ref[...])
        x2 = x1_ref[...] + (1.0 + vec_ref[3:4, :]) * out
        r = lax.rsqrt(jnp.mean(x2 * x2, axis=-1, keepdims=True) + EPS)
        nrm = x2 * r
        fg = vec_ref[4:5, :]
        err = nrm * fg - tg_ref[...]
        acc_ref[1:2, :] += _rowsum(err * err) * (0.5 / D)
        dyf = err * (1.0 / D)
        acc_ref[0:1, :] += _rowsum(dyf * nrm)
        dn = dyf * fg
        dx2 = r * (dn - nrm * jnp.mean(dn * nrm, axis=-1, keepdims=True))
        dx2_ref[...] = dx2
        acc_ref[2:3, :] += _rowsum(dx2 * out)

    row = lambda i: (i, 0)
    halo = lambda col: (lambda i: (jnp.maximum(i * hb - 1, 0), col))
    const = lambda i: (0, 0)
    return pl.pallas_call(
        body, name="odd_out", grid=(s // tm,),
        in_specs=[pl.BlockSpec((tm, DI), lambda i: (i, 0)), pl.BlockSpec((tm, DI), lambda i: (i, 1)),
                  pl.BlockSpec((tm, DI), lambda i: (i, 2)), pl.BlockSpec((tm, DI), lambda i: (i, 3)),
                  pl.BlockSpec((HALO, DI), halo(1)), pl.BlockSpec((HALO, DI), halo(2)),
                  pl.BlockSpec((DI, D), const), pl.BlockSpec((tm, D), row), pl.BlockSpec((8, D), const),
                  pl.BlockSpec((8, DI), const), pl.BlockSpec((1, DI), const), pl.BlockSpec((tm, D), row)],
        out_specs=[pl.BlockSpec((tm, D), row), pl.BlockSpec((tm, DI), row), pl.BlockSpec((8, D), const)],
        out_shape=[jax.ShapeDtypeStruct((s, D), F32), jax.ShapeDtypeStruct((s, DI), BF16),
                   jax.ShapeDtypeStruct((8, D), F32)],
        scratch_shapes=[pltpu.VMEM((tm + HALO, DI), F32)],
        compiler_params=_params("arbitrary"),
    )(proj1, proj1, proj1, proj1, proj1, proj1, wout, x1, vecs, cw, cb, target)


def _odd_bwd(dx2, proj1, wout, vecs, cw, cb):
    s = dx2.shape[0]
    tm = min(TME, s)
    nb = s // tm
    hb = tm // HALO

    def body(dx2_ref, gb_ref, gc_ref, u_ref, gt_ref, hgc_ref, hu_ref, w_ref, vec_ref, cw_ref, cb_ref,
             dout_ref, dproj_ref, accv_ref, uext_ref, dext_ref, dy_ref):
        i = pl.program_id(0)
        blk = nb - 1 - i

        @pl.when(i == 0)
        def _():
            accv_ref[...] = jnp.zeros_like(accv_ref)
            dext_ref[tm:tm + 8, :] = jnp.zeros((8, DI), F32)

        dout = (dx2_ref[...] * (1.0 + vec_ref[3:4, :])).astype(BF16)
        dout_ref[...] = dout
        dy_ref[...] = _dot_nt(dout, w_ref[...])
        uext_ref[HALO:, :] = gc_ref[...].astype(F32) * u_ref[...].astype(F32)
        uext_ref[0:HALO, :] = jnp.where(blk > 0, hgc_ref[...].astype(F32) * hu_ref[...].astype(F32), 0.0)
        for c in range(DI // CT):
            cs = slice(c * CT, (c + 1) * CT)
            u0 = uext_ref[HALO - 2:HALO - 2 + tm, cs]
            u1 = uext_ref[HALO - 1:HALO - 1 + tm, cs]
            u2 = uext_ref[HALO:HALO + tm, cs]
            w0, w1, w2 = cw_ref[0:1, cs], cw_ref[1:2, cs], cw_ref[2:3, cs]
            conv = cb_ref[0:1, cs] + w0 * u0 + w1 * u1 + w2 * u2
            gt = gt_ref[:, cs].astype(F32)
            sg = _sigmoid(gt)
            gb = gb_ref[:, cs].astype(F32)
            dy = dy_ref[:, cs]
            t1 = dy * (gt * sg)
            dproj_ref[:, cs] = (t1 * conv).astype(BF16)
            dconv = t1 * gb
            dproj_ref[:, 3 * DI + c * CT:3 * DI + (c + 1) * CT] = (
                dy * gb * conv * (sg * (1.0 + gt * (1.0 - sg)))).astype(BF16)
            accv_ref[0:1, cs] += _rowsum(dconv * u0)
            accv_ref[1:2, cs] += _rowsum(dconv * u1)
            accv_ref[2:3, cs] += _rowsum(dconv * u2)
            accv_ref[3:4, cs] += _rowsum(dconv)
            dext_ref[0:tm, cs] = dconv
            duu = w2 * dconv + w1 * dext_ref[1:tm + 1, cs] + w0 * dext_ref[2:tm + 2, cs]
            dproj_ref[:, DI + c * CT:DI + (c + 1) * CT] = (duu * u_ref[:, cs].astype(F32)).astype(BF16)
            dproj_ref[:, 2 * DI + c * CT:2 * DI + (c + 1) * CT] = (duu * gc_ref[:, cs].astype(F32)).astype(BF16)
        dext_ref[tm:tm + 8, :] = dext_ref[0:8, :]

    rrow = lambda i: (nb - 1 - i, 0)
    rcol = lambda col: (lambda i: (nb - 1 - i, col))
    halo = lambda col: (lambda i: (jnp.maximum((nb - 1 - i) * hb - 1, 0), col))
    const = lambda i: (0, 0)
    return pl.pallas_call(
        body, name="odd_bwd", grid=(nb,),
        in_specs=[pl.BlockSpec((tm, D), rrow),
                  pl.BlockSpec((tm, DI), rcol(0)), pl.BlockSpec((tm, DI), rcol(1)),
                  pl.BlockSpec((tm, DI), rcol(2)), pl.BlockSpec((tm, DI), rcol(3)),
                  pl.BlockSpec((HALO, DI), halo(1)), pl.BlockSpec((HALO, DI), halo(2)),
                  pl.BlockSpec((DI, D), const), pl.BlockSpec((8, D), const),
                  pl.BlockSpec((8, DI), const), pl.BlockSpec((1, DI), const)],
        out_specs=[pl.BlockSpec((tm, D), rrow), pl.BlockSpec((tm, NO), rrow), pl.BlockSpec((8, DI), const)],
        out_shape=[jax.ShapeDtypeStruct((s, D), BF16), jax.ShapeDtypeStruct((s, NO), BF16),
                   jax.ShapeDtypeStruct((8, DI), F32)],
        scratch_shapes=[pltpu.VMEM((tm + HALO, DI), F32), pltpu.VMEM((tm + 8, DI), F32), pltpu.VMEM((tm, DI), F32)],
        compiler_params=_params("arbitrary"),
    )(dx2, proj1, proj1, proj1, proj1, proj1, proj1, wout, vecs, cw, cb)


def _grad_w_body(nk):
    def body(a_ref, b_ref, o_ref, ob_ref):
        k = pl.program_id(1)

        @pl.when(k == 0)
        def _():
            o_ref[...] = jnp.zeros_like(o_ref)

        o_ref[...] += _dot_tn(a_ref[...], b_ref[...])

        @pl.when(k == nk - 1)
        def _():
            ob_ref[...] = o_ref[...].astype(BF16)

    return body


def _grad_w_cols(a, b, name):
    s, m = a.shape
    ns = b.shape[1] // NCHIP
    ts = min(TM, s)
    blk = pl.BlockSpec((None, m, ns), lambda j, k: (j, 0, 0))
    return pl.pallas_call(
        _grad_w_body(s // ts), name=name, grid=(NCHIP, s // ts),
        in_specs=[pl.BlockSpec((ts, m), lambda j, k: (k, 0)),
                  pl.BlockSpec((ts, ns), lambda j, k: (k, j))],
        out_specs=[blk, blk],
        out_shape=[jax.ShapeDtypeStruct((NCHIP, m, ns), F32), jax.ShapeDtypeStruct((NCHIP, m, ns), BF16)],
        compiler_params=_params("parallel", "arbitrary"),
    )(a, b)


def _grad_w_rows(a, b, name):
    s = a.shape[0]
    ms = a.shape[1] // NCHIP
    n = b.shape[1]
    ts = min(TM, s)
    blk = pl.BlockSpec((None, ms, n), lambda i, k: (i, 0, 0))
    return pl.pallas_call(
        _grad_w_body(s // ts), name=name, grid=(NCHIP, s // ts),
        in_specs=[pl.BlockSpec((ts, ms), lambda i, k: (k, i)),
                  pl.BlockSpec((ts, n), lambda i, k: (k, 0))],
        out_specs=[blk, blk],
        out_shape=[jax.ShapeDtypeStruct((NCHIP, ms, n), F32), jax.ShapeDtypeStruct((NCHIP, ms, n), BF16)],
        compiler_params=_params("parallel", "arbitrary"),
    )(a, b)


def _inproj_bwd(dproj, w, x, dx_in, vecs, name, comm=None):
    s = x.shape[0]
    ns = w.shape[2]
    tm = min(TME, s)
    ni = s // tm
    comm = comm or _NO_COMM
    nci, nco = len(comm.arrays), len(comm.out_shape)

    def body(*refs):
        dp_ref, w_hbm, x_ref, dxin_ref, vec_ref = refs[:5]
        dx_ref, acc_ref = refs[5 + nci:7 + nci]
        w_ref = refs[7 + nci + nco]
        cargs = (refs[5:5 + nci], refs[7 + nci:7 + nci + nco], refs[8 + nci + nco:])
        i = pl.program_id(0)

        @pl.when(i == 0)
        def _():
            acc_ref[...] = jnp.zeros_like(acc_ref)
            if nci:
                comm.start(*cargs)
            pltpu.sync_copy(w_hbm, w_ref)

        dh = _dot_nt(dp_ref[:, 0:ns], w_ref[0])
        for j in range(1, NCHIP):
            dh = dh + _dot_nt(dp_ref[:, j * ns:(j + 1) * ns], w_ref[j])
        xv = x_ref[...]
        r = lax.rsqrt(jnp.mean(xv * xv, axis=-1, keepdims=True) + EPS)
        nrm = xv * r
        g = vec_ref[0:1, :]
        sc1 = 1.0 + vec_ref[1:2, :]
        dhn = dh * nrm
        acc_ref[0:1, :] += _rowsum(dh)
        acc_ref[1:2, :] += _rowsum(dhn) * g
        acc_ref[2:3, :] += _rowsum(dhn) * sc1
        dn = dh * (g * sc1)
        dx_ref[...] = dxin_ref[...] + r * (dn - nrm * jnp.mean(dn * nrm, axis=-1, keepdims=True))

        if nci:
            pl.when(i == ni - 1)(lambda: comm.finish(*cargs))

    row = lambda i: (i, 0)
    const = lambda i: (0, 0)
    return pl.pallas_call(
        body, name=name, grid=(ni,),
        in_specs=[pl.BlockSpec((tm, NCHIP * ns), row), ANY,
                  pl.BlockSpec((tm, D), row), pl.BlockSpec((tm, D), row), pl.BlockSpec((8, D), const)] + [ANY] * nci,
        out_specs=[pl.BlockSpec((tm, D), row), pl.BlockSpec((8, D), const)] + [ANY] * nco,
        out_shape=[jax.ShapeDtypeStruct((s, D), F32), jax.ShapeDtypeStruct((8, D), F32)] + comm.out_shape,
        scratch_shapes=[pltpu.VMEM(w.shape, BF16)] + comm.sems,
        compiler_params=_params("arbitrary"),
    )(dproj, w, x, dx_in, vecs, *comm.arrays)


def _even_bwd(dx1, out0, ypool, ysb, proj0, wout, vecs):
    s = dx1.shape[0]
    tm = min(TME, s)

    def body(dx1_ref, out0_ref, yp_ref, ys_ref, gate_ref, w_ref, vec_ref,
             dout_ref, dyp_ref, dys_ref, dgt_ref, acc_ref):
        @pl.when(pl.program_id(0) == 0)
        def _():
            acc_ref[...] = jnp.zeros_like(acc_ref)

        dx1v = dx1_ref[...]
        acc_ref[0:1, :] += _rowsum(dx1v * out0_ref[...])
        dout = (dx1v * (1.0 + vec_ref[3:4, :])).astype(BF16)
        dout_ref[...] = dout
        dyg = _dot_nt(dout, w_ref[...])
        gt = gate_ref[...].astype(F32)
        sg = _sigmoid(gt)
        sl = gt * sg
        dsl = sg * (1.0 + gt * (1.0 - sg))
        dyp_ref[...] = dyg[:, :DP] * sl[:, :DP]
        dys_ref[...] = dyg[:, DP:] * sl[:, DP:]
        dgt_ref[:, :DP] = (dyg[:, :DP] * yp_ref[...] * dsl[:, :DP]).astype(BF16)
        dgt_ref[:, DP:] = (dyg[:, DP:] * ys_ref[...] * dsl[:, DP:]).astype(BF16)

    row = lambda i: (i, 0)
    const = lambda i: (0, 0)
    return pl.pallas_call(
        body, name="even_bwd", grid=(s // tm,),
        in_specs=[pl.BlockSpec((tm, D), row), pl.BlockSpec((tm, D), row),
                  pl.BlockSpec((tm, DP), row), pl.BlockSpec((tm, DP), row),
                  pl.BlockSpec((tm, DI), lambda i: (i, 2)),
                  pl.BlockSpec((DI, D), const), pl.BlockSpec((8, D), const)],
        out_specs=[pl.BlockSpec((tm, D), row), pl.BlockSpec((tm, DP), row), pl.BlockSpec((tm, DP), row),
                   pl.BlockSpec((tm, DI), lambda i: (i, NE // DI - 1)), pl.BlockSpec((8, D), const)],
        out_shape=[jax.ShapeDtypeStruct((s, D), BF16), jax.ShapeDtypeStruct((s, DP), F32),
                   jax.ShapeDtypeStruct((s, DP), F32), jax.ShapeDtypeStruct((s, NE), BF16),
                   jax.ShapeDtypeStruct((8, D), F32)],
        compiler_params=_params("arbitrary"),
    )(dx1, out0, ypool, ysb, proj0, wout, vecs)


def _pool_bwd(dyp, p, pw, pscale, dproj):
    s = dyp.shape[0]
    tm = min(TM, s)
    nb = s // tm
    hb = tm // 16

    def body(dy_ref, dyh_ref, p_ref, w_ref, sc_ref, dproj_hbm, du_ref, dw_ref, acc_ref, ext_ref):
        i = pl.program_id(0)

        @pl.when(i == 0)
        def _():
            dw_ref[...] = jnp.zeros_like(dw_ref)
            acc_ref[...] = jnp.zeros_like(acc_ref)

        t = i * tm + lax.broadcasted_iota(jnp.int32, (tm + 16, 1), 0)
        for g, wdw in enumerate(WINDOWS):
            cs = slice(g * PG, (g + 1) * PG)
            sc = sc_ref[:, cs]
            dy = dy_ref[:, cs]
            dyh = jnp.where(i < nb - 1, dyh_ref[:, cs], 0.0)
            pb = p_ref[:, cs]
            wg = w_ref[g]
            acc_ref[0:1, cs] += _rowsum(dy * _dot(pb, wg))
            dypre = (dy * sc).astype(BF16)
            dw_ref[g] += _dot_tn(pb, dypre)
            dp = _dot_nt(dypre, wg)
            dph = _dot_nt((dyh * sc).astype(BF16), wg)
            inv = 1.0 / jnp.minimum(t + 1, wdw).astype(F32)
            ext_ref[0:tm, cs] = dp * inv[0:tm]
            ext_ref[tm:tm + 16, cs] = dph * inv[tm:tm + 16]
            acc = ext_ref[0:tm, cs]
            for j in range(1, wdw):
                acc = acc + ext_ref[j:j + tm, cs]
            du_ref[:, cs] = (acc - dp).astype(BF16)

    row = lambda i: (i, 0)
    return pl.pallas_call(
        body, name="pool_bwd", grid=(nb,),
        in_specs=[pl.BlockSpec((tm, DP), row),
                  pl.BlockSpec((16, DP), lambda i: (jnp.minimum((i + 1) * hb, s // 16 - 1), 0)),
                  pl.BlockSpec((tm, DP), row),
                  pl.BlockSpec((4, PG, PG), lambda i: (0, 0, 0)),
                  pl.BlockSpec((1, DP), lambda i: (0, 0)), ANY],
        out_specs=[pl.BlockSpec((tm, DP), row), pl.BlockSpec((4, PG, PG), lambda i: (0, 0, 0)),
                   pl.BlockSpec((8, DP), lambda i: (0, 0))],
        out_shape=[jax.ShapeDtypeStruct((s, NE), BF16), jax.ShapeDtypeStruct((4, PG, PG), F32),
                   jax.ShapeDtypeStruct((8, DP), F32)],
        input_output_aliases={5: 0},
        scratch_shapes=[pltpu.VMEM((tm + 16, DP), F32)],
        compiler_params=_params("arbitrary"),
    )(dyp, dyp, p, pw, pscale, dproj)


def _attn_bwd(proj0, ysb, dys, dproj, comm=None):
    s = proj0.shape[0]
    nq = s // BQ
    kpq = BQ // BK
    nsteps = DP // 128
    comm = comm or _NO_COMM
    nci, nco = len(comm.arrays), len(comm.out_shape)

    def body(*refs):
        q_ref, k_ref, v_ref, o_ref, do_ref = refs[:5]
        dproj_hbm = refs[6 + nci]
        scratch = refs[7 + nci + nco:]
        qn_ref, k8_ref, vb_ref, dob_ref, dka_ref, dva_ref, dq_ref, dk_ref, dv_ref, part_sems = scratch[:10]
        cargs = (refs[6:6 + nci], refs[7 + nci:7 + nci + nco], scratch[10:])
        step = pl.program_id(0)
        if nci:
            pl.when(step == 0)(lambda: comm.start(*cargs))

        qn_ref[...] = (-q_ref[...]).astype(BF16)
        k8_ref[...] = (k_ref[...] * INV_SQRT_HD).astype(BF16)
        vb_ref[...] = v_ref[...].astype(BF16)
        dob_ref[...] = do_ref[...].astype(BF16)
        dka_ref[...] = jnp.zeros_like(dka_ref)
        dva_ref[...] = jnp.zeros_like(dva_ref)
        after, from_on = _tri_masks()
        causal = _causal_mask()
        heads = [slice(HD * h, HD * (h + 1)) for h in range(2)]
        hs = range(len(heads))
        lo, hi = (0, BK), (BK, BK)

        def qstep(qi, carry):
            q0 = pl.multiple_of(qi * BQ, BQ)
            qn = [qn_ref[pl.ds(q0, BQ), ls] for ls in heads]
            do = [dob_ref[pl.ds(q0, BQ), ls] for ls in heads]
            total = [jnp.sum(do[h].astype(F32) * o_ref[pl.ds(q0, BQ), ls], axis=1, keepdims=True)
                     for h, ls in enumerate(heads)]

            def sweep(lanes, state):
                rows_of = lambda i: slice(lanes[i][1][0], lanes[i][1][0] + lanes[i][1][1])
                k8 = [k8_ref[pl.ds(k0, BK), heads[h]] for h, _, k0, _ in lanes]
                nz = [_dot_nt(qn[lane[0]][rows_of(i)], k8[i]) for i, lane in enumerate(lanes)]
                da = [_dot_nt(do[h][rows_of(i)], vb_ref[pl.ds(k0, BK), heads[h]]) for i, (h, _, k0, _) in enumerate(lanes)]
                ll = [_sb_logits(nz[i], lane[3]) for i, lane in enumerate(lanes)]
                aft = [_dot(ll[i][0].astype(BF16), after) for i in range(len(lanes))]
                state = dict(state)
                for i, (h, rows, k0, mask) in enumerate(lanes):
                    dq_acc, c, cg = state[h, rows]
                    a = jnp.exp(ll[i][1] + aft[i] + c)
                    if mask is not None:
                        a = jnp.where(mask, a, 0.0)
                    ab = a.astype(BF16)
                    g = da[i] * ab.astype(F32)
                    suf = _split_dot(g, from_on)
                    dz = g - jnp.exp(ll[i][1]) * (g + ((total[h][rows_of(i)] - cg) - suf))
                    if mask is not None:
                        dz = jnp.where(mask, dz, 0.0)
                    dzb = dz.astype(BF16)
                    dka_ref[pl.ds(k0, BK), heads[h]] += _dot_tn(dzb, qn[h][rows_of(i)])
                    dva_ref[pl.ds(k0, BK), heads[h]] += _dot_tn(ab, do[h][rows_of(i)])
                    state[h, rows] = (dq_acc + _dot(dzb, k8[i]), c + aft[i][:, 0:1] + ll[i][0][:, 0:1],
                                      cg + suf[:, 0:1])
                return state

            k_lo, k_hi = q0, pl.multiple_of(q0 + BK, BK)
            zero = jnp.zeros((BK, 1), F32)
            st = {(h, r): (jnp.zeros((BK, HD), F32), zero, zero) for h in hs for r in (lo, hi)}
            st = sweep([(h, hi, k_hi, causal) for h in hs] + [(h, lo, k_lo, causal) for h in hs]
                       + [(h, hi, k_lo, None) for h in hs], st)
            def left_of(rows):
                def step(j, part):
                    k0 = pl.multiple_of(q0 - (j + 1) * BK, BK)
                    got = sweep([(h, rows, k0, None) for h in hs], {(h, rows): part[h] for h in hs})
                    return tuple(got[h, rows] for h in hs)
                return step

            done = {r: _sweep_left(qi * (BQ // BK), tuple(st[h, r] for h in hs), left_of(r)) for r in (lo, hi)}
            for h, ls in enumerate(heads):
                dq_ref[pl.ds(q0, BQ), ls] = jnp.concatenate([done[lo][h][0], done[hi][h][0]], axis=0).astype(BF16)
            return carry

        lax.fori_loop(0, nq, qstep, 0)
        dk_ref[...] = (dka_ref[...] * (-INV_SQRT_HD)).astype(BF16)
        dv_ref[...] = dva_ref[...].astype(BF16)
        lanes0 = pl.multiple_of(step * 128, 128)
        parts = [pltpu.make_async_copy(src, dproj_hbm.at[:, pl.ds((1 + k) * DP + lanes0, 128)], part_sems.at[k])
                 for k, src in enumerate((dq_ref, dk_ref, dv_ref))]
        for cp in parts:
            cp.start()
        for cp in parts:
            cp.wait()
        if nci:
            pl.when(step == nsteps - 1)(lambda: comm.finish(*cargs))

    col = lambda h: (0, h)
    return pl.pallas_call(
        body, name="attn_bwd", grid=(nsteps,),
        in_specs=[pl.BlockSpec((s, 128), lambda h: (0, 8 + h)),
                  pl.BlockSpec((s, 128), lambda h: (0, 16 + h)),
                  pl.BlockSpec((s, 128), lambda h: (0, 24 + h)),
                  pl.BlockSpec((s, 128), col), pl.BlockSpec((s, 128), col), ANY] + [ANY] * nci,
        out_specs=[ANY] * (1 + nco),
        out_shape=[jax.ShapeDtypeStruct((s, NE), BF16)] + comm.out_shape,
        input_output_aliases={5: 0},
        scratch_shapes=([pltpu.VMEM((s, 128), BF16)] * 4 + [pltpu.VMEM((s, 128), F32)] * 2
                        + [pltpu.VMEM((s, 128), BF16)] * 3 + [pltpu.SemaphoreType.DMA((3,))] + comm.sems),
        compiler_params=_params("arbitrary"),
    )(proj0, proj0, proj0, ysb, dys, dproj, *comm.arrays)


def _adamw_math(w, g, m, v):
    m2 = B1 * m + (1.0 - B1) * g
    v2 = B2 * v + (1.0 - B2) * (g * g)
    m_hat = m2 / (1.0 - B1 ** STEP)
    v_hat = v2 / (1.0 - B2 ** STEP)
    return -LR * (m_hat / (jnp.sqrt(v_hat) + EPS_ADAM) + WD * w), m2, v2


def _adamw(w, g, m, v, name):
    r, c = w.shape
    tr = r
    while tr * c * 4 > (1 << 20) and tr % 16 == 0:
        tr //= 2

    def body(w_ref, g_ref, m_ref, v_ref, d_ref, m2_ref, v2_ref):
        d_ref[...], m2_ref[...], v2_ref[...] = _adamw_math(w_ref[...], g_ref[...], m_ref[...], v_ref[...])

    spec = pl.BlockSpec((tr, c), lambda i: (i, 0))
    return pl.pallas_call(
        body, name=name, grid=(r // tr,),
        in_specs=[spec] * 4, out_specs=[spec] * 3,
        out_shape=[jax.ShapeDtypeStruct((r, c), F32)] * 3,
        compiler_params=_params("parallel"),
    )(w, g, m, v)


def _local_step(x, target, vecs0, vecs1, win0, rest, pscale, cw8, cb, sel=None):
    dist = sel is not None
    h0 = _norm_mod(x, vecs0, "norm0")
    proj0, *got = _inproj(h0, win0, "inproj0", _gather_comm(list(rest[0:2])) if dist else None)
    pw, wout0 = (_group_major(got[0]), got[1].reshape(DI, D)) if dist else rest[0:2]
    p, ypool = _pool_fwd(proj0, pw, pscale)
    ysb, *got = _attn_fwd(proj0, _gather_comm(list(rest[2:4])) if dist else None)
    win1, wout1 = (got[0], got[1].reshape(DI, D)) if dist else rest[2:4]
    x1, out0, yg, h1 = _even_out(ypool, ysb, proj0, wout0, x, vecs0, vecs1)
    proj1, = _inproj(h1, win1, "inproj1")
    dx2, y1, acc_f = _odd_out(proj1, wout1, x1, vecs1, cw8, cb, target)

    def chip_partials(grads, names):
        from_sibling = _send_halves([g16 for _, g16 in grads], "rs_send_halves_" + names[0])
        part = [_add_halves(g32, t, sel[1:2], "rs_add_halves_" + nm)
                for (g32, _), t, nm in zip(grads, from_sibling, names)]
        return [p32 for p32, _ in part], _exchange_comm([p16 for _, p16 in part])

    dout1, dproj1, acc_cv = _odd_bwd(dx2, proj1, wout1, vecs1, cw8, cb)
    g_wout1 = _grad_w_rows(y1, dout1, "grad_wout1")
    g_win1 = _grad_w_cols(h1, dproj1, "grad_win1")
    dx1, acc_n1 = _inproj_bwd(dproj1, win1, x1, dx2, vecs1, "inproj1_bwd")

    dout0, dyp, dys, dproj0, acc_g0 = _even_bwd(dx1, out0, ypool, ysb, proj0, wout0, vecs0)
    g_wout0 = _grad_w_rows(yg, dout0, "grad_wout0")
    dproj0, g_pw, acc_ps = _pool_bwd(dyp, p, pw, pscale, dproj0)
    g_pool = _chip_major(g_pw, PG // NCHIP)
    early = [(g_pool, g_pool.astype(BF16)), g_wout0, g_win1, g_wout1]
    part_a, swap_a = chip_partials(early, ["pool", "wout0", "win1", "wout1"]) if dist else (None, None)
    dproj0, *got_a = _attn_bwd(proj0, ysb, dys, dproj0, swap_a)
    g_win0 = _grad_w_cols(h0, dproj0, "grad_win0")
    part_b, swap_b = chip_partials([g_win0], ["win0"]) if dist else (None, None)
    dx0, acc_n0, *got_b = _inproj_bwd(dproj0, win0, x, dx1, vecs0, "inproj0_bwd", swap_b)

    if dist:
        names = ["win0", "pool", "wout0", "win1", "wout1"]
        halves = [_add_partials(p32, t, sel, "rs_add_partials_" + nm)
                  for p32, t, nm in zip(part_b + part_a, got_b + got_a, names)]
        grads = tuple(_join_halves(halves))
    else:
        grads = (g_win0[0], g_pw, g_wout0[0], g_win1[0], g_wout1[0])

    sums = dict(
        dm0=jnp.concatenate([acc_n0[0:2], acc_g0[0:1]], axis=0),
        dm1=jnp.concatenate([acc_n1[0:2], acc_f[2:3]], axis=0),
        norm_g=jnp.concatenate([acc_n0[2:3], acc_n1[2:3]], axis=0),
        pool_scale=acc_ps[0:1], final_g=acc_f[0:1], loss=acc_f[1:2],
        conv_w=acc_cv[0:3], conv_b=acc_cv[3:4])
    return dx0, grads, sums


ANY = pl.BlockSpec(memory_space=pl.ANY)
CHIP_FLIPS = ((1, 0), (0, 1), (1, 1))


def _place():
    return lax.axis_index("x"), lax.axis_index("y"), lax.axis_index("c")


def _flip(v, f):
    return 1 - v if f else v


def _allgather8(v, name):
    m_per, n = v.shape

    def body(x_ref, out_ref, send_sems, recv_sems, local_sem):
        x, y, c = _place()
        me, sibling = (x, y, c), (x, y, 1 - c)
        chips = [(_flip(x, fx), _flip(y, fy)) for fx, fy in CHIP_FLIPS]

        def rows(px, py, pc):
            return out_ref.at[pl.ds((4 * px + 2 * py + pc) * m_per, m_per), :]

        def copy(k, block, to, src=None):
            return pltpu.make_async_remote_copy(
                src_ref=rows(*block) if src is None else src, dst_ref=rows(*block),
                send_sem=send_sems.at[k], recv_sem=recv_sems.at[k], device_id=to, device_id_type=MESH)

        mine = pltpu.make_async_copy(x_ref, rows(*me), local_sem)
        mine.start()
        first = [copy(0, me, sibling, src=x_ref)]
        first += [copy(1 + j, me, (*chip, c), src=x_ref) for j, chip in enumerate(chips)]
        for cp in first:
            cp.start()
        passed = [copy(4 + j, (*chip, c), sibling) for j, chip in enumerate(chips)]
        for j, chip in enumerate(chips):
            copy(1 + j, (*chip, c), me).wait_recv()
            passed[j].start()
        copy(0, sibling, me).wait_recv()
        for j, chip in enumerate(chips):
            copy(4 + j, (*chip, 1 - c), me).wait_recv()
        for cp in first + passed:
            cp.wait_send()
        mine.wait()

    return pl.pallas_call(
        body, name=name,
        out_shape=jax.ShapeDtypeStruct((NDEV * m_per, n), v.dtype),
        in_specs=[pl.BlockSpec(memory_space=pltpu.VMEM)],
        out_specs=pl.BlockSpec(memory_space=pltpu.VMEM),
        scratch_shapes=[pltpu.SemaphoreType.DMA((7,)), pltpu.SemaphoreType.DMA((7,)), pltpu.SemaphoreType.DMA],
    )(v)


class _Comm:
    def __init__(self, arrays, out_shape, sems, start, finish):
        self.arrays, self.out_shape, self.sems, self.start, self.finish = arrays, out_shape, sems, start, finish


_NO_COMM = _Comm([], [], [], None, None)


def _run_comm(comm, name):
    n = len(comm.arrays)

    def body(*refs):
        args = (refs[:n], refs[n:n + len(comm.out_shape)], refs[n + len(comm.out_shape):])
        comm.start(*args)
        comm.finish(*args)

    return pl.pallas_call(
        body, name=name, out_shape=comm.out_shape,
        in_specs=[ANY] * n, out_specs=[ANY] * len(comm.out_shape), scratch_shapes=comm.sems,
    )(*comm.arrays)


def _gather_comm(shards):
    n = len(shards)

    def pieces(ins, outs, sems, kinds):
        x, y, c = _place()
        ici_send, ici_recv, fwd_send, fwd_recv = sems[:4]
        own, sibling = 2 * x + y, (x, y, 1 - c)
        made = {kind: [] for kind in kinds}
        for w in range(n):
            r2 = ins[w].shape[0] // 2
            mine, other = pl.ds(c * r2, r2), pl.ds((1 - c) * r2, r2)
            for d, (fx, fy) in enumerate(CHIP_FLIPS):
                px, py, k = _flip(x, fx), _flip(y, fy), 3 * w + d
                peer = 2 * px + py
                ici = dict(send_sem=ici_send.at[k], recv_sem=ici_recv.at[k], device_id=(px, py, c), device_id_type=MESH)
                fwd = dict(send_sem=fwd_send.at[k], recv_sem=fwd_recv.at[k], device_id=sibling, device_id_type=MESH)
                if "ici_out" in kinds:
                    made["ici_out"].append(pltpu.make_async_remote_copy(
                        src_ref=ins[w].at[mine, :], dst_ref=outs[w].at[own, mine, :], **ici))
                if "ici_in" in kinds:
                    made["ici_in"].append(pltpu.make_async_remote_copy(
                        src_ref=ins[w].at[mine, :], dst_ref=outs[w].at[peer, mine, :], **ici))
                if "fwd_out" in kinds:
                    made["fwd_out"].append(pltpu.make_async_remote_copy(
                        src_ref=outs[w].at[peer, mine, :], dst_ref=outs[w].at[peer, mine, :], **fwd))
                if "fwd_in" in kinds:
                    made["fwd_in"].append(pltpu.make_async_remote_copy(
                        src_ref=outs[w].at[peer, other, :], dst_ref=outs[w].at[peer, other, :], **fwd))
        return made

    def stage_rows(a):
        return min(a.shape[0], 256)

    def start(ins, outs, sems):
        for cp in pieces(ins, outs, sems, ("ici_out",))["ici_out"]:
            cp.start()
        x, y, _ = _place()
        for w, stage in enumerate(sems[4:]):
            rows = stage.shape[0]
            for r in range(0, ins[w].shape[0], rows):
                pltpu.sync_copy(ins[w].at[pl.ds(r, rows), :], stage)
                pltpu.sync_copy(stage, outs[w].at[2 * x + y, pl.ds(r, rows), :])

    def finish(ins, outs, sems):
        made = pieces(ins, outs, sems, ("ici_out", "ici_in", "fwd_out", "fwd_in"))
        for arrived, onward in zip(made["ici_in"], made["fwd_out"]):
            arrived.wait_recv()
            onward.start()
        for cp in made["fwd_in"]:
            cp.wait_recv()
        for cp in made["ici_out"] + made["fwd_out"]:
            cp.wait_send()

    dma = pltpu.SemaphoreType.DMA
    return _Comm(list(shards), [jax.ShapeDtypeStruct((NCHIP,) + a.shape, a.dtype) for a in shards],
                 [dma((3 * n,))] * 4 + [pltpu.VMEM((stage_rows(a), a.shape[1]), a.dtype) for a in shards],
                 start, finish)


def _exchange_comm(parts):
    n = len(parts)

    def copies(ins, outs, sems):
        x, y, c = _place()
        send_sems, recv_sems = sems
        out = []
        for w in range(n):
            for d, (fx, fy) in enumerate(CHIP_FLIPS):
                px, py = _flip(x, fx), _flip(y, fy)
                out.append(pltpu.make_async_remote_copy(
                    src_ref=ins[w].at[2 * px + py], dst_ref=outs[w].at[d], send_sem=send_sems.at[3 * w + d],
                    recv_sem=recv_sems.at[3 * w + d], device_id=(px, py, c), device_id_type=MESH))
        return out

    def start(ins, outs, sems):
        for cp in copies(ins, outs, sems):
            cp.start()

    def finish(ins, outs, sems):
        cps = copies(ins, outs, sems)
        for cp in cps:
            cp.wait_recv()
        for cp in cps:
            cp.wait_send()

    dma = pltpu.SemaphoreType.DMA
    return _Comm(list(parts), [jax.ShapeDtypeStruct((3,) + p.shape[1:], BF16) for p in parts],
                 [dma((3 * n,))] * 2, start, finish)


def _send_halves(grads, name):
    n = len(grads)

    def body(*refs):
        ins, outs = refs[:n], refs[n:2 * n]
        send_sems, recv_sems = refs[2 * n:]
        x, y, c = _place()
        copies = []
        for w in range(n):
            r2 = ins[w].shape[1] // 2
            cp = pltpu.make_async_remote_copy(
                src_ref=ins[w].at[:, pl.ds((1 - c) * r2, r2), :], dst_ref=outs[w],
                send_sem=send_sems.at[w], recv_sem=recv_sems.at[w], device_id=(x, y, 1 - c), device_id_type=MESH)
            cp.start()
            copies.append(cp)
        for cp in copies:
            cp.wait_recv()
        for cp in copies:
            cp.wait_send()

    return pl.pallas_call(
        body, name=name,
        out_shape=[jax.ShapeDtypeStruct((NCHIP, g.shape[1] // 2, g.shape[2]), g.dtype) for g in grads],
        in_specs=[ANY] * n, out_specs=[ANY] * n,
        scratch_shapes=[pltpu.SemaphoreType.DMA((n,)), pltpu.SemaphoreType.DMA((n,))],
    )(*grads)


def _row_tile(rows, cols):
    tr = rows
    while tr * cols * 4 > (1 << 20) and tr % 16 == 0:
        tr //= 2
    return tr


def _add_halves(g, t1, core, name):
    _, r, cdim = g.shape
    r2 = r // 2
    tr = _row_tile(r2, cdim)
    nt = r2 // tr

    def body(core_ref, g_ref, t_ref, p_ref, pb_ref):
        p = g_ref[...] + t_ref[...].astype(F32)
        p_ref[...] = p
        pb_ref[...] = p.astype(BF16)

    blk = pl.BlockSpec((None, tr, cdim), lambda j, i, core_ref: (j, i, 0))
    return pl.pallas_call(
        body, name=name,
        grid_spec=pltpu.PrefetchScalarGridSpec(
            num_scalar_prefetch=1, grid=(NCHIP, nt),
            in_specs=[pl.BlockSpec((None, tr, cdim), lambda j, i, core_ref: (j, core_ref[0] * nt + i, 0)), blk],
            out_specs=[blk, blk]),
        out_shape=[jax.ShapeDtypeStruct((NCHIP, r2, cdim), F32), jax.ShapeDtypeStruct((NCHIP, r2, cdim), BF16)],
        compiler_params=_params("parallel", "parallel"),
    )(core, g, t1)


def _add_partials(p, t2, sel, name):
    _, r2, cdim = p.shape
    tr = _row_tile(r2, cdim)
    nt = r2 // tr

    def body(sel_ref, p_ref, t_ref, o_ref):
        o_ref[...] = ((p_ref[...] + t_ref[0].astype(F32)) + t_ref[1].astype(F32)) + t_ref[2].astype(F32)

    return pl.pallas_call(
        body, name=name,
        grid_spec=pltpu.PrefetchScalarGridSpec(
            num_scalar_prefetch=1, grid=(nt,),
            in_specs=[pl.BlockSpec((None, tr, cdim), lambda i, sel_ref: (sel_ref[0], i, 0)),
                      pl.BlockSpec((3, tr, cdim), lambda i, sel_ref: (0, i, 0))],
            out_specs=pl.BlockSpec((tr, cdim), lambda i, sel_ref: (sel_ref[1] * nt + i, 0))),
        out_shape=jax.ShapeDtypeStruct((2 * r2, cdim), F32),
        compiler_params=_params("parallel"),
    )(sel, p, t2)


def _join_halves(grads):
    n = len(grads)

    def body(*refs):
        bufs = refs[n:2 * n]
        send_sems, recv_sems = refs[2 * n:]
        x, y, c = _place()
        copies = []
        for w in range(n):
            r2 = bufs[w].shape[0] // 2
            mine = bufs[w].at[pl.ds(c * r2, r2), :]
            cp = pltpu.make_async_remote_copy(
                src_ref=mine, dst_ref=mine, send_sem=send_sems.at[w], recv_sem=recv_sems.at[w],
                device_id=(x, y, 1 - c), device_id_type=MESH)
            cp.start()
            copies.append(cp)
        for w in range(n):
            r2 = bufs[w].shape[0] // 2
            theirs = bufs[w].at[pl.ds((1 - c) * r2, r2), :]
            pltpu.make_async_remote_copy(
                src_ref=theirs, dst_ref=theirs, send_sem=send_sems.at[w], recv_sem=recv_sems.at[w],
                device_id=(x, y, 1 - c), device_id_type=MESH).wait_recv()
        for cp in copies:
            cp.wait_send()

    return pl.pallas_call(
        body, name="rs_join_halves",
        out_shape=[jax.ShapeDtypeStruct(g.shape, F32) for g in grads],
        in_specs=[ANY] * n, out_specs=[ANY] * n, input_output_aliases={w: w for w in range(n)},
        scratch_shapes=[pltpu.SemaphoreType.DMA((n,)), pltpu.SemaphoreType.DMA((n,))],
    )(*grads)


def _ada_fwd(c_all, ada_w):
    nl, _, ns = ada_w.shape

    def body(c_ref, w_ref, o_ref):
        cv = c_ref[...]
        o_ref[...] = _dot((cv * _sigmoid(cv)).astype(BF16), w_ref[...].astype(BF16))

    return pl.pallas_call(
        body, name="ada_fwd", grid=(nl,),
        in_specs=[pl.BlockSpec((NDEV, D), lambda i: (0, 0)), pl.BlockSpec((None, D, ns), lambda i: (i, 0, 0))],
        out_specs=pl.BlockSpec((None, NDEV, ns), lambda i: (i, 0, 0)),
        out_shape=jax.ShapeDtypeStruct((nl, NDEV, ns), F32),
        compiler_params=_params("parallel"),
    )(c_all, ada_w)


PACK_ROWS = 24


def _reduce_packed(gathered):
    def body(g_ref, tot_ref, loss_ref):
        tot = g_ref[0:PACK_ROWS, :]
        for dev in range(1, NDEV):
            tot = tot + g_ref[dev * PACK_ROWS:(dev + 1) * PACK_ROWS, :]
        tot_ref[...] = tot
        loss_ref[...] = jnp.zeros((8, 128), F32) + jnp.sum(tot[10:11, :])

    return pl.pallas_call(
        body, name="reduce_packed",
        out_shape=[jax.ShapeDtypeStruct((PACK_ROWS, D), F32), jax.ShapeDtypeStruct((8, 128), F32)],
    )(gathered)


def _ada_w_update(c_t, dms, w, m, v):
    nl, _, ns = w.shape
    tr = 256

    def body(ct_ref, dm_ref, w_ref, m_ref, v_ref, g_ref, d_ref, m2_ref, v2_ref):
        ct = ct_ref[...]
        sc = ct * _sigmoid(ct)
        dm = dm_ref[...]
        g = sc[:, 0:1] * dm[0:1, :]
        for b in range(1, NDEV):
            g = g + sc[:, b:b + 1] * dm[b:b + 1, :]
        g_ref[...] = g
        d_ref[...], m2_ref[...], v2_ref[...] = _adamw_math(w_ref[...], g, m_ref[...], v_ref[...])

    blk = pl.BlockSpec((None, tr, ns), lambda i, j: (i, j, 0))
    return pl.pallas_call(
        body, name="ada_w_update", grid=(nl, D // tr),
        in_specs=[pl.BlockSpec((tr, NDEV), lambda i, j: (j, 0)),
                  pl.BlockSpec((None, NDEV, ns), lambda i, j: (i, 0, 0)), blk, blk, blk],
        out_specs=[blk] * 4,
        out_shape=[jax.ShapeDtypeStruct((nl, D, ns), F32)] * 4,
        compiler_params=_params("parallel", "parallel"),
    )(c_t, dms, w, m, v)


def _chip_major(a, parts):
    g, _, cdim = a.shape
    return jnp.transpose(a.reshape(g, NCHIP, parts, cdim), (1, 0, 2, 3)).reshape(NCHIP, g * parts, cdim)


def _group_major(a):
    return jnp.transpose(a.reshape(NCHIP, 4, PG // NCHIP, PG), (1, 0, 2, 3)).reshape(4, PG, PG)


def kernel(x, c, norm_g, ada_w, ada_b, even_w_in, pool_w, pool_scale, even_w_out, odd_w_in, conv_w, conv_b, odd_w_out, final_g, loss_target, m_norm_g, m_ada_w, m_ada_b, m_even_w_in, m_pool_w, m_pool_scale, m_even_w_out, m_odd_w_in, m_conv_w, m_conv_b, m_odd_w_out, m_final_g, v_norm_g, v_ada_w, v_ada_b, v_even_w_in, v_pool_w, v_pool_scale, v_even_w_out, v_odd_w_in, v_conv_w, v_conv_b, v_odd_w_out, v_final_g):
    ix, iy, ic = _place()
    chip = 2 * ix + iy
    batch = 2 * chip + ic
    sel = jnp.stack([chip, ic]).astype(jnp.int32)
    ns_ada = ada_w.shape[2]
    ns_conv = conv_b.shape[1]

    conv_rows = jnp.pad(jnp.concatenate([conv_w[0], conv_b], axis=0), ((0, 3), (0, D - ns_conv)))
    first = _allgather8(jnp.concatenate([c, conv_rows], axis=0), "gather_c_conv").reshape(NCHIP, 2, 8, D)
    c_all = first[:, :, 0].reshape(NDEV, D)
    cw_full = jnp.transpose(first[:, 0, 1:5, 0:ns_conv], (1, 0, 2)).reshape(4, DI)
    cw8 = jnp.concatenate([cw_full[0:3], jnp.zeros((5, DI), F32)], axis=0)
    cb_full = cw_full[3:4]

    m_cols = _allgather8(_ada_fwd(c_all, ada_w).reshape(2 * NDEV, ns_ada), "gather_ada")
    m_cols = m_cols.reshape(NCHIP, 2, 2, NDEV, ns_ada)[:, 0]
    m_mine = lax.dynamic_index_in_dim(m_cols, batch, axis=2, keepdims=False)
    m_mine = jnp.transpose(m_mine, (1, 0, 2)).reshape(2, 3 * D) + ada_b
    zrow = jnp.zeros((3, D), F32)

    def vec_rows(i):
        sh, sc, gt = m_mine[i, 0:D], m_mine[i, D:2 * D], m_mine[i, 2 * D:3 * D]
        return jnp.concatenate([jnp.stack([norm_g[i], sc, sh, gt, final_g]), zrow], axis=0)

    win0, = _run_comm(_gather_comm([even_w_in[0].astype(BF16)]), "gather_win0")
    shards = (pool_w[0].astype(BF16).reshape(PG, PG), even_w_out[0].astype(BF16),
              odd_w_in[0].astype(BF16), odd_w_out[0].astype(BF16))
    dx0, grads, sums = _local_step(
        x[0], loss_target[0], vec_rows(0), vec_rows(1), win0, shards, pool_scale, cw8, cb_full, sel)
    r_win0, r_pw, r_wout0, r_win1, r_wout1 = grads

    packed = jnp.concatenate([
        sums["dm0"], sums["dm1"], sums["norm_g"], sums["pool_scale"], sums["final_g"], sums["loss"],
        sums["conv_w"].reshape(6, D), sums["conv_b"].reshape(2, D), jnp.zeros((PACK_ROWS - 19, D), F32)], axis=0)
    gathered = _allgather8(packed, "gather_sums")
    tot, loss8 = _reduce_packed(gathered)
    loss = loss8[0, 0]
    g_norm_g, g_pool_scale, g_final_g = tot[6:8], tot[8:9], tot[9]
    g_ada_b = tot[0:6].reshape(2, 3 * D)
    g_conv_w = lax.dynamic_slice_in_dim(tot[11:17].reshape(3, DI), chip * (DI // NCHIP), DI // NCHIP, axis=1)
    g_conv_b = lax.dynamic_slice_in_dim(tot[17:19].reshape(1, DI), chip * (DI // NCHIP), DI // NCHIP, axis=1)
    dm_all = gathered.reshape(NDEV, PACK_ROWS, D)[:, 0:6].reshape(NDEV, 2, 3 * D)
    dm_cols = jnp.transpose(lax.dynamic_slice_in_dim(dm_all, chip * ns_ada, ns_ada, axis=2), (1, 0, 2))
    g_ada_w, d_ada_w, nm_ada_w, nv_ada_w = _ada_w_update(jnp.transpose(c_all), dm_cols, ada_w, m_ada_w, v_ada_w)

    def upd(w, g, m, v, name):
        shape = w.shape
        w2, m2, v2 = (a.reshape(g.shape) for a in (w, m, v))
        d, nm, nv = _adamw(w2, g, m2, v2, name)
        return g.reshape(shape), d.reshape(shape), nm.reshape(shape), nv.reshape(shape)

    o_win0 = upd(even_w_in, r_win0, m_even_w_in, v_even_w_in, "adamw_win0")
    o_pw = upd(pool_w, r_pw, m_pool_w, v_pool_w, "adamw_pool")
    o_wout0 = upd(even_w_out, r_wout0, m_even_w_out, v_even_w_out, "adamw_wout0")
    o_win1 = upd(odd_w_in, r_win1, m_odd_w_in, v_odd_w_in, "adamw_win1")
    o_wout1 = upd(odd_w_out, r_wout1, m_odd_w_out, v_odd_w_out, "adamw_wout1")

    def pack_small(ng, ab, ps, fg, cwv, cbv):
        conv = jnp.concatenate([cwv.reshape(3, -1), cbv.reshape(1, -1)], axis=0).reshape(2, D)
        return jnp.concatenate([ng, ab.reshape(6, D), ps, fg.reshape(1, D), conv, jnp.zeros((4, D), F32)], axis=0)

    sw = pack_small(norm_g, ada_b, pool_scale, final_g, conv_w, conv_b)
    sg = pack_small(g_norm_g, g_ada_b, g_pool_scale, g_final_g, g_conv_w, g_conv_b)
    sm = pack_small(m_norm_g, m_ada_b, m_pool_scale, m_final_g, m_conv_w, m_conv_b)
    sv = pack_small(v_norm_g, v_ada_b, v_pool_scale, v_final_g, v_conv_w, v_conv_b) + jnp.concatenate(
        [jnp.zeros((12, D), F32), jnp.ones((4, D), F32)], axis=0)
    small = _adamw(sw, sg, sm, sv, "adamw_small")

    def unpack_small(a):
        conv = a[10:12].reshape(4, -1)
        return dict(norm_g=a[0:2], ada_b=a[2:8].reshape(2, 3 * D), pool_scale=a[8:9], final_g=a[9],
                    conv_w=conv[0:3].reshape(conv_w.shape), conv_b=conv[3:4].reshape(conv_b.shape))

    s_grad = dict(norm_g=g_norm_g, ada_b=g_ada_b, pool_scale=g_pool_scale, final_g=g_final_g,
                  conv_w=g_conv_w.reshape(conv_w.shape), conv_b=g_conv_b.reshape(conv_b.shape))
    s_out = [s_grad] + [unpack_small(a) for a in small]

    outs = []
    for k in range(4):
        sm_k = s_out[k]
        outs.append([sm_k["norm_g"], (g_ada_w, d_ada_w, nm_ada_w, nv_ada_w)[k], sm_k["ada_b"], o_win0[k], o_pw[k],
                     sm_k["pool_scale"], o_wout0[k], o_win1[k], sm_k["conv_w"], sm_k["conv_b"], o_wout1[k],
                     sm_k["final_g"]])
    return (loss, dx0[None], *outs[0], *outs[1], *outs[2], *outs[3])
```

```python
import functools

import jax
import jax.numpy as jnp
from jax import lax
from jax.experimental import pallas as pl
from jax.experimental.pallas import tpu as pltpu

F32 = jnp.float32
BF16 = jnp.bfloat16
MESH = pl.DeviceIdType.MESH

D = 1024
DI = 2048
DP = 1024
NE = 6144
NO = 8192
WINDOWS = (2, 4, 8, 16)
PG = 256
HD = 64
NCHIP = 4
NDEV = 8
EPS = 1e-6
INV_SQRT_HD = 0.125

LR, B1, B2, EPS_ADAM, WD, STEP = 0.001, 0.9, 0.999, 1e-08, 0.01, 10

TM = 512
TME = 256
CT = 512
BQ = 512
BK = 256
assert BQ == 2 * BK
HALO = 16
DEAD_LOG_WEIGHT = -104.0
VMEM_LIMIT = 56 * 1024 * 1024


def _dot(a, b):
    return jnp.dot(a, b, preferred_element_type=F32)


def _dot_nt(a, b):
    return lax.dot_general(a, b, (((1,), (1,)), ((), ())), preferred_element_type=F32)


def _dot_tn(a, b):
    return lax.dot_general(a, b, (((0,), (0,)), ((), ())), preferred_element_type=F32)


def _params(*sem):
    return pltpu.CompilerParams(dimension_semantics=sem, vmem_limit_bytes=VMEM_LIMIT)


def _sigmoid(v):
    return 0.5 * jnp.tanh(0.5 * v) + 0.5


def _rowsum(v):
    return jnp.sum(v, axis=0, keepdims=True)


def _modulated_norm(xv, vec_ref):
    r = lax.rsqrt(jnp.mean(xv * xv, axis=-1, keepdims=True) + EPS)
    return (((xv * r) * vec_ref[0:1, :]) * (1.0 + vec_ref[1:2, :]) + vec_ref[2:3, :]).astype(BF16)


def _norm_mod(x, vecs, name):
    s = x.shape[0]
    tm = min(TM, s)

    def body(x_ref, vec_ref, h_ref):
        h_ref[...] = _modulated_norm(x_ref[...], vec_ref)

    return pl.pallas_call(
        body, name=name, grid=(s // tm,),
        in_specs=[pl.BlockSpec((tm, D), lambda i: (i, 0)), pl.BlockSpec((8, D), lambda i: (0, 0))],
        out_specs=pl.BlockSpec((tm, D), lambda i: (i, 0)),
        out_shape=jax.ShapeDtypeStruct((s, D), BF16),
        compiler_params=_params("parallel"),
    )(x, vecs)


def _inproj(h, w, name, comm=None):
    s = h.shape[0]
    ns = w.shape[2]
    tm = min(TM, s)
    ni = s // tm
    comm = comm or _NO_COMM
    nci, nco = len(comm.arrays), len(comm.out_shape)

    def body(*refs):
        h_ref, w_ref = refs[:2]
        proj_ref = refs[2 + nci]
        cargs = (refs[2:2 + nci], refs[3 + nci:3 + nci + nco], refs[3 + nci + nco:])
        j, i = pl.program_id(0), pl.program_id(1)
        if nci:
            pl.when((j == 0) & (i == 0))(lambda: comm.start(*cargs))
        proj_ref[...] = _dot(h_ref[...], w_ref[...]).astype(BF16)
        if nci:
            pl.when((j == NCHIP - 1) & (i == ni - 1))(lambda: comm.finish(*cargs))

    return pl.pallas_call(
        body, name=name, grid=(NCHIP, ni),
        in_specs=[pl.BlockSpec((tm, D), lambda j, i: (i, 0)),
                  pl.BlockSpec((None, D, ns), lambda j, i: (j, 0, 0))] + [ANY] * nci,
        out_specs=[pl.BlockSpec((tm, ns), lambda j, i: (i, j))] + [ANY] * nco,
        out_shape=[jax.ShapeDtypeStruct((s, NCHIP * ns), BF16)] + comm.out_shape,
        scratch_shapes=comm.sems,
        compiler_params=_params("arbitrary", "arbitrary"),
    )(h, w, *comm.arrays)


def _inproj_gathering(h, w_shard, order, name, comm):
    s = h.shape[0]
    ns = w_shard.shape[1]
    r2 = D // 2
    tm = min(TM, s)
    ni = s // tm
    nci, nco = len(comm.arrays), len(comm.out_shape)

    def body(*refs):
        order_ref, h_ref, shard_hbm = refs[:3]
        proj_ref, full_hbm = refs[3 + nci:5 + nci]
        w_ref, ici_send, ici_recv, fwd_send, fwd_recv = refs[5 + nci + nco:10 + nci + nco]
        cargs = (refs[3:3 + nci], refs[5 + nci:5 + nci + nco], refs[10 + nci + nco:])
        j, i = pl.program_id(0), pl.program_id(1)
        x, y, c = _place()
        own, sibling = 2 * x + y, (x, y, 1 - c)
        mine, other = pl.ds(c * r2, r2), pl.ds((1 - c) * r2, r2)

        def copies(d, kinds):
            px, py = _flip(x, CHIP_FLIPS[d][0]), _flip(y, CHIP_FLIPS[d][1])
            peer = 2 * px + py
            ici = dict(send_sem=ici_send.at[d], recv_sem=ici_recv.at[d], device_id=(px, py, c), device_id_type=MESH)
            fwd = dict(send_sem=fwd_send.at[d], recv_sem=fwd_recv.at[d], device_id=sibling, device_id_type=MESH)
            made = {}
            if "ici_out" in kinds:
                made["ici_out"] = pltpu.make_async_remote_copy(
                    src_ref=shard_hbm.at[mine, :], dst_ref=full_hbm.at[own, mine, :], **ici)
            if "ici_in" in kinds:
                made["ici_in"] = pltpu.make_async_remote_copy(
                    src_ref=shard_hbm.at[mine, :], dst_ref=full_hbm.at[peer, mine, :], **ici)
            if "fwd_out" in kinds:
                made["fwd_out"] = pltpu.make_async_remote_copy(
                    src_ref=full_hbm.at[peer, mine, :], dst_ref=full_hbm.at[peer, mine, :], **fwd)
            if "fwd_in" in kinds:
                made["fwd_in"] = pltpu.make_async_remote_copy(
                    src_ref=full_hbm.at[peer, other, :], dst_ref=full_hbm.at[peer, other, :], **fwd)
            return made, peer

        @pl.when((j == 0) & (i == 0))
        def _():
            comm.start(*cargs)
            for d in range(len(CHIP_FLIPS)):
                copies(d, ("ici_out",))[0]["ici_out"].start()
            pltpu.sync_copy(shard_hbm, w_ref)
            pltpu.sync_copy(w_ref, full_hbm.at[own])

        for d in range(len(CHIP_FLIPS)):
            @pl.when((j == d + 1) & (i == 0))
            def _(d=d):
                made, peer = copies(d, ("ici_in", "fwd_out", "fwd_in"))
                made["ici_in"].wait_recv()
                made["fwd_out"].start()
                made["fwd_in"].wait_recv()
                pltpu.sync_copy(full_hbm.at[peer], w_ref)

        proj_ref[...] = _dot(h_ref[...], w_ref[...]).astype(BF16)

        @pl.when((j == NCHIP - 1) & (i == ni - 1))
        def _():
            for d in range(len(CHIP_FLIPS)):
                made, _ = copies(d, ("ici_out", "fwd_out"))
                made["ici_out"].wait_send()
                made["fwd_out"].wait_send()
            comm.finish(*cargs)

    dma = pltpu.SemaphoreType.DMA
    return pl.pallas_call(
        body, name=name,
        grid_spec=pltpu.PrefetchScalarGridSpec(
            num_scalar_prefetch=1, grid=(NCHIP, ni),
            in_specs=[pl.BlockSpec((tm, D), lambda j, i, order_ref: (i, 0)), ANY] + [ANY] * nci,
            out_specs=[pl.BlockSpec((tm, ns), lambda j, i, order_ref: (i, order_ref[j])), ANY] + [ANY] * nco,
            scratch_shapes=[pltpu.VMEM((D, ns), BF16)] + [dma((3,))] * 4 + comm.sems),
        out_shape=[jax.ShapeDtypeStruct((s, NCHIP * ns), BF16),
                   jax.ShapeDtypeStruct((NCHIP, D, ns), BF16)] + comm.out_shape,
        compiler_params=_params("arbitrary", "arbitrary"),
    )(order, h, w_shard, *comm.arrays)


def _pool_fwd(proj0, pw, pscale):
    s = proj0.shape[0]
    tm = min(TM, s)
    hb = tm // 16

    def body(u_ref, halo_ref, w_ref, sc_ref, p_ref, y_ref, ext_ref):
        i = pl.program_id(0)
        ext_ref[16:, :] = u_ref[...].astype(F32)
        ext_ref[0:16, :] = jnp.where(i > 0, halo_ref[...].astype(F32), 0.0)
        t = i * tm + lax.broadcasted_iota(jnp.int32, (tm, 1), 0)
        for g, wdw in enumerate(WINDOWS):
            cs = slice(g * PG, (g + 1) * PG)
            u = ext_ref[16:16 + tm, cs]
            acc = u
            for j in range(1, wdw):
                acc = acc + ext_ref[16 - j:16 - j + tm, cs]
            inv = 1.0 / jnp.minimum(t + 1, wdw).astype(F32)
            pb = (acc * inv - u).astype(BF16)
            p_ref[:, cs] = pb
            y_ref[:, cs] = _dot(pb, w_ref[g]) * sc_ref[:, cs]

    return pl.pallas_call(
        body, name="pool_fwd", grid=(s // tm,),
        in_specs=[pl.BlockSpec((tm, DP), lambda i: (i, 0)),
                  pl.BlockSpec((16, DP), lambda i: (jnp.maximum(i * hb - 1, 0), 0)),
                  pl.BlockSpec((4, PG, PG), lambda i: (0, 0, 0)),
                  pl.BlockSpec((1, DP), lambda i: (0, 0))],
        out_specs=[pl.BlockSpec((tm, DP), lambda i: (i, 0)),
                   pl.BlockSpec((tm, DP), lambda i: (i, 0))],
        out_shape=[jax.ShapeDtypeStruct((s, DP), BF16), jax.ShapeDtypeStruct((s, DP), F32)],
        scratch_shapes=[pltpu.VMEM((tm + 16, DP), F32)],
        compiler_params=_params("parallel"),
    )(proj0, proj0, pw, pscale)


def _sb_logits(nz, mask):
    neg_abs = lax.bitcast_convert_type(lax.bitcast_convert_type(nz, jnp.uint32) | jnp.uint32(0x80000000), F32)
    t = jnp.log(1.0 + jnp.exp(neg_abs))
    lf = jnp.minimum(nz, 0.0) - t
    lam = lf - nz
    if mask is not None:
        lf = jnp.where(mask, lf, 0.0)
    return lf, lam


def _sweep_left(steps, state, step):
    def live(carry):
        j, st = carry
        heaviest = functools.reduce(jnp.maximum, [jnp.max(head[1]) for head in st])
        return (j < steps) & (heaviest > DEAD_LOG_WEIGHT)

    return lax.while_loop(live, lambda carry: (carry[0] + 1, step(carry[0], carry[1])), (0, state))[1]


def _split_dot(v, tri):
    hi = v.astype(BF16)
    lo = (v - hi.astype(F32)).astype(BF16)
    return _dot(hi, tri) + _dot(lo, tri)


def _tri_masks():
    row = lax.broadcasted_iota(jnp.int32, (BK, BK), 0)
    col = lax.broadcasted_iota(jnp.int32, (BK, BK), 1)
    return (row > col).astype(BF16), (row >= col).astype(BF16)


def _causal_mask():
    row = lax.broadcasted_iota(jnp.int32, (BK, BK), 0)
    col = lax.broadcasted_iota(jnp.int32, (BK, BK), 1)
    return col < row


def _attn_fwd(proj0, comm=None):
    s = proj0.shape[0]
    nq = s // BQ
    kpq = BQ // BK
    nsteps = DP // 128
    comm = comm or _NO_COMM
    nci, nco = len(comm.arrays), len(comm.out_shape)

    def body(*refs):
        q_ref, k_ref, v_ref = refs[:3]
        o_ref = refs[3 + nci]
        qn_ref, k8_ref, vb_ref = refs[4 + nci + nco:7 + nci + nco]
        cargs = (refs[3:3 + nci], refs[4 + nci:4 + nci + nco], refs[7 + nci + nco:])
        if nci:
            pl.when(pl.program_id(0) == 0)(lambda: comm.start(*cargs))
        qn_ref[...] = (-q_ref[...]).astype(BF16)
        k8_ref[...] = (k_ref[...] * INV_SQRT_HD).astype(BF16)
        vb_ref[...] = v_ref[...].astype(BF16)
        after, _ = _tri_masks()
        causal = _causal_mask()
        heads = [slice(HD * h, HD * (h + 1)) for h in range(2)]
        hs = range(len(heads))
        lo, hi = (0, BK), (BK, BK)

        def qstep(qi, carry):
            q0 = pl.multiple_of(qi * BQ, BQ)
            qn = [qn_ref[pl.ds(q0, BQ), ls] for ls in heads]

            def sweep(lanes, state):
                nz = [_dot_nt(qn[h][r0:r0 + n], k8_ref[pl.ds(k0, BK), heads[h]]) for h, (r0, n), k0, _ in lanes]
                ll = [_sb_logits(nz[i], lane[3]) for i, lane in enumerate(lanes)]
                aft = [_dot(ll[i][0].astype(BF16), after) for i in range(len(lanes))]
                state = dict(state)
                for i, (h, rows, k0, mask) in enumerate(lanes):
                    o_acc, c = state[h, rows]
                    a = jnp.exp(ll[i][1] + aft[i] + c)
                    if mask is not None:
                        a = jnp.where(mask, a, 0.0)
                    o_acc = o_acc + _dot(a.astype(BF16), vb_ref[pl.ds(k0, BK), heads[h]])
                    state[h, rows] = (o_acc, c + aft[i][:, 0:1] + ll[i][0][:, 0:1])
                return state

            k_lo, k_hi = q0, pl.multiple_of(q0 + BK, BK)
            st = {(h, r): (jnp.zeros((BK, HD), F32), jnp.zeros((BK, 1), F32)) for h in hs for r in (lo, hi)}
            st = sweep([(h, hi, k_hi, causal) for h in hs] + [(h, lo, k_lo, causal) for h in hs]
                       + [(h, hi, k_lo, None) for h in hs], st)
            def left_of(rows):
                def step(j, part):
                    k0 = pl.multiple_of(q0 - (j + 1) * BK, BK)
                    got = sweep([(h, rows, k0, None) for h in hs], {(h, rows): part[h] for h in hs})
                    return tuple(got[h, rows] for h in hs)
                return step

            done = {r: _sweep_left(qi * (BQ // BK), tuple(st[h, r] for h in hs), left_of(r)) for r in (lo, hi)}
            for h, ls in enumerate(heads):
                o_ref[pl.ds(q0, BQ), ls] = jnp.concatenate([done[lo][h][0], done[hi][h][0]], axis=0)
            return carry

        lax.fori_loop(0, nq, qstep, 0)
        if nci:
            pl.when(pl.program_id(0) == nsteps - 1)(lambda: comm.finish(*cargs))

    return pl.pallas_call(
        body, name="attn_fwd", grid=(nsteps,),
        in_specs=[pl.BlockSpec((s, 128), lambda h: (0, 8 + h)),
                  pl.BlockSpec((s, 128), lambda h: (0, 16 + h)),
                  pl.BlockSpec((s, 128), lambda h: (0, 24 + h))] + [ANY] * nci,
        out_specs=[pl.BlockSpec((s, 128), lambda h: (0, h))] + [ANY] * nco,
        out_shape=[jax.ShapeDtypeStruct((s, DP), F32)] + comm.out_shape,
        scratch_shapes=[pltpu.VMEM((s, 128), BF16)] * 3 + comm.sems,
        compiler_params=_params("arbitrary"),
    )(proj0, proj0, proj0, *comm.arrays)


def _even_out(ypool, ysb, proj0, wout, x, vecs, vecs_next):
    s = x.shape[0]
    tm = min(TME, s)

    def body(yp_ref, ys_ref, gate_ref, w_ref, x_ref, vec_ref, vecn_ref, x1_ref, out_ref, yg_ref, hn_ref):
        gt = gate_ref[...].astype(F32)
        sl = gt * _sigmoid(gt)
        yg_ref[:, :DP] = (yp_ref[...] * sl[:, :DP]).astype(BF16)
        yg_ref[:, DP:] = (ys_ref[...] * sl[:, DP:]).astype(BF16)
        out = _dot(yg_ref[...], w_ref[...])
        out_ref[...] = out
        x1 = x_ref[...] + (1.0 + vec_ref[3:4, :]) * out
        x1_ref[...] = x1
        hn_ref[...] = _modulated_norm(x1, vecn_ref)

    row = lambda i: (i, 0)
    const = lambda i: (0, 0)
    return pl.pallas_call(
        body, name="even_out", grid=(s // tm,),
        in_specs=[pl.BlockSpec((tm, DP), row), pl.BlockSpec((tm, DP), row),
                  pl.BlockSpec((tm, DI), lambda i: (i, 2)),
                  pl.BlockSpec((DI, D), const),
                  pl.BlockSpec((tm, D), row), pl.BlockSpec((8, D), const), pl.BlockSpec((8, D), const)],
        out_specs=[pl.BlockSpec((tm, D), row), pl.BlockSpec((tm, D), row), pl.BlockSpec((tm, DI), row),
                   pl.BlockSpec((tm, D), row)],
        out_shape=[jax.ShapeDtypeStruct((s, D), F32), jax.ShapeDtypeStruct((s, D), F32),
                   jax.ShapeDtypeStruct((s, DI), BF16), jax.ShapeDtypeStruct((s, D), BF16)],
        compiler_params=_params("parallel"),
    )(ypool, ysb, proj0, wout, x, vecs, vecs_next)


def _odd_out(proj1, wout, x1, vecs, cw, cb, target):
    s = x1.shape[0]
    tm = min(TME, s)
    hb = tm // HALO

    def body(gb_ref, gc_ref, u_ref, gt_ref, hgc_ref, hu_ref, w_ref, x1_ref, vec_ref, cw_ref, cb_ref, tg_ref,
             dx2_ref, y1_ref, acc_ref, ext_ref):
        i = pl.program_id(0)

        @pl.when(i == 0)
        def _():
            acc_ref[...] = jnp.zeros_like(acc_ref)

        ext_ref[HALO:, :] = gc_ref[...].astype(F32) * u_ref[...].astype(F32)
        ext_ref[0:HALO, :] = jnp.where(i > 0, hgc_ref[...].astype(F32) * hu_ref[...].astype(F32), 0.0)
        for c in range(DI // CT):
            cs = slice(c * CT, (c + 1) * CT)
            conv = (cb_ref[0:1, cs] + cw_ref[0:1, cs] * ext_ref[HALO - 2:HALO - 2 + tm, cs]
                    + cw_ref[1:2, cs] * ext_ref[HALO - 1:HALO - 1 + tm, cs]
                    + cw_ref[2:3, cs] * ext_ref[HALO:HALO + tm, cs])
            gt = gt_ref[:, cs].astype(F32)
            y1_ref[:, cs] = (gb_ref[:, cs].astype(F32) * conv * (gt * _sigmoid(gt))).astype(BF16)
        out = _dot(y1_ref[...], w_ref[...])
        x2 = x1_ref[...] + (1.0 + vec_ref[3:4, :]) * out
        r = lax.rsqrt(jnp.mean(x2 * x2, axis=-1, keepdims=True) + EPS)
        nrm = x2 * r
        fg = vec_ref[4:5, :]
        err = nrm * fg - tg_ref[...]
        acc_ref[1:2, :] += _rowsum(err * err) * (0.5 / D)
        dyf = err * (1.0 / D)
        acc_ref[0:1, :] += _rowsum(dyf * nrm)
        dn = dyf * fg
        dx2 = r * (dn - nrm * jnp.mean(dn * nrm, axis=-1, keepdims=True))
        dx2_ref[...] = dx2
        acc_ref[2:3, :] += _rowsum(dx2 * out)

    row = lambda i: (i, 0)
    halo = lambda col: (lambda i: (jnp.maximum(i * hb - 1, 0), col))
    const = lambda i: (0, 0)
    return pl.pallas_call(
        body, name="odd_out", grid=(s // tm,),
        in_specs=[pl.BlockSpec((tm, DI), lambda i: (i, 0)), pl.BlockSpec((tm, DI), lambda i: (i, 1)),
                  pl.BlockSpec((tm, DI), lambda i: (i, 2)), pl.BlockSpec((tm, DI), lambda i: (i, 3)),
                  pl.BlockSpec((HALO, DI), halo(1)), pl.BlockSpec((HALO, DI), halo(2)),
                  pl.BlockSpec((DI, D), const), pl.BlockSpec((tm, D), row), pl.BlockSpec((8, D), const),
                  pl.BlockSpec((8, DI), const), pl.BlockSpec((1, DI), const), pl.BlockSpec((tm, D), row)],
        out_specs=[pl.BlockSpec((tm, D), row), pl.BlockSpec((tm, DI), row), pl.BlockSpec((8, D), const)],
        out_shape=[jax.ShapeDtypeStruct((s, D), F32), jax.ShapeDtypeStruct((s, DI), BF16),
                   jax.ShapeDtypeStruct((8, D), F32)],
        scratch_shapes=[pltpu.VMEM((tm + HALO, DI), F32)],
        compiler_params=_params("arbitrary"),
    )(proj1, proj1, proj1, proj1, proj1, proj1, wout, x1, vecs, cw, cb, target)


def _odd_bwd(dx2, proj1, wout, vecs, cw, cb):
    s = dx2.shape[0]
    tm = min(TME, s)
    nb = s // tm
    hb = tm // HALO

    def body(dx2_ref, gb_ref, gc_ref, u_ref, gt_ref, hgc_ref, hu_ref, w_ref, vec_ref, cw_ref, cb_ref,
             dout_ref, dproj_ref, accv_ref, uext_ref, dext_ref, dy_ref):
        i = pl.program_id(0)
        blk = nb - 1 - i

        @pl.when(i == 0)
        def _():
            accv_ref[...] = jnp.zeros_like(accv_ref)
            dext_ref[tm:tm + 8, :] = jnp.zeros((8, DI), F32)

        dout = (dx2_ref[...] * (1.0 + vec_ref[3:4, :])).astype(BF16)
        dout_ref[...] = dout
        dy_ref[...] = _dot_nt(dout, w_ref[...])
        uext_ref[HALO:, :] = gc_ref[...].astype(F32) * u_ref[...].astype(F32)
        uext_ref[0:HALO, :] = jnp.where(blk > 0, hgc_ref[...].astype(F32) * hu_ref[...].astype(F32), 0.0)
        for c in range(DI // CT):
            cs = slice(c * CT, (c + 1) * CT)
            u0 = uext_ref[HALO - 2:HALO - 2 + tm, cs]
            u1 = uext_ref[HALO - 1:HALO - 1 + tm, cs]
            u2 = uext_ref[HALO:HALO + tm, cs]
            w0, w1, w2 = cw_ref[0:1, cs], cw_ref[1:2, cs], cw_ref[2:3, cs]
            conv = cb_ref[0:1, cs] + w0 * u0 + w1 * u1 + w2 * u2
            gt = gt_ref[:, cs].astype(F32)
            sg = _sigmoid(gt)
            gb = gb_ref[:, cs].astype(F32)
            dy = dy_ref[:, cs]
            t1 = dy * (gt * sg)
            dproj_ref[:, cs] = (t1 * conv).astype(BF16)
            dconv = t1 * gb
            dproj_ref[:, 3 * DI + c * CT:3 * DI + (c + 1) * CT] = (
                dy * gb * conv * (sg * (1.0 + gt * (1.0 - sg)))).astype(BF16)
            accv_ref[0:1, cs] += _rowsum(dconv * u0)
            accv_ref[1:2, cs] += _rowsum(dconv * u1)
            accv_ref[2:3, cs] += _rowsum(dconv * u2)
            accv_ref[3:4, cs] += _rowsum(dconv)
            dext_ref[0:tm, cs] = dconv
            duu = w2 * dconv + w1 * dext_ref[1:tm + 1, cs] + w0 * dext_ref[2:tm + 2, cs]
            dproj_ref[:, DI + c * CT:DI + (c + 1) * CT] = (duu * u_ref[:, cs].astype(F32)).astype(BF16)
            dproj_ref[:, 2 * DI + c * CT:2 * DI + (c + 1) * CT] = (duu * gc_ref[:, cs].astype(F32)).astype(BF16)
        dext_ref[tm:tm + 8, :] = dext_ref[0:8, :]

    rrow = lambda i: (nb - 1 - i, 0)
    rcol = lambda col: (lambda i: (nb - 1 - i, col))
    halo = lambda col: (lambda i: (jnp.maximum((nb - 1 - i) * hb - 1, 0), col))
    const = lambda i: (0, 0)
    return pl.pallas_call(
        body, name="odd_bwd", grid=(nb,),
        in_specs=[pl.BlockSpec((tm, D), rrow),
                  pl.BlockSpec((tm, DI), rcol(0)), pl.BlockSpec((tm, DI), rcol(1)),
                  pl.BlockSpec((tm, DI), rcol(2)), pl.BlockSpec((tm, DI), rcol(3)),
                  pl.BlockSpec((HALO, DI), halo(1)), pl.BlockSpec((HALO, DI), halo(2)),
                  pl.BlockSpec((DI, D), const), pl.BlockSpec((8, D), const),
                  pl.BlockSpec((8, DI), const), pl.BlockSpec((1, DI), const)],
        out_specs=[pl.BlockSpec((tm, D), rrow), pl.BlockSpec((tm, NO), rrow), pl.BlockSpec((8, DI), const)],
        out_shape=[jax.ShapeDtypeStruct((s, D), BF16), jax.ShapeDtypeStruct((s, NO), BF16),
                   jax.ShapeDtypeStruct((8, DI), F32)],
        scratch_shapes=[pltpu.VMEM((tm + HALO, DI), F32), pltpu.VMEM((tm + 8, DI), F32), pltpu.VMEM((tm, DI), F32)],
        compiler_params=_params("arbitrary"),
    )(dx2, proj1, proj1, proj1, proj1, proj1, proj1, wout, vecs, cw, cb)


def _grad_w_body(nk):
    def body(a_ref, b_ref, o_ref, ob_ref):
        k = pl.program_id(1)

        @pl.when(k == 0)
        def _():
            o_ref[...] = jnp.zeros_like(o_ref)

        o_ref[...] += _dot_tn(a_ref[...], b_ref[...])

        @pl.when(k == nk - 1)
        def _():
            ob_ref[...] = o_ref[...].astype(BF16)

    return body


def _grad_w_cols(a, b, name):
    s, m = a.shape
    ns = b.shape[1] // NCHIP
    ts = min(TM, s)
    blk = pl.BlockSpec((None, m, ns), lambda j, k: (j, 0, 0))
    return pl.pallas_call(
        _grad_w_body(s // ts), name=name, grid=(NCHIP, s // ts),
        in_specs=[pl.BlockSpec((ts, m), lambda j, k: (k, 0)),
                  pl.BlockSpec((ts, ns), lambda j, k: (k, j))],
        out_specs=[blk, blk],
        out_shape=[jax.ShapeDtypeStruct((NCHIP, m, ns), F32), jax.ShapeDtypeStruct((NCHIP, m, ns), BF16)],
        compiler_params=_params("parallel", "arbitrary"),
    )(a, b)


def _grad_w_rows(a, b, name):
    s = a.shape[0]
    ms = a.shape[1] // NCHIP
    n = b.shape[1]
    ts = min(TM, s)
    blk = pl.BlockSpec((None, ms, n), lambda i, k: (i, 0, 0))
    return pl.pallas_call(
        _grad_w_body(s // ts), name=name, grid=(NCHIP, s // ts),
        in_specs=[pl.BlockSpec((ts, ms), lambda i, k: (k, i)),
                  pl.BlockSpec((ts, n), lambda i, k: (k, 0))],
        out_specs=[blk, blk],
        out_shape=[jax.ShapeDtypeStruct((NCHIP, ms, n), F32), jax.ShapeDtypeStruct((NCHIP, ms, n), BF16)],
        compiler_params=_params("parallel", "arbitrary"),
    )(a, b)


def _inproj_bwd(dproj, w, x, dx_in, vecs, name, comm=None):
    s = x.shape[0]
    ns = w.shape[2]
    tm = min(TME, s)
    ni = s // tm
    comm = comm or _NO_COMM
    nci, nco = len(comm.arrays), len(comm.out_shape)

    def body(*refs):
        dp_ref, w_hbm, x_ref, dxin_ref, vec_ref = refs[:5]
        dx_ref, acc_ref = refs[5 + nci:7 + nci]
        w_ref = refs[7 + nci + nco]
        cargs = (refs[5:5 + nci], refs[7 + nci:7 + nci + nco], refs[8 + nci + nco:])
        i = pl.program_id(0)

        @pl.when(i == 0)
        def _():
            acc_ref[...] = jnp.zeros_like(acc_ref)
            if nci:
                comm.start(*cargs)
            pltpu.sync_copy(w_hbm, w_ref)

        dh = _dot_nt(dp_ref[:, 0:ns], w_ref[0])
        for j in range(1, NCHIP):
            dh = dh + _dot_nt(dp_ref[:, j * ns:(j + 1) * ns], w_ref[j])
        xv = x_ref[...]
        r = lax.rsqrt(jnp.mean(xv * xv, axis=-1, keepdims=True) + EPS)
        nrm = xv * r
        g = vec_ref[0:1, :]
        sc1 = 1.0 + vec_ref[1:2, :]
        dhn = dh * nrm
        acc_ref[0:1, :] += _rowsum(dh)
        acc_ref[1:2, :] += _rowsum(dhn) * g
        acc_ref[2:3, :] += _rowsum(dhn) * sc1
        dn = dh * (g * sc1)
        dx_ref[...] = dxin_ref[...] + r * (dn - nrm * jnp.mean(dn * nrm, axis=-1, keepdims=True))

        if nci:
            pl.when(i == ni - 1)(lambda: comm.finish(*cargs))

    row = lambda i: (i, 0)
    const = lambda i: (0, 0)
    return pl.pallas_call(
        body, name=name, grid=(ni,),
        in_specs=[pl.BlockSpec((tm, NCHIP * ns), row), ANY,
                  pl.BlockSpec((tm, D), row), pl.BlockSpec((tm, D), row), pl.BlockSpec((8, D), const)] + [ANY] * nci,
        out_specs=[pl.BlockSpec((tm, D), row), pl.BlockSpec((8, D), const)] + [ANY] * nco,
        out_shape=[jax.ShapeDtypeStruct((s, D), F32), jax.ShapeDtypeStruct((8, D), F32)] + comm.out_shape,
        scratch_shapes=[pltpu.VMEM(w.shape, BF16)] + comm.sems,
        compiler_params=_params("arbitrary"),
    )(dproj, w, x, dx_in, vecs, *comm.arrays)


def _even_bwd(dx1, out0, ypool, ysb, proj0, wout, vecs):
    s = dx1.shape[0]
    tm = min(TME, s)

    def body(dx1_ref, out0_ref, yp_ref, ys_ref, gate_ref, w_ref, vec_ref,
             dout_ref, dyp_ref, dys_ref, dgt_ref, acc_ref):
        @pl.when(pl.program_id(0) == 0)
        def _():
            acc_ref[...] = jnp.zeros_like(acc_ref)

        dx1v = dx1_ref[...]
        acc_ref[0:1, :] += _rowsum(dx1v * out0_ref[...])
        dout = (dx1v * (1.0 + vec_ref[3:4, :])).astype(BF16)
        dout_ref[...] = dout
        dyg = _dot_nt(dout, w_ref[...])
        gt = gate_ref[...].astype(F32)
        sg = _sigmoid(gt)
        sl = gt * sg
        dsl = sg * (1.0 + gt * (1.0 - sg))
        dyp_ref[...] = dyg[:, :DP] * sl[:, :DP]
        dys_ref[...] = dyg[:, DP:] * sl[:, DP:]
        dgt_ref[:, :DP] = (dyg[:, :DP] * yp_ref[...] * dsl[:, :DP]).astype(BF16)
        dgt_ref[:, DP:] = (dyg[:, DP:] * ys_ref[...] * dsl[:, DP:]).astype(BF16)

    row = lambda i: (i, 0)
    const = lambda i: (0, 0)
    return pl.pallas_call(
        body, name="even_bwd", grid=(s // tm,),
        in_specs=[pl.BlockSpec((tm, D), row), pl.BlockSpec((tm, D), row),
                  pl.BlockSpec((tm, DP), row), pl.BlockSpec((tm, DP), row),
                  pl.BlockSpec((tm, DI), lambda i: (i, 2)),
                  pl.BlockSpec((DI, D), const), pl.BlockSpec((8, D), const)],
        out_specs=[pl.BlockSpec((tm, D), row), pl.BlockSpec((tm, DP), row), pl.BlockSpec((tm, DP), row),
                   pl.BlockSpec((tm, DI), lambda i: (i, NE // DI - 1)), pl.BlockSpec((8, D), const)],
        out_shape=[jax.ShapeDtypeStruct((s, D), BF16), jax.ShapeDtypeStruct((s, DP), F32),
                   jax.ShapeDtypeStruct((s, DP), F32), jax.ShapeDtypeStruct((s, NE), BF16),
                   jax.ShapeDtypeStruct((8, D), F32)],
        compiler_params=_params("arbitrary"),
    )(dx1, out0, ypool, ysb, proj0, wout, vecs)


def _pool_bwd(dyp, p, pw, pscale, dproj):
    s = dyp.shape[0]
    tm = min(TM, s)
    nb = s // tm
    hb = tm // 16

    def body(dy_ref, dyh_ref, p_ref, w_ref, sc_ref, dproj_hbm, du_ref, dw_ref, acc_ref, ext_ref):
        i = pl.program_id(0)

        @pl.when(i == 0)
        def _():
            dw_ref[...] = jnp.zeros_like(dw_ref)
            acc_ref[...] = jnp.zeros_like(acc_ref)

        t = i * tm + lax.broadcasted_iota(jnp.int32, (tm + 16, 1), 0)
        for g, wdw in enumerate(WINDOWS):
            cs = slice(g * PG, (g + 1) * PG)
            sc = sc_ref[:, cs]
            dy = dy_ref[:, cs]
            dyh = jnp.where(i < nb - 1, dyh_ref[:, cs], 0.0)
            pb = p_ref[:, cs]
            wg = w_ref[g]
            acc_ref[0:1, cs] += _rowsum(dy * _dot(pb, wg))
            dypre = (dy * sc).astype(BF16)
            dw_ref[g] += _dot_tn(pb, dypre)
            dp = _dot_nt(dypre, wg)
            dph = _dot_nt((dyh * sc).astype(BF16), wg)
            inv = 1.0 / jnp.minimum(t + 1, wdw).astype(F32)
            ext_ref[0:tm, cs] = dp * inv[0:tm]
            ext_ref[tm:tm + 16, cs] = dph * inv[tm:tm + 16]
            acc = ext_ref[0:tm, cs]
            for j in range(1, wdw):
                acc = acc + ext_ref[j:j + tm, cs]
            du_ref[:, cs] = (acc - dp).astype(BF16)

    row = lambda i: (i, 0)
    return pl.pallas_call(
        body, name="pool_bwd", grid=(nb,),
        in_specs=[pl.BlockSpec((tm, DP), row),
                  pl.BlockSpec((16, DP), lambda i: (jnp.minimum((i + 1) * hb, s // 16 - 1), 0)),
                  pl.BlockSpec((tm, DP), row),
                  pl.BlockSpec((4, PG, PG), lambda i: (0, 0, 0)),
                  pl.BlockSpec((1, DP), lambda i: (0, 0)), ANY],
        out_specs=[pl.BlockSpec((tm, DP), row), pl.BlockSpec((4, PG, PG), lambda i: (0, 0, 0)),
                   pl.BlockSpec((8, DP), lambda i: (0, 0))],
        out_shape=[jax.ShapeDtypeStruct((s, NE), BF16), jax.ShapeDtypeStruct((4, PG, PG), F32),
                   jax.ShapeDtypeStruct((8, DP), F32)],
        input_output_aliases={5: 0},
        scratch_shapes=[pltpu.VMEM((tm + 16, DP), F32)],
        compiler_params=_params("arbitrary"),
    )(dyp, dyp, p, pw, pscale, dproj)


def _attn_bwd(proj0, ysb, dys, dproj, comm=None):
    s = proj0.shape[0]
    nq = s // BQ
    kpq = BQ // BK
    nsteps = DP // 128
    comm = comm or _NO_COMM
    nci, nco = len(comm.arrays), len(comm.out_shape)

    def body(*refs):
        q_ref, k_ref, v_ref, o_ref, do_ref = refs[:5]
        dproj_hbm = refs[6 + nci]
        scratch = refs[7 + nci + nco:]
        qn_ref, k8_ref, vb_ref, dob_ref, dka_ref, dva_ref, dq_ref, dk_ref, dv_ref, part_sems = scratch[:10]
        cargs = (refs[6:6 + nci], refs[7 + nci:7 + nci + nco], scratch[10:])
        step = pl.program_id(0)
        if nci:
            pl.when(step == 0)(lambda: comm.start(*cargs))

        qn_ref[...] = (-q_ref[...]).astype(BF16)
        k8_ref[...] = (k_ref[...] * INV_SQRT_HD).astype(BF16)
        vb_ref[...] = v_ref[...].astype(BF16)
        dob_ref[...] = do_ref[...].astype(BF16)
        dka_ref[...] = jnp.zeros_like(dka_ref)
        dva_ref[...] = jnp.zeros_like(dva_ref)
        after, from_on = _tri_masks()
        causal = _causal_mask()
        heads = [slice(HD * h, HD * (h + 1)) for h in range(2)]
        hs = range(len(heads))
        lo, hi = (0, BK), (BK, BK)

        def qstep(qi, carry):
            q0 = pl.multiple_of(qi * BQ, BQ)
            qn = [qn_ref[pl.ds(q0, BQ), ls] for ls in heads]
            do = [dob_ref[pl.ds(q0, BQ), ls] for ls in heads]
            total = [jnp.sum(do[h].astype(F32) * o_ref[pl.ds(q0, BQ), ls], axis=1, keepdims=True)
                     for h, ls in enumerate(heads)]

            def sweep(lanes, state):
                rows_of = lambda i: slice(lanes[i][1][0], lanes[i][1][0] + lanes[i][1][1])
                k8 = [k8_ref[pl.ds(k0, BK), heads[h]] for h, _, k0, _ in lanes]
                nz = [_dot_nt(qn[lane[0]][rows_of(i)], k8[i]) for i, lane in enumerate(lanes)]
                da = [_dot_nt(do[h][rows_of(i)], vb_ref[pl.ds(k0, BK), heads[h]]) for i, (h, _, k0, _) in enumerate(lanes)]
                ll = [_sb_logits(nz[i], lane[3]) for i, lane in enumerate(lanes)]
                aft = [_dot(ll[i][0].astype(BF16), after) for i in range(len(lanes))]
                state = dict(state)
                for i, (h, rows, k0, mask) in enumerate(lanes):
                    dq_acc, c, cg = state[h, rows]
                    a = jnp.exp(ll[i][1] + aft[i] + c)
                    if mask is not None:
                        a = jnp.where(mask, a, 0.0)
                    ab = a.astype(BF16)
                    g = da[i] * ab.astype(F32)
                    suf = _split_dot(g, from_on)
                    dz = g - jnp.exp(ll[i][1]) * (g + ((total[h][rows_of(i)] - cg) - suf))
                    if mask is not None:
                        dz = jnp.where(mask, dz, 0.0)
                    dzb = dz.astype(BF16)
                    dka_ref[pl.ds(k0, BK), heads[h]] += _dot_tn(dzb, qn[h][rows_of(i)])
                    dva_ref[pl.ds(k0, BK), heads[h]] += _dot_tn(ab, do[h][rows_of(i)])
                    state[h, rows] = (dq_acc + _dot(dzb, k8[i]), c + aft[i][:, 0:1] + ll[i][0][:, 0:1],
                                      cg + suf[:, 0:1])
                return state

            k_lo, k_hi = q0, pl.multiple_of(q0 + BK, BK)
            zero = jnp.zeros((BK, 1), F32)
            st = {(h, r): (jnp.zeros((BK, HD), F32), zero, zero) for h in hs for r in (lo, hi)}
            st = sweep([(h, hi, k_hi, causal) for h in hs] + [(h, lo, k_lo, causal) for h in hs]
                       + [(h, hi, k_lo, None) for h in hs], st)
            def left_of(rows):
                def step(j, part):
                    k0 = pl.multiple_of(q0 - (j + 1) * BK, BK)
                    got = sweep([(h, rows, k0, None) for h in hs], {(h, rows): part[h] for h in hs})
                    return tuple(got[h, rows] for h in hs)
                return step

            done = {r: _sweep_left(qi * (BQ // BK), tuple(st[h, r] for h in hs), left_of(r)) for r in (lo, hi)}
            for h, ls in enumerate(heads):
                dq_ref[pl.ds(q0, BQ), ls] = jnp.concatenate([done[lo][h][0], done[hi][h][0]], axis=0).astype(BF16)
            return carry

        lax.fori_loop(0, nq, qstep, 0)
        dk_ref[...] = (dka_ref[...] * (-INV_SQRT_HD)).astype(BF16)
        dv_ref[...] = dva_ref[...].astype(BF16)
        lanes0 = pl.multiple_of(step * 128, 128)
        parts = [pltpu.make_async_copy(src, dproj_hbm.at[:, pl.ds((1 + k) * DP + lanes0, 128)], part_sems.at[k])
                 for k, src in enumerate((dq_ref, dk_ref, dv_ref))]
        for cp in parts:
            cp.start()
        for cp in parts:
            cp.wait()
        if nci:
            pl.when(step == nsteps - 1)(lambda: comm.finish(*cargs))

    col = lambda h: (0, h)
    return pl.pallas_call(
        body, name="attn_bwd", grid=(nsteps,),
        in_specs=[pl.BlockSpec((s, 128), lambda h: (0, 8 + h)),
                  pl.BlockSpec((s, 128), lambda h: (0, 16 + h)),
                  pl.BlockSpec((s, 128), lambda h: (0, 24 + h)),
                  pl.BlockSpec((s, 128), col), pl.BlockSpec((s, 128), col), ANY] + [ANY] * nci,
        out_specs=[ANY] * (1 + nco),
        out_shape=[jax.ShapeDtypeStruct((s, NE), BF16)] + comm.out_shape,
        input_output_aliases={5: 0},
        scratch_shapes=([pltpu.VMEM((s, 128), BF16)] * 4 + [pltpu.VMEM((s, 128), F32)] * 2
                        + [pltpu.VMEM((s, 128), BF16)] * 3 + [pltpu.SemaphoreType.DMA((3,))] + comm.sems),
        compiler_params=_params("arbitrary"),
    )(proj0, proj0, proj0, ysb, dys, dproj, *comm.arrays)


def _adamw_math(w, g, m, v):
    m2 = B1 * m + (1.0 - B1) * g
    v2 = B2 * v + (1.0 - B2) * (g * g)
    m_hat = m2 / (1.0 - B1 ** STEP)
    v_hat = v2 / (1.0 - B2 ** STEP)
    return -LR * (m_hat / (jnp.sqrt(v_hat) + EPS_ADAM) + WD * w), m2, v2


def _adamw(w, g, m, v, name):
    r, c = w.shape
    tr = r
    while tr * c * 4 > (1 << 20) and tr % 16 == 0:
        tr //= 2

    def body(w_ref, g_ref, m_ref, v_ref, d_ref, m2_ref, v2_ref):
        d_ref[...], m2_ref[...], v2_ref[...] = _adamw_math(w_ref[...], g_ref[...], m_ref[...], v_ref[...])

    spec = pl.BlockSpec((tr, c), lambda i: (i, 0))
    return pl.pallas_call(
        body, name=name, grid=(r // tr,),
        in_specs=[spec] * 4, out_specs=[spec] * 3,
        out_shape=[jax.ShapeDtypeStruct((r, c), F32)] * 3,
        compiler_params=_params("parallel"),
    )(w, g, m, v)


def _local_step(x, target, vecs0, vecs1, win0, rest, pscale, cw8, cb, sel=None):
    dist = sel is not None
    h0 = _norm_mod(x, vecs0, "norm0")
    if dist:
        chip = sel[0]
        order = jnp.stack([chip, chip ^ 2, chip ^ 1, chip ^ 3])
        proj0, win0, *got = _inproj_gathering(h0, win0, order, "inproj0", _gather_comm(list(rest[0:2])))
        pw, wout0 = _group_major(got[0]), got[1].reshape(DI, D)
    else:
        proj0, = _inproj(h0, win0, "inproj0")
        pw, wout0 = rest[0:2]
    p, ypool = _pool_fwd(proj0, pw, pscale)
    ysb, *got = _attn_fwd(proj0, _gather_comm(list(rest[2:4])) if dist else None)
    win1, wout1 = (got[0], got[1].reshape(DI, D)) if dist else rest[2:4]
    x1, out0, yg, h1 = _even_out(ypool, ysb, proj0, wout0, x, vecs0, vecs1)
    proj1, = _inproj(h1, win1, "inproj1")
    dx2, y1, acc_f = _odd_out(proj1, wout1, x1, vecs1, cw8, cb, target)

    def chip_partials(grads, names):
        from_sibling = _send_halves([g16 for _, g16 in grads], "rs_send_halves_" + names[0])
        part = [_add_halves(g32, t, sel[1:2], "rs_add_halves_" + nm)
                for (g32, _), t, nm in zip(grads, from_sibling, names)]
        return [p32 for p32, _ in part], _exchange_comm([p16 for _, p16 in part])

    dout1, dproj1, acc_cv = _odd_bwd(dx2, proj1, wout1, vecs1, cw8, cb)
    g_wout1 = _grad_w_rows(y1, dout1, "grad_wout1")
    g_win1 = _grad_w_cols(h1, dproj1, "grad_win1")
    dx1, acc_n1 = _inproj_bwd(dproj1, win1, x1, dx2, vecs1, "inproj1_bwd")

    dout0, dyp, dys, dproj0, acc_g0 = _even_bwd(dx1, out0, ypool, ysb, proj0, wout0, vecs0)
    g_wout0 = _grad_w_rows(yg, dout0, "grad_wout0")
    dproj0, g_pw, acc_ps = _pool_bwd(dyp, p, pw, pscale, dproj0)
    g_pool = _chip_major(g_pw, PG // NCHIP)
    early = [(g_pool, g_pool.astype(BF16)), g_wout0, g_win1, g_wout1]
    part_a, swap_a = chip_partials(early, ["pool", "wout0", "win1", "wout1"]) if dist else (None, None)
    dproj0, *got_a = _attn_bwd(proj0, ysb, dys, dproj0, swap_a)
    g_win0 = _grad_w_cols(h0, dproj0, "grad_win0")
    part_b, swap_b = chip_partials([g_win0], ["win0"]) if dist else (None, None)
    dx0, acc_n0, *got_b = _inproj_bwd(dproj0, win0, x, dx1, vecs0, "inproj0_bwd", swap_b)

    if dist:
        names = ["win0", "pool", "wout0", "win1", "wout1"]
        halves = [_add_partials(p32, t, sel, "rs_add_partials_" + nm)
                  for p32, t, nm in zip(part_b + part_a, got_b + got_a, names)]
        grads = tuple(_join_halves(halves))
    else:
        grads = (g_win0[0], g_pw, g_wout0[0], g_win1[0], g_wout1[0])

    sums = dict(
        dm0=jnp.concatenate([acc_n0[0:2], acc_g0[0:1]], axis=0),
        dm1=jnp.concatenate([acc_n1[0:2], acc_f[2:3]], axis=0),
        norm_g=jnp.concatenate([acc_n0[2:3], acc_n1[2:3]], axis=0),
        pool_scale=acc_ps[0:1], final_g=acc_f[0:1], loss=acc_f[1:2],
        conv_w=acc_cv[0:3], conv_b=acc_cv[3:4])
    return dx0, grads, sums


ANY = pl.BlockSpec(memory_space=pl.ANY)
CHIP_FLIPS = ((1, 0), (0, 1), (1, 1))


def _place():
    return lax.axis_index("x"), lax.axis_index("y"), lax.axis_index("c")


def _flip(v, f):
    return 1 - v if f else v


def _allgather8(v, name):
    m_per, n = v.shape

    def body(x_ref, out_ref, send_sems, recv_sems, local_sem):
        x, y, c = _place()
        me, sibling = (x, y, c), (x, y, 1 - c)
        chips = [(_flip(x, fx), _flip(y, fy)) for fx, fy in CHIP_FLIPS]

        def rows(px, py, pc):
            return out_ref.at[pl.ds((4 * px + 2 * py + pc) * m_per, m_per), :]

        def copy(k, block, to, src=None):
            return pltpu.make_async_remote_copy(
                src_ref=rows(*block) if src is None else src, dst_ref=rows(*block),
                send_sem=send_sems.at[k], recv_sem=recv_sems.at[k], device_id=to, device_id_type=MESH)

        mine = pltpu.make_async_copy(x_ref, rows(*me), local_sem)
        mine.start()
        first = [copy(0, me, sibling, src=x_ref)]
        first += [copy(1 + j, me, (*chip, c), src=x_ref) for j, chip in enumerate(chips)]
        for cp in first:
            cp.start()
        passed = [copy(4 + j, (*chip, c), sibling) for j, chip in enumerate(chips)]
        for j, chip in enumerate(chips):
            copy(1 + j, (*chip, c), me).wait_recv()
            passed[j].start()
        copy(0, sibling, me).wait_recv()
        for j, chip in enumerate(chips):
            copy(4 + j, (*chip, 1 - c), me).wait_recv()
        for cp in first + passed:
            cp.wait_send()
        mine.wait()

    return pl.pallas_call(
        body, name=name,
        out_shape=jax.ShapeDtypeStruct((NDEV * m_per, n), v.dtype),
        in_specs=[pl.BlockSpec(memory_space=pltpu.VMEM)],
        out_specs=pl.BlockSpec(memory_space=pltpu.VMEM),
        scratch_shapes=[pltpu.SemaphoreType.DMA((7,)), pltpu.SemaphoreType.DMA((7,)), pltpu.SemaphoreType.DMA],
    )(v)


class _Comm:
    def __init__(self, arrays, out_shape, sems, start, finish):
        self.arrays, self.out_shape, self.sems, self.start, self.finish = arrays, out_shape, sems, start, finish


_NO_COMM = _Comm([], [], [], None, None)


def _gather_comm(shards):
    n = len(shards)

    def pieces(ins, outs, sems, kinds):
        x, y, c = _place()
        ici_send, ici_recv, fwd_send, fwd_recv = sems[:4]
        own, sibling = 2 * x + y, (x, y, 1 - c)
        made = {kind: [] for kind in kinds}
        for w in range(n):
            r2 = ins[w].shape[0] // 2
            mine, other = pl.ds(c * r2, r2), pl.ds((1 - c) * r2, r2)
            for d, (fx, fy) in enumerate(CHIP_FLIPS):
                px, py, k = _flip(x, fx), _flip(y, fy), 3 * w + d
                peer = 2 * px + py
                ici = dict(send_sem=ici_send.at[k], recv_sem=ici_recv.at[k], device_id=(px, py, c), device_id_type=MESH)
                fwd = dict(send_sem=fwd_send.at[k], recv_sem=fwd_recv.at[k], device_id=sibling, device_id_type=MESH)
                if "ici_out" in kinds:
                    made["ici_out"].append(pltpu.make_async_remote_copy(
                        src_ref=ins[w].at[mine, :], dst_ref=outs[w].at[own, mine, :], **ici))
                if "ici_in" in kinds:
                    made["ici_in"].append(pltpu.make_async_remote_copy(
                        src_ref=ins[w].at[mine, :], dst_ref=outs[w].at[peer, mine, :], **ici))
                if "fwd_out" in kinds:
                    made["fwd_out"].append(pltpu.make_async_remote_copy(
                        src_ref=outs[w].at[peer, mine, :], dst_ref=outs[w].at[peer, mine, :], **fwd))
                if "fwd_in" in kinds:
                    made["fwd_in"].append(pltpu.make_async_remote_copy(
                        src_ref=outs[w].at[peer, other, :], dst_ref=outs[w].at[peer, other, :], **fwd))
        return made

    def stage_rows(a):
        return min(a.shape[0], 256)

    def start(ins, outs, sems):
        for cp in pieces(ins, outs, sems, ("ici_out",))["ici_out"]:
            cp.start()
        x, y, _ = _place()
        for w, stage in enumerate(sems[4:]):
            rows = stage.shape[0]
            for r in range(0, ins[w].shape[0], rows):
                pltpu.sync_copy(ins[w].at[pl.ds(r, rows), :], stage)
                pltpu.sync_copy(stage, outs[w].at[2 * x + y, pl.ds(r, rows), :])

    def finish(ins, outs, sems):
        made = pieces(ins, outs, sems, ("ici_out", "ici_in", "fwd_out", "fwd_in"))
        for arrived, onward in zip(made["ici_in"], made["fwd_out"]):
            arrived.wait_recv()
            onward.start()
        for cp in made["fwd_in"]:
            cp.wait_recv()
        for cp in made["ici_out"] + made["fwd_out"]:
            cp.wait_send()

    dma = pltpu.SemaphoreType.DMA
    return _Comm(list(shards), [jax.ShapeDtypeStruct((NCHIP,) + a.shape, a.dtype) for a in shards],
                 [dma((3 * n,))] * 4 + [pltpu.VMEM((stage_rows(a), a.shape[1]), a.dtype) for a in shards],
                 start, finish)


def _exchange_comm(parts):
    n = len(parts)

    def copies(ins, outs, sems):
        x, y, c = _place()
        send_sems, recv_sems = sems
        out = []
        for w in range(n):
            for d, (fx, fy) in enumerate(CHIP_FLIPS):
                px, py = _flip(x, fx), _flip(y, fy)
                out.append(pltpu.make_async_remote_copy(
                    src_ref=ins[w].at[2 * px + py], dst_ref=outs[w].at[d], send_sem=send_sems.at[3 * w + d],
                    recv_sem=recv_sems.at[3 * w + d], device_id=(px, py, c), device_id_type=MESH))
        return out

    def start(ins, outs, sems):
        for cp in copies(ins, outs, sems):
            cp.start()

    def finish(ins, outs, sems):
        cps = copies(ins, outs, sems)
        for cp in cps:
            cp.wait_recv()
        for cp in cps:
            cp.wait_send()

    dma = pltpu.SemaphoreType.DMA
    return _Comm(list(parts), [jax.ShapeDtypeStruct((3,) + p.shape[1:], BF16) for p in parts],
                 [dma((3 * n,))] * 2, start, finish)


def _send_halves(grads, name):
    n = len(grads)

    def body(*refs):
        ins, outs = refs[:n], refs[n:2 * n]
        send_sems, recv_sems = refs[2 * n:]
        x, y, c = _place()
        copies = []
        for w in range(n):
            r2 = ins[w].shape[1] // 2
            cp = pltpu.make_async_remote_copy(
                src_ref=ins[w].at[:, pl.ds((1 - c) * r2, r2), :], dst_ref=outs[w],
                send_sem=send_sems.at[w], recv_sem=recv_sems.at[w], device_id=(x, y, 1 - c), device_id_type=MESH)
            cp.start()
            copies.append(cp)
        for cp in copies:
            cp.wait_recv()
        for cp in copies:
            cp.wait_send()

    return pl.pallas_call(
        body, name=name,
        out_shape=[jax.ShapeDtypeStruct((NCHIP, g.shape[1] // 2, g.shape[2]), g.dtype) for g in grads],
        in_specs=[ANY] * n, out_specs=[ANY] * n,
        scratch_shapes=[pltpu.SemaphoreType.DMA((n,)), pltpu.SemaphoreType.DMA((n,))],
    )(*grads)


def _row_tile(rows, cols):
    tr = rows
    while tr * cols * 4 > (1 << 20) and tr % 16 == 0:
        tr //= 2
    return tr


def _add_halves(g, t1, core, name):
    _, r, cdim = g.shape
    r2 = r // 2
    tr = _row_tile(r2, cdim)
    nt = r2 // tr

    def body(core_ref, g_ref, t_ref, p_ref, pb_ref):
        p = g_ref[...] + t_ref[...].astype(F32)
        p_ref[...] = p
        pb_ref[...] = p.astype(BF16)

    blk = pl.BlockSpec((None, tr, cdim), lambda j, i, core_ref: (j, i, 0))
    return pl.pallas_call(
        body, name=name,
        grid_spec=pltpu.PrefetchScalarGridSpec(
            num_scalar_prefetch=1, grid=(NCHIP, nt),
            in_specs=[pl.BlockSpec((None, tr, cdim), lambda j, i, core_ref: (j, core_ref[0] * nt + i, 0)), blk],
            out_specs=[blk, blk]),
        out_shape=[jax.ShapeDtypeStruct((NCHIP, r2, cdim), F32), jax.ShapeDtypeStruct((NCHIP, r2, cdim), BF16)],
        compiler_params=_params("parallel", "parallel"),
    )(core, g, t1)


def _add_partials(p, t2, sel, name):
    _, r2, cdim = p.shape
    tr = _row_tile(r2, cdim)
    nt = r2 // tr

    def body(sel_ref, p_ref, t_ref, o_ref):
        o_ref[...] = ((p_ref[...] + t_ref[0].astype(F32)) + t_ref[1].astype(F32)) + t_ref[2].astype(F32)

    return pl.pallas_call(
        body, name=name,
        grid_spec=pltpu.PrefetchScalarGridSpec(
            num_scalar_prefetch=1, grid=(nt,),
            in_specs=[pl.BlockSpec((None, tr, cdim), lambda i, sel_ref: (sel_ref[0], i, 0)),
                      pl.BlockSpec((3, tr, cdim), lambda i, sel_ref: (0, i, 0))],
            out_specs=pl.BlockSpec((tr, cdim), lambda i, sel_ref: (sel_ref[1] * nt + i, 0))),
        out_shape=jax.ShapeDtypeStruct((2 * r2, cdim), F32),
        compiler_params=_params("parallel"),
    )(sel, p, t2)


def _join_halves(grads):
    n = len(grads)

    def body(*refs):
        bufs = refs[n:2 * n]
        send_sems, recv_sems = refs[2 * n:]
        x, y, c = _place()
        copies = []
        for w in range(n):
            r2 = bufs[w].shape[0] // 2
            mine = bufs[w].at[pl.ds(c * r2, r2), :]
            cp = pltpu.make_async_remote_copy(
                src_ref=mine, dst_ref=mine, send_sem=send_sems.at[w], recv_sem=recv_sems.at[w],
                device_id=(x, y, 1 - c), device_id_type=MESH)
            cp.start()
            copies.append(cp)
        for w in range(n):
            r2 = bufs[w].shape[0] // 2
            theirs = bufs[w].at[pl.ds((1 - c) * r2, r2), :]
            pltpu.make_async_remote_copy(
                src_ref=theirs, dst_ref=theirs, send_sem=send_sems.at[w], recv_sem=recv_sems.at[w],
                device_id=(x, y, 1 - c), device_id_type=MESH).wait_recv()
        for cp in copies:
            cp.wait_send()

    return pl.pallas_call(
        body, name="rs_join_halves",
        out_shape=[jax.ShapeDtypeStruct(g.shape, F32) for g in grads],
        in_specs=[ANY] * n, out_specs=[ANY] * n, input_output_aliases={w: w for w in range(n)},
        scratch_shapes=[pltpu.SemaphoreType.DMA((n,)), pltpu.SemaphoreType.DMA((n,))],
    )(*grads)


def _ada_fwd(c_all, ada_w):
    nl, _, ns = ada_w.shape

    def body(c_ref, w_ref, o_ref):
        cv = c_ref[...]
        o_ref[...] = _dot((cv * _sigmoid(cv)).astype(BF16), w_ref[...].astype(BF16))

    return pl.pallas_call(
        body, name="ada_fwd", grid=(nl,),
        in_specs=[pl.BlockSpec((NDEV, D), lambda i: (0, 0)), pl.BlockSpec((None, D, ns), lambda i: (i, 0, 0))],
        out_specs=pl.BlockSpec((None, NDEV, ns), lambda i: (i, 0, 0)),
        out_shape=jax.ShapeDtypeStruct((nl, NDEV, ns), F32),
        compiler_params=_params("parallel"),
    )(c_all, ada_w)


PACK_ROWS = 24


def _reduce_packed(gathered):
    def body(g_ref, tot_ref, loss_ref):
        tot = g_ref[0:PACK_ROWS, :]
        for dev in range(1, NDEV):
            tot = tot + g_ref[dev * PACK_ROWS:(dev + 1) * PACK_ROWS, :]
        tot_ref[...] = tot
        loss_ref[...] = jnp.zeros((8, 128), F32) + jnp.sum(tot[10:11, :])

    return pl.pallas_call(
        body, name="reduce_packed",
        out_shape=[jax.ShapeDtypeStruct((PACK_ROWS, D), F32), jax.ShapeDtypeStruct((8, 128), F32)],
    )(gathered)


def _ada_w_update(c_t, dms, w, m, v):
    nl, _, ns = w.shape
    tr = 256

    def body(ct_ref, dm_ref, w_ref, m_ref, v_ref, g_ref, d_ref, m2_ref, v2_ref):
        ct = ct_ref[...]
        sc = ct * _sigmoid(ct)
        dm = dm_ref[...]
        g = sc[:, 0:1] * dm[0:1, :]
        for b in range(1, NDEV):
            g = g + sc[:, b:b + 1] * dm[b:b + 1, :]
        g_ref[...] = g
        d_ref[...], m2_ref[...], v2_ref[...] = _adamw_math(w_ref[...], g, m_ref[...], v_ref[...])

    blk = pl.BlockSpec((None, tr, ns), lambda i, j: (i, j, 0))
    return pl.pallas_call(
        body, name="ada_w_update", grid=(nl, D // tr),
        in_specs=[pl.BlockSpec((tr, NDEV), lambda i, j: (j, 0)),
                  pl.BlockSpec((None, NDEV, ns), lambda i, j: (i, 0, 0)), blk, blk, blk],
        out_specs=[blk] * 4,
        out_shape=[jax.ShapeDtypeStruct((nl, D, ns), F32)] * 4,
        compiler_params=_params("parallel", "parallel"),
    )(c_t, dms, w, m, v)


def _chip_major(a, parts):
    g, _, cdim = a.shape
    return jnp.transpose(a.reshape(g, NCHIP, parts, cdim), (1, 0, 2, 3)).reshape(NCHIP, g * parts, cdim)


def _group_major(a):
    return jnp.transpose(a.reshape(NCHIP, 4, PG // NCHIP, PG), (1, 0, 2, 3)).reshape(4, PG, PG)


def kernel(x, c, norm_g, ada_w, ada_b, even_w_in, pool_w, pool_scale, even_w_out, odd_w_in, conv_w, conv_b, odd_w_out, final_g, loss_target, m_norm_g, m_ada_w, m_ada_b, m_even_w_in, m_pool_w, m_pool_scale, m_even_w_out, m_odd_w_in, m_conv_w, m_conv_b, m_odd_w_out, m_final_g, v_norm_g, v_ada_w, v_ada_b, v_even_w_in, v_pool_w, v_pool_scale, v_even_w_out, v_odd_w_in, v_conv_w, v_conv_b, v_odd_w_out, v_final_g):
    ix, iy, ic = _place()
    chip = 2 * ix + iy
    batch = 2 * chip + ic
    sel = jnp.stack([chip, ic]).astype(jnp.int32)
    ns_ada = ada_w.shape[2]
    ns_conv = conv_b.shape[1]

    conv_rows = jnp.pad(jnp.concatenate([conv_w[0], conv_b], axis=0), ((0, 3), (0, D - ns_conv)))
    first = _allgather8(jnp.concatenate([c, conv_rows], axis=0), "gather_c_conv").reshape(NCHIP, 2, 8, D)
    c_all = first[:, :, 0].reshape(NDEV, D)
    cw_full = jnp.transpose(first[:, 0, 1:5, 0:ns_conv], (1, 0, 2)).reshape(4, DI)
    cw8 = jnp.concatenate([cw_full[0:3], jnp.zeros((5, DI), F32)], axis=0)
    cb_full = cw_full[3:4]

    m_cols = _allgather8(_ada_fwd(c_all, ada_w).reshape(2 * NDEV, ns_ada), "gather_ada")
    m_cols = m_cols.reshape(NCHIP, 2, 2, NDEV, ns_ada)[:, 0]
    m_mine = lax.dynamic_index_in_dim(m_cols, batch, axis=2, keepdims=False)
    m_mine = jnp.transpose(m_mine, (1, 0, 2)).reshape(2, 3 * D) + ada_b
    zrow = jnp.zeros((3, D), F32)

    def vec_rows(i):
        sh, sc, gt = m_mine[i, 0:D], m_mine[i, D:2 * D], m_mine[i, 2 * D:3 * D]
        return jnp.concatenate([jnp.stack([norm_g[i], sc, sh, gt, final_g]), zrow], axis=0)

    shards = (pool_w[0].astype(BF16).reshape(PG, PG), even_w_out[0].astype(BF16),
              odd_w_in[0].astype(BF16), odd_w_out[0].astype(BF16))
    dx0, grads, sums = _local_step(
        x[0], loss_target[0], vec_rows(0), vec_rows(1), even_w_in[0].astype(BF16), shards, pool_scale, cw8,
        cb_full, sel)
    r_win0, r_pw, r_wout0, r_win1, r_wout1 = grads

    packed = jnp.concatenate([
        sums["dm0"], sums["dm1"], sums["norm_g"], sums["pool_scale"], sums["final_g"], sums["loss"],
        sums["conv_w"].reshape(6, D), sums["conv_b"].reshape(2, D), jnp.zeros((PACK_ROWS - 19, D), F32)], axis=0)
    gathered = _allgather8(packed, "gather_sums")
    tot, loss8 = _reduce_packed(gathered)
    loss = loss8[0, 0]
    g_norm_g, g_pool_scale, g_final_g = tot[6:8], tot[8:9], tot[9]
    g_ada_b = tot[0:6].reshape(2, 3 * D)
    g_conv_w = lax.dynamic_slice_in_dim(tot[11:17].reshape(3, DI), chip * (DI // NCHIP), DI // NCHIP, axis=1)
    g_conv_b = lax.dynamic_slice_in_dim(tot[17:19].reshape(1, DI), chip * (DI // NCHIP), DI // NCHIP, axis=1)
    dm_all = gathered.reshape(NDEV, PACK_ROWS, D)[:, 0:6].reshape(NDEV, 2, 3 * D)
    dm_cols = jnp.transpose(lax.dynamic_slice_in_dim(dm_all, chip * ns_ada, ns_ada, axis=2), (1, 0, 2))
    g_ada_w, d_ada_w, nm_ada_w, nv_ada_w = _ada_w_update(jnp.transpose(c_all), dm_cols, ada_w, m_ada_w, v_ada_w)

    def upd(w, g, m, v, name):
        shape = w.shape
        w2, m2, v2 = (a.reshape(g.shape) for a in (w, m, v))
        d, nm, nv = _adamw(w2, g, m2, v2, name)
        return g.reshape(shape), d.reshape(shape), nm.reshape(shape), nv.reshape(shape)

    o_win0 = upd(even_w_in, r_win0, m_even_w_in, v_even_w_in, "adamw_win0")
    o_pw = upd(pool_w, r_pw, m_pool_w, v_pool_w, "adamw_pool")
    o_wout0 = upd(even_w_out, r_wout0, m_even_w_out, v_even_w_out, "adamw_wout0")
    o_win1 = upd(odd_w_in, r_win1, m_odd_w_in, v_odd_w_in, "adamw_win1")
    o_wout1 = upd(odd_w_out, r_wout1, m_odd_w_out, v_odd_w_out, "adamw_wout1")

    def pack_small(ng, ab, ps, fg, cwv, cbv):
        conv = jnp.concatenate([cwv.reshape(3, -1), cbv.reshape(1, -1)], axis=0).reshape(2, D)
        return jnp.concatenate([ng, ab.reshape(6, D), ps, fg.reshape(1, D), conv, jnp.zeros((4, D), F32)], axis=0)

    sw = pack_small(norm_g, ada_b, pool_scale, final_g, conv_w, conv_b)
    sg = pack_small(g_norm_g, g_ada_b, g_pool_scale, g_final_g, g_conv_w, g_conv_b)
    sm = pack_small(m_norm_g, m_ada_b, m_pool_scale, m_final_g, m_conv_w, m_conv_b)
    sv = pack_small(v_norm_g, v_ada_b, v_pool_scale, v_final_g, v_conv_w, v_conv_b) + jnp.concatenate(
        [jnp.zeros((12, D), F32), jnp.ones((4, D), F32)], axis=0)
    small = _adamw(sw, sg, sm, sv, "adamw_small")

    def unpack_small(a):
        conv = a[10:12].reshape(4, -1)
        return dict(norm_g=a[0:2], ada_b=a[2:8].reshape(2, 3 * D), pool_scale=a[8:9], final_g=a[9],
                    conv_w=conv[0:3].reshape(conv_w.shape), conv_b=conv[3:4].reshape(conv_b.shape))

    s_grad = dict(norm_g=g_norm_g, ada_b=g_ada_b, pool_scale=g_pool_scale, final_g=g_final_g,
                  conv_w=g_conv_w.reshape(conv_w.shape), conv_b=g_conv_b.reshape(conv_b.shape))
    s_out = [s_grad] + [unpack_small(a) for a in small]

    outs = []
    for k in range(4):
        sm_k = s_out[k]
        outs.append([sm_k["norm_g"], (g_ada_w, d_ada_w, nm_ada_w, nv_ada_w)[k], sm_k["ada_b"], o_win0[k], o_pw[k],
                     sm_k["pool_scale"], o_wout0[k], o_win1[k], sm_k["conv_w"], sm_k["conv_b"], o_wout1[k],
                     sm_k["final_g"]])
    return (loss, dx0[None], *outs[0], *outs[1], *outs[2], *outs[3])
```

```python
import functools

import jax
import jax.numpy as jnp
from jax import lax
from jax.experimental import pallas as pl
from jax.experimental.pallas import tpu as pltpu

F32 = jnp.float32
BF16 = jnp.bfloat16
MESH = pl.DeviceIdType.MESH

D = 1024
DI = 2048
DP = 1024
NE = 6144
NO = 8192
WINDOWS = (2, 4, 8, 16)
PG = 256
HD = 64
NCHIP = 4
NDEV = 8
EPS = 1e-6
INV_SQRT_HD = 0.125

LR, B1, B2, EPS_ADAM, WD, STEP = 0.001, 0.9, 0.999, 1e-08, 0.01, 10

TM = 512
TME = 256
CT = 512
BQ = 512
BK = 256
assert BQ == 2 * BK
HALO = 16
DEAD_LOG_WEIGHT = -104.0
VMEM_LIMIT = 56 * 1024 * 1024


def _dot(a, b):
    return jnp.dot(a, b, preferred_element_type=F32)


def _dot_nt(a, b):
    return lax.dot_general(a, b, (((1,), (1,)), ((), ())), preferred_element_type=F32)


def _dot_tn(a, b):
    return lax.dot_general(a, b, (((0,), (0,)), ((), ())), preferred_element_type=F32)


def _params(*sem):
    return pltpu.CompilerParams(dimension_semantics=sem, vmem_limit_bytes=VMEM_LIMIT)


def _sigmoid(v):
    return 0.5 * jnp.tanh(0.5 * v) + 0.5


def _rowsum(v):
    return jnp.sum(v, axis=0, keepdims=True)


def _modulated_norm(xv, vec_ref):
    r = lax.rsqrt(jnp.mean(xv * xv, axis=-1, keepdims=True) + EPS)
    return (((xv * r) * vec_ref[0:1, :]) * (1.0 + vec_ref[1:2, :]) + vec_ref[2:3, :]).astype(BF16)


def _norm_mod(x, vecs, name):
    s = x.shape[0]
    tm = min(TM, s)

    def body(x_ref, vec_ref, h_ref):
        h_ref[...] = _modulated_norm(x_ref[...], vec_ref)

    return pl.pallas_call(
        body, name=name, grid=(s // tm,),
        in_specs=[pl.BlockSpec((tm, D), lambda i: (i, 0)), pl.BlockSpec((8, D), lambda i: (0, 0))],
        out_specs=pl.BlockSpec((tm, D), lambda i: (i, 0)),
        out_shape=jax.ShapeDtypeStruct((s, D), BF16),
        compiler_params=_params("parallel"),
    )(x, vecs)


def _inproj(h, w, name, comm=None):
    s = h.shape[0]
    ns = w.shape[2]
    tm = min(TM, s)
    ni = s // tm
    comm = comm or _NO_COMM
    nci, nco = len(comm.arrays), len(comm.out_shape)

    def body(*refs):
        h_ref, w_ref = refs[:2]
        proj_ref = refs[2 + nci]
        cargs = (refs[2:2 + nci], refs[3 + nci:3 + nci + nco], refs[3 + nci + nco:])
        j, i = pl.program_id(0), pl.program_id(1)
        if nci:
            pl.when((j == 0) & (i == 0))(lambda: comm.start(*cargs))
        proj_ref[...] = _dot(h_ref[...], w_ref[...]).astype(BF16)
        if nci:
            pl.when((j == NCHIP - 1) & (i == ni - 1))(lambda: comm.finish(*cargs))

    return pl.pallas_call(
        body, name=name, grid=(NCHIP, ni),
        in_specs=[pl.BlockSpec((tm, D), lambda j, i: (i, 0)),
                  pl.BlockSpec((None, D, ns), lambda j, i: (j, 0, 0))] + [ANY] * nci,
        out_specs=[pl.BlockSpec((tm, ns), lambda j, i: (i, j))] + [ANY] * nco,
        out_shape=[jax.ShapeDtypeStruct((s, NCHIP * ns), BF16)] + comm.out_shape,
        scratch_shapes=comm.sems,
        compiler_params=_params("arbitrary", "arbitrary"),
    )(h, w, *comm.arrays)


def _inproj_gathering(h, w_shard, order, name, comm):
    s = h.shape[0]
    ns = w_shard.shape[1]
    r2 = D // 2
    tm = min(TM, s)
    ni = s // tm
    nci, nco = len(comm.arrays), len(comm.out_shape)

    def body(*refs):
        order_ref, h_ref, shard_hbm = refs[:3]
        proj_ref, full_hbm = refs[3 + nci:5 + nci]
        w_ref, ici_send, ici_recv, fwd_send, fwd_recv = refs[5 + nci + nco:10 + nci + nco]
        cargs = (refs[3:3 + nci], refs[5 + nci:5 + nci + nco], refs[10 + nci + nco:])
        j, i = pl.program_id(0), pl.program_id(1)
        x, y, c = _place()
        own, sibling = 2 * x + y, (x, y, 1 - c)
        mine, other = pl.ds(c * r2, r2), pl.ds((1 - c) * r2, r2)

        def copies(d, kinds):
            px, py = _flip(x, CHIP_FLIPS[d][0]), _flip(y, CHIP_FLIPS[d][1])
            peer = 2 * px + py
            ici = dict(send_sem=ici_send.at[d], recv_sem=ici_recv.at[d], device_id=(px, py, c), device_id_type=MESH)
            fwd = dict(send_sem=fwd_send.at[d], recv_sem=fwd_recv.at[d], device_id=sibling, device_id_type=MESH)
            made = {}
            if "ici_out" in kinds:
                made["ici_out"] = pltpu.make_async_remote_copy(
                    src_ref=shard_hbm.at[mine, :], dst_ref=full_hbm.at[own, mine, :], **ici)
            if "ici_in" in kinds:
                made["ici_in"] = pltpu.make_async_remote_copy(
                    src_ref=shard_hbm.at[mine, :], dst_ref=full_hbm.at[peer, mine, :], **ici)
            if "fwd_out" in kinds:
                made["fwd_out"] = pltpu.make_async_remote_copy(
                    src_ref=full_hbm.at[peer, mine, :], dst_ref=full_hbm.at[peer, mine, :], **fwd)
            if "fwd_in" in kinds:
                made["fwd_in"] = pltpu.make_async_remote_copy(
                    src_ref=full_hbm.at[peer, other, :], dst_ref=full_hbm.at[peer, other, :], **fwd)
            return made, peer

        neighbours, diagonal = (0, 1), 2

        @pl.when((j == 0) & (i == 0))
        def _():
            comm.start(*cargs)
            for d in neighbours:
                copies(d, ("ici_out",))[0]["ici_out"].start()
            pltpu.sync_copy(shard_hbm, w_ref)
            pltpu.sync_copy(w_ref, full_hbm.at[own])

        for d in range(len(CHIP_FLIPS)):
            @pl.when((j == d + 1) & (i == 0))
            def _(d=d):
                made, peer = copies(d, ("ici_in", "fwd_out", "fwd_in"))
                made["ici_in"].wait_recv()
                made["fwd_out"].start()
                made["fwd_in"].wait_recv()
                pltpu.sync_copy(full_hbm.at[peer], w_ref)
                if d == 0:
                    for n in neighbours:
                        copies(n, ("ici_out",))[0]["ici_out"].wait_send()
                    copies(diagonal, ("ici_out",))[0]["ici_out"].start()

        proj_ref[...] = _dot(h_ref[...], w_ref[...]).astype(BF16)

        @pl.when((j == NCHIP - 1) & (i == ni - 1))
        def _():
            copies(diagonal, ("ici_out",))[0]["ici_out"].wait_send()
            for d in range(len(CHIP_FLIPS)):
                copies(d, ("fwd_out",))[0]["fwd_out"].wait_send()
            comm.finish(*cargs)

    dma = pltpu.SemaphoreType.DMA
    return pl.pallas_call(
        body, name=name,
        grid_spec=pltpu.PrefetchScalarGridSpec(
            num_scalar_prefetch=1, grid=(NCHIP, ni),
            in_specs=[pl.BlockSpec((tm, D), lambda j, i, order_ref: (i, 0)), ANY] + [ANY] * nci,
            out_specs=[pl.BlockSpec((tm, ns), lambda j, i, order_ref: (i, order_ref[j])), ANY] + [ANY] * nco,
            scratch_shapes=[pltpu.VMEM((D, ns), BF16)] + [dma((3,))] * 4 + comm.sems),
        out_shape=[jax.ShapeDtypeStruct((s, NCHIP * ns), BF16),
                   jax.ShapeDtypeStruct((NCHIP, D, ns), BF16)] + comm.out_shape,
        compiler_params=_params("arbitrary", "arbitrary"),
    )(order, h, w_shard, *comm.arrays)


def _pool_fwd(proj0, pw, pscale):
    s = proj0.shape[0]
    tm = min(TM, s)
    hb = tm // 16

    def body(u_ref, halo_ref, w_ref, sc_ref, p_ref, y_ref, ext_ref):
        i = pl.program_id(0)
        ext_ref[16:, :] = u_ref[...].astype(F32)
        ext_ref[0:16, :] = jnp.where(i > 0, halo_ref[...].astype(F32), 0.0)
        t = i * tm + lax.broadcasted_iota(jnp.int32, (tm, 1), 0)
        for g, wdw in enumerate(WINDOWS):
            cs = slice(g * PG, (g + 1) * PG)
            u = ext_ref[16:16 + tm, cs]
            acc = u
            for j in range(1, wdw):
                acc = acc + ext_ref[16 - j:16 - j + tm, cs]
            inv = 1.0 / jnp.minimum(t + 1, wdw).astype(F32)
            pb = (acc * inv - u).astype(BF16)
            p_ref[:, cs] = pb
            y_ref[:, cs] = _dot(pb, w_ref[g]) * sc_ref[:, cs]

    return pl.pallas_call(
        body, name="pool_fwd", grid=(s // tm,),
        in_specs=[pl.BlockSpec((tm, DP), lambda i: (i, 0)),
                  pl.BlockSpec((16, DP), lambda i: (jnp.maximum(i * hb - 1, 0), 0)),
                  pl.BlockSpec((4, PG, PG), lambda i: (0, 0, 0)),
                  pl.BlockSpec((1, DP), lambda i: (0, 0))],
        out_specs=[pl.BlockSpec((tm, DP), lambda i: (i, 0)),
                   pl.BlockSpec((tm, DP), lambda i: (i, 0))],
        out_shape=[jax.ShapeDtypeStruct((s, DP), BF16), jax.ShapeDtypeStruct((s, DP), F32)],
        scratch_shapes=[pltpu.VMEM((tm + 16, DP), F32)],
        compiler_params=_params("parallel"),
    )(proj0, proj0, pw, pscale)


def _sb_logits(nz, mask):
    neg_abs = lax.bitcast_convert_type(lax.bitcast_convert_type(nz, jnp.uint32) | jnp.uint32(0x80000000), F32)
    t = jnp.log(1.0 + jnp.exp(neg_abs))
    lf = jnp.minimum(nz, 0.0) - t
    lam = lf - nz
    if mask is not None:
        lf = jnp.where(mask, lf, 0.0)
    return lf, lam


def _sweep_left(steps, state, step):
    def live(carry):
        j, st = carry
        heaviest = functools.reduce(jnp.maximum, [jnp.max(head[1]) for head in st])
        return (j < steps) & (heaviest > DEAD_LOG_WEIGHT)

    return lax.while_loop(live, lambda carry: (carry[0] + 1, step(carry[0], carry[1])), (0, state))[1]


def _split_dot(v, tri):
    hi = v.astype(BF16)
    lo = (v - hi.astype(F32)).astype(BF16)
    return _dot(hi, tri) + _dot(lo, tri)


def _tri_masks():
    row = lax.broadcasted_iota(jnp.int32, (BK, BK), 0)
    col = lax.broadcasted_iota(jnp.int32, (BK, BK), 1)
    return (row > col).astype(BF16), (row >= col).astype(BF16)


def _causal_mask():
    row = lax.broadcasted_iota(jnp.int32, (BK, BK), 0)
    col = lax.broadcasted_iota(jnp.int32, (BK, BK), 1)
    return col < row


def _attn_fwd(proj0, comm=None):
    s = proj0.shape[0]
    nq = s // BQ
    kpq = BQ // BK
    nsteps = DP // 128
    comm = comm or _NO_COMM
    nci, nco = len(comm.arrays), len(comm.out_shape)

    def body(*refs):
        q_ref, k_ref, v_ref = refs[:3]
        o_ref = refs[3 + nci]
        qn_ref, k8_ref, vb_ref = refs[4 + nci + nco:7 + nci + nco]
        cargs = (refs[3:3 + nci], refs[4 + nci:4 + nci + nco], refs[7 + nci + nco:])
        if nci:
            pl.when(pl.program_id(0) == 0)(lambda: comm.start(*cargs))
        qn_ref[...] = (-q_ref[...]).astype(BF16)
        k8_ref[...] = (k_ref[...] * INV_SQRT_HD).astype(BF16)
        vb_ref[...] = v_ref[...].astype(BF16)
        after, _ = _tri_masks()
        causal = _causal_mask()
        heads = [slice(HD * h, HD * (h + 1)) for h in range(2)]
        hs = range(len(heads))
        lo, hi = (0, BK), (BK, BK)

        def qstep(qi, carry):
            q0 = pl.multiple_of(qi * BQ, BQ)
            qn = [qn_ref[pl.ds(q0, BQ), ls] for ls in heads]

            def sweep(lanes, state):
                nz = [_dot_nt(qn[h][r0:r0 + n], k8_ref[pl.ds(k0, BK), heads[h]]) for h, (r0, n), k0, _ in lanes]
                ll = [_sb_logits(nz[i], lane[3]) for i, lane in enumerate(lanes)]
                aft = [_dot(ll[i][0].astype(BF16), after) for i in range(len(lanes))]
                state = dict(state)
                for i, (h, rows, k0, mask) in enumerate(lanes):
                    o_acc, c = state[h, rows]
                    a = jnp.exp(ll[i][1] + aft[i] + c)
                    if mask is not None:
                        a = jnp.where(mask, a, 0.0)
                    o_acc = o_acc + _dot(a.astype(BF16), vb_ref[pl.ds(k0, BK), heads[h]])
                    state[h, rows] = (o_acc, c + aft[i][:, 0:1] + ll[i][0][:, 0:1])
                return state

            k_lo, k_hi = q0, pl.multiple_of(q0 + BK, BK)
            st = {(h, r): (jnp.zeros((BK, HD), F32), jnp.zeros((BK, 1), F32)) for h in hs for r in (lo, hi)}
            st = sweep([(h, hi, k_hi, causal) for h in hs] + [(h, lo, k_lo, causal) for h in hs]
                       + [(h, hi, k_lo, None) for h in hs], st)
            def left_of(rows):
                def step(j, part):
                    k0 = pl.multiple_of(q0 - (j + 1) * BK, BK)
                    got = sweep([(h, rows, k0, None) for h in hs], {(h, rows): part[h] for h in hs})
                    return tuple(got[h, rows] for h in hs)
                return step

            done = {r: _sweep_left(qi * (BQ // BK), tuple(st[h, r] for h in hs), left_of(r)) for r in (lo, hi)}
            for h, ls in enumerate(heads):
                o_ref[pl.ds(q0, BQ), ls] = jnp.concatenate([done[lo][h][0], done[hi][h][0]], axis=0)
            return carry

        lax.fori_loop(0, nq, qstep, 0)
        if nci:
            pl.when(pl.program_id(0) == nsteps - 1)(lambda: comm.finish(*cargs))

    return pl.pallas_call(
        body, name="attn_fwd", grid=(nsteps,),
        in_specs=[pl.BlockSpec((s, 128), lambda h: (0, 8 + h)),
                  pl.BlockSpec((s, 128), lambda h: (0, 16 + h)),
                  pl.BlockSpec((s, 128), lambda h: (0, 24 + h))] + [ANY] * nci,
        out_specs=[pl.BlockSpec((s, 128), lambda h: (0, h))] + [ANY] * nco,
        out_shape=[jax.ShapeDtypeStruct((s, DP), F32)] + comm.out_shape,
        scratch_shapes=[pltpu.VMEM((s, 128), BF16)] * 3 + comm.sems,
        compiler_params=_params("arbitrary"),
    )(proj0, proj0, proj0, *comm.arrays)


def _even_out(ypool, ysb, proj0, wout, x, vecs, vecs_next):
    s = x.shape[0]
    tm = min(TME, s)

    def body(yp_ref, ys_ref, gate_ref, w_ref, x_ref, vec_ref, vecn_ref, x1_ref, out_ref, yg_ref, hn_ref):
        gt = gate_ref[...].astype(F32)
        sl = gt * _sigmoid(gt)
        yg_ref[:, :DP] = (yp_ref[...] * sl[:, :DP]).astype(BF16)
        yg_ref[:, DP:] = (ys_ref[...] * sl[:, DP:]).astype(BF16)
        out = _dot(yg_ref[...], w_ref[...])
        out_ref[...] = out
        x1 = x_ref[...] + (1.0 + vec_ref[3:4, :]) * out
        x1_ref[...] = x1
        hn_ref[...] = _modulated_norm(x1, vecn_ref)

    row = lambda i: (i, 0)
    const = lambda i: (0, 0)
    return pl.pallas_call(
        body, name="even_out", grid=(s // tm,),
        in_specs=[pl.BlockSpec((tm, DP), row), pl.BlockSpec((tm, DP), row),
                  pl.BlockSpec((tm, DI), lambda i: (i, 2)),
                  pl.BlockSpec((DI, D), const),
                  pl.BlockSpec((tm, D), row), pl.BlockSpec((8, D), const), pl.BlockSpec((8, D), const)],
        out_specs=[pl.BlockSpec((tm, D), row), pl.BlockSpec((tm, D), row), pl.BlockSpec((tm, DI), row),
                   pl.BlockSpec((tm, D), row)],
        out_shape=[jax.ShapeDtypeStruct((s, D), F32), jax.ShapeDtypeStruct((s, D), F32),
                   jax.ShapeDtypeStruct((s, DI), BF16), jax.ShapeDtypeStruct((s, D), BF16)],
        compiler_params=_params("parallel"),
    )(ypool, ysb, proj0, wout, x, vecs, vecs_next)


def _odd_out(proj1, wout, x1, vecs, cw, cb, target):
    s = x1.shape[0]
    tm = min(TME, s)
    hb = tm // HALO

    def body(gb_ref, gc_ref, u_ref, gt_ref, hgc_ref, hu_ref, w_ref, x1_ref, vec_ref, cw_ref, cb_ref, tg_ref,
             dx2_ref, y1_ref, acc_ref, ext_ref):
        i = pl.program_id(0)

        @pl.when(i == 0)
        def _():
            acc_ref[...] = jnp.zeros_like(acc_ref)

        ext_ref[HALO:, :] = gc_ref[...].astype(F32) * u_ref[...].astype(F32)
        ext_ref[0:HALO, :] = jnp.where(i > 0, hgc_ref[...].astype(F32) * hu_ref[...].astype(F32), 0.0)
        for c in range(DI // CT):
            cs = slice(c * CT, (c + 1) * CT)
            conv = (cb_ref[0:1, cs] + cw_ref[0:1, cs] * ext_ref[HALO - 2:HALO - 2 + tm, cs]
                    + cw_ref[1:2, cs] * ext_ref[HALO - 1:HALO - 1 + tm, cs]
                    + cw_ref[2:3, cs] * ext_ref[HALO:HALO + tm, cs])
            gt = gt_ref[:, cs].astype(F32)
            y1_ref[:, cs] = (gb_ref[:, cs].astype(F32) * conv * (gt * _sigmoid(gt))).astype(BF16)
        out = _dot(y1_ref[...], w_ref[...])
        x2 = x1_ref[...] + (1.0 + vec_ref[3:4, :]) * out
        r = lax.rsqrt(jnp.mean(x2 * x2, axis=-1, keepdims=True) + EPS)
        nrm = x2 * r
        fg = vec_ref[4:5, :]
        err = nrm * fg - tg_ref[...]
        acc_ref[1:2, :] += _rowsum(err * err) * (0.5 / D)
        dyf = err * (1.0 / D)
        acc_ref[0:1, :] += _rowsum(dyf * nrm)
        dn = dyf * fg
        dx2 = r * (dn - nrm * jnp.mean(dn * nrm, axis=-1, keepdims=True))
        dx2_ref[...] = dx2
        acc_ref[2:3, :] += _rowsum(dx2 * out)

    row = lambda i: (i, 0)
    halo = lambda col: (lambda i: (jnp.maximum(i * hb - 1, 0), col))
    const = lambda i: (0, 0)
    return pl.pallas_call(
        body, name="odd_out", grid=(s // tm,),
        in_specs=[pl.BlockSpec((tm, DI), lambda i: (i, 0)), pl.BlockSpec((tm, DI), lambda i: (i, 1)),
                  pl.BlockSpec((tm, DI), lambda i: (i, 2)), pl.BlockSpec((tm, DI), lambda i: (i, 3)),
                  pl.BlockSpec((HALO, DI), halo(1)), pl.BlockSpec((HALO, DI), halo(2)),
                  pl.BlockSpec((DI, D), const), pl.BlockSpec((tm, D), row), pl.BlockSpec((8, D), const),
                  pl.BlockSpec((8, DI), const), pl.BlockSpec((1, DI), const), pl.BlockSpec((tm, D), row)],
        out_specs=[pl.BlockSpec((tm, D), row), pl.BlockSpec((tm, DI), row), pl.BlockSpec((8, D), const)],
        out_shape=[jax.ShapeDtypeStruct((s, D), F32), jax.ShapeDtypeStruct((s, DI), BF16),
                   jax.ShapeDtypeStruct((8, D), F32)],
        scratch_shapes=[pltpu.VMEM((tm + HALO, DI), F32)],
        compiler_params=_params("arbitrary"),
    )(proj1, proj1, proj1, proj1, proj1, proj1, wout, x1, vecs, cw, cb, target)


def _odd_bwd(dx2, proj1, wout, vecs, cw, cb):
    s = dx2.shape[0]
    tm = min(TME, s)
    nb = s // tm
    hb = tm // HALO

    def body(dx2_ref, gb_ref, gc_ref, u_ref, gt_ref, hgc_ref, hu_ref, w_ref, vec_ref, cw_ref, cb_ref,
             dout_ref, dproj_ref, accv_ref, uext_ref, dext_ref, dy_ref):
        i = pl.program_id(0)
        blk = nb - 1 - i

        @pl.when(i == 0)
        def _():
            accv_ref[...] = jnp.zeros_like(accv_ref)
            dext_ref[tm:tm + 8, :] = jnp.zeros((8, DI), F32)

        dout = (dx2_ref[...] * (1.0 + vec_ref[3:4, :])).astype(BF16)
        dout_ref[...] = dout
        dy_ref[...] = _dot_nt(dout, w_ref[...])
        uext_ref[HALO:, :] = gc_ref[...].astype(F32) * u_ref[...].astype(F32)
        uext_ref[0:HALO, :] = jnp.where(blk > 0, hgc_ref[...].astype(F32) * hu_ref[...].astype(F32), 0.0)
        for c in range(DI // CT):
            cs = slice(c * CT, (c + 1) * CT)
            u0 = uext_ref[HALO - 2:HALO - 2 + tm, cs]
            u1 = uext_ref[HALO - 1:HALO - 1 + tm, cs]
            u2 = uext_ref[HALO:HALO + tm, cs]
            w0, w1, w2 = cw_ref[0:1, cs], cw_ref[1:2, cs], cw_ref[2:3, cs]
            conv = cb_ref[0:1, cs] + w0 * u0 + w1 * u1 + w2 * u2
            gt = gt_ref[:, cs].astype(F32)
            sg = _sigmoid(gt)
            gb = gb_ref[:, cs].astype(F32)
            dy = dy_ref[:, cs]
            t1 = dy * (gt * sg)
            dproj_ref[:, cs] = (t1 * conv).astype(BF16)
            dconv = t1 * gb
            dproj_ref[:, 3 * DI + c * CT:3 * DI + (c + 1) * CT] = (
                dy * gb * conv * (sg * (1.0 + gt * (1.0 - sg)))).astype(BF16)
            accv_ref[0:1, cs] += _rowsum(dconv * u0)
            accv_ref[1:2, cs] += _rowsum(dconv * u1)
            accv_ref[2:3, cs] += _rowsum(dconv * u2)
            accv_ref[3:4, cs] += _rowsum(dconv)
            dext_ref[0:tm, cs] = dconv
            duu = w2 * dconv + w1 * dext_ref[1:tm + 1, cs] + w0 * dext_ref[2:tm + 2, cs]
            dproj_ref[:, DI + c * CT:DI + (c + 1) * CT] = (duu * u_ref[:, cs].astype(F32)).astype(BF16)
            dproj_ref[:, 2 * DI + c * CT:2 * DI + (c + 1) * CT] = (duu * gc_ref[:, cs].astype(F32)).astype(BF16)
        dext_ref[tm:tm + 8, :] = dext_ref[0:8, :]

    rrow = lambda i: (nb - 1 - i, 0)
    rcol = lambda col: (lambda i: (nb - 1 - i, col))
    halo = lambda col: (lambda i: (jnp.maximum((nb - 1 - i) * hb - 1, 0), col))
    const = lambda i: (0, 0)
    return pl.pallas_call(
        body, name="odd_bwd", grid=(nb,),
        in_specs=[pl.BlockSpec((tm, D), rrow),
                  pl.BlockSpec((tm, DI), rcol(0)), pl.BlockSpec((tm, DI), rcol(1)),
                  pl.BlockSpec((tm, DI), rcol(2)), pl.BlockSpec((tm, DI), rcol(3)),
                  pl.BlockSpec((HALO, DI), halo(1)), pl.BlockSpec((HALO, DI), halo(2)),
                  pl.BlockSpec((DI, D), const), pl.BlockSpec((8, D), const),
                  pl.BlockSpec((8, DI), const), pl.BlockSpec((1, DI), const)],
        out_specs=[pl.BlockSpec((tm, D), rrow), pl.BlockSpec((tm, NO), rrow), pl.BlockSpec((8, DI), const)],
        out_shape=[jax.ShapeDtypeStruct((s, D), BF16), jax.ShapeDtypeStruct((s, NO), BF16),
                   jax.ShapeDtypeStruct((8, DI), F32)],
        scratch_shapes=[pltpu.VMEM((tm + HALO, DI), F32), pltpu.VMEM((tm + 8, DI), F32), pltpu.VMEM((tm, DI), F32)],
        compiler_params=_params("arbitrary"),
    )(dx2, proj1, proj1, proj1, proj1, proj1, proj1, wout, vecs, cw, cb)


def _grad_w_body(nk):
    def body(a_ref, b_ref, o_ref, ob_ref):
        k = pl.program_id(1)

        @pl.when(k == 0)
        def _():
            o_ref[...] = jnp.zeros_like(o_ref)

        o_ref[...] += _dot_tn(a_ref[...], b_ref[...])

        @pl.when(k == nk - 1)
        def _():
            ob_ref[...] = o_ref[...].astype(BF16)

    return body


def _grad_w_cols(a, b, name):
    s, m = a.shape
    ns = b.shape[1] // NCHIP
    ts = min(TM, s)
    blk = pl.BlockSpec((None, m, ns), lambda j, k: (j, 0, 0))
    return pl.pallas_call(
        _grad_w_body(s // ts), name=name, grid=(NCHIP, s // ts),
        in_specs=[pl.BlockSpec((ts, m), lambda j, k: (k, 0)),
                  pl.BlockSpec((ts, ns), lambda j, k: (k, j))],
        out_specs=[blk, blk],
        out_shape=[jax.ShapeDtypeStruct((NCHIP, m, ns), F32), jax.ShapeDtypeStruct((NCHIP, m, ns), BF16)],
        compiler_params=_params("parallel", "arbitrary"),
    )(a, b)


def _grad_w_rows(a, b, name):
    s = a.shape[0]
    ms = a.shape[1] // NCHIP
    n = b.shape[1]
    ts = min(TM, s)
    blk = pl.BlockSpec((None, ms, n), lambda i, k: (i, 0, 0))
    return pl.pallas_call(
        _grad_w_body(s // ts), name=name, grid=(NCHIP, s // ts),
        in_specs=[pl.BlockSpec((ts, ms), lambda i, k: (k, i)),
                  pl.BlockSpec((ts, n), lambda i, k: (k, 0))],
        out_specs=[blk, blk],
        out_shape=[jax.ShapeDtypeStruct((NCHIP, ms, n), F32), jax.ShapeDtypeStruct((NCHIP, ms, n), BF16)],
        compiler_params=_params("parallel", "arbitrary"),
    )(a, b)


def _inproj_bwd(dproj, w, x, dx_in, vecs, name, comm=None):
    s = x.shape[0]
    ns = w.shape[2]
    tm = min(TME, s)
    ni = s // tm
    comm = comm or _NO_COMM
    nci, nco = len(comm.arrays), len(comm.out_shape)

    def body(*refs):
        dp_ref, w_hbm, x_ref, dxin_ref, vec_ref = refs[:5]
        dx_ref, acc_ref = refs[5 + nci:7 + nci]
        w_ref = refs[7 + nci + nco]
        cargs = (refs[5:5 + nci], refs[7 + nci:7 + nci + nco], refs[8 + nci + nco:])
        i = pl.program_id(0)

        @pl.when(i == 0)
        def _():
            acc_ref[...] = jnp.zeros_like(acc_ref)
            if nci:
                comm.start(*cargs)
            pltpu.sync_copy(w_hbm, w_ref)

        dh = _dot_nt(dp_ref[:, 0:ns], w_ref[0])
        for j in range(1, NCHIP):
            dh = dh + _dot_nt(dp_ref[:, j * ns:(j + 1) * ns], w_ref[j])
        xv = x_ref[...]
        r = lax.rsqrt(jnp.mean(xv * xv, axis=-1, keepdims=True) + EPS)
        nrm = xv * r
        g = vec_ref[0:1, :]
        sc1 = 1.0 + vec_ref[1:2, :]
        dhn = dh * nrm
        acc_ref[0:1, :] += _rowsum(dh)
        acc_ref[1:2, :] += _rowsum(dhn) * g
        acc_ref[2:3, :] += _rowsum(dhn) * sc1
        dn = dh * (g * sc1)
        dx_ref[...] = dxin_ref[...] + r * (dn - nrm * jnp.mean(dn * nrm, axis=-1, keepdims=True))

        if nci:
            pl.when(i == ni - 1)(lambda: comm.finish(*cargs))

    row = lambda i: (i, 0)
    const = lambda i: (0, 0)
    return pl.pallas_call(
        body, name=name, grid=(ni,),
        in_specs=[pl.BlockSpec((tm, NCHIP * ns), row), ANY,
                  pl.BlockSpec((tm, D), row), pl.BlockSpec((tm, D), row), pl.BlockSpec((8, D), const)] + [ANY] * nci,
        out_specs=[pl.BlockSpec((tm, D), row), pl.BlockSpec((8, D), const)] + [ANY] * nco,
        out_shape=[jax.ShapeDtypeStruct((s, D), F32), jax.ShapeDtypeStruct((8, D), F32)] + comm.out_shape,
        scratch_shapes=[pltpu.VMEM(w.shape, BF16)] + comm.sems,
        compiler_params=_params("arbitrary"),
    )(dproj, w, x, dx_in, vecs, *comm.arrays)


def _even_bwd(dx1, out0, ypool, ysb, proj0, wout, vecs):
    s = dx1.shape[0]
    tm = min(TME, s)

    def body(dx1_ref, out0_ref, yp_ref, ys_ref, gate_ref, w_ref, vec_ref,
             dout_ref, dyp_ref, dys_ref, dgt_ref, acc_ref):
        @pl.when(pl.program_id(0) == 0)
        def _():
            acc_ref[...] = jnp.zeros_like(acc_ref)

        dx1v = dx1_ref[...]
        acc_ref[0:1, :] += _rowsum(dx1v * out0_ref[...])
        dout = (dx1v * (1.0 + vec_ref[3:4, :])).astype(BF16)
        dout_ref[...] = dout
        dyg = _dot_nt(dout, w_ref[...])
        gt = gate_ref[...].astype(F32)
        sg = _sigmoid(gt)
        sl = gt * sg
        dsl = sg * (1.0 + gt * (1.0 - sg))
        dyp_ref[...] = dyg[:, :DP] * sl[:, :DP]
        dys_ref[...] = dyg[:, DP:] * sl[:, DP:]
        dgt_ref[:, :DP] = (dyg[:, :DP] * yp_ref[...] * dsl[:, :DP]).astype(BF16)
        dgt_ref[:, DP:] = (dyg[:, DP:] * ys_ref[...] * dsl[:, DP:]).astype(BF16)

    row = lambda i: (i, 0)
    const = lambda i: (0, 0)
    return pl.pallas_call(
        body, name="even_bwd", grid=(s // tm,),
        in_specs=[pl.BlockSpec((tm, D), row), pl.BlockSpec((tm, D), row),
                  pl.BlockSpec((tm, DP), row), pl.BlockSpec((tm, DP), row),
                  pl.BlockSpec((tm, DI), lambda i: (i, 2)),
                  pl.BlockSpec((DI, D), const), pl.BlockSpec((8, D), const)],
        out_specs=[pl.BlockSpec((tm, D), row), pl.BlockSpec((tm, DP), row), pl.BlockSpec((tm, DP), row),
                   pl.BlockSpec((tm, DI), lambda i: (i, NE // DI - 1)), pl.BlockSpec((8, D), const)],
        out_shape=[jax.ShapeDtypeStruct((s, D), BF16), jax.ShapeDtypeStruct((s, DP), F32),
                   jax.ShapeDtypeStruct((s, DP), F32), jax.ShapeDtypeStruct((s, NE), BF16),
                   jax.ShapeDtypeStruct((8, D), F32)],
        compiler_params=_params("arbitrary"),
    )(dx1, out0, ypool, ysb, proj0, wout, vecs)


def _pool_bwd(dyp, p, pw, pscale, dproj):
    s = dyp.shape[0]
    tm = min(TM, s)
    nb = s // tm
    hb = tm // 16

    def body(dy_ref, dyh_ref, p_ref, w_ref, sc_ref, dproj_hbm, du_ref, dw_ref, acc_ref, ext_ref):
        i = pl.program_id(0)

        @pl.when(i == 0)
        def _():
            dw_ref[...] = jnp.zeros_like(dw_ref)
            acc_ref[...] = jnp.zeros_like(acc_ref)

        t = i * tm + lax.broadcasted_iota(jnp.int32, (tm + 16, 1), 0)
        for g, wdw in enumerate(WINDOWS):
            cs = slice(g * PG, (g + 1) * PG)
            sc = sc_ref[:, cs]
            dy = dy_ref[:, cs]
            dyh = jnp.where(i < nb - 1, dyh_ref[:, cs], 0.0)
            pb = p_ref[:, cs]
            wg = w_ref[g]
            acc_ref[0:1, cs] += _rowsum(dy * _dot(pb, wg))
            dypre = (dy * sc).astype(BF16)
            dw_ref[g] += _dot_tn(pb, dypre)
            dp = _dot_nt(dypre, wg)
            dph = _dot_nt((dyh * sc).astype(BF16), wg)
            inv = 1.0 / jnp.minimum(t + 1, wdw).astype(F32)
            ext_ref[0:tm, cs] = dp * inv[0:tm]
            ext_ref[tm:tm + 16, cs] = dph * inv[tm:tm + 16]
            acc = ext_ref[0:tm, cs]
            for j in range(1, wdw):
                acc = acc + ext_ref[j:j + tm, cs]
            du_ref[:, cs] = (acc - dp).astype(BF16)

    row = lambda i: (i, 0)
    return pl.pallas_call(
        body, name="pool_bwd", grid=(nb,),
        in_specs=[pl.BlockSpec((tm, DP), row),
                  pl.BlockSpec((16, DP), lambda i: (jnp.minimum((i + 1) * hb, s // 16 - 1), 0)),
                  pl.BlockSpec((tm, DP), row),
                  pl.BlockSpec((4, PG, PG), lambda i: (0, 0, 0)),
                  pl.BlockSpec((1, DP), lambda i: (0, 0)), ANY],
        out_specs=[pl.BlockSpec((tm, DP), row), pl.BlockSpec((4, PG, PG), lambda i: (0, 0, 0)),
                   pl.BlockSpec((8, DP), lambda i: (0, 0))],
        out_shape=[jax.ShapeDtypeStruct((s, NE), BF16), jax.ShapeDtypeStruct((4, PG, PG), F32),
                   jax.ShapeDtypeStruct((8, DP), F32)],
        input_output_aliases={5: 0},
        scratch_shapes=[pltpu.VMEM((tm + 16, DP), F32)],
        compiler_params=_params("arbitrary"),
    )(dyp, dyp, p, pw, pscale, dproj)


def _attn_bwd(proj0, ysb, dys, dproj, comm=None):
    s = proj0.shape[0]
    nq = s // BQ
    kpq = BQ // BK
    nsteps = DP // 128
    comm = comm or _NO_COMM
    nci, nco = len(comm.arrays), len(comm.out_shape)

    def body(*refs):
        q_ref, k_ref, v_ref, o_ref, do_ref = refs[:5]
        dproj_hbm = refs[6 + nci]
        scratch = refs[7 + nci + nco:]
        qn_ref, k8_ref, vb_ref, dob_ref, dka_ref, dva_ref, dq_ref, dk_ref, dv_ref, part_sems = scratch[:10]
        cargs = (refs[6:6 + nci], refs[7 + nci:7 + nci + nco], scratch[10:])
        step = pl.program_id(0)
        if nci:
            pl.when(step == 0)(lambda: comm.start(*cargs))

        qn_ref[...] = (-q_ref[...]).astype(BF16)
        k8_ref[...] = (k_ref[...] * INV_SQRT_HD).astype(BF16)
        vb_ref[...] = v_ref[...].astype(BF16)
        dob_ref[...] = do_ref[...].astype(BF16)
        dka_ref[...] = jnp.zeros_like(dka_ref)
        dva_ref[...] = jnp.zeros_like(dva_ref)
        after, from_on = _tri_masks()
        causal = _causal_mask()
        heads = [slice(HD * h, HD * (h + 1)) for h in range(2)]
        hs = range(len(heads))
        lo, hi = (0, BK), (BK, BK)

        def qstep(qi, carry):
            q0 = pl.multiple_of(qi * BQ, BQ)
            qn = [qn_ref[pl.ds(q0, BQ), ls] for ls in heads]
            do = [dob_ref[pl.ds(q0, BQ), ls] for ls in heads]
            total = [jnp.sum(do[h].astype(F32) * o_ref[pl.ds(q0, BQ), ls], axis=1, keepdims=True)
                     for h, ls in enumerate(heads)]

            def sweep(lanes, state):
                rows_of = lambda i: slice(lanes[i][1][0], lanes[i][1][0] + lanes[i][1][1])
                k8 = [k8_ref[pl.ds(k0, BK), heads[h]] for h, _, k0, _ in lanes]
                nz = [_dot_nt(qn[lane[0]][rows_of(i)], k8[i]) for i, lane in enumerate(lanes)]
                da = [_dot_nt(do[h][rows_of(i)], vb_ref[pl.ds(k0, BK), heads[h]]) for i, (h, _, k0, _) in enumerate(lanes)]
                ll = [_sb_logits(nz[i], lane[3]) for i, lane in enumerate(lanes)]
                aft = [_dot(ll[i][0].astype(BF16), after) for i in range(len(lanes))]
                state = dict(state)
                for i, (h, rows, k0, mask) in enumerate(lanes):
                    dq_acc, c, cg = state[h, rows]
                    a = jnp.exp(ll[i][1] + aft[i] + c)
                    if mask is not None:
                        a = jnp.where(mask, a, 0.0)
                    ab = a.astype(BF16)
                    g = da[i] * ab.astype(F32)
                    suf = _split_dot(g, from_on)
                    dz = g - jnp.exp(ll[i][1]) * (g + ((total[h][rows_of(i)] - cg) - suf))
                    if mask is not None:
                        dz = jnp.where(mask, dz, 0.0)
                    dzb = dz.astype(BF16)
                    dka_ref[pl.ds(k0, BK), heads[h]] += _dot_tn(dzb, qn[h][rows_of(i)])
                    dva_ref[pl.ds(k0, BK), heads[h]] += _dot_tn(ab, do[h][rows_of(i)])
                    state[h, rows] = (dq_acc + _dot(dzb, k8[i]), c + aft[i][:, 0:1] + ll[i][0][:, 0:1],
                                      cg + suf[:, 0:1])
                return state

            k_lo, k_hi = q0, pl.multiple_of(q0 + BK, BK)
            zero = jnp.zeros((BK, 1), F32)
            st = {(h, r): (jnp.zeros((BK, HD), F32), zero, zero) for h in hs for r in (lo, hi)}
            st = sweep([(h, hi, k_hi, causal) for h in hs] + [(h, lo, k_lo, causal) for h in hs]
                       + [(h, hi, k_lo, None) for h in hs], st)
            def left_of(rows):
                def step(j, part):
                    k0 = pl.multiple_of(q0 - (j + 1) * BK, BK)
                    got = sweep([(h, rows, k0, None) for h in hs], {(h, rows): part[h] for h in hs})
                    return tuple(got[h, rows] for h in hs)
                return step

            done = {r: _sweep_left(qi * (BQ // BK), tuple(st[h, r] for h in hs), left_of(r)) for r in (lo, hi)}
            for h, ls in enumerate(heads):
                dq_ref[pl.ds(q0, BQ), ls] = jnp.concatenate([done[lo][h][0], done[hi][h][0]], axis=0).astype(BF16)
            return carry

        lax.fori_loop(0, nq, qstep, 0)
        dk_ref[...] = (dka_ref[...] * (-INV_SQRT_HD)).astype(BF16)
        dv_ref[...] = dva_ref[...].astype(BF16)
        lanes0 = pl.multiple_of(step * 128, 128)
        parts = [pltpu.make_async_copy(src, dproj_hbm.at[:, pl.ds((1 + k) * DP + lanes0, 128)], part_sems.at[k])
                 for k, src in enumerate((dq_ref, dk_ref, dv_ref))]
        for cp in parts:
            cp.start()
        for cp in parts:
            cp.wait()
        if nci:
            pl.when(step == nsteps - 1)(lambda: comm.finish(*cargs))

    col = lambda h: (0, h)
    return pl.pallas_call(
        body, name="attn_bwd", grid=(nsteps,),
        in_specs=[pl.BlockSpec((s, 128), lambda h: (0, 8 + h)),
                  pl.BlockSpec((s, 128), lambda h: (0, 16 + h)),
                  pl.BlockSpec((s, 128), lambda h: (0, 24 + h)),
                  pl.BlockSpec((s, 128), col), pl.BlockSpec((s, 128), col), ANY] + [ANY] * nci,
        out_specs=[ANY] * (1 + nco),
        out_shape=[jax.ShapeDtypeStruct((s, NE), BF16)] + comm.out_shape,
        input_output_aliases={5: 0},
        scratch_shapes=([pltpu.VMEM((s, 128), BF16)] * 4 + [pltpu.VMEM((s, 128), F32)] * 2
                        + [pltpu.VMEM((s, 128), BF16)] * 3 + [pltpu.SemaphoreType.DMA((3,))] + comm.sems),
        compiler_params=_params("arbitrary"),
    )(proj0, proj0, proj0, ysb, dys, dproj, *comm.arrays)


def _adamw_math(w, g, m, v):
    m2 = B1 * m + (1.0 - B1) * g
    v2 = B2 * v + (1.0 - B2) * (g * g)
    m_hat = m2 / (1.0 - B1 ** STEP)
    v_hat = v2 / (1.0 - B2 ** STEP)
    return -LR * (m_hat / (jnp.sqrt(v_hat) + EPS_ADAM) + WD * w), m2, v2


def _adamw(w, g, m, v, name):
    r, c = w.shape
    tr = r
    while tr * c * 4 > (1 << 20) and tr % 16 == 0:
        tr //= 2

    def body(w_ref, g_ref, m_ref, v_ref, d_ref, m2_ref, v2_ref):
        d_ref[...], m2_ref[...], v2_ref[...] = _adamw_math(w_ref[...], g_ref[...], m_ref[...], v_ref[...])

    spec = pl.BlockSpec((tr, c), lambda i: (i, 0))
    return pl.pallas_call(
        body, name=name, grid=(r // tr,),
        in_specs=[spec] * 4, out_specs=[spec] * 3,
        out_shape=[jax.ShapeDtypeStruct((r, c), F32)] * 3,
        compiler_params=_params("parallel"),
    )(w, g, m, v)


def _local_step(x, target, vecs0, vecs1, win0, rest, pscale, cw8, cb, sel=None):
    dist = sel is not None
    h0 = _norm_mod(x, vecs0, "norm0")
    if dist:
        chip = sel[0]
        order = jnp.stack([chip, chip ^ 2, chip ^ 1, chip ^ 3])
        proj0, win0, *got = _inproj_gathering(h0, win0, order, "inproj0", _gather_comm(list(rest[0:2])))
        pw, wout0 = _group_major(got[0]), got[1].reshape(DI, D)
    else:
        proj0, = _inproj(h0, win0, "inproj0")
        pw, wout0 = rest[0:2]
    p, ypool = _pool_fwd(proj0, pw, pscale)
    ysb, *got = _attn_fwd(proj0, _gather_comm(list(rest[2:4])) if dist else None)
    win1, wout1 = (got[0], got[1].reshape(DI, D)) if dist else rest[2:4]
    x1, out0, yg, h1 = _even_out(ypool, ysb, proj0, wout0, x, vecs0, vecs1)
    proj1, = _inproj(h1, win1, "inproj1")
    dx2, y1, acc_f = _odd_out(proj1, wout1, x1, vecs1, cw8, cb, target)

    def chip_partials(grads, names):
        from_sibling = _send_halves([g16 for _, g16 in grads], "rs_send_halves_" + names[0])
        part = [_add_halves(g32, t, sel[1:2], "rs_add_halves_" + nm)
                for (g32, _), t, nm in zip(grads, from_sibling, names)]
        return [p32 for p32, _ in part], _exchange_comm([p16 for _, p16 in part])

    dout1, dproj1, acc_cv = _odd_bwd(dx2, proj1, wout1, vecs1, cw8, cb)
    g_wout1 = _grad_w_rows(y1, dout1, "grad_wout1")
    g_win1 = _grad_w_cols(h1, dproj1, "grad_win1")
    dx1, acc_n1 = _inproj_bwd(dproj1, win1, x1, dx2, vecs1, "inproj1_bwd")

    dout0, dyp, dys, dproj0, acc_g0 = _even_bwd(dx1, out0, ypool, ysb, proj0, wout0, vecs0)
    g_wout0 = _grad_w_rows(yg, dout0, "grad_wout0")
    dproj0, g_pw, acc_ps = _pool_bwd(dyp, p, pw, pscale, dproj0)
    g_pool = _chip_major(g_pw, PG // NCHIP)
    early = [(g_pool, g_pool.astype(BF16)), g_wout0, g_win1, g_wout1]
    part_a, swap_a = chip_partials(early, ["pool", "wout0", "win1", "wout1"]) if dist else (None, None)
    dproj0, *got_a = _attn_bwd(proj0, ysb, dys, dproj0, swap_a)
    g_win0 = _grad_w_cols(h0, dproj0, "grad_win0")
    part_b, swap_b = chip_partials([g_win0], ["win0"]) if dist else (None, None)
    dx0, acc_n0, *got_b = _inproj_bwd(dproj0, win0, x, dx1, vecs0, "inproj0_bwd", swap_b)

    if dist:
        names = ["win0", "pool", "wout0", "win1", "wout1"]
        halves = [_add_partials(p32, t, sel, "rs_add_partials_" + nm)
                  for p32, t, nm in zip(part_b + part_a, got_b + got_a, names)]
        grads = tuple(_join_halves(halves))
    else:
        grads = (g_win0[0], g_pw, g_wout0[0], g_win1[0], g_wout1[0])

    sums = dict(
        dm0=jnp.concatenate([acc_n0[0:2], acc_g0[0:1]], axis=0),
        dm1=jnp.concatenate([acc_n1[0:2], acc_f[2:3]], axis=0),
        norm_g=jnp.concatenate([acc_n0[2:3], acc_n1[2:3]], axis=0),
        pool_scale=acc_ps[0:1], final_g=acc_f[0:1], loss=acc_f[1:2],
        conv_w=acc_cv[0:3], conv_b=acc_cv[3:4])
    return dx0, grads, sums


ANY = pl.BlockSpec(memory_space=pl.ANY)
CHIP_FLIPS = ((1, 0), (0, 1), (1, 1))


def _place():
    return lax.axis_index("x"), lax.axis_index("y"), lax.axis_index("c")


def _flip(v, f):
    return 1 - v if f else v


def _allgather8(v, name):
    m_per, n = v.shape

    def body(x_ref, out_ref, send_sems, recv_sems, local_sem):
        x, y, c = _place()
        me, sibling = (x, y, c), (x, y, 1 - c)
        chips = [(_flip(x, fx), _flip(y, fy)) for fx, fy in CHIP_FLIPS]

        def rows(px, py, pc):
            return out_ref.at[pl.ds((4 * px + 2 * py + pc) * m_per, m_per), :]

        def copy(k, block, to, src=None):
            return pltpu.make_async_remote_copy(
                src_ref=rows(*block) if src is None else src, dst_ref=rows(*block),
                send_sem=send_sems.at[k], recv_sem=recv_sems.at[k], device_id=to, device_id_type=MESH)

        mine = pltpu.make_async_copy(x_ref, rows(*me), local_sem)
        mine.start()
        first = [copy(0, me, sibling, src=x_ref)]
        first += [copy(1 + j, me, (*chip, c), src=x_ref) for j, chip in enumerate(chips)]
        for cp in first:
            cp.start()
        passed = [copy(4 + j, (*chip, c), sibling) for j, chip in enumerate(chips)]
        for j, chip in enumerate(chips):
            copy(1 + j, (*chip, c), me).wait_recv()
            passed[j].start()
        copy(0, sibling, me).wait_recv()
        for j, chip in enumerate(chips):
            copy(4 + j, (*chip, 1 - c), me).wait_recv()
        for cp in first + passed:
            cp.wait_send()
        mine.wait()

    return pl.pallas_call(
        body, name=name,
        out_shape=jax.ShapeDtypeStruct((NDEV * m_per, n), v.dtype),
        in_specs=[pl.BlockSpec(memory_space=pltpu.VMEM)],
        out_specs=pl.BlockSpec(memory_space=pltpu.VMEM),
        scratch_shapes=[pltpu.SemaphoreType.DMA((7,)), pltpu.SemaphoreType.DMA((7,)), pltpu.SemaphoreType.DMA],
    )(v)


class _Comm:
    def __init__(self, arrays, out_shape, sems, start, finish):
        self.arrays, self.out_shape, self.sems, self.start, self.finish = arrays, out_shape, sems, start, finish


_NO_COMM = _Comm([], [], [], None, None)


def _gather_comm(shards):
    n = len(shards)

    def pieces(ins, outs, sems, kinds):
        x, y, c = _place()
        ici_send, ici_recv, fwd_send, fwd_recv = sems[:4]
        own, sibling = 2 * x + y, (x, y, 1 - c)
        made = {kind: [] for kind in kinds}
        for w in range(n):
            r2 = ins[w].shape[0] // 2
            mine, other = pl.ds(c * r2, r2), pl.ds((1 - c) * r2, r2)
            for d, (fx, fy) in enumerate(CHIP_FLIPS):
                px, py, k = _flip(x, fx), _flip(y, fy), 3 * w + d
                peer = 2 * px + py
                ici = dict(send_sem=ici_send.at[k], recv_sem=ici_recv.at[k], device_id=(px, py, c), device_id_type=MESH)
                fwd = dict(send_sem=fwd_send.at[k], recv_sem=fwd_recv.at[k], device_id=sibling, device_id_type=MESH)
                if "ici_out" in kinds:
                    made["ici_out"].append(pltpu.make_async_remote_copy(
                        src_ref=ins[w].at[mine, :], dst_ref=outs[w].at[own, mine, :], **ici))
                if "ici_in" in kinds:
                    made["ici_in"].append(pltpu.make_async_remote_copy(
                        src_ref=ins[w].at[mine, :], dst_ref=outs[w].at[peer, mine, :], **ici))
                if "fwd_out" in kinds:
                    made["fwd_out"].append(pltpu.make_async_remote_copy(
                        src_ref=outs[w].at[peer, mine, :], dst_ref=outs[w].at[peer, mine, :], **fwd))
                if "fwd_in" in kinds:
                    made["fwd_in"].append(pltpu.make_async_remote_copy(
                        src_ref=outs[w].at[peer, other, :], dst_ref=outs[w].at[peer, other, :], **fwd))
        return made

    def stage_rows(a):
        return min(a.shape[0], 256)

    def start(ins, outs, sems):
        for cp in pieces(ins, outs, sems, ("ici_out",))["ici_out"]:
            cp.start()
        x, y, _ = _place()
        for w, stage in enumerate(sems[4:]):
            rows = stage.shape[0]
            for r in range(0, ins[w].shape[0], rows):
                pltpu.sync_copy(ins[w].at[pl.ds(r, rows), :], stage)
                pltpu.sync_copy(stage, outs[w].at[2 * x + y, pl.ds(r, rows), :])

    def finish(ins, outs, sems):
        made = pieces(ins, outs, sems, ("ici_out", "ici_in", "fwd_out", "fwd_in"))
        for arrived, onward in zip(made["ici_in"], made["fwd_out"]):
            arrived.wait_recv()
            onward.start()
        for cp in made["fwd_in"]:
            cp.wait_recv()
        for cp in made["ici_out"] + made["fwd_out"]:
            cp.wait_send()

    dma = pltpu.SemaphoreType.DMA
    return _Comm(list(shards), [jax.ShapeDtypeStruct((NCHIP,) + a.shape, a.dtype) for a in shards],
                 [dma((3 * n,))] * 4 + [pltpu.VMEM((stage_rows(a), a.shape[1]), a.dtype) for a in shards],
                 start, finish)


def _exchange_comm(parts):
    n = len(parts)

    def copies(ins, outs, sems):
        x, y, c = _place()
        send_sems, recv_sems = sems
        out = []
        for w in range(n):
            for d, (fx, fy) in enumerate(CHIP_FLIPS):
                px, py = _flip(x, fx), _flip(y, fy)
                out.append(pltpu.make_async_remote_copy(
                    src_ref=ins[w].at[2 * px + py], dst_ref=outs[w].at[d], send_sem=send_sems.at[3 * w + d],
                    recv_sem=recv_sems.at[3 * w + d], device_id=(px, py, c), device_id_type=MESH))
        return out

    def start(ins, outs, sems):
        for cp in copies(ins, outs, sems):
            cp.start()

    def finish(ins, outs, sems):
        cps = copies(ins, outs, sems)
        for cp in cps:
            cp.wait_recv()
        for cp in cps:
            cp.wait_send()

    dma = pltpu.SemaphoreType.DMA
    return _Comm(list(parts), [jax.ShapeDtypeStruct((3,) + p.shape[1:], BF16) for p in parts],
                 [dma((3 * n,))] * 2, start, finish)


def _send_halves(grads, name):
    n = len(grads)

    def body(*refs):
        ins, outs = refs[:n], refs[n:2 * n]
        send_sems, recv_sems = refs[2 * n:]
        x, y, c = _place()
        copies = []
        for w in range(n):
            r2 = ins[w].shape[1] // 2
            cp = pltpu.make_async_remote_copy(
                src_ref=ins[w].at[:, pl.ds((1 - c) * r2, r2), :], dst_ref=outs[w],
                send_sem=send_sems.at[w], recv_sem=recv_sems.at[w], device_id=(x, y, 1 - c), device_id_type=MESH)
            cp.start()
            copies.append(cp)
        for cp in copies:
            cp.wait_recv()
        for cp in copies:
            cp.wait_send()

    return pl.pallas_call(
        body, name=name,
        out_shape=[jax.ShapeDtypeStruct((NCHIP, g.shape[1] // 2, g.shape[2]), g.dtype) for g in grads],
        in_specs=[ANY] * n, out_specs=[ANY] * n,
        scratch_shapes=[pltpu.SemaphoreType.DMA((n,)), pltpu.SemaphoreType.DMA((n,))],
    )(*grads)


def _row_tile(rows, cols):
    tr = rows
    while tr * cols * 4 > (1 << 20) and tr % 16 == 0:
        tr //= 2
    return tr


def _add_halves(g, t1, core, name):
    _, r, cdim = g.shape
    r2 = r // 2
    tr = _row_tile(r2, cdim)
    nt = r2 // tr

    def body(core_ref, g_ref, t_ref, p_ref, pb_ref):
        p = g_ref[...] + t_ref[...].astype(F32)
        p_ref[...] = p
        pb_ref[...] = p.astype(BF16)

    blk = pl.BlockSpec((None, tr, cdim), lambda j, i, core_ref: (j, i, 0))
    return pl.pallas_call(
        body, name=name,
        grid_spec=pltpu.PrefetchScalarGridSpec(
            num_scalar_prefetch=1, grid=(NCHIP, nt),
            in_specs=[pl.BlockSpec((None, tr, cdim), lambda j, i, core_ref: (j, core_ref[0] * nt + i, 0)), blk],
            out_specs=[blk, blk]),
        out_shape=[jax.ShapeDtypeStruct((NCHIP, r2, cdim), F32), jax.ShapeDtypeStruct((NCHIP, r2, cdim), BF16)],
        compiler_params=_params("parallel", "parallel"),
    )(core, g, t1)


def _add_partials(p, t2, sel, name):
    _, r2, cdim = p.shape
    tr = _row_tile(r2, cdim)
    nt = r2 // tr

    def body(sel_ref, p_ref, t_ref, o_ref):
        o_ref[...] = ((p_ref[...] + t_ref[0].astype(F32)) + t_ref[1].astype(F32)) + t_ref[2].astype(F32)

    return pl.pallas_call(
        body, name=name,
        grid_spec=pltpu.PrefetchScalarGridSpec(
            num_scalar_prefetch=1, grid=(nt,),
            in_specs=[pl.BlockSpec((None, tr, cdim), lambda i, sel_ref: (sel_ref[0], i, 0)),
                      pl.BlockSpec((3, tr, cdim), lambda i, sel_ref: (0, i, 0))],
            out_specs=pl.BlockSpec((tr, cdim), lambda i, sel_ref: (sel_ref[1] * nt + i, 0))),
        out_shape=jax.ShapeDtypeStruct((2 * r2, cdim), F32),
        compiler_params=_params("parallel"),
    )(sel, p, t2)


def _join_halves(grads):
    n = len(grads)

    def body(*refs):
        bufs = refs[n:2 * n]
        send_sems, recv_sems = refs[2 * n:]
        x, y, c = _place()
        copies = []
        for w in range(n):
            r2 = bufs[w].shape[0] // 2
            mine = bufs[w].at[pl.ds(c * r2, r2), :]
            cp = pltpu.make_async_remote_copy(
                src_ref=mine, dst_ref=mine, send_sem=send_sems.at[w], recv_sem=recv_sems.at[w],
                device_id=(x, y, 1 - c), device_id_type=MESH)
            cp.start()
            copies.append(cp)
        for w in range(n):
            r2 = bufs[w].shape[0] // 2
            theirs = bufs[w].at[pl.ds((1 - c) * r2, r2), :]
            pltpu.make_async_remote_copy(
                src_ref=theirs, dst_ref=theirs, send_sem=send_sems.at[w], recv_sem=recv_sems.at[w],
                device_id=(x, y, 1 - c), device_id_type=MESH).wait_recv()
        for cp in copies:
            cp.wait_send()

    return pl.pallas_call(
        body, name="rs_join_halves",
        out_shape=[jax.ShapeDtypeStruct(g.shape, F32) for g in grads],
        in_specs=[ANY] * n, out_specs=[ANY] * n, input_output_aliases={w: w for w in range(n)},
        scratch_shapes=[pltpu.SemaphoreType.DMA((n,)), pltpu.SemaphoreType.DMA((n,))],
    )(*grads)


def _ada_fwd(c_all, ada_w):
    nl, _, ns = ada_w.shape

    def body(c_ref, w_ref, o_ref):
        cv = c_ref[...]
        o_ref[...] = _dot((cv * _sigmoid(cv)).astype(BF16), w_ref[...].astype(BF16))

    return pl.pallas_call(
        body, name="ada_fwd", grid=(nl,),
        in_specs=[pl.BlockSpec((NDEV, D), lambda i: (0, 0)), pl.BlockSpec((None, D, ns), lambda i: (i, 0, 0))],
        out_specs=pl.BlockSpec((None, NDEV, ns), lambda i: (i, 0, 0)),
        out_shape=jax.ShapeDtypeStruct((nl, NDEV, ns), F32),
        compiler_params=_params("parallel"),
    )(c_all, ada_w)


PACK_ROWS = 24


def _reduce_packed(gathered):
    def body(g_ref, tot_ref, loss_ref):
        tot = g_ref[0:PACK_ROWS, :]
        for dev in range(1, NDEV):
            tot = tot + g_ref[dev * PACK_ROWS:(dev + 1) * PACK_ROWS, :]
        tot_ref[...] = tot
        loss_ref[...] = jnp.zeros((8, 128), F32) + jnp.sum(tot[10:11, :])

    return pl.pallas_call(
        body, name="reduce_packed",
        out_shape=[jax.ShapeDtypeStruct((PACK_ROWS, D), F32), jax.ShapeDtypeStruct((8, 128), F32)],
    )(gathered)


def _ada_w_update(c_t, dms, w, m, v):
    nl, _, ns = w.shape
    tr = 256

    def body(ct_ref, dm_ref, w_ref, m_ref, v_ref, g_ref, d_ref, m2_ref, v2_ref):
        ct = ct_ref[...]
        sc = ct * _sigmoid(ct)
        dm = dm_ref[...]
        g = sc[:, 0:1] * dm[0:1, :]
        for b in range(1, NDEV):
            g = g + sc[:, b:b + 1] * dm[b:b + 1, :]
        g_ref[...] = g
        d_ref[...], m2_ref[...], v2_ref[...] = _adamw_math(w_ref[...], g, m_ref[...], v_ref[...])

    blk = pl.BlockSpec((None, tr, ns), lambda i, j: (i, j, 0))
    return pl.pallas_call(
        body, name="ada_w_update", grid=(nl, D // tr),
        in_specs=[pl.BlockSpec((tr, NDEV), lambda i, j: (j, 0)),
                  pl.BlockSpec((None, NDEV, ns), lambda i, j: (i, 0, 0)), blk, blk, blk],
        out_specs=[blk] * 4,
        out_shape=[jax.ShapeDtypeStruct((nl, D, ns), F32)] * 4,
        compiler_params=_params("parallel", "parallel"),
    )(c_t, dms, w, m, v)


def _chip_major(a, parts):
    g, _, cdim = a.shape
    return jnp.transpose(a.reshape(g, NCHIP, parts, cdim), (1, 0, 2, 3)).reshape(NCHIP, g * parts, cdim)


def _group_major(a):
    return jnp.transpose(a.reshape(NCHIP, 4, PG // NCHIP, PG), (1, 0, 2, 3)).reshape(4, PG, PG)


def kernel(x, c, norm_g, ada_w, ada_b, even_w_in, pool_w, pool_scale, even_w_out, odd_w_in, conv_w, conv_b, odd_w_out, final_g, loss_target, m_norm_g, m_ada_w, m_ada_b, m_even_w_in, m_pool_w, m_pool_scale, m_even_w_out, m_odd_w_in, m_conv_w, m_conv_b, m_odd_w_out, m_final_g, v_norm_g, v_ada_w, v_ada_b, v_even_w_in, v_pool_w, v_pool_scale, v_even_w_out, v_odd_w_in, v_conv_w, v_conv_b, v_odd_w_out, v_final_g):
    ix, iy, ic = _place()
    chip = 2 * ix + iy
    batch = 2 * chip + ic
    sel = jnp.stack([chip, ic]).astype(jnp.int32)
    ns_ada = ada_w.shape[2]
    ns_conv = conv_b.shape[1]

    conv_rows = jnp.pad(jnp.concatenate([conv_w[0], conv_b], axis=0), ((0, 3), (0, D - ns_conv)))
    first = _allgather8(jnp.concatenate([c, conv_rows], axis=0), "gather_c_conv").reshape(NCHIP, 2, 8, D)
    c_all = first[:, :, 0].reshape(NDEV, D)
    cw_full = jnp.transpose(first[:, 0, 1:5, 0:ns_conv], (1, 0, 2)).reshape(4, DI)
    cw8 = jnp.concatenate([cw_full[0:3], jnp.zeros((5, DI), F32)], axis=0)
    cb_full = cw_full[3:4]

    m_cols = _allgather8(_ada_fwd(c_all, ada_w).reshape(2 * NDEV, ns_ada), "gather_ada")
    m_cols = m_cols.reshape(NCHIP, 2, 2, NDEV, ns_ada)[:, 0]
    m_mine = lax.dynamic_index_in_dim(m_cols, batch, axis=2, keepdims=False)
    m_mine = jnp.transpose(m_mine, (1, 0, 2)).reshape(2, 3 * D) + ada_b
    zrow = jnp.zeros((3, D), F32)

    def vec_rows(i):
        sh, sc, gt = m_mine[i, 0:D], m_mine[i, D:2 * D], m_mine[i, 2 * D:3 * D]
        return jnp.concatenate([jnp.stack([norm_g[i], sc, sh, gt, final_g]), zrow], axis=0)

    shards = (pool_w[0].astype(BF16).reshape(PG, PG), even_w_out[0].astype(BF16),
              odd_w_in[0].astype(BF16), odd_w_out[0].astype(BF16))
    dx0, grads, sums = _local_step(
        x[0], loss_target[0], vec_rows(0), vec_rows(1), even_w_in[0].astype(BF16), shards, pool_scale, cw8,
        cb_full, sel)
    r_win0, r_pw, r_wout0, r_win1, r_wout1 = grads

    packed = jnp.concatenate([
        sums["dm0"], sums["dm1"], sums["norm_g"], sums["pool_scale"], sums["final_g"], sums["loss"],
        sums["conv_w"].reshape(6, D), sums["conv_b"].reshape(2, D), jnp.zeros((PACK_ROWS - 19, D), F32)], axis=0)
    gathered = _allgather8(packed, "gather_sums")
    tot, loss8 = _reduce_packed(gathered)
    loss = loss8[0, 0]
    g_norm_g, g_pool_scale, g_final_g = tot[6:8], tot[8:9], tot[9]
    g_ada_b = tot[0:6].reshape(2, 3 * D)
    g_conv_w = lax.dynamic_slice_in_dim(tot[11:17].reshape(3, DI), chip * (DI // NCHIP), DI // NCHIP, axis=1)
    g_conv_b = lax.dynamic_slice_in_dim(tot[17:19].reshape(1, DI), chip * (DI // NCHIP), DI // NCHIP, axis=1)
    dm_all = gathered.reshape(NDEV, PACK_ROWS, D)[:, 0:6].reshape(NDEV, 2, 3 * D)
    dm_cols = jnp.transpose(lax.dynamic_slice_in_dim(dm_all, chip * ns_ada, ns_ada, axis=2), (1, 0, 2))
    g_ada_w, d_ada_w, nm_ada_w, nv_ada_w = _ada_w_update(jnp.transpose(c_all), dm_cols, ada_w, m_ada_w, v_ada_w)

    def upd(w, g, m, v, name):
        shape = w.shape
        w2, m2, v2 = (a.reshape(g.shape) for a in (w, m, v))
        d, nm, nv = _adamw(w2, g, m2, v2, name)
        return g.reshape(shape), d.reshape(shape), nm.reshape(shape), nv.reshape(shape)

    o_win0 = upd(even_w_in, r_win0, m_even_w_in, v_even_w_in, "adamw_win0")
    o_pw = upd(pool_w, r_pw, m_pool_w, v_pool_w, "adamw_pool")
    o_wout0 = upd(even_w_out, r_wout0, m_even_w_out, v_even_w_out, "adamw_wout0")
    o_win1 = upd(odd_w_in, r_win1, m_odd_w_in, v_odd_w_in, "adamw_win1")
    o_wout1 = upd(odd_w_out, r_wout1, m_odd_w_out, v_odd_w_out, "adamw_wout1")

    def pack_small(ng, ab, ps, fg, cwv, cbv):
        conv = jnp.concatenate([cwv.reshape(3, -1), cbv.reshape(1, -1)], axis=0).reshape(2, D)
        return jnp.concatenate([ng, ab.reshape(6, D), ps, fg.reshape(1, D), conv, jnp.zeros((4, D), F32)], axis=0)

    sw = pack_small(norm_g, ada_b, pool_scale, final_g, conv_w, conv_b)
    sg = pack_small(g_norm_g, g_ada_b, g_pool_scale, g_final_g, g_conv_w, g_conv_b)
    sm = pack_small(m_norm_g, m_ada_b, m_pool_scale, m_final_g, m_conv_w, m_conv_b)
    sv = pack_small(v_norm_g, v_ada_b, v_pool_scale, v_final_g, v_conv_w, v_conv_b) + jnp.concatenate(
        [jnp.zeros((12, D), F32), jnp.ones((4, D), F32)], axis=0)
    small = _adamw(sw, sg, sm, sv, "adamw_small")

    def unpack_small(a):
        conv = a[10:12].reshape(4, -1)
        return dict(norm_g=a[0:2], ada_b=a[2:8].reshape(2, 3 * D), pool_scale=a[8:9], final_g=a[9],
                    conv_w=conv[0:3].reshape(conv_w.shape), conv_b=conv[3:4].reshape(conv_b.shape))

    s_grad = dict(norm_g=g_norm_g, ada_b=g_ada_b, pool_scale=g_pool_scale, final_g=g_final_g,
                  conv_w=g_conv_w.reshape(conv_w.shape), conv_b=g_conv_b.reshape(conv_b.shape))
    s_out = [s_grad] + [unpack_small(a) for a in small]

    outs = []
    for k in range(4):
        sm_k = s_out[k]
        outs.append([sm_k["norm_g"], (g_ada_w, d_ada_w, nm_ada_w, nv_ada_w)[k], sm_k["ada_b"], o_win0[k], o_pw[k],
                     sm_k["pool_scale"], o_wout0[k], o_win1[k], sm_k["conv_w"], sm_k["conv_b"], o_wout1[k],
                     sm_k["final_g"]])
    return (loss, dx0[None], *outs[0], *outs[1], *outs[2], *outs[3])
```

```python
import functools

import jax
import jax.numpy as jnp
from jax import lax
from jax.experimental import pallas as pl
from jax.experimental.pallas import tpu as pltpu

F32 = jnp.float32
BF16 = jnp.bfloat16
MESH = pl.DeviceIdType.MESH

D = 1024
DI = 2048
DP = 1024
NE = 6144
NO = 8192
WINDOWS = (2, 4, 8, 16)
PG = 256
HD = 64
NCHIP = 4
NDEV = 8
EPS = 1e-6
INV_SQRT_HD = 0.125

LR, B1, B2, EPS_ADAM, WD, STEP = 0.001, 0.9, 0.999, 1e-08, 0.01, 10

TM = 512
TME = 256
CT = 512
BQ = 512
BK = 256
assert BQ == 2 * BK
HALO = 16
DEAD_LOG_WEIGHT = -104.0
VMEM_LIMIT = 56 * 1024 * 1024


def _dot(a, b):
    return jnp.dot(a, b, preferred_element_type=F32)


def _dot_nt(a, b):
    return lax.dot_general(a, b, (((1,), (1,)), ((), ())), preferred_element_type=F32)


def _dot_tn(a, b):
    return lax.dot_general(a, b, (((0,), (0,)), ((), ())), preferred_element_type=F32)


def _params(*sem):
    return pltpu.CompilerParams(dimension_semantics=sem, vmem_limit_bytes=VMEM_LIMIT)


def _sigmoid(v):
    return 0.5 * jnp.tanh(0.5 * v) + 0.5


def _rowsum(v):
    return jnp.sum(v, axis=0, keepdims=True)


def _modulated_norm(xv, vec_ref):
    r = lax.rsqrt(jnp.mean(xv * xv, axis=-1, keepdims=True) + EPS)
    return (((xv * r) * vec_ref[0:1, :]) * (1.0 + vec_ref[1:2, :]) + vec_ref[2:3, :]).astype(BF16)


def _norm_mod(x, vecs, name):
    s = x.shape[0]
    tm = min(TM, s)

    def body(x_ref, vec_ref, h_ref):
        h_ref[...] = _modulated_norm(x_ref[...], vec_ref)

    return pl.pallas_call(
        body, name=name, grid=(s // tm,),
        in_specs=[pl.BlockSpec((tm, D), lambda i: (i, 0)), pl.BlockSpec((8, D), lambda i: (0, 0))],
        out_specs=pl.BlockSpec((tm, D), lambda i: (i, 0)),
        out_shape=jax.ShapeDtypeStruct((s, D), BF16),
        compiler_params=_params("parallel"),
    )(x, vecs)


def _inproj(h, w, name, comm=None):
    s = h.shape[0]
    ns = w.shape[2]
    tm = min(TM, s)
    ni = s // tm
    comm = comm or _NO_COMM
    nci, nco = len(comm.arrays), len(comm.out_shape)

    def body(*refs):
        h_ref, w_ref = refs[:2]
        proj_ref = refs[2 + nci]
        cargs = (refs[2:2 + nci], refs[3 + nci:3 + nci + nco], refs[3 + nci + nco:])
        j, i = pl.program_id(0), pl.program_id(1)
        if nci:
            pl.when((j == 0) & (i == 0))(lambda: comm.start(*cargs))
        proj_ref[...] = _dot(h_ref[...], w_ref[...]).astype(BF16)
        if nci:
            pl.when((j == NCHIP - 1) & (i == ni - 1))(lambda: comm.finish(*cargs))

    return pl.pallas_call(
        body, name=name, grid=(NCHIP, ni),
        in_specs=[pl.BlockSpec((tm, D), lambda j, i: (i, 0)),
                  pl.BlockSpec((None, D, ns), lambda j, i: (j, 0, 0))] + [ANY] * nci,
        out_specs=[pl.BlockSpec((tm, ns), lambda j, i: (i, j))] + [ANY] * nco,
        out_shape=[jax.ShapeDtypeStruct((s, NCHIP * ns), BF16)] + comm.out_shape,
        scratch_shapes=comm.sems,
        compiler_params=_params("arbitrary", "arbitrary"),
    )(h, w, *comm.arrays)


def _inproj_gathering(h, w_shard, order, name, comm):
    s = h.shape[0]
    ns = w_shard.shape[1]
    r2 = D // 2
    tm = min(TM, s)
    ni = s // tm
    nci, nco = len(comm.arrays), len(comm.out_shape)

    def body(*refs):
        order_ref, h_ref, shard_hbm = refs[:3]
        proj_ref, full_hbm = refs[3 + nci:5 + nci]
        w_ref, ici_send, ici_recv, fwd_send, fwd_recv = refs[5 + nci + nco:10 + nci + nco]
        cargs = (refs[3:3 + nci], refs[5 + nci:5 + nci + nco], refs[10 + nci + nco:])
        j, i = pl.program_id(0), pl.program_id(1)
        x, y, c = _place()
        own, sibling = 2 * x + y, (x, y, 1 - c)
        mine, other = pl.ds(c * r2, r2), pl.ds((1 - c) * r2, r2)

        def copies(d, kinds):
            px, py = _flip(x, CHIP_FLIPS[d][0]), _flip(y, CHIP_FLIPS[d][1])
            peer = 2 * px + py
            ici = dict(send_sem=ici_send.at[d], recv_sem=ici_recv.at[d], device_id=(px, py, c), device_id_type=MESH)
            fwd = dict(send_sem=fwd_send.at[d], recv_sem=fwd_recv.at[d], device_id=sibling, device_id_type=MESH)
            made = {}
            if "ici_out" in kinds:
                made["ici_out"] = pltpu.make_async_remote_copy(
                    src_ref=shard_hbm.at[mine, :], dst_ref=full_hbm.at[own, mine, :], **ici)
            if "ici_in" in kinds:
                made["ici_in"] = pltpu.make_async_remote_copy(
                    src_ref=shard_hbm.at[mine, :], dst_ref=full_hbm.at[peer, mine, :], **ici)
            if "fwd_out" in kinds:
                made["fwd_out"] = pltpu.make_async_remote_copy(
                    src_ref=full_hbm.at[peer, mine, :], dst_ref=full_hbm.at[peer, mine, :], **fwd)
            if "fwd_in" in kinds:
                made["fwd_in"] = pltpu.make_async_remote_copy(
                    src_ref=full_hbm.at[peer, other, :], dst_ref=full_hbm.at[peer, other, :], **fwd)
            return made, peer

        neighbours, diagonal = (0, 1), 2

        @pl.when((j == 0) & (i == 0))
        def _():
            comm.start(*cargs)
            for d in neighbours:
                copies(d, ("ici_out",))[0]["ici_out"].start()
            pltpu.sync_copy(shard_hbm, w_ref)
            pltpu.sync_copy(w_ref, full_hbm.at[own])

        for d in range(len(CHIP_FLIPS)):
            @pl.when((j == d + 1) & (i == 0))
            def _(d=d):
                made, peer = copies(d, ("ici_in", "fwd_out", "fwd_in"))
                made["ici_in"].wait_recv()
                made["fwd_out"].start()
                made["fwd_in"].wait_recv()
                pltpu.sync_copy(full_hbm.at[peer], w_ref)
                if d == 0:
                    for n in neighbours:
                        copies(n, ("ici_out",))[0]["ici_out"].wait_send()
                    copies(diagonal, ("ici_out",))[0]["ici_out"].start()

        proj_ref[...] = _dot(h_ref[...], w_ref[...]).astype(BF16)

        @pl.when((j == NCHIP - 1) & (i == ni - 1))
        def _():
            copies(diagonal, ("ici_out",))[0]["ici_out"].wait_send()
            for d in range(len(CHIP_FLIPS)):
                copies(d, ("fwd_out",))[0]["fwd_out"].wait_send()
            comm.finish(*cargs)

    dma = pltpu.SemaphoreType.DMA
    return pl.pallas_call(
        body, name=name,
        grid_spec=pltpu.PrefetchScalarGridSpec(
            num_scalar_prefetch=1, grid=(NCHIP, ni),
            in_specs=[pl.BlockSpec((tm, D), lambda j, i, order_ref: (i, 0)), ANY] + [ANY] * nci,
            out_specs=[pl.BlockSpec((tm, ns), lambda j, i, order_ref: (i, order_ref[j])), ANY] + [ANY] * nco,
            scratch_shapes=[pltpu.VMEM((D, ns), BF16)] + [dma((3,))] * 4 + comm.sems),
        out_shape=[jax.ShapeDtypeStruct((s, NCHIP * ns), BF16),
                   jax.ShapeDtypeStruct((NCHIP, D, ns), BF16)] + comm.out_shape,
        compiler_params=_params("arbitrary", "arbitrary"),
    )(order, h, w_shard, *comm.arrays)


def _pool_fwd(proj0, pw, pscale):
    s = proj0.shape[0]
    tm = min(TM, s)
    hb = tm // 16

    def body(u_ref, halo_ref, w_ref, sc_ref, p_ref, y_ref, ext_ref):
        i = pl.program_id(0)
        ext_ref[16:, :] = u_ref[...].astype(F32)
        ext_ref[0:16, :] = jnp.where(i > 0, halo_ref[...].astype(F32), 0.0)
        t = i * tm + lax.broadcasted_iota(jnp.int32, (tm, 1), 0)
        for g, wdw in enumerate(WINDOWS):
            cs = slice(g * PG, (g + 1) * PG)
            u = ext_ref[16:16 + tm, cs]
            acc = u
            for j in range(1, wdw):
                acc = acc + ext_ref[16 - j:16 - j + tm, cs]
            inv = 1.0 / jnp.minimum(t + 1, wdw).astype(F32)
            pb = (acc * inv - u).astype(BF16)
            p_ref[:, cs] = pb
            y_ref[:, cs] = _dot(pb, w_ref[g]) * sc_ref[:, cs]

    return pl.pallas_call(
        body, name="pool_fwd", grid=(s // tm,),
        in_specs=[pl.BlockSpec((tm, DP), lambda i: (i, 0)),
                  pl.BlockSpec((16, DP), lambda i: (jnp.maximum(i * hb - 1, 0), 0)),
                  pl.BlockSpec((4, PG, PG), lambda i: (0, 0, 0)),
                  pl.BlockSpec((1, DP), lambda i: (0, 0))],
        out_specs=[pl.BlockSpec((tm, DP), lambda i: (i, 0)),
                   pl.BlockSpec((tm, DP), lambda i: (i, 0))],
        out_shape=[jax.ShapeDtypeStruct((s, DP), BF16), jax.ShapeDtypeStruct((s, DP), F32)],
        scratch_shapes=[pltpu.VMEM((tm + 16, DP), F32)],
        compiler_params=_params("parallel"),
    )(proj0, proj0, pw, pscale)


def _sb_logits(nz, mask):
    neg_abs = lax.bitcast_convert_type(lax.bitcast_convert_type(nz, jnp.uint32) | jnp.uint32(0x80000000), F32)
    t = jnp.log(1.0 + jnp.exp(neg_abs))
    lf = jnp.minimum(nz, 0.0) - t
    lam = lf - nz
    if mask is not None:
        lf = jnp.where(mask, lf, 0.0)
    return lf, lam


def _sweep_left(steps, state, step):
    def live(carry):
        j, st = carry
        heaviest = functools.reduce(jnp.maximum, [jnp.max(head[1]) for head in st])
        return (j < steps) & (heaviest > DEAD_LOG_WEIGHT)

    return lax.while_loop(live, lambda carry: (carry[0] + 1, step(carry[0], carry[1])), (0, state))[1]


def _split_dot(v, tri):
    hi = v.astype(BF16)
    lo = (v - hi.astype(F32)).astype(BF16)
    return _dot(hi, tri) + _dot(lo, tri)


def _tri_masks():
    row = lax.broadcasted_iota(jnp.int32, (BK, BK), 0)
    col = lax.broadcasted_iota(jnp.int32, (BK, BK), 1)
    return (row > col).astype(BF16), (row >= col).astype(BF16)


def _causal_mask():
    row = lax.broadcasted_iota(jnp.int32, (BK, BK), 0)
    col = lax.broadcasted_iota(jnp.int32, (BK, BK), 1)
    return col < row


def _attn_fwd(proj0, comm=None):
    s = proj0.shape[0]
    nq = s // BQ
    kpq = BQ // BK
    nsteps = DP // 128
    comm = comm or _NO_COMM
    nci, nco = len(comm.arrays), len(comm.out_shape)

    def body(*refs):
        q_ref, k_ref, v_ref = refs[:3]
        o_ref = refs[3 + nci]
        qn_ref, k8_ref, vb_ref = refs[4 + nci + nco:7 + nci + nco]
        cargs = (refs[3:3 + nci], refs[4 + nci:4 + nci + nco], refs[7 + nci + nco:])
        if nci:
            pl.when(pl.program_id(0) == 0)(lambda: comm.start(*cargs))
        qn_ref[...] = (-q_ref[...]).astype(BF16)
        k8_ref[...] = (k_ref[...] * INV_SQRT_HD).astype(BF16)
        vb_ref[...] = v_ref[...].astype(BF16)
        after, _ = _tri_masks()
        causal = _causal_mask()
        heads = [slice(HD * h, HD * (h + 1)) for h in range(2)]
        hs = range(len(heads))
        lo, hi = (0, BK), (BK, BK)

        def qstep(qi, carry):
            q0 = pl.multiple_of(qi * BQ, BQ)
            qn = [qn_ref[pl.ds(q0, BQ), ls] for ls in heads]

            def sweep(lanes, state):
                nz = [_dot_nt(qn[h][r0:r0 + n], k8_ref[pl.ds(k0, BK), heads[h]]) for h, (r0, n), k0, _ in lanes]
                ll = [_sb_logits(nz[i], lane[3]) for i, lane in enumerate(lanes)]
                aft = [_dot(ll[i][0].astype(BF16), after) for i in range(len(lanes))]
                state = dict(state)
                for i, (h, rows, k0, mask) in enumerate(lanes):
                    o_acc, c = state[h, rows]
                    a = jnp.exp(ll[i][1] + aft[i] + c)
                    if mask is not None:
                        a = jnp.where(mask, a, 0.0)
                    o_acc = o_acc + _dot(a.astype(BF16), vb_ref[pl.ds(k0, BK), heads[h]])
                    state[h, rows] = (o_acc, c + aft[i][:, 0:1] + ll[i][0][:, 0:1])
                return state

            k_lo, k_hi = q0, pl.multiple_of(q0 + BK, BK)
            st = {(h, r): (jnp.zeros((BK, HD), F32), jnp.zeros((BK, 1), F32)) for h in hs for r in (lo, hi)}
            st = sweep([(h, hi, k_hi, causal) for h in hs] + [(h, lo, k_lo, causal) for h in hs]
                       + [(h, hi, k_lo, None) for h in hs], st)
            def left_of(rows):
                def step(j, part):
                    k0 = pl.multiple_of(q0 - (j + 1) * BK, BK)
                    got = sweep([(h, rows, k0, None) for h in hs], {(h, rows): part[h] for h in hs})
                    return tuple(got[h, rows] for h in hs)
                return step

            done = {r: _sweep_left(qi * (BQ // BK), tuple(st[h, r] for h in hs), left_of(r)) for r in (lo, hi)}
            for h, ls in enumerate(heads):
                o_ref[pl.ds(q0, BQ), ls] = jnp.concatenate([done[lo][h][0], done[hi][h][0]], axis=0)
            return carry

        lax.fori_loop(0, nq, qstep, 0)
        if nci:
            pl.when(pl.program_id(0) == nsteps - 1)(lambda: comm.finish(*cargs))

    return pl.pallas_call(
        body, name="attn_fwd", grid=(nsteps,),
        in_specs=[pl.BlockSpec((s, 128), lambda h: (0, 8 + h)),
                  pl.BlockSpec((s, 128), lambda h: (0, 16 + h)),
                  pl.BlockSpec((s, 128), lambda h: (0, 24 + h))] + [ANY] * nci,
        out_specs=[pl.BlockSpec((s, 128), lambda h: (0, h))] + [ANY] * nco,
        out_shape=[jax.ShapeDtypeStruct((s, DP), F32)] + comm.out_shape,
        scratch_shapes=[pltpu.VMEM((s, 128), BF16)] * 3 + comm.sems,
        compiler_params=_params("arbitrary"),
    )(proj0, proj0, proj0, *comm.arrays)


def _even_out(ypool, ysb, proj0, wout, x, vecs, vecs_next):
    s = x.shape[0]
    tm = min(TME, s)

    def body(yp_ref, ys_ref, gate_ref, w_ref, x_ref, vec_ref, vecn_ref, x1_ref, out_ref, yg_ref, hn_ref):
        gt = gate_ref[...].astype(F32)
        sl = gt * _sigmoid(gt)
        yg_ref[:, :DP] = (yp_ref[...] * sl[:, :DP]).astype(BF16)
        yg_ref[:, DP:] = (ys_ref[...] * sl[:, DP:]).astype(BF16)
        out = _dot(yg_ref[...], w_ref[...])
        out_ref[...] = out
        x1 = x_ref[...] + (1.0 + vec_ref[3:4, :]) * out
        x1_ref[...] = x1
        hn_ref[...] = _modulated_norm(x1, vecn_ref)

    row = lambda i: (i, 0)
    const = lambda i: (0, 0)
    return pl.pallas_call(
        body, name="even_out", grid=(s // tm,),
        in_specs=[pl.BlockSpec((tm, DP), row), pl.BlockSpec((tm, DP), row),
                  pl.BlockSpec((tm, DI), lambda i: (i, 2)),
                  pl.BlockSpec((DI, D), const),
                  pl.BlockSpec((tm, D), row), pl.BlockSpec((8, D), const), pl.BlockSpec((8, D), const)],
        out_specs=[pl.BlockSpec((tm, D), row), pl.BlockSpec((tm, D), row), pl.BlockSpec((tm, DI), row),
                   pl.BlockSpec((tm, D), row)],
        out_shape=[jax.ShapeDtypeStruct((s, D), F32), jax.ShapeDtypeStruct((s, D), F32),
                   jax.ShapeDtypeStruct((s, DI), BF16), jax.ShapeDtypeStruct((s, D), BF16)],
        compiler_params=_params("parallel"),
    )(ypool, ysb, proj0, wout, x, vecs, vecs_next)


def _odd_out(proj1, wout, x1, vecs, cw, cb, target):
    s = x1.shape[0]
    tm = min(TME, s)
    hb = tm // HALO

    def body(gb_ref, gc_ref, u_ref, gt_ref, hgc_ref, hu_ref, w_ref, x1_ref, vec_ref, cw_ref, cb_ref, tg_ref,
             dx2_ref, y1_ref, acc_ref, ext_ref):
        i = pl.program_id(0)

        @pl.when(i == 0)
        def _():
            acc_ref[...] = jnp.zeros_like(acc_ref)

        ext_ref[HALO:, :] = gc_ref[...].astype(F32) * u_ref[...].astype(F32)
        ext_ref[0:HALO, :] = jnp.where(i > 0, hgc_ref[...].astype(F32) * hu_ref[...].astype(F32), 0.0)
        for c in range(DI // CT):
            cs = slice(c * CT, (c + 1) * CT)
            conv = (cb_ref[0:1, cs] + cw_ref[0:1, cs] * ext_ref[HALO - 2:HALO - 2 + tm, cs]
                    + cw_ref[1:2, cs] * ext_ref[HALO - 1:HALO - 1 + tm, cs]
                    + cw_ref[2:3, cs] * ext_ref[HALO:HALO + tm, cs])
            gt = gt_ref[:, cs].astype(F32)
            y1_ref[:, cs] = (gb_ref[:, cs].astype(F32) * conv * (gt * _sigmoid(gt))).astype(BF16)
        out = _dot(y1_ref[...], w_ref[...])
        x2 = x1_ref[...] + (1.0 + vec_ref[3:4, :]) * out
        r = lax.rsqrt(jnp.mean(x2 * x2, axis=-1, keepdims=True) + EPS)
        nrm = x2 * r
        fg = vec_ref[4:5, :]
        err = nrm * fg - tg_ref[...]
        acc_ref[1:2, :] += _rowsum(err * err) * (0.5 / D)
        dyf = err * (1.0 / D)
        acc_ref[0:1, :] += _rowsum(dyf * nrm)
        dn = dyf * fg
        dx2 = r * (dn - nrm * jnp.mean(dn * nrm, axis=-1, keepdims=True))
        dx2_ref[...] = dx2
        acc_ref[2:3, :] += _rowsum(dx2 * out)

    row = lambda i: (i, 0)
    halo = lambda col: (lambda i: (jnp.maximum(i * hb - 1, 0), col))
    const = lambda i: (0, 0)
    return pl.pallas_call(
        body, name="odd_out", grid=(s // tm,),
        in_specs=[pl.BlockSpec((tm, DI), lambda i: (i, 0)), pl.BlockSpec((tm, DI), lambda i: (i, 1)),
                  pl.BlockSpec((tm, DI), lambda i: (i, 2)), pl.BlockSpec((tm, DI), lambda i: (i, 3)),
                  pl.BlockSpec((HALO, DI), halo(1)), pl.BlockSpec((HALO, DI), halo(2)),
                  pl.BlockSpec((DI, D), const), pl.BlockSpec((tm, D), row), pl.BlockSpec((8, D), const),
                  pl.BlockSpec((8, DI), const), pl.BlockSpec((1, DI), const), pl.BlockSpec((tm, D), row)],
        out_specs=[pl.BlockSpec((tm, D), row), pl.BlockSpec((tm, DI), row), pl.BlockSpec((8, D), const)],
        out_shape=[jax.ShapeDtypeStruct((s, D), F32), jax.ShapeDtypeStruct((s, DI), BF16),
                   jax.ShapeDtypeStruct((8, D), F32)],
        scratch_shapes=[pltpu.VMEM((tm + HALO, DI), F32)],
        compiler_params=_params("arbitrary"),
    )(proj1, proj1, proj1, proj1, proj1, proj1, wout, x1, vecs, cw, cb, target)


def _odd_bwd(dx2, proj1, wout, vecs, cw, cb):
    s = dx2.shape[0]
    tm = min(TME, s)
    nb = s // tm
    hb = tm // HALO

    def body(dx2_ref, gb_ref, gc_ref, u_ref, gt_ref, hgc_ref, hu_ref, w_ref, vec_ref, cw_ref, cb_ref,
             dout_ref, dproj_ref, accv_ref, uext_ref, dext_ref, dy_ref):
        i = pl.program_id(0)
        blk = nb - 1 - i

        @pl.when(i == 0)
        def _():
            accv_ref[...] = jnp.zeros_like(accv_ref)
            dext_ref[tm:tm + 8, :] = jnp.zeros((8, DI), F32)

        dout = (dx2_ref[...] * (1.0 + vec_ref[3:4, :])).astype(BF16)
        dout_ref[...] = dout
        dy_ref[...] = _dot_nt(dout, w_ref[...])
        uext_ref[HALO:, :] = gc_ref[...].astype(F32) * u_ref[...].astype(F32)
        uext_ref[0:HALO, :] = jnp.where(blk > 0, hgc_ref[...].astype(F32) * hu_ref[...].astype(F32), 0.0)
        for c in range(DI // CT):
            cs = slice(c * CT, (c + 1) * CT)
            u0 = uext_ref[HALO - 2:HALO - 2 + tm, cs]
            u1 = uext_ref[HALO - 1:HALO - 1 + tm, cs]
            u2 = uext_ref[HALO:HALO + tm, cs]
            w0, w1, w2 = cw_ref[0:1, cs], cw_ref[1:2, cs], cw_ref[2:3, cs]
            conv = cb_ref[0:1, cs] + w0 * u0 + w1 * u1 + w2 * u2
            gt = gt_ref[:, cs].astype(F32)
            sg = _sigmoid(gt)
            gb = gb_ref[:, cs].astype(F32)
            dy = dy_ref[:, cs]
            t1 = dy * (gt * sg)
            dproj_ref[:, cs] = (t1 * conv).astype(BF16)
            dconv = t1 * gb
            dproj_ref[:, 3 * DI + c * CT:3 * DI + (c + 1) * CT] = (
                dy * gb * conv * (sg * (1.0 + gt * (1.0 - sg)))).astype(BF16)
            accv_ref[0:1, cs] += _rowsum(dconv * u0)
            accv_ref[1:2, cs] += _rowsum(dconv * u1)
            accv_ref[2:3, cs] += _rowsum(dconv * u2)
            accv_ref[3:4, cs] += _rowsum(dconv)
            dext_ref[0:tm, cs] = dconv
            duu = w2 * dconv + w1 * dext_ref[1:tm + 1, cs] + w0 * dext_ref[2:tm + 2, cs]
            dproj_ref[:, DI + c * CT:DI + (c + 1) * CT] = (duu * u_ref[:, cs].astype(F32)).astype(BF16)
            dproj_ref[:, 2 * DI + c * CT:2 * DI + (c + 1) * CT] = (duu * gc_ref[:, cs].astype(F32)).astype(BF16)
        dext_ref[tm:tm + 8, :] = dext_ref[0:8, :]

    rrow = lambda i: (nb - 1 - i, 0)
    rcol = lambda col: (lambda i: (nb - 1 - i, col))
    halo = lambda col: (lambda i: (jnp.maximum((nb - 1 - i) * hb - 1, 0), col))
    const = lambda i: (0, 0)
    return pl.pallas_call(
        body, name="odd_bwd", grid=(nb,),
        in_specs=[pl.BlockSpec((tm, D), rrow),
                  pl.BlockSpec((tm, DI), rcol(0)), pl.BlockSpec((tm, DI), rcol(1)),
                  pl.BlockSpec((tm, DI), rcol(2)), pl.BlockSpec((tm, DI), rcol(3)),
                  pl.BlockSpec((HALO, DI), halo(1)), pl.BlockSpec((HALO, DI), halo(2)),
                  pl.BlockSpec((DI, D), const), pl.BlockSpec((8, D), const),
                  pl.BlockSpec((8, DI), const), pl.BlockSpec((1, DI), const)],
        out_specs=[pl.BlockSpec((tm, D), rrow), pl.BlockSpec((tm, NO), rrow), pl.BlockSpec((8, DI), const)],
        out_shape=[jax.ShapeDtypeStruct((s, D), BF16), jax.ShapeDtypeStruct((s, NO), BF16),
                   jax.ShapeDtypeStruct((8, DI), F32)],
        scratch_shapes=[pltpu.VMEM((tm + HALO, DI), F32), pltpu.VMEM((tm + 8, DI), F32), pltpu.VMEM((tm, DI), F32)],
        compiler_params=_params("arbitrary"),
    )(dx2, proj1, proj1, proj1, proj1, proj1, proj1, wout, vecs, cw, cb)


def _grad_w_body(nk):
    def body(a_ref, b_ref, o_ref, ob_ref):
        k = pl.program_id(1)

        @pl.when(k == 0)
        def _():
            o_ref[...] = jnp.zeros_like(o_ref)

        o_ref[...] += _dot_tn(a_ref[...], b_ref[...])

        @pl.when(k == nk - 1)
        def _():
            ob_ref[...] = o_ref[...].astype(BF16)

    return body


def _grad_w_cols(a, b, name):
    s, m = a.shape
    ns = b.shape[1] // NCHIP
    ts = min(TM, s)
    blk = pl.BlockSpec((None, m, ns), lambda j, k: (j, 0, 0))
    return pl.pallas_call(
        _grad_w_body(s // ts), name=name, grid=(NCHIP, s // ts),
        in_specs=[pl.BlockSpec((ts, m), lambda j, k: (k, 0)),
                  pl.BlockSpec((ts, ns), lambda j, k: (k, j))],
        out_specs=[blk, blk],
        out_shape=[jax.ShapeDtypeStruct((NCHIP, m, ns), F32), jax.ShapeDtypeStruct((NCHIP, m, ns), BF16)],
        compiler_params=_params("parallel", "arbitrary"),
    )(a, b)


def _grad_w_rows(a, b, name):
    s = a.shape[0]
    ms = a.shape[1] // NCHIP
    n = b.shape[1]
    ts = min(TM, s)
    blk = pl.BlockSpec((None, ms, n), lambda i, k: (i, 0, 0))
    return pl.pallas_call(
        _grad_w_body(s // ts), name=name, grid=(NCHIP, s // ts),
        in_specs=[pl.BlockSpec((ts, ms), lambda i, k: (k, i)),
                  pl.BlockSpec((ts, n), lambda i, k: (k, 0))],
        out_specs=[blk, blk],
        out_shape=[jax.ShapeDtypeStruct((NCHIP, ms, n), F32), jax.ShapeDtypeStruct((NCHIP, ms, n), BF16)],
        compiler_params=_params("parallel", "arbitrary"),
    )(a, b)


def _inproj_bwd(dproj, w, x, dx_in, vecs, name, comm=None):
    s = x.shape[0]
    ns = w.shape[2]
    tm = min(TME, s)
    ni = s // tm
    comm = comm or _NO_COMM
    nci, nco = len(comm.arrays), len(comm.out_shape)

    def body(*refs):
        dp_ref, w_hbm, x_ref, dxin_ref, vec_ref = refs[:5]
        dx_ref, acc_ref = refs[5 + nci:7 + nci]
        w_ref = refs[7 + nci + nco]
        cargs = (refs[5:5 + nci], refs[7 + nci:7 + nci + nco], refs[8 + nci + nco:])
        i = pl.program_id(0)

        @pl.when(i == 0)
        def _():
            acc_ref[...] = jnp.zeros_like(acc_ref)
            if nci:
                comm.start(*cargs)
            pltpu.sync_copy(w_hbm, w_ref)

        dh = _dot_nt(dp_ref[:, 0:ns], w_ref[0])
        for j in range(1, NCHIP):
            dh = dh + _dot_nt(dp_ref[:, j * ns:(j + 1) * ns], w_ref[j])
        xv = x_ref[...]
        r = lax.rsqrt(jnp.mean(xv * xv, axis=-1, keepdims=True) + EPS)
        nrm = xv * r
        g = vec_ref[0:1, :]
        sc1 = 1.0 + vec_ref[1:2, :]
        dhn = dh * nrm
        acc_ref[0:1, :] += _rowsum(dh)
        acc_ref[1:2, :] += _rowsum(dhn) * g
        acc_ref[2:3, :] += _rowsum(dhn) * sc1
        dn = dh * (g * sc1)
        dx_ref[...] = dxin_ref[...] + r * (dn - nrm * jnp.mean(dn * nrm, axis=-1, keepdims=True))

        if nci:
            pl.when(i == ni - 1)(lambda: comm.finish(*cargs))

    row = lambda i: (i, 0)
    const = lambda i: (0, 0)
    return pl.pallas_call(
        body, name=name, grid=(ni,),
        in_specs=[pl.BlockSpec((tm, NCHIP * ns), row), ANY,
                  pl.BlockSpec((tm, D), row), pl.BlockSpec((tm, D), row), pl.BlockSpec((8, D), const)] + [ANY] * nci,
        out_specs=[pl.BlockSpec((tm, D), row), pl.BlockSpec((8, D), const)] + [ANY] * nco,
        out_shape=[jax.ShapeDtypeStruct((s, D), F32), jax.ShapeDtypeStruct((8, D), F32)] + comm.out_shape,
        scratch_shapes=[pltpu.VMEM(w.shape, BF16)] + comm.sems,
        compiler_params=_params("arbitrary"),
    )(dproj, w, x, dx_in, vecs, *comm.arrays)


def _even_bwd(dx1, out0, ypool, ysb, proj0, wout, vecs):
    s = dx1.shape[0]
    tm = min(TME, s)

    def body(dx1_ref, out0_ref, yp_ref, ys_ref, gate_ref, w_ref, vec_ref,
             dout_ref, dyp_ref, dys_ref, dgt_ref, acc_ref):
        @pl.when(pl.program_id(0) == 0)
        def _():
            acc_ref[...] = jnp.zeros_like(acc_ref)

        dx1v = dx1_ref[...]
        acc_ref[0:1, :] += _rowsum(dx1v * out0_ref[...])
        dout = (dx1v * (1.0 + vec_ref[3:4, :])).astype(BF16)
        dout_ref[...] = dout
        dyg = _dot_nt(dout, w_ref[...])
        gt = gate_ref[...].astype(F32)
        sg = _sigmoid(gt)
        sl = gt * sg
        dsl = sg * (1.0 + gt * (1.0 - sg))
        dyp_ref[...] = dyg[:, :DP] * sl[:, :DP]
        dys_ref[...] = dyg[:, DP:] * sl[:, DP:]
        dgt_ref[:, :DP] = (dyg[:, :DP] * yp_ref[...] * dsl[:, :DP]).astype(BF16)
        dgt_ref[:, DP:] = (dyg[:, DP:] * ys_ref[...] * dsl[:, DP:]).astype(BF16)

    row = lambda i: (i, 0)
    const = lambda i: (0, 0)
    return pl.pallas_call(
        body, name="even_bwd", grid=(s // tm,),
        in_specs=[pl.BlockSpec((tm, D), row), pl.BlockSpec((tm, D), row),
                  pl.BlockSpec((tm, DP), row), pl.BlockSpec((tm, DP), row),
                  pl.BlockSpec((tm, DI), lambda i: (i, 2)),
                  pl.BlockSpec((DI, D), const), pl.BlockSpec((8, D), const)],
        out_specs=[pl.BlockSpec((tm, D), row), pl.BlockSpec((tm, DP), row), pl.BlockSpec((tm, DP), row),
                   pl.BlockSpec((tm, DI), lambda i: (i, NE // DI - 1)), pl.BlockSpec((8, D), const)],
        out_shape=[jax.ShapeDtypeStruct((s, D), BF16), jax.ShapeDtypeStruct((s, DP), F32),
                   jax.ShapeDtypeStruct((s, DP), F32), jax.ShapeDtypeStruct((s, NE), BF16),
                   jax.ShapeDtypeStruct((8, D), F32)],
        compiler_params=_params("arbitrary"),
    )(dx1, out0, ypool, ysb, proj0, wout, vecs)


def _pool_bwd(dyp, p, pw, pscale, dproj):
    s = dyp.shape[0]
    tm = min(TM, s)
    nb = s // tm
    hb = tm // 16

    def body(dy_ref, dyh_ref, p_ref, w_ref, sc_ref, dproj_hbm, du_ref, dw_ref, acc_ref, ext_ref):
        i = pl.program_id(0)

        @pl.when(i == 0)
        def _():
            dw_ref[...] = jnp.zeros_like(dw_ref)
            acc_ref[...] = jnp.zeros_like(acc_ref)

        t = i * tm + lax.broadcasted_iota(jnp.int32, (tm + 16, 1), 0)
        for g, wdw in enumerate(WINDOWS):
            cs = slice(g * PG, (g + 1) * PG)
            sc = sc_ref[:, cs]
            dy = dy_ref[:, cs]
            dyh = jnp.where(i < nb - 1, dyh_ref[:, cs], 0.0)
            pb = p_ref[:, cs]
            wg = w_ref[g]
            acc_ref[0:1, cs] += _rowsum(dy * _dot(pb, wg))
            dypre = (dy * sc).astype(BF16)
            dw_ref[g] += _dot_tn(pb, dypre)
            dp = _dot_nt(dypre, wg)
            dph = _dot_nt((dyh * sc).astype(BF16), wg)
            inv = 1.0 / jnp.minimum(t + 1, wdw).astype(F32)
            ext_ref[0:tm, cs] = dp * inv[0:tm]
            ext_ref[tm:tm + 16, cs] = dph * inv[tm:tm + 16]
            acc = ext_ref[0:tm, cs]
            for j in range(1, wdw):
                acc = acc + ext_ref[j:j + tm, cs]
            du_ref[:, cs] = (acc - dp).astype(BF16)

    row = lambda i: (i, 0)
    return pl.pallas_call(
        body, name="pool_bwd", grid=(nb,),
        in_specs=[pl.BlockSpec((tm, DP), row),
                  pl.BlockSpec((16, DP), lambda i: (jnp.minimum((i + 1) * hb, s // 16 - 1), 0)),
                  pl.BlockSpec((tm, DP), row),
                  pl.BlockSpec((4, PG, PG), lambda i: (0, 0, 0)),
                  pl.BlockSpec((1, DP), lambda i: (0, 0)), ANY],
        out_specs=[pl.BlockSpec((tm, DP), row), pl.BlockSpec((4, PG, PG), lambda i: (0, 0, 0)),
                   pl.BlockSpec((8, DP), lambda i: (0, 0))],
        out_shape=[jax.ShapeDtypeStruct((s, NE), BF16), jax.ShapeDtypeStruct((4, PG, PG), F32),
                   jax.ShapeDtypeStruct((8, DP), F32)],
        input_output_aliases={5: 0},
        scratch_shapes=[pltpu.VMEM((tm + 16, DP), F32)],
        compiler_params=_params("arbitrary"),
    )(dyp, dyp, p, pw, pscale, dproj)


def _attn_bwd(proj0, ysb, dys, dproj, comm=None):
    s = proj0.shape[0]
    nq = s // BQ
    kpq = BQ // BK
    nsteps = DP // 128
    comm = comm or _NO_COMM
    nci, nco = len(comm.arrays), len(comm.out_shape)

    def body(*refs):
        q_ref, k_ref, v_ref, o_ref, do_ref = refs[:5]
        dproj_hbm = refs[6 + nci]
        scratch = refs[7 + nci + nco:]
        qn_ref, k8_ref, vb_ref, dob_ref, dka_ref, dva_ref, dq_ref, dk_ref, dv_ref, part_sems = scratch[:10]
        cargs = (refs[6:6 + nci], refs[7 + nci:7 + nci + nco], scratch[10:])
        step = pl.program_id(0)
        if nci:
            pl.when(step == 0)(lambda: comm.start(*cargs))

        qn_ref[...] = (-q_ref[...]).astype(BF16)
        k8_ref[...] = (k_ref[...] * INV_SQRT_HD).astype(BF16)
        vb_ref[...] = v_ref[...].astype(BF16)
        dob_ref[...] = do_ref[...].astype(BF16)
        dka_ref[...] = jnp.zeros_like(dka_ref)
        dva_ref[...] = jnp.zeros_like(dva_ref)
        after, from_on = _tri_masks()
        causal = _causal_mask()
        heads = [slice(HD * h, HD * (h + 1)) for h in range(2)]
        hs = range(len(heads))
        lo, hi = (0, BK), (BK, BK)

        def qstep(qi, carry):
            q0 = pl.multiple_of(qi * BQ, BQ)
            qn = [qn_ref[pl.ds(q0, BQ), ls] for ls in heads]
            do = [dob_ref[pl.ds(q0, BQ), ls] for ls in heads]
            total = [jnp.sum(do[h].astype(F32) * o_ref[pl.ds(q0, BQ), ls], axis=1, keepdims=True)
                     for h, ls in enumerate(heads)]

            def sweep(lanes, state):
                rows_of = lambda i: slice(lanes[i][1][0], lanes[i][1][0] + lanes[i][1][1])
                k8 = [k8_ref[pl.ds(k0, BK), heads[h]] for h, _, k0, _ in lanes]
                nz = [_dot_nt(qn[lane[0]][rows_of(i)], k8[i]) for i, lane in enumerate(lanes)]
                da = [_dot_nt(do[h][rows_of(i)], vb_ref[pl.ds(k0, BK), heads[h]]) for i, (h, _, k0, _) in enumerate(lanes)]
                ll = [_sb_logits(nz[i], lane[3]) for i, lane in enumerate(lanes)]
                aft = [_dot(ll[i][0].astype(BF16), after) for i in range(len(lanes))]
                state = dict(state)
                for i, (h, rows, k0, mask) in enumerate(lanes):
                    dq_acc, c, cg = state[h, rows]
                    a = jnp.exp(ll[i][1] + aft[i] + c)
                    if mask is not None:
                        a = jnp.where(mask, a, 0.0)
                    ab = a.astype(BF16)
                    g = da[i] * ab.astype(F32)
                    suf = _split_dot(g, from_on)
                    dz = g - jnp.exp(ll[i][1]) * (g + ((total[h][rows_of(i)] - cg) - suf))
                    if mask is not None:
                        dz = jnp.where(mask, dz, 0.0)
                    dzb = dz.astype(BF16)
                    dka_ref[pl.ds(k0, BK), heads[h]] += _dot_tn(dzb, qn[h][rows_of(i)])
                    dva_ref[pl.ds(k0, BK), heads[h]] += _dot_tn(ab, do[h][rows_of(i)])
                    state[h, rows] = (dq_acc + _dot(dzb, k8[i]), c + aft[i][:, 0:1] + ll[i][0][:, 0:1],
                                      cg + suf[:, 0:1])
                return state

            k_lo, k_hi = q0, pl.multiple_of(q0 + BK, BK)
            zero = jnp.zeros((BK, 1), F32)
            st = {(h, r): (jnp.zeros((BK, HD), F32), zero, zero) for h in hs for r in (lo, hi)}
            st = sweep([(h, hi, k_hi, causal) for h in hs] + [(h, lo, k_lo, causal) for h in hs]
                       + [(h, hi, k_lo, None) for h in hs], st)
            def left_of(rows):
                def step(j, part):
                    k0 = pl.multiple_of(q0 - (j + 1) * BK, BK)
                    got = sweep([(h, rows, k0, None) for h in hs], {(h, rows): part[h] for h in hs})
                    return tuple(got[h, rows] for h in hs)
                return step

            done = {r: _sweep_left(qi * (BQ // BK), tuple(st[h, r] for h in hs), left_of(r)) for r in (lo, hi)}
            for h, ls in enumerate(heads):
                dq_ref[pl.ds(q0, BQ), ls] = jnp.concatenate([done[lo][h][0], done[hi][h][0]], axis=0).astype(BF16)
            return carry

        lax.fori_loop(0, nq, qstep, 0)
        dk_ref[...] = (dka_ref[...] * (-INV_SQRT_HD)).astype(BF16)
        dv_ref[...] = dva_ref[...].astype(BF16)
        lanes0 = pl.multiple_of(step * 128, 128)
        parts = [pltpu.make_async_copy(src, dproj_hbm.at[:, pl.ds((1 + k) * DP + lanes0, 128)], part_sems.at[k])
                 for k, src in enumerate((dq_ref, dk_ref, dv_ref))]
        for cp in parts:
            cp.start()
        for cp in parts:
            cp.wait()
        if nci:
            pl.when(step == nsteps - 1)(lambda: comm.finish(*cargs))

    col = lambda h: (0, h)
    return pl.pallas_call(
        body, name="attn_bwd", grid=(nsteps,),
        in_specs=[pl.BlockSpec((s, 128), lambda h: (0, 8 + h)),
                  pl.BlockSpec((s, 128), lambda h: (0, 16 + h)),
                  pl.BlockSpec((s, 128), lambda h: (0, 24 + h)),
                  pl.BlockSpec((s, 128), col), pl.BlockSpec((s, 128), col), ANY] + [ANY] * nci,
        out_specs=[ANY] * (1 + nco),
        out_shape=[jax.ShapeDtypeStruct((s, NE), BF16)] + comm.out_shape,
        input_output_aliases={5: 0},
        scratch_shapes=([pltpu.VMEM((s, 128), BF16)] * 4 + [pltpu.VMEM((s, 128), F32)] * 2
                        + [pltpu.VMEM((s, 128), BF16)] * 3 + [pltpu.SemaphoreType.DMA((3,))] + comm.sems),
        compiler_params=_params("arbitrary"),
    )(proj0, proj0, proj0, ysb, dys, dproj, *comm.arrays)


def _adamw_math(w, g, m, v):
    m2 = B1 * m + (1.0 - B1) * g
    v2 = B2 * v + (1.0 - B2) * (g * g)
    m_hat = m2 / (1.0 - B1 ** STEP)
    v_hat = v2 / (1.0 - B2 ** STEP)
    return -LR * (m_hat / (jnp.sqrt(v_hat) + EPS_ADAM) + WD * w), m2, v2


def _adamw(w, g, m, v, name):
    r, c = w.shape
    tr = r
    while tr * c * 4 > (1 << 20) and tr % 16 == 0:
        tr //= 2

    def body(w_ref, g_ref, m_ref, v_ref, d_ref, m2_ref, v2_ref):
        d_ref[...], m2_ref[...], v2_ref[...] = _adamw_math(w_ref[...], g_ref[...], m_ref[...], v_ref[...])

    spec = pl.BlockSpec((tr, c), lambda i: (i, 0))
    return pl.pallas_call(
        body, name=name, grid=(r // tr,),
        in_specs=[spec] * 4, out_specs=[spec] * 3,
        out_shape=[jax.ShapeDtypeStruct((r, c), F32)] * 3,
        compiler_params=_params("parallel"),
    )(w, g, m, v)


def _local_step(x, target, vecs0, vecs1, win0, rest, pscale, cw8, cb, sel=None):
    dist = sel is not None
    h0 = _norm_mod(x, vecs0, "norm0")
    if dist:
        chip = sel[0]
        order = jnp.stack([chip, chip ^ 2, chip ^ 1, chip ^ 3])
        proj0, win0, pw = _inproj_gathering(h0, win0, order, "inproj0", _gather_comm([rest[0]]))
        pw = _group_major(pw)
    else:
        proj0, = _inproj(h0, win0, "inproj0")
        pw = rest[0]
    p, ypool = _pool_fwd(proj0, pw, pscale)
    ysb, *got = _attn_fwd(proj0, _gather_comm(list(rest[1:4])) if dist else None)
    wout0, win1, wout1 = (got[0].reshape(DI, D), got[1], got[2].reshape(DI, D)) if dist else rest[1:4]
    x1, out0, yg, h1 = _even_out(ypool, ysb, proj0, wout0, x, vecs0, vecs1)
    proj1, = _inproj(h1, win1, "inproj1")
    dx2, y1, acc_f = _odd_out(proj1, wout1, x1, vecs1, cw8, cb, target)

    def chip_partials(grads, names):
        from_sibling = _send_halves([g16 for _, g16 in grads], "rs_send_halves_" + names[0])
        part = [_add_halves(g32, t, sel[1:2], "rs_add_halves_" + nm)
                for (g32, _), t, nm in zip(grads, from_sibling, names)]
        return [p32 for p32, _ in part], _exchange_comm([p16 for _, p16 in part])

    dout1, dproj1, acc_cv = _odd_bwd(dx2, proj1, wout1, vecs1, cw8, cb)
    g_wout1 = _grad_w_rows(y1, dout1, "grad_wout1")
    g_win1 = _grad_w_cols(h1, dproj1, "grad_win1")
    dx1, acc_n1 = _inproj_bwd(dproj1, win1, x1, dx2, vecs1, "inproj1_bwd")

    dout0, dyp, dys, dproj0, acc_g0 = _even_bwd(dx1, out0, ypool, ysb, proj0, wout0, vecs0)
    g_wout0 = _grad_w_rows(yg, dout0, "grad_wout0")
    dproj0, g_pw, acc_ps = _pool_bwd(dyp, p, pw, pscale, dproj0)
    g_pool = _chip_major(g_pw, PG // NCHIP)
    early = [(g_pool, g_pool.astype(BF16)), g_wout0, g_win1, g_wout1]
    part_a, swap_a = chip_partials(early, ["pool", "wout0", "win1", "wout1"]) if dist else (None, None)
    dproj0, *got_a = _attn_bwd(proj0, ysb, dys, dproj0, swap_a)
    g_win0 = _grad_w_cols(h0, dproj0, "grad_win0")
    part_b, swap_b = chip_partials([g_win0], ["win0"]) if dist else (None, None)
    dx0, acc_n0, *got_b = _inproj_bwd(dproj0, win0, x, dx1, vecs0, "inproj0_bwd", swap_b)

    if dist:
        names = ["win0", "pool", "wout0", "win1", "wout1"]
        halves = [_add_partials(p32, t, sel, "rs_add_partials_" + nm)
                  for p32, t, nm in zip(part_b + part_a, got_b + got_a, names)]
        grads = tuple(_join_halves(halves))
    else:
        grads = (g_win0[0], g_pw, g_wout0[0], g_win1[0], g_wout1[0])

    sums = dict(
        dm0=jnp.concatenate([acc_n0[0:2], acc_g0[0:1]], axis=0),
        dm1=jnp.concatenate([acc_n1[0:2], acc_f[2:3]], axis=0),
        norm_g=jnp.concatenate([acc_n0[2:3], acc_n1[2:3]], axis=0),
        pool_scale=acc_ps[0:1], final_g=acc_f[0:1], loss=acc_f[1:2],
        conv_w=acc_cv[0:3], conv_b=acc_cv[3:4])
    return dx0, grads, sums


ANY = pl.BlockSpec(memory_space=pl.ANY)
CHIP_FLIPS = ((1, 0), (0, 1), (1, 1))


def _place():
    return lax.axis_index("x"), lax.axis_index("y"), lax.axis_index("c")


def _flip(v, f):
    return 1 - v if f else v


def _allgather8(v, name):
    m_per, n = v.shape

    def body(x_ref, out_ref, send_sems, recv_sems, local_sem):
        x, y, c = _place()
        me, sibling = (x, y, c), (x, y, 1 - c)
        chips = [(_flip(x, fx), _flip(y, fy)) for fx, fy in CHIP_FLIPS]

        def rows(px, py, pc):
            return out_ref.at[pl.ds((4 * px + 2 * py + pc) * m_per, m_per), :]

        def copy(k, block, to, src=None):
            return pltpu.make_async_remote_copy(
                src_ref=rows(*block) if src is None else src, dst_ref=rows(*block),
                send_sem=send_sems.at[k], recv_sem=recv_sems.at[k], device_id=to, device_id_type=MESH)

        mine = pltpu.make_async_copy(x_ref, rows(*me), local_sem)
        mine.start()
        first = [copy(0, me, sibling, src=x_ref)]
        first += [copy(1 + j, me, (*chip, c), src=x_ref) for j, chip in enumerate(chips)]
        for cp in first:
            cp.start()
        passed = [copy(4 + j, (*chip, c), sibling) for j, chip in enumerate(chips)]
        for j, chip in enumerate(chips):
            copy(1 + j, (*chip, c), me).wait_recv()
            passed[j].start()
        copy(0, sibling, me).wait_recv()
        for j, chip in enumerate(chips):
            copy(4 + j, (*chip, 1 - c), me).wait_recv()
        for cp in first + passed:
            cp.wait_send()
        mine.wait()

    return pl.pallas_call(
        body, name=name,
        out_shape=jax.ShapeDtypeStruct((NDEV * m_per, n), v.dtype),
        in_specs=[pl.BlockSpec(memory_space=pltpu.VMEM)],
        out_specs=pl.BlockSpec(memory_space=pltpu.VMEM),
        scratch_shapes=[pltpu.SemaphoreType.DMA((7,)), pltpu.SemaphoreType.DMA((7,)), pltpu.SemaphoreType.DMA],
    )(v)


class _Comm:
    def __init__(self, arrays, out_shape, sems, start, finish):
        self.arrays, self.out_shape, self.sems, self.start, self.finish = arrays, out_shape, sems, start, finish


_NO_COMM = _Comm([], [], [], None, None)


def _gather_comm(shards):
    n = len(shards)

    def pieces(ins, outs, sems, kinds):
        x, y, c = _place()
        ici_send, ici_recv, fwd_send, fwd_recv = sems[:4]
        own, sibling = 2 * x + y, (x, y, 1 - c)
        made = {kind: [] for kind in kinds}
        for w in range(n):
            r2 = ins[w].shape[0] // 2
            mine, other = pl.ds(c * r2, r2), pl.ds((1 - c) * r2, r2)
            for d, (fx, fy) in enumerate(CHIP_FLIPS):
                px, py, k = _flip(x, fx), _flip(y, fy), 3 * w + d
                peer = 2 * px + py
                ici = dict(send_sem=ici_send.at[k], recv_sem=ici_recv.at[k], device_id=(px, py, c), device_id_type=MESH)
                fwd = dict(send_sem=fwd_send.at[k], recv_sem=fwd_recv.at[k], device_id=sibling, device_id_type=MESH)
                if "ici_out" in kinds:
                    made["ici_out"].append(pltpu.make_async_remote_copy(
                        src_ref=ins[w].at[mine, :], dst_ref=outs[w].at[own, mine, :], **ici))
                if "ici_in" in kinds:
                    made["ici_in"].append(pltpu.make_async_remote_copy(
                        src_ref=ins[w].at[mine, :], dst_ref=outs[w].at[peer, mine, :], **ici))
                if "fwd_out" in kinds:
                    made["fwd_out"].append(pltpu.make_async_remote_copy(
                        src_ref=outs[w].at[peer, mine, :], dst_ref=outs[w].at[peer, mine, :], **fwd))
                if "fwd_in" in kinds:
                    made["fwd_in"].append(pltpu.make_async_remote_copy(
                        src_ref=outs[w].at[peer, other, :], dst_ref=outs[w].at[peer, other, :], **fwd))
        return made

    def stage_rows(a):
        return min(a.shape[0], 256)

    def start(ins, outs, sems):
        for cp in pieces(ins, outs, sems, ("ici_out",))["ici_out"]:
            cp.start()
        x, y, _ = _place()
        for w, stage in enumerate(sems[4:]):
            rows = stage.shape[0]
            for r in range(0, ins[w].shape[0], rows):
                pltpu.sync_copy(ins[w].at[pl.ds(r, rows), :], stage)
                pltpu.sync_copy(stage, outs[w].at[2 * x + y, pl.ds(r, rows), :])

    def finish(ins, outs, sems):
        made = pieces(ins, outs, sems, ("ici_out", "ici_in", "fwd_out", "fwd_in"))
        for arrived, onward in zip(made["ici_in"], made["fwd_out"]):
            arrived.wait_recv()
            onward.start()
        for cp in made["fwd_in"]:
            cp.wait_recv()
        for cp in made["ici_out"] + made["fwd_out"]:
            cp.wait_send()

    dma = pltpu.SemaphoreType.DMA
    return _Comm(list(shards), [jax.ShapeDtypeStruct((NCHIP,) + a.shape, a.dtype) for a in shards],
                 [dma((3 * n,))] * 4 + [pltpu.VMEM((stage_rows(a), a.shape[1]), a.dtype) for a in shards],
                 start, finish)


def _exchange_comm(parts):
    n = len(parts)

    def copies(ins, outs, sems):
        x, y, c = _place()
        send_sems, recv_sems = sems
        out = []
        for w in range(n):
            for d, (fx, fy) in enumerate(CHIP_FLIPS):
                px, py = _flip(x, fx), _flip(y, fy)
                out.append(pltpu.make_async_remote_copy(
                    src_ref=ins[w].at[2 * px + py], dst_ref=outs[w].at[d], send_sem=send_sems.at[3 * w + d],
                    recv_sem=recv_sems.at[3 * w + d], device_id=(px, py, c), device_id_type=MESH))
        return out

    def start(ins, outs, sems):
        for cp in copies(ins, outs, sems):
            cp.start()

    def finish(ins, outs, sems):
        cps = copies(ins, outs, sems)
        for cp in cps:
            cp.wait_recv()
        for cp in cps:
            cp.wait_send()

    dma = pltpu.SemaphoreType.DMA
    return _Comm(list(parts), [jax.ShapeDtypeStruct((3,) + p.shape[1:], BF16) for p in parts],
                 [dma((3 * n,))] * 2, start, finish)


def _send_halves(grads, name):
    n = len(grads)

    def body(*refs):
        ins, outs = refs[:n], refs[n:2 * n]
        send_sems, recv_sems = refs[2 * n:]
        x, y, c = _place()
        copies = []
        for w in range(n):
            r2 = ins[w].shape[1] // 2
            cp = pltpu.make_async_remote_copy(
                src_ref=ins[w].at[:, pl.ds((1 - c) * r2, r2), :], dst_ref=outs[w],
                send_sem=send_sems.at[w], recv_sem=recv_sems.at[w], device_id=(x, y, 1 - c), device_id_type=MESH)
            cp.start()
            copies.append(cp)
        for cp in copies:
            cp.wait_recv()
        for cp in copies:
            cp.wait_send()

    return pl.pallas_call(
        body, name=name,
        out_shape=[jax.ShapeDtypeStruct((NCHIP, g.shape[1] // 2, g.shape[2]), g.dtype) for g in grads],
        in_specs=[ANY] * n, out_specs=[ANY] * n,
        scratch_shapes=[pltpu.SemaphoreType.DMA((n,)), pltpu.SemaphoreType.DMA((n,))],
    )(*grads)


def _row_tile(rows, cols):
    tr = rows
    while tr * cols * 4 > (1 << 20) and tr % 16 == 0:
        tr //= 2
    return tr


def _add_halves(g, t1, core, name):
    _, r, cdim = g.shape
    r2 = r // 2
    tr = _row_tile(r2, cdim)
    nt = r2 // tr

    def body(core_ref, g_ref, t_ref, p_ref, pb_ref):
        p = g_ref[...] + t_ref[...].astype(F32)
        p_ref[...] = p
        pb_ref[...] = p.astype(BF16)

    blk = pl.BlockSpec((None, tr, cdim), lambda j, i, core_ref: (j, i, 0))
    return pl.pallas_call(
        body, name=name,
        grid_spec=pltpu.PrefetchScalarGridSpec(
            num_scalar_prefetch=1, grid=(NCHIP, nt),
            in_specs=[pl.BlockSpec((None, tr, cdim), lambda j, i, core_ref: (j, core_ref[0] * nt + i, 0)), blk],
            out_specs=[blk, blk]),
        out_shape=[jax.ShapeDtypeStruct((NCHIP, r2, cdim), F32), jax.ShapeDtypeStruct((NCHIP, r2, cdim), BF16)],
        compiler_params=_params("parallel", "parallel"),
    )(core, g, t1)


def _add_partials(p, t2, sel, name):
    _, r2, cdim = p.shape
    tr = _row_tile(r2, cdim)
    nt = r2 // tr

    def body(sel_ref, p_ref, t_ref, o_ref):
        o_ref[...] = ((p_ref[...] + t_ref[0].astype(F32)) + t_ref[1].astype(F32)) + t_ref[2].astype(F32)

    return pl.pallas_call(
        body, name=name,
        grid_spec=pltpu.PrefetchScalarGridSpec(
            num_scalar_prefetch=1, grid=(nt,),
            in_specs=[pl.BlockSpec((None, tr, cdim), lambda i, sel_ref: (sel_ref[0], i, 0)),
                      pl.BlockSpec((3, tr, cdim), lambda i, sel_ref: (0, i, 0))],
            out_specs=pl.BlockSpec((tr, cdim), lambda i, sel_ref: (sel_ref[1] * nt + i, 0))),
        out_shape=jax.ShapeDtypeStruct((2 * r2, cdim), F32),
        compiler_params=_params("parallel"),
    )(sel, p, t2)


def _join_halves(grads):
    n = len(grads)

    def body(*refs):
        bufs = refs[n:2 * n]
        send_sems, recv_sems = refs[2 * n:]
        x, y, c = _place()
        copies = []
        for w in range(n):
            r2 = bufs[w].shape[0] // 2
            mine = bufs[w].at[pl.ds(c * r2, r2), :]
            cp = pltpu.make_async_remote_copy(
                src_ref=mine, dst_ref=mine, send_sem=send_sems.at[w], recv_sem=recv_sems.at[w],
                device_id=(x, y, 1 - c), device_id_type=MESH)
            cp.start()
            copies.append(cp)
        for w in range(n):
            r2 = bufs[w].shape[0] // 2
            theirs = bufs[w].at[pl.ds((1 - c) * r2, r2), :]
            pltpu.make_async_remote_copy(
                src_ref=theirs, dst_ref=theirs, send_sem=send_sems.at[w], recv_sem=recv_sems.at[w],
                device_id=(x, y, 1 - c), device_id_type=MESH).wait_recv()
        for cp in copies:
            cp.wait_send()

    return pl.pallas_call(
        body, name="rs_join_halves",
        out_shape=[jax.ShapeDtypeStruct(g.shape, F32) for g in grads],
        in_specs=[ANY] * n, out_specs=[ANY] * n, input_output_aliases={w: w for w in range(n)},
        scratch_shapes=[pltpu.SemaphoreType.DMA((n,)), pltpu.SemaphoreType.DMA((n,))],
    )(*grads)


def _ada_fwd(c_all, ada_w):
    nl, _, ns = ada_w.shape

    def body(c_ref, w_ref, o_ref):
        cv = c_ref[...]
        o_ref[...] = _dot((cv * _sigmoid(cv)).astype(BF16), w_ref[...].astype(BF16))

    return pl.pallas_call(
        body, name="ada_fwd", grid=(nl,),
        in_specs=[pl.BlockSpec((NDEV, D), lambda i: (0, 0)), pl.BlockSpec((None, D, ns), lambda i: (i, 0, 0))],
        out_specs=pl.BlockSpec((None, NDEV, ns), lambda i: (i, 0, 0)),
        out_shape=jax.ShapeDtypeStruct((nl, NDEV, ns), F32),
        compiler_params=_params("parallel"),
    )(c_all, ada_w)


PACK_ROWS = 24


def _reduce_packed(gathered):
    def body(g_ref, tot_ref, loss_ref):
        tot = g_ref[0:PACK_ROWS, :]
        for dev in range(1, NDEV):
            tot = tot + g_ref[dev * PACK_ROWS:(dev + 1) * PACK_ROWS, :]
        tot_ref[...] = tot
        loss_ref[...] = jnp.zeros((8, 128), F32) + jnp.sum(tot[10:11, :])

    return pl.pallas_call(
        body, name="reduce_packed",
        out_shape=[jax.ShapeDtypeStruct((PACK_ROWS, D), F32), jax.ShapeDtypeStruct((8, 128), F32)],
    )(gathered)


def _ada_w_update(c_t, dms, w, m, v):
    nl, _, ns = w.shape
    tr = 256

    def body(ct_ref, dm_ref, w_ref, m_ref, v_ref, g_ref, d_ref, m2_ref, v2_ref):
        ct = ct_ref[...]
        sc = ct * _sigmoid(ct)
        dm = dm_ref[...]
        g = sc[:, 0:1] * dm[0:1, :]
        for b in range(1, NDEV):
            g = g + sc[:, b:b + 1] * dm[b:b + 1, :]
        g_ref[...] = g
        d_ref[...], m2_ref[...], v2_ref[...] = _adamw_math(w_ref[...], g, m_ref[...], v_ref[...])

    blk = pl.BlockSpec((None, tr, ns), lambda i, j: (i, j, 0))
    return pl.pallas_call(
        body, name="ada_w_update", grid=(nl, D // tr),
        in_specs=[pl.BlockSpec((tr, NDEV), lambda i, j: (j, 0)),
                  pl.BlockSpec((None, NDEV, ns), lambda i, j: (i, 0, 0)), blk, blk, blk],
        out_specs=[blk] * 4,
        out_shape=[jax.ShapeDtypeStruct((nl, D, ns), F32)] * 4,
        compiler_params=_params("parallel", "parallel"),
    )(c_t, dms, w, m, v)


def _chip_major(a, parts):
    g, _, cdim = a.shape
    return jnp.transpose(a.reshape(g, NCHIP, parts, cdim), (1, 0, 2, 3)).reshape(NCHIP, g * parts, cdim)


def _group_major(a):
    return jnp.transpose(a.reshape(NCHIP, 4, PG // NCHIP, PG), (1, 0, 2, 3)).reshape(4, PG, PG)


def kernel(x, c, norm_g, ada_w, ada_b, even_w_in, pool_w, pool_scale, even_w_out, odd_w_in, conv_w, conv_b, odd_w_out, final_g, loss_target, m_norm_g, m_ada_w, m_ada_b, m_even_w_in, m_pool_w, m_pool_scale, m_even_w_out, m_odd_w_in, m_conv_w, m_conv_b, m_odd_w_out, m_final_g, v_norm_g, v_ada_w, v_ada_b, v_even_w_in, v_pool_w, v_pool_scale, v_even_w_out, v_odd_w_in, v_conv_w, v_conv_b, v_odd_w_out, v_final_g):
    ix, iy, ic = _place()
    chip = 2 * ix + iy
    batch = 2 * chip + ic
    sel = jnp.stack([chip, ic]).astype(jnp.int32)
    ns_ada = ada_w.shape[2]
    ns_conv = conv_b.shape[1]

    conv_rows = jnp.pad(jnp.concatenate([conv_w[0], conv_b], axis=0), ((0, 3), (0, D - ns_conv)))
    first = _allgather8(jnp.concatenate([c, conv_rows], axis=0), "gather_c_conv").reshape(NCHIP, 2, 8, D)
    c_all = first[:, :, 0].reshape(NDEV, D)
    cw_full = jnp.transpose(first[:, 0, 1:5, 0:ns_conv], (1, 0, 2)).reshape(4, DI)
    cw8 = jnp.concatenate([cw_full[0:3], jnp.zeros((5, DI), F32)], axis=0)
    cb_full = cw_full[3:4]

    m_cols = _allgather8(_ada_fwd(c_all, ada_w).reshape(2 * NDEV, ns_ada), "gather_ada")
    m_cols = m_cols.reshape(NCHIP, 2, 2, NDEV, ns_ada)[:, 0]
    m_mine = lax.dynamic_index_in_dim(m_cols, batch, axis=2, keepdims=False)
    m_mine = jnp.transpose(m_mine, (1, 0, 2)).reshape(2, 3 * D) + ada_b
    zrow = jnp.zeros((3, D), F32)

    def vec_rows(i):
        sh, sc, gt = m_mine[i, 0:D], m_mine[i, D:2 * D], m_mine[i, 2 * D:3 * D]
        return jnp.concatenate([jnp.stack([norm_g[i], sc, sh, gt, final_g]), zrow], axis=0)

    shards = (pool_w[0].astype(BF16).reshape(PG, PG), even_w_out[0].astype(BF16),
              odd_w_in[0].astype(BF16), odd_w_out[0].astype(BF16))
    dx0, grads, sums = _local_step(
        x[0], loss_target[0], vec_rows(0), vec_rows(1), even_w_in[0].astype(BF16), shards, pool_scale, cw8,
        cb_full, sel)
    r_win0, r_pw, r_wout0, r_win1, r_wout1 = grads

    packed = jnp.concatenate([
        sums["dm0"], sums["dm1"], sums["norm_g"], sums["pool_scale"], sums["final_g"], sums["loss"],
        sums["conv_w"].reshape(6, D), sums["conv_b"].reshape(2, D), jnp.zeros((PACK_ROWS - 19, D), F32)], axis=0)
    gathered = _allgather8(packed, "gather_sums")
    tot, loss8 = _reduce_packed(gathered)
    loss = loss8[0, 0]
    g_norm_g, g_pool_scale, g_final_g = tot[6:8], tot[8:9], tot[9]
    g_ada_b = tot[0:6].reshape(2, 3 * D)
    g_conv_w = lax.dynamic_slice_in_dim(tot[11:17].reshape(3, DI), chip * (DI // NCHIP), DI // NCHIP, axis=1)
    g_conv_b = lax.dynamic_slice_in_dim(tot[17:19].reshape(1, DI), chip * (DI // NCHIP), DI // NCHIP, axis=1)
    dm_all = gathered.reshape(NDEV, PACK_ROWS, D)[:, 0:6].reshape(NDEV, 2, 3 * D)
    dm_cols = jnp.transpose(lax.dynamic_slice_in_dim(dm_all, chip * ns_ada, ns_ada, axis=2), (1, 0, 2))
    g_ada_w, d_ada_w, nm_ada_w, nv_ada_w = _ada_w_update(jnp.transpose(c_all), dm_cols, ada_w, m_ada_w, v_ada_w)

    def upd(w, g, m, v, name):
        shape = w.shape
        w2, m2, v2 = (a.reshape(g.shape) for a in (w, m, v))
        d, nm, nv = _adamw(w2, g, m2, v2, name)
        return g.reshape(shape), d.reshape(shape), nm.reshape(shape), nv.reshape(shape)

    o_win0 = upd(even_w_in, r_win0, m_even_w_in, v_even_w_in, "adamw_win0")
    o_pw = upd(pool_w, r_pw, m_pool_w, v_pool_w, "adamw_pool")
    o_wout0 = upd(even_w_out, r_wout0, m_even_w_out, v_even_w_out, "adamw_wout0")
    o_win1 = upd(odd_w_in, r_win1, m_odd_w_in, v_odd_w_in, "adamw_win1")
    o_wout1 = upd(odd_w_out, r_wout1, m_odd_w_out, v_odd_w_out, "adamw_wout1")

    def pack_small(ng, ab, ps, fg, cwv, cbv):
        conv = jnp.concatenate([cwv.reshape(3, -1), cbv.reshape(1, -1)], axis=0).reshape(2, D)
        return jnp.concatenate([ng, ab.reshape(6, D), ps, fg.reshape(1, D), conv, jnp.zeros((4, D), F32)], axis=0)

    sw = pack_small(norm_g, ada_b, pool_scale, final_g, conv_w, conv_b)
    sg = pack_small(g_norm_g, g_ada_b, g_pool_scale, g_final_g, g_conv_w, g_conv_b)
    sm = pack_small(m_norm_g, m_ada_b, m_pool_scale, m_final_g, m_conv_w, m_conv_b)
    sv = pack_small(v_norm_g, v_ada_b, v_pool_scale, v_final_g, v_conv_w, v_conv_b) + jnp.concatenate(
        [jnp.zeros((12, D), F32), jnp.ones((4, D), F32)], axis=0)
    small = _adamw(sw, sg, sm, sv, "adamw_small")

    def unpack_small(a):
        conv = a[10:12].reshape(4, -1)
        return dict(norm_g=a[0:2], ada_b=a[2:8].reshape(2, 3 * D), pool_scale=a[8:9], final_g=a[9],
                    conv_w=conv[0:3].reshape(conv_w.shape), conv_b=conv[3:4].reshape(conv_b.shape))

    s_grad = dict(norm_g=g_norm_g, ada_b=g_ada_b, pool_scale=g_pool_scale, final_g=g_final_g,
                  conv_w=g_conv_w.reshape(conv_w.shape), conv_b=g_conv_b.reshape(conv_b.shape))
    s_out = [s_grad] + [unpack_small(a) for a in small]

    outs = []
    for k in range(4):
        sm_k = s_out[k]
        outs.append([sm_k["norm_g"], (g_ada_w, d_ada_w, nm_ada_w, nv_ada_w)[k], sm_k["ada_b"], o_win0[k], o_pw[k],
                     sm_k["pool_scale"], o_wout0[k], o_win1[k], sm_k["conv_w"], sm_k["conv_b"], o_wout1[k],
                     sm_k["final_g"]])
    return (loss, dx0[None], *outs[0], *outs[1], *outs[2], *outs[3])
```

```python
import functools

import jax
import jax.numpy as jnp
from jax import lax
from jax.experimental import pallas as pl
from jax.experimental.pallas import tpu as pltpu

F32 = jnp.float32
BF16 = jnp.bfloat16
MESH = pl.DeviceIdType.MESH

D = 1024
DI = 2048
DP = 1024
NE = 6144
NO = 8192
WINDOWS = (2, 4, 8, 16)
PG = 256
HD = 64
NCHIP = 4
NDEV = 8
EPS = 1e-6
INV_SQRT_HD = 0.125

LR, B1, B2, EPS_ADAM, WD, STEP = 0.001, 0.9, 0.999, 1e-08, 0.01, 10

TM = 512
TME = 256
CT = 512
BQ = 512
BK = 256
assert BQ == 2 * BK
HALO = 16
DEAD_LOG_WEIGHT = -104.0
VMEM_LIMIT = 56 * 1024 * 1024


def _dot(a, b):
    return jnp.dot(a, b, preferred_element_type=F32)


def _dot_nt(a, b):
    return lax.dot_general(a, b, (((1,), (1,)), ((), ())), preferred_element_type=F32)


def _dot_tn(a, b):
    return lax.dot_general(a, b, (((0,), (0,)), ((), ())), preferred_element_type=F32)


def _params(*sem):
    return pltpu.CompilerParams(dimension_semantics=sem, vmem_limit_bytes=VMEM_LIMIT)


def _sigmoid(v):
    return 0.5 * jnp.tanh(0.5 * v) + 0.5


def _rowsum(v):
    return jnp.sum(v, axis=0, keepdims=True)


def _modulated_norm(xv, vec_ref):
    r = lax.rsqrt(jnp.mean(xv * xv, axis=-1, keepdims=True) + EPS)
    return (((xv * r) * vec_ref[0:1, :]) * (1.0 + vec_ref[1:2, :]) + vec_ref[2:3, :]).astype(BF16)


def _norm_mod(x, vecs, name):
    s = x.shape[0]
    tm = min(TM, s)

    def body(x_ref, vec_ref, h_ref):
        h_ref[...] = _modulated_norm(x_ref[...], vec_ref)

    return pl.pallas_call(
        body, name=name, grid=(s // tm,),
        in_specs=[pl.BlockSpec((tm, D), lambda i: (i, 0)), pl.BlockSpec((8, D), lambda i: (0, 0))],
        out_specs=pl.BlockSpec((tm, D), lambda i: (i, 0)),
        out_shape=jax.ShapeDtypeStruct((s, D), BF16),
        compiler_params=_params("parallel"),
    )(x, vecs)


def _inproj(h, w, name, comm=None):
    s = h.shape[0]
    ns = w.shape[2]
    tm = min(TM, s)
    ni = s // tm
    comm = comm or _NO_COMM
    nci, nco = len(comm.arrays), len(comm.out_shape)

    def body(*refs):
        h_ref, w_ref = refs[:2]
        proj_ref = refs[2 + nci]
        cargs = (refs[2:2 + nci], refs[3 + nci:3 + nci + nco], refs[3 + nci + nco:])
        j, i = pl.program_id(0), pl.program_id(1)
        if nci:
            pl.when((j == 0) & (i == 0))(lambda: comm.start(*cargs))
        proj_ref[...] = _dot(h_ref[...], w_ref[...]).astype(BF16)
        if nci:
            pl.when((j == NCHIP - 1) & (i == ni - 1))(lambda: comm.finish(*cargs))

    return pl.pallas_call(
        body, name=name, grid=(NCHIP, ni),
        in_specs=[pl.BlockSpec((tm, D), lambda j, i: (i, 0)),
                  pl.BlockSpec((None, D, ns), lambda j, i: (j, 0, 0))] + [ANY] * nci,
        out_specs=[pl.BlockSpec((tm, ns), lambda j, i: (i, j))] + [ANY] * nco,
        out_shape=[jax.ShapeDtypeStruct((s, NCHIP * ns), BF16)] + comm.out_shape,
        scratch_shapes=comm.sems,
        compiler_params=_params("arbitrary", "arbitrary"),
    )(h, w, *comm.arrays)


def _inproj_gathering(h, w_shard, order, name, comm):
    s = h.shape[0]
    ns = w_shard.shape[1]
    r2 = D // 2
    tm = min(TM, s)
    ni = s // tm
    nci, nco = len(comm.arrays), len(comm.out_shape)

    def body(*refs):
        order_ref, h_ref, shard_hbm = refs[:3]
        proj_ref, full_hbm = refs[3 + nci:5 + nci]
        w_ref, ici_send, ici_recv, fwd_send, fwd_recv = refs[5 + nci + nco:10 + nci + nco]
        cargs = (refs[3:3 + nci], refs[5 + nci:5 + nci + nco], refs[10 + nci + nco:])
        j, i = pl.program_id(0), pl.program_id(1)
        x, y, c = _place()
        own, sibling = 2 * x + y, (x, y, 1 - c)
        mine, other = pl.ds(c * r2, r2), pl.ds((1 - c) * r2, r2)

        def copies(d, kinds):
            px, py = _flip(x, CHIP_FLIPS[d][0]), _flip(y, CHIP_FLIPS[d][1])
            peer = 2 * px + py
            ici = dict(send_sem=ici_send.at[d], recv_sem=ici_recv.at[d], device_id=(px, py, c), device_id_type=MESH)
            fwd = dict(send_sem=fwd_send.at[d], recv_sem=fwd_recv.at[d], device_id=sibling, device_id_type=MESH)
            made = {}
            if "ici_out" in kinds:
                made["ici_out"] = pltpu.make_async_remote_copy(
                    src_ref=shard_hbm.at[mine, :], dst_ref=full_hbm.at[own, mine, :], **ici)
            if "ici_in" in kinds:
                made["ici_in"] = pltpu.make_async_remote_copy(
                    src_ref=shard_hbm.at[mine, :], dst_ref=full_hbm.at[peer, mine, :], **ici)
            if "fwd_out" in kinds:
                made["fwd_out"] = pltpu.make_async_remote_copy(
                    src_ref=full_hbm.at[peer, mine, :], dst_ref=full_hbm.at[peer, mine, :], **fwd)
            if "fwd_in" in kinds:
                made["fwd_in"] = pltpu.make_async_remote_copy(
                    src_ref=full_hbm.at[peer, other, :], dst_ref=full_hbm.at[peer, other, :], **fwd)
            return made, peer

        neighbours, diagonal = (0, 1), 2

        @pl.when((j == 0) & (i == 0))
        def _():
            if nci:
                comm.start(*cargs)
            for d in neighbours:
                copies(d, ("ici_out",))[0]["ici_out"].start()
            pltpu.sync_copy(shard_hbm, w_ref)
            pltpu.sync_copy(w_ref, full_hbm.at[own])

        for d in range(len(CHIP_FLIPS)):
            @pl.when((j == d + 1) & (i == 0))
            def _(d=d):
                made, peer = copies(d, ("ici_in", "fwd_out", "fwd_in"))
                made["ici_in"].wait_recv()
                made["fwd_out"].start()
                made["fwd_in"].wait_recv()
                pltpu.sync_copy(full_hbm.at[peer], w_ref)
                if d == 0:
                    for n in neighbours:
                        copies(n, ("ici_out",))[0]["ici_out"].wait_send()
                    copies(diagonal, ("ici_out",))[0]["ici_out"].start()

        proj_ref[...] = _dot(h_ref[...], w_ref[...]).astype(BF16)

        @pl.when((j == NCHIP - 1) & (i == ni - 1))
        def _():
            copies(diagonal, ("ici_out",))[0]["ici_out"].wait_send()
            for d in range(len(CHIP_FLIPS)):
                copies(d, ("fwd_out",))[0]["fwd_out"].wait_send()
            if nci:
                comm.finish(*cargs)

    dma = pltpu.SemaphoreType.DMA
    return pl.pallas_call(
        body, name=name,
        grid_spec=pltpu.PrefetchScalarGridSpec(
            num_scalar_prefetch=1, grid=(NCHIP, ni),
            in_specs=[pl.BlockSpec((tm, D), lambda j, i, order_ref: (i, 0)), ANY] + [ANY] * nci,
            out_specs=[pl.BlockSpec((tm, ns), lambda j, i, order_ref: (i, order_ref[j])), ANY] + [ANY] * nco,
            scratch_shapes=[pltpu.VMEM((D, ns), BF16)] + [dma((3,))] * 4 + comm.sems),
        out_shape=[jax.ShapeDtypeStruct((s, NCHIP * ns), BF16),
                   jax.ShapeDtypeStruct((NCHIP, D, ns), BF16)] + comm.out_shape,
        compiler_params=_params("arbitrary", "arbitrary"),
    )(order, h, w_shard, *comm.arrays)


def _pool_fwd(proj0, pw, pscale):
    s = proj0.shape[0]
    tm = min(TM, s)
    hb = tm // 16

    def body(u_ref, halo_ref, w_ref, sc_ref, p_ref, y_ref, ext_ref):
        i = pl.program_id(0)
        ext_ref[16:, :] = u_ref[...].astype(F32)
        ext_ref[0:16, :] = jnp.where(i > 0, halo_ref[...].astype(F32), 0.0)
        t = i * tm + lax.broadcasted_iota(jnp.int32, (tm, 1), 0)
        for g, wdw in enumerate(WINDOWS):
            cs = slice(g * PG, (g + 1) * PG)
            u = ext_ref[16:16 + tm, cs]
            acc = u
            for j in range(1, wdw):
                acc = acc + ext_ref[16 - j:16 - j + tm, cs]
            inv = 1.0 / jnp.minimum(t + 1, wdw).astype(F32)
            pb = (acc * inv - u).astype(BF16)
            p_ref[:, cs] = pb
            y_ref[:, cs] = _dot(pb, w_ref[g]) * sc_ref[:, cs]

    return pl.pallas_call(
        body, name="pool_fwd", grid=(s // tm,),
        in_specs=[pl.BlockSpec((tm, DP), lambda i: (i, 0)),
                  pl.BlockSpec((16, DP), lambda i: (jnp.maximum(i * hb - 1, 0), 0)),
                  pl.BlockSpec((4, PG, PG), lambda i: (0, 0, 0)),
                  pl.BlockSpec((1, DP), lambda i: (0, 0))],
        out_specs=[pl.BlockSpec((tm, DP), lambda i: (i, 0)),
                   pl.BlockSpec((tm, DP), lambda i: (i, 0))],
        out_shape=[jax.ShapeDtypeStruct((s, DP), BF16), jax.ShapeDtypeStruct((s, DP), F32)],
        scratch_shapes=[pltpu.VMEM((tm + 16, DP), F32)],
        compiler_params=_params("parallel"),
    )(proj0, proj0, pw, pscale)


def _sb_logits(nz, mask):
    neg_abs = lax.bitcast_convert_type(lax.bitcast_convert_type(nz, jnp.uint32) | jnp.uint32(0x80000000), F32)
    t = jnp.log(1.0 + jnp.exp(neg_abs))
    lf = jnp.minimum(nz, 0.0) - t
    lam = lf - nz
    if mask is not None:
        lf = jnp.where(mask, lf, 0.0)
    return lf, lam


def _sweep_left(steps, state, step):
    def live(carry):
        j, st = carry
        heaviest = functools.reduce(jnp.maximum, [jnp.max(head[1]) for head in st])
        return (j < steps) & (heaviest > DEAD_LOG_WEIGHT)

    return lax.while_loop(live, lambda carry: (carry[0] + 1, step(carry[0], carry[1])), (0, state))[1]


def _split_dot(v, tri):
    hi = v.astype(BF16)
    lo = (v - hi.astype(F32)).astype(BF16)
    return _dot(hi, tri) + _dot(lo, tri)


def _tri_masks():
    row = lax.broadcasted_iota(jnp.int32, (BK, BK), 0)
    col = lax.broadcasted_iota(jnp.int32, (BK, BK), 1)
    return (row > col).astype(BF16), (row >= col).astype(BF16)


def _causal_mask():
    row = lax.broadcasted_iota(jnp.int32, (BK, BK), 0)
    col = lax.broadcasted_iota(jnp.int32, (BK, BK), 1)
    return col < row


def _attn_fwd(proj0, comm=None):
    s = proj0.shape[0]
    nq = s // BQ
    kpq = BQ // BK
    nsteps = DP // 128
    comm = comm or _NO_COMM
    nci, nco = len(comm.arrays), len(comm.out_shape)

    def body(*refs):
        q_ref, k_ref, v_ref = refs[:3]
        o_ref = refs[3 + nci]
        qn_ref, k8_ref, vb_ref = refs[4 + nci + nco:7 + nci + nco]
        cargs = (refs[3:3 + nci], refs[4 + nci:4 + nci + nco], refs[7 + nci + nco:])
        if nci:
            pl.when(pl.program_id(0) == 0)(lambda: comm.start(*cargs))
        qn_ref[...] = (-q_ref[...]).astype(BF16)
        k8_ref[...] = (k_ref[...] * INV_SQRT_HD).astype(BF16)
        vb_ref[...] = v_ref[...].astype(BF16)
        after, _ = _tri_masks()
        causal = _causal_mask()
        heads = [slice(HD * h, HD * (h + 1)) for h in range(2)]
        hs = range(len(heads))
        lo, hi = (0, BK), (BK, BK)

        def qstep(qi, carry):
            q0 = pl.multiple_of(qi * BQ, BQ)
            qn = [qn_ref[pl.ds(q0, BQ), ls] for ls in heads]

            def sweep(lanes, state):
                nz = [_dot_nt(qn[h][r0:r0 + n], k8_ref[pl.ds(k0, BK), heads[h]]) for h, (r0, n), k0, _ in lanes]
                ll = [_sb_logits(nz[i], lane[3]) for i, lane in enumerate(lanes)]
                aft = [_dot(ll[i][0].astype(BF16), after) for i in range(len(lanes))]
                state = dict(state)
                for i, (h, rows, k0, mask) in enumerate(lanes):
                    o_acc, c = state[h, rows]
                    a = jnp.exp(ll[i][1] + aft[i] + c)
                    if mask is not None:
                        a = jnp.where(mask, a, 0.0)
                    o_acc = o_acc + _dot(a.astype(BF16), vb_ref[pl.ds(k0, BK), heads[h]])
                    state[h, rows] = (o_acc, c + aft[i][:, 0:1] + ll[i][0][:, 0:1])
                return state

            k_lo, k_hi = q0, pl.multiple_of(q0 + BK, BK)
            st = {(h, r): (jnp.zeros((BK, HD), F32), jnp.zeros((BK, 1), F32)) for h in hs for r in (lo, hi)}
            st = sweep([(h, hi, k_hi, causal) for h in hs] + [(h, lo, k_lo, causal) for h in hs]
                       + [(h, hi, k_lo, None) for h in hs], st)
            def left_of(rows):
                def step(j, part):
                    k0 = pl.multiple_of(q0 - (j + 1) * BK, BK)
                    got = sweep([(h, rows, k0, None) for h in hs], {(h, rows): part[h] for h in hs})
                    return tuple(got[h, rows] for h in hs)
                return step

            done = {r: _sweep_left(qi * (BQ // BK), tuple(st[h, r] for h in hs), left_of(r)) for r in (lo, hi)}
            for h, ls in enumerate(heads):
                o_ref[pl.ds(q0, BQ), ls] = jnp.concatenate([done[lo][h][0], done[hi][h][0]], axis=0)
            return carry

        lax.fori_loop(0, nq, qstep, 0)
        if nci:
            pl.when(pl.program_id(0) == nsteps - 1)(lambda: comm.finish(*cargs))

    return pl.pallas_call(
        body, name="attn_fwd", grid=(nsteps,),
        in_specs=[pl.BlockSpec((s, 128), lambda h: (0, 8 + h)),
                  pl.BlockSpec((s, 128), lambda h: (0, 16 + h)),
                  pl.BlockSpec((s, 128), lambda h: (0, 24 + h))] + [ANY] * nci,
        out_specs=[pl.BlockSpec((s, 128), lambda h: (0, h))] + [ANY] * nco,
        out_shape=[jax.ShapeDtypeStruct((s, DP), F32)] + comm.out_shape,
        scratch_shapes=[pltpu.VMEM((s, 128), BF16)] * 3 + comm.sems,
        compiler_params=_params("arbitrary"),
    )(proj0, proj0, proj0, *comm.arrays)


def _even_out(ypool, ysb, proj0, wout, x, vecs, vecs_next):
    s = x.shape[0]
    tm = min(TME, s)

    def body(yp_ref, ys_ref, gate_ref, w_ref, x_ref, vec_ref, vecn_ref, x1_ref, out_ref, yg_ref, hn_ref):
        gt = gate_ref[...].astype(F32)
        sl = gt * _sigmoid(gt)
        yg_ref[:, :DP] = (yp_ref[...] * sl[:, :DP]).astype(BF16)
        yg_ref[:, DP:] = (ys_ref[...] * sl[:, DP:]).astype(BF16)
        out = _dot(yg_ref[...], w_ref[...])
        out_ref[...] = out
        x1 = x_ref[...] + (1.0 + vec_ref[3:4, :]) * out
        x1_ref[...] = x1
        hn_ref[...] = _modulated_norm(x1, vecn_ref)

    row = lambda i: (i, 0)
    const = lambda i: (0, 0)
    return pl.pallas_call(
        body, name="even_out", grid=(s // tm,),
        in_specs=[pl.BlockSpec((tm, DP), row), pl.BlockSpec((tm, DP), row),
                  pl.BlockSpec((tm, DI), lambda i: (i, 2)),
                  pl.BlockSpec((DI, D), const),
                  pl.BlockSpec((tm, D), row), pl.BlockSpec((8, D), const), pl.BlockSpec((8, D), const)],
        out_specs=[pl.BlockSpec((tm, D), row), pl.BlockSpec((tm, D), row), pl.BlockSpec((tm, DI), row),
                   pl.BlockSpec((tm, D), row)],
        out_shape=[jax.ShapeDtypeStruct((s, D), F32), jax.ShapeDtypeStruct((s, D), F32),
                   jax.ShapeDtypeStruct((s, DI), BF16), jax.ShapeDtypeStruct((s, D), BF16)],
        compiler_params=_params("parallel"),
    )(ypool, ysb, proj0, wout, x, vecs, vecs_next)


def _odd_out(proj1, wout, x1, vecs, cw, cb, target):
    s = x1.shape[0]
    tm = min(TME, s)
    hb = tm // HALO

    def body(gb_ref, gc_ref, u_ref, gt_ref, hgc_ref, hu_ref, w_ref, x1_ref, vec_ref, cw_ref, cb_ref, tg_ref,
             dx2_ref, y1_ref, acc_ref, ext_ref):
        i = pl.program_id(0)

        @pl.when(i == 0)
        def _():
            acc_ref[...] = jnp.zeros_like(acc_ref)

        ext_ref[HALO:, :] = gc_ref[...].astype(F32) * u_ref[...].astype(F32)
        ext_ref[0:HALO, :] = jnp.where(i > 0, hgc_ref[...].astype(F32) * hu_ref[...].astype(F32), 0.0)
        for c in range(DI // CT):
            cs = slice(c * CT, (c + 1) * CT)
            conv = (cb_ref[0:1, cs] + cw_ref[0:1, cs] * ext_ref[HALO - 2:HALO - 2 + tm, cs]
                    + cw_ref[1:2, cs] * ext_ref[HALO - 1:HALO - 1 + tm, cs]
                    + cw_ref[2:3, cs] * ext_ref[HALO:HALO + tm, cs])
            gt = gt_ref[:, cs].astype(F32)
            y1_ref[:, cs] = (gb_ref[:, cs].astype(F32) * conv * (gt * _sigmoid(gt))).astype(BF16)
        out = _dot(y1_ref[...], w_ref[...])
        x2 = x1_ref[...] + (1.0 + vec_ref[3:4, :]) * out
        r = lax.rsqrt(jnp.mean(x2 * x2, axis=-1, keepdims=True) + EPS)
        nrm = x2 * r
        fg = vec_ref[4:5, :]
        err = nrm * fg - tg_ref[...]
        acc_ref[1:2, :] += _rowsum(err * err) * (0.5 / D)
        dyf = err * (1.0 / D)
        acc_ref[0:1, :] += _rowsum(dyf * nrm)
        dn = dyf * fg
        dx2 = r * (dn - nrm * jnp.mean(dn * nrm, axis=-1, keepdims=True))
        dx2_ref[...] = dx2
        acc_ref[2:3, :] += _rowsum(dx2 * out)

    row = lambda i: (i, 0)
    halo = lambda col: (lambda i: (jnp.maximum(i * hb - 1, 0), col))
    const = lambda i: (0, 0)
    return pl.pallas_call(
        body, name="odd_out", grid=(s // tm,),
        in_specs=[pl.BlockSpec((tm, DI), lambda i: (i, 0)), pl.BlockSpec((tm, DI), lambda i: (i, 1)),
                  pl.BlockSpec((tm, DI), lambda i: (i, 2)), pl.BlockSpec((tm, DI), lambda i: (i, 3)),
                  pl.BlockSpec((HALO, DI), halo(1)), pl.BlockSpec((HALO, DI), halo(2)),
                  pl.BlockSpec((DI, D), const), pl.BlockSpec((tm, D), row), pl.BlockSpec((8, D), const),
                  pl.BlockSpec((8, DI), const), pl.BlockSpec((1, DI), const), pl.BlockSpec((tm, D), row)],
        out_specs=[pl.BlockSpec((tm, D), row), pl.BlockSpec((tm, DI), row), pl.BlockSpec((8, D), const)],
        out_shape=[jax.ShapeDtypeStruct((s, D), F32), jax.ShapeDtypeStruct((s, DI), BF16),
                   jax.ShapeDtypeStruct((8, D), F32)],
        scratch_shapes=[pltpu.VMEM((tm + HALO, DI), F32)],
        compiler_params=_params("arbitrary"),
    )(proj1, proj1, proj1, proj1, proj1, proj1, wout, x1, vecs, cw, cb, target)


def _odd_bwd(dx2, proj1, wout, vecs, cw, cb):
    s = dx2.shape[0]
    tm = min(TME, s)
    nb = s // tm
    hb = tm // HALO

    def body(dx2_ref, gb_ref, gc_ref, u_ref, gt_ref, hgc_ref, hu_ref, w_ref, vec_ref, cw_ref, cb_ref,
             dout_ref, dproj_ref, accv_ref, uext_ref, dext_ref, dy_ref):
        i = pl.program_id(0)
        blk = nb - 1 - i

        @pl.when(i == 0)
        def _():
            accv_ref[...] = jnp.zeros_like(accv_ref)
            dext_ref[tm:tm + 8, :] = jnp.zeros((8, DI), F32)

        dout = (dx2_ref[...] * (1.0 + vec_ref[3:4, :])).astype(BF16)
        dout_ref[...] = dout
        dy_ref[...] = _dot_nt(dout, w_ref[...])
        uext_ref[HALO:, :] = gc_ref[...].astype(F32) * u_ref[...].astype(F32)
        uext_ref[0:HALO, :] = jnp.where(blk > 0, hgc_ref[...].astype(F32) * hu_ref[...].astype(F32), 0.0)
        for c in range(DI // CT):
            cs = slice(c * CT, (c + 1) * CT)
            u0 = uext_ref[HALO - 2:HALO - 2 + tm, cs]
            u1 = uext_ref[HALO - 1:HALO - 1 + tm, cs]
            u2 = uext_ref[HALO:HALO + tm, cs]
            w0, w1, w2 = cw_ref[0:1, cs], cw_ref[1:2, cs], cw_ref[2:3, cs]
            conv = cb_ref[0:1, cs] + w0 * u0 + w1 * u1 + w2 * u2
            gt = gt_ref[:, cs].astype(F32)
            sg = _sigmoid(gt)
            gb = gb_ref[:, cs].astype(F32)
            dy = dy_ref[:, cs]
            t1 = dy * (gt * sg)
            dproj_ref[:, cs] = (t1 * conv).astype(BF16)
            dconv = t1 * gb
            dproj_ref[:, 3 * DI + c * CT:3 * DI + (c + 1) * CT] = (
                dy * gb * conv * (sg * (1.0 + gt * (1.0 - sg)))).astype(BF16)
            accv_ref[0:1, cs] += _rowsum(dconv * u0)
            accv_ref[1:2, cs] += _rowsum(dconv * u1)
            accv_ref[2:3, cs] += _rowsum(dconv * u2)
            accv_ref[3:4, cs] += _rowsum(dconv)
            dext_ref[0:tm, cs] = dconv
            duu = w2 * dconv + w1 * dext_ref[1:tm + 1, cs] + w0 * dext_ref[2:tm + 2, cs]
            dproj_ref[:, DI + c * CT:DI + (c + 1) * CT] = (duu * u_ref[:, cs].astype(F32)).astype(BF16)
            dproj_ref[:, 2 * DI + c * CT:2 * DI + (c + 1) * CT] = (duu * gc_ref[:, cs].astype(F32)).astype(BF16)
        dext_ref[tm:tm + 8, :] = dext_ref[0:8, :]

    rrow = lambda i: (nb - 1 - i, 0)
    rcol = lambda col: (lambda i: (nb - 1 - i, col))
    halo = lambda col: (lambda i: (jnp.maximum((nb - 1 - i) * hb - 1, 0), col))
    const = lambda i: (0, 0)
    return pl.pallas_call(
        body, name="odd_bwd", grid=(nb,),
        in_specs=[pl.BlockSpec((tm, D), rrow),
                  pl.BlockSpec((tm, DI), rcol(0)), pl.BlockSpec((tm, DI), rcol(1)),
                  pl.BlockSpec((tm, DI), rcol(2)), pl.BlockSpec((tm, DI), rcol(3)),
                  pl.BlockSpec((HALO, DI), halo(1)), pl.BlockSpec((HALO, DI), halo(2)),
                  pl.BlockSpec((DI, D), const), pl.BlockSpec((8, D), const),
                  pl.BlockSpec((8, DI), const), pl.BlockSpec((1, DI), const)],
        out_specs=[pl.BlockSpec((tm, D), rrow), pl.BlockSpec((tm, NO), rrow), pl.BlockSpec((8, DI), const)],
        out_shape=[jax.ShapeDtypeStruct((s, D), BF16), jax.ShapeDtypeStruct((s, NO), BF16),
                   jax.ShapeDtypeStruct((8, DI), F32)],
        scratch_shapes=[pltpu.VMEM((tm + HALO, DI), F32), pltpu.VMEM((tm + 8, DI), F32), pltpu.VMEM((tm, DI), F32)],
        compiler_params=_params("arbitrary"),
    )(dx2, proj1, proj1, proj1, proj1, proj1, proj1, wout, vecs, cw, cb)


def _grad_w_body(nk):
    def body(a_ref, b_ref, o_ref, ob_ref):
        k = pl.program_id(1)

        @pl.when(k == 0)
        def _():
            o_ref[...] = jnp.zeros_like(o_ref)

        o_ref[...] += _dot_tn(a_ref[...], b_ref[...])

        @pl.when(k == nk - 1)
        def _():
            ob_ref[...] = o_ref[...].astype(BF16)

    return body


def _grad_w_cols(a, b, name):
    s, m = a.shape
    ns = b.shape[1] // NCHIP
    ts = min(TM, s)
    blk = pl.BlockSpec((None, m, ns), lambda j, k: (j, 0, 0))
    return pl.pallas_call(
        _grad_w_body(s // ts), name=name, grid=(NCHIP, s // ts),
        in_specs=[pl.BlockSpec((ts, m), lambda j, k: (k, 0)),
                  pl.BlockSpec((ts, ns), lambda j, k: (k, j))],
        out_specs=[blk, blk],
        out_shape=[jax.ShapeDtypeStruct((NCHIP, m, ns), F32), jax.ShapeDtypeStruct((NCHIP, m, ns), BF16)],
        compiler_params=_params("parallel", "arbitrary"),
    )(a, b)


def _grad_w_rows(a, b, name):
    s = a.shape[0]
    ms = a.shape[1] // NCHIP
    n = b.shape[1]
    ts = min(TM, s)
    blk = pl.BlockSpec((None, ms, n), lambda i, k: (i, 0, 0))
    return pl.pallas_call(
        _grad_w_body(s // ts), name=name, grid=(NCHIP, s // ts),
        in_specs=[pl.BlockSpec((ts, ms), lambda i, k: (k, i)),
                  pl.BlockSpec((ts, n), lambda i, k: (k, 0))],
        out_specs=[blk, blk],
        out_shape=[jax.ShapeDtypeStruct((NCHIP, ms, n), F32), jax.ShapeDtypeStruct((NCHIP, ms, n), BF16)],
        compiler_params=_params("parallel", "arbitrary"),
    )(a, b)


def _inproj_bwd(dproj, w, x, dx_in, vecs, name, comm=None):
    s = x.shape[0]
    ns = w.shape[2]
    tm = min(TME, s)
    ni = s // tm
    comm = comm or _NO_COMM
    nci, nco = len(comm.arrays), len(comm.out_shape)

    def body(*refs):
        dp_ref, w_hbm, x_ref, dxin_ref, vec_ref = refs[:5]
        dx_ref, acc_ref = refs[5 + nci:7 + nci]
        w_ref = refs[7 + nci + nco]
        cargs = (refs[5:5 + nci], refs[7 + nci:7 + nci + nco], refs[8 + nci + nco:])
        i = pl.program_id(0)

        @pl.when(i == 0)
        def _():
            acc_ref[...] = jnp.zeros_like(acc_ref)
            if nci:
                comm.start(*cargs)
            pltpu.sync_copy(w_hbm, w_ref)

        dh = _dot_nt(dp_ref[:, 0:ns], w_ref[0])
        for j in range(1, NCHIP):
            dh = dh + _dot_nt(dp_ref[:, j * ns:(j + 1) * ns], w_ref[j])
        xv = x_ref[...]
        r = lax.rsqrt(jnp.mean(xv * xv, axis=-1, keepdims=True) + EPS)
        nrm = xv * r
        g = vec_ref[0:1, :]
        sc1 = 1.0 + vec_ref[1:2, :]
        dhn = dh * nrm
        acc_ref[0:1, :] += _rowsum(dh)
        acc_ref[1:2, :] += _rowsum(dhn) * g
        acc_ref[2:3, :] += _rowsum(dhn) * sc1
        dn = dh * (g * sc1)
        dx_ref[...] = dxin_ref[...] + r * (dn - nrm * jnp.mean(dn * nrm, axis=-1, keepdims=True))

        if nci:
            pl.when(i == ni - 1)(lambda: comm.finish(*cargs))

    row = lambda i: (i, 0)
    const = lambda i: (0, 0)
    return pl.pallas_call(
        body, name=name, grid=(ni,),
        in_specs=[pl.BlockSpec((tm, NCHIP * ns), row), ANY,
                  pl.BlockSpec((tm, D), row), pl.BlockSpec((tm, D), row), pl.BlockSpec((8, D), const)] + [ANY] * nci,
        out_specs=[pl.BlockSpec((tm, D), row), pl.BlockSpec((8, D), const)] + [ANY] * nco,
        out_shape=[jax.ShapeDtypeStruct((s, D), F32), jax.ShapeDtypeStruct((8, D), F32)] + comm.out_shape,
        scratch_shapes=[pltpu.VMEM(w.shape, BF16)] + comm.sems,
        compiler_params=_params("arbitrary"),
    )(dproj, w, x, dx_in, vecs, *comm.arrays)


def _even_bwd(dx1, out0, ypool, ysb, proj0, wout, vecs):
    s = dx1.shape[0]
    tm = min(TME, s)

    def body(dx1_ref, out0_ref, yp_ref, ys_ref, gate_ref, w_ref, vec_ref,
             dout_ref, dyp_ref, dys_ref, dgt_ref, acc_ref):
        @pl.when(pl.program_id(0) == 0)
        def _():
            acc_ref[...] = jnp.zeros_like(acc_ref)

        dx1v = dx1_ref[...]
        acc_ref[0:1, :] += _rowsum(dx1v * out0_ref[...])
        dout = (dx1v * (1.0 + vec_ref[3:4, :])).astype(BF16)
        dout_ref[...] = dout
        dyg = _dot_nt(dout, w_ref[...])
        gt = gate_ref[...].astype(F32)
        sg = _sigmoid(gt)
        sl = gt * sg
        dsl = sg * (1.0 + gt * (1.0 - sg))
        dyp_ref[...] = dyg[:, :DP] * sl[:, :DP]
        dys_ref[...] = dyg[:, DP:] * sl[:, DP:]
        dgt_ref[:, :DP] = (dyg[:, :DP] * yp_ref[...] * dsl[:, :DP]).astype(BF16)
        dgt_ref[:, DP:] = (dyg[:, DP:] * ys_ref[...] * dsl[:, DP:]).astype(BF16)

    row = lambda i: (i, 0)
    const = lambda i: (0, 0)
    return pl.pallas_call(
        body, name="even_bwd", grid=(s // tm,),
        in_specs=[pl.BlockSpec((tm, D), row), pl.BlockSpec((tm, D), row),
                  pl.BlockSpec((tm, DP), row), pl.BlockSpec((tm, DP), row),
                  pl.BlockSpec((tm, DI), lambda i: (i, 2)),
                  pl.BlockSpec((DI, D), const), pl.BlockSpec((8, D), const)],
        out_specs=[pl.BlockSpec((tm, D), row), pl.BlockSpec((tm, DP), row), pl.BlockSpec((tm, DP), row),
                   pl.BlockSpec((tm, DI), lambda i: (i, NE // DI - 1)), pl.BlockSpec((8, D), const)],
        out_shape=[jax.ShapeDtypeStruct((s, D), BF16), jax.ShapeDtypeStruct((s, DP), F32),
                   jax.ShapeDtypeStruct((s, DP), F32), jax.ShapeDtypeStruct((s, NE), BF16),
                   jax.ShapeDtypeStruct((8, D), F32)],
        compiler_params=_params("arbitrary"),
    )(dx1, out0, ypool, ysb, proj0, wout, vecs)


def _pool_bwd(dyp, p, pw, pscale, dproj):
    s = dyp.shape[0]
    tm = min(TM, s)
    nb = s // tm
    hb = tm // 16

    def body(dy_ref, dyh_ref, p_ref, w_ref, sc_ref, dproj_hbm, du_ref, dw_ref, acc_ref, ext_ref):
        i = pl.program_id(0)

        @pl.when(i == 0)
        def _():
            dw_ref[...] = jnp.zeros_like(dw_ref)
            acc_ref[...] = jnp.zeros_like(acc_ref)

        t = i * tm + lax.broadcasted_iota(jnp.int32, (tm + 16, 1), 0)
        for g, wdw in enumerate(WINDOWS):
            cs = slice(g * PG, (g + 1) * PG)
            sc = sc_ref[:, cs]
            dy = dy_ref[:, cs]
            dyh = jnp.where(i < nb - 1, dyh_ref[:, cs], 0.0)
            pb = p_ref[:, cs]
            wg = w_ref[g]
            acc_ref[0:1, cs] += _rowsum(dy * _dot(pb, wg))
            dypre = (dy * sc).astype(BF16)
            dw_ref[g] += _dot_tn(pb, dypre)
            dp = _dot_nt(dypre, wg)
            dph = _dot_nt((dyh * sc).astype(BF16), wg)
            inv = 1.0 / jnp.minimum(t + 1, wdw).astype(F32)
            ext_ref[0:tm, cs] = dp * inv[0:tm]
            ext_ref[tm:tm + 16, cs] = dph * inv[tm:tm + 16]
            acc = ext_ref[0:tm, cs]
            for j in range(1, wdw):
                acc = acc + ext_ref[j:j + tm, cs]
            du_ref[:, cs] = (acc - dp).astype(BF16)

    row = lambda i: (i, 0)
    return pl.pallas_call(
        body, name="pool_bwd", grid=(nb,),
        in_specs=[pl.BlockSpec((tm, DP), row),
                  pl.BlockSpec((16, DP), lambda i: (jnp.minimum((i + 1) * hb, s // 16 - 1), 0)),
                  pl.BlockSpec((tm, DP), row),
                  pl.BlockSpec((4, PG, PG), lambda i: (0, 0, 0)),
                  pl.BlockSpec((1, DP), lambda i: (0, 0)), ANY],
        out_specs=[pl.BlockSpec((tm, DP), row), pl.BlockSpec((4, PG, PG), lambda i: (0, 0, 0)),
                   pl.BlockSpec((8, DP), lambda i: (0, 0))],
        out_shape=[jax.ShapeDtypeStruct((s, NE), BF16), jax.ShapeDtypeStruct((4, PG, PG), F32),
                   jax.ShapeDtypeStruct((8, DP), F32)],
        input_output_aliases={5: 0},
        scratch_shapes=[pltpu.VMEM((tm + 16, DP), F32)],
        compiler_params=_params("arbitrary"),
    )(dyp, dyp, p, pw, pscale, dproj)


def _attn_bwd(proj0, ysb, dys, dproj, comm=None):
    s = proj0.shape[0]
    nq = s // BQ
    kpq = BQ // BK
    nsteps = DP // 128
    comm = comm or _NO_COMM
    nci, nco = len(comm.arrays), len(comm.out_shape)

    def body(*refs):
        q_ref, k_ref, v_ref, o_ref, do_ref = refs[:5]
        dproj_hbm = refs[6 + nci]
        scratch = refs[7 + nci + nco:]
        qn_ref, k8_ref, vb_ref, dob_ref, dka_ref, dva_ref, dq_ref, dk_ref, dv_ref, part_sems = scratch[:10]
        cargs = (refs[6:6 + nci], refs[7 + nci:7 + nci + nco], scratch[10:])
        step = pl.program_id(0)
        if nci:
            pl.when(step == 0)(lambda: comm.start(*cargs))

        qn_ref[...] = (-q_ref[...]).astype(BF16)
        k8_ref[...] = (k_ref[...] * INV_SQRT_HD).astype(BF16)
        vb_ref[...] = v_ref[...].astype(BF16)
        dob_ref[...] = do_ref[...].astype(BF16)
        dka_ref[...] = jnp.zeros_like(dka_ref)
        dva_ref[...] = jnp.zeros_like(dva_ref)
        after, from_on = _tri_masks()
        causal = _causal_mask()
        heads = [slice(HD * h, HD * (h + 1)) for h in range(2)]
        hs = range(len(heads))
        lo, hi = (0, BK), (BK, BK)

        def qstep(qi, carry):
            q0 = pl.multiple_of(qi * BQ, BQ)
            qn = [qn_ref[pl.ds(q0, BQ), ls] for ls in heads]
            do = [dob_ref[pl.ds(q0, BQ), ls] for ls in heads]
            total = [jnp.sum(do[h].astype(F32) * o_ref[pl.ds(q0, BQ), ls], axis=1, keepdims=True)
                     for h, ls in enumerate(heads)]

            def sweep(lanes, state):
                rows_of = lambda i: slice(lanes[i][1][0], lanes[i][1][0] + lanes[i][1][1])
                k8 = [k8_ref[pl.ds(k0, BK), heads[h]] for h, _, k0, _ in lanes]
                nz = [_dot_nt(qn[lane[0]][rows_of(i)], k8[i]) for i, lane in enumerate(lanes)]
                da = [_dot_nt(do[h][rows_of(i)], vb_ref[pl.ds(k0, BK), heads[h]]) for i, (h, _, k0, _) in enumerate(lanes)]
                ll = [_sb_logits(nz[i], lane[3]) for i, lane in enumerate(lanes)]
                aft = [_dot(ll[i][0].astype(BF16), after) for i in range(len(lanes))]
                state = dict(state)
                for i, (h, rows, k0, mask) in enumerate(lanes):
                    dq_acc, c, cg = state[h, rows]
                    a = jnp.exp(ll[i][1] + aft[i] + c)
                    if mask is not None:
                        a = jnp.where(mask, a, 0.0)
                    ab = a.astype(BF16)
                    g = da[i] * ab.astype(F32)
                    suf = _split_dot(g, from_on)
                    dz = g - jnp.exp(ll[i][1]) * (g + ((total[h][rows_of(i)] - cg) - suf))
                    if mask is not None:
                        dz = jnp.where(mask, dz, 0.0)
                    dzb = dz.astype(BF16)
                    dka_ref[pl.ds(k0, BK), heads[h]] += _dot_tn(dzb, qn[h][rows_of(i)])
                    dva_ref[pl.ds(k0, BK), heads[h]] += _dot_tn(ab, do[h][rows_of(i)])
                    state[h, rows] = (dq_acc + _dot(dzb, k8[i]), c + aft[i][:, 0:1] + ll[i][0][:, 0:1],
                                      cg + suf[:, 0:1])
                return state

            k_lo, k_hi = q0, pl.multiple_of(q0 + BK, BK)
            zero = jnp.zeros((BK, 1), F32)
            st = {(h, r): (jnp.zeros((BK, HD), F32), zero, zero) for h in hs for r in (lo, hi)}
            st = sweep([(h, hi, k_hi, causal) for h in hs] + [(h, lo, k_lo, causal) for h in hs]
                       + [(h, hi, k_lo, None) for h in hs], st)
            def left_of(rows):
                def step(j, part):
                    k0 = pl.multiple_of(q0 - (j + 1) * BK, BK)
                    got = sweep([(h, rows, k0, None) for h in hs], {(h, rows): part[h] for h in hs})
                    return tuple(got[h, rows] for h in hs)
                return step

            done = {r: _sweep_left(qi * (BQ // BK), tuple(st[h, r] for h in hs), left_of(r)) for r in (lo, hi)}
            for h, ls in enumerate(heads):
                dq_ref[pl.ds(q0, BQ), ls] = jnp.concatenate([done[lo][h][0], done[hi][h][0]], axis=0).astype(BF16)
            return carry

        lax.fori_loop(0, nq, qstep, 0)
        dk_ref[...] = (dka_ref[...] * (-INV_SQRT_HD)).astype(BF16)
        dv_ref[...] = dva_ref[...].astype(BF16)
        lanes0 = pl.multiple_of(step * 128, 128)
        parts = [pltpu.make_async_copy(src, dproj_hbm.at[:, pl.ds((1 + k) * DP + lanes0, 128)], part_sems.at[k])
                 for k, src in enumerate((dq_ref, dk_ref, dv_ref))]
        for cp in parts:
            cp.start()
        for cp in parts:
            cp.wait()
        if nci:
            pl.when(step == nsteps - 1)(lambda: comm.finish(*cargs))

    col = lambda h: (0, h)
    return pl.pallas_call(
        body, name="attn_bwd", grid=(nsteps,),
        in_specs=[pl.BlockSpec((s, 128), lambda h: (0, 8 + h)),
                  pl.BlockSpec((s, 128), lambda h: (0, 16 + h)),
                  pl.BlockSpec((s, 128), lambda h: (0, 24 + h)),
                  pl.BlockSpec((s, 128), col), pl.BlockSpec((s, 128), col), ANY] + [ANY] * nci,
        out_specs=[ANY] * (1 + nco),
        out_shape=[jax.ShapeDtypeStruct((s, NE), BF16)] + comm.out_shape,
        input_output_aliases={5: 0},
        scratch_shapes=([pltpu.VMEM((s, 128), BF16)] * 4 + [pltpu.VMEM((s, 128), F32)] * 2
                        + [pltpu.VMEM((s, 128), BF16)] * 3 + [pltpu.SemaphoreType.DMA((3,))] + comm.sems),
        compiler_params=_params("arbitrary"),
    )(proj0, proj0, proj0, ysb, dys, dproj, *comm.arrays)


def _adamw_math(w, g, m, v):
    m2 = B1 * m + (1.0 - B1) * g
    v2 = B2 * v + (1.0 - B2) * (g * g)
    m_hat = m2 / (1.0 - B1 ** STEP)
    v_hat = v2 / (1.0 - B2 ** STEP)
    return -LR * (m_hat / (jnp.sqrt(v_hat) + EPS_ADAM) + WD * w), m2, v2


def _adamw(w, g, m, v, name):
    r, c = w.shape
    tr = r
    while tr * c * 4 > (1 << 20) and tr % 16 == 0:
        tr //= 2

    def body(w_ref, g_ref, m_ref, v_ref, d_ref, m2_ref, v2_ref):
        d_ref[...], m2_ref[...], v2_ref[...] = _adamw_math(w_ref[...], g_ref[...], m_ref[...], v_ref[...])

    spec = pl.BlockSpec((tr, c), lambda i: (i, 0))
    return pl.pallas_call(
        body, name=name, grid=(r // tr,),
        in_specs=[spec] * 4, out_specs=[spec] * 3,
        out_shape=[jax.ShapeDtypeStruct((r, c), F32)] * 3,
        compiler_params=_params("parallel"),
    )(w, g, m, v)


def _local_step(x, target, vecs0, vecs1, win0, rest, pscale, cw8, cb, sel=None):
    dist = sel is not None
    h0 = _norm_mod(x, vecs0, "norm0")
    if dist:
        chip = sel[0]
        order = jnp.stack([chip, chip ^ 2, chip ^ 1, chip ^ 3])
        proj0, win0 = _inproj_gathering(h0, win0, order, "inproj0", _NO_COMM)
    else:
        proj0, = _inproj(h0, win0, "inproj0")
    ysb, *got = _attn_fwd(proj0, _gather_comm(list(rest)) if dist else None)
    pw, wout0, win1, wout1 = ((_group_major(got[0]), got[1].reshape(DI, D), got[2], got[3].reshape(DI, D))
                              if dist else rest)
    p, ypool = _pool_fwd(proj0, pw, pscale)
    x1, out0, yg, h1 = _even_out(ypool, ysb, proj0, wout0, x, vecs0, vecs1)
    proj1, = _inproj(h1, win1, "inproj1")
    dx2, y1, acc_f = _odd_out(proj1, wout1, x1, vecs1, cw8, cb, target)

    def chip_partials(grads, names):
        from_sibling = _send_halves([g16 for _, g16 in grads], "rs_send_halves_" + names[0])
        part = [_add_halves(g32, t, sel[1:2], "rs_add_halves_" + nm)
                for (g32, _), t, nm in zip(grads, from_sibling, names)]
        return [p32 for p32, _ in part], _exchange_comm([p16 for _, p16 in part])

    dout1, dproj1, acc_cv = _odd_bwd(dx2, proj1, wout1, vecs1, cw8, cb)
    g_wout1 = _grad_w_rows(y1, dout1, "grad_wout1")
    g_win1 = _grad_w_cols(h1, dproj1, "grad_win1")
    dx1, acc_n1 = _inproj_bwd(dproj1, win1, x1, dx2, vecs1, "inproj1_bwd")

    dout0, dyp, dys, dproj0, acc_g0 = _even_bwd(dx1, out0, ypool, ysb, proj0, wout0, vecs0)
    g_wout0 = _grad_w_rows(yg, dout0, "grad_wout0")
    dproj0, g_pw, acc_ps = _pool_bwd(dyp, p, pw, pscale, dproj0)
    g_pool = _chip_major(g_pw, PG // NCHIP)
    early = [(g_pool, g_pool.astype(BF16)), g_wout0, g_win1, g_wout1]
    part_a, swap_a = chip_partials(early, ["pool", "wout0", "win1", "wout1"]) if dist else (None, None)
    dproj0, *got_a = _attn_bwd(proj0, ysb, dys, dproj0, swap_a)
    g_win0 = _grad_w_cols(h0, dproj0, "grad_win0")
    part_b, swap_b = chip_partials([g_win0], ["win0"]) if dist else (None, None)
    dx0, acc_n0, *got_b = _inproj_bwd(dproj0, win0, x, dx1, vecs0, "inproj0_bwd", swap_b)

    if dist:
        names = ["win0", "pool", "wout0", "win1", "wout1"]
        halves = [_add_partials(p32, t, sel, "rs_add_partials_" + nm)
                  for p32, t, nm in zip(part_b + part_a, got_b + got_a, names)]
        grads = tuple(_join_halves(halves))
    else:
        grads = (g_win0[0], g_pw, g_wout0[0], g_win1[0], g_wout1[0])

    sums = dict(
        dm0=jnp.concatenate([acc_n0[0:2], acc_g0[0:1]], axis=0),
        dm1=jnp.concatenate([acc_n1[0:2], acc_f[2:3]], axis=0),
        norm_g=jnp.concatenate([acc_n0[2:3], acc_n1[2:3]], axis=0),
        pool_scale=acc_ps[0:1], final_g=acc_f[0:1], loss=acc_f[1:2],
        conv_w=acc_cv[0:3], conv_b=acc_cv[3:4])
    return dx0, grads, sums


ANY = pl.BlockSpec(memory_space=pl.ANY)
CHIP_FLIPS = ((1, 0), (0, 1), (1, 1))


def _place():
    return lax.axis_index("x"), lax.axis_index("y"), lax.axis_index("c")


def _flip(v, f):
    return 1 - v if f else v


def _allgather8(v, name):
    m_per, n = v.shape

    def body(x_ref, out_ref, send_sems, recv_sems, local_sem):
        x, y, c = _place()
        me, sibling = (x, y, c), (x, y, 1 - c)
        chips = [(_flip(x, fx), _flip(y, fy)) for fx, fy in CHIP_FLIPS]

        def rows(px, py, pc):
            return out_ref.at[pl.ds((4 * px + 2 * py + pc) * m_per, m_per), :]

        def copy(k, block, to, src=None):
            return pltpu.make_async_remote_copy(
                src_ref=rows(*block) if src is None else src, dst_ref=rows(*block),
                send_sem=send_sems.at[k], recv_sem=recv_sems.at[k], device_id=to, device_id_type=MESH)

        mine = pltpu.make_async_copy(x_ref, rows(*me), local_sem)
        mine.start()
        first = [copy(0, me, sibling, src=x_ref)]
        first += [copy(1 + j, me, (*chip, c), src=x_ref) for j, chip in enumerate(chips)]
        for cp in first:
            cp.start()
        passed = [copy(4 + j, (*chip, c), sibling) for j, chip in enumerate(chips)]
        for j, chip in enumerate(chips):
            copy(1 + j, (*chip, c), me).wait_recv()
            passed[j].start()
        copy(0, sibling, me).wait_recv()
        for j, chip in enumerate(chips):
            copy(4 + j, (*chip, 1 - c), me).wait_recv()
        for cp in first + passed:
            cp.wait_send()
        mine.wait()

    return pl.pallas_call(
        body, name=name,
        out_shape=jax.ShapeDtypeStruct((NDEV * m_per, n), v.dtype),
        in_specs=[pl.BlockSpec(memory_space=pltpu.VMEM)],
        out_specs=pl.BlockSpec(memory_space=pltpu.VMEM),
        scratch_shapes=[pltpu.SemaphoreType.DMA((7,)), pltpu.SemaphoreType.DMA((7,)), pltpu.SemaphoreType.DMA],
    )(v)


class _Comm:
    def __init__(self, arrays, out_shape, sems, start, finish):
        self.arrays, self.out_shape, self.sems, self.start, self.finish = arrays, out_shape, sems, start, finish


_NO_COMM = _Comm([], [], [], None, None)


def _gather_comm(shards):
    n = len(shards)

    def pieces(ins, outs, sems, kinds):
        x, y, c = _place()
        ici_send, ici_recv, fwd_send, fwd_recv = sems[:4]
        own, sibling = 2 * x + y, (x, y, 1 - c)
        made = {kind: [] for kind in kinds}
        for w in range(n):
            r2 = ins[w].shape[0] // 2
            mine, other = pl.ds(c * r2, r2), pl.ds((1 - c) * r2, r2)
            for d, (fx, fy) in enumerate(CHIP_FLIPS):
                px, py, k = _flip(x, fx), _flip(y, fy), 3 * w + d
                peer = 2 * px + py
                ici = dict(send_sem=ici_send.at[k], recv_sem=ici_recv.at[k], device_id=(px, py, c), device_id_type=MESH)
                fwd = dict(send_sem=fwd_send.at[k], recv_sem=fwd_recv.at[k], device_id=sibling, device_id_type=MESH)
                if "ici_out" in kinds:
                    made["ici_out"].append(pltpu.make_async_remote_copy(
                        src_ref=ins[w].at[mine, :], dst_ref=outs[w].at[own, mine, :], **ici))
                if "ici_in" in kinds:
                    made["ici_in"].append(pltpu.make_async_remote_copy(
                        src_ref=ins[w].at[mine, :], dst_ref=outs[w].at[peer, mine, :], **ici))
                if "fwd_out" in kinds:
                    made["fwd_out"].append(pltpu.make_async_remote_copy(
                        src_ref=outs[w].at[peer, mine, :], dst_ref=outs[w].at[peer, mine, :], **fwd))
                if "fwd_in" in kinds:
                    made["fwd_in"].append(pltpu.make_async_remote_copy(
                        src_ref=outs[w].at[peer, other, :], dst_ref=outs[w].at[peer, other, :], **fwd))
        return made

    def stage_rows(a):
        return min(a.shape[0], 256)

    def start(ins, outs, sems):
        for cp in pieces(ins, outs, sems, ("ici_out",))["ici_out"]:
            cp.start()
        x, y, _ = _place()
        for w, stage in enumerate(sems[4:]):
            rows = stage.shape[0]
            for r in range(0, ins[w].shape[0], rows):
                pltpu.sync_copy(ins[w].at[pl.ds(r, rows), :], stage)
                pltpu.sync_copy(stage, outs[w].at[2 * x + y, pl.ds(r, rows), :])

    def finish(ins, outs, sems):
        made = pieces(ins, outs, sems, ("ici_out", "ici_in", "fwd_out", "fwd_in"))
        for arrived, onward in zip(made["ici_in"], made["fwd_out"]):
            arrived.wait_recv()
            onward.start()
        for cp in made["fwd_in"]:
            cp.wait_recv()
        for cp in made["ici_out"] + made["fwd_out"]:
            cp.wait_send()

    dma = pltpu.SemaphoreType.DMA
    return _Comm(list(shards), [jax.ShapeDtypeStruct((NCHIP,) + a.shape, a.dtype) for a in shards],
                 [dma((3 * n,))] * 4 + [pltpu.VMEM((stage_rows(a), a.shape[1]), a.dtype) for a in shards],
                 start, finish)


def _exchange_comm(parts):
    n = len(parts)

    def copies(ins, outs, sems):
        x, y, c = _place()
        send_sems, recv_sems = sems
        out = []
        for w in range(n):
            for d, (fx, fy) in enumerate(CHIP_FLIPS):
                px, py = _flip(x, fx), _flip(y, fy)
                out.append(pltpu.make_async_remote_copy(
                    src_ref=ins[w].at[2 * px + py], dst_ref=outs[w].at[d], send_sem=send_sems.at[3 * w + d],
                    recv_sem=recv_sems.at[3 * w + d], device_id=(px, py, c), device_id_type=MESH))
        return out

    def start(ins, outs, sems):
        for cp in copies(ins, outs, sems):
            cp.start()

    def finish(ins, outs, sems):
        cps = copies(ins, outs, sems)
        for cp in cps:
            cp.wait_recv()
        for cp in cps:
            cp.wait_send()

    dma = pltpu.SemaphoreType.DMA
    return _Comm(list(parts), [jax.ShapeDtypeStruct((3,) + p.shape[1:], BF16) for p in parts],
                 [dma((3 * n,))] * 2, start, finish)


def _send_halves(grads, name):
    n = len(grads)

    def body(*refs):
        ins, outs = refs[:n], refs[n:2 * n]
        send_sems, recv_sems = refs[2 * n:]
        x, y, c = _place()
        copies = []
        for w in range(n):
            r2 = ins[w].shape[1] // 2
            cp = pltpu.make_async_remote_copy(
                src_ref=ins[w].at[:, pl.ds((1 - c) * r2, r2), :], dst_ref=outs[w],
                send_sem=send_sems.at[w], recv_sem=recv_sems.at[w], device_id=(x, y, 1 - c), device_id_type=MESH)
            cp.start()
            copies.append(cp)
        for cp in copies:
            cp.wait_recv()
        for cp in copies:
            cp.wait_send()

    return pl.pallas_call(
        body, name=name,
        out_shape=[jax.ShapeDtypeStruct((NCHIP, g.shape[1] // 2, g.shape[2]), g.dtype) for g in grads],
        in_specs=[ANY] * n, out_specs=[ANY] * n,
        scratch_shapes=[pltpu.SemaphoreType.DMA((n,)), pltpu.SemaphoreType.DMA((n,))],
    )(*grads)


def _row_tile(rows, cols):
    tr = rows
    while tr * cols * 4 > (1 << 20) and tr % 16 == 0:
        tr //= 2
    return tr


def _add_halves(g, t1, core, name):
    _, r, cdim = g.shape
    r2 = r // 2
    tr = _row_tile(r2, cdim)
    nt = r2 // tr

    def body(core_ref, g_ref, t_ref, p_ref, pb_ref):
        p = g_ref[...] + t_ref[...].astype(F32)
        p_ref[...] = p
        pb_ref[...] = p.astype(BF16)

    blk = pl.BlockSpec((None, tr, cdim), lambda j, i, core_ref: (j, i, 0))
    return pl.pallas_call(
        body, name=name,
        grid_spec=pltpu.PrefetchScalarGridSpec(
            num_scalar_prefetch=1, grid=(NCHIP, nt),
            in_specs=[pl.BlockSpec((None, tr, cdim), lambda j, i, core_ref: (j, core_ref[0] * nt + i, 0)), blk],
            out_specs=[blk, blk]),
        out_shape=[jax.ShapeDtypeStruct((NCHIP, r2, cdim), F32), jax.ShapeDtypeStruct((NCHIP, r2, cdim), BF16)],
        compiler_params=_params("parallel", "parallel"),
    )(core, g, t1)


def _add_partials(p, t2, sel, name):
    _, r2, cdim = p.shape
    tr = _row_tile(r2, cdim)
    nt = r2 // tr

    def body(sel_ref, p_ref, t_ref, o_ref):
        o_ref[...] = ((p_ref[...] + t_ref[0].astype(F32)) + t_ref[1].astype(F32)) + t_ref[2].astype(F32)

    return pl.pallas_call(
        body, name=name,
        grid_spec=pltpu.PrefetchScalarGridSpec(
            num_scalar_prefetch=1, grid=(nt,),
            in_specs=[pl.BlockSpec((None, tr, cdim), lambda i, sel_ref: (sel_ref[0], i, 0)),
                      pl.BlockSpec((3, tr, cdim), lambda i, sel_ref: (0, i, 0))],
            out_specs=pl.BlockSpec((tr, cdim), lambda i, sel_ref: (sel_ref[1] * nt + i, 0))),
        out_shape=jax.ShapeDtypeStruct((2 * r2, cdim), F32),
        compiler_params=_params("parallel"),
    )(sel, p, t2)


def _join_halves(grads):
    n = len(grads)

    def body(*refs):
        bufs = refs[n:2 * n]
        send_sems, recv_sems = refs[2 * n:]
        x, y, c = _place()
        copies = []
        for w in range(n):
            r2 = bufs[w].shape[0] // 2
            mine = bufs[w].at[pl.ds(c * r2, r2), :]
            cp = pltpu.make_async_remote_copy(
                src_ref=mine, dst_ref=mine, send_sem=send_sems.at[w], recv_sem=recv_sems.at[w],
                device_id=(x, y, 1 - c), device_id_type=MESH)
            cp.start()
            copies.append(cp)
        for w in range(n):
            r2 = bufs[w].shape[0] // 2
            theirs = bufs[w].at[pl.ds((1 - c) * r2, r2), :]
            pltpu.make_async_remote_copy(
                src_ref=theirs, dst_ref=theirs, send_sem=send_sems.at[w], recv_sem=recv_sems.at[w],
                device_id=(x, y, 1 - c), device_id_type=MESH).wait_recv()
        for cp in copies:
            cp.wait_send()

    return pl.pallas_call(
        body, name="rs_join_halves",
        out_shape=[jax.ShapeDtypeStruct(g.shape, F32) for g in grads],
        in_specs=[ANY] * n, out_specs=[ANY] * n, input_output_aliases={w: w for w in range(n)},
        scratch_shapes=[pltpu.SemaphoreType.DMA((n,)), pltpu.SemaphoreType.DMA((n,))],
    )(*grads)


def _ada_fwd(c_all, ada_w):
    nl, _, ns = ada_w.shape

    def body(c_ref, w_ref, o_ref):
        cv = c_ref[...]
        o_ref[...] = _dot((cv * _sigmoid(cv)).astype(BF16), w_ref[...].astype(BF16))

    return pl.pallas_call(
        body, name="ada_fwd", grid=(nl,),
        in_specs=[pl.BlockSpec((NDEV, D), lambda i: (0, 0)), pl.BlockSpec((None, D, ns), lambda i: (i, 0, 0))],
        out_specs=pl.BlockSpec((None, NDEV, ns), lambda i: (i, 0, 0)),
        out_shape=jax.ShapeDtypeStruct((nl, NDEV, ns), F32),
        compiler_params=_params("parallel"),
    )(c_all, ada_w)


PACK_ROWS = 24


def _reduce_packed(gathered):
    def body(g_ref, tot_ref, loss_ref):
        tot = g_ref[0:PACK_ROWS, :]
        for dev in range(1, NDEV):
            tot = tot + g_ref[dev * PACK_ROWS:(dev + 1) * PACK_ROWS, :]
        tot_ref[...] = tot
        loss_ref[...] = jnp.zeros((8, 128), F32) + jnp.sum(tot[10:11, :])

    return pl.pallas_call(
        body, name="reduce_packed",
        out_shape=[jax.ShapeDtypeStruct((PACK_ROWS, D), F32), jax.ShapeDtypeStruct((8, 128), F32)],
    )(gathered)


def _ada_w_update(c_t, dms, w, m, v):
    nl, _, ns = w.shape
    tr = 256

    def body(ct_ref, dm_ref, w_ref, m_ref, v_ref, g_ref, d_ref, m2_ref, v2_ref):
        ct = ct_ref[...]
        sc = ct * _sigmoid(ct)
        dm = dm_ref[...]
        g = sc[:, 0:1] * dm[0:1, :]
        for b in range(1, NDEV):
            g = g + sc[:, b:b + 1] * dm[b:b + 1, :]
        g_ref[...] = g
        d_ref[...], m2_ref[...], v2_ref[...] = _adamw_math(w_ref[...], g, m_ref[...], v_ref[...])

    blk = pl.BlockSpec((None, tr, ns), lambda i, j: (i, j, 0))
    return pl.pallas_call(
        body, name="ada_w_update", grid=(nl, D // tr),
        in_specs=[pl.BlockSpec((tr, NDEV), lambda i, j: (j, 0)),
                  pl.BlockSpec((None, NDEV, ns), lambda i, j: (i, 0, 0)), blk, blk, blk],
        out_specs=[blk] * 4,
        out_shape=[jax.ShapeDtypeStruct((nl, D, ns), F32)] * 4,
        compiler_params=_params("parallel", "parallel"),
    )(c_t, dms, w, m, v)


def _chip_major(a, parts):
    g, _, cdim = a.shape
    return jnp.transpose(a.reshape(g, NCHIP, parts, cdim), (1, 0, 2, 3)).reshape(NCHIP, g * parts, cdim)


def _group_major(a):
    return jnp.transpose(a.reshape(NCHIP, 4, PG // NCHIP, PG), (1, 0, 2, 3)).reshape(4, PG, PG)


def kernel(x, c, norm_g, ada_w, ada_b, even_w_in, pool_w, pool_scale, even_w_out, odd_w_in, conv_w, conv_b, odd_w_out, final_g, loss_target, m_norm_g, m_ada_w, m_ada_b, m_even_w_in, m_pool_w, m_pool_scale, m_even_w_out, m_odd_w_in, m_conv_w, m_conv_b, m_odd_w_out, m_final_g, v_norm_g, v_ada_w, v_ada_b, v_even_w_in, v_pool_w, v_pool_scale, v_even_w_out, v_odd_w_in, v_conv_w, v_conv_b, v_odd_w_out, v_final_g):
    ix, iy, ic = _place()
    chip = 2 * ix + iy
    batch = 2 * chip + ic
    sel = jnp.stack([chip, ic]).astype(jnp.int32)
    ns_ada = ada_w.shape[2]
    ns_conv = conv_b.shape[1]

    conv_rows = jnp.pad(jnp.concatenate([conv_w[0], conv_b], axis=0), ((0, 3), (0, D - ns_conv)))
    first = _allgather8(jnp.concatenate([c, conv_rows], axis=0), "gather_c_conv").reshape(NCHIP, 2, 8, D)
    c_all = first[:, :, 0].reshape(NDEV, D)
    cw_full = jnp.transpose(first[:, 0, 1:5, 0:ns_conv], (1, 0, 2)).reshape(4, DI)
    cw8 = jnp.concatenate([cw_full[0:3], jnp.zeros((5, DI), F32)], axis=0)
    cb_full = cw_full[3:4]

    m_cols = _allgather8(_ada_fwd(c_all, ada_w).reshape(2 * NDEV, ns_ada), "gather_ada")
    m_cols = m_cols.reshape(NCHIP, 2, 2, NDEV, ns_ada)[:, 0]
    m_mine = lax.dynamic_index_in_dim(m_cols, batch, axis=2, keepdims=False)
    m_mine = jnp.transpose(m_mine, (1, 0, 2)).reshape(2, 3 * D) + ada_b
    zrow = jnp.zeros((3, D), F32)

    def vec_rows(i):
        sh, sc, gt = m_mine[i, 0:D], m_mine[i, D:2 * D], m_mine[i, 2 * D:3 * D]
        return jnp.concatenate([jnp.stack([norm_g[i], sc, sh, gt, final_g]), zrow], axis=0)

    shards = (pool_w[0].astype(BF16).reshape(PG, PG), even_w_out[0].astype(BF16),
              odd_w_in[0].astype(BF16), odd_w_out[0].astype(BF16))
    dx0, grads, sums = _local_step(
        x[0], loss_target[0], vec_rows(0), vec_rows(1), even_w_in[0].astype(BF16), shards, pool_scale, cw8,
        cb_full, sel)
    r_win0, r_pw, r_wout0, r_win1, r_wout1 = grads

    packed = jnp.concatenate([
        sums["dm0"], sums["dm1"], sums["norm_g"], sums["pool_scale"], sums["final_g"], sums["loss"],
        sums["conv_w"].reshape(6, D), sums["conv_b"].reshape(2, D), jnp.zeros((PACK_ROWS - 19, D), F32)], axis=0)
    gathered = _allgather8(packed, "gather_sums")
    tot, loss8 = _reduce_packed(gathered)
    loss = loss8[0, 0]
    g_norm_g, g_pool_scale, g_final_g = tot[6:8], tot[8:9], tot[9]
    g_ada_b = tot[0:6].reshape(2, 3 * D)
    g_conv_w = lax.dynamic_slice_in_dim(tot[11:17].reshape(3, DI), chip * (DI // NCHIP), DI // NCHIP, axis=1)
    g_conv_b = lax.dynamic_slice_in_dim(tot[17:19].reshape(1, DI), chip * (DI // NCHIP), DI // NCHIP, axis=1)
    dm_all = gathered.reshape(NDEV, PACK_ROWS, D)[:, 0:6].reshape(NDEV, 2, 3 * D)
    dm_cols = jnp.transpose(lax.dynamic_slice_in_dim(dm_all, chip * ns_ada, ns_ada, axis=2), (1, 0, 2))
    g_ada_w, d_ada_w, nm_ada_w, nv_ada_w = _ada_w_update(jnp.transpose(c_all), dm_cols, ada_w, m_ada_w, v_ada_w)

    def upd(w, g, m, v, name):
        shape = w.shape
        w2, m2, v2 = (a.reshape(g.shape) for a in (w, m, v))
        d, nm, nv = _adamw(w2, g, m2, v2, name)
        return g.reshape(shape), d.reshape(shape), nm.reshape(shape), nv.reshape(shape)

    o_win0 = upd(even_w_in, r_win0, m_even_w_in, v_even_w_in, "adamw_win0")
    o_pw = upd(pool_w, r_pw, m_pool_w, v_pool_w, "adamw_pool")
    o_wout0 = upd(even_w_out, r_wout0, m_even_w_out, v_even_w_out, "adamw_wout0")
    o_win1 = upd(odd_w_in, r_win1, m_odd_w_in, v_odd_w_in, "adamw_win1")
    o_wout1 = upd(odd_w_out, r_wout1, m_odd_w_out, v_odd_w_out, "adamw_wout1")

    def pack_small(ng, ab, ps, fg, cwv, cbv):
        conv = jnp.concatenate([cwv.reshape(3, -1), cbv.reshape(1, -1)], axis=0).reshape(2, D)
        return jnp.concatenate([ng, ab.reshape(6, D), ps, fg.reshape(1, D), conv, jnp.zeros((4, D), F32)], axis=0)

    sw = pack_small(norm_g, ada_b, pool_scale, final_g, conv_w, conv_b)
    sg = pack_small(g_norm_g, g_ada_b, g_pool_scale, g_final_g, g_conv_w, g_conv_b)
    sm = pack_small(m_norm_g, m_ada_b, m_pool_scale, m_final_g, m_conv_w, m_conv_b)
    sv = pack_small(v_norm_g, v_ada_b, v_pool_scale, v_final_g, v_conv_w, v_conv_b) + jnp.concatenate(
        [jnp.zeros((12, D), F32), jnp.ones((4, D), F32)], axis=0)
    small = _adamw(sw, sg, sm, sv, "adamw_small")

    def unpack_small(a):
        conv = a[10:12].reshape(4, -1)
        return dict(norm_g=a[0:2], ada_b=a[2:8].reshape(2, 3 * D), pool_scale=a[8:9], final_g=a[9],
                    conv_w=conv[0:3].reshape(conv_w.shape), conv_b=conv[3:4].reshape(conv_b.shape))

    s_grad = dict(norm_g=g_norm_g, ada_b=g_ada_b, pool_scale=g_pool_scale, final_g=g_final_g,
                  conv_w=g_conv_w.reshape(conv_w.shape), conv_b=g_conv_b.reshape(conv_b.shape))
    s_out = [s_grad] + [unpack_small(a) for a in small]

    outs = []
    for k in range(4):
        sm_k = s_out[k]
        outs.append([sm_k["norm_g"], (g_ada_w, d_ada_w, nm_ada_w, nv_ada_w)[k], sm_k["ada_b"], o_win0[k], o_pw[k],
                     sm_k["pool_scale"], o_wout0[k], o_win1[k], sm_k["conv_w"], sm_k["conv_b"], o_wout1[k],
                     sm_k["final_g"]])
    return (loss, dx0[None], *outs[0], *outs[1], *outs[2], *outs[3])
```
